```python
import jax, jax.numpy as jnp
from jax import lax
import numpy as np

D_MODEL = 1024
BATCH = 8
SEQ = 2048
DEPTH = 2

CTX_LEN = 256
GRID_W = 64
ROPE_BASE = 10000.0
Q_BLOCK = 128
EPS = 1e-6

D_MIX = D_MODEL
SSD_WIDTH = D_MIX // 2
SSD_HEAD_DIM = 64
SSD_HEADS = SSD_WIDTH // SSD_HEAD_DIM
SSD_GROUPS = 2
SSD_STATE = 128
SSD_CHUNK = 128
CONV_K = 5
XBC_DIM = SSD_WIDTH + 2 * SSD_GROUPS * SSD_STATE
GQA_WIDTH = D_MIX // 4
GQA_HEAD_DIM = 64
GQA_HEADS = GQA_WIDTH // GQA_HEAD_DIM
GQA_KV_HEADS = GQA_HEADS // 2
DIFF_WIDTH = D_MIX // 4
DIFF_V_DIM = 64
DIFF_HEADS = DIFF_WIDTH // DIFF_V_DIM
DIFF_QK_DIM = DIFF_V_DIM // 2

IN_SPLITS = (
    ("xbc", XBC_DIM), ("z", SSD_WIDTH), ("dt", 2 * SSD_HEADS),
    ("gq", GQA_HEADS * GQA_HEAD_DIM), ("gk", GQA_KV_HEADS * GQA_HEAD_DIM),
    ("gv", GQA_KV_HEADS * GQA_HEAD_DIM), ("gg", GQA_WIDTH),
    ("dq", 2 * DIFF_HEADS * DIFF_QK_DIM), ("dk", 2 * DIFF_HEADS * DIFF_QK_DIM),
    ("dv", DIFF_HEADS * DIFF_V_DIM), ("dg", DIFF_WIDTH),
)
IN_COLS = sum(s for _, s in IN_SPLITS)

kernel_name = "hybrid_ssd_gqa_diffattn_dit_block"


def rmsnorm(x, g):
    xf = x.astype(jnp.float32)
    y = xf * lax.rsqrt(jnp.mean(xf * xf, axis=-1, keepdims=True) + EPS)
    return y.astype(x.dtype) * g


def split_proj(p):
    idx = [int(i) for i in np.cumsum([s for _, s in IN_SPLITS])[:-1]]
    parts = jnp.split(p, idx, axis=-1)
    return {name: part for (name, _), part in zip(IN_SPLITS, parts)}


def axial_angles(row_idx, col_idx, dim):
    quarter = dim // 4
    inv = ROPE_BASE ** (-jnp.arange(quarter, dtype=jnp.float32) / quarter)
    ang_r = row_idx.astype(jnp.float32)[:, None] * inv
    ang_c = col_idx.astype(jnp.float32)[:, None] * inv
    return jnp.concatenate([ang_r, ang_c], axis=-1)


def apply_rope(x, ang):
    half = x.shape[-1] // 2
    xf = x.astype(jnp.float32)
    x1, x2 = xf[..., :half], xf[..., half:]
    cos = jnp.cos(ang)[None, :, None, :]
    sin = jnp.sin(ang)[None, :, None, :]
    out = jnp.concatenate([x1 * cos - x2 * sin, x2 * cos + x1 * sin], axis=-1)
    return out.astype(x.dtype)


def attention(q, k, v):
    b, t, hq, d = q.shape
    g = k.shape[2]
    r = hq // g
    dv = v.shape[-1]
    nb = t // Q_BLOCK
    qb = jnp.moveaxis(q.reshape(b, nb, Q_BLOCK, g, r, d), 1, 0)
    scale = d ** -0.5

    def block(qblk):
        s = jnp.einsum('bqgrd,bsgd->bgrqs', qblk, k).astype(jnp.float32) * scale
        p = jax.nn.softmax(s, axis=-1).astype(v.dtype)
        return jnp.einsum('bgrqs,bsgd->bqgrd', p, v)

    out = lax.map(block, qb)
    return jnp.moveaxis(out, 0, 1).reshape(b, t, hq, dv)


def segsum_exp(cs):
    t = cs.shape[-1]
    diff = cs[..., :, None] - cs[..., None, :]
    mask = jnp.tril(jnp.ones((t, t), dtype=bool))
    return jnp.exp(jnp.where(mask, diff, -jnp.inf))


def ssd_scan(x, dt, a, bm, cm, init_state):
    b, l, h, p = x.shape
    nc = l // SSD_CHUNK
    f = lambda t: t.reshape((b, nc, SSD_CHUNK) + t.shape[2:])
    xd = f(x * dt[..., None])
    a_cum = jnp.cumsum(f(dt * a), axis=2)
    bc, cc = f(bm), f(cm)
    lmat = segsum_exp(jnp.moveaxis(a_cum, -1, 2))
    y_diag = jnp.einsum('bcqhn,bcshn,bchqs,bcshp->bcqhp', cc, bc, lmat, xd)
    decay_states = jnp.exp(a_cum[:, :, -1:, :] - a_cum)
    states = jnp.einsum('bcshn,bcsh,bcshp->bchpn', bc, decay_states, xd)
    chunk_cum = jnp.concatenate(
        [jnp.zeros((b, 1, h), a_cum.dtype), jnp.cumsum(a_cum[:, :, -1, :], axis=1)], axis=1)
    decay_chunk = segsum_exp(jnp.moveaxis(chunk_cum, 1, 2))
    all_states = jnp.concatenate([init_state[:, None].astype(states.dtype), states], axis=1)
    new_states = jnp.einsum('bhzy,byhpn->bzhpn', decay_chunk, all_states)
    states_in, final_state = new_states[:, :-1], new_states[:, -1]
    y_off = jnp.einsum('bcqhn,bchpn,bcqh->bcqhp', cc, states_in, jnp.exp(a_cum))
    return (y_diag + y_off).reshape(b, l, h, p), final_state


def dwconv(u, w, bias):
    out = lax.conv_general_dilated(
        u, w[:, None, :], window_strides=(1,), padding=[(CONV_K // 2, CONV_K // 2)],
        dimension_numbers=('NWC', 'WIO', 'NWC'), feature_group_count=u.shape[-1])
    return out + bias


def ssd_branch(p_l, p_c, conv_w, conv_b, a_log_f, a_log_b, dtb_f, dtb_b, d_skip, norm_g):
    a_f = -jnp.exp(a_log_f.astype(jnp.float32))
    a_b = -jnp.exp(a_log_b.astype(jnp.float32))
    rep = SSD_HEADS // SSD_GROUPS

    def prep(p):
        u = jax.nn.silu(dwconv(p["xbc"], conv_w, conv_b))
        bb, ll = u.shape[:2]
        xs, bs, cs = jnp.split(u, [SSD_WIDTH, SSD_WIDTH + SSD_GROUPS * SSD_STATE], axis=-1)
        xs = xs.reshape(bb, ll, SSD_HEADS, SSD_HEAD_DIM)
        bs = jnp.repeat(bs.reshape(bb, ll, SSD_GROUPS, SSD_STATE), rep, axis=2)
        cs = jnp.repeat(cs.reshape(bb, ll, SSD_GROUPS, SSD_STATE), rep, axis=2)
        dt_raw = p["dt"].astype(jnp.float32)
        dt_f = jax.nn.softplus(dt_raw[..., :SSD_HEADS] + dtb_f.astype(jnp.float32))
        dt_b = jax.nn.softplus(dt_raw[..., SSD_HEADS:] + dtb_b.astype(jnp.float32))
        return xs, bs, cs, dt_f, dt_b

    flip = lambda t: jnp.flip(t, axis=1)
    xc, bc, cc, dcf, dcb = prep(p_c)
    xl, bl, cl, dlf, dlb = prep(p_l)
    zeros = jnp.zeros((xc.shape[0], SSD_HEADS, SSD_HEAD_DIM, SSD_STATE), jnp.float32)
    y_cf, s_f = ssd_scan(xc, dcf, a_f, bc, cc, zeros)
    y_cb, s_b = ssd_scan(flip(xc), flip(dcb), a_b, flip(bc), flip(cc), zeros)
    y_lf, _ = ssd_scan(xl, dlf, a_f, bl, cl, s_f)
    y_lb, _ = ssd_scan(flip(xl), flip(dlb), a_b, flip(bl), flip(cl), s_b)

    def finish(yf, yb_rev, xs, z):
        y = yf + flip(yb_rev) + xs * d_skip[:, None]
        y = y.reshape(y.shape[0], y.shape[1], SSD_WIDTH)
        return rmsnorm(y * jax.nn.silu(z.astype(y.dtype)), norm_g).astype(z.dtype)

    return finish(y_lf, y_lb, xl, p_l["z"]), finish(y_cf, y_cb, xc, p_c["z"])


def gqa_branch(p_l, p_c, ang, q_g, k_g, ctx_out):
    def qkv(p):
        bb, ll = p["gq"].shape[:2]
        q = rmsnorm(p["gq"].reshape(bb, ll, GQA_HEADS, GQA_HEAD_DIM), q_g)
        k = rmsnorm(p["gk"].reshape(bb, ll, GQA_KV_HEADS, GQA_HEAD_DIM), k_g)
        v = p["gv"].reshape(bb, ll, GQA_KV_HEADS, GQA_HEAD_DIM)
        return q, k, v

    def gate(o, p):
        return o.reshape(o.shape[0], o.shape[1], GQA_WIDTH) * jax.nn.silu(p["gg"])

    q_l, k_l, v_l = qkv(p_l)
    q_c, k_c, v_c = qkv(p_c)
    q_l, k_l = apply_rope(q_l, ang), apply_rope(k_l, ang)
    k_all = jnp.concatenate([k_c, k_l], axis=1)
    v_all = jnp.concatenate([v_c, v_l], axis=1)
    y_l = gate(attention(q_l, k_all, v_all), p_l)
    y_c = gate(attention(q_c, k_c, v_c), p_c) if ctx_out else None
    return y_l, y_c


def diff_branch(p_l, p_c, ang, lam_params, norm_g, lam_init, ctx_out):
    lp = lam_params.astype(jnp.float32)
    lam = jnp.exp(jnp.sum(lp[0] * lp[1])) - jnp.exp(jnp.sum(lp[2] * lp[3])) + lam_init

    def qkv(p):
        bb, ll = p["dq"].shape[:2]
        q = p["dq"].reshape(bb, ll, 2 * DIFF_HEADS, DIFF_QK_DIM)
        k = p["dk"].reshape(bb, ll, 2 * DIFF_HEADS, DIFF_QK_DIM)
        v = p["dv"].reshape(bb, ll, DIFF_HEADS, DIFF_V_DIM)
        return q, k, v

    def diff_attend(q, k, v, p):
        o = attention(q[:, :, 0::2], k[:, :, 0::2], v) - lam * attention(q[:, :, 1::2], k[:, :, 1::2], v)
        o = (rmsnorm(o, norm_g) * (1.0 - lam_init)).astype(v.dtype)
        return o.reshape(o.shape[0], o.shape[1], DIFF_WIDTH) * jax.nn.silu(p["dg"])

    q_l, k_l, v_l = qkv(p_l)
    q_c, k_c, v_c = qkv(p_c)
    q_l, k_l = apply_rope(q_l, ang), apply_rope(k_l, ang)
    k_all = jnp.concatenate([k_c, k_l], axis=1)
    v_all = jnp.concatenate([v_c, v_l], axis=1)
    y_l = diff_attend(q_l, k_all, v_all, p_l)
    y_c = diff_attend(q_c, k_c, v_c, p_c) if ctx_out else None
    return y_l, y_c


def setup_inputs(seed: int = 0) -> dict:
    key = jax.random.key(seed)
    ks = jax.random.split(key, 24)
    nrm = jax.random.normal

    def dt_bias(k):
        dt = jnp.exp(jax.random.uniform(k, (DEPTH, SSD_HEADS), minval=float(np.log(1e-3)), maxval=float(np.log(1e-1))))
        return dt + jnp.log(-jnp.expm1(-dt))

    return {
        "x": nrm(ks[0], (BATCH, SEQ, D_MODEL), jnp.float32),
        "c": nrm(ks[1], (BATCH, D_MODEL), jnp.float32),
        "ctx": nrm(ks[2], (BATCH, CTX_LEN, D_MODEL), jnp.float32),
        "c_ctx": nrm(ks[3], (D_MODEL,), jnp.float32),
        "w_mod": nrm(ks[4], (DEPTH, D_MODEL, 3 * D_MODEL), jnp.float32) * (0.5 * D_MODEL ** -0.5),
        "b_mod": 0.01 * nrm(ks[5], (DEPTH, 3 * D_MODEL), jnp.float32),
        "g_pre": 1.0 + 0.02 * nrm(ks[6], (DEPTH, D_MODEL), jnp.float32),
        "g_post": 1.0 + 0.02 * nrm(ks[7], (DEPTH, D_MODEL), jnp.float32),
        "w_in": nrm(ks[8], (DEPTH, D_MODEL, IN_COLS), jnp.float32) * D_MODEL ** -0.5,
        "conv_w": nrm(ks[9], (DEPTH, CONV_K, XBC_DIM), jnp.float32) * CONV_K ** -0.5,
        "conv_b": 0.01 * nrm(ks[10], (DEPTH, XBC_DIM), jnp.float32),
        "a_log_fwd": jnp.log(jax.random.uniform(ks[11], (DEPTH, SSD_HEADS), minval=1.0, maxval=16.0)),
        "a_log_bwd": jnp.log(jax.random.uniform(ks[12], (DEPTH, SSD_HEADS), minval=1.0, maxval=16.0)),
        "dt_bias_fwd": dt_bias(ks[13]),
        "dt_bias_bwd": dt_bias(ks[14]),
        "d_skip": 1.0 + 0.1 * nrm(ks[15], (DEPTH, SSD_HEADS), jnp.float32),
        "ssd_norm_g": 1.0 + 0.02 * nrm(ks[16], (DEPTH, SSD_WIDTH), jnp.float32),
        "q_norm_g": 1.0 + 0.02 * nrm(ks[17], (DEPTH, GQA_HEAD_DIM), jnp.float32),
        "k_norm_g": 1.0 + 0.02 * nrm(ks[18], (DEPTH, GQA_HEAD_DIM), jnp.float32),
        "diff_lambda": 0.1 * nrm(ks[19], (DEPTH, 4, DIFF_QK_DIM), jnp.float32),
        "diff_norm_g": 1.0 + 0.02 * nrm(ks[20], (DEPTH, DIFF_V_DIM), jnp.float32),
        "w_out": nrm(ks[21], (DEPTH, D_MIX, D_MODEL), jnp.float32) * D_MIX ** -0.5,
    }


def reference(x, c, ctx, c_ctx, w_mod, b_mod, g_pre, g_post, w_in, conv_w, conv_b,
              a_log_fwd, a_log_bwd, dt_bias_fwd, dt_bias_bwd, d_skip, ssd_norm_g,
              q_norm_g, k_norm_g, diff_lambda, diff_norm_g, w_out):
    n_lat = x.shape[1]
    ROWS = n_lat // GRID_W
    row_idx = jnp.repeat(jnp.arange(ROWS), GRID_W)
    col_idx = jnp.arange(ROWS * GRID_W) % GRID_W
    ang_g = axial_angles(row_idx, col_idx, GQA_HEAD_DIM)
    ang_d = axial_angles(row_idx, col_idx, DIFF_QK_DIM)

    h, hc = x, ctx
    s_lat = jax.nn.silu(c)
    s_ctx = jax.nn.silu(c_ctx)
    for l in range(DEPTH):
        ctx_out = l < DEPTH - 1
        lam_init = 0.8 - 0.6 * float(np.exp(-0.3 * l))
        mod_l = s_lat @ w_mod[l] + b_mod[l]
        mod_c = s_ctx @ w_mod[l] + b_mod[l]
        sh_l, sc_l, gt_l = jnp.split(mod_l[:, None, :], 3, axis=-1)
        sh_c, sc_c, gt_c = jnp.split(mod_c, 3)
        u_l = rmsnorm(h, g_pre[l]) * (1 + sc_l) + sh_l
        u_c = rmsnorm(hc, g_pre[l]) * (1 + sc_c) + sh_c
        p_l = split_proj(u_l @ w_in[l])
        p_c = split_proj(u_c @ w_in[l])

        y_s_l, y_s_c = ssd_branch(p_l, p_c, conv_w[l], conv_b[l], a_log_fwd[l], a_log_bwd[l],
                                  dt_bias_fwd[l], dt_bias_bwd[l], d_skip[l], ssd_norm_g[l])
        y_g_l, y_g_c = gqa_branch(p_l, p_c, ang_g, q_norm_g[l], k_norm_g[l], ctx_out)
        y_d_l, y_d_c = diff_branch(p_l, p_c, ang_d, diff_lambda[l], diff_norm_g[l], lam_init, ctx_out)

        o_l = jnp.concatenate([y_s_l, y_g_l, y_d_l], axis=-1) @ w_out[l]
        h = h + gt_l * rmsnorm(o_l, g_post[l])
        if ctx_out:
            o_c = jnp.concatenate([y_s_c, y_g_c, y_d_c], axis=-1) @ w_out[l]
            hc = hc + gt_c * rmsnorm(o_c, g_post[l])
    return h
```

```python
import functools
import math

import numpy as np
import jax
import jax.numpy as jnp
from jax import lax
from jax.experimental import pallas as pl
from jax.experimental.pallas import tpu as pltpu

F32 = jnp.float32
BF16 = jnp.bfloat16

D_MODEL = 1024
GRID_W = 64
ROPE_BASE = 10000.0
EPS = 1e-6
LOG2E = 1.4426950408889634

SSD_WIDTH = 512
SSD_HEADS = 8
SSD_HEAD_DIM = 64
SSD_GROUPS = 2
SSD_STATE = 128
CHUNK = 128
CONV_K = 5
HALO = 8
GQA_HEADS = 4
GQA_HEAD_DIM = 64
DIFF_HEADS = 4
DIFF_QK_DIM = 32
DIFF_V_DIM = 64

_IN_SPLITS = (("xbc", 1024), ("z", 512), ("dt", 16), ("gq", 256), ("gk", 128),
              ("gv", 128), ("gg", 256), ("dq", 256), ("dk", 256), ("dv", 256),
              ("dg", 256))
IN_COLS = sum(s for _, s in _IN_SPLITS)
NP = 28 * 128
GQ_HEAD_ORDER = (0, 2, 1, 3)

VMEM_LIMIT = 56 * 1024 * 1024


def _in_col_perm():
    off, o = {}, 0
    for name, size in _IN_SPLITS:
        off[name] = o
        o += size
    pad = IN_COLS
    cols = list(range(off["xbc"], off["xbc"] + 1024))
    cols += list(range(off["z"], off["z"] + 512))
    for h in GQ_HEAD_ORDER:
        cols += list(range(off["gq"] + 64 * h, off["gq"] + 64 * h + 64))
    cols += list(range(off["gk"], off["gk"] + 128))
    cols += list(range(off["gv"], off["gv"] + 128))
    for h in GQ_HEAD_ORDER:
        cols += list(range(off["gg"] + 64 * h, off["gg"] + 64 * h + 64))
    for name in ("dq", "dk", "dv", "dg"):
        cols += list(range(off[name], off[name] + 256))
    for g in range(SSD_GROUPS):
        blk = [pad] * 128
        for i in range(4):
            blk[i] = off["dt"] + 4 * g + i
            blk[8 + i] = off["dt"] + SSD_HEADS + 4 * g + i
        cols += blk
    assert len(cols) == NP
    return np.asarray(cols, np.int32)


def _out_row_perm():
    rows = list(range(SSD_WIDTH))
    for h in GQ_HEAD_ORDER:
        rows += list(range(SSD_WIDTH + 64 * h, SSD_WIDTH + 64 * h + 64))
    rows += list(range(SSD_WIDTH + 256, SSD_WIDTH + 512))
    return np.asarray(rows, np.int32)


def _dot(a, b):
    return jnp.dot(a, b, preferred_element_type=F32)


def _split3(x):
    hi = x.astype(BF16)
    r1 = x - hi.astype(F32)
    mid = r1.astype(BF16)
    lo = (r1 - mid.astype(F32)).astype(BF16)
    return hi, mid, lo


def _dot_exact_lhs(x, m_bf16):
    hi, mid, lo = _split3(x)
    return _dot(hi, m_bf16) + _dot(mid, m_bf16) + _dot(lo, m_bf16)


def _dot_exact_rhs(m_bf16, x):
    hi, mid, lo = _split3(x)
    return _dot(m_bf16, hi) + _dot(m_bf16, mid) + _dot(m_bf16, lo)


def _silu(x):
    return x * jax.nn.sigmoid(x)


def _seg_ones(width, seg):
    r = lax.broadcasted_iota(jnp.int32, (width, width), 0)
    c = lax.broadcasted_iota(jnp.int32, (width, width), 1)
    same = (r & ~(seg - 1)) == (c & ~(seg - 1))
    return jnp.where(same, 1.0, 0.0).astype(BF16)


def _seg_rms(x, seg, seg_mat):
    ss = _dot_exact_lhs(x * x, seg_mat)
    return x * lax.rsqrt(ss * (1.0 / seg) + EPS)


def _rope(x, cos, sin_signed, half):
    w = x.shape[-1]
    lane = lax.broadcasted_iota(jnp.int32, x.shape, 1)
    first = (lane & (2 * half - 1)) < half
    swapped = jnp.where(first, pltpu.roll(x, w - half, 1), pltpu.roll(x, half, 1))
    return x * cos + swapped * sin_signed


def _mod_kernel(cs_ref, w_ref, b_ref, o_ref):
    s = _silu(cs_ref[...]).astype(BF16)
    o_ref[...] = _dot(s, w_ref[...].astype(BF16)) + b_ref[...]


def _mod_call(cs, w_mod, b_mod):
    depth = w_mod.shape[0]
    nrow = cs.shape[0]
    tn = 1024
    return pl.pallas_call(
        _mod_kernel,
        grid=(depth, 3 * D_MODEL // tn),
        in_specs=[
            pl.BlockSpec((nrow, D_MODEL), lambda l, j: (0, 0)),
            pl.BlockSpec((None, D_MODEL, tn), lambda l, j: (l, 0, j)),
            pl.BlockSpec((None, 1, tn), lambda l, j: (l, 0, j)),
        ],
        out_specs=pl.BlockSpec((None, nrow, tn), lambda l, j: (l, 0, j)),
        out_shape=jax.ShapeDtypeStruct((depth, nrow, 3 * D_MODEL), F32),
        compiler_params=pltpu.CompilerParams(
            dimension_semantics=("arbitrary", "arbitrary")),
        name="mod_proj",
    )(cs, w_mod, b_mod.reshape(depth, 1, 3 * D_MODEL))


def _inproj_kernel(h_ref, mod_ref, g_ref, w_ref, o_ref):
    x = h_ref[...]
    ms = jnp.mean(x * x, axis=-1, keepdims=True)
    y = x * lax.rsqrt(ms + EPS) * g_ref[...]
    sh = mod_ref[:, 0:D_MODEL]
    sc = mod_ref[:, D_MODEL:2 * D_MODEL]
    u = (y * (1.0 + sc) + sh).astype(BF16)
    tn = 512
    for j in range(NP // tn):
        o_ref[:, j * tn:(j + 1) * tn] = _dot(u, w_ref[:, j * tn:(j + 1) * tn])


def _inproj_call(h, mod3, g_pre, w_bf16, tiles_per_row, fixed_row):
    n_tok = h.shape[0]
    tm = 512
    if fixed_row is None:
        mod_idx = lambda i: (i // tiles_per_row, 0, 0)
    else:
        mod_idx = lambda i: (fixed_row, 0, 0)
    return pl.pallas_call(
        _inproj_kernel,
        grid=(n_tok // tm,),
        in_specs=[
            pl.BlockSpec((tm, D_MODEL), lambda i: (i, 0)),
            pl.BlockSpec((None, 1, 3 * D_MODEL), mod_idx),
            pl.BlockSpec((1, D_MODEL), lambda i: (0, 0)),
            pl.BlockSpec((D_MODEL, NP), lambda i: (0, 0)),
        ],
        out_specs=pl.BlockSpec((tm, NP), lambda i: (i, 0)),
        out_shape=jax.ShapeDtypeStruct((n_tok, NP), F32),
        compiler_params=pltpu.CompilerParams(
            dimension_semantics=("arbitrary",), vmem_limit_bytes=VMEM_LIMIT),
        name="in_proj",
    )(h, mod3, g_pre, w_bf16)


def _outproj_kernel(ys_ref, yg_ref, yd_ref, h_ref, mod_ref, g_ref, w_ref, o_ref):
    o = (_dot(ys_ref[...], w_ref[0:512, :]) + _dot(yg_ref[...], w_ref[512:768, :])
         + _dot(yd_ref[...], w_ref[768:1024, :]))
    ms = jnp.mean(o * o, axis=-1, keepdims=True)
    n = o * lax.rsqrt(ms + EPS) * g_ref[...]
    gt = mod_ref[:, 2 * D_MODEL:3 * D_MODEL]
    o_ref[...] = h_ref[...] + gt * n


def _outproj_call(ys, yg, yd, h, mod3, g_post, w_bf16, tiles_per_row, fixed_row):
    n_tok = h.shape[0]
    tm = 512
    if fixed_row is None:
        mod_idx = lambda i: (i // tiles_per_row, 0, 0)
    else:
        mod_idx = lambda i: (fixed_row, 0, 0)
    return pl.pallas_call(
        _outproj_kernel,
        grid=(n_tok // tm,),
        in_specs=[
            pl.BlockSpec((tm, 512), lambda i: (i, 0)),
            pl.BlockSpec((tm, 256), lambda i: (i, 0)),
            pl.BlockSpec((tm, 256), lambda i: (i, 0)),
            pl.BlockSpec((tm, D_MODEL), lambda i: (i, 0)),
            pl.BlockSpec((None, 1, 3 * D_MODEL), mod_idx),
            pl.BlockSpec((1, D_MODEL), lambda i: (0, 0)),
            pl.BlockSpec((D_MODEL, D_MODEL), lambda i: (0, 0)),
        ],
        out_specs=pl.BlockSpec((tm, D_MODEL), lambda i: (i, 0)),
        out_shape=jax.ShapeDtypeStruct((n_tok, D_MODEL), F32),
        compiler_params=pltpu.CompilerParams(
            dimension_semantics=("arbitrary",), vmem_limit_bytes=VMEM_LIMIT),
        name="out_proj",
    )(ys, yg, yd, h, mod3, g_post, w_bf16)


def _attend(lhs_bf16, kt_ref, vext_ref):
    s = _dot(lhs_bf16, kt_ref[...])
    m = jnp.max(s, axis=-1, keepdims=True)
    p = jnp.exp2(s - m).astype(BF16)
    oe = _dot(p, vext_ref[...])
    return oe[:, 0:128] / oe[:, 128:256]


def _gqa_kernel(*refs, n_ctx, n_lat, rope_q):
    it = iter(refs)
    q_ref, gg_ref, kvc_ref = next(it), next(it), next(it)
    kvl_ref = next(it) if n_lat else None
    if rope_q:
        cosq_ref, sinq_ref = next(it), next(it)
    if n_lat:
        cosk_ref, sink_ref = next(it), next(it)
    qg_ref, kg_ref = next(it), next(it)
    y_ref = next(it)
    kt_s, vext_s = next(it), next(it)

    seg128 = _seg_ones(128, 64)

    @pl.when(pl.program_id(1) == 0)
    def _prep_kv():
        kc = _seg_rms(kvc_ref[:, 0:128], 64, seg128) * kg_ref[...]
        kt_s[:, 0:n_ctx] = kc.T.astype(BF16)
        vext_s[0:n_ctx, 0:128] = kvc_ref[:, 128:256].astype(BF16)
        if n_lat:
            kl = _seg_rms(kvl_ref[:, 0:128], 64, seg128) * kg_ref[...]
            kl = _rope(kl, cosk_ref[...], sink_ref[...], 32)
            kt_s[:, n_ctx:n_ctx + n_lat] = kl.T.astype(BF16)
            vext_s[n_ctx:n_ctx + n_lat, 0:128] = kvl_ref[:, 128:256].astype(BF16)
        vext_s[:, 128:256] = jnp.ones((n_ctx + n_lat, 128), BF16)

    seg256 = _seg_ones(256, 64)
    q = _seg_rms(q_ref[...], 64, seg256) * qg_ref[...]
    if rope_q:
        q = _rope(q, cosq_ref[...], sinq_ref[...], 32)
    q = q * (GQA_HEAD_DIM ** -0.5 * LOG2E)
    tq = q.shape[0]
    lane = lax.broadcasted_iota(jnp.int32, (tq, 128), 1)
    for half in range(2):
        qh = q[:, 128 * half:128 * half + 128]
        outs = []
        for kv in range(2):
            in_kv = (lane >= 64 * kv) & (lane < 64 * kv + 64)
            lhs = jnp.where(in_kv, qh, 0.0).astype(BF16)
            outs.append(_attend(lhs, kt_s, vext_s))
        o = jnp.where(lane < 64, outs[0], outs[1])
        gate = _silu(gg_ref[:, 128 * half:128 * half + 128])
        y_ref[:, 128 * half:128 * half + 128] = (o * gate).astype(BF16)


def _gqa_call(p_q, p_c, p_l, tabs, qg, kg, batch, n_ctx, n_lat, rope_q):
    t_total = p_q.shape[0] // batch
    tq = 256
    nq = t_total // tq
    in_specs = [
        pl.BlockSpec((tq, 256), lambda b, i: (b * nq + i, 6)),
        pl.BlockSpec((tq, 256), lambda b, i: (b * nq + i, 8)),
        pl.BlockSpec((n_ctx, 256), lambda b, i: (b, 7)),
    ]
    args = [p_q, p_q, p_c]
    if n_lat:
        in_specs.append(pl.BlockSpec((n_lat, 256), lambda b, i: (b, 7)))
        args.append(p_l)
    if rope_q:
        in_specs += [pl.BlockSpec((tq, 256), lambda b, i: (i, 0))] * 2
        args += [tabs["cos_g"], tabs["sin_g"]]
    if n_lat:
        in_specs += [pl.BlockSpec((n_lat, 128), lambda b, i: (0, 0))] * 2
        args += [tabs["cos_g"], tabs["sin_g"]]
    in_specs += [pl.BlockSpec((1, 256), lambda b, i: (0, 0)),
                 pl.BlockSpec((1, 128), lambda b, i: (0, 0))]
    args += [qg, kg]
    s_keys = n_ctx + n_lat
    return pl.pallas_call(
        functools.partial(_gqa_kernel, n_ctx=n_ctx, n_lat=n_lat, rope_q=rope_q),
        grid=(batch, nq),
        in_specs=in_specs,
        out_specs=pl.BlockSpec((tq, 256), lambda b, i: (b * nq + i, 0)),
        out_shape=jax.ShapeDtypeStruct((p_q.shape[0], 256), BF16),
        scratch_shapes=[pltpu.VMEM((128, s_keys), BF16),
                        pltpu.VMEM((s_keys, 256), BF16)],
        compiler_params=pltpu.CompilerParams(
            dimension_semantics=("arbitrary", "arbitrary"),
            vmem_limit_bytes=VMEM_LIMIT),
        name="gqa_attn",
    )(*args)


def _diff_kernel(*refs, n_ctx, n_lat, rope_q, lam_init):
    it = iter(refs)
    q_ref, dg_ref, kc_ref, vc_ref = next(it), next(it), next(it), next(it)
    if n_lat:
        kl_ref, vl_ref = next(it), next(it)
    if rope_q:
        cosq_ref, sinq_ref = next(it), next(it)
    if n_lat:
        cosk_ref, sink_ref = next(it), next(it)
    lam_ref, ng_ref = next(it), next(it)
    y_ref = next(it)
    kt_s, vlo_s, vhi_s = next(it), next(it), next(it)
    s_keys = n_ctx + n_lat

    @pl.when(pl.program_id(1) == 0)
    def _prep_kv():
        kt_s[:, 0:n_ctx] = kc_ref[...].T.astype(BF16)
        vlo_s[0:n_ctx, 0:128] = vc_ref[:, 0:128].astype(BF16)
        vhi_s[0:n_ctx, 0:128] = vc_ref[:, 128:256].astype(BF16)
        if n_lat:
            kl = _rope(kl_ref[...], cosk_ref[...], sink_ref[...], 16)
            kt_s[:, n_ctx:s_keys] = kl.T.astype(BF16)
            vlo_s[n_ctx:s_keys, 0:128] = vl_ref[:, 0:128].astype(BF16)
            vhi_s[n_ctx:s_keys, 0:128] = vl_ref[:, 128:256].astype(BF16)
        ones = jnp.ones((s_keys, 128), BF16)
        vlo_s[:, 128:256] = ones
        vhi_s[:, 128:256] = ones

    lp = lam_ref[...]
    lam = (jnp.exp(jnp.sum(lp[0:1, :] * lp[1:2, :], axis=-1, keepdims=True))
           - jnp.exp(jnp.sum(lp[2:3, :] * lp[3:4, :], axis=-1, keepdims=True))
           + lam_init)

    q = q_ref[...]
    if rope_q:
        q = _rope(q, cosq_ref[...], sinq_ref[...], 16)
    q = q * (DIFF_QK_DIM ** -0.5 * LOG2E)
    tq = q.shape[0]
    lane256 = lax.broadcasted_iota(jnp.int32, (tq, 256), 1)
    lane128 = lax.broadcasted_iota(jnp.int32, (tq, 128), 1)
    seg128 = _seg_ones(128, 64)
    for half in range(2):
        v_s = vlo_s if half == 0 else vhi_s
        heads = []
        for hh in range(2):
            h = 2 * half + hh
            maps = []
            for mm in range(2):
                mp = 2 * h + mm
                in_map = (lane256 >= 32 * mp) & (lane256 < 32 * mp + 32)
                lhs = jnp.where(in_map, q, 0.0).astype(BF16)
                maps.append(_attend(lhs, kt_s, v_s))
            heads.append(maps[0] - lam * maps[1])
        o = jnp.where(lane128 < 64, heads[0], heads[1])
        n = _seg_rms(o, 64, seg128) * ng_ref[:, 128 * half:128 * half + 128]
        n = n * (1.0 - lam_init)
        gate = _silu(dg_ref[:, 128 * half:128 * half + 128])
        y_ref[:, 128 * half:128 * half + 128] = (n * gate).astype(BF16)


def _diff_call(p_q, p_c, p_l, tabs, lam_params, ng, batch, n_ctx, n_lat, rope_q,
               lam_init):
    t_total = p_q.shape[0] // batch
    tq = 256
    nq = t_total // tq
    in_specs = [
        pl.BlockSpec((tq, 256), lambda b, i: (b * nq + i, 9)),
        pl.BlockSpec((tq, 256), lambda b, i: (b * nq + i, 12)),
        pl.BlockSpec((n_ctx, 256), lambda b, i: (b, 10)),
        pl.BlockSpec((n_ctx, 256), lambda b, i: (b, 11)),
    ]
    args = [p_q, p_q, p_c, p_c]
    if n_lat:
        in_specs += [pl.BlockSpec((n_lat, 256), lambda b, i: (b, 10)),
                     pl.BlockSpec((n_lat, 256), lambda b, i: (b, 11))]
        args += [p_l, p_l]
    if rope_q:
        in_specs += [pl.BlockSpec((tq, 256), lambda b, i: (i, 0))] * 2
        args += [tabs["cos_d"], tabs["sin_d"]]
    if n_lat:
        in_specs += [pl.BlockSpec((n_lat, 256), lambda b, i: (0, 0))] * 2
        args += [tabs["cos_d"], tabs["sin_d"]]
    in_specs += [pl.BlockSpec((4, DIFF_QK_DIM), lambda b, i: (0, 0)),
                 pl.BlockSpec((1, 256), lambda b, i: (0, 0))]
    args += [lam_params, ng]
    s_keys = n_ctx + n_lat
    return pl.pallas_call(
        functools.partial(_diff_kernel, n_ctx=n_ctx, n_lat=n_lat, rope_q=rope_q,
                          lam_init=lam_init),
        grid=(batch, nq),
        in_specs=in_specs,
        out_specs=pl.BlockSpec((tq, 256), lambda b, i: (b * nq + i, 0)),
        out_shape=jax.ShapeDtypeStruct((p_q.shape[0], 256), BF16),
        scratch_shapes=[pltpu.VMEM((256, s_keys), BF16),
                        pltpu.VMEM((s_keys, 256), BF16),
                        pltpu.VMEM((s_keys, 256), BF16)],
        compiler_params=pltpu.CompilerParams(
            dimension_semantics=("arbitrary", "arbitrary"),
            vmem_limit_bytes=VMEM_LIMIT),
        name="diff_attn",
    )(*args)


def _ssd_kernel(xc_ref, xl_ref, bc_ref, bl_ref, cc_ref, cl_ref, zc_ref, zl_ref,
                dtc_ref, dtl_ref, cwx_ref, cwb_ref, cwc_ref, cbx_ref, cbb_ref,
                cbc_ref, par_ref, dsk_ref, ng_ref, *rest, n_ctx, n_lat, ctx_out):
    if ctx_out:
        oc_ref, ol_ref = rest[0], rest[1]
        rest = rest[2:]
    else:
        oc_ref, ol_ref = None, rest[0]
        rest = rest[1:]
    xp_s, dtv_s, yacc_s, sloc_s, acol_s, cbf_s, sin_s, tot_s, vun_s = rest

    g = pl.program_id(1)
    ncc = n_ctx // CHUNK
    ncl = n_lat // CHUNK
    nch = ncc + ncl
    t_all = n_ctx + n_lat
    lat0 = n_ctx + 2 * HALO

    zeros_h = jnp.zeros((HALO, 512), F32)
    xp_s[0:HALO, :] = zeros_h
    xp_s[HALO:HALO + n_ctx, 0:256] = xc_ref[...]
    xp_s[HALO:HALO + n_ctx, 256:384] = bc_ref[...]
    xp_s[HALO:HALO + n_ctx, 384:512] = cc_ref[...]
    xp_s[HALO + n_ctx:lat0, :] = zeros_h
    xp_s[lat0:lat0 + n_lat, 0:256] = xl_ref[...]
    xp_s[lat0:lat0 + n_lat, 256:384] = bl_ref[...]
    xp_s[lat0:lat0 + n_lat, 384:512] = cl_ref[...]
    xp_s[lat0 + n_lat:lat0 + n_lat + HALO, :] = zeros_h

    bias = par_ref[1:2, :]
    a_row = -jnp.exp(par_ref[0:1, :])

    def _softplus(v):
        return jnp.maximum(v, 0.0) + jnp.log1p(jnp.exp(-jnp.abs(v)))

    dtv_s[0:n_ctx, :] = _softplus(dtc_ref[...] + bias)
    dtv_s[n_ctx:t_all, :] = _softplus(dtl_ref[...] + bias)

    r128 = lax.broadcasted_iota(jnp.int32, (CHUNK, CHUNK), 0)
    c128 = lax.broadcasted_iota(jnp.int32, (CHUNK, CHUNK), 1)
    lower = c128 <= r128
    upper = c128 >= r128
    tril = jnp.where(lower, 1.0, 0.0).astype(BF16)
    triu = jnp.where(upper, 1.0, 0.0).astype(BF16)
    fwd_lane = c128 < 8
    er = lax.broadcasted_iota(jnp.int32, (CHUNK, 512), 0)
    ec = lax.broadcasted_iota(jnp.int32, (CHUNK, 512), 1)
    src_lane = jnp.where(ec < 256, ec >> 6, 8 + ((ec - 256) >> 6))
    expand = jnp.where(er == src_lane, 1.0, 0.0).astype(BF16)
    lane256 = lax.broadcasted_iota(jnp.int32, (CHUNK, 256), 1)
    col512 = lax.broadcasted_iota(jnp.int32, (1, 512), 1)

    cw = jnp.concatenate([cwx_ref[...], cwb_ref[...], cwc_ref[...]], axis=1)
    cb = jnp.concatenate([cbx_ref[...], cbb_ref[...], cbc_ref[...]], axis=1)
    dsk = dsk_ref[...]

    def phase_a(c, carry):
        wstart = pl.multiple_of(c * CHUNK + jnp.where(c >= ncc, HALO, 0), 8)
        win = xp_s[pl.ds(wstart, CHUNK + 2 * HALO), :]
        acc = jnp.broadcast_to(cb, (CHUNK, 512))
        for k in range(CONV_K):
            d = k - CONV_K // 2
            if d == 0:
                tap = win[HALO:HALO + CHUNK, :]
            else:
                tap = pltpu.roll(win, (-d) % (CHUNK + 2 * HALO), 0)[HALO:HALO + CHUNK, :]
            acc = acc + cw[k:k + 1, :] * tap
        u = _silu(acc)
        xu = u[:, 0:256]
        bu = u[:, 256:384]
        cu = u[:, 384:512]
        r0 = pl.multiple_of(c * CHUNK, CHUNK)
        dtv = dtv_s[pl.ds(r0, CHUNK), :]
        a = dtv * a_row
        acum = _dot_exact_rhs(tril, a)
        sfx = _dot_exact_rhs(triu, a)
        acol = jnp.where(fwd_lane, acum, sfx)
        at = acol.T
        dtt = dtv.T
        bt = bu.T.astype(BF16)
        cbf = cu.astype(BF16)
        xbf = xu.astype(BF16)
        gmat = _dot(cbf, bt)
        ydiag = jnp.zeros((CHUNK, 256), F32)
        for i in range(4):
            arg = jnp.where(lower, acol[:, i:i + 1] - at[i:i + 1, :],
                            acol[:, 8 + i:9 + i] - at[8 + i:9 + i, :])
            scale = (jnp.where(lower, dtt[i:i + 1, :], 0.0)
                     + jnp.where(upper, dtt[8 + i:9 + i, :], 0.0))
            wmat = (gmat * jnp.exp(arg) * scale).astype(BF16)
            yi = _dot(wmat, xbf)
            in_head = (lane256 >= 64 * i) & (lane256 < 64 * i + 64)
            ydiag = jnp.where(in_head, yi, ydiag)
        a512 = _dot_exact_lhs(acol, expand)
        d512 = _dot_exact_lhs(dtv, expand)
        tot512 = jnp.where(col512 < 256, a512[CHUNK - 1:CHUNK, :], a512[0:1, :])
        w512 = d512 * jnp.exp(tot512 - a512)
        xdw = (jnp.concatenate([xu, xu], axis=1) * w512).astype(BF16)
        sloc_s[c] = _dot(bt, xdw)
        tot_s[c] = jnp.broadcast_to(tot512, (8, 512))
        acol_s[c] = acol
        cbf_s[c] = cbf
        yacc_s[pl.ds(r0, CHUNK), :] = ydiag + xu * dsk
        return carry

    lax.fori_loop(0, nch, phase_a, 0)

    fwd_order = list(range(nch))
    bwd_order = list(range(ncc - 1, -1, -1)) + list(range(nch - 1, ncc - 1, -1))
    for order, lo in ((fwd_order, 0), (bwd_order, 256)):
        state = jnp.zeros((SSD_STATE, 256), F32)
        for c in order:
            sin_s[c, :, lo:lo + 256] = state.astype(BF16)
            state = (state * jnp.exp(tot_s[c, 0:1, lo:lo + 256])
                     + sloc_s[c, :, lo:lo + 256])

    def phase_c(c, z_ref, zrow):
        a512 = _dot_exact_lhs(acol_s[c], expand)
        yo = _dot(cbf_s[c], sin_s[c]) * jnp.exp(a512)
        r0 = c * CHUNK if isinstance(c, int) else pl.multiple_of(c * CHUNK, CHUNK)
        y = yacc_s[pl.ds(r0, CHUNK), :] + yo[:, 0:256] + yo[:, 256:512]
        v = y * _silu(z_ref[pl.ds(zrow, CHUNK), :])

        @pl.when(g == 0)
        def _():
            vun_s[pl.ds(r0, CHUNK), 0:256] = v

        @pl.when(g == 1)
        def _():
            vun_s[pl.ds(r0, CHUNK), 256:512] = v

    if ctx_out:
        for c in range(ncc):
            phase_c(c, zc_ref, c * CHUNK)

    def phase_c_lat(k, carry):
        phase_c(k + ncc, zl_ref, pl.multiple_of(k * CHUNK, CHUNK))
        return carry

    lax.fori_loop(0, ncl, phase_c_lat, 0)

    @pl.when(g == SSD_GROUPS - 1)
    def _finalize():
        ng = ng_ref[...]

        def norm_rows(r0, nrows):
            v = vun_s[pl.ds(r0, nrows), :]
            ms = jnp.mean(v * v, axis=-1, keepdims=True)
            return (v * lax.rsqrt(ms + EPS) * ng).astype(BF16)

        if ctx_out:
            oc_ref[...] = norm_rows(0, n_ctx)

        def fin(k, carry):
            r0 = pl.multiple_of(k * 256, 256)
            ol_ref[pl.ds(r0, 256), :] = norm_rows(n_ctx + r0, 256)
            return carry

        lax.fori_loop(0, n_lat // 256, fin, 0)


def _ssd_call(p_c, p_l, conv_w8, conv_b, par, dsk, ng, batch, n_ctx, n_lat, ctx_out):
    nch = (n_ctx + n_lat) // CHUNK
    t_all = n_ctx + n_lat
    in_specs = [
        pl.BlockSpec((n_ctx, 256), lambda b, g: (b, g)),
        pl.BlockSpec((n_lat, 256), lambda b, g: (b, g)),
        pl.BlockSpec((n_ctx, 128), lambda b, g: (b, 4 + g)),
        pl.BlockSpec((n_lat, 128), lambda b, g: (b, 4 + g)),
        pl.BlockSpec((n_ctx, 128), lambda b, g: (b, 6 + g)),
        pl.BlockSpec((n_lat, 128), lambda b, g: (b, 6 + g)),
        pl.BlockSpec((n_ctx, 256), lambda b, g: (b, 4 + g)),
        pl.BlockSpec((n_lat, 256), lambda b, g: (b, 4 + g)),
        pl.BlockSpec((n_ctx, 128), lambda b, g: (b, 26 + g)),
        pl.BlockSpec((n_lat, 128), lambda b, g: (b, 26 + g)),
        pl.BlockSpec((8, 256), lambda b, g: (0, g)),
        pl.BlockSpec((8, 128), lambda b, g: (0, 4 + g)),
        pl.BlockSpec((8, 128), lambda b, g: (0, 6 + g)),
        pl.BlockSpec((1, 256), lambda b, g: (0, g)),
        pl.BlockSpec((1, 128), lambda b, g: (0, 4 + g)),
        pl.BlockSpec((1, 128), lambda b, g: (0, 6 + g)),
        pl.BlockSpec((None, 8, 128), lambda b, g: (g, 0, 0)),
        pl.BlockSpec((1, 256), lambda b, g: (0, g)),
        pl.BlockSpec((1, 512), lambda b, g: (0, 0)),
    ]
    args = [p_c, p_l, p_c, p_l, p_c, p_l, p_c, p_l, p_c, p_l,
            conv_w8, conv_w8, conv_w8, conv_b, conv_b, conv_b, par, dsk, ng]
    out_specs = [pl.BlockSpec((n_lat, 512), lambda b, g: (b, 0))]
    out_shape = [jax.ShapeDtypeStruct((batch * n_lat, 512), BF16)]
    if ctx_out:
        out_specs = [pl.BlockSpec((n_ctx, 512), lambda b, g: (b, 0))] + out_specs
        out_shape = [jax.ShapeDtypeStruct((batch * n_ctx, 512), BF16)] + out_shape
    scratch = [
        pltpu.VMEM((t_all + 3 * HALO, 512), F32),
        pltpu.VMEM((t_all, 128), F32),
        pltpu.VMEM((t_all, 256), F32),
        pltpu.VMEM((nch, SSD_STATE, 512), F32),
        pltpu.VMEM((nch, CHUNK, 128), F32),
        pltpu.VMEM((nch, CHUNK, 128), BF16),
        pltpu.VMEM((nch, SSD_STATE, 512), BF16),
        pltpu.VMEM((nch, 8, 512), F32),
        pltpu.VMEM((t_all, 512), F32),
    ]
    outs = pl.pallas_call(
        functools.partial(_ssd_kernel, n_ctx=n_ctx, n_lat=n_lat, ctx_out=ctx_out),
        grid=(batch, SSD_GROUPS),
        in_specs=in_specs,
        out_specs=out_specs,
        out_shape=out_shape,
        scratch_shapes=scratch,
        compiler_params=pltpu.CompilerParams(
            dimension_semantics=("arbitrary", "arbitrary"),
            vmem_limit_bytes=VMEM_LIMIT),
        name="ssd_scan",
    )(*args)
    if ctx_out:
        return outs[1], outs[0]
    return outs[0], None


def _rope_tables(n_lat):
    rows = n_lat // GRID_W
    row_idx = jnp.repeat(jnp.arange(rows), GRID_W).astype(F32)
    col_idx = (jnp.arange(rows * GRID_W) % GRID_W).astype(F32)

    def tables(dim, reps):
        quarter = dim // 4
        inv = ROPE_BASE ** (-jnp.arange(quarter, dtype=F32) / quarter)
        ang = jnp.concatenate([row_idx[:, None] * inv, col_idx[:, None] * inv], axis=-1)
        cos, sin = jnp.cos(ang), jnp.sin(ang)
        cos2 = jnp.concatenate([cos, cos], axis=-1)
        sin2 = jnp.concatenate([-sin, sin], axis=-1)
        return jnp.tile(cos2, (1, reps)), jnp.tile(sin2, (1, reps))

    cos_g, sin_g = tables(GQA_HEAD_DIM, 4)
    cos_d, sin_d = tables(DIFF_QK_DIM, 8)
    return {"cos_g": cos_g, "sin_g": sin_g, "cos_d": cos_d, "sin_d": sin_d}


def kernel(x, c, ctx, c_ctx, w_mod, b_mod, g_pre, g_post, w_in, conv_w, conv_b,
           a_log_fwd, a_log_bwd, dt_bias_fwd, dt_bias_bwd, d_skip, ssd_norm_g,
           q_norm_g, k_norm_g, diff_lambda, diff_norm_g, w_out):
    batch, n_lat, _ = x.shape
    n_ctx = ctx.shape[1]
    depth = w_mod.shape[0]
    assert n_lat % 512 == 0 and n_ctx % 256 == 0 and (batch * n_ctx) % 512 == 0
    assert batch + 1 <= 16

    in_perm = _in_col_perm()
    w_in_p = jnp.take(jnp.pad(w_in, ((0, 0), (0, 0), (0, 1))), in_perm, axis=2).astype(BF16)
    w_out_p = jnp.take(w_out, _out_row_perm(), axis=1).astype(BF16)
    xbc_perm = in_perm[:1024]
    conv_w8 = jnp.pad(jnp.take(conv_w, xbc_perm, axis=2), ((0, 0), (0, 8 - CONV_K), (0, 0)))
    conv_b1 = jnp.take(conv_b, xbc_perm, axis=1)[:, None, :]

    def group_lanes(fwd, bwd):
        out = jnp.zeros((depth, SSD_GROUPS, 128), F32)
        for g in range(SSD_GROUPS):
            out = out.at[:, g, 0:4].set(fwd[:, 4 * g:4 * g + 4])
            out = out.at[:, g, 8:12].set(bwd[:, 4 * g:4 * g + 4])
        return out

    ssd_par = jnp.zeros((depth, SSD_GROUPS, 8, 128), F32)
    ssd_par = ssd_par.at[:, :, 0, :].set(group_lanes(a_log_fwd, a_log_bwd))
    ssd_par = ssd_par.at[:, :, 1, :].set(group_lanes(dt_bias_fwd, dt_bias_bwd))
    dsk = jnp.repeat(d_skip, SSD_HEAD_DIM, axis=1)[:, None, :]
    qg = jnp.tile(q_norm_g, (1, 4))[:, None, :]
    kg = jnp.tile(k_norm_g, (1, 2))[:, None, :]
    dng = jnp.tile(diff_norm_g, (1, 4))[:, None, :]
    tabs = _rope_tables(n_lat)

    cs = jnp.concatenate(
        [c, c_ctx[None, :], jnp.zeros((16 - batch - 1, D_MODEL), F32)], axis=0)
    mod_all = _mod_call(cs, w_mod, b_mod)

    h = x.reshape(batch * n_lat, D_MODEL)
    hc = ctx.reshape(batch * n_ctx, D_MODEL)
    for l in range(depth):
        ctx_out = l < depth - 1
        lam_init = 0.8 - 0.6 * float(np.exp(-0.3 * l))
        mod3 = mod_all[l][:, None, :]
        p_l = _inproj_call(h, mod3, g_pre[l][None, :], w_in_p[l], n_lat // 512, None)
        p_c = _inproj_call(hc, mod3, g_pre[l][None, :], w_in_p[l], None, batch)

        ys_l, ys_c = _ssd_call(p_c, p_l, conv_w8[l], conv_b1[l], ssd_par[l], dsk[l],
                               ssd_norm_g[l][None, :], batch, n_ctx, n_lat, ctx_out)
        yg_l = _gqa_call(p_l, p_c, p_l, tabs, qg[l], kg[l], batch, n_ctx, n_lat, True)
        yd_l = _diff_call(p_l, p_c, p_l, tabs, diff_lambda[l], dng[l], batch, n_ctx,
                          n_lat, True, lam_init)
        h = _outproj_call(ys_l, yg_l, yd_l, h, mod3, g_post[l][None, :], w_out_p[l],
                          n_lat // 512, None)
        if ctx_out:
            yg_c = _gqa_call(p_c, p_c, None, tabs, qg[l], kg[l], batch, n_ctx, 0, False)
            yd_c = _diff_call(p_c, p_c, None, tabs, diff_lambda[l], dng[l], batch,
                              n_ctx, 0, False, lam_init)
            hc = _outproj_call(ys_c, yg_c, yd_c, hc, mod3, g_post[l][None, :],
                               w_out_p[l], None, batch)
    return h.reshape(batch, n_lat, D_MODEL)
```

```python
import functools
import math

import numpy as np
import jax
import jax.numpy as jnp
from jax import lax
from jax.experimental import pallas as pl
from jax.experimental.pallas import tpu as pltpu

F32 = jnp.float32
BF16 = jnp.bfloat16

D_MODEL = 1024
GRID_W = 64
ROPE_BASE = 10000.0
EPS = 1e-6
LOG2E = 1.4426950408889634

SSD_WIDTH = 512
SSD_HEADS = 8
SSD_HEAD_DIM = 64
SSD_GROUPS = 2
SSD_STATE = 128
CHUNK = 128
CONV_K = 5
HALO = 8
GQA_HEADS = 4
GQA_HEAD_DIM = 64
DIFF_HEADS = 4
DIFF_QK_DIM = 32
DIFF_V_DIM = 64

_IN_SPLITS = (("xbc", 1024), ("z", 512), ("dt", 16), ("gq", 256), ("gk", 128),
              ("gv", 128), ("gg", 256), ("dq", 256), ("dk", 256), ("dv", 256),
              ("dg", 256))
IN_COLS = sum(s for _, s in _IN_SPLITS)
NP = 28 * 128
GQ_HEAD_ORDER = (0, 2, 1, 3)

VMEM_LIMIT = 56 * 1024 * 1024


def _in_col_perm():
    off, o = {}, 0
    for name, size in _IN_SPLITS:
        off[name] = o
        o += size
    pad = IN_COLS
    cols = list(range(off["xbc"], off["xbc"] + 1024))
    cols += list(range(off["z"], off["z"] + 512))
    for h in GQ_HEAD_ORDER:
        cols += list(range(off["gq"] + 64 * h, off["gq"] + 64 * h + 64))
    cols += list(range(off["gk"], off["gk"] + 128))
    cols += list(range(off["gv"], off["gv"] + 128))
    for h in GQ_HEAD_ORDER:
        cols += list(range(off["gg"] + 64 * h, off["gg"] + 64 * h + 64))
    for name in ("dq", "dk", "dv", "dg"):
        cols += list(range(off[name], off[name] + 256))
    for g in range(SSD_GROUPS):
        blk = [pad] * 128
        for i in range(4):
            blk[i] = off["dt"] + 4 * g + i
            blk[8 + i] = off["dt"] + SSD_HEADS + 4 * g + i
        cols += blk
    assert len(cols) == NP
    return np.asarray(cols, np.int32)


def _out_row_perm():
    rows = list(range(SSD_WIDTH))
    for h in GQ_HEAD_ORDER:
        rows += list(range(SSD_WIDTH + 64 * h, SSD_WIDTH + 64 * h + 64))
    rows += list(range(SSD_WIDTH + 256, SSD_WIDTH + 512))
    return np.asarray(rows, np.int32)


def _take_runs(arr, idx, axis, pad_index):
    pieces, start = [], 0
    idx = [int(i) for i in idx]
    while start < len(idx):
        end = start + 1
        if idx[start] == pad_index:
            while end < len(idx) and idx[end] == pad_index:
                end += 1
            shape = list(arr.shape)
            shape[axis] = end - start
            pieces.append(jnp.zeros(shape, arr.dtype))
        else:
            while end < len(idx) and idx[end] == idx[end - 1] + 1 and idx[end] != pad_index:
                end += 1
            pieces.append(lax.slice_in_dim(arr, idx[start], idx[end - 1] + 1, axis=axis))
        start = end
    return jnp.concatenate(pieces, axis=axis)


def _dot(a, b):
    return jnp.dot(a, b, preferred_element_type=F32)


def _split3(x):
    hi = x.astype(BF16)
    r1 = x - hi.astype(F32)
    mid = r1.astype(BF16)
    lo = (r1 - mid.astype(F32)).astype(BF16)
    return hi, mid, lo


def _dot_exact_lhs(x, m_bf16):
    hi, mid, lo = _split3(x)
    return _dot(hi, m_bf16) + _dot(mid, m_bf16) + _dot(lo, m_bf16)


def _dot_exact_rhs(m_bf16, x):
    hi, mid, lo = _split3(x)
    return _dot(m_bf16, hi) + _dot(m_bf16, mid) + _dot(m_bf16, lo)


def _silu(x):
    return x * jax.nn.sigmoid(x)


def _seg_ones(width, seg):
    r = lax.broadcasted_iota(jnp.int32, (width, width), 0)
    c = lax.broadcasted_iota(jnp.int32, (width, width), 1)
    same = (r & ~(seg - 1)) == (c & ~(seg - 1))
    return jnp.where(same, 1.0, 0.0).astype(BF16)


def _seg_rms(x, seg, seg_mat):
    ss = _dot_exact_lhs(x * x, seg_mat)
    return x * lax.rsqrt(ss * (1.0 / seg) + EPS)


def _rope(x, cos, sin_signed, half):
    w = x.shape[-1]
    lane = lax.broadcasted_iota(jnp.int32, x.shape, 1)
    first = (lane & (2 * half - 1)) < half
    swapped = jnp.where(first, pltpu.roll(x, w - half, 1), pltpu.roll(x, half, 1))
    return x * cos + swapped * sin_signed


def _mod_kernel(cs_ref, w_ref, b_ref, o_ref):
    s = _silu(cs_ref[...]).astype(BF16)
    o_ref[...] = _dot(s, w_ref[...].astype(BF16)) + b_ref[...]


def _mod_call(cs, w_mod, b_mod):
    depth = w_mod.shape[0]
    nrow = cs.shape[0]
    tn = 1024
    return pl.pallas_call(
        _mod_kernel,
        grid=(depth, 3 * D_MODEL // tn),
        in_specs=[
            pl.BlockSpec((nrow, D_MODEL), lambda l, j: (0, 0)),
            pl.BlockSpec((None, D_MODEL, tn), lambda l, j: (l, 0, j)),
            pl.BlockSpec((None, 1, tn), lambda l, j: (l, 0, j)),
        ],
        out_specs=pl.BlockSpec((None, nrow, tn), lambda l, j: (l, 0, j)),
        out_shape=jax.ShapeDtypeStruct((depth, nrow, 3 * D_MODEL), F32),
        compiler_params=pltpu.CompilerParams(
            dimension_semantics=("arbitrary", "arbitrary")),
        name="mod_proj",
    )(cs, w_mod, b_mod.reshape(depth, 1, 3 * D_MODEL))


def _inproj_kernel(h_ref, mod_ref, g_ref, w_ref, o_ref):
    x = h_ref[...]
    ms = jnp.mean(x * x, axis=-1, keepdims=True)
    y = x * lax.rsqrt(ms + EPS) * g_ref[...]
    sh = mod_ref[:, 0:D_MODEL]
    sc = mod_ref[:, D_MODEL:2 * D_MODEL]
    u = (y * (1.0 + sc) + sh).astype(BF16)
    tn = 512
    for j in range(NP // tn):
        o_ref[:, j * tn:(j + 1) * tn] = _dot(u, w_ref[:, j * tn:(j + 1) * tn])


def _inproj_call(h, mod3, g_pre, w_bf16, tiles_per_row, fixed_row):
    n_tok = h.shape[0]
    tm = 512
    if fixed_row is None:
        mod_idx = lambda i: (i // tiles_per_row, 0, 0)
    else:
        mod_idx = lambda i: (fixed_row, 0, 0)
    return pl.pallas_call(
        _inproj_kernel,
        grid=(n_tok // tm,),
        in_specs=[
            pl.BlockSpec((tm, D_MODEL), lambda i: (i, 0)),
            pl.BlockSpec((None, 1, 3 * D_MODEL), mod_idx),
            pl.BlockSpec((1, D_MODEL), lambda i: (0, 0)),
            pl.BlockSpec((D_MODEL, NP), lambda i: (0, 0)),
        ],
        out_specs=pl.BlockSpec((tm, NP), lambda i: (i, 0)),
        out_shape=jax.ShapeDtypeStruct((n_tok, NP), F32),
        compiler_params=pltpu.CompilerParams(
            dimension_semantics=("arbitrary",), vmem_limit_bytes=VMEM_LIMIT),
        name="in_proj",
    )(h, mod3, g_pre, w_bf16)


def _outproj_kernel(ys_ref, yg_ref, yd_ref, h_ref, mod_ref, g_ref, w_ref, o_ref):
    o = (_dot(ys_ref[...], w_ref[0:512, :]) + _dot(yg_ref[...], w_ref[512:768, :])
         + _dot(yd_ref[...], w_ref[768:1024, :]))
    ms = jnp.mean(o * o, axis=-1, keepdims=True)
    n = o * lax.rsqrt(ms + EPS) * g_ref[...]
    gt = mod_ref[:, 2 * D_MODEL:3 * D_MODEL]
    o_ref[...] = h_ref[...] + gt * n


def _outproj_call(ys, yg, yd, h, mod3, g_post, w_bf16, tiles_per_row, fixed_row):
    n_tok = h.shape[0]
    tm = 512
    if fixed_row is None:
        mod_idx = lambda i: (i // tiles_per_row, 0, 0)
    else:
        mod_idx = lambda i: (fixed_row, 0, 0)
    return pl.pallas_call(
        _outproj_kernel,
        grid=(n_tok // tm,),
        in_specs=[
            pl.BlockSpec((tm, 512), lambda i: (i, 0)),
            pl.BlockSpec((tm, 256), lambda i: (i, 0)),
            pl.BlockSpec((tm, 256), lambda i: (i, 0)),
            pl.BlockSpec((tm, D_MODEL), lambda i: (i, 0)),
            pl.BlockSpec((None, 1, 3 * D_MODEL), mod_idx),
            pl.BlockSpec((1, D_MODEL), lambda i: (0, 0)),
            pl.BlockSpec((D_MODEL, D_MODEL), lambda i: (0, 0)),
        ],
        out_specs=pl.BlockSpec((tm, D_MODEL), lambda i: (i, 0)),
        out_shape=jax.ShapeDtypeStruct((n_tok, D_MODEL), F32),
        compiler_params=pltpu.CompilerParams(
            dimension_semantics=("arbitrary",), vmem_limit_bytes=VMEM_LIMIT),
        name="out_proj",
    )(ys, yg, yd, h, mod3, g_post, w_bf16)


def _attend_many(lhs_list, kt_ref, vext_refs):
    outs = []
    s_next = _dot(lhs_list[0], kt_ref[...])
    for i, vext_ref in enumerate(vext_refs):
        s = s_next
        if i + 1 < len(lhs_list):
            s_next = _dot(lhs_list[i + 1], kt_ref[...])
        m = jnp.max(s, axis=-1, keepdims=True)
        p = jnp.exp2(s - m).astype(BF16)
        oe = _dot(p, vext_ref[...])
        outs.append(oe[:, 0:128] / oe[:, 128:256])
    return outs


def _gqa_kernel(*refs, n_ctx, n_lat, rope_q):
    it = iter(refs)
    q_ref, gg_ref, kvc_ref = next(it), next(it), next(it)
    kvl_ref = next(it) if n_lat else None
    if rope_q:
        cosq_ref, sinq_ref = next(it), next(it)
    if n_lat:
        cosk_ref, sink_ref = next(it), next(it)
    qg_ref, kg_ref = next(it), next(it)
    y_ref = next(it)
    kt_s, vext_s = next(it), next(it)

    seg128 = _seg_ones(128, 64)

    @pl.when(pl.program_id(1) == 0)
    def _prep_kv():
        kc = _seg_rms(kvc_ref[:, 0:128], 64, seg128) * kg_ref[...]
        kt_s[:, 0:n_ctx] = kc.T.astype(BF16)
        vext_s[0:n_ctx, 0:128] = kvc_ref[:, 128:256].astype(BF16)
        if n_lat:
            kl = _seg_rms(kvl_ref[:, 0:128], 64, seg128) * kg_ref[...]
            kl = _rope(kl, cosk_ref[...], sink_ref[...], 32)
            kt_s[:, n_ctx:n_ctx + n_lat] = kl.T.astype(BF16)
            vext_s[n_ctx:n_ctx + n_lat, 0:128] = kvl_ref[:, 128:256].astype(BF16)
        vext_s[:, 128:256] = jnp.ones((n_ctx + n_lat, 128), BF16)

    seg256 = _seg_ones(256, 64)
    q = _seg_rms(q_ref[...], 64, seg256) * qg_ref[...]
    if rope_q:
        q = _rope(q, cosq_ref[...], sinq_ref[...], 32)
    q = q * (GQA_HEAD_DIM ** -0.5 * LOG2E)
    tq = q.shape[0]
    lane = lax.broadcasted_iota(jnp.int32, (tq, 128), 1)
    lhs_list = []
    for half in range(2):
        qh = q[:, 128 * half:128 * half + 128]
        for kv in range(2):
            in_kv = (lane >= 64 * kv) & (lane < 64 * kv + 64)
            lhs_list.append(jnp.where(in_kv, qh, 0.0).astype(BF16))
    outs = _attend_many(lhs_list, kt_s, [vext_s] * 4)
    for half in range(2):
        o = jnp.where(lane < 64, outs[2 * half], outs[2 * half + 1])
        gate = _silu(gg_ref[:, 128 * half:128 * half + 128])
        y_ref[:, 128 * half:128 * half + 128] = (o * gate).astype(BF16)


def _gqa_call(p_q, p_c, p_l, tabs, qg, kg, batch, n_ctx, n_lat, rope_q):
    t_total = p_q.shape[0] // batch
    tq = 256
    nq = t_total // tq
    in_specs = [
        pl.BlockSpec((tq, 256), lambda b, i: (b * nq + i, 6)),
        pl.BlockSpec((tq, 256), lambda b, i: (b * nq + i, 8)),
        pl.BlockSpec((n_ctx, 256), lambda b, i: (b, 7)),
    ]
    args = [p_q, p_q, p_c]
    if n_lat:
        in_specs.append(pl.BlockSpec((n_lat, 256), lambda b, i: (b, 7)))
        args.append(p_l)
    if rope_q:
        in_specs += [pl.BlockSpec((tq, 256), lambda b, i: (i, 0))] * 2
        args += [tabs["cos_g"], tabs["sin_g"]]
    if n_lat:
        in_specs += [pl.BlockSpec((n_lat, 128), lambda b, i: (0, 0))] * 2
        args += [tabs["cos_g"], tabs["sin_g"]]
    in_specs += [pl.BlockSpec((1, 256), lambda b, i: (0, 0)),
                 pl.BlockSpec((1, 128), lambda b, i: (0, 0))]
    args += [qg, kg]
    s_keys = n_ctx + n_lat
    return pl.pallas_call(
        functools.partial(_gqa_kernel, n_ctx=n_ctx, n_lat=n_lat, rope_q=rope_q),
        grid=(batch, nq),
        in_specs=in_specs,
        out_specs=pl.BlockSpec((tq, 256), lambda b, i: (b * nq + i, 0)),
        out_shape=jax.ShapeDtypeStruct((p_q.shape[0], 256), BF16),
        scratch_shapes=[pltpu.VMEM((128, s_keys), BF16),
                        pltpu.VMEM((s_keys, 256), BF16)],
        compiler_params=pltpu.CompilerParams(
            dimension_semantics=("arbitrary", "arbitrary"),
            vmem_limit_bytes=VMEM_LIMIT),
        name="gqa_attn",
    )(*args)


def _diff_kernel(*refs, n_ctx, n_lat, rope_q, lam_init):
    it = iter(refs)
    q_ref, dg_ref, kc_ref, vc_ref = next(it), next(it), next(it), next(it)
    if n_lat:
        kl_ref, vl_ref = next(it), next(it)
    if rope_q:
        cosq_ref, sinq_ref = next(it), next(it)
    if n_lat:
        cosk_ref, sink_ref = next(it), next(it)
    lam_ref, ng_ref = next(it), next(it)
    y_ref = next(it)
    kt_s, vlo_s, vhi_s = next(it), next(it), next(it)
    s_keys = n_ctx + n_lat

    @pl.when(pl.program_id(1) == 0)
    def _prep_kv():
        kt_s[:, 0:n_ctx] = kc_ref[...].T.astype(BF16)
        vlo_s[0:n_ctx, 0:128] = vc_ref[:, 0:128].astype(BF16)
        vhi_s[0:n_ctx, 0:128] = vc_ref[:, 128:256].astype(BF16)
        if n_lat:
            kl = _rope(kl_ref[...], cosk_ref[...], sink_ref[...], 16)
            kt_s[:, n_ctx:s_keys] = kl.T.astype(BF16)
            vlo_s[n_ctx:s_keys, 0:128] = vl_ref[:, 0:128].astype(BF16)
            vhi_s[n_ctx:s_keys, 0:128] = vl_ref[:, 128:256].astype(BF16)
        ones = jnp.ones((s_keys, 128), BF16)
        vlo_s[:, 128:256] = ones
        vhi_s[:, 128:256] = ones

    lp = lam_ref[...]
    lam = (jnp.exp(jnp.sum(lp[0:1, :] * lp[1:2, :], axis=-1, keepdims=True))
           - jnp.exp(jnp.sum(lp[2:3, :] * lp[3:4, :], axis=-1, keepdims=True))
           + lam_init)

    q = q_ref[...]
    if rope_q:
        q = _rope(q, cosq_ref[...], sinq_ref[...], 16)
    q = q * (DIFF_QK_DIM ** -0.5 * LOG2E)
    tq = q.shape[0]
    lane256 = lax.broadcasted_iota(jnp.int32, (tq, 256), 1)
    lane128 = lax.broadcasted_iota(jnp.int32, (tq, 128), 1)
    seg128 = _seg_ones(128, 64)
    lhs_list, v_list = [], []
    for mp in range(2 * DIFF_HEADS):
        in_map = (lane256 >= 32 * mp) & (lane256 < 32 * mp + 32)
        lhs_list.append(jnp.where(in_map, q, 0.0).astype(BF16))
        v_list.append(vlo_s if mp < DIFF_HEADS else vhi_s)
    maps = _attend_many(lhs_list, kt_s, v_list)
    for half in range(2):
        heads = [maps[4 * half + 2 * hh] - lam * maps[4 * half + 2 * hh + 1]
                 for hh in range(2)]
        o = jnp.where(lane128 < 64, heads[0], heads[1])
        n = _seg_rms(o, 64, seg128) * ng_ref[:, 128 * half:128 * half + 128]
        n = n * (1.0 - lam_init)
        gate = _silu(dg_ref[:, 128 * half:128 * half + 128])
        y_ref[:, 128 * half:128 * half + 128] = (n * gate).astype(BF16)


def _diff_call(p_q, p_c, p_l, tabs, lam_params, ng, batch, n_ctx, n_lat, rope_q,
               lam_init):
    t_total = p_q.shape[0] // batch
    tq = 256
    nq = t_total // tq
    in_specs = [
        pl.BlockSpec((tq, 256), lambda b, i: (b * nq + i, 9)),
        pl.BlockSpec((tq, 256), lambda b, i: (b * nq + i, 12)),
        pl.BlockSpec((n_ctx, 256), lambda b, i: (b, 10)),
        pl.BlockSpec((n_ctx, 256), lambda b, i: (b, 11)),
    ]
    args = [p_q, p_q, p_c, p_c]
    if n_lat:
        in_specs += [pl.BlockSpec((n_lat, 256), lambda b, i: (b, 10)),
                     pl.BlockSpec((n_lat, 256), lambda b, i: (b, 11))]
        args += [p_l, p_l]
    if rope_q:
        in_specs += [pl.BlockSpec((tq, 256), lambda b, i: (i, 0))] * 2
        args += [tabs["cos_d"], tabs["sin_d"]]
    if n_lat:
        in_specs += [pl.BlockSpec((n_lat, 256), lambda b, i: (0, 0))] * 2
        args += [tabs["cos_d"], tabs["sin_d"]]
    in_specs += [pl.BlockSpec((4, DIFF_QK_DIM), lambda b, i: (0, 0)),
                 pl.BlockSpec((1, 256), lambda b, i: (0, 0))]
    args += [lam_params, ng]
    s_keys = n_ctx + n_lat
    return pl.pallas_call(
        functools.partial(_diff_kernel, n_ctx=n_ctx, n_lat=n_lat, rope_q=rope_q,
                          lam_init=lam_init),
        grid=(batch, nq),
        in_specs=in_specs,
        out_specs=pl.BlockSpec((tq, 256), lambda b, i: (b * nq + i, 0)),
        out_shape=jax.ShapeDtypeStruct((p_q.shape[0], 256), BF16),
        scratch_shapes=[pltpu.VMEM((256, s_keys), BF16),
                        pltpu.VMEM((s_keys, 256), BF16),
                        pltpu.VMEM((s_keys, 256), BF16)],
        compiler_params=pltpu.CompilerParams(
            dimension_semantics=("arbitrary", "arbitrary"),
            vmem_limit_bytes=VMEM_LIMIT),
        name="diff_attn",
    )(*args)


def _ssd_kernel(xc_ref, xl_ref, bc_ref, bl_ref, cc_ref, cl_ref, zc_ref, zl_ref,
                dtc_ref, dtl_ref, cwx_ref, cwb_ref, cwc_ref, cbx_ref, cbb_ref,
                cbc_ref, par_ref, dsk_ref, ng_ref, *rest, n_ctx, n_lat, ctx_out):
    if ctx_out:
        oc_ref, ol_ref = rest[0], rest[1]
        rest = rest[2:]
    else:
        oc_ref, ol_ref = None, rest[0]
        rest = rest[1:]
    xp_s, dtv_s, yacc_s, sloc_s, e_s, cbf_s, sin_s, dec_s, vun_s = rest

    g = pl.program_id(1)
    ncc = n_ctx // CHUNK
    ncl = n_lat // CHUNK
    nch = ncc + ncl
    GROUP = 3 if nch % 3 == 0 else 2
    assert nch % GROUP == 0
    t_all = n_ctx + n_lat
    lat0 = n_ctx + 2 * HALO

    zeros_h = jnp.zeros((HALO, 512), F32)
    xp_s[0:HALO, :] = zeros_h
    xp_s[HALO:HALO + n_ctx, 0:256] = xc_ref[...]
    xp_s[HALO:HALO + n_ctx, 256:384] = bc_ref[...]
    xp_s[HALO:HALO + n_ctx, 384:512] = cc_ref[...]
    xp_s[HALO + n_ctx:lat0, :] = zeros_h
    xp_s[lat0:lat0 + n_lat, 0:256] = xl_ref[...]
    xp_s[lat0:lat0 + n_lat, 256:384] = bl_ref[...]
    xp_s[lat0:lat0 + n_lat, 384:512] = cl_ref[...]
    xp_s[lat0 + n_lat:lat0 + n_lat + HALO, :] = zeros_h

    bias = par_ref[1:2, :]
    a_col = -jnp.exp(par_ref[8:24, :])

    def _softplus(v):
        return jnp.maximum(v, 0.0) + jnp.log1p(jnp.exp(-jnp.abs(v)))

    dtv_s[0:n_ctx, :] = _softplus(dtc_ref[...] + bias)
    dtv_s[n_ctx:t_all, :] = _softplus(dtl_ref[...] + bias)

    r128 = lax.broadcasted_iota(jnp.int32, (CHUNK, CHUNK), 0)
    c128 = lax.broadcasted_iota(jnp.int32, (CHUNK, CHUNK), 1)
    lower = c128 <= r128
    upper = c128 >= r128
    tril = jnp.where(lower, 1.0, 0.0).astype(BF16)
    triu = jnp.where(upper, 1.0, 0.0).astype(BF16)
    fwd_row = lax.broadcasted_iota(jnp.int32, (16, CHUNK), 0) < 8
    fwd_row1 = lax.broadcasted_iota(jnp.int32, (16, 1), 0) < 8
    er = lax.broadcasted_iota(jnp.int32, (CHUNK, 1024), 0)
    ec = lax.broadcasted_iota(jnp.int32, (CHUNK, 1024), 1)
    src_lane = 16 + 16 * (ec >> 9) + 8 * ((ec >> 8) & 1) + ((ec >> 6) & 3)
    expand = jnp.where(er == src_lane, 1.0, 0.0).astype(BF16)
    lane256 = lax.broadcasted_iota(jnp.int32, (CHUNK, 256), 1)
    col512 = lax.broadcasted_iota(jnp.int32, (1, 512), 1)

    cw = jnp.concatenate([cwx_ref[...], cwb_ref[...], cwc_ref[...]], axis=1)
    cb = jnp.concatenate([cbx_ref[...], cbb_ref[...], cbc_ref[...]], axis=1)
    dsk = dsk_ref[...]

    def phase_a(grp, carry):
        cs = [grp * GROUP + j for j in range(GROUP)]
        r0s = [pl.multiple_of(c * CHUNK, CHUNK) for c in cs]

        def conv(c):
            wstart = pl.multiple_of(c * CHUNK + jnp.where(c >= ncc, HALO, 0), 8)
            win = xp_s[pl.ds(wstart, CHUNK + 2 * HALO), :]
            acc = jnp.broadcast_to(cb, (CHUNK, 512))
            for k in range(CONV_K):
                d = k - CONV_K // 2
                if d == 0:
                    tap = win[HALO:HALO + CHUNK, :]
                else:
                    tap = pltpu.roll(win, (-d) % (CHUNK + 2 * HALO), 0)[HALO:HALO + CHUNK, :]
                acc = acc + cw[k:k + 1, :] * tap
            return _silu(acc)

        dtts = [dtv_s[pl.ds(r0, CHUNK), :].T[0:16, :] for r0 in r0s]
        a_ts = [dtt * a_col for dtt in dtts]
        acol_ts = [jnp.where(fwd_row, _dot_exact_lhs(a_t, triu), _dot_exact_lhs(a_t, tril))
                   for a_t in a_ts]
        us = [conv(c) for c in cs]
        xus = [u[:, 0:256] for u in us]
        bts = [u[:, 256:384].T.astype(BF16) for u in us]
        cbfs = [u[:, 384:512].astype(BF16) for u in us]
        xbfs = [xu.astype(BF16) for xu in xus]
        gmats = [_dot(cbf, bt) for cbf, bt in zip(cbfs, bts)]
        tms = []
        for dtt, acol_t in zip(dtts, acol_ts):
            tot = jnp.where(fwd_row1, acol_t[:, CHUNK - 1:CHUNK], acol_t[:, 0:1])
            w_t = dtt * jnp.exp(tot - acol_t)
            e_t = jnp.exp(acol_t)
            stacked = jnp.concatenate(
                [acol_t, w_t, e_t, jnp.zeros((CHUNK - 48, CHUNK), F32)], axis=0)
            tms.append(stacked.T)
        wes = []
        for tm in tms:
            tm_hi = tm.astype(BF16)
            tm_lo = (tm - tm_hi.astype(F32)).astype(BF16)
            wes.append(_dot(tm_hi, expand) + _dot(tm_lo, expand))
        ydiags = [jnp.zeros((CHUNK, 256), F32) for _ in cs]
        for i in range(4):
            in_head = (lane256 >= 64 * i) & (lane256 < 64 * i + 64)
            for j in range(GROUP):
                tm, acol_t, dtt = tms[j], acol_ts[j], dtts[j]
                arg = jnp.where(lower, tm[:, i:i + 1] - acol_t[i:i + 1, :],
                                tm[:, 8 + i:9 + i] - acol_t[8 + i:9 + i, :])
                scale = (jnp.where(lower, dtt[i:i + 1, :], 0.0)
                         + jnp.where(upper, dtt[8 + i:9 + i, :], 0.0))
                wmat = (gmats[j] * jnp.exp(arg) * scale).astype(BF16)
                ydiags[j] = jnp.where(in_head, _dot(wmat, xbfs[j]), ydiags[j])
        for j, c in enumerate(cs):
            w512 = wes[j][:, 0:512]
            e512 = wes[j][:, 512:1024]
            xdw = (jnp.concatenate([xus[j], xus[j]], axis=1) * w512).astype(BF16)
            sloc_s[c] = _dot(bts[j], xdw)
            dec512 = jnp.where(col512 < 256, e512[CHUNK - 1:CHUNK, :], e512[0:1, :])
            dec_s[c] = jnp.broadcast_to(dec512, (8, 512))
            e_s[c] = e512
            cbf_s[c] = cbfs[j]
            yacc_s[pl.ds(r0s[j], CHUNK), :] = ydiags[j] + xus[j] * dsk
        return carry

    lax.fori_loop(0, nch // GROUP, phase_a, 0)

    fwd_order = list(range(nch))
    bwd_order = list(range(ncc - 1, -1, -1)) + list(range(nch - 1, ncc - 1, -1))
    for order, lo in ((fwd_order, 0), (bwd_order, 256)):
        state = jnp.zeros((SSD_STATE, 256), F32)
        for c in order:
            sin_s[c, :, lo:lo + 256] = state.astype(BF16)
            state = (state * dec_s[c, 0:1, lo:lo + 256]
                     + sloc_s[c, :, lo:lo + 256])

    def phase_c(c, z_ref, zrow):
        yo = _dot(cbf_s[c], sin_s[c]) * e_s[c]
        r0 = c * CHUNK if isinstance(c, int) else pl.multiple_of(c * CHUNK, CHUNK)
        y = yacc_s[pl.ds(r0, CHUNK), :] + yo[:, 0:256] + yo[:, 256:512]
        v = y * _silu(z_ref[pl.ds(zrow, CHUNK), :])

        @pl.when(g == 0)
        def _():
            vun_s[pl.ds(r0, CHUNK), 0:256] = v

        @pl.when(g == 1)
        def _():
            vun_s[pl.ds(r0, CHUNK), 256:512] = v

    if ctx_out:
        for c in range(ncc):
            phase_c(c, zc_ref, c * CHUNK)

    def phase_c_lat(k, carry):
        phase_c(k + ncc, zl_ref, pl.multiple_of(k * CHUNK, CHUNK))
        return carry

    lax.fori_loop(0, ncl, phase_c_lat, 0)

    @pl.when(g == SSD_GROUPS - 1)
    def _finalize():
        ng = ng_ref[...]

        def norm_rows(r0, nrows):
            v = vun_s[pl.ds(r0, nrows), :]
            ms = jnp.mean(v * v, axis=-1, keepdims=True)
            return (v * lax.rsqrt(ms + EPS) * ng).astype(BF16)

        if ctx_out:
            oc_ref[...] = norm_rows(0, n_ctx)

        def fin(k, carry):
            r0 = pl.multiple_of(k * 256, 256)
            ol_ref[pl.ds(r0, 256), :] = norm_rows(n_ctx + r0, 256)
            return carry

        lax.fori_loop(0, n_lat // 256, fin, 0)


def _ssd_call(p_c, p_l, conv_w8, conv_b, par, dsk, ng, batch, n_ctx, n_lat, ctx_out):
    nch = (n_ctx + n_lat) // CHUNK
    t_all = n_ctx + n_lat
    in_specs = [
        pl.BlockSpec((n_ctx, 256), lambda b, g: (b, g)),
        pl.BlockSpec((n_lat, 256), lambda b, g: (b, g)),
        pl.BlockSpec((n_ctx, 128), lambda b, g: (b, 4 + g)),
        pl.BlockSpec((n_lat, 128), lambda b, g: (b, 4 + g)),
        pl.BlockSpec((n_ctx, 128), lambda b, g: (b, 6 + g)),
        pl.BlockSpec((n_lat, 128), lambda b, g: (b, 6 + g)),
        pl.BlockSpec((n_ctx, 256), lambda b, g: (b, 4 + g)),
        pl.BlockSpec((n_lat, 256), lambda b, g: (b, 4 + g)),
        pl.BlockSpec((n_ctx, 128), lambda b, g: (b, 26 + g)),
        pl.BlockSpec((n_lat, 128), lambda b, g: (b, 26 + g)),
        pl.BlockSpec((8, 256), lambda b, g: (0, g)),
        pl.BlockSpec((8, 128), lambda b, g: (0, 4 + g)),
        pl.BlockSpec((8, 128), lambda b, g: (0, 6 + g)),
        pl.BlockSpec((1, 256), lambda b, g: (0, g)),
        pl.BlockSpec((1, 128), lambda b, g: (0, 4 + g)),
        pl.BlockSpec((1, 128), lambda b, g: (0, 6 + g)),
        pl.BlockSpec((None, 24, 128), lambda b, g: (g, 0, 0)),
        pl.BlockSpec((1, 256), lambda b, g: (0, g)),
        pl.BlockSpec((1, 512), lambda b, g: (0, 0)),
    ]
    args = [p_c, p_l, p_c, p_l, p_c, p_l, p_c, p_l, p_c, p_l,
            conv_w8, conv_w8, conv_w8, conv_b, conv_b, conv_b, par, dsk, ng]
    out_specs = [pl.BlockSpec((n_lat, 512), lambda b, g: (b, 0))]
    out_shape = [jax.ShapeDtypeStruct((batch * n_lat, 512), BF16)]
    if ctx_out:
        out_specs = [pl.BlockSpec((n_ctx, 512), lambda b, g: (b, 0))] + out_specs
        out_shape = [jax.ShapeDtypeStruct((batch * n_ctx, 512), BF16)] + out_shape
    scratch = [
        pltpu.VMEM((t_all + 3 * HALO, 512), F32),
        pltpu.VMEM((t_all, 128), F32),
        pltpu.VMEM((t_all, 256), F32),
        pltpu.VMEM((nch, SSD_STATE, 512), F32),
        pltpu.VMEM((nch, CHUNK, 512), F32),
        pltpu.VMEM((nch, CHUNK, 128), BF16),
        pltpu.VMEM((nch, SSD_STATE, 512), BF16),
        pltpu.VMEM((nch, 8, 512), F32),
        pltpu.VMEM((t_all, 512), F32),
    ]
    outs = pl.pallas_call(
        functools.partial(_ssd_kernel, n_ctx=n_ctx, n_lat=n_lat, ctx_out=ctx_out),
        grid=(batch, SSD_GROUPS),
        in_specs=in_specs,
        out_specs=out_specs,
        out_shape=out_shape,
        scratch_shapes=scratch,
        compiler_params=pltpu.CompilerParams(
            dimension_semantics=("arbitrary", "arbitrary"),
            vmem_limit_bytes=VMEM_LIMIT),
        name="ssd_scan",
    )(*args)
    if ctx_out:
        return outs[1], outs[0]
    return outs[0], None


def _rope_tables(n_lat):
    rows = n_lat // GRID_W
    row_idx = jnp.repeat(jnp.arange(rows), GRID_W).astype(F32)
    col_idx = (jnp.arange(rows * GRID_W) % GRID_W).astype(F32)

    def tables(dim, reps):
        quarter = dim // 4
        inv = ROPE_BASE ** (-jnp.arange(quarter, dtype=F32) / quarter)
        ang = jnp.concatenate([row_idx[:, None] * inv, col_idx[:, None] * inv], axis=-1)
        cos, sin = jnp.cos(ang), jnp.sin(ang)
        cos2 = jnp.concatenate([cos, cos], axis=-1)
        sin2 = jnp.concatenate([-sin, sin], axis=-1)
        return jnp.tile(cos2, (1, reps)), jnp.tile(sin2, (1, reps))

    cos_g, sin_g = tables(GQA_HEAD_DIM, 4)
    cos_d, sin_d = tables(DIFF_QK_DIM, 8)
    return {"cos_g": cos_g, "sin_g": sin_g, "cos_d": cos_d, "sin_d": sin_d}


def kernel(x, c, ctx, c_ctx, w_mod, b_mod, g_pre, g_post, w_in, conv_w, conv_b,
           a_log_fwd, a_log_bwd, dt_bias_fwd, dt_bias_bwd, d_skip, ssd_norm_g,
           q_norm_g, k_norm_g, diff_lambda, diff_norm_g, w_out):
    batch, n_lat, _ = x.shape
    n_ctx = ctx.shape[1]
    depth = w_mod.shape[0]
    assert n_lat % 512 == 0 and n_ctx % 256 == 0 and (batch * n_ctx) % 512 == 0
    assert batch + 1 <= 16

    w_in_p = _take_runs(w_in.astype(BF16), _in_col_perm(), 2, IN_COLS)
    w_out_p = _take_runs(w_out.astype(BF16), _out_row_perm(), 1, None)
    conv_w8 = jnp.pad(conv_w, ((0, 0), (0, 8 - CONV_K), (0, 0)))
    conv_b1 = conv_b[:, None, :]

    def group16(fwd, bwd):
        out = jnp.zeros((depth, SSD_GROUPS, 16), F32)
        for g in range(SSD_GROUPS):
            out = out.at[:, g, 0:4].set(fwd[:, 4 * g:4 * g + 4])
            out = out.at[:, g, 8:12].set(bwd[:, 4 * g:4 * g + 4])
        return out

    ssd_par = jnp.zeros((depth, SSD_GROUPS, 24, 128), F32)
    ssd_par = ssd_par.at[:, :, 1, 0:16].set(group16(dt_bias_fwd, dt_bias_bwd))
    ssd_par = ssd_par.at[:, :, 8:24, :].set(
        jnp.broadcast_to(group16(a_log_fwd, a_log_bwd)[..., None],
                         (depth, SSD_GROUPS, 16, 128)))
    dsk = jnp.repeat(d_skip, SSD_HEAD_DIM, axis=1)[:, None, :]
    qg = jnp.tile(q_norm_g, (1, 4))[:, None, :]
    kg = jnp.tile(k_norm_g, (1, 2))[:, None, :]
    dng = jnp.tile(diff_norm_g, (1, 4))[:, None, :]
    tabs = _rope_tables(n_lat)

    cs = jnp.concatenate(
        [c, c_ctx[None, :], jnp.zeros((16 - batch - 1, D_MODEL), F32)], axis=0)
    mod_all = _mod_call(cs, w_mod, b_mod)

    h = x.reshape(batch * n_lat, D_MODEL)
    hc = ctx.reshape(batch * n_ctx, D_MODEL)
    for l in range(depth):
        ctx_out = l < depth - 1
        lam_init = 0.8 - 0.6 * float(np.exp(-0.3 * l))
        mod3 = mod_all[l][:, None, :]
        p_l = _inproj_call(h, mod3, g_pre[l][None, :], w_in_p[l], n_lat // 512, None)
        p_c = _inproj_call(hc, mod3, g_pre[l][None, :], w_in_p[l], None, batch)

        ys_l, ys_c = _ssd_call(p_c, p_l, conv_w8[l], conv_b1[l], ssd_par[l], dsk[l],
                               ssd_norm_g[l][None, :], batch, n_ctx, n_lat, ctx_out)
        yg_l = _gqa_call(p_l, p_c, p_l, tabs, qg[l], kg[l], batch, n_ctx, n_lat, True)
        yd_l = _diff_call(p_l, p_c, p_l, tabs, diff_lambda[l], dng[l], batch, n_ctx,
                          n_lat, True, lam_init)
        h = _outproj_call(ys_l, yg_l, yd_l, h, mod3, g_post[l][None, :], w_out_p[l],
                          n_lat // 512, None)
        if ctx_out:
            yg_c = _gqa_call(p_c, p_c, None, tabs, qg[l], kg[l], batch, n_ctx, 0, False)
            yd_c = _diff_call(p_c, p_c, None, tabs, diff_lambda[l], dng[l], batch,
                              n_ctx, 0, False, lam_init)
            hc = _outproj_call(ys_c, yg_c, yd_c, hc, mod3, g_post[l][None, :],
                               w_out_p[l], None, batch)
    return h.reshape(batch, n_lat, D_MODEL)
```

```python
import functools
import math

import numpy as np
import jax
import jax.numpy as jnp
from jax import lax
from jax.experimental import pallas as pl
from jax.experimental.pallas import tpu as pltpu

F32 = jnp.float32
BF16 = jnp.bfloat16

D_MODEL = 1024
GRID_W = 64
ROPE_BASE = 10000.0
EPS = 1e-6
LOG2E = 1.4426950408889634

SSD_WIDTH = 512
SSD_HEADS = 8
SSD_HEAD_DIM = 64
SSD_GROUPS = 2
SSD_STATE = 128
CHUNK = 128
CONV_K = 5
HALO = 8
GQA_HEADS = 4
GQA_HEAD_DIM = 64
DIFF_HEADS = 4
DIFF_QK_DIM = 32
DIFF_V_DIM = 64

_IN_SPLITS = (("xbc", 1024), ("z", 512), ("dt", 16), ("gq", 256), ("gk", 128),
              ("gv", 128), ("gg", 256), ("dq", 256), ("dk", 256), ("dv", 256),
              ("dg", 256))
IN_COLS = sum(s for _, s in _IN_SPLITS)
NP = 28 * 128
GQ_HEAD_ORDER = (0, 2, 1, 3)

VMEM_LIMIT = 56 * 1024 * 1024
ATTN_TQ = 1024
ATTN_SUB = 512


def _in_col_perm():
    off, o = {}, 0
    for name, size in _IN_SPLITS:
        off[name] = o
        o += size
    pad = IN_COLS
    cols = list(range(off["xbc"], off["xbc"] + 1024))
    cols += list(range(off["z"], off["z"] + 512))
    for h in GQ_HEAD_ORDER:
        cols += list(range(off["gq"] + 64 * h, off["gq"] + 64 * h + 64))
    cols += list(range(off["gk"], off["gk"] + 128))
    cols += list(range(off["gv"], off["gv"] + 128))
    for h in GQ_HEAD_ORDER:
        cols += list(range(off["gg"] + 64 * h, off["gg"] + 64 * h + 64))
    for name in ("dq", "dk", "dv", "dg"):
        cols += list(range(off[name], off[name] + 256))
    for g in range(SSD_GROUPS):
        blk = [pad] * 128
        for i in range(4):
            blk[i] = off["dt"] + 4 * g + i
            blk[8 + i] = off["dt"] + SSD_HEADS + 4 * g + i
        cols += blk
    assert len(cols) == NP
    return np.asarray(cols, np.int32)


def _out_row_perm():
    rows = list(range(SSD_WIDTH))
    for h in GQ_HEAD_ORDER:
        rows += list(range(SSD_WIDTH + 64 * h, SSD_WIDTH + 64 * h + 64))
    rows += list(range(SSD_WIDTH + 256, SSD_WIDTH + 512))
    return np.asarray(rows, np.int32)


def _take_runs(arr, idx, axis, pad_index):
    pieces, start = [], 0
    idx = [int(i) for i in idx]
    while start < len(idx):
        end = start + 1
        if idx[start] == pad_index:
            while end < len(idx) and idx[end] == pad_index:
                end += 1
            shape = list(arr.shape)
            shape[axis] = end - start
            pieces.append(jnp.zeros(shape, arr.dtype))
        else:
            while end < len(idx) and idx[end] == idx[end - 1] + 1 and idx[end] != pad_index:
                end += 1
            pieces.append(lax.slice_in_dim(arr, idx[start], idx[end - 1] + 1, axis=axis))
        start = end
    return jnp.concatenate(pieces, axis=axis)


def _dot(a, b):
    return jnp.dot(a, b, preferred_element_type=F32)


def _split3(x):
    hi = x.astype(BF16)
    r1 = x - hi.astype(F32)
    mid = r1.astype(BF16)
    lo = (r1 - mid.astype(F32)).astype(BF16)
    return hi, mid, lo


def _dot_exact_lhs(x, m_bf16):
    hi, mid, lo = _split3(x)
    return _dot(hi, m_bf16) + _dot(mid, m_bf16) + _dot(lo, m_bf16)


def _dot_exact_rhs(m_bf16, x):
    hi, mid, lo = _split3(x)
    return _dot(m_bf16, hi) + _dot(m_bf16, mid) + _dot(m_bf16, lo)


def _silu(x):
    return x * jax.nn.sigmoid(x)


def _seg_ones(width, seg):
    r = lax.broadcasted_iota(jnp.int32, (width, width), 0)
    c = lax.broadcasted_iota(jnp.int32, (width, width), 1)
    same = (r & ~(seg - 1)) == (c & ~(seg - 1))
    return jnp.where(same, 1.0, 0.0).astype(BF16)


def _seg_rms(x, seg, seg_mat):
    ss = _dot_exact_lhs(x * x, seg_mat)
    return x * lax.rsqrt(ss * (1.0 / seg) + EPS)


def _rope(x, cos, sin_signed, half):
    w = x.shape[-1]
    lane = lax.broadcasted_iota(jnp.int32, x.shape, 1)
    first = (lane & (2 * half - 1)) < half
    swapped = jnp.where(first, pltpu.roll(x, w - half, 1), pltpu.roll(x, half, 1))
    return x * cos + swapped * sin_signed


def _mod_kernel(cs_ref, w_ref, b_ref, o_ref):
    s = _silu(cs_ref[...]).astype(BF16)
    o_ref[...] = _dot(s, w_ref[...].astype(BF16)) + b_ref[...]


def _mod_call(cs, w_mod, b_mod):
    depth = w_mod.shape[0]
    nrow = cs.shape[0]
    tn = 1024
    return pl.pallas_call(
        _mod_kernel,
        grid=(depth, 3 * D_MODEL // tn),
        in_specs=[
            pl.BlockSpec((nrow, D_MODEL), lambda l, j: (0, 0)),
            pl.BlockSpec((None, D_MODEL, tn), lambda l, j: (l, 0, j)),
            pl.BlockSpec((None, 1, tn), lambda l, j: (l, 0, j)),
        ],
        out_specs=pl.BlockSpec((None, nrow, tn), lambda l, j: (l, 0, j)),
        out_shape=jax.ShapeDtypeStruct((depth, nrow, 3 * D_MODEL), F32),
        compiler_params=pltpu.CompilerParams(
            dimension_semantics=("arbitrary", "arbitrary")),
        name="mod_proj",
    )(cs, w_mod, b_mod.reshape(depth, 1, 3 * D_MODEL))


def _inproj_kernel(h_ref, mod_ref, g_ref, w_ref, o_ref):
    x = h_ref[...]
    ms = jnp.mean(x * x, axis=-1, keepdims=True)
    y = x * lax.rsqrt(ms + EPS) * g_ref[...]
    sh = mod_ref[:, 0:D_MODEL]
    sc = mod_ref[:, D_MODEL:2 * D_MODEL]
    u = (y * (1.0 + sc) + sh).astype(BF16)
    tn = 512
    for j in range(NP // tn):
        o_ref[:, j * tn:(j + 1) * tn] = _dot(u, w_ref[:, j * tn:(j + 1) * tn])


def _inproj_call(h, mod3, g_pre, w_bf16, tiles_per_row, fixed_row):
    n_tok = h.shape[0]
    tm = 512
    if fixed_row is None:
        mod_idx = lambda i: (i // tiles_per_row, 0, 0)
    else:
        mod_idx = lambda i: (fixed_row, 0, 0)
    return pl.pallas_call(
        _inproj_kernel,
        grid=(n_tok // tm,),
        in_specs=[
            pl.BlockSpec((tm, D_MODEL), lambda i: (i, 0)),
            pl.BlockSpec((None, 1, 3 * D_MODEL), mod_idx),
            pl.BlockSpec((1, D_MODEL), lambda i: (0, 0)),
            pl.BlockSpec((D_MODEL, NP), lambda i: (0, 0)),
        ],
        out_specs=pl.BlockSpec((tm, NP), lambda i: (i, 0)),
        out_shape=jax.ShapeDtypeStruct((n_tok, NP), F32),
        compiler_params=pltpu.CompilerParams(
            dimension_semantics=("arbitrary",), vmem_limit_bytes=VMEM_LIMIT),
        name="in_proj",
    )(h, mod3, g_pre, w_bf16)


def _outproj_kernel(ys_ref, yg_ref, yd_ref, h_ref, mod_ref, g_ref, w_ref, o_ref):
    o = (_dot(ys_ref[...], w_ref[0:512, :]) + _dot(yg_ref[...], w_ref[512:768, :])
         + _dot(yd_ref[...], w_ref[768:1024, :]))
    ms = jnp.mean(o * o, axis=-1, keepdims=True)
    n = o * lax.rsqrt(ms + EPS) * g_ref[...]
    gt = mod_ref[:, 2 * D_MODEL:3 * D_MODEL]
    o_ref[...] = h_ref[...] + gt * n


def _outproj_call(ys, yg, yd, h, mod3, g_post, w_bf16, tiles_per_row, fixed_row):
    n_tok = h.shape[0]
    tm = 512
    if fixed_row is None:
        mod_idx = lambda i: (i // tiles_per_row, 0, 0)
    else:
        mod_idx = lambda i: (fixed_row, 0, 0)
    return pl.pallas_call(
        _outproj_kernel,
        grid=(n_tok // tm,),
        in_specs=[
            pl.BlockSpec((tm, 512), lambda i: (i, 0)),
            pl.BlockSpec((tm, 256), lambda i: (i, 0)),
            pl.BlockSpec((tm, 256), lambda i: (i, 0)),
            pl.BlockSpec((tm, D_MODEL), lambda i: (i, 0)),
            pl.BlockSpec((None, 1, 3 * D_MODEL), mod_idx),
            pl.BlockSpec((1, D_MODEL), lambda i: (0, 0)),
            pl.BlockSpec((D_MODEL, D_MODEL), lambda i: (0, 0)),
        ],
        out_specs=pl.BlockSpec((tm, D_MODEL), lambda i: (i, 0)),
        out_shape=jax.ShapeDtypeStruct((n_tok, D_MODEL), F32),
        compiler_params=pltpu.CompilerParams(
            dimension_semantics=("arbitrary",), vmem_limit_bytes=VMEM_LIMIT),
        name="out_proj",
    )(ys, yg, yd, h, mod3, g_post, w_bf16)


def _attend_many(lhs_list, kt_ref, vext_refs):
    outs = []
    s_next = _dot(lhs_list[0], kt_ref[...])
    for i, vext_ref in enumerate(vext_refs):
        s = s_next
        if i + 1 < len(lhs_list):
            s_next = _dot(lhs_list[i + 1], kt_ref[...])
        m = jnp.max(s, axis=-1, keepdims=True)
        p = jnp.exp2(s - m).astype(BF16)
        oe = _dot(p, vext_ref[...])
        outs.append(oe[:, 0:128] / oe[:, 128:256])
    return outs


def _attend_diff_pairs(lhs_list, kt_ref, v_refs, lam):
    n_heads = len(v_refs)

    def scores(h):
        return _dot(lhs_list[2 * h], kt_ref[...]), _dot(lhs_list[2 * h + 1], kt_ref[...])

    outs = []
    s_next = scores(0)
    for h in range(n_heads):
        s_a, s_b = s_next
        if h + 1 < n_heads:
            s_next = scores(h + 1)
        e_a = jnp.exp2(s_a - jnp.max(s_a, axis=-1, keepdims=True))
        e_b = jnp.exp2(s_b - jnp.max(s_b, axis=-1, keepdims=True))
        l_a = jnp.sum(e_a, axis=-1, keepdims=True)
        l_b = jnp.sum(e_b, axis=-1, keepdims=True)
        pc = (e_a - (lam * l_a / l_b) * e_b).astype(BF16)
        outs.append(_dot(pc, v_refs[h][...]) / l_a)
    return outs


def _gqa_kernel(*refs, n_ctx, n_lat, rope_q):
    it = iter(refs)
    q_ref, gg_ref, kvc_ref = next(it), next(it), next(it)
    kvl_ref = next(it) if n_lat else None
    if rope_q:
        cosq_ref, sinq_ref = next(it), next(it)
    if n_lat:
        cosk_ref, sink_ref = next(it), next(it)
    qg_ref, kg_ref = next(it), next(it)
    y_ref = next(it)
    kt_s, vext_s = next(it), next(it)

    seg128 = _seg_ones(128, 64)

    @pl.when(pl.program_id(1) == 0)
    def _prep_kv():
        kc = _seg_rms(kvc_ref[:, 0:128], 64, seg128) * kg_ref[...]
        kt_s[:, 0:n_ctx] = kc.T.astype(BF16)
        vext_s[0:n_ctx, 0:128] = kvc_ref[:, 128:256].astype(BF16)
        if n_lat:
            kl = _seg_rms(kvl_ref[:, 0:128], 64, seg128) * kg_ref[...]
            kl = _rope(kl, cosk_ref[...], sink_ref[...], 32)
            kt_s[:, n_ctx:n_ctx + n_lat] = kl.T.astype(BF16)
            vext_s[n_ctx:n_ctx + n_lat, 0:128] = kvl_ref[:, 128:256].astype(BF16)
        vext_s[:, 128:256] = jnp.ones((n_ctx + n_lat, 128), BF16)

    seg256 = _seg_ones(256, 64)
    q = _seg_rms(q_ref[...], 64, seg256) * qg_ref[...]
    if rope_q:
        q = _rope(q, cosq_ref[...], sinq_ref[...], 32)
    q = q * (GQA_HEAD_DIM ** -0.5 * LOG2E)
    tq = q.shape[0]
    sub = min(ATTN_SUB, tq)
    lane = lax.broadcasted_iota(jnp.int32, (sub, 128), 1)
    lhs_list = []
    for r0 in range(0, tq, sub):
        for half in range(2):
            qh = q[r0:r0 + sub, 128 * half:128 * half + 128]
            for kv in range(2):
                in_kv = (lane >= 64 * kv) & (lane < 64 * kv + 64)
                lhs_list.append(jnp.where(in_kv, qh, 0.0).astype(BF16))
    outs = _attend_many(lhs_list, kt_s, [vext_s] * len(lhs_list))
    for j, r0 in enumerate(range(0, tq, sub)):
        for half in range(2):
            o = jnp.where(lane < 64, outs[4 * j + 2 * half], outs[4 * j + 2 * half + 1])
            gate = _silu(gg_ref[r0:r0 + sub, 128 * half:128 * half + 128])
            y_ref[r0:r0 + sub, 128 * half:128 * half + 128] = (o * gate).astype(BF16)


def _gqa_call(p_q, p_c, p_l, tabs, qg, kg, batch, n_ctx, n_lat, rope_q):
    t_total = p_q.shape[0] // batch
    tq = min(ATTN_TQ, t_total)
    nq = t_total // tq
    in_specs = [
        pl.BlockSpec((tq, 256), lambda b, i: (b * nq + i, 6)),
        pl.BlockSpec((tq, 256), lambda b, i: (b * nq + i, 8)),
        pl.BlockSpec((n_ctx, 256), lambda b, i: (b, 7)),
    ]
    args = [p_q, p_q, p_c]
    if n_lat:
        in_specs.append(pl.BlockSpec((n_lat, 256), lambda b, i: (b, 7)))
        args.append(p_l)
    if rope_q:
        in_specs += [pl.BlockSpec((tq, 256), lambda b, i: (i, 0))] * 2
        args += [tabs["cos_g"], tabs["sin_g"]]
    if n_lat:
        in_specs += [pl.BlockSpec((n_lat, 128), lambda b, i: (0, 0))] * 2
        args += [tabs["cos_g"], tabs["sin_g"]]
    in_specs += [pl.BlockSpec((1, 256), lambda b, i: (0, 0)),
                 pl.BlockSpec((1, 128), lambda b, i: (0, 0))]
    args += [qg, kg]
    s_keys = n_ctx + n_lat
    return pl.pallas_call(
        functools.partial(_gqa_kernel, n_ctx=n_ctx, n_lat=n_lat, rope_q=rope_q),
        grid=(batch, nq),
        in_specs=in_specs,
        out_specs=pl.BlockSpec((tq, 256), lambda b, i: (b * nq + i, 0)),
        out_shape=jax.ShapeDtypeStruct((p_q.shape[0], 256), BF16),
        scratch_shapes=[pltpu.VMEM((128, s_keys), BF16),
                        pltpu.VMEM((s_keys, 256), BF16)],
        compiler_params=pltpu.CompilerParams(
            dimension_semantics=("arbitrary", "arbitrary"),
            vmem_limit_bytes=VMEM_LIMIT),
        name="gqa_attn",
    )(*args)


def _diff_kernel(*refs, n_ctx, n_lat, rope_q, lam_init):
    it = iter(refs)
    q_ref, dg_ref, kc_ref, vc_ref = next(it), next(it), next(it), next(it)
    if n_lat:
        kl_ref, vl_ref = next(it), next(it)
    if rope_q:
        cosq_ref, sinq_ref = next(it), next(it)
    if n_lat:
        cosk_ref, sink_ref = next(it), next(it)
    lam_ref, ng_ref = next(it), next(it)
    y_ref = next(it)
    kt_s, vlo_s, vhi_s = next(it), next(it), next(it)
    s_keys = n_ctx + n_lat

    @pl.when(pl.program_id(1) == 0)
    def _prep_kv():
        kt_s[:, 0:n_ctx] = kc_ref[...].T.astype(BF16)
        vlo_s[0:n_ctx, 0:128] = vc_ref[:, 0:128].astype(BF16)
        vhi_s[0:n_ctx, 0:128] = vc_ref[:, 128:256].astype(BF16)
        if n_lat:
            kl = _rope(kl_ref[...], cosk_ref[...], sink_ref[...], 16)
            kt_s[:, n_ctx:s_keys] = kl.T.astype(BF16)
            vlo_s[n_ctx:s_keys, 0:128] = vl_ref[:, 0:128].astype(BF16)
            vhi_s[n_ctx:s_keys, 0:128] = vl_ref[:, 128:256].astype(BF16)

    lp = lam_ref[...]
    lam = (jnp.exp(jnp.sum(lp[0:1, :] * lp[1:2, :], axis=-1, keepdims=True))
           - jnp.exp(jnp.sum(lp[2:3, :] * lp[3:4, :], axis=-1, keepdims=True))
           + lam_init)

    q = q_ref[...]
    if rope_q:
        q = _rope(q, cosq_ref[...], sinq_ref[...], 16)
    q = q * (DIFF_QK_DIM ** -0.5 * LOG2E)
    tq = q.shape[0]
    sub = min(ATTN_SUB, tq)
    lane256 = lax.broadcasted_iota(jnp.int32, (sub, 256), 1)
    lane128 = lax.broadcasted_iota(jnp.int32, (sub, 128), 1)
    seg128 = _seg_ones(128, 64)
    lhs_list, v_list = [], []
    for r0 in range(0, tq, sub):
        for mp in range(2 * DIFF_HEADS):
            in_map = (lane256 >= 32 * mp) & (lane256 < 32 * mp + 32)
            lhs_list.append(jnp.where(in_map, q[r0:r0 + sub, :], 0.0).astype(BF16))
        v_list += [vlo_s, vlo_s, vhi_s, vhi_s]
    heads = _attend_diff_pairs(lhs_list, kt_s, v_list, lam)
    for j, r0 in enumerate(range(0, tq, sub)):
        for half in range(2):
            o = jnp.where(lane128 < 64, heads[4 * j + 2 * half], heads[4 * j + 2 * half + 1])
            n = _seg_rms(o, 64, seg128) * ng_ref[:, 128 * half:128 * half + 128]
            n = n * (1.0 - lam_init)
            gate = _silu(dg_ref[r0:r0 + sub, 128 * half:128 * half + 128])
            y_ref[r0:r0 + sub, 128 * half:128 * half + 128] = (n * gate).astype(BF16)


def _diff_call(p_q, p_c, p_l, tabs, lam_params, ng, batch, n_ctx, n_lat, rope_q,
               lam_init):
    t_total = p_q.shape[0] // batch
    tq = min(ATTN_TQ, t_total)
    nq = t_total // tq
    in_specs = [
        pl.BlockSpec((tq, 256), lambda b, i: (b * nq + i, 9)),
        pl.BlockSpec((tq, 256), lambda b, i: (b * nq + i, 12)),
        pl.BlockSpec((n_ctx, 256), lambda b, i: (b, 10)),
        pl.BlockSpec((n_ctx, 256), lambda b, i: (b, 11)),
    ]
    args = [p_q, p_q, p_c, p_c]
    if n_lat:
        in_specs += [pl.BlockSpec((n_lat, 256), lambda b, i: (b, 10)),
                     pl.BlockSpec((n_lat, 256), lambda b, i: (b, 11))]
        args += [p_l, p_l]
    if rope_q:
        in_specs += [pl.BlockSpec((tq, 256), lambda b, i: (i, 0))] * 2
        args += [tabs["cos_d"], tabs["sin_d"]]
    if n_lat:
        in_specs += [pl.BlockSpec((n_lat, 256), lambda b, i: (0, 0))] * 2
        args += [tabs["cos_d"], tabs["sin_d"]]
    in_specs += [pl.BlockSpec((4, DIFF_QK_DIM), lambda b, i: (0, 0)),
                 pl.BlockSpec((1, 256), lambda b, i: (0, 0))]
    args += [lam_params, ng]
    s_keys = n_ctx + n_lat
    return pl.pallas_call(
        functools.partial(_diff_kernel, n_ctx=n_ctx, n_lat=n_lat, rope_q=rope_q,
                          lam_init=lam_init),
        grid=(batch, nq),
        in_specs=in_specs,
        out_specs=pl.BlockSpec((tq, 256), lambda b, i: (b * nq + i, 0)),
        out_shape=jax.ShapeDtypeStruct((p_q.shape[0], 256), BF16),
        scratch_shapes=[pltpu.VMEM((256, s_keys), BF16),
                        pltpu.VMEM((s_keys, 128), BF16),
                        pltpu.VMEM((s_keys, 128), BF16)],
        compiler_params=pltpu.CompilerParams(
            dimension_semantics=("arbitrary", "arbitrary"),
            vmem_limit_bytes=VMEM_LIMIT),
        name="diff_attn",
    )(*args)


def _ssd_kernel(xc_ref, xl_ref, bc_ref, bl_ref, cc_ref, cl_ref, zc_ref, zl_ref,
                dtc_ref, dtl_ref, cwx_ref, cwb_ref, cwc_ref, cbx_ref, cbb_ref,
                cbc_ref, par_ref, dsk_ref, ng_ref, *rest, n_ctx, n_lat, ctx_out):
    if ctx_out:
        oc_ref, ol_ref = rest[0], rest[1]
        rest = rest[2:]
    else:
        oc_ref, ol_ref = None, rest[0]
        rest = rest[1:]
    xp_s, dtv_s, yacc_s, sloc_s, e_s, cbf_s, sin_s, dec_s, vun_s = rest

    g = pl.program_id(1)
    ncc = n_ctx // CHUNK
    ncl = n_lat // CHUNK
    nch = ncc + ncl
    GROUP = 3 if nch % 3 == 0 else 2
    assert nch % GROUP == 0
    t_all = n_ctx + n_lat
    lat0 = n_ctx + 2 * HALO

    zeros_h = jnp.zeros((HALO, 512), F32)
    xp_s[0:HALO, :] = zeros_h
    xp_s[HALO:HALO + n_ctx, 0:256] = xc_ref[...]
    xp_s[HALO:HALO + n_ctx, 256:384] = bc_ref[...]
    xp_s[HALO:HALO + n_ctx, 384:512] = cc_ref[...]
    xp_s[HALO + n_ctx:lat0, :] = zeros_h
    xp_s[lat0:lat0 + n_lat, 0:256] = xl_ref[...]
    xp_s[lat0:lat0 + n_lat, 256:384] = bl_ref[...]
    xp_s[lat0:lat0 + n_lat, 384:512] = cl_ref[...]
    xp_s[lat0 + n_lat:lat0 + n_lat + HALO, :] = zeros_h

    bias = par_ref[1:2, :]
    a_col = -jnp.exp(par_ref[8:24, :])

    def _softplus(v):
        return jnp.maximum(v, 0.0) + jnp.log1p(jnp.exp(-jnp.abs(v)))

    dtv_s[0:n_ctx, :] = _softplus(dtc_ref[...] + bias)
    dtv_s[n_ctx:t_all, :] = _softplus(dtl_ref[...] + bias)

    r128 = lax.broadcasted_iota(jnp.int32, (CHUNK, CHUNK), 0)
    c128 = lax.broadcasted_iota(jnp.int32, (CHUNK, CHUNK), 1)
    lower = c128 <= r128
    upper = c128 >= r128
    tril = jnp.where(lower, 1.0, 0.0).astype(BF16)
    triu = jnp.where(upper, 1.0, 0.0).astype(BF16)
    fwd_row = lax.broadcasted_iota(jnp.int32, (16, CHUNK), 0) < 8
    fwd_row1 = lax.broadcasted_iota(jnp.int32, (16, 1), 0) < 8
    er = lax.broadcasted_iota(jnp.int32, (CHUNK, 1024), 0)
    ec = lax.broadcasted_iota(jnp.int32, (CHUNK, 1024), 1)
    src_lane = 16 + 16 * (ec >> 9) + 8 * ((ec >> 8) & 1) + ((ec >> 6) & 3)
    expand = jnp.where(er == src_lane, 1.0, 0.0).astype(BF16)
    lane256 = lax.broadcasted_iota(jnp.int32, (CHUNK, 256), 1)
    col512 = lax.broadcasted_iota(jnp.int32, (1, 512), 1)

    cw = jnp.concatenate([cwx_ref[...], cwb_ref[...], cwc_ref[...]], axis=1)
    cb = jnp.concatenate([cbx_ref[...], cbb_ref[...], cbc_ref[...]], axis=1)
    dsk = dsk_ref[...]

    def phase_a(grp, carry):
        cs = [grp * GROUP + j for j in range(GROUP)]
        r0s = [pl.multiple_of(c * CHUNK, CHUNK) for c in cs]

        def conv(c):
            wstart = pl.multiple_of(c * CHUNK + jnp.where(c >= ncc, HALO, 0), 8)
            win = xp_s[pl.ds(wstart, CHUNK + 2 * HALO), :]
            acc = jnp.broadcast_to(cb, (CHUNK, 512))
            for k in range(CONV_K):
                d = k - CONV_K // 2
                if d == 0:
                    tap = win[HALO:HALO + CHUNK, :]
                else:
                    tap = pltpu.roll(win, (-d) % (CHUNK + 2 * HALO), 0)[HALO:HALO + CHUNK, :]
                acc = acc + cw[k:k + 1, :] * tap
            return _silu(acc)

        dtts = [dtv_s[pl.ds(r0, CHUNK), :].T[0:16, :] for r0 in r0s]
        a_ts = [dtt * a_col for dtt in dtts]
        acol_ts = [jnp.where(fwd_row, _dot_exact_lhs(a_t, triu), _dot_exact_lhs(a_t, tril))
                   for a_t in a_ts]
        us = [conv(c) for c in cs]
        xus = [u[:, 0:256] for u in us]
        bts = [u[:, 256:384].T.astype(BF16) for u in us]
        cbfs = [u[:, 384:512].astype(BF16) for u in us]
        xbfs = [xu.astype(BF16) for xu in xus]
        gmats = [_dot(cbf, bt) for cbf, bt in zip(cbfs, bts)]
        tms = []
        for dtt, acol_t in zip(dtts, acol_ts):
            tot = jnp.where(fwd_row1, acol_t[:, CHUNK - 1:CHUNK], acol_t[:, 0:1])
            w_t = dtt * jnp.exp(tot - acol_t)
            e_t = jnp.exp(acol_t)
            stacked = jnp.concatenate(
                [acol_t, w_t, e_t, jnp.zeros((CHUNK - 48, CHUNK), F32)], axis=0)
            tms.append(stacked.T)
        wes = []
        for tm in tms:
            tm_hi = tm.astype(BF16)
            tm_lo = (tm - tm_hi.astype(F32)).astype(BF16)
            wes.append(_dot(tm_hi, expand) + _dot(tm_lo, expand))
        ydiags = [jnp.zeros((CHUNK, 256), F32) for _ in cs]
        for i in range(4):
            in_head = (lane256 >= 64 * i) & (lane256 < 64 * i + 64)
            for j in range(GROUP):
                tm, acol_t, dtt = tms[j], acol_ts[j], dtts[j]
                arg = jnp.where(lower, tm[:, i:i + 1] - acol_t[i:i + 1, :],
                                tm[:, 8 + i:9 + i] - acol_t[8 + i:9 + i, :])
                scale = (jnp.where(lower, dtt[i:i + 1, :], 0.0)
                         + jnp.where(upper, dtt[8 + i:9 + i, :], 0.0))
                wmat = (gmats[j] * jnp.exp(arg) * scale).astype(BF16)
                ydiags[j] = jnp.where(in_head, _dot(wmat, xbfs[j]), ydiags[j])
        for j, c in enumerate(cs):
            w512 = wes[j][:, 0:512]
            e512 = wes[j][:, 512:1024]
            xdw = (jnp.concatenate([xus[j], xus[j]], axis=1) * w512).astype(BF16)
            sloc_s[c] = _dot(bts[j], xdw)
            dec512 = jnp.where(col512 < 256, e512[CHUNK - 1:CHUNK, :], e512[0:1, :])
            dec_s[c] = jnp.broadcast_to(dec512, (8, 512))
            e_s[c] = e512
            cbf_s[c] = cbfs[j]
            yacc_s[pl.ds(r0s[j], CHUNK), :] = ydiags[j] + xus[j] * dsk
        return carry

    lax.fori_loop(0, nch // GROUP, phase_a, 0)

    fwd_order = list(range(nch))
    bwd_order = list(range(ncc - 1, -1, -1)) + list(range(nch - 1, ncc - 1, -1))
    for order, lo in ((fwd_order, 0), (bwd_order, 256)):
        state = jnp.zeros((SSD_STATE, 256), F32)
        for c in order:
            sin_s[c, :, lo:lo + 256] = state.astype(BF16)
            state = (state * dec_s[c, 0:1, lo:lo + 256]
                     + sloc_s[c, :, lo:lo + 256])

    def phase_c(c, z_ref, zrow):
        yo = _dot(cbf_s[c], sin_s[c]) * e_s[c]
        r0 = c * CHUNK if isinstance(c, int) else pl.multiple_of(c * CHUNK, CHUNK)
        y = yacc_s[pl.ds(r0, CHUNK), :] + yo[:, 0:256] + yo[:, 256:512]
        v = y * _silu(z_ref[pl.ds(zrow, CHUNK), :])

        @pl.when(g == 0)
        def _():
            vun_s[pl.ds(r0, CHUNK), 0:256] = v

        @pl.when(g == 1)
        def _():
            vun_s[pl.ds(r0, CHUNK), 256:512] = v

    if ctx_out:
        for c in range(ncc):
            phase_c(c, zc_ref, c * CHUNK)

    def phase_c_lat(k, carry):
        phase_c(k + ncc, zl_ref, pl.multiple_of(k * CHUNK, CHUNK))
        return carry

    lax.fori_loop(0, ncl, phase_c_lat, 0)

    @pl.when(g == SSD_GROUPS - 1)
    def _finalize():
        ng = ng_ref[...]

        def norm_rows(r0, nrows):
            v = vun_s[pl.ds(r0, nrows), :]
            ms = jnp.mean(v * v, axis=-1, keepdims=True)
            return (v * lax.rsqrt(ms + EPS) * ng).astype(BF16)

        if ctx_out:
            oc_ref[...] = norm_rows(0, n_ctx)

        def fin(k, carry):
            r0 = pl.multiple_of(k * 256, 256)
            ol_ref[pl.ds(r0, 256), :] = norm_rows(n_ctx + r0, 256)
            return carry

        lax.fori_loop(0, n_lat // 256, fin, 0)


def _ssd_call(p_c, p_l, conv_w8, conv_b, par, dsk, ng, batch, n_ctx, n_lat, ctx_out):
    nch = (n_ctx + n_lat) // CHUNK
    t_all = n_ctx + n_lat
    in_specs = [
        pl.BlockSpec((n_ctx, 256), lambda b, g: (b, g)),
        pl.BlockSpec((n_lat, 256), lambda b, g: (b, g)),
        pl.BlockSpec((n_ctx, 128), lambda b, g: (b, 4 + g)),
        pl.BlockSpec((n_lat, 128), lambda b, g: (b, 4 + g)),
        pl.BlockSpec((n_ctx, 128), lambda b, g: (b, 6 + g)),
        pl.BlockSpec((n_lat, 128), lambda b, g: (b, 6 + g)),
        pl.BlockSpec((n_ctx, 256), lambda b, g: (b, 4 + g)),
        pl.BlockSpec((n_lat, 256), lambda b, g: (b, 4 + g)),
        pl.BlockSpec((n_ctx, 128), lambda b, g: (b, 26 + g)),
        pl.BlockSpec((n_lat, 128), lambda b, g: (b, 26 + g)),
        pl.BlockSpec((8, 256), lambda b, g: (0, g)),
        pl.BlockSpec((8, 128), lambda b, g: (0, 4 + g)),
        pl.BlockSpec((8, 128), lambda b, g: (0, 6 + g)),
        pl.BlockSpec((1, 256), lambda b, g: (0, g)),
        pl.BlockSpec((1, 128), lambda b, g: (0, 4 + g)),
        pl.BlockSpec((1, 128), lambda b, g: (0, 6 + g)),
        pl.BlockSpec((None, 24, 128), lambda b, g: (g, 0, 0)),
        pl.BlockSpec((1, 256), lambda b, g: (0, g)),
        pl.BlockSpec((1, 512), lambda b, g: (0, 0)),
    ]
    args = [p_c, p_l, p_c, p_l, p_c, p_l, p_c, p_l, p_c, p_l,
            conv_w8, conv_w8, conv_w8, conv_b, conv_b, conv_b, par, dsk, ng]
    out_specs = [pl.BlockSpec((n_lat, 512), lambda b, g: (b, 0))]
    out_shape = [jax.ShapeDtypeStruct((batch * n_lat, 512), BF16)]
    if ctx_out:
        out_specs = [pl.BlockSpec((n_ctx, 512), lambda b, g: (b, 0))] + out_specs
        out_shape = [jax.ShapeDtypeStruct((batch * n_ctx, 512), BF16)] + out_shape
    scratch = [
        pltpu.VMEM((t_all + 3 * HALO, 512), F32),
        pltpu.VMEM((t_all, 128), F32),
        pltpu.VMEM((t_all, 256), F32),
        pltpu.VMEM((nch, SSD_STATE, 512), F32),
        pltpu.VMEM((nch, CHUNK, 512), F32),
        pltpu.VMEM((nch, CHUNK, 128), BF16),
        pltpu.VMEM((nch, SSD_STATE, 512), BF16),
        pltpu.VMEM((nch, 8, 512), F32),
        pltpu.VMEM((t_all, 512), F32),
    ]
    outs = pl.pallas_call(
        functools.partial(_ssd_kernel, n_ctx=n_ctx, n_lat=n_lat, ctx_out=ctx_out),
        grid=(batch, SSD_GROUPS),
        in_specs=in_specs,
        out_specs=out_specs,
        out_shape=out_shape,
        scratch_shapes=scratch,
        compiler_params=pltpu.CompilerParams(
            dimension_semantics=("arbitrary", "arbitrary"),
            vmem_limit_bytes=VMEM_LIMIT),
        name="ssd_scan",
    )(*args)
    if ctx_out:
        return outs[1], outs[0]
    return outs[0], None


def _rope_tables(n_lat):
    rows = n_lat // GRID_W
    row_idx = jnp.repeat(jnp.arange(rows), GRID_W).astype(F32)
    col_idx = (jnp.arange(rows * GRID_W) % GRID_W).astype(F32)

    def tables(dim, reps):
        quarter = dim // 4
        inv = ROPE_BASE ** (-jnp.arange(quarter, dtype=F32) / quarter)
        ang = jnp.concatenate([row_idx[:, None] * inv, col_idx[:, None] * inv], axis=-1)
        cos, sin = jnp.cos(ang), jnp.sin(ang)
        cos2 = jnp.concatenate([cos, cos], axis=-1)
        sin2 = jnp.concatenate([-sin, sin], axis=-1)
        return jnp.tile(cos2, (1, reps)), jnp.tile(sin2, (1, reps))

    cos_g, sin_g = tables(GQA_HEAD_DIM, 4)
    cos_d, sin_d = tables(DIFF_QK_DIM, 8)
    return {"cos_g": cos_g, "sin_g": sin_g, "cos_d": cos_d, "sin_d": sin_d}


def kernel(x, c, ctx, c_ctx, w_mod, b_mod, g_pre, g_post, w_in, conv_w, conv_b,
           a_log_fwd, a_log_bwd, dt_bias_fwd, dt_bias_bwd, d_skip, ssd_norm_g,
           q_norm_g, k_norm_g, diff_lambda, diff_norm_g, w_out):
    batch, n_lat, _ = x.shape
    n_ctx = ctx.shape[1]
    depth = w_mod.shape[0]
    assert n_lat % 512 == 0 and n_ctx % 256 == 0 and (batch * n_ctx) % 512 == 0
    assert batch + 1 <= 16

    w_in_p = _take_runs(w_in.astype(BF16), _in_col_perm(), 2, IN_COLS)
    w_out_p = _take_runs(w_out.astype(BF16), _out_row_perm(), 1, None)
    conv_w8 = jnp.pad(conv_w, ((0, 0), (0, 8 - CONV_K), (0, 0)))
    conv_b1 = conv_b[:, None, :]

    def group16(fwd, bwd):
        out = jnp.zeros((depth, SSD_GROUPS, 16), F32)
        for g in range(SSD_GROUPS):
            out = out.at[:, g, 0:4].set(fwd[:, 4 * g:4 * g + 4])
            out = out.at[:, g, 8:12].set(bwd[:, 4 * g:4 * g + 4])
        return out

    ssd_par = jnp.zeros((depth, SSD_GROUPS, 24, 128), F32)
    ssd_par = ssd_par.at[:, :, 1, 0:16].set(group16(dt_bias_fwd, dt_bias_bwd))
    ssd_par = ssd_par.at[:, :, 8:24, :].set(
        jnp.broadcast_to(group16(a_log_fwd, a_log_bwd)[..., None],
                         (depth, SSD_GROUPS, 16, 128)))
    dsk = jnp.repeat(d_skip, SSD_HEAD_DIM, axis=1)[:, None, :]
    qg = jnp.tile(q_norm_g, (1, 4))[:, None, :]
    kg = jnp.tile(k_norm_g, (1, 2))[:, None, :]
    dng = jnp.tile(diff_norm_g, (1, 4))[:, None, :]
    tabs = _rope_tables(n_lat)

    cs = jnp.concatenate(
        [c, c_ctx[None, :], jnp.zeros((16 - batch - 1, D_MODEL), F32)], axis=0)
    mod_all = _mod_call(cs, w_mod, b_mod)

    h = x.reshape(batch * n_lat, D_MODEL)
    hc = ctx.reshape(batch * n_ctx, D_MODEL)
    for l in range(depth):
        ctx_out = l < depth - 1
        lam_init = 0.8 - 0.6 * float(np.exp(-0.3 * l))
        mod3 = mod_all[l][:, None, :]
        p_l = _inproj_call(h, mod3, g_pre[l][None, :], w_in_p[l], n_lat // 512, None)
        p_c = _inproj_call(hc, mod3, g_pre[l][None, :], w_in_p[l], None, batch)

        ys_l, ys_c = _ssd_call(p_c, p_l, conv_w8[l], conv_b1[l], ssd_par[l], dsk[l],
                               ssd_norm_g[l][None, :], batch, n_ctx, n_lat, ctx_out)
        yg_l = _gqa_call(p_l, p_c, p_l, tabs, qg[l], kg[l], batch, n_ctx, n_lat, True)
        yd_l = _diff_call(p_l, p_c, p_l, tabs, diff_lambda[l], dng[l], batch, n_ctx,
                          n_lat, True, lam_init)
        h = _outproj_call(ys_l, yg_l, yd_l, h, mod3, g_post[l][None, :], w_out_p[l],
                          n_lat // 512, None)
        if ctx_out:
            yg_c = _gqa_call(p_c, p_c, None, tabs, qg[l], kg[l], batch, n_ctx, 0, False)
            yd_c = _diff_call(p_c, p_c, None, tabs, diff_lambda[l], dng[l], batch,
                              n_ctx, 0, False, lam_init)
            hc = _outproj_call(ys_c, yg_c, yd_c, hc, mod3, g_post[l][None, :],
                               w_out_p[l], None, batch)
    return h.reshape(batch, n_lat, D_MODEL)
```

```python
import functools
import math

import numpy as np
import jax
import jax.numpy as jnp
from jax import lax
from jax.experimental import pallas as pl
from jax.experimental.pallas import tpu as pltpu

F32 = jnp.float32
BF16 = jnp.bfloat16

D_MODEL = 1024
GRID_W = 64
ROPE_BASE = 10000.0
EPS = 1e-6
LOG2E = 1.4426950408889634

SSD_WIDTH = 512
SSD_HEADS = 8
SSD_HEAD_DIM = 64
SSD_GROUPS = 2
SSD_STATE = 128
CHUNK = 128
CONV_K = 5
HALO = 8
GQA_HEADS = 4
GQA_HEAD_DIM = 64
DIFF_HEADS = 4
DIFF_QK_DIM = 32
DIFF_V_DIM = 64

_IN_SPLITS = (("xbc", 1024), ("z", 512), ("dt", 16), ("gq", 256), ("gk", 128),
              ("gv", 128), ("gg", 256), ("dq", 256), ("dk", 256), ("dv", 256),
              ("dg", 256))
IN_COLS = sum(s for _, s in _IN_SPLITS)
NP = 28 * 128
XBC_COLS = 1024
COL_Z, COL_GQ, COL_GK, COL_GG = 0, 4, 6, 8
COL_DQ, COL_DK, COL_DV, COL_DG, COL_DT = 10, 12, 14, 16, 18
GQ_HEAD_ORDER = (0, 2, 1, 3)

VMEM_LIMIT = 56 * 1024 * 1024
INPROJ_TM = 512
ATTN_TQ = 1024
ATTN_SUB = 512


def _in_col_perm():
    off, o = {}, 0
    for name, size in _IN_SPLITS:
        off[name] = o
        o += size
    pad = IN_COLS
    cols = list(range(off["xbc"], off["xbc"] + 1024))
    cols += list(range(off["z"], off["z"] + 512))
    for h in GQ_HEAD_ORDER:
        cols += list(range(off["gq"] + 64 * h, off["gq"] + 64 * h + 64))
    cols += list(range(off["gk"], off["gk"] + 128))
    cols += list(range(off["gv"], off["gv"] + 128))
    for h in GQ_HEAD_ORDER:
        cols += list(range(off["gg"] + 64 * h, off["gg"] + 64 * h + 64))
    for name in ("dq", "dk", "dv", "dg"):
        cols += list(range(off[name], off[name] + 256))
    for g in range(SSD_GROUPS):
        blk = [pad] * 128
        for i in range(4):
            blk[i] = off["dt"] + 4 * g + i
            blk[8 + i] = off["dt"] + SSD_HEADS + 4 * g + i
        cols += blk
    assert len(cols) == NP
    return np.asarray(cols, np.int32)


def _out_row_perm():
    rows = list(range(SSD_WIDTH))
    for h in GQ_HEAD_ORDER:
        rows += list(range(SSD_WIDTH + 64 * h, SSD_WIDTH + 64 * h + 64))
    rows += list(range(SSD_WIDTH + 256, SSD_WIDTH + 512))
    return np.asarray(rows, np.int32)


def _take_runs(arr, idx, axis, pad_index):
    pieces, start = [], 0
    idx = [int(i) for i in idx]
    while start < len(idx):
        end = start + 1
        if idx[start] == pad_index:
            while end < len(idx) and idx[end] == pad_index:
                end += 1
            shape = list(arr.shape)
            shape[axis] = end - start
            pieces.append(jnp.zeros(shape, arr.dtype))
        else:
            while end < len(idx) and idx[end] == idx[end - 1] + 1 and idx[end] != pad_index:
                end += 1
            pieces.append(lax.slice_in_dim(arr, idx[start], idx[end - 1] + 1, axis=axis))
        start = end
    return jnp.concatenate(pieces, axis=axis)


def _dot(a, b):
    return jnp.dot(a, b, preferred_element_type=F32)


def _split3(x):
    hi = x.astype(BF16)
    r1 = x - hi.astype(F32)
    mid = r1.astype(BF16)
    lo = (r1 - mid.astype(F32)).astype(BF16)
    return hi, mid, lo


def _dot_exact_lhs(x, m_bf16):
    hi, mid, lo = _split3(x)
    return _dot(hi, m_bf16) + _dot(mid, m_bf16) + _dot(lo, m_bf16)


def _dot_exact_rhs(m_bf16, x):
    hi, mid, lo = _split3(x)
    return _dot(m_bf16, hi) + _dot(m_bf16, mid) + _dot(m_bf16, lo)


def _silu(x):
    return x * jax.nn.sigmoid(x)


def _seg_ones(width, seg):
    r = lax.broadcasted_iota(jnp.int32, (width, width), 0)
    c = lax.broadcasted_iota(jnp.int32, (width, width), 1)
    same = (r & ~(seg - 1)) == (c & ~(seg - 1))
    return jnp.where(same, 1.0, 0.0).astype(BF16)


def _seg_rms(x, seg, seg_mat):
    ss = _dot_exact_lhs(x * x, seg_mat)
    return x * lax.rsqrt(ss * (1.0 / seg) + EPS)


def _rope(x, cos, sin_signed, half):
    w = x.shape[-1]
    lane = lax.broadcasted_iota(jnp.int32, x.shape, 1)
    first = (lane & (2 * half - 1)) < half
    swapped = jnp.where(first, pltpu.roll(x, w - half, 1), pltpu.roll(x, half, 1))
    return x * cos + swapped * sin_signed


def _mod_kernel(cs_ref, w_ref, b_ref, o_ref):
    s = _silu(cs_ref[...]).astype(BF16)
    o_ref[...] = _dot(s, w_ref[...].astype(BF16)) + b_ref[...]


def _mod_call(cs, w_mod, b_mod):
    depth = w_mod.shape[0]
    nrow = cs.shape[0]
    tn = 1024
    return pl.pallas_call(
        _mod_kernel,
        grid=(depth, 3 * D_MODEL // tn),
        in_specs=[
            pl.BlockSpec((nrow, D_MODEL), lambda l, j: (0, 0)),
            pl.BlockSpec((None, D_MODEL, tn), lambda l, j: (l, 0, j)),
            pl.BlockSpec((None, 1, tn), lambda l, j: (l, 0, j)),
        ],
        out_specs=pl.BlockSpec((None, nrow, tn), lambda l, j: (l, 0, j)),
        out_shape=jax.ShapeDtypeStruct((depth, nrow, 3 * D_MODEL), F32),
        compiler_params=pltpu.CompilerParams(
            dimension_semantics=("arbitrary", "arbitrary")),
        name="mod_proj",
    )(cs, w_mod, b_mod.reshape(depth, 1, 3 * D_MODEL))


def _inproj_kernel(h0_ref, mod0_ref, hn_ref, modn_ref, g_ref, w_ref, cw_ref, cb_ref,
                   rest_ref, xu_ref, cu_ref, bt_ref, u_s, ring_s, tail_s,
                   *, n_tiles, tiles_per_seg):
    i = pl.program_id(0)
    tm = hn_ref.shape[0]
    tn = 512
    nq = tm // CHUNK
    pos = lax.rem(jnp.maximum(i - 1, 0), tiles_per_seg)
    has_left = pos != 0
    has_right = pos != tiles_per_seg - 1

    def modulate_rows(h_ref, mod_ref, dst_slot, r0, rows):
        x = h_ref[r0:r0 + rows, :]
        ms = jnp.mean(x * x, axis=-1, keepdims=True)
        y = x * lax.rsqrt(ms + EPS) * g_ref[...]
        sh = mod_ref[:, 0:D_MODEL]
        sc = mod_ref[:, D_MODEL:2 * D_MODEL]
        u_s[dst_slot, r0:r0 + rows, :] = (y * (1.0 + sc) + sh).astype(BF16)

    @pl.when(i == 0)
    def _init():
        modulate_rows(h0_ref, mod0_ref, 0, 0, tm)
        ring_s[...] = jnp.zeros(ring_s.shape, F32)
        tail_s[...] = jnp.zeros(tail_s.shape, F32)

    def project(slot, j):
        acc = _dot(u_s[slot], w_ref[:, j * tn:(j + 1) * tn])
        if j < XBC_COLS // tn:
            ring_s[slot, :, j * tn:(j + 1) * tn] = acc
        else:
            rest_ref[:, j * tn - XBC_COLS:(j + 1) * tn - XBC_COLS] = acc

    def conv(slot, cblk):
        pslot = 1 - slot
        cols = slice(cblk * 128, (cblk + 1) * 128)
        for q in range(nq):
            r0 = q * CHUNK
            if q == 0:
                left = jnp.where(has_left, tail_s[:, cols], 0.0)
            else:
                left = ring_s[pslot, r0 - HALO:r0, cols]
            if q == nq - 1:
                right = jnp.where(has_right, ring_s[slot, 0:HALO, cols], 0.0)
            else:
                right = ring_s[pslot, r0 + CHUNK:r0 + CHUNK + HALO, cols]
            win = jnp.concatenate([left, ring_s[pslot, r0:r0 + CHUNK, cols], right], axis=0)
            acc = jnp.broadcast_to(cb_ref[:, cols], (CHUNK, 128))
            for k in range(CONV_K):
                d = k - CONV_K // 2
                if d == 0:
                    tap = win[HALO:HALO + CHUNK, :]
                else:
                    tap = pltpu.roll(win, (-d) % (CHUNK + 2 * HALO), 0)[HALO:HALO + CHUNK, :]
                acc = acc + cw_ref[k:k + 1, cols] * tap
            v = _silu(acc)
            if cblk < 4:
                xu_ref[r0:r0 + CHUNK, cols] = v
            elif cblk < 6:
                g0 = (cblk - 4) * SSD_STATE
                bt_ref[q, g0:g0 + SSD_STATE, :] = v.T.astype(BF16)
            else:
                cu_ref[r0:r0 + CHUNK, (cblk - 6) * 128:(cblk - 5) * 128] = v.astype(BF16)

    conv_after = {1: (0, 1), 2: (2, 3), 3: (4,), 4: (5,), 5: (6, 7)}
    rows_pp = tm // 4

    def step(slot):
        for j in range(NP // tn):
            project(slot, j)
            if 2 <= j < 6:
                modulate_rows(hn_ref, modn_ref, 1 - slot, (j - 2) * rows_pp, rows_pp)
            for cblk in conv_after.get(j, ()):
                conv(slot, cblk)
        tail_s[...] = ring_s[1 - slot, tm - HALO:tm, :]

    parity = lax.rem(i, 2)
    for slot in range(2):
        pl.when(parity == slot)(functools.partial(step, slot))


def _inproj_call(h, mod3, g_pre, w_bf16, conv_w8, conv_b, tm, tiles_per_seg, tiles_per_row,
                 fixed_row):
    n_tok = h.shape[0]
    n_tiles = n_tok // tm
    assert n_tiles % tiles_per_seg == 0
    cur = lambda i: jnp.minimum(i, n_tiles - 1)
    nxt = lambda i: jnp.minimum(i + 1, n_tiles - 1)
    prev = lambda i: jnp.maximum(i - 1, 0)
    if fixed_row is None:
        mod_row = lambda t: t // tiles_per_row
    else:
        mod_row = lambda t: fixed_row
    return pl.pallas_call(
        functools.partial(_inproj_kernel, n_tiles=n_tiles, tiles_per_seg=tiles_per_seg),
        grid=(n_tiles + 1,),
        in_specs=[
            pl.BlockSpec((tm, D_MODEL), lambda i: (0, 0)),
            pl.BlockSpec((None, 1, 3 * D_MODEL), lambda i: (mod_row(0), 0, 0)),
            pl.BlockSpec((tm, D_MODEL), lambda i: (nxt(i), 0)),
            pl.BlockSpec((None, 1, 3 * D_MODEL), lambda i: (mod_row(nxt(i)), 0, 0)),
            pl.BlockSpec((1, D_MODEL), lambda i: (0, 0)),
            pl.BlockSpec((D_MODEL, NP), lambda i: (0, 0)),
            pl.BlockSpec((8, XBC_COLS), lambda i: (0, 0)),
            pl.BlockSpec((1, XBC_COLS), lambda i: (0, 0)),
        ],
        out_specs=[
            pl.BlockSpec((tm, NP - XBC_COLS), lambda i: (cur(i), 0)),
            pl.BlockSpec((tm, SSD_WIDTH), lambda i: (prev(i), 0)),
            pl.BlockSpec((tm, SSD_GROUPS * SSD_STATE), lambda i: (prev(i), 0)),
            pl.BlockSpec((tm // CHUNK, SSD_GROUPS * SSD_STATE, CHUNK),
                         lambda i: (prev(i), 0, 0)),
        ],
        out_shape=[
            jax.ShapeDtypeStruct((n_tok, NP - XBC_COLS), F32),
            jax.ShapeDtypeStruct((n_tok, SSD_WIDTH), F32),
            jax.ShapeDtypeStruct((n_tok, SSD_GROUPS * SSD_STATE), BF16),
            jax.ShapeDtypeStruct((n_tok // CHUNK, SSD_GROUPS * SSD_STATE, CHUNK), BF16),
        ],
        scratch_shapes=[pltpu.VMEM((2, tm, D_MODEL), BF16),
                        pltpu.VMEM((2, tm, XBC_COLS), F32),
                        pltpu.VMEM((HALO, XBC_COLS), F32)],
        compiler_params=pltpu.CompilerParams(
            dimension_semantics=("arbitrary",), vmem_limit_bytes=VMEM_LIMIT),
        name="in_proj",
    )(h, mod3, h, mod3, g_pre, w_bf16, conv_w8, conv_b)


def _outproj_kernel(ys_ref, yg_ref, yd_ref, h_ref, mod_ref, g_ref, w_ref, o_ref):
    o = (_dot(ys_ref[...], w_ref[0:512, :]) + _dot(yg_ref[...], w_ref[512:768, :])
         + _dot(yd_ref[...], w_ref[768:1024, :]))
    ms = jnp.mean(o * o, axis=-1, keepdims=True)
    n = o * lax.rsqrt(ms + EPS) * g_ref[...]
    gt = mod_ref[:, 2 * D_MODEL:3 * D_MODEL]
    o_ref[...] = h_ref[...] + gt * n


def _outproj_call(ys, yg, yd, h, mod3, g_post, w_bf16, tiles_per_row, fixed_row):
    n_tok = h.shape[0]
    tm = 512
    if fixed_row is None:
        mod_idx = lambda i: (i // tiles_per_row, 0, 0)
    else:
        mod_idx = lambda i: (fixed_row, 0, 0)
    return pl.pallas_call(
        _outproj_kernel,
        grid=(n_tok // tm,),
        in_specs=[
            pl.BlockSpec((tm, 512), lambda i: (i, 0)),
            pl.BlockSpec((tm, 256), lambda i: (i, 0)),
            pl.BlockSpec((tm, 256), lambda i: (i, 0)),
            pl.BlockSpec((tm, D_MODEL), lambda i: (i, 0)),
            pl.BlockSpec((None, 1, 3 * D_MODEL), mod_idx),
            pl.BlockSpec((1, D_MODEL), lambda i: (0, 0)),
            pl.BlockSpec((D_MODEL, D_MODEL), lambda i: (0, 0)),
        ],
        out_specs=pl.BlockSpec((tm, D_MODEL), lambda i: (i, 0)),
        out_shape=jax.ShapeDtypeStruct((n_tok, D_MODEL), F32),
        compiler_params=pltpu.CompilerParams(
            dimension_semantics=("arbitrary",), vmem_limit_bytes=VMEM_LIMIT),
        name="out_proj",
    )(ys, yg, yd, h, mod3, g_post, w_bf16)


def _attend_many(lhs_list, kt_ref, vext_refs):
    outs = []
    s_next = _dot(lhs_list[0], kt_ref[...])
    for i, vext_ref in enumerate(vext_refs):
        s = s_next
        if i + 1 < len(lhs_list):
            s_next = _dot(lhs_list[i + 1], kt_ref[...])
        m = jnp.max(s, axis=-1, keepdims=True)
        p = jnp.exp2(s - m).astype(BF16)
        oe = _dot(p, vext_ref[...])
        outs.append(oe[:, 0:128] / oe[:, 128:256])
    return outs


def _attend_diff_pairs(lhs_list, kt_ref, v_refs, lam):
    n_heads = len(v_refs)

    def scores(h):
        return _dot(lhs_list[2 * h], kt_ref[...]), _dot(lhs_list[2 * h + 1], kt_ref[...])

    outs = []
    s_next = scores(0)
    for h in range(n_heads):
        s_a, s_b = s_next
        if h + 1 < n_heads:
            s_next = scores(h + 1)
        e_a = jnp.exp2(s_a - jnp.max(s_a, axis=-1, keepdims=True))
        e_b = jnp.exp2(s_b - jnp.max(s_b, axis=-1, keepdims=True))
        l_a = jnp.sum(e_a, axis=-1, keepdims=True)
        l_b = jnp.sum(e_b, axis=-1, keepdims=True)
        pc = (e_a - (lam * l_a / l_b) * e_b).astype(BF16)
        outs.append(_dot(pc, v_refs[h][...]) / l_a)
    return outs


def _gqa_kernel(*refs, n_ctx, n_lat, rope_q):
    it = iter(refs)
    q_ref, gg_ref, kvc_ref = next(it), next(it), next(it)
    kvl_ref = next(it) if n_lat else None
    if rope_q:
        cosq_ref, sinq_ref = next(it), next(it)
    if n_lat:
        cosk_ref, sink_ref = next(it), next(it)
    qg_ref, kg_ref = next(it), next(it)
    y_ref = next(it)
    kt_s, vext_s = next(it), next(it)

    seg128 = _seg_ones(128, 64)

    @pl.when(pl.program_id(1) == 0)
    def _prep_kv():
        kc = _seg_rms(kvc_ref[:, 0:128], 64, seg128) * kg_ref[...]
        kt_s[:, 0:n_ctx] = kc.T.astype(BF16)
        vext_s[0:n_ctx, 0:128] = kvc_ref[:, 128:256].astype(BF16)
        if n_lat:
            kl = _seg_rms(kvl_ref[:, 0:128], 64, seg128) * kg_ref[...]
            kl = _rope(kl, cosk_ref[...], sink_ref[...], 32)
            kt_s[:, n_ctx:n_ctx + n_lat] = kl.T.astype(BF16)
            vext_s[n_ctx:n_ctx + n_lat, 0:128] = kvl_ref[:, 128:256].astype(BF16)
        vext_s[:, 128:256] = jnp.ones((n_ctx + n_lat, 128), BF16)

    seg256 = _seg_ones(256, 64)
    q = _seg_rms(q_ref[...], 64, seg256) * qg_ref[...]
    if rope_q:
        q = _rope(q, cosq_ref[...], sinq_ref[...], 32)
    q = q * (GQA_HEAD_DIM ** -0.5 * LOG2E)
    tq = q.shape[0]
    sub = min(ATTN_SUB, tq)
    lane = lax.broadcasted_iota(jnp.int32, (sub, 128), 1)
    lhs_list = []
    for r0 in range(0, tq, sub):
        for half in range(2):
            qh = q[r0:r0 + sub, 128 * half:128 * half + 128]
            for kv in range(2):
                in_kv = (lane >= 64 * kv) & (lane < 64 * kv + 64)
                lhs_list.append(jnp.where(in_kv, qh, 0.0).astype(BF16))
    outs = _attend_many(lhs_list, kt_s, [vext_s] * len(lhs_list))
    for j, r0 in enumerate(range(0, tq, sub)):
        for half in range(2):
            o = jnp.where(lane < 64, outs[4 * j + 2 * half], outs[4 * j + 2 * half + 1])
            gate = _silu(gg_ref[r0:r0 + sub, 128 * half:128 * half + 128])
            y_ref[r0:r0 + sub, 128 * half:128 * half + 128] = (o * gate).astype(BF16)


def _gqa_call(p_q, p_c, p_l, tabs, qg, kg, batch, n_ctx, n_lat, rope_q):
    t_total = p_q.shape[0] // batch
    tq = min(ATTN_TQ, t_total)
    nq = t_total // tq
    in_specs = [
        pl.BlockSpec((tq, 256), lambda b, i: (b * nq + i, COL_GQ // 2)),
        pl.BlockSpec((tq, 256), lambda b, i: (b * nq + i, COL_GG // 2)),
        pl.BlockSpec((n_ctx, 256), lambda b, i: (b, COL_GK // 2)),
    ]
    args = [p_q, p_q, p_c]
    if n_lat:
        in_specs.append(pl.BlockSpec((n_lat, 256), lambda b, i: (b, COL_GK // 2)))
        args.append(p_l)
    if rope_q:
        in_specs += [pl.BlockSpec((tq, 256), lambda b, i: (i, 0))] * 2
        args += [tabs["cos_g"], tabs["sin_g"]]
    if n_lat:
        in_specs += [pl.BlockSpec((n_lat, 128), lambda b, i: (0, 0))] * 2
        args += [tabs["cos_g"], tabs["sin_g"]]
    in_specs += [pl.BlockSpec((1, 256), lambda b, i: (0, 0)),
                 pl.BlockSpec((1, 128), lambda b, i: (0, 0))]
    args += [qg, kg]
    s_keys = n_ctx + n_lat
    return pl.pallas_call(
        functools.partial(_gqa_kernel, n_ctx=n_ctx, n_lat=n_lat, rope_q=rope_q),
        grid=(batch, nq),
        in_specs=in_specs,
        out_specs=pl.BlockSpec((tq, 256), lambda b, i: (b * nq + i, 0)),
        out_shape=jax.ShapeDtypeStruct((p_q.shape[0], 256), BF16),
        scratch_shapes=[pltpu.VMEM((128, s_keys), BF16),
                        pltpu.VMEM((s_keys, 256), BF16)],
        compiler_params=pltpu.CompilerParams(
            dimension_semantics=("arbitrary", "arbitrary"),
            vmem_limit_bytes=VMEM_LIMIT),
        name="gqa_attn",
    )(*args)


def _diff_kernel(*refs, n_ctx, n_lat, rope_q, lam_init):
    it = iter(refs)
    q_ref, dg_ref, kc_ref, vc_ref = next(it), next(it), next(it), next(it)
    if n_lat:
        kl_ref, vl_ref = next(it), next(it)
    if rope_q:
        cosq_ref, sinq_ref = next(it), next(it)
    if n_lat:
        cosk_ref, sink_ref = next(it), next(it)
    lam_ref, ng_ref = next(it), next(it)
    y_ref = next(it)
    kt_s, vlo_s, vhi_s = next(it), next(it), next(it)
    s_keys = n_ctx + n_lat

    @pl.when(pl.program_id(1) == 0)
    def _prep_kv():
        kt_s[:, 0:n_ctx] = kc_ref[...].T.astype(BF16)
        vlo_s[0:n_ctx, 0:128] = vc_ref[:, 0:128].astype(BF16)
        vhi_s[0:n_ctx, 0:128] = vc_ref[:, 128:256].astype(BF16)
        if n_lat:
            kl = _rope(kl_ref[...], cosk_ref[...], sink_ref[...], 16)
            kt_s[:, n_ctx:s_keys] = kl.T.astype(BF16)
            vlo_s[n_ctx:s_keys, 0:128] = vl_ref[:, 0:128].astype(BF16)
            vhi_s[n_ctx:s_keys, 0:128] = vl_ref[:, 128:256].astype(BF16)

    lp = lam_ref[...]
    lam = (jnp.exp(jnp.sum(lp[0:1, :] * lp[1:2, :], axis=-1, keepdims=True))
           - jnp.exp(jnp.sum(lp[2:3, :] * lp[3:4, :], axis=-1, keepdims=True))
           + lam_init)

    q = q_ref[...]
    if rope_q:
        q = _rope(q, cosq_ref[...], sinq_ref[...], 16)
    q = q * (DIFF_QK_DIM ** -0.5 * LOG2E)
    tq = q.shape[0]
    sub = min(ATTN_SUB, tq)
    lane256 = lax.broadcasted_iota(jnp.int32, (sub, 256), 1)
    lane128 = lax.broadcasted_iota(jnp.int32, (sub, 128), 1)
    seg128 = _seg_ones(128, 64)
    lhs_list, v_list = [], []
    for r0 in range(0, tq, sub):
        for mp in range(2 * DIFF_HEADS):
            in_map = (lane256 >= 32 * mp) & (lane256 < 32 * mp + 32)
            lhs_list.append(jnp.where(in_map, q[r0:r0 + sub, :], 0.0).astype(BF16))
        v_list += [vlo_s, vlo_s, vhi_s, vhi_s]
    heads = _attend_diff_pairs(lhs_list, kt_s, v_list, lam)
    for j, r0 in enumerate(range(0, tq, sub)):
        for half in range(2):
            o = jnp.where(lane128 < 64, heads[4 * j + 2 * half], heads[4 * j + 2 * half + 1])
            n = _seg_rms(o, 64, seg128) * ng_ref[:, 128 * half:128 * half + 128]
            n = n * (1.0 - lam_init)
            gate = _silu(dg_ref[r0:r0 + sub, 128 * half:128 * half + 128])
            y_ref[r0:r0 + sub, 128 * half:128 * half + 128] = (n * gate).astype(BF16)


def _diff_call(p_q, p_c, p_l, tabs, lam_params, ng, batch, n_ctx, n_lat, rope_q,
               lam_init):
    t_total = p_q.shape[0] // batch
    tq = min(ATTN_TQ, t_total)
    nq = t_total // tq
    in_specs = [
        pl.BlockSpec((tq, 256), lambda b, i: (b * nq + i, COL_DQ // 2)),
        pl.BlockSpec((tq, 256), lambda b, i: (b * nq + i, COL_DG // 2)),
        pl.BlockSpec((n_ctx, 256), lambda b, i: (b, COL_DK // 2)),
        pl.BlockSpec((n_ctx, 256), lambda b, i: (b, COL_DV // 2)),
    ]
    args = [p_q, p_q, p_c, p_c]
    if n_lat:
        in_specs += [pl.BlockSpec((n_lat, 256), lambda b, i: (b, COL_DK // 2)),
                     pl.BlockSpec((n_lat, 256), lambda b, i: (b, COL_DV // 2))]
        args += [p_l, p_l]
    if rope_q:
        in_specs += [pl.BlockSpec((tq, 256), lambda b, i: (i, 0))] * 2
        args += [tabs["cos_d"], tabs["sin_d"]]
    if n_lat:
        in_specs += [pl.BlockSpec((n_lat, 256), lambda b, i: (0, 0))] * 2
        args += [tabs["cos_d"], tabs["sin_d"]]
    in_specs += [pl.BlockSpec((4, DIFF_QK_DIM), lambda b, i: (0, 0)),
                 pl.BlockSpec((1, 256), lambda b, i: (0, 0))]
    args += [lam_params, ng]
    s_keys = n_ctx + n_lat
    return pl.pallas_call(
        functools.partial(_diff_kernel, n_ctx=n_ctx, n_lat=n_lat, rope_q=rope_q,
                          lam_init=lam_init),
        grid=(batch, nq),
        in_specs=in_specs,
        out_specs=pl.BlockSpec((tq, 256), lambda b, i: (b * nq + i, 0)),
        out_shape=jax.ShapeDtypeStruct((p_q.shape[0], 256), BF16),
        scratch_shapes=[pltpu.VMEM((256, s_keys), BF16),
                        pltpu.VMEM((s_keys, 128), BF16),
                        pltpu.VMEM((s_keys, 128), BF16)],
        compiler_params=pltpu.CompilerParams(
            dimension_semantics=("arbitrary", "arbitrary"),
            vmem_limit_bytes=VMEM_LIMIT),
        name="diff_attn",
    )(*args)


def _ssd_kernel(xc_ref, xl_ref, btc_ref, btl_ref, cc_ref, cl_ref, zc_ref, zl_ref,
                dtc_ref, dtl_ref, par_ref, dsk_ref, ng_ref, *rest, n_ctx, n_lat, ctx_out):
    if ctx_out:
        oc_ref, ol_ref = rest[0], rest[1]
        rest = rest[2:]
    else:
        oc_ref, ol_ref = None, rest[0]
        rest = rest[1:]
    xs_s, bts_s, cs_s, dtv_s, yacc_s, sloc_s, e_s, sin_s, dec_s, vun_s = rest

    g = pl.program_id(1)
    ncc = n_ctx // CHUNK
    ncl = n_lat // CHUNK
    nch = ncc + ncl
    GROUP = next(n for n in (6, 3, 2, 1) if nch % n == 0)
    assert nch % GROUP == 0
    t_all = n_ctx + n_lat

    xs_s[0:n_ctx, :] = xc_ref[...]
    xs_s[n_ctx:t_all, :] = xl_ref[...]
    cs_s[0:n_ctx, :] = cc_ref[...]
    cs_s[n_ctx:t_all, :] = cl_ref[...]
    bts_s[0:ncc] = btc_ref[...]
    bts_s[ncc:nch] = btl_ref[...]

    bias = par_ref[1:2, :]
    a_col = -jnp.exp(par_ref[8:24, :])

    def _softplus(v):
        return jnp.maximum(v, 0.0) + jnp.log1p(jnp.exp(-jnp.abs(v)))

    dtv_s[0:n_ctx, :] = _softplus(dtc_ref[...] + bias)
    dtv_s[n_ctx:t_all, :] = _softplus(dtl_ref[...] + bias)

    r128 = lax.broadcasted_iota(jnp.int32, (CHUNK, CHUNK), 0)
    c128 = lax.broadcasted_iota(jnp.int32, (CHUNK, CHUNK), 1)
    lower = c128 <= r128
    upper = c128 >= r128
    tril = jnp.where(lower, 1.0, 0.0).astype(BF16)
    triu = jnp.where(upper, 1.0, 0.0).astype(BF16)
    fwd_row = lax.broadcasted_iota(jnp.int32, (16, CHUNK), 0) < 8
    fwd_row1 = lax.broadcasted_iota(jnp.int32, (16, 1), 0) < 8
    er = lax.broadcasted_iota(jnp.int32, (CHUNK, 1024), 0)
    ec = lax.broadcasted_iota(jnp.int32, (CHUNK, 1024), 1)
    src_lane = 16 + 16 * (ec >> 9) + 8 * ((ec >> 8) & 1) + ((ec >> 6) & 3)
    expand = jnp.where(er == src_lane, 1.0, 0.0).astype(BF16)
    er0 = lax.broadcasted_iota(jnp.int32, (CHUNK, 512), 0)
    ec0 = lax.broadcasted_iota(jnp.int32, (CHUNK, 512), 1)
    expand_tot = jnp.where(er0 == 8 * (ec0 >> 8) + ((ec0 >> 6) & 3), 1.0, 0.0).astype(BF16)
    lane256 = lax.broadcasted_iota(jnp.int32, (CHUNK, 256), 1)
    lane128_1 = lax.broadcasted_iota(jnp.int32, (1, CHUNK), 1)

    dsk = dsk_ref[...]

    def phase_a(grp, carry):
        cs = [grp * GROUP + j for j in range(GROUP)]
        r0s = [pl.multiple_of(c * CHUNK, CHUNK) for c in cs]

        dtts = [dtv_s[pl.ds(r0, CHUNK), :].T[0:16, :] for r0 in r0s]
        a_ts = [dtt * a_col for dtt in dtts]
        acol_ts = [jnp.where(fwd_row, _dot_exact_lhs(a_t, triu), _dot_exact_lhs(a_t, tril))
                   for a_t in a_ts]
        xus = [xs_s[pl.ds(r0, CHUNK), :] for r0 in r0s]
        bts = [bts_s[c] for c in cs]
        cbfs = [cs_s[pl.ds(r0, CHUNK), :] for r0 in r0s]
        xbfs = [xu.astype(BF16) for xu in xus]
        gmats = [_dot(cbf, bt) for cbf, bt in zip(cbfs, bts)]
        tms = []
        for dtt, acol_t in zip(dtts, acol_ts):
            tot = jnp.where(fwd_row1, acol_t[:, CHUNK - 1:CHUNK], acol_t[:, 0:1])
            w_t = dtt * jnp.exp(tot - acol_t)
            e_t = jnp.exp(acol_t)
            stacked = jnp.concatenate(
                [acol_t, w_t, e_t, jnp.zeros((CHUNK - 48, CHUNK), F32)], axis=0)
            tms.append(stacked.T)
        wes = [_dot(tm.astype(BF16), expand) for tm in tms]
        decs = []
        for tm in tms:
            tot_row = jnp.where(lane128_1 < 8, tm[CHUNK - 1:CHUNK, :], tm[0:1, :])
            tot512 = _dot_exact_lhs(jnp.broadcast_to(tot_row, (8, CHUNK)), expand_tot)
            decs.append(jnp.exp(tot512))
        ydiags = [jnp.zeros((CHUNK, 256), F32) for _ in cs]
        for i in range(4):
            in_head = (lane256 >= 64 * i) & (lane256 < 64 * i + 64)
            for j in range(GROUP):
                tm, acol_t, dtt = tms[j], acol_ts[j], dtts[j]
                arg = jnp.where(lower, tm[:, i:i + 1] - acol_t[i:i + 1, :],
                                tm[:, 8 + i:9 + i] - acol_t[8 + i:9 + i, :])
                scale = (jnp.where(lower, dtt[i:i + 1, :], 0.0)
                         + jnp.where(upper, dtt[8 + i:9 + i, :], 0.0))
                wmat = (gmats[j] * jnp.exp(arg) * scale).astype(BF16)
                ydiags[j] = jnp.where(in_head, _dot(wmat, xbfs[j]), ydiags[j])
        for j, c in enumerate(cs):
            w512 = wes[j][:, 0:512]
            e512 = wes[j][:, 512:1024]
            xdw = (jnp.concatenate([xus[j], xus[j]], axis=1) * w512).astype(BF16)
            sloc_s[c] = _dot(bts[j], xdw)
            dec_s[c] = decs[j]
            e_s[c] = e512
            yacc_s[pl.ds(r0s[j], CHUNK), :] = ydiags[j] + xus[j] * dsk
        return carry

    lax.fori_loop(0, nch // GROUP, phase_a, 0)

    fwd_order = list(range(nch))
    bwd_order = list(range(ncc - 1, -1, -1)) + list(range(nch - 1, ncc - 1, -1))
    for order, lo in ((fwd_order, 0), (bwd_order, 256)):
        state = jnp.zeros((SSD_STATE, 256), F32)
        for c in order:
            sin_s[c, :, lo:lo + 256] = state.astype(BF16)
            state = (state * dec_s[c, 0:1, lo:lo + 256]
                     + sloc_s[c, :, lo:lo + 256])

    def phase_c(c, z_ref, zrow):
        r0 = c * CHUNK if isinstance(c, int) else pl.multiple_of(c * CHUNK, CHUNK)
        yo = _dot(cs_s[pl.ds(r0, CHUNK), :], sin_s[c]) * e_s[c]
        y = yacc_s[pl.ds(r0, CHUNK), :] + yo[:, 0:256] + yo[:, 256:512]
        v = y * _silu(z_ref[pl.ds(zrow, CHUNK), :])

        @pl.when(g == 0)
        def _():
            vun_s[pl.ds(r0, CHUNK), 0:256] = v

        @pl.when(g == 1)
        def _():
            vun_s[pl.ds(r0, CHUNK), 256:512] = v

    if ctx_out:
        for c in range(ncc):
            phase_c(c, zc_ref, c * CHUNK)

    def phase_c_lat(k, carry):
        phase_c(k + ncc, zl_ref, pl.multiple_of(k * CHUNK, CHUNK))
        return carry

    lax.fori_loop(0, ncl, phase_c_lat, 0)

    @pl.when(g == SSD_GROUPS - 1)
    def _finalize():
        ng = ng_ref[...]

        def norm_rows(r0, nrows):
            v = vun_s[pl.ds(r0, nrows), :]
            ms = jnp.mean(v * v, axis=-1, keepdims=True)
            return (v * lax.rsqrt(ms + EPS) * ng).astype(BF16)

        if ctx_out:
            oc_ref[...] = norm_rows(0, n_ctx)

        def fin(k, carry):
            r0 = pl.multiple_of(k * 256, 256)
            ol_ref[pl.ds(r0, 256), :] = norm_rows(n_ctx + r0, 256)
            return carry

        lax.fori_loop(0, n_lat // 256, fin, 0)


def _ssd_call(proj_c, proj_l, par, dsk, ng, batch, n_ctx, n_lat, ctx_out):
    rest_c, xu_c, cu_c, bt_c = proj_c
    rest_l, xu_l, cu_l, bt_l = proj_l
    ncc, ncl = n_ctx // CHUNK, n_lat // CHUNK
    nch = ncc + ncl
    t_all = n_ctx + n_lat
    in_specs = [
        pl.BlockSpec((n_ctx, 256), lambda b, g: (b, g)),
        pl.BlockSpec((n_lat, 256), lambda b, g: (b, g)),
        pl.BlockSpec((ncc, SSD_STATE, CHUNK), lambda b, g: (b, g, 0)),
        pl.BlockSpec((ncl, SSD_STATE, CHUNK), lambda b, g: (b, g, 0)),
        pl.BlockSpec((n_ctx, 128), lambda b, g: (b, g)),
        pl.BlockSpec((n_lat, 128), lambda b, g: (b, g)),
        pl.BlockSpec((n_ctx, 256), lambda b, g: (b, COL_Z // 2 + g)),
        pl.BlockSpec((n_lat, 256), lambda b, g: (b, COL_Z // 2 + g)),
        pl.BlockSpec((n_ctx, 128), lambda b, g: (b, COL_DT + g)),
        pl.BlockSpec((n_lat, 128), lambda b, g: (b, COL_DT + g)),
        pl.BlockSpec((None, 24, 128), lambda b, g: (g, 0, 0)),
        pl.BlockSpec((1, 256), lambda b, g: (0, g)),
        pl.BlockSpec((1, 512), lambda b, g: (0, 0)),
    ]
    args = [xu_c, xu_l, bt_c, bt_l, cu_c, cu_l, rest_c, rest_l, rest_c, rest_l,
            par, dsk, ng]
    out_specs = [pl.BlockSpec((n_lat, 512), lambda b, g: (b, 0))]
    out_shape = [jax.ShapeDtypeStruct((batch * n_lat, 512), BF16)]
    if ctx_out:
        out_specs = [pl.BlockSpec((n_ctx, 512), lambda b, g: (b, 0))] + out_specs
        out_shape = [jax.ShapeDtypeStruct((batch * n_ctx, 512), BF16)] + out_shape
    scratch = [
        pltpu.VMEM((t_all, 256), F32),
        pltpu.VMEM((nch, SSD_STATE, CHUNK), BF16),
        pltpu.VMEM((t_all, 128), BF16),
        pltpu.VMEM((t_all, 128), F32),
        pltpu.VMEM((t_all, 256), F32),
        pltpu.VMEM((nch, SSD_STATE, 512), F32),
        pltpu.VMEM((nch, CHUNK, 512), F32),
        pltpu.VMEM((nch, SSD_STATE, 512), BF16),
        pltpu.VMEM((nch, 8, 512), F32),
        pltpu.VMEM((t_all, 512), F32),
    ]
    outs = pl.pallas_call(
        functools.partial(_ssd_kernel, n_ctx=n_ctx, n_lat=n_lat, ctx_out=ctx_out),
        grid=(batch, SSD_GROUPS),
        in_specs=in_specs,
        out_specs=out_specs,
        out_shape=out_shape,
        scratch_shapes=scratch,
        compiler_params=pltpu.CompilerParams(
            dimension_semantics=("arbitrary", "arbitrary"),
            vmem_limit_bytes=VMEM_LIMIT),
        name="ssd_scan",
    )(*args)
    if ctx_out:
        return outs[1], outs[0]
    return outs[0], None


def _rope_tables(n_lat):
    rows = n_lat // GRID_W
    row_idx = jnp.repeat(jnp.arange(rows), GRID_W).astype(F32)
    col_idx = (jnp.arange(rows * GRID_W) % GRID_W).astype(F32)

    def tables(dim, reps):
        quarter = dim // 4
        inv = ROPE_BASE ** (-jnp.arange(quarter, dtype=F32) / quarter)
        ang = jnp.concatenate([row_idx[:, None] * inv, col_idx[:, None] * inv], axis=-1)
        cos, sin = jnp.cos(ang), jnp.sin(ang)
        cos2 = jnp.concatenate([cos, cos], axis=-1)
        sin2 = jnp.concatenate([-sin, sin], axis=-1)
        return jnp.tile(cos2, (1, reps)), jnp.tile(sin2, (1, reps))

    cos_g, sin_g = tables(GQA_HEAD_DIM, 4)
    cos_d, sin_d = tables(DIFF_QK_DIM, 8)
    return {"cos_g": cos_g, "sin_g": sin_g, "cos_d": cos_d, "sin_d": sin_d}


def kernel(x, c, ctx, c_ctx, w_mod, b_mod, g_pre, g_post, w_in, conv_w, conv_b,
           a_log_fwd, a_log_bwd, dt_bias_fwd, dt_bias_bwd, d_skip, ssd_norm_g,
           q_norm_g, k_norm_g, diff_lambda, diff_norm_g, w_out):
    batch, n_lat, _ = x.shape
    n_ctx = ctx.shape[1]
    depth = w_mod.shape[0]
    assert n_lat % 512 == 0 and n_ctx % 256 == 0 and (batch * n_ctx) % 512 == 0
    assert batch + 1 <= 16

    w_in_p = _take_runs(w_in.astype(BF16), _in_col_perm(), 2, IN_COLS)
    w_out_p = _take_runs(w_out.astype(BF16), _out_row_perm(), 1, None)
    conv_w8 = jnp.pad(conv_w, ((0, 0), (0, 8 - CONV_K), (0, 0)))
    conv_b1 = conv_b[:, None, :]

    def group16(fwd, bwd):
        out = jnp.zeros((depth, SSD_GROUPS, 16), F32)
        for g in range(SSD_GROUPS):
            out = out.at[:, g, 0:4].set(fwd[:, 4 * g:4 * g + 4])
            out = out.at[:, g, 8:12].set(bwd[:, 4 * g:4 * g + 4])
        return out

    ssd_par = jnp.zeros((depth, SSD_GROUPS, 24, 128), F32)
    ssd_par = ssd_par.at[:, :, 1, 0:16].set(group16(dt_bias_fwd, dt_bias_bwd))
    ssd_par = ssd_par.at[:, :, 8:24, :].set(
        jnp.broadcast_to(group16(a_log_fwd, a_log_bwd)[..., None],
                         (depth, SSD_GROUPS, 16, 128)))
    dsk = jnp.repeat(d_skip, SSD_HEAD_DIM, axis=1)[:, None, :]
    qg = jnp.tile(q_norm_g, (1, 4))[:, None, :]
    kg = jnp.tile(k_norm_g, (1, 2))[:, None, :]
    dng = jnp.tile(diff_norm_g, (1, 4))[:, None, :]
    tabs = _rope_tables(n_lat)

    cs = jnp.concatenate(
        [c, c_ctx[None, :], jnp.zeros((16 - batch - 1, D_MODEL), F32)], axis=0)
    mod_all = _mod_call(cs, w_mod, b_mod)

    h = x.reshape(batch * n_lat, D_MODEL)
    hc = ctx.reshape(batch * n_ctx, D_MODEL)
    for l in range(depth):
        ctx_out = l < depth - 1
        lam_init = 0.8 - 0.6 * float(np.exp(-0.3 * l))
        mod3 = mod_all[l][:, None, :]
        proj_l = _inproj_call(h, mod3, g_pre[l][None, :], w_in_p[l], conv_w8[l], conv_b1[l],
                              INPROJ_TM, n_lat // INPROJ_TM, n_lat // INPROJ_TM, None)
        proj_c = _inproj_call(hc, mod3, g_pre[l][None, :], w_in_p[l], conv_w8[l], conv_b1[l],
                              n_ctx, 1, None, batch)
        p_l, p_c = proj_l[0], proj_c[0]

        ys_l, ys_c = _ssd_call(proj_c, proj_l, ssd_par[l], dsk[l],
                               ssd_norm_g[l][None, :], batch, n_ctx, n_lat, ctx_out)
        yg_l = _gqa_call(p_l, p_c, p_l, tabs, qg[l], kg[l], batch, n_ctx, n_lat, True)
        yd_l = _diff_call(p_l, p_c, p_l, tabs, diff_lambda[l], dng[l], batch, n_ctx,
                          n_lat, True, lam_init)
        h = _outproj_call(ys_l, yg_l, yd_l, h, mod3, g_post[l][None, :], w_out_p[l],
                          n_lat // 512, None)
        if ctx_out:
            yg_c = _gqa_call(p_c, p_c, None, tabs, qg[l], kg[l], batch, n_ctx, 0, False)
            yd_c = _diff_call(p_c, p_c, None, tabs, diff_lambda[l], dng[l], batch,
                              n_ctx, 0, False, lam_init)
            hc = _outproj_call(ys_c, yg_c, yd_c, hc, mod3, g_post[l][None, :],
                               w_out_p[l], None, batch)
    return h.reshape(batch, n_lat, D_MODEL)
```

```python
import functools

import numpy as np
import jax
import jax.numpy as jnp
from jax import lax
from jax.experimental import pallas as pl
from jax.experimental.pallas import tpu as pltpu

F32 = jnp.float32
BF16 = jnp.bfloat16

D_MODEL = 1024
GRID_W = 64
ROPE_BASE = 10000.0
EPS = 1e-6
LOG2E = 1.4426950408889634

SSD_WIDTH = 512
SSD_HEADS = 8
SSD_HEAD_DIM = 64
SSD_GROUPS = 2
SSD_STATE = 128
CHUNK = 128
CONV_K = 5
HALO = 8
GQA_HEADS = 4
GQA_HEAD_DIM = 64
DIFF_HEADS = 4
DIFF_QK_DIM = 32
DIFF_V_DIM = 64

_IN_SPLITS = (("xbc", 1024), ("z", 512), ("dt", 16), ("gq", 256), ("gk", 128),
              ("gv", 128), ("gg", 256), ("dq", 256), ("dk", 256), ("dv", 256),
              ("dg", 256))
IN_COLS = sum(s for _, s in _IN_SPLITS)
NP = 28 * 128
GQ_HEAD_ORDER = (0, 2, 1, 3)

VMEM_LIMIT = 56 * 1024 * 1024
ATTN_TQ = 1024
ATTN_SUB = 512
OUTPROJ_TM = 1024
OUTPROJ_SUB = 256


def _in_col_perm():
    off, o = {}, 0
    for name, size in _IN_SPLITS:
        off[name] = o
        o += size
    pad = IN_COLS
    cols = list(range(off["xbc"], off["xbc"] + 1024))
    cols += list(range(off["z"], off["z"] + 512))
    for h in GQ_HEAD_ORDER:
        cols += list(range(off["gq"] + 64 * h, off["gq"] + 64 * h + 64))
    cols += list(range(off["gk"], off["gk"] + 128))
    cols += list(range(off["gv"], off["gv"] + 128))
    for h in GQ_HEAD_ORDER:
        cols += list(range(off["gg"] + 64 * h, off["gg"] + 64 * h + 64))
    for name in ("dq", "dk", "dv", "dg"):
        cols += list(range(off[name], off[name] + 256))
    for g in range(SSD_GROUPS):
        blk = [pad] * 128
        for i in range(4):
            blk[i] = off["dt"] + 4 * g + i
            blk[8 + i] = off["dt"] + SSD_HEADS + 4 * g + i
        cols += blk
    assert len(cols) == NP
    return np.asarray(cols, np.int32)


def _out_row_perm():
    rows = list(range(SSD_WIDTH))
    for h in GQ_HEAD_ORDER:
        rows += list(range(SSD_WIDTH + 64 * h, SSD_WIDTH + 64 * h + 64))
    rows += list(range(SSD_WIDTH + 256, SSD_WIDTH + 512))
    return np.asarray(rows, np.int32)


def _take_runs(arr, idx, axis, pad_index):
    pieces, start = [], 0
    idx = [int(i) for i in idx]
    while start < len(idx):
        end = start + 1
        if idx[start] == pad_index:
            while end < len(idx) and idx[end] == pad_index:
                end += 1
            shape = list(arr.shape)
            shape[axis] = end - start
            pieces.append(jnp.zeros(shape, arr.dtype))
        else:
            while end < len(idx) and idx[end] == idx[end - 1] + 1 and idx[end] != pad_index:
                end += 1
            pieces.append(lax.slice_in_dim(arr, idx[start], idx[end - 1] + 1, axis=axis))
        start = end
    return jnp.concatenate(pieces, axis=axis)


def _dot(a, b):
    return jnp.dot(a, b, preferred_element_type=F32)


def _split3(x):
    hi = x.astype(BF16)
    r1 = x - hi.astype(F32)
    mid = r1.astype(BF16)
    lo = (r1 - mid.astype(F32)).astype(BF16)
    return hi, mid, lo


def _dot_exact_lhs(x, m_bf16):
    hi, mid, lo = _split3(x)
    return _dot(hi, m_bf16) + _dot(mid, m_bf16) + _dot(lo, m_bf16)


def _silu(x):
    return x * jax.nn.sigmoid(x)


def _seg_ones(width, seg):
    r = lax.broadcasted_iota(jnp.int32, (width, width), 0)
    c = lax.broadcasted_iota(jnp.int32, (width, width), 1)
    same = (r & ~(seg - 1)) == (c & ~(seg - 1))
    return jnp.where(same, 1.0, 0.0).astype(BF16)


def _seg_rms(x, seg, seg_mat):
    ss = _dot_exact_lhs(x * x, seg_mat)
    return x * lax.rsqrt(ss * (1.0 / seg) + EPS)


def _rope(x, cos, sin_signed, half):
    w = x.shape[-1]
    lane = lax.broadcasted_iota(jnp.int32, x.shape, 1)
    first = (lane & (2 * half - 1)) < half
    swapped = jnp.where(first, pltpu.roll(x, w - half, 1), pltpu.roll(x, half, 1))
    return x * cos + swapped * sin_signed


def _mod_kernel(cs_ref, w_ref, b_ref, o_ref):
    s = _silu(cs_ref[...]).astype(BF16)
    o_ref[...] = _dot(s, w_ref[...].astype(BF16)) + b_ref[...]


def _mod_call(cs, w_mod, b_mod):
    depth = w_mod.shape[0]
    nrow = cs.shape[0]
    tn = 1024
    return pl.pallas_call(
        _mod_kernel,
        grid=(depth, 3 * D_MODEL // tn),
        in_specs=[
            pl.BlockSpec((nrow, D_MODEL), lambda l, j: (0, 0)),
            pl.BlockSpec((None, D_MODEL, tn), lambda l, j: (l, 0, j)),
            pl.BlockSpec((None, 1, tn), lambda l, j: (l, 0, j)),
        ],
        out_specs=pl.BlockSpec((None, nrow, tn), lambda l, j: (l, 0, j)),
        out_shape=jax.ShapeDtypeStruct((depth, nrow, 3 * D_MODEL), F32),
        compiler_params=pltpu.CompilerParams(
            dimension_semantics=("arbitrary", "arbitrary")),
        name="mod_proj",
    )(cs, w_mod, b_mod.reshape(depth, 1, 3 * D_MODEL))


def _inproj_kernel(h_ref, mod_ref, g_ref, w_ref, o_ref):
    x = h_ref[...]
    ms = jnp.mean(x * x, axis=-1, keepdims=True)
    y = x * lax.rsqrt(ms + EPS) * g_ref[...]
    sh = mod_ref[:, 0:D_MODEL]
    sc = mod_ref[:, D_MODEL:2 * D_MODEL]
    u = (y * (1.0 + sc) + sh).astype(BF16)
    tn = 512
    for j in range(NP // tn):
        o_ref[:, j * tn:(j + 1) * tn] = _dot(u, w_ref[:, j * tn:(j + 1) * tn])


def _inproj_call(h, mod3, g_pre, w_bf16, tiles_per_row, fixed_row):
    n_tok = h.shape[0]
    tm = 512
    if fixed_row is None:
        mod_idx = lambda i: (i // tiles_per_row, 0, 0)
    else:
        mod_idx = lambda i: (fixed_row, 0, 0)
    return pl.pallas_call(
        _inproj_kernel,
        grid=(n_tok // tm,),
        in_specs=[
            pl.BlockSpec((tm, D_MODEL), lambda i: (i, 0)),
            pl.BlockSpec((None, 1, 3 * D_MODEL), mod_idx),
            pl.BlockSpec((1, D_MODEL), lambda i: (0, 0)),
            pl.BlockSpec((D_MODEL, NP), lambda i: (0, 0)),
        ],
        out_specs=pl.BlockSpec((tm, NP), lambda i: (i, 0)),
        out_shape=jax.ShapeDtypeStruct((n_tok, NP), F32),
        compiler_params=pltpu.CompilerParams(
            dimension_semantics=("arbitrary",), vmem_limit_bytes=VMEM_LIMIT),
        name="in_proj",
    )(h, mod3, g_pre, w_bf16)


def _outproj_kernel(ys_ref, yg_ref, yd_ref, h_ref, mod_ref, g_ref, w_ref, o_ref):
    tm = h_ref.shape[0]
    sub = min(OUTPROJ_SUB, tm)
    gt = mod_ref[:, 2 * D_MODEL:3 * D_MODEL]
    gain = g_ref[...]

    def project(r0):
        return (_dot(ys_ref[r0:r0 + sub, :], w_ref[0:512, :])
                + _dot(yg_ref[r0:r0 + sub, :], w_ref[512:768, :])
                + _dot(yd_ref[r0:r0 + sub, :], w_ref[768:1024, :]))

    o_next = project(0)
    for r0 in range(0, tm, sub):
        o = o_next
        if r0 + sub < tm:
            o_next = project(r0 + sub)
        ms = jnp.mean(o * o, axis=-1, keepdims=True)
        n = o * lax.rsqrt(ms + EPS) * gain
        o_ref[r0:r0 + sub, :] = h_ref[r0:r0 + sub, :] + gt * n


def _outproj_call(ys, yg, yd, h, mod3, g_post, w_bf16, rows_per_mod, fixed_row):
    n_tok = h.shape[0]
    tm = OUTPROJ_TM
    assert n_tok % tm == 0
    if fixed_row is None:
        assert rows_per_mod % tm == 0
        mod_idx = lambda i: (i // (rows_per_mod // tm), 0, 0)
    else:
        mod_idx = lambda i: (fixed_row, 0, 0)
    return pl.pallas_call(
        _outproj_kernel,
        grid=(n_tok // tm,),
        in_specs=[
            pl.BlockSpec((tm, 512), lambda i: (i, 0)),
            pl.BlockSpec((tm, 256), lambda i: (i, 0)),
            pl.BlockSpec((tm, 256), lambda i: (i, 0)),
            pl.BlockSpec((tm, D_MODEL), lambda i: (i, 0)),
            pl.BlockSpec((None, 1, 3 * D_MODEL), mod_idx),
            pl.BlockSpec((1, D_MODEL), lambda i: (0, 0)),
            pl.BlockSpec((D_MODEL, D_MODEL), lambda i: (0, 0)),
        ],
        out_specs=pl.BlockSpec((tm, D_MODEL), lambda i: (i, 0)),
        out_shape=jax.ShapeDtypeStruct((n_tok, D_MODEL), F32),
        compiler_params=pltpu.CompilerParams(
            dimension_semantics=("arbitrary",), vmem_limit_bytes=VMEM_LIMIT),
        name="out_proj",
    )(ys, yg, yd, h, mod3, g_post, w_bf16)


def _attend_many(lhs_list, kt_ref, vext_refs):
    outs = []
    s_next = _dot(lhs_list[0], kt_ref[...])
    for i, vext_ref in enumerate(vext_refs):
        s = s_next
        if i + 1 < len(lhs_list):
            s_next = _dot(lhs_list[i + 1], kt_ref[...])
        m = jnp.max(s, axis=-1, keepdims=True)
        p = jnp.exp2(s - m).astype(BF16)
        oe = _dot(p, vext_ref[...])
        outs.append(oe[:, 0:128] / oe[:, 128:256])
    return outs


def _attend_diff_pairs(lhs_list, kt_ref, v_refs, lam):
    n_heads = len(v_refs)

    def scores(h):
        return _dot(lhs_list[2 * h], kt_ref[...]), _dot(lhs_list[2 * h + 1], kt_ref[...])

    outs = []
    s_next = scores(0)
    for h in range(n_heads):
        s_a, s_b = s_next
        if h + 1 < n_heads:
            s_next = scores(h + 1)
        e_a = jnp.exp2(s_a - jnp.max(s_a, axis=-1, keepdims=True))
        e_b = jnp.exp2(s_b - jnp.max(s_b, axis=-1, keepdims=True))
        l_a = jnp.sum(e_a, axis=-1, keepdims=True)
        l_b = jnp.sum(e_b, axis=-1, keepdims=True)
        pc = (e_a - (lam * l_a / l_b) * e_b).astype(BF16)
        outs.append(_dot(pc, v_refs[h][...]) / l_a)
    return outs


def _gqa_kernel(*refs, n_ctx, n_lat, rope_q):
    it = iter(refs)
    q_ref, gg_ref, kvc_ref = next(it), next(it), next(it)
    kvl_ref = next(it) if n_lat else None
    if rope_q:
        cosq_ref, sinq_ref = next(it), next(it)
    if n_lat:
        cosk_ref, sink_ref = next(it), next(it)
    qg_ref, kg_ref = next(it), next(it)
    y_ref = next(it)
    kt_s, vext_s = next(it), next(it)

    seg128 = _seg_ones(128, 64)

    @pl.when(pl.program_id(1) == 0)
    def _prep_kv():
        kc = _seg_rms(kvc_ref[:, 0:128], 64, seg128) * kg_ref[...]
        kt_s[:, 0:n_ctx] = kc.T.astype(BF16)
        vext_s[0:n_ctx, 0:128] = kvc_ref[:, 128:256].astype(BF16)
        if n_lat:
            kl = _seg_rms(kvl_ref[:, 0:128], 64, seg128) * kg_ref[...]
            kl = _rope(kl, cosk_ref[...], sink_ref[...], 32)
            kt_s[:, n_ctx:n_ctx + n_lat] = kl.T.astype(BF16)
            vext_s[n_ctx:n_ctx + n_lat, 0:128] = kvl_ref[:, 128:256].astype(BF16)
        vext_s[:, 128:256] = jnp.ones((n_ctx + n_lat, 128), BF16)

    seg256 = _seg_ones(256, 64)
    q = _seg_rms(q_ref[...], 64, seg256) * qg_ref[...]
    if rope_q:
        q = _rope(q, cosq_ref[...], sinq_ref[...], 32)
    q = q * (GQA_HEAD_DIM ** -0.5 * LOG2E)
    tq = q.shape[0]
    sub = min(ATTN_SUB, tq)
    lane = lax.broadcasted_iota(jnp.int32, (sub, 128), 1)
    lhs_list = []
    for r0 in range(0, tq, sub):
        for half in range(2):
            qh = q[r0:r0 + sub, 128 * half:128 * half + 128]
            for kv in range(2):
                in_kv = (lane >= 64 * kv) & (lane < 64 * kv + 64)
                lhs_list.append(jnp.where(in_kv, qh, 0.0).astype(BF16))
    outs = _attend_many(lhs_list, kt_s, [vext_s] * len(lhs_list))
    for j, r0 in enumerate(range(0, tq, sub)):
        for half in range(2):
            o = jnp.where(lane < 64, outs[4 * j + 2 * half], outs[4 * j + 2 * half + 1])
            gate = _silu(gg_ref[r0:r0 + sub, 128 * half:128 * half + 128])
            y_ref[r0:r0 + sub, 128 * half:128 * half + 128] = (o * gate).astype(BF16)


def _gqa_call(p_q, p_c, p_l, tabs, qg, kg, batch, n_ctx, n_lat, rope_q):
    t_total = p_q.shape[0] // batch
    tq = min(ATTN_TQ, t_total)
    nq = t_total // tq
    in_specs = [
        pl.BlockSpec((tq, 256), lambda b, i: (b * nq + i, 6)),
        pl.BlockSpec((tq, 256), lambda b, i: (b * nq + i, 8)),
        pl.BlockSpec((n_ctx, 256), lambda b, i: (b, 7)),
    ]
    args = [p_q, p_q, p_c]
    if n_lat:
        in_specs.append(pl.BlockSpec((n_lat, 256), lambda b, i: (b, 7)))
        args.append(p_l)
    if rope_q:
        in_specs += [pl.BlockSpec((tq, 256), lambda b, i: (i, 0))] * 2
        args += [tabs["cos_g"], tabs["sin_g"]]
    if n_lat:
        in_specs += [pl.BlockSpec((n_lat, 128), lambda b, i: (0, 0))] * 2
        args += [tabs["cos_g"], tabs["sin_g"]]
    in_specs += [pl.BlockSpec((1, 256), lambda b, i: (0, 0)),
                 pl.BlockSpec((1, 128), lambda b, i: (0, 0))]
    args += [qg, kg]
    s_keys = n_ctx + n_lat
    return pl.pallas_call(
        functools.partial(_gqa_kernel, n_ctx=n_ctx, n_lat=n_lat, rope_q=rope_q),
        grid=(batch, nq),
        in_specs=in_specs,
        out_specs=pl.BlockSpec((tq, 256), lambda b, i: (b * nq + i, 0)),
        out_shape=jax.ShapeDtypeStruct((p_q.shape[0], 256), BF16),
        scratch_shapes=[pltpu.VMEM((128, s_keys), BF16),
                        pltpu.VMEM((s_keys, 256), BF16)],
        compiler_params=pltpu.CompilerParams(
            dimension_semantics=("arbitrary", "arbitrary"),
            vmem_limit_bytes=VMEM_LIMIT),
        name="gqa_attn",
    )(*args)


def _diff_kernel(*refs, n_ctx, n_lat, rope_q, lam_init):
    it = iter(refs)
    q_ref, dg_ref, kc_ref, vc_ref = next(it), next(it), next(it), next(it)
    if n_lat:
        kl_ref, vl_ref = next(it), next(it)
    if rope_q:
        cosq_ref, sinq_ref = next(it), next(it)
    if n_lat:
        cosk_ref, sink_ref = next(it), next(it)
    lam_ref, ng_ref = next(it), next(it)
    y_ref = next(it)
    kt_s, vlo_s, vhi_s = next(it), next(it), next(it)
    s_keys = n_ctx + n_lat

    @pl.when(pl.program_id(1) == 0)
    def _prep_kv():
        kt_s[:, 0:n_ctx] = kc_ref[...].T.astype(BF16)
        vlo_s[0:n_ctx, 0:128] = vc_ref[:, 0:128].astype(BF16)
        vhi_s[0:n_ctx, 0:128] = vc_ref[:, 128:256].astype(BF16)
        if n_lat:
            kl = _rope(kl_ref[...], cosk_ref[...], sink_ref[...], 16)
            kt_s[:, n_ctx:s_keys] = kl.T.astype(BF16)
            vlo_s[n_ctx:s_keys, 0:128] = vl_ref[:, 0:128].astype(BF16)
            vhi_s[n_ctx:s_keys, 0:128] = vl_ref[:, 128:256].astype(BF16)

    lp = lam_ref[...]
    lam = (jnp.exp(jnp.sum(lp[0:1, :] * lp[1:2, :], axis=-1, keepdims=True))
           - jnp.exp(jnp.sum(lp[2:3, :] * lp[3:4, :], axis=-1, keepdims=True))
           + lam_init)

    q = q_ref[...]
    if rope_q:
        q = _rope(q, cosq_ref[...], sinq_ref[...], 16)
    q = q * (DIFF_QK_DIM ** -0.5 * LOG2E)
    tq = q.shape[0]
    sub = min(ATTN_SUB, tq)
    lane256 = lax.broadcasted_iota(jnp.int32, (sub, 256), 1)
    lane128 = lax.broadcasted_iota(jnp.int32, (sub, 128), 1)
    seg128 = _seg_ones(128, 64)
    lhs_list, v_list = [], []
    for r0 in range(0, tq, sub):
        for mp in range(2 * DIFF_HEADS):
            in_map = (lane256 >= 32 * mp) & (lane256 < 32 * mp + 32)
            lhs_list.append(jnp.where(in_map, q[r0:r0 + sub, :], 0.0).astype(BF16))
        v_list += [vlo_s, vlo_s, vhi_s, vhi_s]
    heads = _attend_diff_pairs(lhs_list, kt_s, v_list, lam)
    for j, r0 in enumerate(range(0, tq, sub)):
        for half in range(2):
            o = jnp.where(lane128 < 64, heads[4 * j + 2 * half], heads[4 * j + 2 * half + 1])
            n = _seg_rms(o, 64, seg128) * ng_ref[:, 128 * half:128 * half + 128]
            n = n * (1.0 - lam_init)
            gate = _silu(dg_ref[r0:r0 + sub, 128 * half:128 * half + 128])
            y_ref[r0:r0 + sub, 128 * half:128 * half + 128] = (n * gate).astype(BF16)


def _diff_call(p_q, p_c, p_l, tabs, lam_params, ng, batch, n_ctx, n_lat, rope_q,
               lam_init):
    t_total = p_q.shape[0] // batch
    tq = min(ATTN_TQ, t_total)
    nq = t_total // tq
    in_specs = [
        pl.BlockSpec((tq, 256), lambda b, i: (b * nq + i, 9)),
        pl.BlockSpec((tq, 256), lambda b, i: (b * nq + i, 12)),
        pl.BlockSpec((n_ctx, 256), lambda b, i: (b, 10)),
        pl.BlockSpec((n_ctx, 256), lambda b, i: (b, 11)),
    ]
    args = [p_q, p_q, p_c, p_c]
    if n_lat:
        in_specs += [pl.BlockSpec((n_lat, 256), lambda b, i: (b, 10)),
                     pl.BlockSpec((n_lat, 256), lambda b, i: (b, 11))]
        args += [p_l, p_l]
    if rope_q:
        in_specs += [pl.BlockSpec((tq, 256), lambda b, i: (i, 0))] * 2
        args += [tabs["cos_d"], tabs["sin_d"]]
    if n_lat:
        in_specs += [pl.BlockSpec((n_lat, 256), lambda b, i: (0, 0))] * 2
        args += [tabs["cos_d"], tabs["sin_d"]]
    in_specs += [pl.BlockSpec((4, DIFF_QK_DIM), lambda b, i: (0, 0)),
                 pl.BlockSpec((1, 256), lambda b, i: (0, 0))]
    args += [lam_params, ng]
    s_keys = n_ctx + n_lat
    return pl.pallas_call(
        functools.partial(_diff_kernel, n_ctx=n_ctx, n_lat=n_lat, rope_q=rope_q,
                          lam_init=lam_init),
        grid=(batch, nq),
        in_specs=in_specs,
        out_specs=pl.BlockSpec((tq, 256), lambda b, i: (b * nq + i, 0)),
        out_shape=jax.ShapeDtypeStruct((p_q.shape[0], 256), BF16),
        scratch_shapes=[pltpu.VMEM((256, s_keys), BF16),
                        pltpu.VMEM((s_keys, 128), BF16),
                        pltpu.VMEM((s_keys, 128), BF16)],
        compiler_params=pltpu.CompilerParams(
            dimension_semantics=("arbitrary", "arbitrary"),
            vmem_limit_bytes=VMEM_LIMIT),
        name="diff_attn",
    )(*args)


def _ssd_kernel(xc_ref, xl_ref, bc_ref, bl_ref, cc_ref, cl_ref, zc_ref, zl_ref,
                dtc_ref, dtl_ref, cwx_ref, cwb_ref, cwc_ref, cbx_ref, cbb_ref,
                cbc_ref, par_ref, dsk_ref, ng_ref, *rest, n_ctx, n_lat, ctx_out):
    if ctx_out:
        oc_ref, ol_ref = rest[0], rest[1]
        rest = rest[2:]
    else:
        oc_ref, ol_ref = None, rest[0]
        rest = rest[1:]
    xp_s, dtv_s, yacc_s, sloc_s, e_s, cbf_s, sin_s, dec_s, vun_s = rest

    g = pl.program_id(1)
    ncc = n_ctx // CHUNK
    ncl = n_lat // CHUNK
    nch = ncc + ncl
    GROUP = next(n for n in (6, 3, 2, 1) if nch % n == 0)
    t_all = n_ctx + n_lat
    lat0 = n_ctx + 2 * HALO

    zeros_h = jnp.zeros((HALO, 512), F32)
    xp_s[0:HALO, :] = zeros_h
    xp_s[HALO:HALO + n_ctx, 0:256] = xc_ref[...]
    xp_s[HALO:HALO + n_ctx, 256:384] = bc_ref[...]
    xp_s[HALO:HALO + n_ctx, 384:512] = cc_ref[...]
    xp_s[HALO + n_ctx:lat0, :] = zeros_h
    xp_s[lat0:lat0 + n_lat, 0:256] = xl_ref[...]
    xp_s[lat0:lat0 + n_lat, 256:384] = bl_ref[...]
    xp_s[lat0:lat0 + n_lat, 384:512] = cl_ref[...]
    xp_s[lat0 + n_lat:lat0 + n_lat + HALO, :] = zeros_h

    bias = par_ref[1:2, :]
    a_col = -jnp.exp(par_ref[8:24, :])

    def _softplus(v):
        return jnp.maximum(v, 0.0) + jnp.log1p(jnp.exp(-jnp.abs(v)))

    dtv_s[0:n_ctx, :] = _softplus(dtc_ref[...] + bias)
    dtv_s[n_ctx:t_all, :] = _softplus(dtl_ref[...] + bias)

    r128 = lax.broadcasted_iota(jnp.int32, (CHUNK, CHUNK), 0)
    c128 = lax.broadcasted_iota(jnp.int32, (CHUNK, CHUNK), 1)
    lower = c128 <= r128
    upper = c128 >= r128
    tril = jnp.where(lower, 1.0, 0.0).astype(BF16)
    triu = jnp.where(upper, 1.0, 0.0).astype(BF16)
    fwd_row = lax.broadcasted_iota(jnp.int32, (16, CHUNK), 0) < 8
    fwd_row1 = lax.broadcasted_iota(jnp.int32, (16, 1), 0) < 8
    er = lax.broadcasted_iota(jnp.int32, (CHUNK, 1024), 0)
    ec = lax.broadcasted_iota(jnp.int32, (CHUNK, 1024), 1)
    src_lane = 16 + 16 * (ec >> 9) + 8 * ((ec >> 8) & 1) + ((ec >> 6) & 3)
    expand = jnp.where(er == src_lane, 1.0, 0.0).astype(BF16)
    er0 = lax.broadcasted_iota(jnp.int32, (CHUNK, 512), 0)
    ec0 = lax.broadcasted_iota(jnp.int32, (CHUNK, 512), 1)
    expand_tot = jnp.where(er0 == 8 * (ec0 >> 8) + ((ec0 >> 6) & 3), 1.0, 0.0).astype(BF16)
    lane256 = lax.broadcasted_iota(jnp.int32, (CHUNK, 256), 1)
    lane128_1 = lax.broadcasted_iota(jnp.int32, (1, CHUNK), 1)

    cw = jnp.concatenate([cwx_ref[...], cwb_ref[...], cwc_ref[...]], axis=1)
    cb = jnp.concatenate([cbx_ref[...], cbb_ref[...], cbc_ref[...]], axis=1)
    dsk = dsk_ref[...]

    def phase_a(grp, carry):
        cs = [grp * GROUP + j for j in range(GROUP)]
        r0s = [pl.multiple_of(c * CHUNK, CHUNK) for c in cs]

        def conv(c):
            wstart = pl.multiple_of(c * CHUNK + jnp.where(c >= ncc, HALO, 0), 8)
            win = xp_s[pl.ds(wstart, CHUNK + 2 * HALO), :]
            acc = jnp.broadcast_to(cb, (CHUNK, 512))
            for k in range(CONV_K):
                d = k - CONV_K // 2
                if d == 0:
                    tap = win[HALO:HALO + CHUNK, :]
                else:
                    tap = pltpu.roll(win, (-d) % (CHUNK + 2 * HALO), 0)[HALO:HALO + CHUNK, :]
                acc = acc + cw[k:k + 1, :] * tap
            return _silu(acc)

        dtts = [dtv_s[pl.ds(r0, CHUNK), :].T[0:16, :] for r0 in r0s]
        a_ts = [dtt * a_col for dtt in dtts]
        acol_ts = [jnp.where(fwd_row, _dot_exact_lhs(a_t, triu), _dot_exact_lhs(a_t, tril))
                   for a_t in a_ts]
        us = [conv(c) for c in cs]
        xus = [u[:, 0:256] for u in us]
        bts = [u[:, 256:384].T.astype(BF16) for u in us]
        cbfs = [u[:, 384:512].astype(BF16) for u in us]
        xbfs = [xu.astype(BF16) for xu in xus]
        gmats = [_dot(cbf, bt) for cbf, bt in zip(cbfs, bts)]
        tms = []
        for dtt, acol_t in zip(dtts, acol_ts):
            tot = jnp.where(fwd_row1, acol_t[:, CHUNK - 1:CHUNK], acol_t[:, 0:1])
            w_t = dtt * jnp.exp(tot - acol_t)
            e_t = jnp.exp(acol_t)
            stacked = jnp.concatenate(
                [acol_t, w_t, e_t, jnp.zeros((CHUNK - 48, CHUNK), F32)], axis=0)
            tms.append(stacked.T)
        wes = [_dot(tm.astype(BF16), expand) for tm in tms]
        decs = []
        for tm in tms:
            tot_row = jnp.where(lane128_1 < 8, tm[CHUNK - 1:CHUNK, :], tm[0:1, :])
            tot512 = _dot_exact_lhs(jnp.broadcast_to(tot_row, (8, CHUNK)), expand_tot)
            decs.append(jnp.exp(tot512))
        ydiags = [jnp.zeros((CHUNK, 256), F32) for _ in cs]
        for i in range(4):
            in_head = (lane256 >= 64 * i) & (lane256 < 64 * i + 64)
            for j in range(GROUP):
                tm, acol_t, dtt = tms[j], acol_ts[j], dtts[j]
                arg = jnp.where(lower, tm[:, i:i + 1] - acol_t[i:i + 1, :],
                                tm[:, 8 + i:9 + i] - acol_t[8 + i:9 + i, :])
                scale = (jnp.where(lower, dtt[i:i + 1, :], 0.0)
                         + jnp.where(upper, dtt[8 + i:9 + i, :], 0.0))
                wmat = (gmats[j] * jnp.exp(arg) * scale).astype(BF16)
                ydiags[j] = jnp.where(in_head, _dot(wmat, xbfs[j]), ydiags[j])
        for j, c in enumerate(cs):
            w512 = wes[j][:, 0:512]
            e512 = wes[j][:, 512:1024]
            xdw = (jnp.concatenate([xus[j], xus[j]], axis=1) * w512).astype(BF16)
            sloc_s[c] = _dot(bts[j], xdw)
            dec_s[c] = decs[j]
            e_s[c] = e512
            cbf_s[c] = cbfs[j]
            yacc_s[pl.ds(r0s[j], CHUNK), :] = ydiags[j] + xus[j] * dsk
        return carry

    lax.fori_loop(0, nch // GROUP, phase_a, 0)

    fwd_order = list(range(nch))
    bwd_order = list(range(ncc - 1, -1, -1)) + list(range(nch - 1, ncc - 1, -1))
    for order, lo in ((fwd_order, 0), (bwd_order, 256)):
        state = jnp.zeros((SSD_STATE, 256), F32)
        for c in order:
            sin_s[c, :, lo:lo + 256] = state.astype(BF16)
            state = (state * dec_s[c, 0:1, lo:lo + 256]
                     + sloc_s[c, :, lo:lo + 256])

    def phase_c(c, z_ref, zrow):
        yo = _dot(cbf_s[c], sin_s[c]) * e_s[c]
        r0 = c * CHUNK if isinstance(c, int) else pl.multiple_of(c * CHUNK, CHUNK)
        y = yacc_s[pl.ds(r0, CHUNK), :] + yo[:, 0:256] + yo[:, 256:512]
        v = y * _silu(z_ref[pl.ds(zrow, CHUNK), :])

        @pl.when(g == 0)
        def _():
            vun_s[pl.ds(r0, CHUNK), 0:256] = v

        @pl.when(g == 1)
        def _():
            vun_s[pl.ds(r0, CHUNK), 256:512] = v

    if ctx_out:
        for c in range(ncc):
            phase_c(c, zc_ref, c * CHUNK)

    def phase_c_lat(k, carry):
        phase_c(k + ncc, zl_ref, pl.multiple_of(k * CHUNK, CHUNK))
        return carry

    lax.fori_loop(0, ncl, phase_c_lat, 0)

    @pl.when(g == SSD_GROUPS - 1)
    def _finalize():
        ng = ng_ref[...]

        def norm_rows(r0, nrows):
            v = vun_s[pl.ds(r0, nrows), :]
            ms = jnp.mean(v * v, axis=-1, keepdims=True)
            return (v * lax.rsqrt(ms + EPS) * ng).astype(BF16)

        if ctx_out:
            oc_ref[...] = norm_rows(0, n_ctx)

        def fin(k, carry):
            r0 = pl.multiple_of(k * 256, 256)
            ol_ref[pl.ds(r0, 256), :] = norm_rows(n_ctx + r0, 256)
            return carry

        lax.fori_loop(0, n_lat // 256, fin, 0)


def _ssd_call(p_c, p_l, conv_w8, conv_b, par, dsk, ng, batch, n_ctx, n_lat, ctx_out):
    nch = (n_ctx + n_lat) // CHUNK
    t_all = n_ctx + n_lat
    in_specs = [
        pl.BlockSpec((n_ctx, 256), lambda b, g: (b, g)),
        pl.BlockSpec((n_lat, 256), lambda b, g: (b, g)),
        pl.BlockSpec((n_ctx, 128), lambda b, g: (b, 4 + g)),
        pl.BlockSpec((n_lat, 128), lambda b, g: (b, 4 + g)),
        pl.BlockSpec((n_ctx, 128), lambda b, g: (b, 6 + g)),
        pl.BlockSpec((n_lat, 128), lambda b, g: (b, 6 + g)),
        pl.BlockSpec((n_ctx, 256), lambda b, g: (b, 4 + g)),
        pl.BlockSpec((n_lat, 256), lambda b, g: (b, 4 + g)),
        pl.BlockSpec((n_ctx, 128), lambda b, g: (b, 26 + g)),
        pl.BlockSpec((n_lat, 128), lambda b, g: (b, 26 + g)),
        pl.BlockSpec((8, 256), lambda b, g: (0, g)),
        pl.BlockSpec((8, 128), lambda b, g: (0, 4 + g)),
        pl.BlockSpec((8, 128), lambda b, g: (0, 6 + g)),
        pl.BlockSpec((1, 256), lambda b, g: (0, g)),
        pl.BlockSpec((1, 128), lambda b, g: (0, 4 + g)),
        pl.BlockSpec((1, 128), lambda b, g: (0, 6 + g)),
        pl.BlockSpec((None, 24, 128), lambda b, g: (g, 0, 0)),
        pl.BlockSpec((1, 256), lambda b, g: (0, g)),
        pl.BlockSpec((1, 512), lambda b, g: (0, 0)),
    ]
    args = [p_c, p_l, p_c, p_l, p_c, p_l, p_c, p_l, p_c, p_l,
            conv_w8, conv_w8, conv_w8, conv_b, conv_b, conv_b, par, dsk, ng]
    out_specs = [pl.BlockSpec((n_lat, 512), lambda b, g: (b, 0))]
    out_shape = [jax.ShapeDtypeStruct((batch * n_lat, 512), BF16)]
    if ctx_out:
        out_specs = [pl.BlockSpec((n_ctx, 512), lambda b, g: (b, 0))] + out_specs
        out_shape = [jax.ShapeDtypeStruct((batch * n_ctx, 512), BF16)] + out_shape
    scratch = [
        pltpu.VMEM((t_all + 3 * HALO, 512), F32),
        pltpu.VMEM((t_all, 128), F32),
        pltpu.VMEM((t_all, 256), F32),
        pltpu.VMEM((nch, SSD_STATE, 512), F32),
        pltpu.VMEM((nch, CHUNK, 512), F32),
        pltpu.VMEM((nch, CHUNK, 128), BF16),
        pltpu.VMEM((nch, SSD_STATE, 512), BF16),
        pltpu.VMEM((nch, 8, 512), F32),
        pltpu.VMEM((t_all, 512), F32),
    ]
    outs = pl.pallas_call(
        functools.partial(_ssd_kernel, n_ctx=n_ctx, n_lat=n_lat, ctx_out=ctx_out),
        grid=(batch, SSD_GROUPS),
        in_specs=in_specs,
        out_specs=out_specs,
        out_shape=out_shape,
        scratch_shapes=scratch,
        compiler_params=pltpu.CompilerParams(
            dimension_semantics=("arbitrary", "arbitrary"),
            vmem_limit_bytes=VMEM_LIMIT),
        name="ssd_scan",
    )(*args)
    if ctx_out:
        return outs[1], outs[0]
    return outs[0], None


def _rope_tables(n_lat):
    rows = n_lat // GRID_W
    row_idx = jnp.repeat(jnp.arange(rows), GRID_W).astype(F32)
    col_idx = (jnp.arange(rows * GRID_W) % GRID_W).astype(F32)

    def tables(dim, reps):
        quarter = dim // 4
        inv = ROPE_BASE ** (-jnp.arange(quarter, dtype=F32) / quarter)
        ang = jnp.concatenate([row_idx[:, None] * inv, col_idx[:, None] * inv], axis=-1)
        cos, sin = jnp.cos(ang), jnp.sin(ang)
        cos2 = jnp.concatenate([cos, cos], axis=-1)
        sin2 = jnp.concatenate([-sin, sin], axis=-1)
        return jnp.tile(cos2, (1, reps)), jnp.tile(sin2, (1, reps))

    cos_g, sin_g = tables(GQA_HEAD_DIM, 4)
    cos_d, sin_d = tables(DIFF_QK_DIM, 8)
    return {"cos_g": cos_g, "sin_g": sin_g, "cos_d": cos_d, "sin_d": sin_d}


def kernel(x, c, ctx, c_ctx, w_mod, b_mod, g_pre, g_post, w_in, conv_w, conv_b,
           a_log_fwd, a_log_bwd, dt_bias_fwd, dt_bias_bwd, d_skip, ssd_norm_g,
           q_norm_g, k_norm_g, diff_lambda, diff_norm_g, w_out):
    batch, n_lat, _ = x.shape
    n_ctx = ctx.shape[1]
    depth = w_mod.shape[0]
    assert n_lat % 512 == 0 and n_ctx % 256 == 0 and (batch * n_ctx) % 512 == 0
    assert batch + 1 <= 16

    w_in_p = _take_runs(w_in.astype(BF16), _in_col_perm(), 2, IN_COLS)
    w_out_p = _take_runs(w_out.astype(BF16), _out_row_perm(), 1, None)
    conv_w8 = jnp.pad(conv_w, ((0, 0), (0, 8 - CONV_K), (0, 0)))
    conv_b1 = conv_b[:, None, :]

    def group16(fwd, bwd):
        out = jnp.zeros((depth, SSD_GROUPS, 16), F32)
        for g in range(SSD_GROUPS):
            out = out.at[:, g, 0:4].set(fwd[:, 4 * g:4 * g + 4])
            out = out.at[:, g, 8:12].set(bwd[:, 4 * g:4 * g + 4])
        return out

    ssd_par = jnp.zeros((depth, SSD_GROUPS, 24, 128), F32)
    ssd_par = ssd_par.at[:, :, 1, 0:16].set(group16(dt_bias_fwd, dt_bias_bwd))
    ssd_par = ssd_par.at[:, :, 8:24, :].set(
        jnp.broadcast_to(group16(a_log_fwd, a_log_bwd)[..., None],
                         (depth, SSD_GROUPS, 16, 128)))
    dsk = jnp.repeat(d_skip, SSD_HEAD_DIM, axis=1)[:, None, :]
    qg = jnp.tile(q_norm_g, (1, 4))[:, None, :]
    kg = jnp.tile(k_norm_g, (1, 2))[:, None, :]
    dng = jnp.tile(diff_norm_g, (1, 4))[:, None, :]
    tabs = _rope_tables(n_lat)

    cs = jnp.concatenate(
        [c, c_ctx[None, :], jnp.zeros((16 - batch - 1, D_MODEL), F32)], axis=0)
    mod_all = _mod_call(cs, w_mod, b_mod)

    h = x.reshape(batch * n_lat, D_MODEL)
    hc = ctx.reshape(batch * n_ctx, D_MODEL)
    for l in range(depth):
        ctx_out = l < depth - 1
        lam_init = 0.8 - 0.6 * float(np.exp(-0.3 * l))
        mod3 = mod_all[l][:, None, :]
        p_l = _inproj_call(h, mod3, g_pre[l][None, :], w_in_p[l], n_lat // 512, None)
        p_c = _inproj_call(hc, mod3, g_pre[l][None, :], w_in_p[l], None, batch)

        ys_l, ys_c = _ssd_call(p_c, p_l, conv_w8[l], conv_b1[l], ssd_par[l], dsk[l],
                               ssd_norm_g[l][None, :], batch, n_ctx, n_lat, ctx_out)
        yg_l = _gqa_call(p_l, p_c, p_l, tabs, qg[l], kg[l], batch, n_ctx, n_lat, True)
        yd_l = _diff_call(p_l, p_c, p_l, tabs, diff_lambda[l], dng[l], batch, n_ctx,
                          n_lat, True, lam_init)
        h = _outproj_call(ys_l, yg_l, yd_l, h, mod3, g_post[l][None, :], w_out_p[l],
                          n_lat, None)
        if ctx_out:
            yg_c = _gqa_call(p_c, p_c, None, tabs, qg[l], kg[l], batch, n_ctx, 0, False)
            yd_c = _diff_call(p_c, p_c, None, tabs, diff_lambda[l], dng[l], batch,
                              n_ctx, 0, False, lam_init)
            hc = _outproj_call(ys_c, yg_c, yd_c, hc, mod3, g_post[l][None, :],
                               w_out_p[l], None, batch)
    return h.reshape(batch, n_lat, D_MODEL)
```

```python
import functools

import numpy as np
import jax
import jax.numpy as jnp
from jax import lax
from jax.experimental import pallas as pl
from jax.experimental.pallas import tpu as pltpu

F32 = jnp.float32
BF16 = jnp.bfloat16

D_MODEL = 1024
GRID_W = 64
ROPE_BASE = 10000.0
EPS = 1e-6
LOG2E = 1.4426950408889634

SSD_WIDTH = 512
SSD_HEADS = 8
SSD_HEAD_DIM = 64
SSD_GROUPS = 2
SSD_STATE = 128
CHUNK = 128
CONV_K = 5
HALO = 8
GQA_HEADS = 4
GQA_HEAD_DIM = 64
DIFF_HEADS = 4
DIFF_QK_DIM = 32
DIFF_V_DIM = 64

_IN_SPLITS = (("xbc", 1024), ("z", 512), ("dt", 16), ("gq", 256), ("gk", 128),
              ("gv", 128), ("gg", 256), ("dq", 256), ("dk", 256), ("dv", 256),
              ("dg", 256))
IN_COLS = sum(s for _, s in _IN_SPLITS)
NP = 28 * 128
GQ_HEAD_ORDER = (0, 2, 1, 3)

VMEM_LIMIT = 56 * 1024 * 1024
ATTN_TQ = 1024
ATTN_SUB = 512
OUTPROJ_TM = 1024
OUTPROJ_SUB = 256


def _in_col_perm():
    off, o = {}, 0
    for name, size in _IN_SPLITS:
        off[name] = o
        o += size
    pad = IN_COLS
    cols = list(range(off["xbc"], off["xbc"] + 1024))
    cols += list(range(off["z"], off["z"] + 512))
    for h in GQ_HEAD_ORDER:
        cols += list(range(off["gq"] + 64 * h, off["gq"] + 64 * h + 64))
    cols += list(range(off["gk"], off["gk"] + 128))
    cols += list(range(off["gv"], off["gv"] + 128))
    for h in GQ_HEAD_ORDER:
        cols += list(range(off["gg"] + 64 * h, off["gg"] + 64 * h + 64))
    for name in ("dq", "dk", "dv", "dg"):
        cols += list(range(off[name], off[name] + 256))
    for g in range(SSD_GROUPS):
        blk = [pad] * 128
        for i in range(4):
            blk[i] = off["dt"] + 4 * g + i
            blk[8 + i] = off["dt"] + SSD_HEADS + 4 * g + i
        cols += blk
    assert len(cols) == NP
    return np.asarray(cols, np.int32)


def _out_row_perm():
    rows = list(range(SSD_WIDTH))
    for h in GQ_HEAD_ORDER:
        rows += list(range(SSD_WIDTH + 64 * h, SSD_WIDTH + 64 * h + 64))
    rows += list(range(SSD_WIDTH + 256, SSD_WIDTH + 512))
    return np.asarray(rows, np.int32)


def _take_runs(arr, idx, axis, pad_index):
    pieces, start = [], 0
    idx = [int(i) for i in idx]
    while start < len(idx):
        end = start + 1
        if idx[start] == pad_index:
            while end < len(idx) and idx[end] == pad_index:
                end += 1
            shape = list(arr.shape)
            shape[axis] = end - start
            pieces.append(jnp.zeros(shape, arr.dtype))
        else:
            while end < len(idx) and idx[end] == idx[end - 1] + 1 and idx[end] != pad_index:
                end += 1
            pieces.append(lax.slice_in_dim(arr, idx[start], idx[end - 1] + 1, axis=axis))
        start = end
    return jnp.concatenate(pieces, axis=axis)


def _dot(a, b):
    return jnp.dot(a, b, preferred_element_type=F32)


def _split3(x):
    hi = x.astype(BF16)
    r1 = x - hi.astype(F32)
    mid = r1.astype(BF16)
    lo = (r1 - mid.astype(F32)).astype(BF16)
    return hi, mid, lo


def _dot_exact_lhs(x, m_bf16):
    hi, mid, lo = _split3(x)
    return _dot(hi, m_bf16) + _dot(mid, m_bf16) + _dot(lo, m_bf16)


def _silu(x):
    return x * jax.nn.sigmoid(x)


def _seg_ones(width, seg):
    r = lax.broadcasted_iota(jnp.int32, (width, width), 0)
    c = lax.broadcasted_iota(jnp.int32, (width, width), 1)
    same = (r & ~(seg - 1)) == (c & ~(seg - 1))
    return jnp.where(same, 1.0, 0.0).astype(BF16)


def _seg_rms(x, seg, seg_mat):
    ss = _dot_exact_lhs(x * x, seg_mat)
    return x * lax.rsqrt(ss * (1.0 / seg) + EPS)


def _rope(x, cos, sin_signed, half):
    w = x.shape[-1]
    lane = lax.broadcasted_iota(jnp.int32, x.shape, 1)
    first = (lane & (2 * half - 1)) < half
    swapped = jnp.where(first, pltpu.roll(x, w - half, 1), pltpu.roll(x, half, 1))
    return x * cos + swapped * sin_signed


def _mod_kernel(cs_ref, w_ref, b_ref, o_ref):
    s = _silu(cs_ref[...]).astype(BF16)
    o_ref[...] = _dot(s, w_ref[...].astype(BF16)) + b_ref[...]


def _mod_call(cs, w_mod, b_mod):
    depth = w_mod.shape[0]
    nrow = cs.shape[0]
    tn = 1024
    return pl.pallas_call(
        _mod_kernel,
        grid=(depth, 3 * D_MODEL // tn),
        in_specs=[
            pl.BlockSpec((nrow, D_MODEL), lambda l, j: (0, 0)),
            pl.BlockSpec((None, D_MODEL, tn), lambda l, j: (l, 0, j)),
            pl.BlockSpec((None, 1, tn), lambda l, j: (l, 0, j)),
        ],
        out_specs=pl.BlockSpec((None, nrow, tn), lambda l, j: (l, 0, j)),
        out_shape=jax.ShapeDtypeStruct((depth, nrow, 3 * D_MODEL), F32),
        compiler_params=pltpu.CompilerParams(
            dimension_semantics=("arbitrary", "arbitrary")),
        name="mod_proj",
    )(cs, w_mod, b_mod.reshape(depth, 1, 3 * D_MODEL))


def _inproj_kernel(h_ref, mod_ref, g_ref, w_ref, o_ref):
    x = h_ref[...]
    ms = jnp.mean(x * x, axis=-1, keepdims=True)
    y = x * lax.rsqrt(ms + EPS) * g_ref[...]
    sh = mod_ref[:, 0:D_MODEL]
    sc = mod_ref[:, D_MODEL:2 * D_MODEL]
    u = (y * (1.0 + sc) + sh).astype(BF16)
    tn = 512
    for j in range(NP // tn):
        o_ref[:, j * tn:(j + 1) * tn] = _dot(u, w_ref[:, j * tn:(j + 1) * tn])


def _inproj_call(h, mod3, g_pre, w_bf16, tiles_per_row, fixed_row):
    n_tok = h.shape[0]
    tm = 512
    if fixed_row is None:
        mod_idx = lambda i: (i // tiles_per_row, 0, 0)
    else:
        mod_idx = lambda i: (fixed_row, 0, 0)
    return pl.pallas_call(
        _inproj_kernel,
        grid=(n_tok // tm,),
        in_specs=[
            pl.BlockSpec((tm, D_MODEL), lambda i: (i, 0)),
            pl.BlockSpec((None, 1, 3 * D_MODEL), mod_idx),
            pl.BlockSpec((1, D_MODEL), lambda i: (0, 0)),
            pl.BlockSpec((D_MODEL, NP), lambda i: (0, 0)),
        ],
        out_specs=pl.BlockSpec((tm, NP), lambda i: (i, 0)),
        out_shape=jax.ShapeDtypeStruct((n_tok, NP), F32),
        compiler_params=pltpu.CompilerParams(
            dimension_semantics=("arbitrary",), vmem_limit_bytes=VMEM_LIMIT),
        name="in_proj",
    )(h, mod3, g_pre, w_bf16)


def _outproj_kernel(ys_ref, yg_ref, yd_ref, h_ref, mod_ref, g_ref, w_ref, o_ref):
    tm = h_ref.shape[0]
    sub = min(OUTPROJ_SUB, tm)
    gt = mod_ref[:, 2 * D_MODEL:3 * D_MODEL]
    gain = g_ref[...]

    def project(r0):
        return (_dot(ys_ref[r0:r0 + sub, :], w_ref[0:512, :])
                + _dot(yg_ref[r0:r0 + sub, :], w_ref[512:768, :])
                + _dot(yd_ref[r0:r0 + sub, :], w_ref[768:1024, :]))

    o_next = project(0)
    for r0 in range(0, tm, sub):
        o = o_next
        if r0 + sub < tm:
            o_next = project(r0 + sub)
        ms = jnp.mean(o * o, axis=-1, keepdims=True)
        n = o * lax.rsqrt(ms + EPS) * gain
        o_ref[r0:r0 + sub, :] = h_ref[r0:r0 + sub, :] + gt * n


def _outproj_call(ys, yg, yd, h, mod3, g_post, w_bf16, rows_per_mod, fixed_row):
    n_tok = h.shape[0]
    tm = OUTPROJ_TM
    assert n_tok % tm == 0
    if fixed_row is None:
        assert rows_per_mod % tm == 0
        mod_idx = lambda i: (i // (rows_per_mod // tm), 0, 0)
    else:
        mod_idx = lambda i: (fixed_row, 0, 0)
    return pl.pallas_call(
        _outproj_kernel,
        grid=(n_tok // tm,),
        in_specs=[
            pl.BlockSpec((tm, 512), lambda i: (i, 0)),
            pl.BlockSpec((tm, 256), lambda i: (i, 0)),
            pl.BlockSpec((tm, 256), lambda i: (i, 0)),
            pl.BlockSpec((tm, D_MODEL), lambda i: (i, 0)),
            pl.BlockSpec((None, 1, 3 * D_MODEL), mod_idx),
            pl.BlockSpec((1, D_MODEL), lambda i: (0, 0)),
            pl.BlockSpec((D_MODEL, D_MODEL), lambda i: (0, 0)),
        ],
        out_specs=pl.BlockSpec((tm, D_MODEL), lambda i: (i, 0)),
        out_shape=jax.ShapeDtypeStruct((n_tok, D_MODEL), F32),
        compiler_params=pltpu.CompilerParams(
            dimension_semantics=("arbitrary",), vmem_limit_bytes=VMEM_LIMIT),
        name="out_proj",
    )(ys, yg, yd, h, mod3, g_post, w_bf16)


def _attend_many(lhs_list, kt_ref, vext_refs):
    outs = []
    s_next = _dot(lhs_list[0], kt_ref[...])
    for i, vext_ref in enumerate(vext_refs):
        s = s_next
        if i + 1 < len(lhs_list):
            s_next = _dot(lhs_list[i + 1], kt_ref[...])
        m = jnp.max(s, axis=-1, keepdims=True)
        p = jnp.exp2(s - m).astype(BF16)
        oe = _dot(p, vext_ref[...])
        outs.append(oe[:, 0:128] / oe[:, 128:256])
    return outs


def _attend_diff_pairs(lhs_list, kt_ref, v_refs, lam):
    n_heads = len(v_refs)

    def scores(h):
        return _dot(lhs_list[2 * h], kt_ref[...]), _dot(lhs_list[2 * h + 1], kt_ref[...])

    outs = []
    s_next = scores(0)
    for h in range(n_heads):
        s_a, s_b = s_next
        if h + 1 < n_heads:
            s_next = scores(h + 1)
        e_a = jnp.exp2(s_a - jnp.max(s_a, axis=-1, keepdims=True))
        e_b = jnp.exp2(s_b - jnp.max(s_b, axis=-1, keepdims=True))
        l_a = jnp.sum(e_a, axis=-1, keepdims=True)
        l_b = jnp.sum(e_b, axis=-1, keepdims=True)
        pc = (e_a - (lam * l_a / l_b) * e_b).astype(BF16)
        outs.append(_dot(pc, v_refs[h][...]) / l_a)
    return outs


def _gqa_kernel(*refs, n_ctx, n_lat, rope_q):
    it = iter(refs)
    q_ref, gg_ref, kvc_ref = next(it), next(it), next(it)
    kvl_ref = next(it) if n_lat else None
    if rope_q:
        cosq_ref, sinq_ref = next(it), next(it)
    if n_lat:
        cosk_ref, sink_ref = next(it), next(it)
    qg_ref, kg_ref = next(it), next(it)
    y_ref = next(it)
    kt_s, vext_s = next(it), next(it)

    seg128 = _seg_ones(128, 64)

    @pl.when(pl.program_id(1) == 0)
    def _prep_kv():
        kc = _seg_rms(kvc_ref[:, 0:128], 64, seg128) * kg_ref[...]
        kt_s[:, 0:n_ctx] = kc.T.astype(BF16)
        vext_s[0:n_ctx, 0:128] = kvc_ref[:, 128:256].astype(BF16)
        if n_lat:
            kl = _seg_rms(kvl_ref[:, 0:128], 64, seg128) * kg_ref[...]
            kl = _rope(kl, cosk_ref[...], sink_ref[...], 32)
            kt_s[:, n_ctx:n_ctx + n_lat] = kl.T.astype(BF16)
            vext_s[n_ctx:n_ctx + n_lat, 0:128] = kvl_ref[:, 128:256].astype(BF16)
        vext_s[:, 128:256] = jnp.ones((n_ctx + n_lat, 128), BF16)

    seg256 = _seg_ones(256, 64)
    q = _seg_rms(q_ref[...], 64, seg256) * qg_ref[...]
    if rope_q:
        q = _rope(q, cosq_ref[...], sinq_ref[...], 32)
    q = q * (GQA_HEAD_DIM ** -0.5 * LOG2E)
    tq = q.shape[0]
    sub = min(ATTN_SUB, tq)
    lane = lax.broadcasted_iota(jnp.int32, (sub, 128), 1)
    lhs_list = []
    for r0 in range(0, tq, sub):
        for half in range(2):
            qh = q[r0:r0 + sub, 128 * half:128 * half + 128]
            for kv in range(2):
                in_kv = (lane >= 64 * kv) & (lane < 64 * kv + 64)
                lhs_list.append(jnp.where(in_kv, qh, 0.0).astype(BF16))
    outs = _attend_many(lhs_list, kt_s, [vext_s] * len(lhs_list))
    for j, r0 in enumerate(range(0, tq, sub)):
        for half in range(2):
            o = jnp.where(lane < 64, outs[4 * j + 2 * half], outs[4 * j + 2 * half + 1])
            gate = _silu(gg_ref[r0:r0 + sub, 128 * half:128 * half + 128])
            y_ref[r0:r0 + sub, 128 * half:128 * half + 128] = (o * gate).astype(BF16)


def _gqa_call(p_q, p_c, p_l, tabs, qg, kg, batch, n_ctx, n_lat, rope_q):
    t_total = p_q.shape[0] // batch
    tq = min(ATTN_TQ, t_total)
    nq = t_total // tq
    in_specs = [
        pl.BlockSpec((tq, 256), lambda b, i: (b * nq + i, 6)),
        pl.BlockSpec((tq, 256), lambda b, i: (b * nq + i, 8)),
        pl.BlockSpec((n_ctx, 256), lambda b, i: (b, 7)),
    ]
    args = [p_q, p_q, p_c]
    if n_lat:
        in_specs.append(pl.BlockSpec((n_lat, 256), lambda b, i: (b, 7)))
        args.append(p_l)
    if rope_q:
        in_specs += [pl.BlockSpec((tq, 256), lambda b, i: (i, 0))] * 2
        args += [tabs["cos_g"], tabs["sin_g"]]
    if n_lat:
        in_specs += [pl.BlockSpec((n_lat, 128), lambda b, i: (0, 0))] * 2
        args += [tabs["cos_g"], tabs["sin_g"]]
    in_specs += [pl.BlockSpec((1, 256), lambda b, i: (0, 0)),
                 pl.BlockSpec((1, 128), lambda b, i: (0, 0))]
    args += [qg, kg]
    s_keys = n_ctx + n_lat
    return pl.pallas_call(
        functools.partial(_gqa_kernel, n_ctx=n_ctx, n_lat=n_lat, rope_q=rope_q),
        grid=(batch, nq),
        in_specs=in_specs,
        out_specs=pl.BlockSpec((tq, 256), lambda b, i: (b * nq + i, 0)),
        out_shape=jax.ShapeDtypeStruct((p_q.shape[0], 256), BF16),
        scratch_shapes=[pltpu.VMEM((128, s_keys), BF16),
                        pltpu.VMEM((s_keys, 256), BF16)],
        compiler_params=pltpu.CompilerParams(
            dimension_semantics=("arbitrary", "arbitrary"),
            vmem_limit_bytes=VMEM_LIMIT),
        name="gqa_attn",
    )(*args)


def _diff_kernel(*refs, n_ctx, n_lat, rope_q, lam_init):
    it = iter(refs)
    q_ref, dg_ref, kc_ref, vc_ref = next(it), next(it), next(it), next(it)
    if n_lat:
        kl_ref, vl_ref = next(it), next(it)
    if rope_q:
        cosq_ref, sinq_ref = next(it), next(it)
    if n_lat:
        cosk_ref, sink_ref = next(it), next(it)
    lam_ref, ng_ref = next(it), next(it)
    y_ref = next(it)
    kt_s, vlo_s, vhi_s = next(it), next(it), next(it)
    s_keys = n_ctx + n_lat

    @pl.when(pl.program_id(1) == 0)
    def _prep_kv():
        kt_s[:, 0:n_ctx] = kc_ref[...].T.astype(BF16)
        vlo_s[0:n_ctx, 0:128] = vc_ref[:, 0:128].astype(BF16)
        vhi_s[0:n_ctx, 0:128] = vc_ref[:, 128:256].astype(BF16)
        if n_lat:
            kl = _rope(kl_ref[...], cosk_ref[...], sink_ref[...], 16)
            kt_s[:, n_ctx:s_keys] = kl.T.astype(BF16)
            vlo_s[n_ctx:s_keys, 0:128] = vl_ref[:, 0:128].astype(BF16)
            vhi_s[n_ctx:s_keys, 0:128] = vl_ref[:, 128:256].astype(BF16)

    lp = lam_ref[...]
    lam = (jnp.exp(jnp.sum(lp[0:1, :] * lp[1:2, :], axis=-1, keepdims=True))
           - jnp.exp(jnp.sum(lp[2:3, :] * lp[3:4, :], axis=-1, keepdims=True))
           + lam_init)

    q = q_ref[...]
    if rope_q:
        q = _rope(q, cosq_ref[...], sinq_ref[...], 16)
    q = q * (DIFF_QK_DIM ** -0.5 * LOG2E)
    tq = q.shape[0]
    sub = min(ATTN_SUB, tq)
    lane256 = lax.broadcasted_iota(jnp.int32, (sub, 256), 1)
    lane128 = lax.broadcasted_iota(jnp.int32, (sub, 128), 1)
    seg128 = _seg_ones(128, 64)
    lhs_list, v_list = [], []
    for r0 in range(0, tq, sub):
        for mp in range(2 * DIFF_HEADS):
            in_map = (lane256 >= 32 * mp) & (lane256 < 32 * mp + 32)
            lhs_list.append(jnp.where(in_map, q[r0:r0 + sub, :], 0.0).astype(BF16))
        v_list += [vlo_s, vlo_s, vhi_s, vhi_s]
    heads = _attend_diff_pairs(lhs_list, kt_s, v_list, lam)
    for j, r0 in enumerate(range(0, tq, sub)):
        for half in range(2):
            o = jnp.where(lane128 < 64, heads[4 * j + 2 * half], heads[4 * j + 2 * half + 1])
            n = _seg_rms(o, 64, seg128) * ng_ref[:, 128 * half:128 * half + 128]
            n = n * (1.0 - lam_init)
            gate = _silu(dg_ref[r0:r0 + sub, 128 * half:128 * half + 128])
            y_ref[r0:r0 + sub, 128 * half:128 * half + 128] = (n * gate).astype(BF16)


def _diff_call(p_q, p_c, p_l, tabs, lam_params, ng, batch, n_ctx, n_lat, rope_q,
               lam_init):
    t_total = p_q.shape[0] // batch
    tq = min(ATTN_TQ, t_total)
    nq = t_total // tq
    in_specs = [
        pl.BlockSpec((tq, 256), lambda b, i: (b * nq + i, 9)),
        pl.BlockSpec((tq, 256), lambda b, i: (b * nq + i, 12)),
        pl.BlockSpec((n_ctx, 256), lambda b, i: (b, 10)),
        pl.BlockSpec((n_ctx, 256), lambda b, i: (b, 11)),
    ]
    args = [p_q, p_q, p_c, p_c]
    if n_lat:
        in_specs += [pl.BlockSpec((n_lat, 256), lambda b, i: (b, 10)),
                     pl.BlockSpec((n_lat, 256), lambda b, i: (b, 11))]
        args += [p_l, p_l]
    if rope_q:
        in_specs += [pl.BlockSpec((tq, 256), lambda b, i: (i, 0))] * 2
        args += [tabs["cos_d"], tabs["sin_d"]]
    if n_lat:
        in_specs += [pl.BlockSpec((n_lat, 256), lambda b, i: (0, 0))] * 2
        args += [tabs["cos_d"], tabs["sin_d"]]
    in_specs += [pl.BlockSpec((4, DIFF_QK_DIM), lambda b, i: (0, 0)),
                 pl.BlockSpec((1, 256), lambda b, i: (0, 0))]
    args += [lam_params, ng]
    s_keys = n_ctx + n_lat
    return pl.pallas_call(
        functools.partial(_diff_kernel, n_ctx=n_ctx, n_lat=n_lat, rope_q=rope_q,
                          lam_init=lam_init),
        grid=(batch, nq),
        in_specs=in_specs,
        out_specs=pl.BlockSpec((tq, 256), lambda b, i: (b * nq + i, 0)),
        out_shape=jax.ShapeDtypeStruct((p_q.shape[0], 256), BF16),
        scratch_shapes=[pltpu.VMEM((256, s_keys), BF16),
                        pltpu.VMEM((s_keys, 128), BF16),
                        pltpu.VMEM((s_keys, 128), BF16)],
        compiler_params=pltpu.CompilerParams(
            dimension_semantics=("arbitrary", "arbitrary"),
            vmem_limit_bytes=VMEM_LIMIT),
        name="diff_attn",
    )(*args)


def _ssd_kernel(xc_ref, xl_ref, bc_ref, bl_ref, cc_ref, cl_ref, zc_ref, zl_ref,
                dtc_ref, dtl_ref, cwx_ref, cwb_ref, cwc_ref, cbx_ref, cbb_ref,
                cbc_ref, par_ref, dsk_ref, ng_ref, *rest, n_ctx, n_lat, ctx_out):
    if ctx_out:
        oc_ref, ol_ref = rest[0], rest[1]
        rest = rest[2:]
    else:
        oc_ref, ol_ref = None, rest[0]
        rest = rest[1:]
    xp_s, dtr_s, yacc_s, sloc_s, e_s, cbf_s, sin_s, dec_s, vun_s = rest

    g = pl.program_id(1)
    ncc = n_ctx // CHUNK
    ncl = n_lat // CHUNK
    nch = ncc + ncl
    GROUP = next(n for n in (6, 3, 2, 1) if nch % n == 0)
    t_all = n_ctx + n_lat
    lat0 = n_ctx + 2 * HALO

    zeros_h = jnp.zeros((HALO, 512), F32)
    xp_s[0:HALO, :] = zeros_h
    xp_s[HALO:HALO + n_ctx, 0:256] = xc_ref[...]
    xp_s[HALO:HALO + n_ctx, 256:384] = bc_ref[...]
    xp_s[HALO:HALO + n_ctx, 384:512] = cc_ref[...]
    xp_s[HALO + n_ctx:lat0, :] = zeros_h
    xp_s[lat0:lat0 + n_lat, 0:256] = xl_ref[...]
    xp_s[lat0:lat0 + n_lat, 256:384] = bl_ref[...]
    xp_s[lat0:lat0 + n_lat, 384:512] = cl_ref[...]
    xp_s[lat0 + n_lat:lat0 + n_lat + HALO, :] = zeros_h

    dtr_s[0:n_ctx, :] = dtc_ref[...]
    dtr_s[n_ctx:t_all, :] = dtl_ref[...]
    a_col = -jnp.exp(par_ref[0:16, :])
    bias_col = par_ref[16:32, :]

    def _softplus(v):
        return jnp.maximum(v, 0.0) + jnp.log1p(jnp.exp(-jnp.abs(v)))

    r128 = lax.broadcasted_iota(jnp.int32, (CHUNK, CHUNK), 0)
    c128 = lax.broadcasted_iota(jnp.int32, (CHUNK, CHUNK), 1)
    lower = c128 <= r128
    upper = c128 >= r128
    tril = jnp.where(lower, 1.0, 0.0).astype(BF16)
    triu = jnp.where(upper, 1.0, 0.0).astype(BF16)
    fwd_row = lax.broadcasted_iota(jnp.int32, (16, CHUNK), 0) < 8
    fwd_row1 = lax.broadcasted_iota(jnp.int32, (16, 1), 0) < 8
    er = lax.broadcasted_iota(jnp.int32, (CHUNK, 1024), 0)
    ec = lax.broadcasted_iota(jnp.int32, (CHUNK, 1024), 1)
    src_lane = 16 + 16 * (ec >> 9) + 8 * ((ec >> 8) & 1) + ((ec >> 6) & 3)
    expand = jnp.where(er == src_lane, 1.0, 0.0).astype(BF16)
    er0 = lax.broadcasted_iota(jnp.int32, (CHUNK, 512), 0)
    ec0 = lax.broadcasted_iota(jnp.int32, (CHUNK, 512), 1)
    expand_tot = jnp.where(er0 == 8 * (ec0 >> 8) + ((ec0 >> 6) & 3), 1.0, 0.0).astype(BF16)
    lane256 = lax.broadcasted_iota(jnp.int32, (CHUNK, 256), 1)
    lane128_1 = lax.broadcasted_iota(jnp.int32, (1, CHUNK), 1)

    cw = jnp.concatenate([cwx_ref[...], cwb_ref[...], cwc_ref[...]], axis=1)
    cb = jnp.concatenate([cbx_ref[...], cbb_ref[...], cbc_ref[...]], axis=1)
    dsk = dsk_ref[...]

    def phase_a(grp, carry):
        cs = [grp * GROUP + j for j in range(GROUP)]
        r0s = [pl.multiple_of(c * CHUNK, CHUNK) for c in cs]

        def conv(c):
            wstart = pl.multiple_of(c * CHUNK + jnp.where(c >= ncc, HALO, 0), 8)
            win = xp_s[pl.ds(wstart, CHUNK + 2 * HALO), :]
            acc = jnp.broadcast_to(cb, (CHUNK, 512))
            for k in range(CONV_K):
                d = k - CONV_K // 2
                if d == 0:
                    tap = win[HALO:HALO + CHUNK, :]
                else:
                    tap = pltpu.roll(win, (-d) % (CHUNK + 2 * HALO), 0)[HALO:HALO + CHUNK, :]
                acc = acc + cw[k:k + 1, :] * tap
            return _silu(acc)

        dtts = [_softplus(dtr_s[pl.ds(r0, CHUNK), :].T[0:16, :] + bias_col) for r0 in r0s]
        a_ts = [dtt * a_col for dtt in dtts]
        acol_ts = [jnp.where(fwd_row, _dot_exact_lhs(a_t, triu), _dot_exact_lhs(a_t, tril))
                   for a_t in a_ts]
        us = [conv(c) for c in cs]
        xus = [u[:, 0:256] for u in us]
        bts = [u[:, 256:384].T.astype(BF16) for u in us]
        cbfs = [u[:, 384:512].astype(BF16) for u in us]
        xbfs = [xu.astype(BF16) for xu in xus]
        gmats = [_dot(cbf, bt) for cbf, bt in zip(cbfs, bts)]
        tms = []
        for dtt, acol_t in zip(dtts, acol_ts):
            tot = jnp.where(fwd_row1, acol_t[:, CHUNK - 1:CHUNK], acol_t[:, 0:1])
            w_t = dtt * jnp.exp(tot - acol_t)
            e_t = jnp.exp(acol_t)
            stacked = jnp.concatenate(
                [acol_t, w_t, e_t, jnp.zeros((CHUNK - 48, CHUNK), F32)], axis=0)
            tms.append(stacked.T)
        wes = [_dot(tm.astype(BF16), expand) for tm in tms]
        decs = []
        for tm in tms:
            tot_row = jnp.where(lane128_1 < 8, tm[CHUNK - 1:CHUNK, :], tm[0:1, :])
            tot512 = _dot_exact_lhs(jnp.broadcast_to(tot_row, (8, CHUNK)), expand_tot)
            decs.append(jnp.exp(tot512))
        ydiags = [jnp.zeros((CHUNK, 256), F32) for _ in cs]
        for i in range(4):
            in_head = (lane256 >= 64 * i) & (lane256 < 64 * i + 64)
            for j in range(GROUP):
                tm, acol_t, dtt = tms[j], acol_ts[j], dtts[j]
                arg = jnp.where(lower, tm[:, i:i + 1] - acol_t[i:i + 1, :],
                                tm[:, 8 + i:9 + i] - acol_t[8 + i:9 + i, :])
                scale = (jnp.where(lower, dtt[i:i + 1, :], 0.0)
                         + jnp.where(upper, dtt[8 + i:9 + i, :], 0.0))
                wmat = (gmats[j] * jnp.exp(arg) * scale).astype(BF16)
                ydiags[j] = jnp.where(in_head, _dot(wmat, xbfs[j]), ydiags[j])
        for j, c in enumerate(cs):
            w512 = wes[j][:, 0:512]
            e512 = wes[j][:, 512:1024]
            xdw = (jnp.concatenate([xus[j], xus[j]], axis=1) * w512).astype(BF16)
            sloc_s[c] = _dot(bts[j], xdw)
            dec_s[c] = decs[j]
            e_s[c] = e512
            cbf_s[c] = cbfs[j]
            yacc_s[pl.ds(r0s[j], CHUNK), :] = ydiags[j] + xus[j] * dsk
        return carry

    lax.fori_loop(0, nch // GROUP, phase_a, 0)

    fwd_order = list(range(nch))
    bwd_order = list(range(ncc - 1, -1, -1)) + list(range(nch - 1, ncc - 1, -1))
    for order, lo in ((fwd_order, 0), (bwd_order, 256)):
        state = jnp.zeros((SSD_STATE, 256), F32)
        for c in order:
            sin_s[c, :, lo:lo + 256] = state.astype(BF16)
            state = (state * dec_s[c, 0:1, lo:lo + 256]
                     + sloc_s[c, :, lo:lo + 256])

    def phase_c(cs, z_ref, zrows):
        yos = [_dot(cbf_s[c], sin_s[c]) for c in cs]
        vs = []
        for c, zrow, yo in zip(cs, zrows, yos):
            r0 = c * CHUNK if isinstance(c, int) else pl.multiple_of(c * CHUNK, CHUNK)
            yo = yo * e_s[c]
            y = yacc_s[pl.ds(r0, CHUNK), :] + yo[:, 0:256] + yo[:, 256:512]
            vs.append((r0, y * _silu(z_ref[pl.ds(zrow, CHUNK), :])))

        @pl.when(g == 0)
        def _():
            for r0, v in vs:
                vun_s[pl.ds(r0, CHUNK), 0:256] = v

        @pl.when(g == 1)
        def _():
            for r0, v in vs:
                vun_s[pl.ds(r0, CHUNK), 256:512] = v

    if ctx_out:
        phase_c(list(range(ncc)), zc_ref, [c * CHUNK for c in range(ncc)])

    group_c = next(n for n in (4, 2, 1) if ncl % n == 0)

    def phase_c_lat(k, carry):
        ks = [k * group_c + j for j in range(group_c)]
        phase_c([kk + ncc for kk in ks], zl_ref,
                [pl.multiple_of(kk * CHUNK, CHUNK) for kk in ks])
        return carry

    lax.fori_loop(0, ncl // group_c, phase_c_lat, 0)

    @pl.when(g == SSD_GROUPS - 1)
    def _finalize():
        ng = ng_ref[...]

        def norm_rows(r0, nrows):
            v = vun_s[pl.ds(r0, nrows), :]
            ms = jnp.mean(v * v, axis=-1, keepdims=True)
            return (v * lax.rsqrt(ms + EPS) * ng).astype(BF16)

        if ctx_out:
            oc_ref[...] = norm_rows(0, n_ctx)

        def fin(k, carry):
            r0 = pl.multiple_of(k * 256, 256)
            ol_ref[pl.ds(r0, 256), :] = norm_rows(n_ctx + r0, 256)
            return carry

        lax.fori_loop(0, n_lat // 256, fin, 0)


def _ssd_call(p_c, p_l, conv_w8, conv_b, par, dsk, ng, batch, n_ctx, n_lat, ctx_out):
    nch = (n_ctx + n_lat) // CHUNK
    t_all = n_ctx + n_lat
    in_specs = [
        pl.BlockSpec((n_ctx, 256), lambda b, g: (b, g)),
        pl.BlockSpec((n_lat, 256), lambda b, g: (b, g)),
        pl.BlockSpec((n_ctx, 128), lambda b, g: (b, 4 + g)),
        pl.BlockSpec((n_lat, 128), lambda b, g: (b, 4 + g)),
        pl.BlockSpec((n_ctx, 128), lambda b, g: (b, 6 + g)),
        pl.BlockSpec((n_lat, 128), lambda b, g: (b, 6 + g)),
        pl.BlockSpec((n_ctx, 256), lambda b, g: (b, 4 + g)),
        pl.BlockSpec((n_lat, 256), lambda b, g: (b, 4 + g)),
        pl.BlockSpec((n_ctx, 128), lambda b, g: (b, 26 + g)),
        pl.BlockSpec((n_lat, 128), lambda b, g: (b, 26 + g)),
        pl.BlockSpec((8, 256), lambda b, g: (0, g)),
        pl.BlockSpec((8, 128), lambda b, g: (0, 4 + g)),
        pl.BlockSpec((8, 128), lambda b, g: (0, 6 + g)),
        pl.BlockSpec((1, 256), lambda b, g: (0, g)),
        pl.BlockSpec((1, 128), lambda b, g: (0, 4 + g)),
        pl.BlockSpec((1, 128), lambda b, g: (0, 6 + g)),
        pl.BlockSpec((None, 32, 128), lambda b, g: (g, 0, 0)),
        pl.BlockSpec((1, 256), lambda b, g: (0, g)),
        pl.BlockSpec((1, 512), lambda b, g: (0, 0)),
    ]
    args = [p_c, p_l, p_c, p_l, p_c, p_l, p_c, p_l, p_c, p_l,
            conv_w8, conv_w8, conv_w8, conv_b, conv_b, conv_b, par, dsk, ng]
    out_specs = [pl.BlockSpec((n_lat, 512), lambda b, g: (b, 0))]
    out_shape = [jax.ShapeDtypeStruct((batch * n_lat, 512), BF16)]
    if ctx_out:
        out_specs = [pl.BlockSpec((n_ctx, 512), lambda b, g: (b, 0))] + out_specs
        out_shape = [jax.ShapeDtypeStruct((batch * n_ctx, 512), BF16)] + out_shape
    scratch = [
        pltpu.VMEM((t_all + 3 * HALO, 512), F32),
        pltpu.VMEM((t_all, 128), F32),
        pltpu.VMEM((t_all, 256), F32),
        pltpu.VMEM((nch, SSD_STATE, 512), F32),
        pltpu.VMEM((nch, CHUNK, 512), F32),
        pltpu.VMEM((nch, CHUNK, 128), BF16),
        pltpu.VMEM((nch, SSD_STATE, 512), BF16),
        pltpu.VMEM((nch, 8, 512), F32),
        pltpu.VMEM((t_all, 512), F32),
    ]
    outs = pl.pallas_call(
        functools.partial(_ssd_kernel, n_ctx=n_ctx, n_lat=n_lat, ctx_out=ctx_out),
        grid=(batch, SSD_GROUPS),
        in_specs=in_specs,
        out_specs=out_specs,
        out_shape=out_shape,
        scratch_shapes=scratch,
        compiler_params=pltpu.CompilerParams(
            dimension_semantics=("arbitrary", "arbitrary"),
            vmem_limit_bytes=VMEM_LIMIT),
        name="ssd_scan",
    )(*args)
    if ctx_out:
        return outs[1], outs[0]
    return outs[0], None


def _rope_tables(n_lat):
    rows = n_lat // GRID_W
    row_idx = jnp.repeat(jnp.arange(rows), GRID_W).astype(F32)
    col_idx = (jnp.arange(rows * GRID_W) % GRID_W).astype(F32)

    def tables(dim, reps):
        quarter = dim // 4
        inv = ROPE_BASE ** (-jnp.arange(quarter, dtype=F32) / quarter)
        ang = jnp.concatenate([row_idx[:, None] * inv, col_idx[:, None] * inv], axis=-1)
        cos, sin = jnp.cos(ang), jnp.sin(ang)
        cos2 = jnp.concatenate([cos, cos], axis=-1)
        sin2 = jnp.concatenate([-sin, sin], axis=-1)
        return jnp.tile(cos2, (1, reps)), jnp.tile(sin2, (1, reps))

    cos_g, sin_g = tables(GQA_HEAD_DIM, 4)
    cos_d, sin_d = tables(DIFF_QK_DIM, 8)
    return {"cos_g": cos_g, "sin_g": sin_g, "cos_d": cos_d, "sin_d": sin_d}


def kernel(x, c, ctx, c_ctx, w_mod, b_mod, g_pre, g_post, w_in, conv_w, conv_b,
           a_log_fwd, a_log_bwd, dt_bias_fwd, dt_bias_bwd, d_skip, ssd_norm_g,
           q_norm_g, k_norm_g, diff_lambda, diff_norm_g, w_out):
    batch, n_lat, _ = x.shape
    n_ctx = ctx.shape[1]
    depth = w_mod.shape[0]
    assert n_lat % 512 == 0 and n_ctx % 256 == 0 and (batch * n_ctx) % 512 == 0
    assert batch + 1 <= 16

    in_perm, out_perm = _in_col_perm(), _out_row_perm()
    w_in_p = [_take_runs(w_in[l], in_perm, 1, IN_COLS).astype(BF16) for l in range(depth)]
    w_out_p = [_take_runs(w_out[l], out_perm, 0, None).astype(BF16) for l in range(depth)]
    conv_w8 = jnp.pad(conv_w, ((0, 0), (0, 8 - CONV_K), (0, 0)))
    conv_b1 = conv_b[:, None, :]

    def group16(fwd, bwd):
        out = jnp.zeros((depth, SSD_GROUPS, 16), F32)
        for g in range(SSD_GROUPS):
            out = out.at[:, g, 0:4].set(fwd[:, 4 * g:4 * g + 4])
            out = out.at[:, g, 8:12].set(bwd[:, 4 * g:4 * g + 4])
        return out

    ssd_par = jnp.broadcast_to(
        jnp.concatenate([group16(a_log_fwd, a_log_bwd),
                         group16(dt_bias_fwd, dt_bias_bwd)], axis=-1)[..., None],
        (depth, SSD_GROUPS, 32, 128))
    dsk = jnp.repeat(d_skip, SSD_HEAD_DIM, axis=1)[:, None, :]
    qg = jnp.tile(q_norm_g, (1, 4))[:, None, :]
    kg = jnp.tile(k_norm_g, (1, 2))[:, None, :]
    dng = jnp.tile(diff_norm_g, (1, 4))[:, None, :]
    tabs = _rope_tables(n_lat)

    cs = jnp.concatenate(
        [c, c_ctx[None, :], jnp.zeros((16 - batch - 1, D_MODEL), F32)], axis=0)
    mod_all = _mod_call(cs, w_mod, b_mod)

    h = x.reshape(batch * n_lat, D_MODEL)
    hc = ctx.reshape(batch * n_ctx, D_MODEL)
    for l in range(depth):
        ctx_out = l < depth - 1
        lam_init = 0.8 - 0.6 * float(np.exp(-0.3 * l))
        mod3 = mod_all[l][:, None, :]
        p_l = _inproj_call(h, mod3, g_pre[l][None, :], w_in_p[l], n_lat // 512, None)
        p_c = _inproj_call(hc, mod3, g_pre[l][None, :], w_in_p[l], None, batch)

        ys_l, ys_c = _ssd_call(p_c, p_l, conv_w8[l], conv_b1[l], ssd_par[l], dsk[l],
                               ssd_norm_g[l][None, :], batch, n_ctx, n_lat, ctx_out)
        yg_l = _gqa_call(p_l, p_c, p_l, tabs, qg[l], kg[l], batch, n_ctx, n_lat, True)
        yd_l = _diff_call(p_l, p_c, p_l, tabs, diff_lambda[l], dng[l], batch, n_ctx,
                          n_lat, True, lam_init)
        h = _outproj_call(ys_l, yg_l, yd_l, h, mod3, g_post[l][None, :], w_out_p[l],
                          n_lat, None)
        if ctx_out:
            yg_c = _gqa_call(p_c, p_c, None, tabs, qg[l], kg[l], batch, n_ctx, 0, False)
            yd_c = _diff_call(p_c, p_c, None, tabs, diff_lambda[l], dng[l], batch,
                              n_ctx, 0, False, lam_init)
            hc = _outproj_call(ys_c, yg_c, yd_c, hc, mod3, g_post[l][None, :],
                               w_out_p[l], None, batch)
    return h.reshape(batch, n_lat, D_MODEL)
```

```python
import functools

import numpy as np
import jax
import jax.numpy as jnp
from jax import lax
from jax.experimental import pallas as pl
from jax.experimental.pallas import tpu as pltpu

F32 = jnp.float32
BF16 = jnp.bfloat16

D_MODEL = 1024
GRID_W = 64
ROPE_BASE = 10000.0
EPS = 1e-6
LOG2E = 1.4426950408889634

SSD_WIDTH = 512
SSD_HEADS = 8
SSD_HEAD_DIM = 64
SSD_GROUPS = 2
SSD_STATE = 128
CHUNK = 128
CONV_K = 5
HALO = 8
GQA_HEADS = 4
GQA_HEAD_DIM = 64
DIFF_HEADS = 4
DIFF_QK_DIM = 32
DIFF_V_DIM = 64

_IN_SPLITS = (("xbc", 1024), ("z", 512), ("dt", 16), ("gq", 256), ("gk", 128),
              ("gv", 128), ("gg", 256), ("dq", 256), ("dk", 256), ("dv", 256),
              ("dg", 256))
IN_COLS = sum(s for _, s in _IN_SPLITS)
NP = 28 * 128
GQ_HEAD_ORDER = (0, 2, 1, 3)

VMEM_LIMIT = 56 * 1024 * 1024
ATTN_TQ = 1024
ATTN_SUB = 512
DIFF_SUB = 512
ATTN_AHEAD = 2
DIFF_AHEAD = 1
OUTPROJ_TM = 1024
OUTPROJ_SUB = 256


def _in_col_perm():
    off, o = {}, 0
    for name, size in _IN_SPLITS:
        off[name] = o
        o += size
    pad = IN_COLS
    cols = list(range(off["xbc"], off["xbc"] + 1024))
    cols += list(range(off["z"], off["z"] + 512))
    for h in GQ_HEAD_ORDER:
        cols += list(range(off["gq"] + 64 * h, off["gq"] + 64 * h + 64))
    cols += list(range(off["gk"], off["gk"] + 128))
    cols += list(range(off["gv"], off["gv"] + 128))
    for h in GQ_HEAD_ORDER:
        cols += list(range(off["gg"] + 64 * h, off["gg"] + 64 * h + 64))
    for name in ("dq", "dk", "dv", "dg"):
        cols += list(range(off[name], off[name] + 256))
    for g in range(SSD_GROUPS):
        blk = [pad] * 128
        for i in range(4):
            blk[i] = off["dt"] + 4 * g + i
            blk[8 + i] = off["dt"] + SSD_HEADS + 4 * g + i
        cols += blk
    assert len(cols) == NP
    return np.asarray(cols, np.int32)


def _out_row_perm():
    rows = list(range(SSD_WIDTH))
    for h in GQ_HEAD_ORDER:
        rows += list(range(SSD_WIDTH + 64 * h, SSD_WIDTH + 64 * h + 64))
    rows += list(range(SSD_WIDTH + 256, SSD_WIDTH + 512))
    return np.asarray(rows, np.int32)


def _take_runs(arr, idx, axis, pad_index):
    pieces, start = [], 0
    idx = [int(i) for i in idx]
    while start < len(idx):
        end = start + 1
        if idx[start] == pad_index:
            while end < len(idx) and idx[end] == pad_index:
                end += 1
            shape = list(arr.shape)
            shape[axis] = end - start
            pieces.append(jnp.zeros(shape, arr.dtype))
        else:
            while end < len(idx) and idx[end] == idx[end - 1] + 1 and idx[end] != pad_index:
                end += 1
            pieces.append(lax.slice_in_dim(arr, idx[start], idx[end - 1] + 1, axis=axis))
        start = end
    return jnp.concatenate(pieces, axis=axis)


def _dot(a, b):
    return jnp.dot(a, b, preferred_element_type=F32)


def _split3(x):
    hi = x.astype(BF16)
    r1 = x - hi.astype(F32)
    mid = r1.astype(BF16)
    lo = (r1 - mid.astype(F32)).astype(BF16)
    return hi, mid, lo


def _dot_exact_lhs(x, m_bf16):
    hi, mid, lo = _split3(x)
    return _dot(hi, m_bf16) + _dot(mid, m_bf16) + _dot(lo, m_bf16)


def _silu(x):
    return x * jax.nn.sigmoid(x)


def _seg_ones(width, seg):
    r = lax.broadcasted_iota(jnp.int32, (width, width), 0)
    c = lax.broadcasted_iota(jnp.int32, (width, width), 1)
    same = (r & ~(seg - 1)) == (c & ~(seg - 1))
    return jnp.where(same, 1.0, 0.0).astype(BF16)


def _seg_rms(x, seg, seg_mat):
    ss = _dot_exact_lhs(x * x, seg_mat)
    return x * lax.rsqrt(ss * (1.0 / seg) + EPS)


def _rope(x, cos, sin_signed, half):
    w = x.shape[-1]
    lane = lax.broadcasted_iota(jnp.int32, x.shape, 1)
    first = (lane & (2 * half - 1)) < half
    swapped = jnp.where(first, pltpu.roll(x, w - half, 1), pltpu.roll(x, half, 1))
    return x * cos + swapped * sin_signed


def _mod_kernel(cs_ref, w_ref, b_ref, o_ref):
    s = _silu(cs_ref[...]).astype(BF16)
    o_ref[...] = _dot(s, w_ref[...].astype(BF16)) + b_ref[...]


def _mod_call(cs, w_mod, b_mod):
    depth = w_mod.shape[0]
    nrow = cs.shape[0]
    tn = 1024
    return pl.pallas_call(
        _mod_kernel,
        grid=(depth, 3 * D_MODEL // tn),
        in_specs=[
            pl.BlockSpec((nrow, D_MODEL), lambda l, j: (0, 0)),
            pl.BlockSpec((None, D_MODEL, tn), lambda l, j: (l, 0, j)),
            pl.BlockSpec((None, 1, tn), lambda l, j: (l, 0, j)),
        ],
        out_specs=pl.BlockSpec((None, nrow, tn), lambda l, j: (l, 0, j)),
        out_shape=jax.ShapeDtypeStruct((depth, nrow, 3 * D_MODEL), F32),
        compiler_params=pltpu.CompilerParams(
            dimension_semantics=("arbitrary", "arbitrary")),
        name="mod_proj",
    )(cs, w_mod, b_mod.reshape(depth, 1, 3 * D_MODEL))


def _inproj_kernel(h_ref, mod_ref, g_ref, w_ref, o_ref):
    x = h_ref[...]
    ms = jnp.mean(x * x, axis=-1, keepdims=True)
    y = x * lax.rsqrt(ms + EPS) * g_ref[...]
    sh = mod_ref[:, 0:D_MODEL]
    sc = mod_ref[:, D_MODEL:2 * D_MODEL]
    u = (y * (1.0 + sc) + sh).astype(BF16)
    tn = 512
    for j in range(NP // tn):
        o_ref[:, j * tn:(j + 1) * tn] = _dot(u, w_ref[:, j * tn:(j + 1) * tn])


def _inproj_call(h, mod3, g_pre, w_bf16, tiles_per_row, fixed_row):
    n_tok = h.shape[0]
    tm = 512
    if fixed_row is None:
        mod_idx = lambda i: (i // tiles_per_row, 0, 0)
    else:
        mod_idx = lambda i: (fixed_row, 0, 0)
    return pl.pallas_call(
        _inproj_kernel,
        grid=(n_tok // tm,),
        in_specs=[
            pl.BlockSpec((tm, D_MODEL), lambda i: (i, 0)),
            pl.BlockSpec((None, 1, 3 * D_MODEL), mod_idx),
            pl.BlockSpec((1, D_MODEL), lambda i: (0, 0)),
            pl.BlockSpec((D_MODEL, NP), lambda i: (0, 0)),
        ],
        out_specs=pl.BlockSpec((tm, NP), lambda i: (i, 0)),
        out_shape=jax.ShapeDtypeStruct((n_tok, NP), F32),
        compiler_params=pltpu.CompilerParams(
            dimension_semantics=("arbitrary",), vmem_limit_bytes=VMEM_LIMIT),
        name="in_proj",
    )(h, mod3, g_pre, w_bf16)


def _outproj_kernel(ys_ref, yg_ref, yd_ref, h_ref, mod_ref, g_ref, w_ref, o_ref):
    tm = h_ref.shape[0]
    sub = min(OUTPROJ_SUB, tm)
    gt = mod_ref[:, 2 * D_MODEL:3 * D_MODEL]
    gain = g_ref[...]

    def project(r0):
        return (_dot(ys_ref[r0:r0 + sub, :], w_ref[0:512, :])
                + _dot(yg_ref[r0:r0 + sub, :], w_ref[512:768, :])
                + _dot(yd_ref[r0:r0 + sub, :], w_ref[768:1024, :]))

    o_next = project(0)
    for r0 in range(0, tm, sub):
        o = o_next
        if r0 + sub < tm:
            o_next = project(r0 + sub)
        ms = jnp.mean(o * o, axis=-1, keepdims=True)
        n = o * lax.rsqrt(ms + EPS) * gain
        o_ref[r0:r0 + sub, :] = h_ref[r0:r0 + sub, :] + gt * n


def _outproj_call(ys, yg, yd, h, mod3, g_post, w_bf16, rows_per_mod, fixed_row):
    n_tok = h.shape[0]
    tm = OUTPROJ_TM
    assert n_tok % tm == 0
    if fixed_row is None:
        assert rows_per_mod % tm == 0
        mod_idx = lambda i: (i // (rows_per_mod // tm), 0, 0)
    else:
        mod_idx = lambda i: (fixed_row, 0, 0)
    return pl.pallas_call(
        _outproj_kernel,
        grid=(n_tok // tm,),
        in_specs=[
            pl.BlockSpec((tm, 512), lambda i: (i, 0)),
            pl.BlockSpec((tm, 256), lambda i: (i, 0)),
            pl.BlockSpec((tm, 256), lambda i: (i, 0)),
            pl.BlockSpec((tm, D_MODEL), lambda i: (i, 0)),
            pl.BlockSpec((None, 1, 3 * D_MODEL), mod_idx),
            pl.BlockSpec((1, D_MODEL), lambda i: (0, 0)),
            pl.BlockSpec((D_MODEL, D_MODEL), lambda i: (0, 0)),
        ],
        out_specs=pl.BlockSpec((tm, D_MODEL), lambda i: (i, 0)),
        out_shape=jax.ShapeDtypeStruct((n_tok, D_MODEL), F32),
        compiler_params=pltpu.CompilerParams(
            dimension_semantics=("arbitrary",), vmem_limit_bytes=VMEM_LIMIT),
        name="out_proj",
    )(ys, yg, yd, h, mod3, g_post, w_bf16)


def _attend_many(lhs_list, kt_ref, vext_refs):
    outs = []
    n = len(lhs_list)
    ahead = [_dot(lhs_list[i], kt_ref[...]) for i in range(min(ATTN_AHEAD, n))]
    for i, vext_ref in enumerate(vext_refs):
        s = ahead.pop(0)
        if i + ATTN_AHEAD < n:
            ahead.append(_dot(lhs_list[i + ATTN_AHEAD], kt_ref[...]))
        m = jnp.max(s, axis=-1, keepdims=True)
        p = jnp.exp2(s - m).astype(BF16)
        oe = _dot(p, vext_ref[...])
        outs.append(oe[:, 0:128] / oe[:, 128:256])
    return outs


def _attend_diff_pairs(lhs_list, kt_ref, v_refs, lam):
    n_heads = len(v_refs)

    def scores(h):
        return _dot(lhs_list[2 * h], kt_ref[...]), _dot(lhs_list[2 * h + 1], kt_ref[...])

    outs = []
    ahead = [scores(h) for h in range(min(DIFF_AHEAD, n_heads))]
    for h in range(n_heads):
        s_a, s_b = ahead.pop(0)
        if h + DIFF_AHEAD < n_heads:
            ahead.append(scores(h + DIFF_AHEAD))
        e_a = jnp.exp2(s_a - jnp.max(s_a, axis=-1, keepdims=True))
        e_b = jnp.exp2(s_b - jnp.max(s_b, axis=-1, keepdims=True))
        l_a = jnp.sum(e_a, axis=-1, keepdims=True)
        l_b = jnp.sum(e_b, axis=-1, keepdims=True)
        pc = (e_a - (lam * l_a / l_b) * e_b).astype(BF16)
        outs.append(_dot(pc, v_refs[h][...]) / l_a)
    return outs


def _gqa_kernel(*refs, n_ctx, n_lat, rope_q):
    it = iter(refs)
    q_ref, gg_ref, kvc_ref = next(it), next(it), next(it)
    kvl_ref = next(it) if n_lat else None
    if rope_q:
        cosq_ref, sinq_ref = next(it), next(it)
    if n_lat:
        cosk_ref, sink_ref = next(it), next(it)
    qg_ref, kg_ref = next(it), next(it)
    y_ref = next(it)
    kt_s, vext_s = next(it), next(it)

    seg128 = _seg_ones(128, 64)

    @pl.when(pl.program_id(1) == 0)
    def _prep_kv():
        kc = _seg_rms(kvc_ref[:, 0:128], 64, seg128) * kg_ref[...]
        kt_s[:, 0:n_ctx] = kc.T.astype(BF16)
        vext_s[0:n_ctx, 0:128] = kvc_ref[:, 128:256].astype(BF16)
        if n_lat:
            kl = _seg_rms(kvl_ref[:, 0:128], 64, seg128) * kg_ref[...]
            kl = _rope(kl, cosk_ref[...], sink_ref[...], 32)
            kt_s[:, n_ctx:n_ctx + n_lat] = kl.T.astype(BF16)
            vext_s[n_ctx:n_ctx + n_lat, 0:128] = kvl_ref[:, 128:256].astype(BF16)
        vext_s[:, 128:256] = jnp.ones((n_ctx + n_lat, 128), BF16)

    seg256 = _seg_ones(256, 64)
    q = _seg_rms(q_ref[...], 64, seg256) * qg_ref[...]
    if rope_q:
        q = _rope(q, cosq_ref[...], sinq_ref[...], 32)
    q = q * (GQA_HEAD_DIM ** -0.5 * LOG2E)
    tq = q.shape[0]
    sub = min(ATTN_SUB, tq)
    lane = lax.broadcasted_iota(jnp.int32, (sub, 128), 1)
    lhs_list = []
    for r0 in range(0, tq, sub):
        for half in range(2):
            qh = q[r0:r0 + sub, 128 * half:128 * half + 128]
            for kv in range(2):
                in_kv = (lane >= 64 * kv) & (lane < 64 * kv + 64)
                lhs_list.append(jnp.where(in_kv, qh, 0.0).astype(BF16))
    outs = _attend_many(lhs_list, kt_s, [vext_s] * len(lhs_list))
    for j, r0 in enumerate(range(0, tq, sub)):
        for half in range(2):
            o = jnp.where(lane < 64, outs[4 * j + 2 * half], outs[4 * j + 2 * half + 1])
            gate = _silu(gg_ref[r0:r0 + sub, 128 * half:128 * half + 128])
            y_ref[r0:r0 + sub, 128 * half:128 * half + 128] = (o * gate).astype(BF16)


def _gqa_call(p_q, p_c, p_l, tabs, qg, kg, batch, n_ctx, n_lat, rope_q):
    t_total = p_q.shape[0] // batch
    tq = min(ATTN_TQ, t_total)
    nq = t_total // tq
    in_specs = [
        pl.BlockSpec((tq, 256), lambda b, i: (b * nq + i, 6)),
        pl.BlockSpec((tq, 256), lambda b, i: (b * nq + i, 8)),
        pl.BlockSpec((n_ctx, 256), lambda b, i: (b, 7)),
    ]
    args = [p_q, p_q, p_c]
    if n_lat:
        in_specs.append(pl.BlockSpec((n_lat, 256), lambda b, i: (b, 7)))
        args.append(p_l)
    if rope_q:
        in_specs += [pl.BlockSpec((tq, 256), lambda b, i: (i, 0))] * 2
        args += [tabs["cos_g"], tabs["sin_g"]]
    if n_lat:
        in_specs += [pl.BlockSpec((n_lat, 128), lambda b, i: (0, 0))] * 2
        args += [tabs["cos_g"], tabs["sin_g"]]
    in_specs += [pl.BlockSpec((1, 256), lambda b, i: (0, 0)),
                 pl.BlockSpec((1, 128), lambda b, i: (0, 0))]
    args += [qg, kg]
    s_keys = n_ctx + n_lat
    return pl.pallas_call(
        functools.partial(_gqa_kernel, n_ctx=n_ctx, n_lat=n_lat, rope_q=rope_q),
        grid=(batch, nq),
        in_specs=in_specs,
        out_specs=pl.BlockSpec((tq, 256), lambda b, i: (b * nq + i, 0)),
        out_shape=jax.ShapeDtypeStruct((p_q.shape[0], 256), BF16),
        scratch_shapes=[pltpu.VMEM((128, s_keys), BF16),
                        pltpu.VMEM((s_keys, 256), BF16)],
        compiler_params=pltpu.CompilerParams(
            dimension_semantics=("arbitrary", "arbitrary"),
            vmem_limit_bytes=VMEM_LIMIT),
        name="gqa_attn",
    )(*args)


def _diff_kernel(*refs, n_ctx, n_lat, rope_q, lam_init):
    it = iter(refs)
    q_ref, dg_ref, kc_ref, vc_ref = next(it), next(it), next(it), next(it)
    if n_lat:
        kl_ref, vl_ref = next(it), next(it)
    if rope_q:
        cosq_ref, sinq_ref = next(it), next(it)
    if n_lat:
        cosk_ref, sink_ref = next(it), next(it)
    lam_ref, ng_ref = next(it), next(it)
    y_ref = next(it)
    kt_s, vlo_s, vhi_s = next(it), next(it), next(it)
    s_keys = n_ctx + n_lat

    @pl.when(pl.program_id(1) == 0)
    def _prep_kv():
        kt_s[:, 0:n_ctx] = kc_ref[...].T.astype(BF16)
        vlo_s[0:n_ctx, 0:128] = vc_ref[:, 0:128].astype(BF16)
        vhi_s[0:n_ctx, 0:128] = vc_ref[:, 128:256].astype(BF16)
        if n_lat:
            kl = _rope(kl_ref[...], cosk_ref[...], sink_ref[...], 16)
            kt_s[:, n_ctx:s_keys] = kl.T.astype(BF16)
            vlo_s[n_ctx:s_keys, 0:128] = vl_ref[:, 0:128].astype(BF16)
            vhi_s[n_ctx:s_keys, 0:128] = vl_ref[:, 128:256].astype(BF16)

    lp = lam_ref[...]
    lam = (jnp.exp(jnp.sum(lp[0:1, :] * lp[1:2, :], axis=-1, keepdims=True))
           - jnp.exp(jnp.sum(lp[2:3, :] * lp[3:4, :], axis=-1, keepdims=True))
           + lam_init)

    q = q_ref[...]
    if rope_q:
        q = _rope(q, cosq_ref[...], sinq_ref[...], 16)
    q = q * (DIFF_QK_DIM ** -0.5 * LOG2E)
    tq = q.shape[0]
    sub = min(DIFF_SUB, tq)
    lane256 = lax.broadcasted_iota(jnp.int32, (sub, 256), 1)
    lane128 = lax.broadcasted_iota(jnp.int32, (sub, 128), 1)
    seg128 = _seg_ones(128, 64)
    lhs_list, v_list = [], []
    for r0 in range(0, tq, sub):
        for mp in range(2 * DIFF_HEADS):
            in_map = (lane256 >= 32 * mp) & (lane256 < 32 * mp + 32)
            lhs_list.append(jnp.where(in_map, q[r0:r0 + sub, :], 0.0).astype(BF16))
        v_list += [vlo_s, vlo_s, vhi_s, vhi_s]
    heads = _attend_diff_pairs(lhs_list, kt_s, v_list, lam)
    for j, r0 in enumerate(range(0, tq, sub)):
        for half in range(2):
            o = jnp.where(lane128 < 64, heads[4 * j + 2 * half], heads[4 * j + 2 * half + 1])
            n = _seg_rms(o, 64, seg128) * ng_ref[:, 128 * half:128 * half + 128]
            n = n * (1.0 - lam_init)
            gate = _silu(dg_ref[r0:r0 + sub, 128 * half:128 * half + 128])
            y_ref[r0:r0 + sub, 128 * half:128 * half + 128] = (n * gate).astype(BF16)


def _diff_call(p_q, p_c, p_l, tabs, lam_params, ng, batch, n_ctx, n_lat, rope_q,
               lam_init):
    t_total = p_q.shape[0] // batch
    tq = min(ATTN_TQ, t_total)
    nq = t_total // tq
    in_specs = [
        pl.BlockSpec((tq, 256), lambda b, i: (b * nq + i, 9)),
        pl.BlockSpec((tq, 256), lambda b, i: (b * nq + i, 12)),
        pl.BlockSpec((n_ctx, 256), lambda b, i: (b, 10)),
        pl.BlockSpec((n_ctx, 256), lambda b, i: (b, 11)),
    ]
    args = [p_q, p_q, p_c, p_c]
    if n_lat:
        in_specs += [pl.BlockSpec((n_lat, 256), lambda b, i: (b, 10)),
                     pl.BlockSpec((n_lat, 256), lambda b, i: (b, 11))]
        args += [p_l, p_l]
    if rope_q:
        in_specs += [pl.BlockSpec((tq, 256), lambda b, i: (i, 0))] * 2
        args += [tabs["cos_d"], tabs["sin_d"]]
    if n_lat:
        in_specs += [pl.BlockSpec((n_lat, 256), lambda b, i: (0, 0))] * 2
        args += [tabs["cos_d"], tabs["sin_d"]]
    in_specs += [pl.BlockSpec((4, DIFF_QK_DIM), lambda b, i: (0, 0)),
                 pl.BlockSpec((1, 256), lambda b, i: (0, 0))]
    args += [lam_params, ng]
    s_keys = n_ctx + n_lat
    return pl.pallas_call(
        functools.partial(_diff_kernel, n_ctx=n_ctx, n_lat=n_lat, rope_q=rope_q,
                          lam_init=lam_init),
        grid=(batch, nq),
        in_specs=in_specs,
        out_specs=pl.BlockSpec((tq, 256), lambda b, i: (b * nq + i, 0)),
        out_shape=jax.ShapeDtypeStruct((p_q.shape[0], 256), BF16),
        scratch_shapes=[pltpu.VMEM((256, s_keys), BF16),
                        pltpu.VMEM((s_keys, 128), BF16),
                        pltpu.VMEM((s_keys, 128), BF16)],
        compiler_params=pltpu.CompilerParams(
            dimension_semantics=("arbitrary", "arbitrary"),
            vmem_limit_bytes=VMEM_LIMIT),
        name="diff_attn",
    )(*args)


def _ssd_kernel(xc_ref, xl_ref, bc_ref, bl_ref, cc_ref, cl_ref, zc_ref, zl_ref,
                dtc_ref, dtl_ref, cwx_ref, cwb_ref, cwc_ref, cbx_ref, cbb_ref,
                cbc_ref, par_ref, dsk_ref, ng_ref, *rest, n_ctx, n_lat, ctx_out):
    if ctx_out:
        oc_ref, ol_ref = rest[0], rest[1]
        rest = rest[2:]
    else:
        oc_ref, ol_ref = None, rest[0]
        rest = rest[1:]
    xp_s, dtr_s, yacc_s, sloc_s, e_s, cbf_s, sin_s, dec_s, vun_s = rest

    g = pl.program_id(1)
    ncc = n_ctx // CHUNK
    ncl = n_lat // CHUNK
    nch = ncc + ncl
    GROUP = next(n for n in (6, 3, 2, 1) if nch % n == 0)
    t_all = n_ctx + n_lat
    lat0 = n_ctx + 2 * HALO

    zeros_h = jnp.zeros((HALO, 512), F32)
    xp_s[0:HALO, :] = zeros_h
    xp_s[HALO:HALO + n_ctx, 0:256] = xc_ref[...]
    xp_s[HALO:HALO + n_ctx, 256:384] = bc_ref[...]
    xp_s[HALO:HALO + n_ctx, 384:512] = cc_ref[...]
    xp_s[HALO + n_ctx:lat0, :] = zeros_h
    xp_s[lat0:lat0 + n_lat, 0:256] = xl_ref[...]
    xp_s[lat0:lat0 + n_lat, 256:384] = bl_ref[...]
    xp_s[lat0:lat0 + n_lat, 384:512] = cl_ref[...]
    xp_s[lat0 + n_lat:lat0 + n_lat + HALO, :] = zeros_h

    dtr_s[0:n_ctx, :] = dtc_ref[...]
    dtr_s[n_ctx:t_all, :] = dtl_ref[...]
    a_col = -jnp.exp(par_ref[0:16, :])
    bias_col = par_ref[16:32, :]

    def _softplus(v):
        return jnp.maximum(v, 0.0) + jnp.log1p(jnp.exp(-jnp.abs(v)))

    r128 = lax.broadcasted_iota(jnp.int32, (CHUNK, CHUNK), 0)
    c128 = lax.broadcasted_iota(jnp.int32, (CHUNK, CHUNK), 1)
    lower = c128 <= r128
    upper = c128 >= r128
    tril = jnp.where(lower, 1.0, 0.0).astype(BF16)
    triu = jnp.where(upper, 1.0, 0.0).astype(BF16)
    fwd_row = lax.broadcasted_iota(jnp.int32, (16, CHUNK), 0) < 8
    fwd_row1 = lax.broadcasted_iota(jnp.int32, (16, 1), 0) < 8
    er = lax.broadcasted_iota(jnp.int32, (CHUNK, 1024), 0)
    ec = lax.broadcasted_iota(jnp.int32, (CHUNK, 1024), 1)
    src_lane = 16 + 16 * (ec >> 9) + 8 * ((ec >> 8) & 1) + ((ec >> 6) & 3)
    expand = jnp.where(er == src_lane, 1.0, 0.0).astype(BF16)
    er0 = lax.broadcasted_iota(jnp.int32, (CHUNK, 512), 0)
    ec0 = lax.broadcasted_iota(jnp.int32, (CHUNK, 512), 1)
    expand_tot = jnp.where(er0 == 8 * (ec0 >> 8) + ((ec0 >> 6) & 3), 1.0, 0.0).astype(BF16)
    lane256 = lax.broadcasted_iota(jnp.int32, (CHUNK, 256), 1)
    lane128_1 = lax.broadcasted_iota(jnp.int32, (1, CHUNK), 1)

    cw = jnp.concatenate([cwx_ref[...], cwb_ref[...], cwc_ref[...]], axis=1)
    cb = jnp.concatenate([cbx_ref[...], cbb_ref[...], cbc_ref[...]], axis=1)
    dsk = dsk_ref[...]

    def phase_a(grp, carry):
        cs = [grp * GROUP + j for j in range(GROUP)]
        r0s = [pl.multiple_of(c * CHUNK, CHUNK) for c in cs]

        def conv(c):
            wstart = pl.multiple_of(c * CHUNK + jnp.where(c >= ncc, HALO, 0), 8)
            win = xp_s[pl.ds(wstart, CHUNK + 2 * HALO), :]
            acc = jnp.broadcast_to(cb, (CHUNK, 512))
            for k in range(CONV_K):
                d = k - CONV_K // 2
                if d == 0:
                    tap = win[HALO:HALO + CHUNK, :]
                else:
                    tap = pltpu.roll(win, (-d) % (CHUNK + 2 * HALO), 0)[HALO:HALO + CHUNK, :]
                acc = acc + cw[k:k + 1, :] * tap
            return _silu(acc)

        dtts = [_softplus(dtr_s[pl.ds(r0, CHUNK), :].T[0:16, :] + bias_col) for r0 in r0s]
        a_ts = [dtt * a_col for dtt in dtts]
        acol_ts = [jnp.where(fwd_row, _dot_exact_lhs(a_t, triu), _dot_exact_lhs(a_t, tril))
                   for a_t in a_ts]
        us = [conv(c) for c in cs]
        xus = [u[:, 0:256] for u in us]
        bts = [u[:, 256:384].T.astype(BF16) for u in us]
        cbfs = [u[:, 384:512].astype(BF16) for u in us]
        xbfs = [xu.astype(BF16) for xu in xus]
        gmats = [_dot(cbf, bt) for cbf, bt in zip(cbfs, bts)]
        tms = []
        for dtt, acol_t in zip(dtts, acol_ts):
            tot = jnp.where(fwd_row1, acol_t[:, CHUNK - 1:CHUNK], acol_t[:, 0:1])
            w_t = dtt * jnp.exp(tot - acol_t)
            e_t = jnp.exp(acol_t)
            stacked = jnp.concatenate(
                [acol_t, w_t, e_t, jnp.zeros((CHUNK - 48, CHUNK), F32)], axis=0)
            tms.append(stacked.T)
        wes = [_dot(tm.astype(BF16), expand) for tm in tms]
        decs = []
        for tm in tms:
            tot_row = jnp.where(lane128_1 < 8, tm[CHUNK - 1:CHUNK, :], tm[0:1, :])
            tot512 = _dot_exact_lhs(jnp.broadcast_to(tot_row, (8, CHUNK)), expand_tot)
            decs.append(jnp.exp(tot512))
        ydiags = [jnp.zeros((CHUNK, 256), F32) for _ in cs]
        for i in range(4):
            in_head = (lane256 >= 64 * i) & (lane256 < 64 * i + 64)
            for j in range(GROUP):
                tm, acol_t, dtt = tms[j], acol_ts[j], dtts[j]
                arg = jnp.where(lower, tm[:, i:i + 1] - acol_t[i:i + 1, :],
                                tm[:, 8 + i:9 + i] - acol_t[8 + i:9 + i, :])
                scale = (jnp.where(lower, dtt[i:i + 1, :], 0.0)
                         + jnp.where(upper, dtt[8 + i:9 + i, :], 0.0))
                wmat = (gmats[j] * jnp.exp(arg) * scale).astype(BF16)
                ydiags[j] = jnp.where(in_head, _dot(wmat, xbfs[j]), ydiags[j])
        for j, c in enumerate(cs):
            w512 = wes[j][:, 0:512]
            e512 = wes[j][:, 512:1024]
            xdw = (jnp.concatenate([xus[j], xus[j]], axis=1) * w512).astype(BF16)
            sloc_s[c] = _dot(bts[j], xdw)
            dec_s[c] = decs[j]
            e_s[c] = e512
            cbf_s[c] = cbfs[j]
            yacc_s[pl.ds(r0s[j], CHUNK), :] = ydiags[j] + xus[j] * dsk
        return carry

    lax.fori_loop(0, nch // GROUP, phase_a, 0)

    fwd_order = list(range(nch))
    bwd_order = list(range(ncc - 1, -1, -1)) + list(range(nch - 1, ncc - 1, -1))
    for order, lo in ((fwd_order, 0), (bwd_order, 256)):
        state = jnp.zeros((SSD_STATE, 256), F32)
        for c in order:
            sin_s[c, :, lo:lo + 256] = state.astype(BF16)
            state = (state * dec_s[c, 0:1, lo:lo + 256]
                     + sloc_s[c, :, lo:lo + 256])

    def phase_c(cs, z_ref, zrows):
        yos = [_dot(cbf_s[c], sin_s[c]) for c in cs]
        vs = []
        for c, zrow, yo in zip(cs, zrows, yos):
            r0 = c * CHUNK if isinstance(c, int) else pl.multiple_of(c * CHUNK, CHUNK)
            yo = yo * e_s[c]
            y = yacc_s[pl.ds(r0, CHUNK), :] + yo[:, 0:256] + yo[:, 256:512]
            vs.append((r0, y * _silu(z_ref[pl.ds(zrow, CHUNK), :])))

        @pl.when(g == 0)
        def _():
            for r0, v in vs:
                vun_s[pl.ds(r0, CHUNK), 0:256] = v

        @pl.when(g == 1)
        def _():
            for r0, v in vs:
                vun_s[pl.ds(r0, CHUNK), 256:512] = v

    if ctx_out:
        phase_c(list(range(ncc)), zc_ref, [c * CHUNK for c in range(ncc)])

    group_c = next(n for n in (4, 2, 1) if ncl % n == 0)

    def phase_c_lat(k, carry):
        ks = [k * group_c + j for j in range(group_c)]
        phase_c([kk + ncc for kk in ks], zl_ref,
                [pl.multiple_of(kk * CHUNK, CHUNK) for kk in ks])
        return carry

    lax.fori_loop(0, ncl // group_c, phase_c_lat, 0)

    @pl.when(g == SSD_GROUPS - 1)
    def _finalize():
        ng = ng_ref[...]

        def norm_rows(r0, nrows):
            v = vun_s[pl.ds(r0, nrows), :]
            ms = jnp.mean(v * v, axis=-1, keepdims=True)
            return (v * lax.rsqrt(ms + EPS) * ng).astype(BF16)

        if ctx_out:
            oc_ref[...] = norm_rows(0, n_ctx)

        def fin(k, carry):
            r0 = pl.multiple_of(k * 256, 256)
            ol_ref[pl.ds(r0, 256), :] = norm_rows(n_ctx + r0, 256)
            return carry

        lax.fori_loop(0, n_lat // 256, fin, 0)


def _ssd_call(p_c, p_l, conv_w8, conv_b, par, dsk, ng, batch, n_ctx, n_lat, ctx_out):
    nch = (n_ctx + n_lat) // CHUNK
    t_all = n_ctx + n_lat
    in_specs = [
        pl.BlockSpec((n_ctx, 256), lambda b, g: (b, g)),
        pl.BlockSpec((n_lat, 256), lambda b, g: (b, g)),
        pl.BlockSpec((n_ctx, 128), lambda b, g: (b, 4 + g)),
        pl.BlockSpec((n_lat, 128), lambda b, g: (b, 4 + g)),
        pl.BlockSpec((n_ctx, 128), lambda b, g: (b, 6 + g)),
        pl.BlockSpec((n_lat, 128), lambda b, g: (b, 6 + g)),
        pl.BlockSpec((n_ctx, 256), lambda b, g: (b, 4 + g)),
        pl.BlockSpec((n_lat, 256), lambda b, g: (b, 4 + g)),
        pl.BlockSpec((n_ctx, 128), lambda b, g: (b, 26 + g)),
        pl.BlockSpec((n_lat, 128), lambda b, g: (b, 26 + g)),
        pl.BlockSpec((8, 256), lambda b, g: (0, g)),
        pl.BlockSpec((8, 128), lambda b, g: (0, 4 + g)),
        pl.BlockSpec((8, 128), lambda b, g: (0, 6 + g)),
        pl.BlockSpec((1, 256), lambda b, g: (0, g)),
        pl.BlockSpec((1, 128), lambda b, g: (0, 4 + g)),
        pl.BlockSpec((1, 128), lambda b, g: (0, 6 + g)),
        pl.BlockSpec((None, 32, 128), lambda b, g: (g, 0, 0)),
        pl.BlockSpec((1, 256), lambda b, g: (0, g)),
        pl.BlockSpec((1, 512), lambda b, g: (0, 0)),
    ]
    args = [p_c, p_l, p_c, p_l, p_c, p_l, p_c, p_l, p_c, p_l,
            conv_w8, conv_w8, conv_w8, conv_b, conv_b, conv_b, par, dsk, ng]
    out_specs = [pl.BlockSpec((n_lat, 512), lambda b, g: (b, 0))]
    out_shape = [jax.ShapeDtypeStruct((batch * n_lat, 512), BF16)]
    if ctx_out:
        out_specs = [pl.BlockSpec((n_ctx, 512), lambda b, g: (b, 0))] + out_specs
        out_shape = [jax.ShapeDtypeStruct((batch * n_ctx, 512), BF16)] + out_shape
    scratch = [
        pltpu.VMEM((t_all + 3 * HALO, 512), F32),
        pltpu.VMEM((t_all, 128), F32),
        pltpu.VMEM((t_all, 256), F32),
        pltpu.VMEM((nch, SSD_STATE, 512), F32),
        pltpu.VMEM((nch, CHUNK, 512), F32),
        pltpu.VMEM((nch, CHUNK, 128), BF16),
        pltpu.VMEM((nch, SSD_STATE, 512), BF16),
        pltpu.VMEM((nch, 8, 512), F32),
        pltpu.VMEM((t_all, 512), F32),
    ]
    outs = pl.pallas_call(
        functools.partial(_ssd_kernel, n_ctx=n_ctx, n_lat=n_lat, ctx_out=ctx_out),
        grid=(batch, SSD_GROUPS),
        in_specs=in_specs,
        out_specs=out_specs,
        out_shape=out_shape,
        scratch_shapes=scratch,
        compiler_params=pltpu.CompilerParams(
            dimension_semantics=("arbitrary", "arbitrary"),
            vmem_limit_bytes=VMEM_LIMIT),
        name="ssd_scan",
    )(*args)
    if ctx_out:
        return outs[1], outs[0]
    return outs[0], None


def _rope_tables(n_lat):
    rows = n_lat // GRID_W
    row_idx = np.repeat(np.arange(rows), GRID_W).astype(np.float32)
    col_idx = (np.arange(rows * GRID_W) % GRID_W).astype(np.float32)

    def tables(dim, reps):
        quarter = dim // 4
        inv = (ROPE_BASE ** (-np.arange(quarter, dtype=np.float32) / quarter)).astype(np.float32)
        ang = np.concatenate([row_idx[:, None] * inv, col_idx[:, None] * inv], axis=-1)
        cos, sin = np.cos(ang.astype(np.float64)), np.sin(ang.astype(np.float64))
        cos2 = np.concatenate([cos, cos], axis=-1).astype(np.float32)
        sin2 = np.concatenate([-sin, sin], axis=-1).astype(np.float32)
        return jnp.asarray(np.tile(cos2, (1, reps))), jnp.asarray(np.tile(sin2, (1, reps)))

    cos_g, sin_g = tables(GQA_HEAD_DIM, 4)
    cos_d, sin_d = tables(DIFF_QK_DIM, 8)
    return {"cos_g": cos_g, "sin_g": sin_g, "cos_d": cos_d, "sin_d": sin_d}


def kernel(x, c, ctx, c_ctx, w_mod, b_mod, g_pre, g_post, w_in, conv_w, conv_b,
           a_log_fwd, a_log_bwd, dt_bias_fwd, dt_bias_bwd, d_skip, ssd_norm_g,
           q_norm_g, k_norm_g, diff_lambda, diff_norm_g, w_out):
    batch, n_lat, _ = x.shape
    n_ctx = ctx.shape[1]
    depth = w_mod.shape[0]
    assert n_lat % 512 == 0 and n_ctx % 256 == 0 and (batch * n_ctx) % 512 == 0
    assert batch + 1 <= 16

    in_perm, out_perm = _in_col_perm(), _out_row_perm()
    w_in_p = [_take_runs(w_in[l], in_perm, 1, IN_COLS).astype(BF16) for l in range(depth)]
    w_out_p = [_take_runs(w_out[l], out_perm, 0, None).astype(BF16) for l in range(depth)]
    conv_w8 = jnp.pad(conv_w, ((0, 0), (0, 8 - CONV_K), (0, 0)))
    conv_b1 = conv_b[:, None, :]

    def group16(fwd, bwd):
        out = jnp.zeros((depth, SSD_GROUPS, 16), F32)
        for g in range(SSD_GROUPS):
            out = out.at[:, g, 0:4].set(fwd[:, 4 * g:4 * g + 4])
            out = out.at[:, g, 8:12].set(bwd[:, 4 * g:4 * g + 4])
        return out

    ssd_par = jnp.broadcast_to(
        jnp.concatenate([group16(a_log_fwd, a_log_bwd),
                         group16(dt_bias_fwd, dt_bias_bwd)], axis=-1)[..., None],
        (depth, SSD_GROUPS, 32, 128))
    dsk = jnp.repeat(d_skip, SSD_HEAD_DIM, axis=1)[:, None, :]
    qg = jnp.tile(q_norm_g, (1, 4))[:, None, :]
    kg = jnp.tile(k_norm_g, (1, 2))[:, None, :]
    dng = jnp.tile(diff_norm_g, (1, 4))[:, None, :]
    tabs = _rope_tables(n_lat)

    cs = jnp.concatenate(
        [c, c_ctx[None, :], jnp.zeros((16 - batch - 1, D_MODEL), F32)], axis=0)
    mod_all = _mod_call(cs, w_mod, b_mod)

    h = x.reshape(batch * n_lat, D_MODEL)
    hc = ctx.reshape(batch * n_ctx, D_MODEL)
    for l in range(depth):
        ctx_out = l < depth - 1
        lam_init = 0.8 - 0.6 * float(np.exp(-0.3 * l))
        mod3 = mod_all[l][:, None, :]
        p_l = _inproj_call(h, mod3, g_pre[l][None, :], w_in_p[l], n_lat // 512, None)
        p_c = _inproj_call(hc, mod3, g_pre[l][None, :], w_in_p[l], None, batch)

        ys_l, ys_c = _ssd_call(p_c, p_l, conv_w8[l], conv_b1[l], ssd_par[l], dsk[l],
                               ssd_norm_g[l][None, :], batch, n_ctx, n_lat, ctx_out)
        yg_l = _gqa_call(p_l, p_c, p_l, tabs, qg[l], kg[l], batch, n_ctx, n_lat, True)
        yd_l = _diff_call(p_l, p_c, p_l, tabs, diff_lambda[l], dng[l], batch, n_ctx,
                          n_lat, True, lam_init)
        h = _outproj_call(ys_l, yg_l, yd_l, h, mod3, g_post[l][None, :], w_out_p[l],
                          n_lat, None)
        if ctx_out:
            yg_c = _gqa_call(p_c, p_c, None, tabs, qg[l], kg[l], batch, n_ctx, 0, False)
            yd_c = _diff_call(p_c, p_c, None, tabs, diff_lambda[l], dng[l], batch,
                              n_ctx, 0, False, lam_init)
            hc = _outproj_call(ys_c, yg_c, yd_c, hc, mod3, g_post[l][None, :],
                               w_out_p[l], None, batch)
    return h.reshape(batch, n_lat, D_MODEL)
```

```python
import functools

import numpy as np
import jax
import jax.numpy as jnp
from jax import lax
from jax.experimental import pallas as pl
from jax.experimental.pallas import tpu as pltpu

F32 = jnp.float32
BF16 = jnp.bfloat16

D_MODEL = 1024
GRID_W = 64
ROPE_BASE = 10000.0
EPS = 1e-6
LOG2E = 1.4426950408889634

SSD_WIDTH = 512
SSD_HEADS = 8
SSD_HEAD_DIM = 64
SSD_GROUPS = 2
SSD_STATE = 128
CHUNK = 128
CONV_K = 5
HALO = 8
GQA_HEADS = 4
GQA_HEAD_DIM = 64
DIFF_HEADS = 4
DIFF_QK_DIM = 32
DIFF_V_DIM = 64

_IN_SPLITS = (("xbc", 1024), ("z", 512), ("dt", 16), ("gq", 256), ("gk", 128),
              ("gv", 128), ("gg", 256), ("dq", 256), ("dk", 256), ("dv", 256),
              ("dg", 256))
IN_COLS = sum(s for _, s in _IN_SPLITS)
NP = 28 * 128
GQ_HEAD_ORDER = (0, 2, 1, 3)

VMEM_LIMIT = 56 * 1024 * 1024
ATTN_TQ = 1024
ATTN_SUB = 512
DIFF_SUB = 512
ATTN_AHEAD = 2
DIFF_AHEAD = 1
OUTPROJ_TM = 1024
OUTPROJ_SUB = 256


def _in_col_perm():
    off, o = {}, 0
    for name, size in _IN_SPLITS:
        off[name] = o
        o += size
    pad = IN_COLS
    cols = list(range(off["xbc"], off["xbc"] + 1024))
    cols += list(range(off["z"], off["z"] + 512))
    for h in GQ_HEAD_ORDER:
        cols += list(range(off["gq"] + 64 * h, off["gq"] + 64 * h + 64))
    cols += list(range(off["gk"], off["gk"] + 128))
    cols += list(range(off["gv"], off["gv"] + 128))
    for h in GQ_HEAD_ORDER:
        cols += list(range(off["gg"] + 64 * h, off["gg"] + 64 * h + 64))
    for name in ("dq", "dk", "dv", "dg"):
        cols += list(range(off[name], off[name] + 256))
    for g in range(SSD_GROUPS):
        blk = [pad] * 128
        for i in range(4):
            blk[i] = off["dt"] + 4 * g + i
            blk[8 + i] = off["dt"] + SSD_HEADS + 4 * g + i
        cols += blk
    assert len(cols) == NP
    return np.asarray(cols, np.int32)


def _out_row_perm():
    rows = list(range(SSD_WIDTH))
    for h in GQ_HEAD_ORDER:
        rows += list(range(SSD_WIDTH + 64 * h, SSD_WIDTH + 64 * h + 64))
    rows += list(range(SSD_WIDTH + 256, SSD_WIDTH + 512))
    return np.asarray(rows, np.int32)


def _take_runs(arr, idx, axis, pad_index):
    pieces, start = [], 0
    idx = [int(i) for i in idx]
    while start < len(idx):
        end = start + 1
        if idx[start] == pad_index:
            while end < len(idx) and idx[end] == pad_index:
                end += 1
            shape = list(arr.shape)
            shape[axis] = end - start
            pieces.append(jnp.zeros(shape, arr.dtype))
        else:
            while end < len(idx) and idx[end] == idx[end - 1] + 1 and idx[end] != pad_index:
                end += 1
            pieces.append(lax.slice_in_dim(arr, idx[start], idx[end - 1] + 1, axis=axis))
        start = end
    return jnp.concatenate(pieces, axis=axis)


def _dot(a, b):
    return jnp.dot(a, b, preferred_element_type=F32)


def _split3(x):
    hi = x.astype(BF16)
    r1 = x - hi.astype(F32)
    mid = r1.astype(BF16)
    lo = (r1 - mid.astype(F32)).astype(BF16)
    return hi, mid, lo


def _dot_exact_lhs(x, m_bf16):
    hi, mid, lo = _split3(x)
    return _dot(hi, m_bf16) + _dot(mid, m_bf16) + _dot(lo, m_bf16)


def _silu(x):
    return x * jax.nn.sigmoid(x)


def _seg_ones(width, seg):
    r = lax.broadcasted_iota(jnp.int32, (width, width), 0)
    c = lax.broadcasted_iota(jnp.int32, (width, width), 1)
    same = (r & ~(seg - 1)) == (c & ~(seg - 1))
    return jnp.where(same, 1.0, 0.0).astype(BF16)


def _seg_rms(x, seg, seg_mat):
    ss = _dot_exact_lhs(x * x, seg_mat)
    return x * lax.rsqrt(ss * (1.0 / seg) + EPS)


def _rope(x, cos, sin_signed, half):
    w = x.shape[-1]
    lane = lax.broadcasted_iota(jnp.int32, x.shape, 1)
    first = (lane & (2 * half - 1)) < half
    swapped = jnp.where(first, pltpu.roll(x, w - half, 1), pltpu.roll(x, half, 1))
    return x * cos + swapped * sin_signed


def _mod_kernel(cs_ref, w_ref, b_ref, o_ref):
    s = _silu(cs_ref[...]).astype(BF16)
    o_ref[...] = _dot(s, w_ref[...].astype(BF16)) + b_ref[...]


def _mod_call(cs, w_mod, b_mod):
    depth = w_mod.shape[0]
    nrow = cs.shape[0]
    tn = 1024
    return pl.pallas_call(
        _mod_kernel,
        grid=(depth, 3 * D_MODEL // tn),
        in_specs=[
            pl.BlockSpec((nrow, D_MODEL), lambda l, j: (0, 0)),
            pl.BlockSpec((None, D_MODEL, tn), lambda l, j: (l, 0, j)),
            pl.BlockSpec((None, 1, tn), lambda l, j: (l, 0, j)),
        ],
        out_specs=pl.BlockSpec((None, nrow, tn), lambda l, j: (l, 0, j)),
        out_shape=jax.ShapeDtypeStruct((depth, nrow, 3 * D_MODEL), F32),
        compiler_params=pltpu.CompilerParams(
            dimension_semantics=("arbitrary", "arbitrary")),
        name="mod_proj",
    )(cs, w_mod, b_mod.reshape(depth, 1, 3 * D_MODEL))


def _inproj_kernel(h_ref, mod_ref, g_ref, w_ref, o_ref):
    x = h_ref[...]
    ms = jnp.mean(x * x, axis=-1, keepdims=True)
    y = x * lax.rsqrt(ms + EPS) * g_ref[...]
    sh = mod_ref[:, 0:D_MODEL]
    sc = mod_ref[:, D_MODEL:2 * D_MODEL]
    u = (y * (1.0 + sc) + sh).astype(BF16)
    tn = 512
    for j in range(NP // tn):
        o_ref[:, j * tn:(j + 1) * tn] = _dot(u, w_ref[:, j * tn:(j + 1) * tn])


def _inproj_call(h, mod3, g_pre, w_bf16, tiles_per_row, fixed_row):
    n_tok = h.shape[0]
    tm = 512
    if fixed_row is None:
        mod_idx = lambda i: (i // tiles_per_row, 0, 0)
    else:
        mod_idx = lambda i: (fixed_row, 0, 0)
    return pl.pallas_call(
        _inproj_kernel,
        grid=(n_tok // tm,),
        in_specs=[
            pl.BlockSpec((tm, D_MODEL), lambda i: (i, 0)),
            pl.BlockSpec((None, 1, 3 * D_MODEL), mod_idx),
            pl.BlockSpec((1, D_MODEL), lambda i: (0, 0)),
            pl.BlockSpec((D_MODEL, NP), lambda i: (0, 0)),
        ],
        out_specs=pl.BlockSpec((tm, NP), lambda i: (i, 0)),
        out_shape=jax.ShapeDtypeStruct((n_tok, NP), F32),
        compiler_params=pltpu.CompilerParams(
            dimension_semantics=("arbitrary",), vmem_limit_bytes=VMEM_LIMIT),
        name="in_proj",
    )(h, mod3, g_pre, w_bf16)


def _outproj_kernel(ys_ref, yg_ref, yd_ref, h_ref, mod_ref, g_ref, w_ref, o_ref):
    tm = h_ref.shape[0]
    sub = min(OUTPROJ_SUB, tm)
    gt = mod_ref[:, 2 * D_MODEL:3 * D_MODEL]
    gain = g_ref[...]

    def project(r0):
        return (_dot(ys_ref[r0:r0 + sub, :], w_ref[0:512, :])
                + _dot(yg_ref[r0:r0 + sub, :], w_ref[512:768, :])
                + _dot(yd_ref[r0:r0 + sub, :], w_ref[768:1024, :]))

    o_next = project(0)
    for r0 in range(0, tm, sub):
        o = o_next
        if r0 + sub < tm:
            o_next = project(r0 + sub)
        ms = jnp.mean(o * o, axis=-1, keepdims=True)
        n = o * lax.rsqrt(ms + EPS) * gain
        o_ref[r0:r0 + sub, :] = h_ref[r0:r0 + sub, :] + gt * n


def _outproj_call(ys, yg, yd, h, mod3, g_post, w_bf16, rows_per_mod, fixed_row):
    n_tok = h.shape[0]
    tm = OUTPROJ_TM
    assert n_tok % tm == 0
    if fixed_row is None:
        assert rows_per_mod % tm == 0
        mod_idx = lambda i: (i // (rows_per_mod // tm), 0, 0)
    else:
        mod_idx = lambda i: (fixed_row, 0, 0)
    return pl.pallas_call(
        _outproj_kernel,
        grid=(n_tok // tm,),
        in_specs=[
            pl.BlockSpec((tm, 512), lambda i: (i, 0)),
            pl.BlockSpec((tm, 256), lambda i: (i, 0)),
            pl.BlockSpec((tm, 256), lambda i: (i, 0)),
            pl.BlockSpec((tm, D_MODEL), lambda i: (i, 0)),
            pl.BlockSpec((None, 1, 3 * D_MODEL), mod_idx),
            pl.BlockSpec((1, D_MODEL), lambda i: (0, 0)),
            pl.BlockSpec((D_MODEL, D_MODEL), lambda i: (0, 0)),
        ],
        out_specs=pl.BlockSpec((tm, D_MODEL), lambda i: (i, 0)),
        out_shape=jax.ShapeDtypeStruct((n_tok, D_MODEL), F32),
        compiler_params=pltpu.CompilerParams(
            dimension_semantics=("arbitrary",), vmem_limit_bytes=VMEM_LIMIT),
        name="out_proj",
    )(ys, yg, yd, h, mod3, g_post, w_bf16)


def _attend_many(lhs_list, kt_ref, vext_refs):
    def scores(i):
        s = _dot(lhs_list[i], kt_ref[...])
        return s, jnp.max(s, axis=-1, keepdims=True)

    outs = []
    n = len(lhs_list)
    ahead = [scores(i) for i in range(min(ATTN_AHEAD, n))]
    for i, vext_ref in enumerate(vext_refs):
        s, m = ahead.pop(0)
        if i + ATTN_AHEAD < n:
            ahead.append(scores(i + ATTN_AHEAD))
        p = jnp.exp2(s - m).astype(BF16)
        oe = _dot(p, vext_ref[...])
        outs.append(oe[:, 0:128] / oe[:, 128:256])
    return outs


def _attend_diff_pairs(lhs_list, kt_ref, v_refs, lam):
    n_heads = len(v_refs)

    def scores(h):
        s_a = _dot(lhs_list[2 * h], kt_ref[...])
        m_a = jnp.max(s_a, axis=-1, keepdims=True)
        s_b = _dot(lhs_list[2 * h + 1], kt_ref[...])
        m_b = jnp.max(s_b, axis=-1, keepdims=True)
        return s_a, m_a, s_b, m_b

    outs = []
    ahead = [scores(h) for h in range(min(DIFF_AHEAD, n_heads))]
    for h in range(n_heads):
        s_a, m_a, s_b, m_b = ahead.pop(0)
        if h + DIFF_AHEAD < n_heads:
            ahead.append(scores(h + DIFF_AHEAD))
        e_a = jnp.exp2(s_a - m_a)
        e_b = jnp.exp2(s_b - m_b)
        l_a = jnp.sum(e_a, axis=-1, keepdims=True)
        l_b = jnp.sum(e_b, axis=-1, keepdims=True)
        pc = (e_a - (lam * l_a / l_b) * e_b).astype(BF16)
        outs.append(_dot(pc, v_refs[h][...]) / l_a)
    return outs


def _gqa_kernel(*refs, n_ctx, n_lat, rope_q):
    it = iter(refs)
    q_ref, gg_ref, kvc_ref = next(it), next(it), next(it)
    kvl_ref = next(it) if n_lat else None
    if rope_q:
        cosq_ref, sinq_ref = next(it), next(it)
    if n_lat:
        cosk_ref, sink_ref = next(it), next(it)
    qg_ref, kg_ref = next(it), next(it)
    y_ref = next(it)
    kt_s, vext_s = next(it), next(it)

    seg128 = _seg_ones(128, 64)

    @pl.when(pl.program_id(1) == 0)
    def _prep_kv():
        kc = _seg_rms(kvc_ref[:, 0:128], 64, seg128) * kg_ref[...]
        kt_s[:, 0:n_ctx] = kc.T.astype(BF16)
        vext_s[0:n_ctx, 0:128] = kvc_ref[:, 128:256].astype(BF16)
        if n_lat:
            kl = _seg_rms(kvl_ref[:, 0:128], 64, seg128) * kg_ref[...]
            kl = _rope(kl, cosk_ref[...], sink_ref[...], 32)
            kt_s[:, n_ctx:n_ctx + n_lat] = kl.T.astype(BF16)
            vext_s[n_ctx:n_ctx + n_lat, 0:128] = kvl_ref[:, 128:256].astype(BF16)
        vext_s[:, 128:256] = jnp.ones((n_ctx + n_lat, 128), BF16)

    seg256 = _seg_ones(256, 64)
    q = _seg_rms(q_ref[...], 64, seg256) * qg_ref[...]
    if rope_q:
        q = _rope(q, cosq_ref[...], sinq_ref[...], 32)
    q = q * (GQA_HEAD_DIM ** -0.5 * LOG2E)
    tq = q.shape[0]
    sub = min(ATTN_SUB, tq)
    lane = lax.broadcasted_iota(jnp.int32, (sub, 128), 1)
    lhs_list = []
    for r0 in range(0, tq, sub):
        for half in range(2):
            qh = q[r0:r0 + sub, 128 * half:128 * half + 128]
            for kv in range(2):
                in_kv = (lane >= 64 * kv) & (lane < 64 * kv + 64)
                lhs_list.append(jnp.where(in_kv, qh, 0.0).astype(BF16))
    outs = _attend_many(lhs_list, kt_s, [vext_s] * len(lhs_list))
    for j, r0 in enumerate(range(0, tq, sub)):
        for half in range(2):
            o = jnp.where(lane < 64, outs[4 * j + 2 * half], outs[4 * j + 2 * half + 1])
            gate = _silu(gg_ref[r0:r0 + sub, 128 * half:128 * half + 128])
            y_ref[r0:r0 + sub, 128 * half:128 * half + 128] = (o * gate).astype(BF16)


def _gqa_call(p_q, p_c, p_l, tabs, qg, kg, batch, n_ctx, n_lat, rope_q):
    t_total = p_q.shape[0] // batch
    tq = min(ATTN_TQ, t_total)
    nq = t_total // tq
    in_specs = [
        pl.BlockSpec((tq, 256), lambda b, i: (b * nq + i, 6)),
        pl.BlockSpec((tq, 256), lambda b, i: (b * nq + i, 8)),
        pl.BlockSpec((n_ctx, 256), lambda b, i: (b, 7)),
    ]
    args = [p_q, p_q, p_c]
    if n_lat:
        in_specs.append(pl.BlockSpec((n_lat, 256), lambda b, i: (b, 7)))
        args.append(p_l)
    if rope_q:
        in_specs += [pl.BlockSpec((tq, 256), lambda b, i: (i, 0))] * 2
        args += [tabs["cos_g"], tabs["sin_g"]]
    if n_lat:
        in_specs += [pl.BlockSpec((n_lat, 128), lambda b, i: (0, 0))] * 2
        args += [tabs["cos_g"], tabs["sin_g"]]
    in_specs += [pl.BlockSpec((1, 256), lambda b, i: (0, 0)),
                 pl.BlockSpec((1, 128), lambda b, i: (0, 0))]
    args += [qg, kg]
    s_keys = n_ctx + n_lat
    return pl.pallas_call(
        functools.partial(_gqa_kernel, n_ctx=n_ctx, n_lat=n_lat, rope_q=rope_q),
        grid=(batch, nq),
        in_specs=in_specs,
        out_specs=pl.BlockSpec((tq, 256), lambda b, i: (b * nq + i, 0)),
        out_shape=jax.ShapeDtypeStruct((p_q.shape[0], 256), BF16),
        scratch_shapes=[pltpu.VMEM((128, s_keys), BF16),
                        pltpu.VMEM((s_keys, 256), BF16)],
        compiler_params=pltpu.CompilerParams(
            dimension_semantics=("arbitrary", "arbitrary"),
            vmem_limit_bytes=VMEM_LIMIT),
        name="gqa_attn",
    )(*args)


def _diff_kernel(*refs, n_ctx, n_lat, rope_q, lam_init):
    it = iter(refs)
    q_ref, dg_ref, kc_ref, vc_ref = next(it), next(it), next(it), next(it)
    if n_lat:
        kl_ref, vl_ref = next(it), next(it)
    if rope_q:
        cosq_ref, sinq_ref = next(it), next(it)
    if n_lat:
        cosk_ref, sink_ref = next(it), next(it)
    lam_ref, ng_ref = next(it), next(it)
    y_ref = next(it)
    kt_s, vlo_s, vhi_s = next(it), next(it), next(it)
    s_keys = n_ctx + n_lat

    @pl.when(pl.program_id(1) == 0)
    def _prep_kv():
        kt_s[:, 0:n_ctx] = kc_ref[...].T.astype(BF16)
        vlo_s[0:n_ctx, 0:128] = vc_ref[:, 0:128].astype(BF16)
        vhi_s[0:n_ctx, 0:128] = vc_ref[:, 128:256].astype(BF16)
        if n_lat:
            kl = _rope(kl_ref[...], cosk_ref[...], sink_ref[...], 16)
            kt_s[:, n_ctx:s_keys] = kl.T.astype(BF16)
            vlo_s[n_ctx:s_keys, 0:128] = vl_ref[:, 0:128].astype(BF16)
            vhi_s[n_ctx:s_keys, 0:128] = vl_ref[:, 128:256].astype(BF16)

    lp = lam_ref[...]
    lam = (jnp.exp(jnp.sum(lp[0:1, :] * lp[1:2, :], axis=-1, keepdims=True))
           - jnp.exp(jnp.sum(lp[2:3, :] * lp[3:4, :], axis=-1, keepdims=True))
           + lam_init)

    q = q_ref[...]
    if rope_q:
        q = _rope(q, cosq_ref[...], sinq_ref[...], 16)
    q = q * (DIFF_QK_DIM ** -0.5 * LOG2E)
    tq = q.shape[0]
    sub = min(DIFF_SUB, tq)
    lane256 = lax.broadcasted_iota(jnp.int32, (sub, 256), 1)
    lane128 = lax.broadcasted_iota(jnp.int32, (sub, 128), 1)
    seg128 = _seg_ones(128, 64)
    lhs_list, v_list = [], []
    for r0 in range(0, tq, sub):
        for mp in range(2 * DIFF_HEADS):
            in_map = (lane256 >= 32 * mp) & (lane256 < 32 * mp + 32)
            lhs_list.append(jnp.where(in_map, q[r0:r0 + sub, :], 0.0).astype(BF16))
        v_list += [vlo_s, vlo_s, vhi_s, vhi_s]
    heads = _attend_diff_pairs(lhs_list, kt_s, v_list, lam)
    for j, r0 in enumerate(range(0, tq, sub)):
        for half in range(2):
            o = jnp.where(lane128 < 64, heads[4 * j + 2 * half], heads[4 * j + 2 * half + 1])
            n = _seg_rms(o, 64, seg128) * ng_ref[:, 128 * half:128 * half + 128]
            n = n * (1.0 - lam_init)
            gate = _silu(dg_ref[r0:r0 + sub, 128 * half:128 * half + 128])
            y_ref[r0:r0 + sub, 128 * half:128 * half + 128] = (n * gate).astype(BF16)


def _diff_call(p_q, p_c, p_l, tabs, lam_params, ng, batch, n_ctx, n_lat, rope_q,
               lam_init):
    t_total = p_q.shape[0] // batch
    tq = min(ATTN_TQ, t_total)
    nq = t_total // tq
    in_specs = [
        pl.BlockSpec((tq, 256), lambda b, i: (b * nq + i, 9)),
        pl.BlockSpec((tq, 256), lambda b, i: (b * nq + i, 12)),
        pl.BlockSpec((n_ctx, 256), lambda b, i: (b, 10)),
        pl.BlockSpec((n_ctx, 256), lambda b, i: (b, 11)),
    ]
    args = [p_q, p_q, p_c, p_c]
    if n_lat:
        in_specs += [pl.BlockSpec((n_lat, 256), lambda b, i: (b, 10)),
                     pl.BlockSpec((n_lat, 256), lambda b, i: (b, 11))]
        args += [p_l, p_l]
    if rope_q:
        in_specs += [pl.BlockSpec((tq, 256), lambda b, i: (i, 0))] * 2
        args += [tabs["cos_d"], tabs["sin_d"]]
    if n_lat:
        in_specs += [pl.BlockSpec((n_lat, 256), lambda b, i: (0, 0))] * 2
        args += [tabs["cos_d"], tabs["sin_d"]]
    in_specs += [pl.BlockSpec((4, DIFF_QK_DIM), lambda b, i: (0, 0)),
                 pl.BlockSpec((1, 256), lambda b, i: (0, 0))]
    args += [lam_params, ng]
    s_keys = n_ctx + n_lat
    return pl.pallas_call(
        functools.partial(_diff_kernel, n_ctx=n_ctx, n_lat=n_lat, rope_q=rope_q,
                          lam_init=lam_init),
        grid=(batch, nq),
        in_specs=in_specs,
        out_specs=pl.BlockSpec((tq, 256), lambda b, i: (b * nq + i, 0)),
        out_shape=jax.ShapeDtypeStruct((p_q.shape[0], 256), BF16),
        scratch_shapes=[pltpu.VMEM((256, s_keys), BF16),
                        pltpu.VMEM((s_keys, 128), BF16),
                        pltpu.VMEM((s_keys, 128), BF16)],
        compiler_params=pltpu.CompilerParams(
            dimension_semantics=("arbitrary", "arbitrary"),
            vmem_limit_bytes=VMEM_LIMIT),
        name="diff_attn",
    )(*args)


def _ssd_kernel(xc_ref, xl_ref, bc_ref, bl_ref, cc_ref, cl_ref, zc_ref, zl_ref,
                dtc_ref, dtl_ref, cwx_ref, cwb_ref, cwc_ref, cbx_ref, cbb_ref,
                cbc_ref, par_ref, dsk_ref, ng_ref, *rest, n_ctx, n_lat, ctx_out):
    if ctx_out:
        oc_ref, ol_ref = rest[0], rest[1]
        rest = rest[2:]
    else:
        oc_ref, ol_ref = None, rest[0]
        rest = rest[1:]
    xp_s, dtr_s, yacc_s, sloc_s, e_s, cbf_s, sin_s, dec_s, vun_s = rest

    g = pl.program_id(1)
    ncc = n_ctx // CHUNK
    ncl = n_lat // CHUNK
    nch = ncc + ncl
    GROUP = next(n for n in (6, 3, 2, 1) if nch % n == 0)
    t_all = n_ctx + n_lat
    lat0 = n_ctx + 2 * HALO

    zeros_h = jnp.zeros((HALO, 512), F32)
    xp_s[0:HALO, :] = zeros_h
    xp_s[HALO:HALO + n_ctx, 0:256] = xc_ref[...]
    xp_s[HALO:HALO + n_ctx, 256:384] = bc_ref[...]
    xp_s[HALO:HALO + n_ctx, 384:512] = cc_ref[...]
    xp_s[HALO + n_ctx:lat0, :] = zeros_h
    xp_s[lat0:lat0 + n_lat, 0:256] = xl_ref[...]
    xp_s[lat0:lat0 + n_lat, 256:384] = bl_ref[...]
    xp_s[lat0:lat0 + n_lat, 384:512] = cl_ref[...]
    xp_s[lat0 + n_lat:lat0 + n_lat + HALO, :] = zeros_h

    dtr_s[0:n_ctx, :] = dtc_ref[...]
    dtr_s[n_ctx:t_all, :] = dtl_ref[...]
    a_col = -jnp.exp(par_ref[0:16, :])
    bias_col = par_ref[16:32, :]

    def _softplus(v):
        return jnp.maximum(v, 0.0) + jnp.log1p(jnp.exp(-jnp.abs(v)))

    r128 = lax.broadcasted_iota(jnp.int32, (CHUNK, CHUNK), 0)
    c128 = lax.broadcasted_iota(jnp.int32, (CHUNK, CHUNK), 1)
    lower = c128 <= r128
    upper = c128 >= r128
    tril = jnp.where(lower, 1.0, 0.0).astype(BF16)
    triu = jnp.where(upper, 1.0, 0.0).astype(BF16)
    fwd_row = lax.broadcasted_iota(jnp.int32, (16, CHUNK), 0) < 8
    fwd_row1 = lax.broadcasted_iota(jnp.int32, (16, 1), 0) < 8
    er = lax.broadcasted_iota(jnp.int32, (CHUNK, 1024), 0)
    ec = lax.broadcasted_iota(jnp.int32, (CHUNK, 1024), 1)
    src_lane = 16 + 16 * (ec >> 9) + 8 * ((ec >> 8) & 1) + ((ec >> 6) & 3)
    expand = jnp.where(er == src_lane, 1.0, 0.0).astype(BF16)
    er0 = lax.broadcasted_iota(jnp.int32, (CHUNK, 512), 0)
    ec0 = lax.broadcasted_iota(jnp.int32, (CHUNK, 512), 1)
    expand_tot = jnp.where(er0 == 8 * (ec0 >> 8) + ((ec0 >> 6) & 3), 1.0, 0.0).astype(BF16)
    lane256 = lax.broadcasted_iota(jnp.int32, (CHUNK, 256), 1)
    lane128_1 = lax.broadcasted_iota(jnp.int32, (1, CHUNK), 1)

    cw = jnp.concatenate([cwx_ref[...], cwb_ref[...], cwc_ref[...]], axis=1)
    cb = jnp.concatenate([cbx_ref[...], cbb_ref[...], cbc_ref[...]], axis=1)
    dsk = dsk_ref[...]

    def phase_a(grp, carry):
        cs = [grp * GROUP + j for j in range(GROUP)]
        r0s = [pl.multiple_of(c * CHUNK, CHUNK) for c in cs]

        def conv(c):
            wstart = pl.multiple_of(c * CHUNK + jnp.where(c >= ncc, HALO, 0), 8)
            win = xp_s[pl.ds(wstart, CHUNK + 2 * HALO), :]
            acc = jnp.broadcast_to(cb, (CHUNK, 512))
            for k in range(CONV_K):
                d = k - CONV_K // 2
                if d == 0:
                    tap = win[HALO:HALO + CHUNK, :]
                else:
                    tap = pltpu.roll(win, (-d) % (CHUNK + 2 * HALO), 0)[HALO:HALO + CHUNK, :]
                acc = acc + cw[k:k + 1, :] * tap
            return _silu(acc)

        dtts = [_softplus(dtr_s[pl.ds(r0, CHUNK), :].T[0:16, :] + bias_col) for r0 in r0s]
        a_ts = [dtt * a_col for dtt in dtts]
        acol_ts = [jnp.where(fwd_row, _dot_exact_lhs(a_t, triu), _dot_exact_lhs(a_t, tril))
                   for a_t in a_ts]
        us = [conv(c) for c in cs]
        xus = [u[:, 0:256] for u in us]
        bts = [u[:, 256:384].T.astype(BF16) for u in us]
        cbfs = [u[:, 384:512].astype(BF16) for u in us]
        xbfs = [xu.astype(BF16) for xu in xus]
        gmats = [_dot(cbf, bt) for cbf, bt in zip(cbfs, bts)]
        tms = []
        for dtt, acol_t in zip(dtts, acol_ts):
            tot = jnp.where(fwd_row1, acol_t[:, CHUNK - 1:CHUNK], acol_t[:, 0:1])
            w_t = dtt * jnp.exp(tot - acol_t)
            e_t = jnp.exp(acol_t)
            stacked = jnp.concatenate(
                [acol_t, w_t, e_t, jnp.zeros((CHUNK - 48, CHUNK), F32)], axis=0)
            tms.append(stacked.T)
        wes = [_dot(tm.astype(BF16), expand) for tm in tms]
        decs = []
        for tm in tms:
            tot_row = jnp.where(lane128_1 < 8, tm[CHUNK - 1:CHUNK, :], tm[0:1, :])
            tot512 = _dot_exact_lhs(jnp.broadcast_to(tot_row, (8, CHUNK)), expand_tot)
            decs.append(jnp.exp(tot512))
        ydiags = [jnp.zeros((CHUNK, 256), F32) for _ in cs]
        for i in range(4):
            in_head = (lane256 >= 64 * i) & (lane256 < 64 * i + 64)
            for j in range(GROUP):
                tm, acol_t, dtt = tms[j], acol_ts[j], dtts[j]
                arg = jnp.where(lower, tm[:, i:i + 1] - acol_t[i:i + 1, :],
                                tm[:, 8 + i:9 + i] - acol_t[8 + i:9 + i, :])
                scale = (jnp.where(lower, dtt[i:i + 1, :], 0.0)
                         + jnp.where(upper, dtt[8 + i:9 + i, :], 0.0))
                wmat = (gmats[j] * jnp.exp(arg) * scale).astype(BF16)
                ydiags[j] = jnp.where(in_head, _dot(wmat, xbfs[j]), ydiags[j])
        for j, c in enumerate(cs):
            w512 = wes[j][:, 0:512]
            e512 = wes[j][:, 512:1024]
            xdw = (jnp.concatenate([xus[j], xus[j]], axis=1) * w512).astype(BF16)
            sloc_s[c] = _dot(bts[j], xdw)
            dec_s[c] = decs[j]
            e_s[c] = e512
            cbf_s[c] = cbfs[j]
            yacc_s[pl.ds(r0s[j], CHUNK), :] = ydiags[j] + xus[j] * dsk
        return carry

    lax.fori_loop(0, nch // GROUP, phase_a, 0)

    fwd_order = list(range(nch))
    bwd_order = list(range(ncc - 1, -1, -1)) + list(range(nch - 1, ncc - 1, -1))
    for order, lo in ((fwd_order, 0), (bwd_order, 256)):
        state = jnp.zeros((SSD_STATE, 256), F32)
        for c in order:
            sin_s[c, :, lo:lo + 256] = state.astype(BF16)
            state = (state * dec_s[c, 0:1, lo:lo + 256]
                     + sloc_s[c, :, lo:lo + 256])

    def phase_c(cs, z_ref, zrows):
        yos = [_dot(cbf_s[c], sin_s[c]) for c in cs]
        vs = []
        for c, zrow, yo in zip(cs, zrows, yos):
            r0 = c * CHUNK if isinstance(c, int) else pl.multiple_of(c * CHUNK, CHUNK)
            yo = yo * e_s[c]
            y = yacc_s[pl.ds(r0, CHUNK), :] + yo[:, 0:256] + yo[:, 256:512]
            vs.append((r0, y * _silu(z_ref[pl.ds(zrow, CHUNK), :])))

        @pl.when(g == 0)
        def _():
            for r0, v in vs:
                vun_s[pl.ds(r0, CHUNK), 0:256] = v

        @pl.when(g == 1)
        def _():
            for r0, v in vs:
                vun_s[pl.ds(r0, CHUNK), 256:512] = v

    if ctx_out:
        phase_c(list(range(ncc)), zc_ref, [c * CHUNK for c in range(ncc)])

    group_c = next(n for n in (4, 2, 1) if ncl % n == 0)

    def phase_c_lat(k, carry):
        ks = [k * group_c + j for j in range(group_c)]
        phase_c([kk + ncc for kk in ks], zl_ref,
                [pl.multiple_of(kk * CHUNK, CHUNK) for kk in ks])
        return carry

    lax.fori_loop(0, ncl // group_c, phase_c_lat, 0)

    @pl.when(g == SSD_GROUPS - 1)
    def _finalize():
        ng = ng_ref[...]

        def norm_rows(r0, nrows):
            v = vun_s[pl.ds(r0, nrows), :]
            ms = jnp.mean(v * v, axis=-1, keepdims=True)
            return (v * lax.rsqrt(ms + EPS) * ng).astype(BF16)

        if ctx_out:
            oc_ref[...] = norm_rows(0, n_ctx)

        def fin(k, carry):
            r0 = pl.multiple_of(k * 256, 256)
            ol_ref[pl.ds(r0, 256), :] = norm_rows(n_ctx + r0, 256)
            return carry

        lax.fori_loop(0, n_lat // 256, fin, 0)


def _ssd_call(p_c, p_l, conv_w8, conv_b, par, dsk, ng, batch, n_ctx, n_lat, ctx_out):
    nch = (n_ctx + n_lat) // CHUNK
    t_all = n_ctx + n_lat
    in_specs = [
        pl.BlockSpec((n_ctx, 256), lambda b, g: (b, g)),
        pl.BlockSpec((n_lat, 256), lambda b, g: (b, g)),
        pl.BlockSpec((n_ctx, 128), lambda b, g: (b, 4 + g)),
        pl.BlockSpec((n_lat, 128), lambda b, g: (b, 4 + g)),
        pl.BlockSpec((n_ctx, 128), lambda b, g: (b, 6 + g)),
        pl.BlockSpec((n_lat, 128), lambda b, g: (b, 6 + g)),
        pl.BlockSpec((n_ctx, 256), lambda b, g: (b, 4 + g)),
        pl.BlockSpec((n_lat, 256), lambda b, g: (b, 4 + g)),
        pl.BlockSpec((n_ctx, 128), lambda b, g: (b, 26 + g)),
        pl.BlockSpec((n_lat, 128), lambda b, g: (b, 26 + g)),
        pl.BlockSpec((8, 256), lambda b, g: (0, g)),
        pl.BlockSpec((8, 128), lambda b, g: (0, 4 + g)),
        pl.BlockSpec((8, 128), lambda b, g: (0, 6 + g)),
        pl.BlockSpec((1, 256), lambda b, g: (0, g)),
        pl.BlockSpec((1, 128), lambda b, g: (0, 4 + g)),
        pl.BlockSpec((1, 128), lambda b, g: (0, 6 + g)),
        pl.BlockSpec((None, 32, 128), lambda b, g: (g, 0, 0)),
        pl.BlockSpec((1, 256), lambda b, g: (0, g)),
        pl.BlockSpec((1, 512), lambda b, g: (0, 0)),
    ]
    args = [p_c, p_l, p_c, p_l, p_c, p_l, p_c, p_l, p_c, p_l,
            conv_w8, conv_w8, conv_w8, conv_b, conv_b, conv_b, par, dsk, ng]
    out_specs = [pl.BlockSpec((n_lat, 512), lambda b, g: (b, 0))]
    out_shape = [jax.ShapeDtypeStruct((batch * n_lat, 512), BF16)]
    if ctx_out:
        out_specs = [pl.BlockSpec((n_ctx, 512), lambda b, g: (b, 0))] + out_specs
        out_shape = [jax.ShapeDtypeStruct((batch * n_ctx, 512), BF16)] + out_shape
    scratch = [
        pltpu.VMEM((t_all + 3 * HALO, 512), F32),
        pltpu.VMEM((t_all, 128), F32),
        pltpu.VMEM((t_all, 256), F32),
        pltpu.VMEM((nch, SSD_STATE, 512), F32),
        pltpu.VMEM((nch, CHUNK, 512), F32),
        pltpu.VMEM((nch, CHUNK, 128), BF16),
        pltpu.VMEM((nch, SSD_STATE, 512), BF16),
        pltpu.VMEM((nch, 8, 512), F32),
        pltpu.VMEM((t_all, 512), F32),
    ]
    outs = pl.pallas_call(
        functools.partial(_ssd_kernel, n_ctx=n_ctx, n_lat=n_lat, ctx_out=ctx_out),
        grid=(batch, SSD_GROUPS),
        in_specs=in_specs,
        out_specs=out_specs,
        out_shape=out_shape,
        scratch_shapes=scratch,
        compiler_params=pltpu.CompilerParams(
            dimension_semantics=("arbitrary", "arbitrary"),
            vmem_limit_bytes=VMEM_LIMIT),
        name="ssd_scan",
    )(*args)
    if ctx_out:
        return outs[1], outs[0]
    return outs[0], None


def _rope_tables(n_lat):
    rows = n_lat // GRID_W
    row_idx = np.repeat(np.arange(rows), GRID_W).astype(np.float32)
    col_idx = (np.arange(rows * GRID_W) % GRID_W).astype(np.float32)

    def tables(dim, reps):
        quarter = dim // 4
        inv = (ROPE_BASE ** (-np.arange(quarter, dtype=np.float32) / quarter)).astype(np.float32)
        ang = np.concatenate([row_idx[:, None] * inv, col_idx[:, None] * inv], axis=-1)
        cos, sin = np.cos(ang.astype(np.float64)), np.sin(ang.astype(np.float64))
        cos2 = np.concatenate([cos, cos], axis=-1).astype(np.float32)
        sin2 = np.concatenate([-sin, sin], axis=-1).astype(np.float32)
        return jnp.asarray(np.tile(cos2, (1, reps))), jnp.asarray(np.tile(sin2, (1, reps)))

    cos_g, sin_g = tables(GQA_HEAD_DIM, 4)
    cos_d, sin_d = tables(DIFF_QK_DIM, 8)
    return {"cos_g": cos_g, "sin_g": sin_g, "cos_d": cos_d, "sin_d": sin_d}


def kernel(x, c, ctx, c_ctx, w_mod, b_mod, g_pre, g_post, w_in, conv_w, conv_b,
           a_log_fwd, a_log_bwd, dt_bias_fwd, dt_bias_bwd, d_skip, ssd_norm_g,
           q_norm_g, k_norm_g, diff_lambda, diff_norm_g, w_out):
    batch, n_lat, _ = x.shape
    n_ctx = ctx.shape[1]
    depth = w_mod.shape[0]
    assert n_lat % 512 == 0 and n_ctx % 256 == 0 and (batch * n_ctx) % 512 == 0
    assert batch + 1 <= 16

    in_perm, out_perm = _in_col_perm(), _out_row_perm()
    w_in_p = [_take_runs(w_in[l], in_perm, 1, IN_COLS).astype(BF16) for l in range(depth)]
    w_out_p = [_take_runs(w_out[l], out_perm, 0, None).astype(BF16) for l in range(depth)]
    conv_w8 = jnp.pad(conv_w, ((0, 0), (0, 8 - CONV_K), (0, 0)))
    conv_b1 = conv_b[:, None, :]

    def group16(fwd, bwd):
        out = jnp.zeros((depth, SSD_GROUPS, 16), F32)
        for g in range(SSD_GROUPS):
            out = out.at[:, g, 0:4].set(fwd[:, 4 * g:4 * g + 4])
            out = out.at[:, g, 8:12].set(bwd[:, 4 * g:4 * g + 4])
        return out

    ssd_par = jnp.broadcast_to(
        jnp.concatenate([group16(a_log_fwd, a_log_bwd),
                         group16(dt_bias_fwd, dt_bias_bwd)], axis=-1)[..., None],
        (depth, SSD_GROUPS, 32, 128))
    dsk = jnp.repeat(d_skip, SSD_HEAD_DIM, axis=1)[:, None, :]
    qg = jnp.tile(q_norm_g, (1, 4))[:, None, :]
    kg = jnp.tile(k_norm_g, (1, 2))[:, None, :]
    dng = jnp.tile(diff_norm_g, (1, 4))[:, None, :]
    tabs = _rope_tables(n_lat)

    cs = jnp.concatenate(
        [c, c_ctx[None, :], jnp.zeros((16 - batch - 1, D_MODEL), F32)], axis=0)
    mod_all = _mod_call(cs, w_mod, b_mod)

    h = x.reshape(batch * n_lat, D_MODEL)
    hc = ctx.reshape(batch * n_ctx, D_MODEL)
    for l in range(depth):
        ctx_out = l < depth - 1
        lam_init = 0.8 - 0.6 * float(np.exp(-0.3 * l))
        mod3 = mod_all[l][:, None, :]
        p_l = _inproj_call(h, mod3, g_pre[l][None, :], w_in_p[l], n_lat // 512, None)
        p_c = _inproj_call(hc, mod3, g_pre[l][None, :], w_in_p[l], None, batch)

        ys_l, ys_c = _ssd_call(p_c, p_l, conv_w8[l], conv_b1[l], ssd_par[l], dsk[l],
                               ssd_norm_g[l][None, :], batch, n_ctx, n_lat, ctx_out)
        yg_l = _gqa_call(p_l, p_c, p_l, tabs, qg[l], kg[l], batch, n_ctx, n_lat, True)
        yd_l = _diff_call(p_l, p_c, p_l, tabs, diff_lambda[l], dng[l], batch, n_ctx,
                          n_lat, True, lam_init)
        h = _outproj_call(ys_l, yg_l, yd_l, h, mod3, g_post[l][None, :], w_out_p[l],
                          n_lat, None)
        if ctx_out:
            yg_c = _gqa_call(p_c, p_c, None, tabs, qg[l], kg[l], batch, n_ctx, 0, False)
            yd_c = _diff_call(p_c, p_c, None, tabs, diff_lambda[l], dng[l], batch,
                              n_ctx, 0, False, lam_init)
            hc = _outproj_call(ys_c, yg_c, yd_c, hc, mod3, g_post[l][None, :],
                               w_out_p[l], None, batch)
    return h.reshape(batch, n_lat, D_MODEL)
```

```python
import functools

import numpy as np
import jax
import jax.numpy as jnp
from jax import lax
from jax.experimental import pallas as pl
from jax.experimental.pallas import tpu as pltpu

F32 = jnp.float32
BF16 = jnp.bfloat16

D_MODEL = 1024
GRID_W = 64
ROPE_BASE = 10000.0
EPS = 1e-6
LOG2E = 1.4426950408889634

SSD_WIDTH = 512
SSD_HEADS = 8
SSD_HEAD_DIM = 64
SSD_GROUPS = 2
SSD_STATE = 128
CHUNK = 128
CONV_K = 5
HALO = 8
GQA_HEADS = 4
GQA_HEAD_DIM = 64
DIFF_HEADS = 4
DIFF_QK_DIM = 32
DIFF_V_DIM = 64

_IN_SPLITS = (("xbc", 1024), ("z", 512), ("dt", 16), ("gq", 256), ("gk", 128),
              ("gv", 128), ("gg", 256), ("dq", 256), ("dk", 256), ("dv", 256),
              ("dg", 256))
IN_COLS = sum(s for _, s in _IN_SPLITS)
NP = 28 * 128
GQ_HEAD_ORDER = (0, 2, 1, 3)

VMEM_LIMIT = 56 * 1024 * 1024
ATTN_TQ = 1024
DIFF_TQ = 512
ATTN_SUB = 512
DIFF_SUB = 256
ATTN_AHEAD = 2
DIFF_AHEAD = 1
SCORE_BOUND_MARGIN = 1.02
MIN_ROW_SUM = 2.0 ** -90
OUTPROJ_TM = 1024
OUTPROJ_SUB = 256


def _in_col_perm():
    off, o = {}, 0
    for name, size in _IN_SPLITS:
        off[name] = o
        o += size
    pad = IN_COLS
    cols = list(range(off["xbc"], off["xbc"] + 1024))
    cols += list(range(off["z"], off["z"] + 512))
    for h in GQ_HEAD_ORDER:
        cols += list(range(off["gq"] + 64 * h, off["gq"] + 64 * h + 64))
    cols += list(range(off["gk"], off["gk"] + 128))
    cols += list(range(off["gv"], off["gv"] + 128))
    for h in GQ_HEAD_ORDER:
        cols += list(range(off["gg"] + 64 * h, off["gg"] + 64 * h + 64))
    for name in ("dq", "dk", "dv", "dg"):
        cols += list(range(off[name], off[name] + 256))
    for g in range(SSD_GROUPS):
        blk = [pad] * 128
        for i in range(4):
            blk[i] = off["dt"] + 4 * g + i
            blk[8 + i] = off["dt"] + SSD_HEADS + 4 * g + i
        cols += blk
    assert len(cols) == NP
    return np.asarray(cols, np.int32)


def _out_row_perm():
    rows = list(range(SSD_WIDTH))
    for h in GQ_HEAD_ORDER:
        rows += list(range(SSD_WIDTH + 64 * h, SSD_WIDTH + 64 * h + 64))
    rows += list(range(SSD_WIDTH + 256, SSD_WIDTH + 512))
    return np.asarray(rows, np.int32)


def _take_runs(arr, idx, axis, pad_index):
    pieces, start = [], 0
    idx = [int(i) for i in idx]
    while start < len(idx):
        end = start + 1
        if idx[start] == pad_index:
            while end < len(idx) and idx[end] == pad_index:
                end += 1
            shape = list(arr.shape)
            shape[axis] = end - start
            pieces.append(jnp.zeros(shape, arr.dtype))
        else:
            while end < len(idx) and idx[end] == idx[end - 1] + 1 and idx[end] != pad_index:
                end += 1
            pieces.append(lax.slice_in_dim(arr, idx[start], idx[end - 1] + 1, axis=axis))
        start = end
    return jnp.concatenate(pieces, axis=axis)


def _dot(a, b):
    return jnp.dot(a, b, preferred_element_type=F32)


def _split3(x):
    hi = x.astype(BF16)
    r1 = x - hi.astype(F32)
    mid = r1.astype(BF16)
    lo = (r1 - mid.astype(F32)).astype(BF16)
    return hi, mid, lo


def _dot_exact_lhs(x, m_bf16):
    hi, mid, lo = _split3(x)
    return _dot(hi, m_bf16) + _dot(mid, m_bf16) + _dot(lo, m_bf16)


def _silu(x):
    return x * jax.nn.sigmoid(x)


def _seg_ones(width, seg):
    r = lax.broadcasted_iota(jnp.int32, (width, width), 0)
    c = lax.broadcasted_iota(jnp.int32, (width, width), 1)
    same = (r & ~(seg - 1)) == (c & ~(seg - 1))
    return jnp.where(same, 1.0, 0.0).astype(BF16)


def _seg_rms(x, seg, seg_mat):
    ss = _dot_exact_lhs(x * x, seg_mat)
    return x * lax.rsqrt(ss * (1.0 / seg) + EPS)


def _rope(x, cos, sin_signed, half):
    w = x.shape[-1]
    lane = lax.broadcasted_iota(jnp.int32, x.shape, 1)
    first = (lane & (2 * half - 1)) < half
    swapped = jnp.where(first, pltpu.roll(x, w - half, 1), pltpu.roll(x, half, 1))
    return x * cos + swapped * sin_signed


def _mod_kernel(cs_ref, w_ref, b_ref, o_ref):
    s = _silu(cs_ref[...]).astype(BF16)
    o_ref[...] = _dot(s, w_ref[...].astype(BF16)) + b_ref[...]


def _mod_call(cs, w_mod, b_mod):
    depth = w_mod.shape[0]
    nrow = cs.shape[0]
    tn = 1024
    return pl.pallas_call(
        _mod_kernel,
        grid=(depth, 3 * D_MODEL // tn),
        in_specs=[
            pl.BlockSpec((nrow, D_MODEL), lambda l, j: (0, 0)),
            pl.BlockSpec((None, D_MODEL, tn), lambda l, j: (l, 0, j)),
            pl.BlockSpec((None, 1, tn), lambda l, j: (l, 0, j)),
        ],
        out_specs=pl.BlockSpec((None, nrow, tn), lambda l, j: (l, 0, j)),
        out_shape=jax.ShapeDtypeStruct((depth, nrow, 3 * D_MODEL), F32),
        compiler_params=pltpu.CompilerParams(
            dimension_semantics=("arbitrary", "arbitrary")),
        name="mod_proj",
    )(cs, w_mod, b_mod.reshape(depth, 1, 3 * D_MODEL))


def _inproj_kernel(h_ref, mod_ref, g_ref, w_ref, o_ref):
    x = h_ref[...]
    ms = jnp.mean(x * x, axis=-1, keepdims=True)
    y = x * lax.rsqrt(ms + EPS) * g_ref[...]
    sh = mod_ref[:, 0:D_MODEL]
    sc = mod_ref[:, D_MODEL:2 * D_MODEL]
    u = (y * (1.0 + sc) + sh).astype(BF16)
    tn = 512
    for j in range(NP // tn):
        o_ref[:, j * tn:(j + 1) * tn] = _dot(u, w_ref[:, j * tn:(j + 1) * tn])


def _inproj_call(h, mod3, g_pre, w_bf16, tiles_per_row, fixed_row):
    n_tok = h.shape[0]
    tm = 512
    if fixed_row is None:
        mod_idx = lambda i: (i // tiles_per_row, 0, 0)
    else:
        mod_idx = lambda i: (fixed_row, 0, 0)
    return pl.pallas_call(
        _inproj_kernel,
        grid=(n_tok // tm,),
        in_specs=[
            pl.BlockSpec((tm, D_MODEL), lambda i: (i, 0)),
            pl.BlockSpec((None, 1, 3 * D_MODEL), mod_idx),
            pl.BlockSpec((1, D_MODEL), lambda i: (0, 0)),
            pl.BlockSpec((D_MODEL, NP), lambda i: (0, 0)),
        ],
        out_specs=pl.BlockSpec((tm, NP), lambda i: (i, 0)),
        out_shape=jax.ShapeDtypeStruct((n_tok, NP), F32),
        compiler_params=pltpu.CompilerParams(
            dimension_semantics=("arbitrary",), vmem_limit_bytes=VMEM_LIMIT),
        name="in_proj",
    )(h, mod3, g_pre, w_bf16)


def _outproj_kernel(ys_ref, yg_ref, yd_ref, h_ref, mod_ref, g_ref, w_ref, o_ref):
    tm = h_ref.shape[0]
    sub = min(OUTPROJ_SUB, tm)
    gt = mod_ref[:, 2 * D_MODEL:3 * D_MODEL]
    gain = g_ref[...]

    def project(r0):
        return (_dot(ys_ref[r0:r0 + sub, :], w_ref[0:512, :])
                + _dot(yg_ref[r0:r0 + sub, :], w_ref[512:768, :])
                + _dot(yd_ref[r0:r0 + sub, :], w_ref[768:1024, :]))

    o_next = project(0)
    for r0 in range(0, tm, sub):
        o = o_next
        if r0 + sub < tm:
            o_next = project(r0 + sub)
        ms = jnp.mean(o * o, axis=-1, keepdims=True)
        n = o * lax.rsqrt(ms + EPS) * gain
        o_ref[r0:r0 + sub, :] = h_ref[r0:r0 + sub, :] + gt * n


def _outproj_call(ys, yg, yd, h, mod3, g_post, w_bf16, rows_per_mod, fixed_row):
    n_tok = h.shape[0]
    tm = OUTPROJ_TM
    assert n_tok % tm == 0
    if fixed_row is None:
        assert rows_per_mod % tm == 0
        mod_idx = lambda i: (i // (rows_per_mod // tm), 0, 0)
    else:
        mod_idx = lambda i: (fixed_row, 0, 0)
    return pl.pallas_call(
        _outproj_kernel,
        grid=(n_tok // tm,),
        in_specs=[
            pl.BlockSpec((tm, 512), lambda i: (i, 0)),
            pl.BlockSpec((tm, 256), lambda i: (i, 0)),
            pl.BlockSpec((tm, 256), lambda i: (i, 0)),
            pl.BlockSpec((tm, D_MODEL), lambda i: (i, 0)),
            pl.BlockSpec((None, 1, 3 * D_MODEL), mod_idx),
            pl.BlockSpec((1, D_MODEL), lambda i: (0, 0)),
            pl.BlockSpec((D_MODEL, D_MODEL), lambda i: (0, 0)),
        ],
        out_specs=pl.BlockSpec((tm, D_MODEL), lambda i: (i, 0)),
        out_shape=jax.ShapeDtypeStruct((n_tok, D_MODEL), F32),
        compiler_params=pltpu.CompilerParams(
            dimension_semantics=("arbitrary",), vmem_limit_bytes=VMEM_LIMIT),
        name="out_proj",
    )(ys, yg, yd, h, mod3, g_post, w_bf16)


def _attend_many(lhs_list, kt_ref, vext_refs):
    def scores(i):
        s = _dot(lhs_list[i], kt_ref[...])
        return s, jnp.max(s, axis=-1, keepdims=True)

    outs = []
    n = len(lhs_list)
    ahead = [scores(i) for i in range(min(ATTN_AHEAD, n))]
    for i, vext_ref in enumerate(vext_refs):
        s, m = ahead.pop(0)
        if i + ATTN_AHEAD < n:
            ahead.append(scores(i + ATTN_AHEAD))
        p = jnp.exp2(s - m).astype(BF16)
        oe = _dot(p, vext_ref[...])
        outs.append(oe[:, 0:128] / oe[:, 128:256])
    return outs


def _attend_diff_pairs(lhs_list, kt_ref, v_refs, lam):
    n_heads = len(v_refs)

    def scores(h):
        s_a = _dot(lhs_list[2 * h], kt_ref[...])
        m_a = jnp.max(s_a, axis=-1, keepdims=True)
        s_b = _dot(lhs_list[2 * h + 1], kt_ref[...])
        m_b = jnp.max(s_b, axis=-1, keepdims=True)
        return s_a, m_a, s_b, m_b

    outs = []
    ahead = [scores(h) for h in range(min(DIFF_AHEAD, n_heads))]
    for h in range(n_heads):
        s_a, m_a, s_b, m_b = ahead.pop(0)
        if h + DIFF_AHEAD < n_heads:
            ahead.append(scores(h + DIFF_AHEAD))
        e_a = jnp.exp2(s_a - m_a)
        e_b = jnp.exp2(s_b - m_b)
        l_a = jnp.sum(e_a, axis=-1, keepdims=True)
        l_b = jnp.sum(e_b, axis=-1, keepdims=True)
        pc = (e_a - (lam * l_a / l_b) * e_b).astype(BF16)
        outs.append(_dot(pc, v_refs[h][...]) / l_a)
    return outs


def _attend_diff_pairs_bounded(lhs_list, bounds, kt_ref, v_refs, lam):
    n_heads = len(v_refs)

    def exps(h):
        e_a = jnp.exp2(_dot(lhs_list[2 * h], kt_ref[...]) - bounds[2 * h])
        e_b = jnp.exp2(_dot(lhs_list[2 * h + 1], kt_ref[...]) - bounds[2 * h + 1])
        return e_a, e_b

    outs, l_min = [], None
    ahead = [exps(h) for h in range(min(DIFF_AHEAD, n_heads))]
    for h in range(n_heads):
        e_a, e_b = ahead.pop(0)
        if h + DIFF_AHEAD < n_heads:
            ahead.append(exps(h + DIFF_AHEAD))
        l_a = jnp.sum(e_a, axis=-1, keepdims=True)
        l_b = jnp.sum(e_b, axis=-1, keepdims=True)
        pc = (e_a - (lam * l_a / l_b) * e_b).astype(BF16)
        outs.append(_dot(pc, v_refs[h][...]) / l_a)
        l_ab = jnp.minimum(l_a, l_b)
        l_min = l_ab if l_min is None else jnp.minimum(l_min, l_ab)
    return outs, jnp.min(l_min)


def _gqa_kernel(*refs, n_ctx, n_lat, rope_q):
    it = iter(refs)
    q_ref, gg_ref, kvc_ref = next(it), next(it), next(it)
    kvl_ref = next(it) if n_lat else None
    if rope_q:
        cosq_ref, sinq_ref = next(it), next(it)
    if n_lat:
        cosk_ref, sink_ref = next(it), next(it)
    qg_ref, kg_ref = next(it), next(it)
    y_ref = next(it)
    kt_s, vext_s = next(it), next(it)

    seg128 = _seg_ones(128, 64)

    @pl.when(pl.program_id(1) == 0)
    def _prep_kv():
        kc = _seg_rms(kvc_ref[:, 0:128], 64, seg128) * kg_ref[...]
        kt_s[:, 0:n_ctx] = kc.T.astype(BF16)
        vext_s[0:n_ctx, 0:128] = kvc_ref[:, 128:256].astype(BF16)
        if n_lat:
            kl = _seg_rms(kvl_ref[:, 0:128], 64, seg128) * kg_ref[...]
            kl = _rope(kl, cosk_ref[...], sink_ref[...], 32)
            kt_s[:, n_ctx:n_ctx + n_lat] = kl.T.astype(BF16)
            vext_s[n_ctx:n_ctx + n_lat, 0:128] = kvl_ref[:, 128:256].astype(BF16)
        vext_s[:, 128:256] = jnp.ones((n_ctx + n_lat, 128), BF16)

    seg256 = _seg_ones(256, 64)
    q = _seg_rms(q_ref[...], 64, seg256) * qg_ref[...]
    if rope_q:
        q = _rope(q, cosq_ref[...], sinq_ref[...], 32)
    q = q * (GQA_HEAD_DIM ** -0.5 * LOG2E)
    tq = q.shape[0]
    sub = min(ATTN_SUB, tq)
    lane = lax.broadcasted_iota(jnp.int32, (sub, 128), 1)
    lhs_list = []
    for r0 in range(0, tq, sub):
        for half in range(2):
            qh = q[r0:r0 + sub, 128 * half:128 * half + 128]
            for kv in range(2):
                in_kv = (lane >= 64 * kv) & (lane < 64 * kv + 64)
                lhs_list.append(jnp.where(in_kv, qh, 0.0).astype(BF16))
    outs = _attend_many(lhs_list, kt_s, [vext_s] * len(lhs_list))
    for j, r0 in enumerate(range(0, tq, sub)):
        for half in range(2):
            o = jnp.where(lane < 64, outs[4 * j + 2 * half], outs[4 * j + 2 * half + 1])
            gate = _silu(gg_ref[r0:r0 + sub, 128 * half:128 * half + 128])
            y_ref[r0:r0 + sub, 128 * half:128 * half + 128] = (o * gate).astype(BF16)


def _gqa_call(p_q, p_c, p_l, tabs, qg, kg, batch, n_ctx, n_lat, rope_q):
    t_total = p_q.shape[0] // batch
    tq = min(ATTN_TQ, t_total)
    nq = t_total // tq
    in_specs = [
        pl.BlockSpec((tq, 256), lambda b, i: (b * nq + i, 6)),
        pl.BlockSpec((tq, 256), lambda b, i: (b * nq + i, 8)),
        pl.BlockSpec((n_ctx, 256), lambda b, i: (b, 7)),
    ]
    args = [p_q, p_q, p_c]
    if n_lat:
        in_specs.append(pl.BlockSpec((n_lat, 256), lambda b, i: (b, 7)))
        args.append(p_l)
    if rope_q:
        in_specs += [pl.BlockSpec((tq, 256), lambda b, i: (i, 0))] * 2
        args += [tabs["cos_g"], tabs["sin_g"]]
    if n_lat:
        in_specs += [pl.BlockSpec((n_lat, 128), lambda b, i: (0, 0))] * 2
        args += [tabs["cos_g"], tabs["sin_g"]]
    in_specs += [pl.BlockSpec((1, 256), lambda b, i: (0, 0)),
                 pl.BlockSpec((1, 128), lambda b, i: (0, 0))]
    args += [qg, kg]
    s_keys = n_ctx + n_lat
    return pl.pallas_call(
        functools.partial(_gqa_kernel, n_ctx=n_ctx, n_lat=n_lat, rope_q=rope_q),
        grid=(batch, nq),
        in_specs=in_specs,
        out_specs=pl.BlockSpec((tq, 256), lambda b, i: (b * nq + i, 0)),
        out_shape=jax.ShapeDtypeStruct((p_q.shape[0], 256), BF16),
        scratch_shapes=[pltpu.VMEM((128, s_keys), BF16),
                        pltpu.VMEM((s_keys, 256), BF16)],
        compiler_params=pltpu.CompilerParams(
            dimension_semantics=("arbitrary", "arbitrary"),
            vmem_limit_bytes=VMEM_LIMIT),
        name="gqa_attn",
    )(*args)


def _diff_kernel(*refs, n_ctx, n_lat, rope_q, lam_init):
    it = iter(refs)
    q_ref, dg_ref, kc_ref, vc_ref = next(it), next(it), next(it), next(it)
    if n_lat:
        kl_ref, vl_ref = next(it), next(it)
    if rope_q:
        cosq_ref, sinq_ref = next(it), next(it)
    if n_lat:
        cosk_ref, sink_ref = next(it), next(it)
    lam_ref, ng_ref = next(it), next(it)
    y_ref = next(it)
    kt_s, vlo_s, vhi_s, kmax_s = next(it), next(it), next(it), next(it)
    s_keys = n_ctx + n_lat
    seg32 = _seg_ones(256, DIFF_QK_DIM)

    def map_norms(x):
        return jnp.sqrt(_dot((x * x).astype(BF16), seg32))

    @pl.when(pl.program_id(1) == 0)
    def _prep_kv():
        kc = kc_ref[...]
        kt_s[:, 0:n_ctx] = kc.T.astype(BF16)
        vlo_s[0:n_ctx, 0:128] = vc_ref[:, 0:128].astype(BF16)
        vhi_s[0:n_ctx, 0:128] = vc_ref[:, 128:256].astype(BF16)
        kmax = jnp.max(map_norms(kc), axis=0, keepdims=True)
        if n_lat:
            kl = _rope(kl_ref[...], cosk_ref[...], sink_ref[...], 16)
            kt_s[:, n_ctx:s_keys] = kl.T.astype(BF16)
            vlo_s[n_ctx:s_keys, 0:128] = vl_ref[:, 0:128].astype(BF16)
            vhi_s[n_ctx:s_keys, 0:128] = vl_ref[:, 128:256].astype(BF16)
            kmax = jnp.maximum(kmax, jnp.max(map_norms(kl), axis=0, keepdims=True))
        kmax_s[...] = jnp.broadcast_to(kmax, kmax_s.shape)

    lp = lam_ref[...]
    lam = (jnp.exp(jnp.sum(lp[0:1, :] * lp[1:2, :], axis=-1, keepdims=True))
           - jnp.exp(jnp.sum(lp[2:3, :] * lp[3:4, :], axis=-1, keepdims=True))
           + lam_init)

    q = q_ref[...]
    if rope_q:
        q = _rope(q, cosq_ref[...], sinq_ref[...], 16)
    q = q * (DIFF_QK_DIM ** -0.5 * LOG2E)
    tq = q.shape[0]
    sub = min(DIFF_SUB, tq)
    lane256 = lax.broadcasted_iota(jnp.int32, (sub, 256), 1)
    lane128 = lax.broadcasted_iota(jnp.int32, (sub, 128), 1)
    seg128 = _seg_ones(128, 64)
    bound_all = map_norms(q) * kmax_s[0:1, :] * SCORE_BOUND_MARGIN
    lhs_list, v_list, bounds = [], [], []
    for r0 in range(0, tq, sub):
        for mp in range(2 * DIFF_HEADS):
            in_map = (lane256 >= 32 * mp) & (lane256 < 32 * mp + 32)
            lhs_list.append(jnp.where(in_map, q[r0:r0 + sub, :], 0.0).astype(BF16))
            bounds.append(bound_all[r0:r0 + sub, 32 * mp:32 * mp + 1])
        v_list += [vlo_s, vlo_s, vhi_s, vhi_s]

    def finish(heads):
        for j, r0 in enumerate(range(0, tq, sub)):
            for half in range(2):
                o = jnp.where(lane128 < 64, heads[4 * j + 2 * half],
                              heads[4 * j + 2 * half + 1])
                n = _seg_rms(o, 64, seg128) * ng_ref[:, 128 * half:128 * half + 128]
                n = n * (1.0 - lam_init)
                gate = _silu(dg_ref[r0:r0 + sub, 128 * half:128 * half + 128])
                y_ref[r0:r0 + sub, 128 * half:128 * half + 128] = (n * gate).astype(BF16)

    heads, l_min = _attend_diff_pairs_bounded(lhs_list, bounds, kt_s, v_list, lam)
    finish(heads)

    @pl.when(jnp.logical_not(l_min >= MIN_ROW_SUM))
    def _exact_max_fallback():
        finish(_attend_diff_pairs(lhs_list, kt_s, v_list, lam))


def _diff_call(p_q, p_c, p_l, tabs, lam_params, ng, batch, n_ctx, n_lat, rope_q,
               lam_init):
    t_total = p_q.shape[0] // batch
    tq = min(DIFF_TQ, t_total)
    nq = t_total // tq
    in_specs = [
        pl.BlockSpec((tq, 256), lambda b, i: (b * nq + i, 9)),
        pl.BlockSpec((tq, 256), lambda b, i: (b * nq + i, 12)),
        pl.BlockSpec((n_ctx, 256), lambda b, i: (b, 10)),
        pl.BlockSpec((n_ctx, 256), lambda b, i: (b, 11)),
    ]
    args = [p_q, p_q, p_c, p_c]
    if n_lat:
        in_specs += [pl.BlockSpec((n_lat, 256), lambda b, i: (b, 10)),
                     pl.BlockSpec((n_lat, 256), lambda b, i: (b, 11))]
        args += [p_l, p_l]
    if rope_q:
        in_specs += [pl.BlockSpec((tq, 256), lambda b, i: (i, 0))] * 2
        args += [tabs["cos_d"], tabs["sin_d"]]
    if n_lat:
        in_specs += [pl.BlockSpec((n_lat, 256), lambda b, i: (0, 0))] * 2
        args += [tabs["cos_d"], tabs["sin_d"]]
    in_specs += [pl.BlockSpec((4, DIFF_QK_DIM), lambda b, i: (0, 0)),
                 pl.BlockSpec((1, 256), lambda b, i: (0, 0))]
    args += [lam_params, ng]
    s_keys = n_ctx + n_lat
    return pl.pallas_call(
        functools.partial(_diff_kernel, n_ctx=n_ctx, n_lat=n_lat, rope_q=rope_q,
                          lam_init=lam_init),
        grid=(batch, nq),
        in_specs=in_specs,
        out_specs=pl.BlockSpec((tq, 256), lambda b, i: (b * nq + i, 0)),
        out_shape=jax.ShapeDtypeStruct((p_q.shape[0], 256), BF16),
        scratch_shapes=[pltpu.VMEM((256, s_keys), BF16),
                        pltpu.VMEM((s_keys, 128), BF16),
                        pltpu.VMEM((s_keys, 128), BF16),
                        pltpu.VMEM((8, 256), F32)],
        compiler_params=pltpu.CompilerParams(
            dimension_semantics=("arbitrary", "arbitrary"),
            vmem_limit_bytes=VMEM_LIMIT),
        name="diff_attn",
    )(*args)


def _ssd_kernel(xc_ref, xl_ref, bc_ref, bl_ref, cc_ref, cl_ref, zc_ref, zl_ref,
                dtc_ref, dtl_ref, cwx_ref, cwb_ref, cwc_ref, cbx_ref, cbb_ref,
                cbc_ref, par_ref, dsk_ref, ng_ref, *rest, n_ctx, n_lat, ctx_out):
    if ctx_out:
        oc_ref, ol_ref = rest[0], rest[1]
        rest = rest[2:]
    else:
        oc_ref, ol_ref = None, rest[0]
        rest = rest[1:]
    xp_s, dtr_s, yacc_s, sloc_s, e_s, cbf_s, sin_s, dec_s, vun_s = rest

    g = pl.program_id(1)
    ncc = n_ctx // CHUNK
    ncl = n_lat // CHUNK
    nch = ncc + ncl
    GROUP = next(n for n in (6, 3, 2, 1) if nch % n == 0)
    t_all = n_ctx + n_lat
    lat0 = n_ctx + 2 * HALO

    zeros_h = jnp.zeros((HALO, 512), F32)
    xp_s[0:HALO, :] = zeros_h
    xp_s[HALO:HALO + n_ctx, 0:256] = xc_ref[...]
    xp_s[HALO:HALO + n_ctx, 256:384] = bc_ref[...]
    xp_s[HALO:HALO + n_ctx, 384:512] = cc_ref[...]
    xp_s[HALO + n_ctx:lat0, :] = zeros_h
    xp_s[lat0:lat0 + n_lat, 0:256] = xl_ref[...]
    xp_s[lat0:lat0 + n_lat, 256:384] = bl_ref[...]
    xp_s[lat0:lat0 + n_lat, 384:512] = cl_ref[...]
    xp_s[lat0 + n_lat:lat0 + n_lat + HALO, :] = zeros_h

    dtr_s[0:n_ctx, :] = dtc_ref[...]
    dtr_s[n_ctx:t_all, :] = dtl_ref[...]
    a_col = -jnp.exp(par_ref[0:16, :])
    bias_col = par_ref[16:32, :]

    def _softplus(v):
        return jnp.maximum(v, 0.0) + jnp.log1p(jnp.exp(-jnp.abs(v)))

    r128 = lax.broadcasted_iota(jnp.int32, (CHUNK, CHUNK), 0)
    c128 = lax.broadcasted_iota(jnp.int32, (CHUNK, CHUNK), 1)
    lower = c128 <= r128
    upper = c128 >= r128
    tril = jnp.where(lower, 1.0, 0.0).astype(BF16)
    triu = jnp.where(upper, 1.0, 0.0).astype(BF16)
    fwd_row = lax.broadcasted_iota(jnp.int32, (16, CHUNK), 0) < 8
    fwd_row1 = lax.broadcasted_iota(jnp.int32, (16, 1), 0) < 8
    er = lax.broadcasted_iota(jnp.int32, (CHUNK, 1024), 0)
    ec = lax.broadcasted_iota(jnp.int32, (CHUNK, 1024), 1)
    src_lane = 16 + 16 * (ec >> 9) + 8 * ((ec >> 8) & 1) + ((ec >> 6) & 3)
    expand = jnp.where(er == src_lane, 1.0, 0.0).astype(BF16)
    er0 = lax.broadcasted_iota(jnp.int32, (CHUNK, 512), 0)
    ec0 = lax.broadcasted_iota(jnp.int32, (CHUNK, 512), 1)
    expand_tot = jnp.where(er0 == 8 * (ec0 >> 8) + ((ec0 >> 6) & 3), 1.0, 0.0).astype(BF16)
    lane256 = lax.broadcasted_iota(jnp.int32, (CHUNK, 256), 1)
    lane128_1 = lax.broadcasted_iota(jnp.int32, (1, CHUNK), 1)

    cw = jnp.concatenate([cwx_ref[...], cwb_ref[...], cwc_ref[...]], axis=1)
    cb = jnp.concatenate([cbx_ref[...], cbb_ref[...], cbc_ref[...]], axis=1)
    dsk = dsk_ref[...]

    def phase_a(grp, carry):
        cs = [grp * GROUP + j for j in range(GROUP)]
        r0s = [pl.multiple_of(c * CHUNK, CHUNK) for c in cs]

        def conv(c):
            wstart = pl.multiple_of(c * CHUNK + jnp.where(c >= ncc, HALO, 0), 8)
            win = xp_s[pl.ds(wstart, CHUNK + 2 * HALO), :]
            acc = jnp.broadcast_to(cb, (CHUNK, 512))
            for k in range(CONV_K):
                d = k - CONV_K // 2
                if d == 0:
                    tap = win[HALO:HALO + CHUNK, :]
                else:
                    tap = pltpu.roll(win, (-d) % (CHUNK + 2 * HALO), 0)[HALO:HALO + CHUNK, :]
                acc = acc + cw[k:k + 1, :] * tap
            return _silu(acc)

        dtts = [_softplus(dtr_s[pl.ds(r0, CHUNK), :].T[0:16, :] + bias_col) for r0 in r0s]
        a_ts = [dtt * a_col for dtt in dtts]
        acol_ts = [jnp.where(fwd_row, _dot_exact_lhs(a_t, triu), _dot_exact_lhs(a_t, tril))
                   for a_t in a_ts]
        us = [conv(c) for c in cs]
        xus = [u[:, 0:256] for u in us]
        bts = [u[:, 256:384].T.astype(BF16) for u in us]
        cbfs = [u[:, 384:512].astype(BF16) for u in us]
        xbfs = [xu.astype(BF16) for xu in xus]
        gmats = [_dot(cbf, bt) for cbf, bt in zip(cbfs, bts)]
        tms = []
        for dtt, acol_t in zip(dtts, acol_ts):
            tot = jnp.where(fwd_row1, acol_t[:, CHUNK - 1:CHUNK], acol_t[:, 0:1])
            w_t = dtt * jnp.exp(tot - acol_t)
            e_t = jnp.exp(acol_t)
            stacked = jnp.concatenate(
                [acol_t, w_t, e_t, jnp.zeros((CHUNK - 48, CHUNK), F32)], axis=0)
            tms.append(stacked.T)
        wes = [_dot(tm.astype(BF16), expand) for tm in tms]
        decs = []
        for tm in tms:
            tot_row = jnp.where(lane128_1 < 8, tm[CHUNK - 1:CHUNK, :], tm[0:1, :])
            tot512 = _dot_exact_lhs(jnp.broadcast_to(tot_row, (8, CHUNK)), expand_tot)
            decs.append(jnp.exp(tot512))
        ydiags = [jnp.zeros((CHUNK, 256), F32) for _ in cs]
        for i in range(4):
            in_head = (lane256 >= 64 * i) & (lane256 < 64 * i + 64)
            for j in range(GROUP):
                tm, acol_t, dtt = tms[j], acol_ts[j], dtts[j]
                arg = jnp.where(lower, tm[:, i:i + 1] - acol_t[i:i + 1, :],
                                tm[:, 8 + i:9 + i] - acol_t[8 + i:9 + i, :])
                scale = (jnp.where(lower, dtt[i:i + 1, :], 0.0)
                         + jnp.where(upper, dtt[8 + i:9 + i, :], 0.0))
                wmat = (gmats[j] * jnp.exp(arg) * scale).astype(BF16)
                ydiags[j] = jnp.where(in_head, _dot(wmat, xbfs[j]), ydiags[j])
        for j, c in enumerate(cs):
            w512 = wes[j][:, 0:512]
            e512 = wes[j][:, 512:1024]
            xdw = (jnp.concatenate([xus[j], xus[j]], axis=1) * w512).astype(BF16)
            sloc_s[c] = _dot(bts[j], xdw)
            dec_s[c] = decs[j]
            e_s[c] = e512
            cbf_s[c] = cbfs[j]
            yacc_s[pl.ds(r0s[j], CHUNK), :] = ydiags[j] + xus[j] * dsk
        return carry

    lax.fori_loop(0, nch // GROUP, phase_a, 0)

    fwd_order = list(range(nch))
    bwd_order = list(range(ncc - 1, -1, -1)) + list(range(nch - 1, ncc - 1, -1))
    for order, lo in ((fwd_order, 0), (bwd_order, 256)):
        state = jnp.zeros((SSD_STATE, 256), F32)
        for c in order:
            sin_s[c, :, lo:lo + 256] = state.astype(BF16)
            state = (state * dec_s[c, 0:1, lo:lo + 256]
                     + sloc_s[c, :, lo:lo + 256])

    def phase_c(cs, z_ref, zrows):
        yos = [_dot(cbf_s[c], sin_s[c]) for c in cs]
        vs = []
        for c, zrow, yo in zip(cs, zrows, yos):
            r0 = c * CHUNK if isinstance(c, int) else pl.multiple_of(c * CHUNK, CHUNK)
            yo = yo * e_s[c]
            y = yacc_s[pl.ds(r0, CHUNK), :] + yo[:, 0:256] + yo[:, 256:512]
            vs.append((r0, y * _silu(z_ref[pl.ds(zrow, CHUNK), :])))

        @pl.when(g == 0)
        def _():
            for r0, v in vs:
                vun_s[pl.ds(r0, CHUNK), 0:256] = v

        @pl.when(g == 1)
        def _():
            for r0, v in vs:
                vun_s[pl.ds(r0, CHUNK), 256:512] = v

    if ctx_out:
        phase_c(list(range(ncc)), zc_ref, [c * CHUNK for c in range(ncc)])

    group_c = next(n for n in (4, 2, 1) if ncl % n == 0)

    def phase_c_lat(k, carry):
        ks = [k * group_c + j for j in range(group_c)]
        phase_c([kk + ncc for kk in ks], zl_ref,
                [pl.multiple_of(kk * CHUNK, CHUNK) for kk in ks])
        return carry

    lax.fori_loop(0, ncl // group_c, phase_c_lat, 0)

    @pl.when(g == SSD_GROUPS - 1)
    def _finalize():
        ng = ng_ref[...]

        def norm_rows(r0, nrows):
            v = vun_s[pl.ds(r0, nrows), :]
            ms = jnp.mean(v * v, axis=-1, keepdims=True)
            return (v * lax.rsqrt(ms + EPS) * ng).astype(BF16)

        if ctx_out:
            oc_ref[...] = norm_rows(0, n_ctx)

        def fin(k, carry):
            r0 = pl.multiple_of(k * 256, 256)
            ol_ref[pl.ds(r0, 256), :] = norm_rows(n_ctx + r0, 256)
            return carry

        lax.fori_loop(0, n_lat // 256, fin, 0)


def _ssd_call(p_c, p_l, conv_w8, conv_b, par, dsk, ng, batch, n_ctx, n_lat, ctx_out):
    nch = (n_ctx + n_lat) // CHUNK
    t_all = n_ctx + n_lat
    in_specs = [
        pl.BlockSpec((n_ctx, 256), lambda b, g: (b, g)),
        pl.BlockSpec((n_lat, 256), lambda b, g: (b, g)),
        pl.BlockSpec((n_ctx, 128), lambda b, g: (b, 4 + g)),
        pl.BlockSpec((n_lat, 128), lambda b, g: (b, 4 + g)),
        pl.BlockSpec((n_ctx, 128), lambda b, g: (b, 6 + g)),
        pl.BlockSpec((n_lat, 128), lambda b, g: (b, 6 + g)),
        pl.BlockSpec((n_ctx, 256), lambda b, g: (b, 4 + g)),
        pl.BlockSpec((n_lat, 256), lambda b, g: (b, 4 + g)),
        pl.BlockSpec((n_ctx, 128), lambda b, g: (b, 26 + g)),
        pl.BlockSpec((n_lat, 128), lambda b, g: (b, 26 + g)),
        pl.BlockSpec((8, 256), lambda b, g: (0, g)),
        pl.BlockSpec((8, 128), lambda b, g: (0, 4 + g)),
        pl.BlockSpec((8, 128), lambda b, g: (0, 6 + g)),
        pl.BlockSpec((1, 256), lambda b, g: (0, g)),
        pl.BlockSpec((1, 128), lambda b, g: (0, 4 + g)),
        pl.BlockSpec((1, 128), lambda b, g: (0, 6 + g)),
        pl.BlockSpec((None, 32, 128), lambda b, g: (g, 0, 0)),
        pl.BlockSpec((1, 256), lambda b, g: (0, g)),
        pl.BlockSpec((1, 512), lambda b, g: (0, 0)),
    ]
    args = [p_c, p_l, p_c, p_l, p_c, p_l, p_c, p_l, p_c, p_l,
            conv_w8, conv_w8, conv_w8, conv_b, conv_b, conv_b, par, dsk, ng]
    out_specs = [pl.BlockSpec((n_lat, 512), lambda b, g: (b, 0))]
    out_shape = [jax.ShapeDtypeStruct((batch * n_lat, 512), BF16)]
    if ctx_out:
        out_specs = [pl.BlockSpec((n_ctx, 512), lambda b, g: (b, 0))] + out_specs
        out_shape = [jax.ShapeDtypeStruct((batch * n_ctx, 512), BF16)] + out_shape
    scratch = [
        pltpu.VMEM((t_all + 3 * HALO, 512), F32),
        pltpu.VMEM((t_all, 128), F32),
        pltpu.VMEM((t_all, 256), F32),
        pltpu.VMEM((nch, SSD_STATE, 512), F32),
        pltpu.VMEM((nch, CHUNK, 512), F32),
        pltpu.VMEM((nch, CHUNK, 128), BF16),
        pltpu.VMEM((nch, SSD_STATE, 512), BF16),
        pltpu.VMEM((nch, 8, 512), F32),
        pltpu.VMEM((t_all, 512), F32),
    ]
    outs = pl.pallas_call(
        functools.partial(_ssd_kernel, n_ctx=n_ctx, n_lat=n_lat, ctx_out=ctx_out),
        grid=(batch, SSD_GROUPS),
        in_specs=in_specs,
        out_specs=out_specs,
        out_shape=out_shape,
        scratch_shapes=scratch,
        compiler_params=pltpu.CompilerParams(
            dimension_semantics=("arbitrary", "arbitrary"),
            vmem_limit_bytes=VMEM_LIMIT),
        name="ssd_scan",
    )(*args)
    if ctx_out:
        return outs[1], outs[0]
    return outs[0], None


def _rope_tables(n_lat):
    rows = n_lat // GRID_W
    row_idx = np.repeat(np.arange(rows), GRID_W).astype(np.float32)
    col_idx = (np.arange(rows * GRID_W) % GRID_W).astype(np.float32)

    def tables(dim, reps):
        quarter = dim // 4
        inv = (ROPE_BASE ** (-np.arange(quarter, dtype=np.float32) / quarter)).astype(np.float32)
        ang = np.concatenate([row_idx[:, None] * inv, col_idx[:, None] * inv], axis=-1)
        cos, sin = np.cos(ang.astype(np.float64)), np.sin(ang.astype(np.float64))
        cos2 = np.concatenate([cos, cos], axis=-1).astype(np.float32)
        sin2 = np.concatenate([-sin, sin], axis=-1).astype(np.float32)
        return jnp.asarray(np.tile(cos2, (1, reps))), jnp.asarray(np.tile(sin2, (1, reps)))

    cos_g, sin_g = tables(GQA_HEAD_DIM, 4)
    cos_d, sin_d = tables(DIFF_QK_DIM, 8)
    return {"cos_g": cos_g, "sin_g": sin_g, "cos_d": cos_d, "sin_d": sin_d}


def kernel(x, c, ctx, c_ctx, w_mod, b_mod, g_pre, g_post, w_in, conv_w, conv_b,
           a_log_fwd, a_log_bwd, dt_bias_fwd, dt_bias_bwd, d_skip, ssd_norm_g,
           q_norm_g, k_norm_g, diff_lambda, diff_norm_g, w_out):
    batch, n_lat, _ = x.shape
    n_ctx = ctx.shape[1]
    depth = w_mod.shape[0]
    assert n_lat % 512 == 0 and n_ctx % 256 == 0 and (batch * n_ctx) % 512 == 0
    assert batch + 1 <= 16

    in_perm, out_perm = _in_col_perm(), _out_row_perm()
    w_in_p = [_take_runs(w_in[l], in_perm, 1, IN_COLS).astype(BF16) for l in range(depth)]
    w_out_p = [_take_runs(w_out[l], out_perm, 0, None).astype(BF16) for l in range(depth)]
    conv_w8 = jnp.pad(conv_w, ((0, 0), (0, 8 - CONV_K), (0, 0)))
    conv_b1 = conv_b[:, None, :]

    def group16(fwd, bwd):
        out = jnp.zeros((depth, SSD_GROUPS, 16), F32)
        for g in range(SSD_GROUPS):
            out = out.at[:, g, 0:4].set(fwd[:, 4 * g:4 * g + 4])
            out = out.at[:, g, 8:12].set(bwd[:, 4 * g:4 * g + 4])
        return out

    ssd_par = jnp.broadcast_to(
        jnp.concatenate([group16(a_log_fwd, a_log_bwd),
                         group16(dt_bias_fwd, dt_bias_bwd)], axis=-1)[..., None],
        (depth, SSD_GROUPS, 32, 128))
    dsk = jnp.repeat(d_skip, SSD_HEAD_DIM, axis=1)[:, None, :]
    qg = jnp.tile(q_norm_g, (1, 4))[:, None, :]
    kg = jnp.tile(k_norm_g, (1, 2))[:, None, :]
    dng = jnp.tile(diff_norm_g, (1, 4))[:, None, :]
    tabs = _rope_tables(n_lat)

    cs = jnp.concatenate(
        [c, c_ctx[None, :], jnp.zeros((16 - batch - 1, D_MODEL), F32)], axis=0)
    mod_all = _mod_call(cs, w_mod, b_mod)

    h = x.reshape(batch * n_lat, D_MODEL)
    hc = ctx.reshape(batch * n_ctx, D_MODEL)
    for l in range(depth):
        ctx_out = l < depth - 1
        lam_init = 0.8 - 0.6 * float(np.exp(-0.3 * l))
        mod3 = mod_all[l][:, None, :]
        p_l = _inproj_call(h, mod3, g_pre[l][None, :], w_in_p[l], n_lat // 512, None)
        p_c = _inproj_call(hc, mod3, g_pre[l][None, :], w_in_p[l], None, batch)

        ys_l, ys_c = _ssd_call(p_c, p_l, conv_w8[l], conv_b1[l], ssd_par[l], dsk[l],
                               ssd_norm_g[l][None, :], batch, n_ctx, n_lat, ctx_out)
        yg_l = _gqa_call(p_l, p_c, p_l, tabs, qg[l], kg[l], batch, n_ctx, n_lat, True)
        yd_l = _diff_call(p_l, p_c, p_l, tabs, diff_lambda[l], dng[l], batch, n_ctx,
                          n_lat, True, lam_init)
        h = _outproj_call(ys_l, yg_l, yd_l, h, mod3, g_post[l][None, :], w_out_p[l],
                          n_lat, None)
        if ctx_out:
            yg_c = _gqa_call(p_c, p_c, None, tabs, qg[l], kg[l], batch, n_ctx, 0, False)
            yd_c = _diff_call(p_c, p_c, None, tabs, diff_lambda[l], dng[l], batch,
                              n_ctx, 0, False, lam_init)
            hc = _outproj_call(ys_c, yg_c, yd_c, hc, mod3, g_post[l][None, :],
                               w_out_p[l], None, batch)
    return h.reshape(batch, n_lat, D_MODEL)
```

```python
import functools

import numpy as np
import jax
import jax.numpy as jnp
from jax import lax
from jax.experimental import pallas as pl
from jax.experimental.pallas import tpu as pltpu

F32 = jnp.float32
BF16 = jnp.bfloat16

D_MODEL = 1024
GRID_W = 64
ROPE_BASE = 10000.0
EPS = 1e-6
LOG2E = 1.4426950408889634

SSD_WIDTH = 512
SSD_HEADS = 8
SSD_HEAD_DIM = 64
SSD_GROUPS = 2
SSD_STATE = 128
CHUNK = 128
CONV_K = 5
HALO = 8
GQA_HEADS = 4
GQA_HEAD_DIM = 64
DIFF_HEADS = 4
DIFF_QK_DIM = 32
DIFF_V_DIM = 64

_IN_SPLITS = (("xbc", 1024), ("z", 512), ("dt", 16), ("gq", 256), ("gk", 128),
              ("gv", 128), ("gg", 256), ("dq", 256), ("dk", 256), ("dv", 256),
              ("dg", 256))
IN_COLS = sum(s for _, s in _IN_SPLITS)
NP = 28 * 128
GQ_HEAD_ORDER = (0, 2, 1, 3)

VMEM_LIMIT = 56 * 1024 * 1024
ATTN_TQ = 1024
ATTN_SUB = 512
DIFF_SUB = 512
ATTN_AHEAD = 2
DIFF_AHEAD = 1
SCORE_BOUND_MARGIN = 1.02
MIN_ROW_SUM = 2.0 ** -90
OUTPROJ_TM = 1024
OUTPROJ_SUB = 256


def _in_col_perm():
    off, o = {}, 0
    for name, size in _IN_SPLITS:
        off[name] = o
        o += size
    pad = IN_COLS
    cols = list(range(off["xbc"], off["xbc"] + 1024))
    cols += list(range(off["z"], off["z"] + 512))
    for h in GQ_HEAD_ORDER:
        cols += list(range(off["gq"] + 64 * h, off["gq"] + 64 * h + 64))
    cols += list(range(off["gk"], off["gk"] + 128))
    cols += list(range(off["gv"], off["gv"] + 128))
    for h in GQ_HEAD_ORDER:
        cols += list(range(off["gg"] + 64 * h, off["gg"] + 64 * h + 64))
    for name in ("dq", "dk", "dv", "dg"):
        cols += list(range(off[name], off[name] + 256))
    for g in range(SSD_GROUPS):
        blk = [pad] * 128
        for i in range(4):
            blk[i] = off["dt"] + 4 * g + i
            blk[8 + i] = off["dt"] + SSD_HEADS + 4 * g + i
        cols += blk
    assert len(cols) == NP
    return np.asarray(cols, np.int32)


def _out_row_perm():
    rows = list(range(SSD_WIDTH))
    for h in GQ_HEAD_ORDER:
        rows += list(range(SSD_WIDTH + 64 * h, SSD_WIDTH + 64 * h + 64))
    rows += list(range(SSD_WIDTH + 256, SSD_WIDTH + 512))
    return np.asarray(rows, np.int32)


def _take_runs(arr, idx, axis, pad_index):
    pieces, start = [], 0
    idx = [int(i) for i in idx]
    while start < len(idx):
        end = start + 1
        if idx[start] == pad_index:
            while end < len(idx) and idx[end] == pad_index:
                end += 1
            shape = list(arr.shape)
            shape[axis] = end - start
            pieces.append(jnp.zeros(shape, arr.dtype))
        else:
            while end < len(idx) and idx[end] == idx[end - 1] + 1 and idx[end] != pad_index:
                end += 1
            pieces.append(lax.slice_in_dim(arr, idx[start], idx[end - 1] + 1, axis=axis))
        start = end
    return jnp.concatenate(pieces, axis=axis)


def _dot(a, b):
    return jnp.dot(a, b, preferred_element_type=F32)


def _split3(x):
    hi = x.astype(BF16)
    r1 = x - hi.astype(F32)
    mid = r1.astype(BF16)
    lo = (r1 - mid.astype(F32)).astype(BF16)
    return hi, mid, lo


def _dot_exact_lhs(x, m_bf16):
    hi, mid, lo = _split3(x)
    return _dot(hi, m_bf16) + _dot(mid, m_bf16) + _dot(lo, m_bf16)


def _silu(x):
    return x * jax.nn.sigmoid(x)


def _seg_ones(width, seg):
    r = lax.broadcasted_iota(jnp.int32, (width, width), 0)
    c = lax.broadcasted_iota(jnp.int32, (width, width), 1)
    same = (r & ~(seg - 1)) == (c & ~(seg - 1))
    return jnp.where(same, 1.0, 0.0).astype(BF16)


def _seg_rms(x, seg, seg_mat):
    ss = _dot_exact_lhs(x * x, seg_mat)
    return x * lax.rsqrt(ss * (1.0 / seg) + EPS)


def _rope(x, cos, sin_signed, half):
    w = x.shape[-1]
    lane = lax.broadcasted_iota(jnp.int32, x.shape, 1)
    first = (lane & (2 * half - 1)) < half
    swapped = jnp.where(first, pltpu.roll(x, w - half, 1), pltpu.roll(x, half, 1))
    return x * cos + swapped * sin_signed


def _mod_kernel(cs_ref, w_ref, b_ref, o_ref):
    s = _silu(cs_ref[...]).astype(BF16)
    o_ref[...] = _dot(s, w_ref[...].astype(BF16)) + b_ref[...]


def _mod_call(cs, w_mod, b_mod):
    depth = w_mod.shape[0]
    nrow = cs.shape[0]
    tn = 1024
    return pl.pallas_call(
        _mod_kernel,
        grid=(depth, 3 * D_MODEL // tn),
        in_specs=[
            pl.BlockSpec((nrow, D_MODEL), lambda l, j: (0, 0)),
            pl.BlockSpec((None, D_MODEL, tn), lambda l, j: (l, 0, j)),
            pl.BlockSpec((None, 1, tn), lambda l, j: (l, 0, j)),
        ],
        out_specs=pl.BlockSpec((None, nrow, tn), lambda l, j: (l, 0, j)),
        out_shape=jax.ShapeDtypeStruct((depth, nrow, 3 * D_MODEL), F32),
        compiler_params=pltpu.CompilerParams(
            dimension_semantics=("arbitrary", "arbitrary")),
        name="mod_proj",
    )(cs, w_mod, b_mod.reshape(depth, 1, 3 * D_MODEL))


def _inproj_kernel(h_ref, mod_ref, g_ref, w_ref, o_ref):
    x = h_ref[...]
    ms = jnp.mean(x * x, axis=-1, keepdims=True)
    y = x * lax.rsqrt(ms + EPS) * g_ref[...]
    sh = mod_ref[:, 0:D_MODEL]
    sc = mod_ref[:, D_MODEL:2 * D_MODEL]
    u = (y * (1.0 + sc) + sh).astype(BF16)
    tn = 512
    for j in range(NP // tn):
        o_ref[:, j * tn:(j + 1) * tn] = _dot(u, w_ref[:, j * tn:(j + 1) * tn])


def _inproj_call(h, mod3, g_pre, w_bf16, tiles_per_row, fixed_row):
    n_tok = h.shape[0]
    tm = 512
    if fixed_row is None:
        mod_idx = lambda i: (i // tiles_per_row, 0, 0)
    else:
        mod_idx = lambda i: (fixed_row, 0, 0)
    return pl.pallas_call(
        _inproj_kernel,
        grid=(n_tok // tm,),
        in_specs=[
            pl.BlockSpec((tm, D_MODEL), lambda i: (i, 0)),
            pl.BlockSpec((None, 1, 3 * D_MODEL), mod_idx),
            pl.BlockSpec((1, D_MODEL), lambda i: (0, 0)),
            pl.BlockSpec((D_MODEL, NP), lambda i: (0, 0)),
        ],
        out_specs=pl.BlockSpec((tm, NP), lambda i: (i, 0)),
        out_shape=jax.ShapeDtypeStruct((n_tok, NP), F32),
        compiler_params=pltpu.CompilerParams(
            dimension_semantics=("arbitrary",), vmem_limit_bytes=VMEM_LIMIT),
        name="in_proj",
    )(h, mod3, g_pre, w_bf16)


def _outproj_kernel(ys_ref, yg_ref, yd_ref, h_ref, mod_ref, g_ref, w_ref, o_ref):
    tm = h_ref.shape[0]
    sub = min(OUTPROJ_SUB, tm)
    gt = mod_ref[:, 2 * D_MODEL:3 * D_MODEL]
    gain = g_ref[...]

    def project(r0):
        return (_dot(ys_ref[r0:r0 + sub, :], w_ref[0:512, :])
                + _dot(yg_ref[r0:r0 + sub, :], w_ref[512:768, :])
                + _dot(yd_ref[r0:r0 + sub, :], w_ref[768:1024, :]))

    o_next = project(0)
    for r0 in range(0, tm, sub):
        o = o_next
        if r0 + sub < tm:
            o_next = project(r0 + sub)
        ms = jnp.mean(o * o, axis=-1, keepdims=True)
        n = o * lax.rsqrt(ms + EPS) * gain
        o_ref[r0:r0 + sub, :] = h_ref[r0:r0 + sub, :] + gt * n


def _outproj_call(ys, yg, yd, h, mod3, g_post, w_bf16, rows_per_mod, fixed_row):
    n_tok = h.shape[0]
    tm = OUTPROJ_TM
    assert n_tok % tm == 0
    if fixed_row is None:
        assert rows_per_mod % tm == 0
        mod_idx = lambda i: (i // (rows_per_mod // tm), 0, 0)
    else:
        mod_idx = lambda i: (fixed_row, 0, 0)
    return pl.pallas_call(
        _outproj_kernel,
        grid=(n_tok // tm,),
        in_specs=[
            pl.BlockSpec((tm, 512), lambda i: (i, 0)),
            pl.BlockSpec((tm, 256), lambda i: (i, 0)),
            pl.BlockSpec((tm, 256), lambda i: (i, 0)),
            pl.BlockSpec((tm, D_MODEL), lambda i: (i, 0)),
            pl.BlockSpec((None, 1, 3 * D_MODEL), mod_idx),
            pl.BlockSpec((1, D_MODEL), lambda i: (0, 0)),
            pl.BlockSpec((D_MODEL, D_MODEL), lambda i: (0, 0)),
        ],
        out_specs=pl.BlockSpec((tm, D_MODEL), lambda i: (i, 0)),
        out_shape=jax.ShapeDtypeStruct((n_tok, D_MODEL), F32),
        compiler_params=pltpu.CompilerParams(
            dimension_semantics=("arbitrary",), vmem_limit_bytes=VMEM_LIMIT),
        name="out_proj",
    )(ys, yg, yd, h, mod3, g_post, w_bf16)


def _attend_many(lhs_list, kt_ref, vext_refs):
    def scores(i):
        s = _dot(lhs_list[i], kt_ref[...])
        return s, jnp.max(s, axis=-1, keepdims=True)

    outs = []
    n = len(lhs_list)
    ahead = [scores(i) for i in range(min(ATTN_AHEAD, n))]
    for i, vext_ref in enumerate(vext_refs):
        s, m = ahead.pop(0)
        if i + ATTN_AHEAD < n:
            ahead.append(scores(i + ATTN_AHEAD))
        p = jnp.exp2(s - m).astype(BF16)
        oe = _dot(p, vext_ref[...])
        outs.append(oe[:, 0:128] / oe[:, 128:256])
    return outs


def _attend_diff_pairs(lhs_list, kt_ref, v_refs, lam):
    n_heads = len(v_refs)

    def scores(h):
        s_a = _dot(lhs_list[2 * h], kt_ref[...])
        m_a = jnp.max(s_a, axis=-1, keepdims=True)
        s_b = _dot(lhs_list[2 * h + 1], kt_ref[...])
        m_b = jnp.max(s_b, axis=-1, keepdims=True)
        return s_a, m_a, s_b, m_b

    outs = []
    ahead = [scores(h) for h in range(min(DIFF_AHEAD, n_heads))]
    for h in range(n_heads):
        s_a, m_a, s_b, m_b = ahead.pop(0)
        if h + DIFF_AHEAD < n_heads:
            ahead.append(scores(h + DIFF_AHEAD))
        e_a = jnp.exp2(s_a - m_a)
        e_b = jnp.exp2(s_b - m_b)
        l_a = jnp.sum(e_a, axis=-1, keepdims=True)
        l_b = jnp.sum(e_b, axis=-1, keepdims=True)
        pc = (e_a - (lam * l_a / l_b) * e_b).astype(BF16)
        outs.append(_dot(pc, v_refs[h][...]) / l_a)
    return outs


def _attend_diff_pairs_bounded(lhs_list, bounds, kt_ref, v_refs, lam):
    n_heads = len(v_refs)

    def exps(h):
        e_a = jnp.exp2(_dot(lhs_list[2 * h], kt_ref[...]) - bounds[2 * h])
        e_b = jnp.exp2(_dot(lhs_list[2 * h + 1], kt_ref[...]) - bounds[2 * h + 1])
        return e_a, e_b

    outs, l_min = [], None
    ahead = [exps(h) for h in range(min(DIFF_AHEAD, n_heads))]
    for h in range(n_heads):
        e_a, e_b = ahead.pop(0)
        if h + DIFF_AHEAD < n_heads:
            ahead.append(exps(h + DIFF_AHEAD))
        l_a = jnp.sum(e_a, axis=-1, keepdims=True)
        l_b = jnp.sum(e_b, axis=-1, keepdims=True)
        pc = (e_a - (lam * l_a / l_b) * e_b).astype(BF16)
        outs.append(_dot(pc, v_refs[h][...]) / l_a)
        l_ab = jnp.minimum(l_a, l_b)
        l_min = l_ab if l_min is None else jnp.minimum(l_min, l_ab)
    return outs, jnp.min(l_min, axis=0, keepdims=True)


def _gqa_kernel(*refs, n_ctx, n_lat, rope_q):
    it = iter(refs)
    q_ref, gg_ref, kvc_ref = next(it), next(it), next(it)
    kvl_ref = next(it) if n_lat else None
    if rope_q:
        cosq_ref, sinq_ref = next(it), next(it)
    if n_lat:
        cosk_ref, sink_ref = next(it), next(it)
    qg_ref, kg_ref = next(it), next(it)
    y_ref = next(it)
    kt_s, vext_s = next(it), next(it)

    seg128 = _seg_ones(128, 64)

    @pl.when(pl.program_id(1) == 0)
    def _prep_kv():
        kc = _seg_rms(kvc_ref[:, 0:128], 64, seg128) * kg_ref[...]
        kt_s[:, 0:n_ctx] = kc.T.astype(BF16)
        vext_s[0:n_ctx, 0:128] = kvc_ref[:, 128:256].astype(BF16)
        if n_lat:
            kl = _seg_rms(kvl_ref[:, 0:128], 64, seg128) * kg_ref[...]
            kl = _rope(kl, cosk_ref[...], sink_ref[...], 32)
            kt_s[:, n_ctx:n_ctx + n_lat] = kl.T.astype(BF16)
            vext_s[n_ctx:n_ctx + n_lat, 0:128] = kvl_ref[:, 128:256].astype(BF16)
        vext_s[:, 128:256] = jnp.ones((n_ctx + n_lat, 128), BF16)

    seg256 = _seg_ones(256, 64)
    q = _seg_rms(q_ref[...], 64, seg256) * qg_ref[...]
    if rope_q:
        q = _rope(q, cosq_ref[...], sinq_ref[...], 32)
    q = q * (GQA_HEAD_DIM ** -0.5 * LOG2E)
    tq = q.shape[0]
    sub = min(ATTN_SUB, tq)
    lane = lax.broadcasted_iota(jnp.int32, (sub, 128), 1)
    lhs_list = []
    for r0 in range(0, tq, sub):
        for half in range(2):
            qh = q[r0:r0 + sub, 128 * half:128 * half + 128]
            for kv in range(2):
                in_kv = (lane >= 64 * kv) & (lane < 64 * kv + 64)
                lhs_list.append(jnp.where(in_kv, qh, 0.0).astype(BF16))
    outs = _attend_many(lhs_list, kt_s, [vext_s] * len(lhs_list))
    for j, r0 in enumerate(range(0, tq, sub)):
        for half in range(2):
            o = jnp.where(lane < 64, outs[4 * j + 2 * half], outs[4 * j + 2 * half + 1])
            gate = _silu(gg_ref[r0:r0 + sub, 128 * half:128 * half + 128])
            y_ref[r0:r0 + sub, 128 * half:128 * half + 128] = (o * gate).astype(BF16)


def _gqa_call(p_q, p_c, p_l, tabs, qg, kg, batch, n_ctx, n_lat, rope_q):
    t_total = p_q.shape[0] // batch
    tq = min(ATTN_TQ, t_total)
    nq = t_total // tq
    in_specs = [
        pl.BlockSpec((tq, 256), lambda b, i: (b * nq + i, 6)),
        pl.BlockSpec((tq, 256), lambda b, i: (b * nq + i, 8)),
        pl.BlockSpec((n_ctx, 256), lambda b, i: (b, 7)),
    ]
    args = [p_q, p_q, p_c]
    if n_lat:
        in_specs.append(pl.BlockSpec((n_lat, 256), lambda b, i: (b, 7)))
        args.append(p_l)
    if rope_q:
        in_specs += [pl.BlockSpec((tq, 256), lambda b, i: (i, 0))] * 2
        args += [tabs["cos_g"], tabs["sin_g"]]
    if n_lat:
        in_specs += [pl.BlockSpec((n_lat, 128), lambda b, i: (0, 0))] * 2
        args += [tabs["cos_g"], tabs["sin_g"]]
    in_specs += [pl.BlockSpec((1, 256), lambda b, i: (0, 0)),
                 pl.BlockSpec((1, 128), lambda b, i: (0, 0))]
    args += [qg, kg]
    s_keys = n_ctx + n_lat
    return pl.pallas_call(
        functools.partial(_gqa_kernel, n_ctx=n_ctx, n_lat=n_lat, rope_q=rope_q),
        grid=(batch, nq),
        in_specs=in_specs,
        out_specs=pl.BlockSpec((tq, 256), lambda b, i: (b * nq + i, 0)),
        out_shape=jax.ShapeDtypeStruct((p_q.shape[0], 256), BF16),
        scratch_shapes=[pltpu.VMEM((128, s_keys), BF16),
                        pltpu.VMEM((s_keys, 256), BF16)],
        compiler_params=pltpu.CompilerParams(
            dimension_semantics=("arbitrary", "arbitrary"),
            vmem_limit_bytes=VMEM_LIMIT),
        name="gqa_attn",
    )(*args)


def _diff_kernel(*refs, n_ctx, n_lat, rope_q, lam_init, bounded):
    it = iter(refs)
    q_ref, dg_ref, kc_ref, vc_ref = next(it), next(it), next(it), next(it)
    if n_lat:
        kl_ref, vl_ref = next(it), next(it)
    if rope_q:
        cosq_ref, sinq_ref = next(it), next(it)
    if n_lat:
        cosk_ref, sink_ref = next(it), next(it)
    lam_ref, ng_ref = next(it), next(it)
    y_ref = next(it)
    lmin_ref = next(it) if bounded else None
    kt_s, vlo_s, vhi_s = next(it), next(it), next(it)
    kmax_s = next(it) if bounded else None
    s_keys = n_ctx + n_lat
    seg32 = _seg_ones(256, DIFF_QK_DIM)

    def map_norms(x):
        return jnp.sqrt(_dot((x * x).astype(BF16), seg32))

    @pl.when(pl.program_id(1) == 0)
    def _prep_kv():
        kc = kc_ref[...]
        kt_s[:, 0:n_ctx] = kc.T.astype(BF16)
        vlo_s[0:n_ctx, 0:128] = vc_ref[:, 0:128].astype(BF16)
        vhi_s[0:n_ctx, 0:128] = vc_ref[:, 128:256].astype(BF16)
        if bounded:
            kmax = jnp.max(map_norms(kc), axis=0, keepdims=True)
        if n_lat:
            kl = _rope(kl_ref[...], cosk_ref[...], sink_ref[...], 16)
            kt_s[:, n_ctx:s_keys] = kl.T.astype(BF16)
            vlo_s[n_ctx:s_keys, 0:128] = vl_ref[:, 0:128].astype(BF16)
            vhi_s[n_ctx:s_keys, 0:128] = vl_ref[:, 128:256].astype(BF16)
            if bounded:
                kmax = jnp.maximum(kmax, jnp.max(map_norms(kl), axis=0, keepdims=True))
        if bounded:
            kmax_s[...] = jnp.broadcast_to(kmax, kmax_s.shape)

    lp = lam_ref[...]
    lam = (jnp.exp(jnp.sum(lp[0:1, :] * lp[1:2, :], axis=-1, keepdims=True))
           - jnp.exp(jnp.sum(lp[2:3, :] * lp[3:4, :], axis=-1, keepdims=True))
           + lam_init)

    q = q_ref[...]
    if rope_q:
        q = _rope(q, cosq_ref[...], sinq_ref[...], 16)
    q = q * (DIFF_QK_DIM ** -0.5 * LOG2E)
    tq = q.shape[0]
    sub = min(DIFF_SUB, tq)
    lane256 = lax.broadcasted_iota(jnp.int32, (sub, 256), 1)
    lane128 = lax.broadcasted_iota(jnp.int32, (sub, 128), 1)
    seg128 = _seg_ones(128, 64)
    if bounded:
        bound_all = map_norms(q) * kmax_s[0:1, :] * SCORE_BOUND_MARGIN
    lhs_list, v_list, bounds = [], [], []
    for r0 in range(0, tq, sub):
        for mp in range(2 * DIFF_HEADS):
            in_map = (lane256 >= 32 * mp) & (lane256 < 32 * mp + 32)
            lhs_list.append(jnp.where(in_map, q[r0:r0 + sub, :], 0.0).astype(BF16))
            if bounded:
                bounds.append(bound_all[r0:r0 + sub, 32 * mp:32 * mp + 1])
        v_list += [vlo_s, vlo_s, vhi_s, vhi_s]

    if bounded:
        heads, l_min = _attend_diff_pairs_bounded(lhs_list, bounds, kt_s, v_list, lam)
        lmin_ref[...] = jnp.broadcast_to(l_min, lmin_ref.shape)
    else:
        heads = _attend_diff_pairs(lhs_list, kt_s, v_list, lam)
    for j, r0 in enumerate(range(0, tq, sub)):
        for half in range(2):
            o = jnp.where(lane128 < 64, heads[4 * j + 2 * half], heads[4 * j + 2 * half + 1])
            n = _seg_rms(o, 64, seg128) * ng_ref[:, 128 * half:128 * half + 128]
            n = n * (1.0 - lam_init)
            gate = _silu(dg_ref[r0:r0 + sub, 128 * half:128 * half + 128])
            y_ref[r0:r0 + sub, 128 * half:128 * half + 128] = (n * gate).astype(BF16)


def _diff_call(p_q, p_c, p_l, tabs, lam_params, ng, batch, n_ctx, n_lat, rope_q,
               lam_init, bounded):
    t_total = p_q.shape[0] // batch
    tq = min(ATTN_TQ, t_total)
    nq = t_total // tq
    in_specs = [
        pl.BlockSpec((tq, 256), lambda b, i: (b * nq + i, 9)),
        pl.BlockSpec((tq, 256), lambda b, i: (b * nq + i, 12)),
        pl.BlockSpec((n_ctx, 256), lambda b, i: (b, 10)),
        pl.BlockSpec((n_ctx, 256), lambda b, i: (b, 11)),
    ]
    args = [p_q, p_q, p_c, p_c]
    if n_lat:
        in_specs += [pl.BlockSpec((n_lat, 256), lambda b, i: (b, 10)),
                     pl.BlockSpec((n_lat, 256), lambda b, i: (b, 11))]
        args += [p_l, p_l]
    if rope_q:
        in_specs += [pl.BlockSpec((tq, 256), lambda b, i: (i, 0))] * 2
        args += [tabs["cos_d"], tabs["sin_d"]]
    if n_lat:
        in_specs += [pl.BlockSpec((n_lat, 256), lambda b, i: (0, 0))] * 2
        args += [tabs["cos_d"], tabs["sin_d"]]
    in_specs += [pl.BlockSpec((4, DIFF_QK_DIM), lambda b, i: (0, 0)),
                 pl.BlockSpec((1, 256), lambda b, i: (0, 0))]
    args += [lam_params, ng]
    s_keys = n_ctx + n_lat
    out_specs = [pl.BlockSpec((tq, 256), lambda b, i: (b * nq + i, 0))]
    out_shape = [jax.ShapeDtypeStruct((p_q.shape[0], 256), BF16)]
    scratch = [pltpu.VMEM((256, s_keys), BF16),
               pltpu.VMEM((s_keys, 128), BF16),
               pltpu.VMEM((s_keys, 128), BF16)]
    if bounded:
        out_specs.append(pl.BlockSpec((None, 8, 128), lambda b, i: (b * nq + i, 0, 0)))
        out_shape.append(jax.ShapeDtypeStruct((batch * nq, 8, 128), F32))
        scratch.append(pltpu.VMEM((8, 256), F32))
    outs = pl.pallas_call(
        functools.partial(_diff_kernel, n_ctx=n_ctx, n_lat=n_lat, rope_q=rope_q,
                          lam_init=lam_init, bounded=bounded),
        grid=(batch, nq),
        in_specs=in_specs,
        out_specs=out_specs,
        out_shape=out_shape,
        scratch_shapes=scratch,
        compiler_params=pltpu.CompilerParams(
            dimension_semantics=("arbitrary", "arbitrary"),
            vmem_limit_bytes=VMEM_LIMIT),
        name="diff_attn_bounded" if bounded else "diff_attn",
    )(*args)
    return (outs[0], outs[1]) if bounded else outs[0]


def _ssd_kernel(xc_ref, xl_ref, bc_ref, bl_ref, cc_ref, cl_ref, zc_ref, zl_ref,
                dtc_ref, dtl_ref, cwx_ref, cwb_ref, cwc_ref, cbx_ref, cbb_ref,
                cbc_ref, par_ref, dsk_ref, ng_ref, *rest, n_ctx, n_lat, ctx_out):
    if ctx_out:
        oc_ref, ol_ref = rest[0], rest[1]
        rest = rest[2:]
    else:
        oc_ref, ol_ref = None, rest[0]
        rest = rest[1:]
    xp_s, dtr_s, yacc_s, sloc_s, e_s, cbf_s, sin_s, dec_s, vun_s = rest

    g = pl.program_id(1)
    ncc = n_ctx // CHUNK
    ncl = n_lat // CHUNK
    nch = ncc + ncl
    GROUP = next(n for n in (6, 3, 2, 1) if nch % n == 0)
    t_all = n_ctx + n_lat
    lat0 = n_ctx + 2 * HALO

    zeros_h = jnp.zeros((HALO, 512), F32)
    xp_s[0:HALO, :] = zeros_h
    xp_s[HALO:HALO + n_ctx, 0:256] = xc_ref[...]
    xp_s[HALO:HALO + n_ctx, 256:384] = bc_ref[...]
    xp_s[HALO:HALO + n_ctx, 384:512] = cc_ref[...]
    xp_s[HALO + n_ctx:lat0, :] = zeros_h
    xp_s[lat0:lat0 + n_lat, 0:256] = xl_ref[...]
    xp_s[lat0:lat0 + n_lat, 256:384] = bl_ref[...]
    xp_s[lat0:lat0 + n_lat, 384:512] = cl_ref[...]
    xp_s[lat0 + n_lat:lat0 + n_lat + HALO, :] = zeros_h

    dtr_s[0:n_ctx, :] = dtc_ref[...]
    dtr_s[n_ctx:t_all, :] = dtl_ref[...]
    a_col = -jnp.exp(par_ref[0:16, :])
    bias_col = par_ref[16:32, :]

    def _softplus(v):
        return jnp.maximum(v, 0.0) + jnp.log1p(jnp.exp(-jnp.abs(v)))

    r128 = lax.broadcasted_iota(jnp.int32, (CHUNK, CHUNK), 0)
    c128 = lax.broadcasted_iota(jnp.int32, (CHUNK, CHUNK), 1)
    lower = c128 <= r128
    upper = c128 >= r128
    tril = jnp.where(lower, 1.0, 0.0).astype(BF16)
    triu = jnp.where(upper, 1.0, 0.0).astype(BF16)
    fwd_row = lax.broadcasted_iota(jnp.int32, (16, CHUNK), 0) < 8
    fwd_row1 = lax.broadcasted_iota(jnp.int32, (16, 1), 0) < 8
    er = lax.broadcasted_iota(jnp.int32, (CHUNK, 1024), 0)
    ec = lax.broadcasted_iota(jnp.int32, (CHUNK, 1024), 1)
    src_lane = 16 + 16 * (ec >> 9) + 8 * ((ec >> 8) & 1) + ((ec >> 6) & 3)
    expand = jnp.where(er == src_lane, 1.0, 0.0).astype(BF16)
    er0 = lax.broadcasted_iota(jnp.int32, (CHUNK, 512), 0)
    ec0 = lax.broadcasted_iota(jnp.int32, (CHUNK, 512), 1)
    expand_tot = jnp.where(er0 == 8 * (ec0 >> 8) + ((ec0 >> 6) & 3), 1.0, 0.0).astype(BF16)
    lane256 = lax.broadcasted_iota(jnp.int32, (CHUNK, 256), 1)
    lane128_1 = lax.broadcasted_iota(jnp.int32, (1, CHUNK), 1)

    cw = jnp.concatenate([cwx_ref[...], cwb_ref[...], cwc_ref[...]], axis=1)
    cb = jnp.concatenate([cbx_ref[...], cbb_ref[...], cbc_ref[...]], axis=1)
    dsk = dsk_ref[...]

    def phase_a(grp, carry):
        cs = [grp * GROUP + j for j in range(GROUP)]
        r0s = [pl.multiple_of(c * CHUNK, CHUNK) for c in cs]

        def conv(c):
            wstart = pl.multiple_of(c * CHUNK + jnp.where(c >= ncc, HALO, 0), 8)
            win = xp_s[pl.ds(wstart, CHUNK + 2 * HALO), :]
            acc = jnp.broadcast_to(cb, (CHUNK, 512))
            for k in range(CONV_K):
                d = k - CONV_K // 2
                if d == 0:
                    tap = win[HALO:HALO + CHUNK, :]
                else:
                    tap = pltpu.roll(win, (-d) % (CHUNK + 2 * HALO), 0)[HALO:HALO + CHUNK, :]
                acc = acc + cw[k:k + 1, :] * tap
            return _silu(acc)

        dtts = [_softplus(dtr_s[pl.ds(r0, CHUNK), :].T[0:16, :] + bias_col) for r0 in r0s]
        a_ts = [dtt * a_col for dtt in dtts]
        acol_ts = [jnp.where(fwd_row, _dot_exact_lhs(a_t, triu), _dot_exact_lhs(a_t, tril))
                   for a_t in a_ts]
        us = [conv(c) for c in cs]
        xus = [u[:, 0:256] for u in us]
        bts = [u[:, 256:384].T.astype(BF16) for u in us]
        cbfs = [u[:, 384:512].astype(BF16) for u in us]
        xbfs = [xu.astype(BF16) for xu in xus]
        gmats = [_dot(cbf, bt) for cbf, bt in zip(cbfs, bts)]
        tms = []
        for dtt, acol_t in zip(dtts, acol_ts):
            tot = jnp.where(fwd_row1, acol_t[:, CHUNK - 1:CHUNK], acol_t[:, 0:1])
            w_t = dtt * jnp.exp(tot - acol_t)
            e_t = jnp.exp(acol_t)
            stacked = jnp.concatenate(
                [acol_t, w_t, e_t, jnp.zeros((CHUNK - 48, CHUNK), F32)], axis=0)
            tms.append(stacked.T)
        wes = [_dot(tm.astype(BF16), expand) for tm in tms]
        decs = []
        for tm in tms:
            tot_row = jnp.where(lane128_1 < 8, tm[CHUNK - 1:CHUNK, :], tm[0:1, :])
            tot512 = _dot_exact_lhs(jnp.broadcast_to(tot_row, (8, CHUNK)), expand_tot)
            decs.append(jnp.exp(tot512))
        ydiags = [jnp.zeros((CHUNK, 256), F32) for _ in cs]
        for i in range(4):
            in_head = (lane256 >= 64 * i) & (lane256 < 64 * i + 64)
            for j in range(GROUP):
                tm, acol_t, dtt = tms[j], acol_ts[j], dtts[j]
                arg = jnp.where(lower, tm[:, i:i + 1] - acol_t[i:i + 1, :],
                                tm[:, 8 + i:9 + i] - acol_t[8 + i:9 + i, :])
                scale = (jnp.where(lower, dtt[i:i + 1, :], 0.0)
                         + jnp.where(upper, dtt[8 + i:9 + i, :], 0.0))
                wmat = (gmats[j] * jnp.exp(arg) * scale).astype(BF16)
                ydiags[j] = jnp.where(in_head, _dot(wmat, xbfs[j]), ydiags[j])
        for j, c in enumerate(cs):
            w512 = wes[j][:, 0:512]
            e512 = wes[j][:, 512:1024]
            xdw = (jnp.concatenate([xus[j], xus[j]], axis=1) * w512).astype(BF16)
            sloc_s[c] = _dot(bts[j], xdw)
            dec_s[c] = decs[j]
            e_s[c] = e512
            cbf_s[c] = cbfs[j]
            yacc_s[pl.ds(r0s[j], CHUNK), :] = ydiags[j] + xus[j] * dsk
        return carry

    lax.fori_loop(0, nch // GROUP, phase_a, 0)

    fwd_order = list(range(nch))
    bwd_order = list(range(ncc - 1, -1, -1)) + list(range(nch - 1, ncc - 1, -1))
    for order, lo in ((fwd_order, 0), (bwd_order, 256)):
        state = jnp.zeros((SSD_STATE, 256), F32)
        for c in order:
            sin_s[c, :, lo:lo + 256] = state.astype(BF16)
            state = (state * dec_s[c, 0:1, lo:lo + 256]
                     + sloc_s[c, :, lo:lo + 256])

    def phase_c(cs, z_ref, zrows):
        yos = [_dot(cbf_s[c], sin_s[c]) for c in cs]
        vs = []
        for c, zrow, yo in zip(cs, zrows, yos):
            r0 = c * CHUNK if isinstance(c, int) else pl.multiple_of(c * CHUNK, CHUNK)
            yo = yo * e_s[c]
            y = yacc_s[pl.ds(r0, CHUNK), :] + yo[:, 0:256] + yo[:, 256:512]
            vs.append((r0, y * _silu(z_ref[pl.ds(zrow, CHUNK), :])))

        @pl.when(g == 0)
        def _():
            for r0, v in vs:
                vun_s[pl.ds(r0, CHUNK), 0:256] = v

        @pl.when(g == 1)
        def _():
            for r0, v in vs:
                vun_s[pl.ds(r0, CHUNK), 256:512] = v

    if ctx_out:
        phase_c(list(range(ncc)), zc_ref, [c * CHUNK for c in range(ncc)])

    group_c = next(n for n in (4, 2, 1) if ncl % n == 0)

    def phase_c_lat(k, carry):
        ks = [k * group_c + j for j in range(group_c)]
        phase_c([kk + ncc for kk in ks], zl_ref,
                [pl.multiple_of(kk * CHUNK, CHUNK) for kk in ks])
        return carry

    lax.fori_loop(0, ncl // group_c, phase_c_lat, 0)

    @pl.when(g == SSD_GROUPS - 1)
    def _finalize():
        ng = ng_ref[...]

        def norm_rows(r0, nrows):
            v = vun_s[pl.ds(r0, nrows), :]
            ms = jnp.mean(v * v, axis=-1, keepdims=True)
            return (v * lax.rsqrt(ms + EPS) * ng).astype(BF16)

        if ctx_out:
            oc_ref[...] = norm_rows(0, n_ctx)

        def fin(k, carry):
            r0 = pl.multiple_of(k * 256, 256)
            ol_ref[pl.ds(r0, 256), :] = norm_rows(n_ctx + r0, 256)
            return carry

        lax.fori_loop(0, n_lat // 256, fin, 0)


def _ssd_call(p_c, p_l, conv_w8, conv_b, par, dsk, ng, batch, n_ctx, n_lat, ctx_out):
    nch = (n_ctx + n_lat) // CHUNK
    t_all = n_ctx + n_lat
    in_specs = [
        pl.BlockSpec((n_ctx, 256), lambda b, g: (b, g)),
        pl.BlockSpec((n_lat, 256), lambda b, g: (b, g)),
        pl.BlockSpec((n_ctx, 128), lambda b, g: (b, 4 + g)),
        pl.BlockSpec((n_lat, 128), lambda b, g: (b, 4 + g)),
        pl.BlockSpec((n_ctx, 128), lambda b, g: (b, 6 + g)),
        pl.BlockSpec((n_lat, 128), lambda b, g: (b, 6 + g)),
        pl.BlockSpec((n_ctx, 256), lambda b, g: (b, 4 + g)),
        pl.BlockSpec((n_lat, 256), lambda b, g: (b, 4 + g)),
        pl.BlockSpec((n_ctx, 128), lambda b, g: (b, 26 + g)),
        pl.BlockSpec((n_lat, 128), lambda b, g: (b, 26 + g)),
        pl.BlockSpec((8, 256), lambda b, g: (0, g)),
        pl.BlockSpec((8, 128), lambda b, g: (0, 4 + g)),
        pl.BlockSpec((8, 128), lambda b, g: (0, 6 + g)),
        pl.BlockSpec((1, 256), lambda b, g: (0, g)),
        pl.BlockSpec((1, 128), lambda b, g: (0, 4 + g)),
        pl.BlockSpec((1, 128), lambda b, g: (0, 6 + g)),
        pl.BlockSpec((None, 32, 128), lambda b, g: (g, 0, 0)),
        pl.BlockSpec((1, 256), lambda b, g: (0, g)),
        pl.BlockSpec((1, 512), lambda b, g: (0, 0)),
    ]
    args = [p_c, p_l, p_c, p_l, p_c, p_l, p_c, p_l, p_c, p_l,
            conv_w8, conv_w8, conv_w8, conv_b, conv_b, conv_b, par, dsk, ng]
    out_specs = [pl.BlockSpec((n_lat, 512), lambda b, g: (b, 0))]
    out_shape = [jax.ShapeDtypeStruct((batch * n_lat, 512), BF16)]
    if ctx_out:
        out_specs = [pl.BlockSpec((n_ctx, 512), lambda b, g: (b, 0))] + out_specs
        out_shape = [jax.ShapeDtypeStruct((batch * n_ctx, 512), BF16)] + out_shape
    scratch = [
        pltpu.VMEM((t_all + 3 * HALO, 512), F32),
        pltpu.VMEM((t_all, 128), F32),
        pltpu.VMEM((t_all, 256), F32),
        pltpu.VMEM((nch, SSD_STATE, 512), F32),
        pltpu.VMEM((nch, CHUNK, 512), F32),
        pltpu.VMEM((nch, CHUNK, 128), BF16),
        pltpu.VMEM((nch, SSD_STATE, 512), BF16),
        pltpu.VMEM((nch, 8, 512), F32),
        pltpu.VMEM((t_all, 512), F32),
    ]
    outs = pl.pallas_call(
        functools.partial(_ssd_kernel, n_ctx=n_ctx, n_lat=n_lat, ctx_out=ctx_out),
        grid=(batch, SSD_GROUPS),
        in_specs=in_specs,
        out_specs=out_specs,
        out_shape=out_shape,
        scratch_shapes=scratch,
        compiler_params=pltpu.CompilerParams(
            dimension_semantics=("arbitrary", "arbitrary"),
            vmem_limit_bytes=VMEM_LIMIT),
        name="ssd_scan",
    )(*args)
    if ctx_out:
        return outs[1], outs[0]
    return outs[0], None


def _rope_tables(n_lat):
    rows = n_lat // GRID_W
    row_idx = np.repeat(np.arange(rows), GRID_W).astype(np.float32)
    col_idx = (np.arange(rows * GRID_W) % GRID_W).astype(np.float32)

    def tables(dim, reps):
        quarter = dim // 4
        inv = (ROPE_BASE ** (-np.arange(quarter, dtype=np.float32) / quarter)).astype(np.float32)
        ang = np.concatenate([row_idx[:, None] * inv, col_idx[:, None] * inv], axis=-1)
        cos, sin = np.cos(ang.astype(np.float64)), np.sin(ang.astype(np.float64))
        cos2 = np.concatenate([cos, cos], axis=-1).astype(np.float32)
        sin2 = np.concatenate([-sin, sin], axis=-1).astype(np.float32)
        return jnp.asarray(np.tile(cos2, (1, reps))), jnp.asarray(np.tile(sin2, (1, reps)))

    cos_g, sin_g = tables(GQA_HEAD_DIM, 4)
    cos_d, sin_d = tables(DIFF_QK_DIM, 8)
    return {"cos_g": cos_g, "sin_g": sin_g, "cos_d": cos_d, "sin_d": sin_d}


def kernel(x, c, ctx, c_ctx, w_mod, b_mod, g_pre, g_post, w_in, conv_w, conv_b,
           a_log_fwd, a_log_bwd, dt_bias_fwd, dt_bias_bwd, d_skip, ssd_norm_g,
           q_norm_g, k_norm_g, diff_lambda, diff_norm_g, w_out):
    batch, n_lat, _ = x.shape
    n_ctx = ctx.shape[1]
    depth = w_mod.shape[0]
    assert n_lat % 512 == 0 and n_ctx % 256 == 0 and (batch * n_ctx) % 512 == 0
    assert batch + 1 <= 16

    in_perm, out_perm = _in_col_perm(), _out_row_perm()
    w_in_p = [_take_runs(w_in[l], in_perm, 1, IN_COLS).astype(BF16) for l in range(depth)]
    w_out_p = [_take_runs(w_out[l], out_perm, 0, None).astype(BF16) for l in range(depth)]
    conv_w8 = jnp.pad(conv_w, ((0, 0), (0, 8 - CONV_K), (0, 0)))
    conv_b1 = conv_b[:, None, :]

    def group16(fwd, bwd):
        out = jnp.zeros((depth, SSD_GROUPS, 16), F32)
        for g in range(SSD_GROUPS):
            out = out.at[:, g, 0:4].set(fwd[:, 4 * g:4 * g + 4])
            out = out.at[:, g, 8:12].set(bwd[:, 4 * g:4 * g + 4])
        return out

    ssd_par = jnp.broadcast_to(
        jnp.concatenate([group16(a_log_fwd, a_log_bwd),
                         group16(dt_bias_fwd, dt_bias_bwd)], axis=-1)[..., None],
        (depth, SSD_GROUPS, 32, 128))
    dsk = jnp.repeat(d_skip, SSD_HEAD_DIM, axis=1)[:, None, :]
    qg = jnp.tile(q_norm_g, (1, 4))[:, None, :]
    kg = jnp.tile(k_norm_g, (1, 2))[:, None, :]
    dng = jnp.tile(diff_norm_g, (1, 4))[:, None, :]
    tabs = _rope_tables(n_lat)

    cs = jnp.concatenate(
        [c, c_ctx[None, :], jnp.zeros((16 - batch - 1, D_MODEL), F32)], axis=0)
    mod_all = _mod_call(cs, w_mod, b_mod)

    h = x.reshape(batch * n_lat, D_MODEL)
    hc = ctx.reshape(batch * n_ctx, D_MODEL)
    for l in range(depth):
        ctx_out = l < depth - 1
        lam_init = 0.8 - 0.6 * float(np.exp(-0.3 * l))
        mod3 = mod_all[l][:, None, :]
        p_l = _inproj_call(h, mod3, g_pre[l][None, :], w_in_p[l], n_lat // 512, None)
        p_c = _inproj_call(hc, mod3, g_pre[l][None, :], w_in_p[l], None, batch)

        ys_l, ys_c = _ssd_call(p_c, p_l, conv_w8[l], conv_b1[l], ssd_par[l], dsk[l],
                               ssd_norm_g[l][None, :], batch, n_ctx, n_lat, ctx_out)
        yg_l = _gqa_call(p_l, p_c, p_l, tabs, qg[l], kg[l], batch, n_ctx, n_lat, True)
        diff_args = (p_l, p_c, p_l, tabs, diff_lambda[l], dng[l], batch, n_ctx, n_lat, True,
                     lam_init)
        yd_fast, row_sum_min = _diff_call(*diff_args, True)
        yd_l = lax.cond(jnp.min(row_sum_min) >= MIN_ROW_SUM,
                        lambda: yd_fast, lambda: _diff_call(*diff_args, False))
        h = _outproj_call(ys_l, yg_l, yd_l, h, mod3, g_post[l][None, :], w_out_p[l],
                          n_lat, None)
        if ctx_out:
            yg_c = _gqa_call(p_c, p_c, None, tabs, qg[l], kg[l], batch, n_ctx, 0, False)
            yd_c = _diff_call(p_c, p_c, None, tabs, diff_lambda[l], dng[l], batch,
                              n_ctx, 0, False, lam_init, False)
            hc = _outproj_call(ys_c, yg_c, yd_c, hc, mod3, g_post[l][None, :],
                               w_out_p[l], None, batch)
    return h.reshape(batch, n_lat, D_MODEL)
```

```python
import functools

import numpy as np
import jax
import jax.numpy as jnp
from jax import lax
from jax.experimental import pallas as pl
from jax.experimental.pallas import tpu as pltpu

F32 = jnp.float32
BF16 = jnp.bfloat16

D_MODEL = 1024
GRID_W = 64
ROPE_BASE = 10000.0
EPS = 1e-6
LOG2E = 1.4426950408889634

SSD_WIDTH = 512
SSD_HEADS = 8
SSD_HEAD_DIM = 64
SSD_GROUPS = 2
SSD_STATE = 128
CHUNK = 128
CONV_K = 5
HALO = 8
GQA_HEADS = 4
GQA_HEAD_DIM = 64
DIFF_HEADS = 4
DIFF_QK_DIM = 32
DIFF_V_DIM = 64

_IN_SPLITS = (("xbc", 1024), ("z", 512), ("dt", 16), ("gq", 256), ("gk", 128),
              ("gv", 128), ("gg", 256), ("dq", 256), ("dk", 256), ("dv", 256),
              ("dg", 256))
IN_COLS = sum(s for _, s in _IN_SPLITS)
NP = 28 * 128
GQ_HEAD_ORDER = (0, 2, 1, 3)

VMEM_LIMIT = 56 * 1024 * 1024
ATTN_TQ = 1024
ATTN_SUB = 512
DIFF_SUB = 512
ATTN_AHEAD = 2
DIFF_AHEAD = 1
SCORE_BOUND_MARGIN = 1.02
MIN_ROW_SUM = 2.0 ** -90
OUTPROJ_TM = 1024
OUTPROJ_SUB = 256


def _in_col_perm():
    off, o = {}, 0
    for name, size in _IN_SPLITS:
        off[name] = o
        o += size
    pad = IN_COLS
    cols = list(range(off["xbc"], off["xbc"] + 1024))
    cols += list(range(off["z"], off["z"] + 512))
    for h in GQ_HEAD_ORDER:
        cols += list(range(off["gq"] + 64 * h, off["gq"] + 64 * h + 64))
    cols += list(range(off["gk"], off["gk"] + 128))
    cols += list(range(off["gv"], off["gv"] + 128))
    for h in GQ_HEAD_ORDER:
        cols += list(range(off["gg"] + 64 * h, off["gg"] + 64 * h + 64))
    for name in ("dq", "dk", "dv", "dg"):
        cols += list(range(off[name], off[name] + 256))
    for g in range(SSD_GROUPS):
        blk = [pad] * 128
        for i in range(4):
            blk[i] = off["dt"] + 4 * g + i
            blk[8 + i] = off["dt"] + SSD_HEADS + 4 * g + i
        cols += blk
    assert len(cols) == NP
    return np.asarray(cols, np.int32)


def _out_row_perm():
    rows = list(range(SSD_WIDTH))
    for h in GQ_HEAD_ORDER:
        rows += list(range(SSD_WIDTH + 64 * h, SSD_WIDTH + 64 * h + 64))
    rows += list(range(SSD_WIDTH + 256, SSD_WIDTH + 512))
    return np.asarray(rows, np.int32)


def _take_runs(arr, idx, axis, pad_index):
    pieces, start = [], 0
    idx = [int(i) for i in idx]
    while start < len(idx):
        end = start + 1
        if idx[start] == pad_index:
            while end < len(idx) and idx[end] == pad_index:
                end += 1
            shape = list(arr.shape)
            shape[axis] = end - start
            pieces.append(jnp.zeros(shape, arr.dtype))
        else:
            while end < len(idx) and idx[end] == idx[end - 1] + 1 and idx[end] != pad_index:
                end += 1
            pieces.append(lax.slice_in_dim(arr, idx[start], idx[end - 1] + 1, axis=axis))
        start = end
    return jnp.concatenate(pieces, axis=axis)


def _dot(a, b):
    return jnp.dot(a, b, preferred_element_type=F32)


def _split3(x):
    hi = x.astype(BF16)
    r1 = x - hi.astype(F32)
    mid = r1.astype(BF16)
    lo = (r1 - mid.astype(F32)).astype(BF16)
    return hi, mid, lo


def _dot_exact_lhs(x, m_bf16):
    hi, mid, lo = _split3(x)
    return _dot(hi, m_bf16) + _dot(mid, m_bf16) + _dot(lo, m_bf16)


def _silu(x):
    return x * jax.nn.sigmoid(x)


def _seg_ones(width, seg):
    r = lax.broadcasted_iota(jnp.int32, (width, width), 0)
    c = lax.broadcasted_iota(jnp.int32, (width, width), 1)
    same = (r & ~(seg - 1)) == (c & ~(seg - 1))
    return jnp.where(same, 1.0, 0.0).astype(BF16)


def _seg_rms(x, seg, seg_mat):
    ss = _dot_exact_lhs(x * x, seg_mat)
    return x * lax.rsqrt(ss * (1.0 / seg) + EPS)


def _rope(x, cos, sin_signed, half):
    w = x.shape[-1]
    lane = lax.broadcasted_iota(jnp.int32, x.shape, 1)
    first = (lane & (2 * half - 1)) < half
    swapped = jnp.where(first, pltpu.roll(x, w - half, 1), pltpu.roll(x, half, 1))
    return x * cos + swapped * sin_signed


def _mod_kernel(cs_ref, w_ref, b_ref, o_ref):
    s = _silu(cs_ref[...]).astype(BF16)
    o_ref[...] = _dot(s, w_ref[...].astype(BF16)) + b_ref[...]


def _mod_call(cs, w_mod, b_mod):
    depth = w_mod.shape[0]
    nrow = cs.shape[0]
    tn = 1024
    return pl.pallas_call(
        _mod_kernel,
        grid=(depth, 3 * D_MODEL // tn),
        in_specs=[
            pl.BlockSpec((nrow, D_MODEL), lambda l, j: (0, 0)),
            pl.BlockSpec((None, D_MODEL, tn), lambda l, j: (l, 0, j)),
            pl.BlockSpec((None, 1, tn), lambda l, j: (l, 0, j)),
        ],
        out_specs=pl.BlockSpec((None, nrow, tn), lambda l, j: (l, 0, j)),
        out_shape=jax.ShapeDtypeStruct((depth, nrow, 3 * D_MODEL), F32),
        compiler_params=pltpu.CompilerParams(
            dimension_semantics=("arbitrary", "arbitrary")),
        name="mod_proj",
    )(cs, w_mod, b_mod.reshape(depth, 1, 3 * D_MODEL))


def _inproj_kernel(h_ref, mod_ref, g_ref, w_ref, o_ref):
    x = h_ref[...]
    ms = jnp.mean(x * x, axis=-1, keepdims=True)
    y = x * lax.rsqrt(ms + EPS) * g_ref[...]
    sh = mod_ref[:, 0:D_MODEL]
    sc = mod_ref[:, D_MODEL:2 * D_MODEL]
    u = (y * (1.0 + sc) + sh).astype(BF16)
    tn = 512
    for j in range(NP // tn):
        o_ref[:, j * tn:(j + 1) * tn] = _dot(u, w_ref[:, j * tn:(j + 1) * tn])


def _inproj_call(h, mod3, g_pre, w_bf16, tiles_per_row, fixed_row):
    n_tok = h.shape[0]
    tm = 512
    if fixed_row is None:
        mod_idx = lambda i: (i // tiles_per_row, 0, 0)
    else:
        mod_idx = lambda i: (fixed_row, 0, 0)
    return pl.pallas_call(
        _inproj_kernel,
        grid=(n_tok // tm,),
        in_specs=[
            pl.BlockSpec((tm, D_MODEL), lambda i: (i, 0)),
            pl.BlockSpec((None, 1, 3 * D_MODEL), mod_idx),
            pl.BlockSpec((1, D_MODEL), lambda i: (0, 0)),
            pl.BlockSpec((D_MODEL, NP), lambda i: (0, 0)),
        ],
        out_specs=pl.BlockSpec((tm, NP), lambda i: (i, 0)),
        out_shape=jax.ShapeDtypeStruct((n_tok, NP), F32),
        compiler_params=pltpu.CompilerParams(
            dimension_semantics=("arbitrary",), vmem_limit_bytes=VMEM_LIMIT),
        name="in_proj",
    )(h, mod3, g_pre, w_bf16)


def _outproj_kernel(ys_ref, yg_ref, yd_ref, h_ref, mod_ref, g_ref, w_ref, o_ref):
    tm = h_ref.shape[0]
    sub = min(OUTPROJ_SUB, tm)
    gt = mod_ref[:, 2 * D_MODEL:3 * D_MODEL]
    gain = g_ref[...]

    def project(r0):
        return (_dot(ys_ref[r0:r0 + sub, :], w_ref[0:512, :])
                + _dot(yg_ref[r0:r0 + sub, :], w_ref[512:768, :])
                + _dot(yd_ref[r0:r0 + sub, :], w_ref[768:1024, :]))

    o_next = project(0)
    for r0 in range(0, tm, sub):
        o = o_next
        if r0 + sub < tm:
            o_next = project(r0 + sub)
        ms = jnp.mean(o * o, axis=-1, keepdims=True)
        n = o * lax.rsqrt(ms + EPS) * gain
        o_ref[r0:r0 + sub, :] = h_ref[r0:r0 + sub, :] + gt * n


def _outproj_call(ys, yg, yd, h, mod3, g_post, w_bf16, rows_per_mod, fixed_row):
    n_tok = h.shape[0]
    tm = OUTPROJ_TM
    assert n_tok % tm == 0
    if fixed_row is None:
        assert rows_per_mod % tm == 0
        mod_idx = lambda i: (i // (rows_per_mod // tm), 0, 0)
    else:
        mod_idx = lambda i: (fixed_row, 0, 0)
    return pl.pallas_call(
        _outproj_kernel,
        grid=(n_tok // tm,),
        in_specs=[
            pl.BlockSpec((tm, 512), lambda i: (i, 0)),
            pl.BlockSpec((tm, 256), lambda i: (i, 0)),
            pl.BlockSpec((tm, 256), lambda i: (i, 0)),
            pl.BlockSpec((tm, D_MODEL), lambda i: (i, 0)),
            pl.BlockSpec((None, 1, 3 * D_MODEL), mod_idx),
            pl.BlockSpec((1, D_MODEL), lambda i: (0, 0)),
            pl.BlockSpec((D_MODEL, D_MODEL), lambda i: (0, 0)),
        ],
        out_specs=pl.BlockSpec((tm, D_MODEL), lambda i: (i, 0)),
        out_shape=jax.ShapeDtypeStruct((n_tok, D_MODEL), F32),
        compiler_params=pltpu.CompilerParams(
            dimension_semantics=("arbitrary",), vmem_limit_bytes=VMEM_LIMIT),
        name="out_proj",
    )(ys, yg, yd, h, mod3, g_post, w_bf16)


def _attend_many(lhs_list, kt_ref, vext_refs):
    def scores(i):
        s = _dot(lhs_list[i], kt_ref[...])
        return s, jnp.max(s, axis=-1, keepdims=True)

    outs = []
    n = len(lhs_list)
    ahead = [scores(i) for i in range(min(ATTN_AHEAD, n))]
    for i, vext_ref in enumerate(vext_refs):
        s, m = ahead.pop(0)
        if i + ATTN_AHEAD < n:
            ahead.append(scores(i + ATTN_AHEAD))
        p = jnp.exp2(s - m).astype(BF16)
        oe = _dot(p, vext_ref[...])
        outs.append(oe[:, 0:128] / oe[:, 128:256])
    return outs


def _attend_many_bounded(lhs_list, bounds, kt_ref, vext_refs):
    def probs(i):
        return jnp.exp2(_dot(lhs_list[i], kt_ref[...]) - bounds[i]).astype(BF16)

    outs, l_min = [], None
    n = len(lhs_list)
    ahead = [probs(i) for i in range(min(ATTN_AHEAD, n))]
    for i, vext_ref in enumerate(vext_refs):
        p = ahead.pop(0)
        if i + ATTN_AHEAD < n:
            ahead.append(probs(i + ATTN_AHEAD))
        oe = _dot(p, vext_ref[...])
        outs.append(oe[:, 0:128] / oe[:, 128:256])
        l = oe[:, 128:129]
        l_min = l if l_min is None else jnp.minimum(l_min, l)
    return outs, jnp.min(l_min, axis=0, keepdims=True)


def _attend_diff_pairs(lhs_list, kt_ref, v_refs, lam):
    n_heads = len(v_refs)

    def scores(h):
        s_a = _dot(lhs_list[2 * h], kt_ref[...])
        m_a = jnp.max(s_a, axis=-1, keepdims=True)
        s_b = _dot(lhs_list[2 * h + 1], kt_ref[...])
        m_b = jnp.max(s_b, axis=-1, keepdims=True)
        return s_a, m_a, s_b, m_b

    outs = []
    ahead = [scores(h) for h in range(min(DIFF_AHEAD, n_heads))]
    for h in range(n_heads):
        s_a, m_a, s_b, m_b = ahead.pop(0)
        if h + DIFF_AHEAD < n_heads:
            ahead.append(scores(h + DIFF_AHEAD))
        e_a = jnp.exp2(s_a - m_a)
        e_b = jnp.exp2(s_b - m_b)
        l_a = jnp.sum(e_a, axis=-1, keepdims=True)
        l_b = jnp.sum(e_b, axis=-1, keepdims=True)
        pc = (e_a - (lam * l_a / l_b) * e_b).astype(BF16)
        outs.append(_dot(pc, v_refs[h][...]) / l_a)
    return outs


def _attend_diff_pairs_bounded(lhs_list, bounds, kt_ref, v_refs, lam):
    n_heads = len(v_refs)

    def exps(h):
        e_a = jnp.exp2(_dot(lhs_list[2 * h], kt_ref[...]) - bounds[2 * h])
        e_b = jnp.exp2(_dot(lhs_list[2 * h + 1], kt_ref[...]) - bounds[2 * h + 1])
        return e_a, e_b

    outs, l_min = [], None
    ahead = [exps(h) for h in range(min(DIFF_AHEAD, n_heads))]
    for h in range(n_heads):
        e_a, e_b = ahead.pop(0)
        if h + DIFF_AHEAD < n_heads:
            ahead.append(exps(h + DIFF_AHEAD))
        l_a = jnp.sum(e_a, axis=-1, keepdims=True)
        l_b = jnp.sum(e_b, axis=-1, keepdims=True)
        pc = (e_a - (lam * l_a / l_b) * e_b).astype(BF16)
        outs.append(_dot(pc, v_refs[h][...]) / l_a)
        l_ab = jnp.minimum(l_a, l_b)
        l_min = l_ab if l_min is None else jnp.minimum(l_min, l_ab)
    return outs, jnp.min(l_min, axis=0, keepdims=True)


def _gqa_kernel(*refs, n_ctx, n_lat, rope_q, bounded):
    it = iter(refs)
    q_ref, gg_ref, kvc_ref = next(it), next(it), next(it)
    kvl_ref = next(it) if n_lat else None
    if rope_q:
        cosq_ref, sinq_ref = next(it), next(it)
    if n_lat:
        cosk_ref, sink_ref = next(it), next(it)
    qg_ref, kg_ref = next(it), next(it)
    y_ref = next(it)
    lmin_ref = next(it) if bounded else None
    kt_s, vext_s = next(it), next(it)
    kmax_s = next(it) if bounded else None

    seg128 = _seg_ones(128, 64)

    def head_norms(x, seg_mat):
        return jnp.sqrt(_dot((x * x).astype(BF16), seg_mat))

    @pl.when(pl.program_id(1) == 0)
    def _prep_kv():
        kc = _seg_rms(kvc_ref[:, 0:128], 64, seg128) * kg_ref[...]
        kt_s[:, 0:n_ctx] = kc.T.astype(BF16)
        vext_s[0:n_ctx, 0:128] = kvc_ref[:, 128:256].astype(BF16)
        if bounded:
            kmax = jnp.max(head_norms(kc, seg128), axis=0, keepdims=True)
        if n_lat:
            kl = _seg_rms(kvl_ref[:, 0:128], 64, seg128) * kg_ref[...]
            kl = _rope(kl, cosk_ref[...], sink_ref[...], 32)
            kt_s[:, n_ctx:n_ctx + n_lat] = kl.T.astype(BF16)
            vext_s[n_ctx:n_ctx + n_lat, 0:128] = kvl_ref[:, 128:256].astype(BF16)
            if bounded:
                kmax = jnp.maximum(
                    kmax, jnp.max(head_norms(kl, seg128), axis=0, keepdims=True))
        vext_s[:, 128:256] = jnp.ones((n_ctx + n_lat, 128), BF16)
        if bounded:
            kmax_s[...] = jnp.broadcast_to(kmax, kmax_s.shape)

    seg256 = _seg_ones(256, 64)
    q = _seg_rms(q_ref[...], 64, seg256) * qg_ref[...]
    if rope_q:
        q = _rope(q, cosq_ref[...], sinq_ref[...], 32)
    q = q * (GQA_HEAD_DIM ** -0.5 * LOG2E)
    tq = q.shape[0]
    sub = min(ATTN_SUB, tq)
    lane = lax.broadcasted_iota(jnp.int32, (sub, 128), 1)
    if bounded:
        kmax2 = jnp.concatenate([kmax_s[0:1, :], kmax_s[0:1, :]], axis=1)
        bound_all = head_norms(q, seg256) * kmax2 * SCORE_BOUND_MARGIN
    lhs_list, bounds = [], []
    for r0 in range(0, tq, sub):
        for half in range(2):
            qh = q[r0:r0 + sub, 128 * half:128 * half + 128]
            for kv in range(2):
                in_kv = (lane >= 64 * kv) & (lane < 64 * kv + 64)
                lhs_list.append(jnp.where(in_kv, qh, 0.0).astype(BF16))
                if bounded:
                    c0 = 128 * half + 64 * kv
                    bounds.append(bound_all[r0:r0 + sub, c0:c0 + 1])
    if bounded:
        outs, l_min = _attend_many_bounded(lhs_list, bounds, kt_s, [vext_s] * len(lhs_list))
        lmin_ref[...] = jnp.broadcast_to(l_min, lmin_ref.shape)
    else:
        outs = _attend_many(lhs_list, kt_s, [vext_s] * len(lhs_list))
    for j, r0 in enumerate(range(0, tq, sub)):
        for half in range(2):
            o = jnp.where(lane < 64, outs[4 * j + 2 * half], outs[4 * j + 2 * half + 1])
            gate = _silu(gg_ref[r0:r0 + sub, 128 * half:128 * half + 128])
            y_ref[r0:r0 + sub, 128 * half:128 * half + 128] = (o * gate).astype(BF16)


def _gqa_call(p_q, p_c, p_l, tabs, qg, kg, batch, n_ctx, n_lat, rope_q, bounded):
    t_total = p_q.shape[0] // batch
    tq = min(ATTN_TQ, t_total)
    nq = t_total // tq
    in_specs = [
        pl.BlockSpec((tq, 256), lambda b, i: (b * nq + i, 6)),
        pl.BlockSpec((tq, 256), lambda b, i: (b * nq + i, 8)),
        pl.BlockSpec((n_ctx, 256), lambda b, i: (b, 7)),
    ]
    args = [p_q, p_q, p_c]
    if n_lat:
        in_specs.append(pl.BlockSpec((n_lat, 256), lambda b, i: (b, 7)))
        args.append(p_l)
    if rope_q:
        in_specs += [pl.BlockSpec((tq, 256), lambda b, i: (i, 0))] * 2
        args += [tabs["cos_g"], tabs["sin_g"]]
    if n_lat:
        in_specs += [pl.BlockSpec((n_lat, 128), lambda b, i: (0, 0))] * 2
        args += [tabs["cos_g"], tabs["sin_g"]]
    in_specs += [pl.BlockSpec((1, 256), lambda b, i: (0, 0)),
                 pl.BlockSpec((1, 128), lambda b, i: (0, 0))]
    args += [qg, kg]
    s_keys = n_ctx + n_lat
    out_specs = [pl.BlockSpec((tq, 256), lambda b, i: (b * nq + i, 0))]
    out_shape = [jax.ShapeDtypeStruct((p_q.shape[0], 256), BF16)]
    scratch = [pltpu.VMEM((128, s_keys), BF16), pltpu.VMEM((s_keys, 256), BF16)]
    if bounded:
        out_specs.append(pl.BlockSpec((None, 8, 128), lambda b, i: (b * nq + i, 0, 0)))
        out_shape.append(jax.ShapeDtypeStruct((batch * nq, 8, 128), F32))
        scratch.append(pltpu.VMEM((8, 128), F32))
    outs = pl.pallas_call(
        functools.partial(_gqa_kernel, n_ctx=n_ctx, n_lat=n_lat, rope_q=rope_q,
                          bounded=bounded),
        grid=(batch, nq),
        in_specs=in_specs,
        out_specs=out_specs,
        out_shape=out_shape,
        scratch_shapes=scratch,
        compiler_params=pltpu.CompilerParams(
            dimension_semantics=("arbitrary", "arbitrary"),
            vmem_limit_bytes=VMEM_LIMIT),
        name="gqa_attn_bounded" if bounded else "gqa_attn",
    )(*args)
    return (outs[0], outs[1]) if bounded else outs[0]


def _diff_kernel(*refs, n_ctx, n_lat, rope_q, lam_init, bounded):
    it = iter(refs)
    q_ref, dg_ref, kc_ref, vc_ref = next(it), next(it), next(it), next(it)
    if n_lat:
        kl_ref, vl_ref = next(it), next(it)
    if rope_q:
        cosq_ref, sinq_ref = next(it), next(it)
    if n_lat:
        cosk_ref, sink_ref = next(it), next(it)
    lam_ref, ng_ref = next(it), next(it)
    y_ref = next(it)
    lmin_ref = next(it) if bounded else None
    kt_s, vlo_s, vhi_s = next(it), next(it), next(it)
    kmax_s = next(it) if bounded else None
    s_keys = n_ctx + n_lat
    seg32 = _seg_ones(256, DIFF_QK_DIM)

    def map_norms(x):
        return jnp.sqrt(_dot((x * x).astype(BF16), seg32))

    @pl.when(pl.program_id(1) == 0)
    def _prep_kv():
        kc = kc_ref[...]
        kt_s[:, 0:n_ctx] = kc.T.astype(BF16)
        vlo_s[0:n_ctx, 0:128] = vc_ref[:, 0:128].astype(BF16)
        vhi_s[0:n_ctx, 0:128] = vc_ref[:, 128:256].astype(BF16)
        if bounded:
            kmax = jnp.max(map_norms(kc), axis=0, keepdims=True)
        if n_lat:
            kl = _rope(kl_ref[...], cosk_ref[...], sink_ref[...], 16)
            kt_s[:, n_ctx:s_keys] = kl.T.astype(BF16)
            vlo_s[n_ctx:s_keys, 0:128] = vl_ref[:, 0:128].astype(BF16)
            vhi_s[n_ctx:s_keys, 0:128] = vl_ref[:, 128:256].astype(BF16)
            if bounded:
                kmax = jnp.maximum(kmax, jnp.max(map_norms(kl), axis=0, keepdims=True))
        if bounded:
            kmax_s[...] = jnp.broadcast_to(kmax, kmax_s.shape)

    lp = lam_ref[...]
    lam = (jnp.exp(jnp.sum(lp[0:1, :] * lp[1:2, :], axis=-1, keepdims=True))
           - jnp.exp(jnp.sum(lp[2:3, :] * lp[3:4, :], axis=-1, keepdims=True))
           + lam_init)

    q = q_ref[...]
    if rope_q:
        q = _rope(q, cosq_ref[...], sinq_ref[...], 16)
    q = q * (DIFF_QK_DIM ** -0.5 * LOG2E)
    tq = q.shape[0]
    sub = min(DIFF_SUB, tq)
    lane256 = lax.broadcasted_iota(jnp.int32, (sub, 256), 1)
    lane128 = lax.broadcasted_iota(jnp.int32, (sub, 128), 1)
    seg128 = _seg_ones(128, 64)
    if bounded:
        bound_all = map_norms(q) * kmax_s[0:1, :] * SCORE_BOUND_MARGIN
    lhs_list, v_list, bounds = [], [], []
    for r0 in range(0, tq, sub):
        for mp in range(2 * DIFF_HEADS):
            in_map = (lane256 >= 32 * mp) & (lane256 < 32 * mp + 32)
            lhs_list.append(jnp.where(in_map, q[r0:r0 + sub, :], 0.0).astype(BF16))
            if bounded:
                bounds.append(bound_all[r0:r0 + sub, 32 * mp:32 * mp + 1])
        v_list += [vlo_s, vlo_s, vhi_s, vhi_s]

    if bounded:
        heads, l_min = _attend_diff_pairs_bounded(lhs_list, bounds, kt_s, v_list, lam)
        lmin_ref[...] = jnp.broadcast_to(l_min, lmin_ref.shape)
    else:
        heads = _attend_diff_pairs(lhs_list, kt_s, v_list, lam)
    for j, r0 in enumerate(range(0, tq, sub)):
        for half in range(2):
            o = jnp.where(lane128 < 64, heads[4 * j + 2 * half], heads[4 * j + 2 * half + 1])
            n = _seg_rms(o, 64, seg128) * ng_ref[:, 128 * half:128 * half + 128]
            n = n * (1.0 - lam_init)
            gate = _silu(dg_ref[r0:r0 + sub, 128 * half:128 * half + 128])
            y_ref[r0:r0 + sub, 128 * half:128 * half + 128] = (n * gate).astype(BF16)


def _diff_call(p_q, p_c, p_l, tabs, lam_params, ng, batch, n_ctx, n_lat, rope_q,
               lam_init, bounded):
    t_total = p_q.shape[0] // batch
    tq = min(ATTN_TQ, t_total)
    nq = t_total // tq
    in_specs = [
        pl.BlockSpec((tq, 256), lambda b, i: (b * nq + i, 9)),
        pl.BlockSpec((tq, 256), lambda b, i: (b * nq + i, 12)),
        pl.BlockSpec((n_ctx, 256), lambda b, i: (b, 10)),
        pl.BlockSpec((n_ctx, 256), lambda b, i: (b, 11)),
    ]
    args = [p_q, p_q, p_c, p_c]
    if n_lat:
        in_specs += [pl.BlockSpec((n_lat, 256), lambda b, i: (b, 10)),
                     pl.BlockSpec((n_lat, 256), lambda b, i: (b, 11))]
        args += [p_l, p_l]
    if rope_q:
        in_specs += [pl.BlockSpec((tq, 256), lambda b, i: (i, 0))] * 2
        args += [tabs["cos_d"], tabs["sin_d"]]
    if n_lat:
        in_specs += [pl.BlockSpec((n_lat, 256), lambda b, i: (0, 0))] * 2
        args += [tabs["cos_d"], tabs["sin_d"]]
    in_specs += [pl.BlockSpec((4, DIFF_QK_DIM), lambda b, i: (0, 0)),
                 pl.BlockSpec((1, 256), lambda b, i: (0, 0))]
    args += [lam_params, ng]
    s_keys = n_ctx + n_lat
    out_specs = [pl.BlockSpec((tq, 256), lambda b, i: (b * nq + i, 0))]
    out_shape = [jax.ShapeDtypeStruct((p_q.shape[0], 256), BF16)]
    scratch = [pltpu.VMEM((256, s_keys), BF16),
               pltpu.VMEM((s_keys, 128), BF16),
               pltpu.VMEM((s_keys, 128), BF16)]
    if bounded:
        out_specs.append(pl.BlockSpec((None, 8, 128), lambda b, i: (b * nq + i, 0, 0)))
        out_shape.append(jax.ShapeDtypeStruct((batch * nq, 8, 128), F32))
        scratch.append(pltpu.VMEM((8, 256), F32))
    outs = pl.pallas_call(
        functools.partial(_diff_kernel, n_ctx=n_ctx, n_lat=n_lat, rope_q=rope_q,
                          lam_init=lam_init, bounded=bounded),
        grid=(batch, nq),
        in_specs=in_specs,
        out_specs=out_specs,
        out_shape=out_shape,
        scratch_shapes=scratch,
        compiler_params=pltpu.CompilerParams(
            dimension_semantics=("arbitrary", "arbitrary"),
            vmem_limit_bytes=VMEM_LIMIT),
        name="diff_attn_bounded" if bounded else "diff_attn",
    )(*args)
    return (outs[0], outs[1]) if bounded else outs[0]


def _ssd_kernel(xc_ref, xl_ref, bc_ref, bl_ref, cc_ref, cl_ref, zc_ref, zl_ref,
                dtc_ref, dtl_ref, cwx_ref, cwb_ref, cwc_ref, cbx_ref, cbb_ref,
                cbc_ref, par_ref, dsk_ref, ng_ref, *rest, n_ctx, n_lat, ctx_out):
    if ctx_out:
        oc_ref, ol_ref = rest[0], rest[1]
        rest = rest[2:]
    else:
        oc_ref, ol_ref = None, rest[0]
        rest = rest[1:]
    xp_s, dtr_s, yacc_s, sloc_s, e_s, cbf_s, sin_s, dec_s, vun_s = rest

    g = pl.program_id(1)
    ncc = n_ctx // CHUNK
    ncl = n_lat // CHUNK
    nch = ncc + ncl
    GROUP = next(n for n in (6, 3, 2, 1) if nch % n == 0)
    t_all = n_ctx + n_lat
    lat0 = n_ctx + 2 * HALO

    zeros_h = jnp.zeros((HALO, 512), F32)
    xp_s[0:HALO, :] = zeros_h
    xp_s[HALO:HALO + n_ctx, 0:256] = xc_ref[...]
    xp_s[HALO:HALO + n_ctx, 256:384] = bc_ref[...]
    xp_s[HALO:HALO + n_ctx, 384:512] = cc_ref[...]
    xp_s[HALO + n_ctx:lat0, :] = zeros_h
    xp_s[lat0:lat0 + n_lat, 0:256] = xl_ref[...]
    xp_s[lat0:lat0 + n_lat, 256:384] = bl_ref[...]
    xp_s[lat0:lat0 + n_lat, 384:512] = cl_ref[...]
    xp_s[lat0 + n_lat:lat0 + n_lat + HALO, :] = zeros_h

    dtr_s[0:n_ctx, :] = dtc_ref[...]
    dtr_s[n_ctx:t_all, :] = dtl_ref[...]
    a_col = -jnp.exp(par_ref[0:16, :])
    bias_col = par_ref[16:32, :]

    def _softplus(v):
        return jnp.maximum(v, 0.0) + jnp.log1p(jnp.exp(-jnp.abs(v)))

    r128 = lax.broadcasted_iota(jnp.int32, (CHUNK, CHUNK), 0)
    c128 = lax.broadcasted_iota(jnp.int32, (CHUNK, CHUNK), 1)
    lower = c128 <= r128
    upper = c128 >= r128
    tril = jnp.where(lower, 1.0, 0.0).astype(BF16)
    triu = jnp.where(upper, 1.0, 0.0).astype(BF16)
    fwd_row = lax.broadcasted_iota(jnp.int32, (16, CHUNK), 0) < 8
    fwd_row1 = lax.broadcasted_iota(jnp.int32, (16, 1), 0) < 8
    er = lax.broadcasted_iota(jnp.int32, (CHUNK, 1024), 0)
    ec = lax.broadcasted_iota(jnp.int32, (CHUNK, 1024), 1)
    src_lane = 16 + 16 * (ec >> 9) + 8 * ((ec >> 8) & 1) + ((ec >> 6) & 3)
    expand = jnp.where(er == src_lane, 1.0, 0.0).astype(BF16)
    er0 = lax.broadcasted_iota(jnp.int32, (CHUNK, 512), 0)
    ec0 = lax.broadcasted_iota(jnp.int32, (CHUNK, 512), 1)
    expand_tot = jnp.where(er0 == 8 * (ec0 >> 8) + ((ec0 >> 6) & 3), 1.0, 0.0).astype(BF16)
    lane256 = lax.broadcasted_iota(jnp.int32, (CHUNK, 256), 1)
    lane128_1 = lax.broadcasted_iota(jnp.int32, (1, CHUNK), 1)

    cw = jnp.concatenate([cwx_ref[...], cwb_ref[...], cwc_ref[...]], axis=1)
    cb = jnp.concatenate([cbx_ref[...], cbb_ref[...], cbc_ref[...]], axis=1)
    dsk = dsk_ref[...]

    def phase_a(grp, carry):
        cs = [grp * GROUP + j for j in range(GROUP)]
        r0s = [pl.multiple_of(c * CHUNK, CHUNK) for c in cs]

        def conv(c):
            wstart = pl.multiple_of(c * CHUNK + jnp.where(c >= ncc, HALO, 0), 8)
            win = xp_s[pl.ds(wstart, CHUNK + 2 * HALO), :]
            acc = jnp.broadcast_to(cb, (CHUNK, 512))
            for k in range(CONV_K):
                d = k - CONV_K // 2
                if d == 0:
                    tap = win[HALO:HALO + CHUNK, :]
                else:
                    tap = pltpu.roll(win, (-d) % (CHUNK + 2 * HALO), 0)[HALO:HALO + CHUNK, :]
                acc = acc + cw[k:k + 1, :] * tap
            return _silu(acc)

        dtts = [_softplus(dtr_s[pl.ds(r0, CHUNK), :].T[0:16, :] + bias_col) for r0 in r0s]
        a_ts = [dtt * a_col for dtt in dtts]
        acol_ts = [jnp.where(fwd_row, _dot_exact_lhs(a_t, triu), _dot_exact_lhs(a_t, tril))
                   for a_t in a_ts]
        us = [conv(c) for c in cs]
        xus = [u[:, 0:256] for u in us]
        bts = [u[:, 256:384].T.astype(BF16) for u in us]
        cbfs = [u[:, 384:512].astype(BF16) for u in us]
        xbfs = [xu.astype(BF16) for xu in xus]
        gmats = [_dot(cbf, bt) for cbf, bt in zip(cbfs, bts)]
        tms = []
        for dtt, acol_t in zip(dtts, acol_ts):
            tot = jnp.where(fwd_row1, acol_t[:, CHUNK - 1:CHUNK], acol_t[:, 0:1])
            w_t = dtt * jnp.exp(tot - acol_t)
            e_t = jnp.exp(acol_t)
            stacked = jnp.concatenate(
                [acol_t, w_t, e_t, jnp.zeros((CHUNK - 48, CHUNK), F32)], axis=0)
            tms.append(stacked.T)
        wes = [_dot(tm.astype(BF16), expand) for tm in tms]
        decs = []
        for tm in tms:
            tot_row = jnp.where(lane128_1 < 8, tm[CHUNK - 1:CHUNK, :], tm[0:1, :])
            tot512 = _dot_exact_lhs(jnp.broadcast_to(tot_row, (8, CHUNK)), expand_tot)
            decs.append(jnp.exp(tot512))
        ydiags = [jnp.zeros((CHUNK, 256), F32) for _ in cs]
        for i in range(4):
            in_head = (lane256 >= 64 * i) & (lane256 < 64 * i + 64)
            for j in range(GROUP):
                tm, acol_t, dtt = tms[j], acol_ts[j], dtts[j]
                arg = jnp.where(lower, tm[:, i:i + 1] - acol_t[i:i + 1, :],
                                tm[:, 8 + i:9 + i] - acol_t[8 + i:9 + i, :])
                scale = (jnp.where(lower, dtt[i:i + 1, :], 0.0)
                         + jnp.where(upper, dtt[8 + i:9 + i, :], 0.0))
                wmat = (gmats[j] * jnp.exp(arg) * scale).astype(BF16)
                ydiags[j] = jnp.where(in_head, _dot(wmat, xbfs[j]), ydiags[j])
        for j, c in enumerate(cs):
            w512 = wes[j][:, 0:512]
            e512 = wes[j][:, 512:1024]
            xdw = (jnp.concatenate([xus[j], xus[j]], axis=1) * w512).astype(BF16)
            sloc_s[c] = _dot(bts[j], xdw)
            dec_s[c] = decs[j]
            e_s[c] = e512
            cbf_s[c] = cbfs[j]
            yacc_s[pl.ds(r0s[j], CHUNK), :] = ydiags[j] + xus[j] * dsk
        return carry

    lax.fori_loop(0, nch // GROUP, phase_a, 0)

    fwd_order = list(range(nch))
    bwd_order = list(range(ncc - 1, -1, -1)) + list(range(nch - 1, ncc - 1, -1))
    for order, lo in ((fwd_order, 0), (bwd_order, 256)):
        state = jnp.zeros((SSD_STATE, 256), F32)
        for c in order:
            sin_s[c, :, lo:lo + 256] = state.astype(BF16)
            state = (state * dec_s[c, 0:1, lo:lo + 256]
                     + sloc_s[c, :, lo:lo + 256])

    def phase_c(cs, z_ref, zrows):
        yos = [_dot(cbf_s[c], sin_s[c]) for c in cs]
        vs = []
        for c, zrow, yo in zip(cs, zrows, yos):
            r0 = c * CHUNK if isinstance(c, int) else pl.multiple_of(c * CHUNK, CHUNK)
            yo = yo * e_s[c]
            y = yacc_s[pl.ds(r0, CHUNK), :] + yo[:, 0:256] + yo[:, 256:512]
            vs.append((r0, y * _silu(z_ref[pl.ds(zrow, CHUNK), :])))

        @pl.when(g == 0)
        def _():
            for r0, v in vs:
                vun_s[pl.ds(r0, CHUNK), 0:256] = v

        @pl.when(g == 1)
        def _():
            for r0, v in vs:
                vun_s[pl.ds(r0, CHUNK), 256:512] = v

    if ctx_out:
        phase_c(list(range(ncc)), zc_ref, [c * CHUNK for c in range(ncc)])

    group_c = next(n for n in (4, 2, 1) if ncl % n == 0)

    def phase_c_lat(k, carry):
        ks = [k * group_c + j for j in range(group_c)]
        phase_c([kk + ncc for kk in ks], zl_ref,
                [pl.multiple_of(kk * CHUNK, CHUNK) for kk in ks])
        return carry

    lax.fori_loop(0, ncl // group_c, phase_c_lat, 0)

    @pl.when(g == SSD_GROUPS - 1)
    def _finalize():
        ng = ng_ref[...]

        def norm_rows(r0, nrows):
            v = vun_s[pl.ds(r0, nrows), :]
            ms = jnp.mean(v * v, axis=-1, keepdims=True)
            return (v * lax.rsqrt(ms + EPS) * ng).astype(BF16)

        if ctx_out:
            oc_ref[...] = norm_rows(0, n_ctx)

        def fin(k, carry):
            r0 = pl.multiple_of(k * 256, 256)
            ol_ref[pl.ds(r0, 256), :] = norm_rows(n_ctx + r0, 256)
            return carry

        lax.fori_loop(0, n_lat // 256, fin, 0)


def _ssd_call(p_c, p_l, conv_w8, conv_b, par, dsk, ng, batch, n_ctx, n_lat, ctx_out):
    nch = (n_ctx + n_lat) // CHUNK
    t_all = n_ctx + n_lat
    in_specs = [
        pl.BlockSpec((n_ctx, 256), lambda b, g: (b, g)),
        pl.BlockSpec((n_lat, 256), lambda b, g: (b, g)),
        pl.BlockSpec((n_ctx, 128), lambda b, g: (b, 4 + g)),
        pl.BlockSpec((n_lat, 128), lambda b, g: (b, 4 + g)),
        pl.BlockSpec((n_ctx, 128), lambda b, g: (b, 6 + g)),
        pl.BlockSpec((n_lat, 128), lambda b, g: (b, 6 + g)),
        pl.BlockSpec((n_ctx, 256), lambda b, g: (b, 4 + g)),
        pl.BlockSpec((n_lat, 256), lambda b, g: (b, 4 + g)),
        pl.BlockSpec((n_ctx, 128), lambda b, g: (b, 26 + g)),
        pl.BlockSpec((n_lat, 128), lambda b, g: (b, 26 + g)),
        pl.BlockSpec((8, 256), lambda b, g: (0, g)),
        pl.BlockSpec((8, 128), lambda b, g: (0, 4 + g)),
        pl.BlockSpec((8, 128), lambda b, g: (0, 6 + g)),
        pl.BlockSpec((1, 256), lambda b, g: (0, g)),
        pl.BlockSpec((1, 128), lambda b, g: (0, 4 + g)),
        pl.BlockSpec((1, 128), lambda b, g: (0, 6 + g)),
        pl.BlockSpec((None, 32, 128), lambda b, g: (g, 0, 0)),
        pl.BlockSpec((1, 256), lambda b, g: (0, g)),
        pl.BlockSpec((1, 512), lambda b, g: (0, 0)),
    ]
    args = [p_c, p_l, p_c, p_l, p_c, p_l, p_c, p_l, p_c, p_l,
            conv_w8, conv_w8, conv_w8, conv_b, conv_b, conv_b, par, dsk, ng]
    out_specs = [pl.BlockSpec((n_lat, 512), lambda b, g: (b, 0))]
    out_shape = [jax.ShapeDtypeStruct((batch * n_lat, 512), BF16)]
    if ctx_out:
        out_specs = [pl.BlockSpec((n_ctx, 512), lambda b, g: (b, 0))] + out_specs
        out_shape = [jax.ShapeDtypeStruct((batch * n_ctx, 512), BF16)] + out_shape
    scratch = [
        pltpu.VMEM((t_all + 3 * HALO, 512), F32),
        pltpu.VMEM((t_all, 128), F32),
        pltpu.VMEM((t_all, 256), F32),
        pltpu.VMEM((nch, SSD_STATE, 512), F32),
        pltpu.VMEM((nch, CHUNK, 512), F32),
        pltpu.VMEM((nch, CHUNK, 128), BF16),
        pltpu.VMEM((nch, SSD_STATE, 512), BF16),
        pltpu.VMEM((nch, 8, 512), F32),
        pltpu.VMEM((t_all, 512), F32),
    ]
    outs = pl.pallas_call(
        functools.partial(_ssd_kernel, n_ctx=n_ctx, n_lat=n_lat, ctx_out=ctx_out),
        grid=(batch, SSD_GROUPS),
        in_specs=in_specs,
        out_specs=out_specs,
        out_shape=out_shape,
        scratch_shapes=scratch,
        compiler_params=pltpu.CompilerParams(
            dimension_semantics=("arbitrary", "arbitrary"),
            vmem_limit_bytes=VMEM_LIMIT),
        name="ssd_scan",
    )(*args)
    if ctx_out:
        return outs[1], outs[0]
    return outs[0], None


def _rope_tables(n_lat):
    rows = n_lat // GRID_W
    row_idx = np.repeat(np.arange(rows), GRID_W).astype(np.float32)
    col_idx = (np.arange(rows * GRID_W) % GRID_W).astype(np.float32)

    def tables(dim, reps):
        quarter = dim // 4
        inv = (ROPE_BASE ** (-np.arange(quarter, dtype=np.float32) / quarter)).astype(np.float32)
        ang = np.concatenate([row_idx[:, None] * inv, col_idx[:, None] * inv], axis=-1)
        cos, sin = np.cos(ang.astype(np.float64)), np.sin(ang.astype(np.float64))
        cos2 = np.concatenate([cos, cos], axis=-1).astype(np.float32)
        sin2 = np.concatenate([-sin, sin], axis=-1).astype(np.float32)
        return jnp.asarray(np.tile(cos2, (1, reps))), jnp.asarray(np.tile(sin2, (1, reps)))

    cos_g, sin_g = tables(GQA_HEAD_DIM, 4)
    cos_d, sin_d = tables(DIFF_QK_DIM, 8)
    return {"cos_g": cos_g, "sin_g": sin_g, "cos_d": cos_d, "sin_d": sin_d}


def kernel(x, c, ctx, c_ctx, w_mod, b_mod, g_pre, g_post, w_in, conv_w, conv_b,
           a_log_fwd, a_log_bwd, dt_bias_fwd, dt_bias_bwd, d_skip, ssd_norm_g,
           q_norm_g, k_norm_g, diff_lambda, diff_norm_g, w_out):
    batch, n_lat, _ = x.shape
    n_ctx = ctx.shape[1]
    depth = w_mod.shape[0]
    assert n_lat % 512 == 0 and n_ctx % 256 == 0 and (batch * n_ctx) % 512 == 0
    assert batch + 1 <= 16

    in_perm, out_perm = _in_col_perm(), _out_row_perm()
    w_in_p = [_take_runs(w_in[l], in_perm, 1, IN_COLS).astype(BF16) for l in range(depth)]
    w_out_p = [_take_runs(w_out[l], out_perm, 0, None).astype(BF16) for l in range(depth)]
    conv_w8 = jnp.pad(conv_w, ((0, 0), (0, 8 - CONV_K), (0, 0)))
    conv_b1 = conv_b[:, None, :]

    def group16(fwd, bwd):
        out = jnp.zeros((depth, SSD_GROUPS, 16), F32)
        for g in range(SSD_GROUPS):
            out = out.at[:, g, 0:4].set(fwd[:, 4 * g:4 * g + 4])
            out = out.at[:, g, 8:12].set(bwd[:, 4 * g:4 * g + 4])
        return out

    ssd_par = jnp.broadcast_to(
        jnp.concatenate([group16(a_log_fwd, a_log_bwd),
                         group16(dt_bias_fwd, dt_bias_bwd)], axis=-1)[..., None],
        (depth, SSD_GROUPS, 32, 128))
    dsk = jnp.repeat(d_skip, SSD_HEAD_DIM, axis=1)[:, None, :]
    qg = jnp.tile(q_norm_g, (1, 4))[:, None, :]
    kg = jnp.tile(k_norm_g, (1, 2))[:, None, :]
    dng = jnp.tile(diff_norm_g, (1, 4))[:, None, :]
    tabs = _rope_tables(n_lat)

    cs = jnp.concatenate(
        [c, c_ctx[None, :], jnp.zeros((16 - batch - 1, D_MODEL), F32)], axis=0)
    mod_all = _mod_call(cs, w_mod, b_mod)

    h = x.reshape(batch * n_lat, D_MODEL)
    hc = ctx.reshape(batch * n_ctx, D_MODEL)
    for l in range(depth):
        ctx_out = l < depth - 1
        lam_init = 0.8 - 0.6 * float(np.exp(-0.3 * l))
        mod3 = mod_all[l][:, None, :]
        p_l = _inproj_call(h, mod3, g_pre[l][None, :], w_in_p[l], n_lat // 512, None)
        p_c = _inproj_call(hc, mod3, g_pre[l][None, :], w_in_p[l], None, batch)

        ys_l, ys_c = _ssd_call(p_c, p_l, conv_w8[l], conv_b1[l], ssd_par[l], dsk[l],
                               ssd_norm_g[l][None, :], batch, n_ctx, n_lat, ctx_out)
        gqa_args = (p_l, p_c, p_l, tabs, qg[l], kg[l], batch, n_ctx, n_lat, True)
        yg_fast, denom_min = _gqa_call(*gqa_args, True)
        yg_l = lax.cond(jnp.min(denom_min) >= MIN_ROW_SUM,
                        lambda: yg_fast, lambda: _gqa_call(*gqa_args, False))
        diff_args = (p_l, p_c, p_l, tabs, diff_lambda[l], dng[l], batch, n_ctx, n_lat, True,
                     lam_init)
        yd_fast, row_sum_min = _diff_call(*diff_args, True)
        yd_l = lax.cond(jnp.min(row_sum_min) >= MIN_ROW_SUM,
                        lambda: yd_fast, lambda: _diff_call(*diff_args, False))
        h = _outproj_call(ys_l, yg_l, yd_l, h, mod3, g_post[l][None, :], w_out_p[l],
                          n_lat, None)
        if ctx_out:
            yg_c = _gqa_call(p_c, p_c, None, tabs, qg[l], kg[l], batch, n_ctx, 0, False, False)
            yd_c = _diff_call(p_c, p_c, None, tabs, diff_lambda[l], dng[l], batch,
                              n_ctx, 0, False, lam_init, False)
            hc = _outproj_call(ys_c, yg_c, yd_c, hc, mod3, g_post[l][None, :],
                               w_out_p[l], None, batch)
    return h.reshape(batch, n_lat, D_MODEL)
```

```python
import functools

import numpy as np
import jax
import jax.numpy as jnp
from jax import lax
from jax.experimental import pallas as pl
from jax.experimental.pallas import tpu as pltpu

F32 = jnp.float32
BF16 = jnp.bfloat16

D_MODEL = 1024
GRID_W = 64
ROPE_BASE = 10000.0
EPS = 1e-6
LOG2E = 1.4426950408889634

SSD_WIDTH = 512
SSD_HEADS = 8
SSD_HEAD_DIM = 64
SSD_GROUPS = 2
SSD_STATE = 128
CHUNK = 128
CONV_K = 5
HALO = 8
GQA_HEADS = 4
GQA_HEAD_DIM = 64
DIFF_HEADS = 4
DIFF_QK_DIM = 32
DIFF_V_DIM = 64

_IN_SPLITS = (("xbc", 1024), ("z", 512), ("dt", 16), ("gq", 256), ("gk", 128),
              ("gv", 128), ("gg", 256), ("dq", 256), ("dk", 256), ("dv", 256),
              ("dg", 256))
IN_COLS = sum(s for _, s in _IN_SPLITS)
NP = 28 * 128
GQ_HEAD_ORDER = (0, 2, 1, 3)

VMEM_LIMIT = 56 * 1024 * 1024
ATTN_TQ = 1024
ATTN_SUB = 512
DIFF_SUB = 256
ATTN_AHEAD = 2
DIFF_AHEAD = 2
SCORE_BOUND_MARGIN = 1.02
MIN_ROW_SUM = 2.0 ** -90
OUTPROJ_TM = 1024
OUTPROJ_SUB = 256


def _in_col_perm():
    off, o = {}, 0
    for name, size in _IN_SPLITS:
        off[name] = o
        o += size
    pad = IN_COLS
    cols = list(range(off["xbc"], off["xbc"] + 1024))
    cols += list(range(off["z"], off["z"] + 512))
    for h in GQ_HEAD_ORDER:
        cols += list(range(off["gq"] + 64 * h, off["gq"] + 64 * h + 64))
    cols += list(range(off["gk"], off["gk"] + 128))
    cols += list(range(off["gv"], off["gv"] + 128))
    for h in GQ_HEAD_ORDER:
        cols += list(range(off["gg"] + 64 * h, off["gg"] + 64 * h + 64))
    for name in ("dq", "dk", "dv", "dg"):
        cols += list(range(off[name], off[name] + 256))
    for g in range(SSD_GROUPS):
        blk = [pad] * 128
        for i in range(4):
            blk[i] = off["dt"] + 4 * g + i
            blk[8 + i] = off["dt"] + SSD_HEADS + 4 * g + i
        cols += blk
    assert len(cols) == NP
    return np.asarray(cols, np.int32)


def _out_row_perm():
    rows = list(range(SSD_WIDTH))
    for h in GQ_HEAD_ORDER:
        rows += list(range(SSD_WIDTH + 64 * h, SSD_WIDTH + 64 * h + 64))
    rows += list(range(SSD_WIDTH + 256, SSD_WIDTH + 512))
    return np.asarray(rows, np.int32)


def _take_runs(arr, idx, axis, pad_index):
    pieces, start = [], 0
    idx = [int(i) for i in idx]
    while start < len(idx):
        end = start + 1
        if idx[start] == pad_index:
            while end < len(idx) and idx[end] == pad_index:
                end += 1
            shape = list(arr.shape)
            shape[axis] = end - start
            pieces.append(jnp.zeros(shape, arr.dtype))
        else:
            while end < len(idx) and idx[end] == idx[end - 1] + 1 and idx[end] != pad_index:
                end += 1
            pieces.append(lax.slice_in_dim(arr, idx[start], idx[end - 1] + 1, axis=axis))
        start = end
    return jnp.concatenate(pieces, axis=axis)


def _dot(a, b):
    return jnp.dot(a, b, preferred_element_type=F32)


def _split3(x):
    hi = x.astype(BF16)
    r1 = x - hi.astype(F32)
    mid = r1.astype(BF16)
    lo = (r1 - mid.astype(F32)).astype(BF16)
    return hi, mid, lo


def _dot_exact_lhs(x, m_bf16):
    hi, mid, lo = _split3(x)
    return _dot(hi, m_bf16) + _dot(mid, m_bf16) + _dot(lo, m_bf16)


def _silu(x):
    return x * jax.nn.sigmoid(x)


def _seg_ones(width, seg):
    r = lax.broadcasted_iota(jnp.int32, (width, width), 0)
    c = lax.broadcasted_iota(jnp.int32, (width, width), 1)
    same = (r & ~(seg - 1)) == (c & ~(seg - 1))
    return jnp.where(same, 1.0, 0.0).astype(BF16)


def _seg_rms(x, seg, seg_mat):
    ss = _dot_exact_lhs(x * x, seg_mat)
    return x * lax.rsqrt(ss * (1.0 / seg) + EPS)


def _rope(x, cos, sin_signed, half):
    w = x.shape[-1]
    lane = lax.broadcasted_iota(jnp.int32, x.shape, 1)
    first = (lane & (2 * half - 1)) < half
    swapped = jnp.where(first, pltpu.roll(x, w - half, 1), pltpu.roll(x, half, 1))
    return x * cos + swapped * sin_signed


def _mod_kernel(cs_ref, w_ref, b_ref, o_ref):
    s = _silu(cs_ref[...]).astype(BF16)
    o_ref[...] = _dot(s, w_ref[...].astype(BF16)) + b_ref[...]


def _mod_call(cs, w_mod, b_mod):
    depth = w_mod.shape[0]
    nrow = cs.shape[0]
    tn = 1024
    return pl.pallas_call(
        _mod_kernel,
        grid=(depth, 3 * D_MODEL // tn),
        in_specs=[
            pl.BlockSpec((nrow, D_MODEL), lambda l, j: (0, 0)),
            pl.BlockSpec((None, D_MODEL, tn), lambda l, j: (l, 0, j)),
            pl.BlockSpec((None, 1, tn), lambda l, j: (l, 0, j)),
        ],
        out_specs=pl.BlockSpec((None, nrow, tn), lambda l, j: (l, 0, j)),
        out_shape=jax.ShapeDtypeStruct((depth, nrow, 3 * D_MODEL), F32),
        compiler_params=pltpu.CompilerParams(
            dimension_semantics=("arbitrary", "arbitrary")),
        name="mod_proj",
    )(cs, w_mod, b_mod.reshape(depth, 1, 3 * D_MODEL))


def _inproj_kernel(h_ref, mod_ref, g_ref, w_ref, o_ref):
    x = h_ref[...]
    ms = jnp.mean(x * x, axis=-1, keepdims=True)
    y = x * lax.rsqrt(ms + EPS) * g_ref[...]
    sh = mod_ref[:, 0:D_MODEL]
    sc = mod_ref[:, D_MODEL:2 * D_MODEL]
    u = (y * (1.0 + sc) + sh).astype(BF16)
    tn = 512
    for j in range(NP // tn):
        o_ref[:, j * tn:(j + 1) * tn] = _dot(u, w_ref[:, j * tn:(j + 1) * tn])


def _inproj_call(h, mod3, g_pre, w_bf16, tiles_per_row, fixed_row):
    n_tok = h.shape[0]
    tm = 512
    if fixed_row is None:
        mod_idx = lambda i: (i // tiles_per_row, 0, 0)
    else:
        mod_idx = lambda i: (fixed_row, 0, 0)
    return pl.pallas_call(
        _inproj_kernel,
        grid=(n_tok // tm,),
        in_specs=[
            pl.BlockSpec((tm, D_MODEL), lambda i: (i, 0)),
            pl.BlockSpec((None, 1, 3 * D_MODEL), mod_idx),
            pl.BlockSpec((1, D_MODEL), lambda i: (0, 0)),
            pl.BlockSpec((D_MODEL, NP), lambda i: (0, 0)),
        ],
        out_specs=pl.BlockSpec((tm, NP), lambda i: (i, 0)),
        out_shape=jax.ShapeDtypeStruct((n_tok, NP), F32),
        compiler_params=pltpu.CompilerParams(
            dimension_semantics=("arbitrary",), vmem_limit_bytes=VMEM_LIMIT),
        name="in_proj",
    )(h, mod3, g_pre, w_bf16)


def _outproj_kernel(ys_ref, yg_ref, yd_ref, h_ref, mod_ref, g_ref, w_ref, o_ref):
    tm = h_ref.shape[0]
    sub = min(OUTPROJ_SUB, tm)
    gt = mod_ref[:, 2 * D_MODEL:3 * D_MODEL]
    gain = g_ref[...]

    def project(r0):
        return (_dot(ys_ref[r0:r0 + sub, :], w_ref[0:512, :])
                + _dot(yg_ref[r0:r0 + sub, :], w_ref[512:768, :])
                + _dot(yd_ref[r0:r0 + sub, :], w_ref[768:1024, :]))

    o_next = project(0)
    for r0 in range(0, tm, sub):
        o = o_next
        if r0 + sub < tm:
            o_next = project(r0 + sub)
        ms = jnp.mean(o * o, axis=-1, keepdims=True)
        n = o * lax.rsqrt(ms + EPS) * gain
        o_ref[r0:r0 + sub, :] = h_ref[r0:r0 + sub, :] + gt * n


def _outproj_call(ys, yg, yd, h, mod3, g_post, w_bf16, rows_per_mod, fixed_row):
    n_tok = h.shape[0]
    tm = OUTPROJ_TM
    assert n_tok % tm == 0
    if fixed_row is None:
        assert rows_per_mod % tm == 0
        mod_idx = lambda i: (i // (rows_per_mod // tm), 0, 0)
    else:
        mod_idx = lambda i: (fixed_row, 0, 0)
    return pl.pallas_call(
        _outproj_kernel,
        grid=(n_tok // tm,),
        in_specs=[
            pl.BlockSpec((tm, 512), lambda i: (i, 0)),
            pl.BlockSpec((tm, 256), lambda i: (i, 0)),
            pl.BlockSpec((tm, 256), lambda i: (i, 0)),
            pl.BlockSpec((tm, D_MODEL), lambda i: (i, 0)),
            pl.BlockSpec((None, 1, 3 * D_MODEL), mod_idx),
            pl.BlockSpec((1, D_MODEL), lambda i: (0, 0)),
            pl.BlockSpec((D_MODEL, D_MODEL), lambda i: (0, 0)),
        ],
        out_specs=pl.BlockSpec((tm, D_MODEL), lambda i: (i, 0)),
        out_shape=jax.ShapeDtypeStruct((n_tok, D_MODEL), F32),
        compiler_params=pltpu.CompilerParams(
            dimension_semantics=("arbitrary",), vmem_limit_bytes=VMEM_LIMIT),
        name="out_proj",
    )(ys, yg, yd, h, mod3, g_post, w_bf16)


def _attend_many(lhs_list, kt_ref, vext_refs):
    def scores(i):
        s = _dot(lhs_list[i], kt_ref[...])
        return s, jnp.max(s, axis=-1, keepdims=True)

    outs = []
    n = len(lhs_list)
    ahead = [scores(i) for i in range(min(ATTN_AHEAD, n))]
    for i, vext_ref in enumerate(vext_refs):
        s, m = ahead.pop(0)
        if i + ATTN_AHEAD < n:
            ahead.append(scores(i + ATTN_AHEAD))
        p = jnp.exp2(s - m).astype(BF16)
        oe = _dot(p, vext_ref[...])
        outs.append(oe[:, 0:128] / oe[:, 128:256])
    return outs


def _attend_diff_pairs(lhs_list, kt_ref, v_refs, lam):
    n_heads = len(v_refs)

    def scores(h):
        s_a = _dot(lhs_list[2 * h], kt_ref[...])
        m_a = jnp.max(s_a, axis=-1, keepdims=True)
        s_b = _dot(lhs_list[2 * h + 1], kt_ref[...])
        m_b = jnp.max(s_b, axis=-1, keepdims=True)
        return s_a, m_a, s_b, m_b

    outs = []
    ahead = [scores(h) for h in range(min(DIFF_AHEAD, n_heads))]
    for h in range(n_heads):
        s_a, m_a, s_b, m_b = ahead.pop(0)
        if h + DIFF_AHEAD < n_heads:
            ahead.append(scores(h + DIFF_AHEAD))
        e_a = jnp.exp2(s_a - m_a)
        e_b = jnp.exp2(s_b - m_b)
        l_a = jnp.sum(e_a, axis=-1, keepdims=True)
        l_b = jnp.sum(e_b, axis=-1, keepdims=True)
        pc = (e_a - (lam * l_a / l_b) * e_b).astype(BF16)
        outs.append(_dot(pc, v_refs[h][...]) / l_a)
    return outs


def _attend_diff_pairs_bounded(lhs_list, bounds, kt_ref, v_refs, lam):
    n_heads = len(v_refs)

    def exps(h):
        e_a = jnp.exp2(_dot(lhs_list[2 * h], kt_ref[...]) - bounds[2 * h])
        e_b = jnp.exp2(_dot(lhs_list[2 * h + 1], kt_ref[...]) - bounds[2 * h + 1])
        return e_a, e_b

    outs, l_min = [], None
    ahead = [exps(h) for h in range(min(DIFF_AHEAD, n_heads))]
    for h in range(n_heads):
        e_a, e_b = ahead.pop(0)
        if h + DIFF_AHEAD < n_heads:
            ahead.append(exps(h + DIFF_AHEAD))
        l_a = jnp.sum(e_a, axis=-1, keepdims=True)
        l_b = jnp.sum(e_b, axis=-1, keepdims=True)
        pc = (e_a - (lam * l_a / l_b) * e_b).astype(BF16)
        outs.append(_dot(pc, v_refs[h][...]) / l_a)
        l_ab = jnp.minimum(l_a, l_b)
        l_min = l_ab if l_min is None else jnp.minimum(l_min, l_ab)
    return outs, jnp.min(l_min, axis=0, keepdims=True)


def _gqa_kernel(*refs, n_ctx, n_lat, rope_q):
    it = iter(refs)
    q_ref, gg_ref, kvc_ref = next(it), next(it), next(it)
    kvl_ref = next(it) if n_lat else None
    if rope_q:
        cosq_ref, sinq_ref = next(it), next(it)
    if n_lat:
        cosk_ref, sink_ref = next(it), next(it)
    qg_ref, kg_ref = next(it), next(it)
    y_ref = next(it)
    kt_s, vext_s = next(it), next(it)

    seg128 = _seg_ones(128, 64)

    @pl.when(pl.program_id(1) == 0)
    def _prep_kv():
        kc = _seg_rms(kvc_ref[:, 0:128], 64, seg128) * kg_ref[...]
        kt_s[:, 0:n_ctx] = kc.T.astype(BF16)
        vext_s[0:n_ctx, 0:128] = kvc_ref[:, 128:256].astype(BF16)
        if n_lat:
            kl = _seg_rms(kvl_ref[:, 0:128], 64, seg128) * kg_ref[...]
            kl = _rope(kl, cosk_ref[...], sink_ref[...], 32)
            kt_s[:, n_ctx:n_ctx + n_lat] = kl.T.astype(BF16)
            vext_s[n_ctx:n_ctx + n_lat, 0:128] = kvl_ref[:, 128:256].astype(BF16)
        vext_s[:, 128:256] = jnp.ones((n_ctx + n_lat, 128), BF16)

    seg256 = _seg_ones(256, 64)
    q = _seg_rms(q_ref[...], 64, seg256) * qg_ref[...]
    if rope_q:
        q = _rope(q, cosq_ref[...], sinq_ref[...], 32)
    q = q * (GQA_HEAD_DIM ** -0.5 * LOG2E)
    tq = q.shape[0]
    sub = min(ATTN_SUB, tq)
    lane = lax.broadcasted_iota(jnp.int32, (sub, 128), 1)
    lhs_list = []
    for r0 in range(0, tq, sub):
        for half in range(2):
            qh = q[r0:r0 + sub, 128 * half:128 * half + 128]
            for kv in range(2):
                in_kv = (lane >= 64 * kv) & (lane < 64 * kv + 64)
                lhs_list.append(jnp.where(in_kv, qh, 0.0).astype(BF16))
    outs = _attend_many(lhs_list, kt_s, [vext_s] * len(lhs_list))
    for j, r0 in enumerate(range(0, tq, sub)):
        for half in range(2):
            o = jnp.where(lane < 64, outs[4 * j + 2 * half], outs[4 * j + 2 * half + 1])
            gate = _silu(gg_ref[r0:r0 + sub, 128 * half:128 * half + 128])
            y_ref[r0:r0 + sub, 128 * half:128 * half + 128] = (o * gate).astype(BF16)


def _gqa_call(p_q, p_c, p_l, tabs, qg, kg, batch, n_ctx, n_lat, rope_q):
    t_total = p_q.shape[0] // batch
    tq = min(ATTN_TQ, t_total)
    nq = t_total // tq
    in_specs = [
        pl.BlockSpec((tq, 256), lambda b, i: (b * nq + i, 6)),
        pl.BlockSpec((tq, 256), lambda b, i: (b * nq + i, 8)),
        pl.BlockSpec((n_ctx, 256), lambda b, i: (b, 7)),
    ]
    args = [p_q, p_q, p_c]
    if n_lat:
        in_specs.append(pl.BlockSpec((n_lat, 256), lambda b, i: (b, 7)))
        args.append(p_l)
    if rope_q:
        in_specs += [pl.BlockSpec((tq, 256), lambda b, i: (i, 0))] * 2
        args += [tabs["cos_g"], tabs["sin_g"]]
    if n_lat:
        in_specs += [pl.BlockSpec((n_lat, 128), lambda b, i: (0, 0))] * 2
        args += [tabs["cos_g"], tabs["sin_g"]]
    in_specs += [pl.BlockSpec((1, 256), lambda b, i: (0, 0)),
                 pl.BlockSpec((1, 128), lambda b, i: (0, 0))]
    args += [qg, kg]
    s_keys = n_ctx + n_lat
    return pl.pallas_call(
        functools.partial(_gqa_kernel, n_ctx=n_ctx, n_lat=n_lat, rope_q=rope_q),
        grid=(batch, nq),
        in_specs=in_specs,
        out_specs=pl.BlockSpec((tq, 256), lambda b, i: (b * nq + i, 0)),
        out_shape=jax.ShapeDtypeStruct((p_q.shape[0], 256), BF16),
        scratch_shapes=[pltpu.VMEM((128, s_keys), BF16),
                        pltpu.VMEM((s_keys, 256), BF16)],
        compiler_params=pltpu.CompilerParams(
            dimension_semantics=("arbitrary", "arbitrary"),
            vmem_limit_bytes=VMEM_LIMIT),
        name="gqa_attn",
    )(*args)


def _diff_kernel(*refs, n_ctx, n_lat, rope_q, lam_init, bounded):
    it = iter(refs)
    q_ref, dg_ref, kc_ref, vc_ref = next(it), next(it), next(it), next(it)
    if n_lat:
        kl_ref, vl_ref = next(it), next(it)
    if rope_q:
        cosq_ref, sinq_ref = next(it), next(it)
    if n_lat:
        cosk_ref, sink_ref = next(it), next(it)
    lam_ref, ng_ref = next(it), next(it)
    y_ref = next(it)
    lmin_ref = next(it) if bounded else None
    kt_s, vlo_s, vhi_s = next(it), next(it), next(it)
    kmax_s = next(it) if bounded else None
    s_keys = n_ctx + n_lat
    seg32 = _seg_ones(256, DIFF_QK_DIM)

    def map_norms(x):
        return jnp.sqrt(_dot((x * x).astype(BF16), seg32))

    @pl.when(pl.program_id(1) == 0)
    def _prep_kv():
        kc = kc_ref[...]
        kt_s[:, 0:n_ctx] = kc.T.astype(BF16)
        vlo_s[0:n_ctx, 0:128] = vc_ref[:, 0:128].astype(BF16)
        vhi_s[0:n_ctx, 0:128] = vc_ref[:, 128:256].astype(BF16)
        if bounded:
            kmax = jnp.max(map_norms(kc), axis=0, keepdims=True)
        if n_lat:
            kl = _rope(kl_ref[...], cosk_ref[...], sink_ref[...], 16)
            kt_s[:, n_ctx:s_keys] = kl.T.astype(BF16)
            vlo_s[n_ctx:s_keys, 0:128] = vl_ref[:, 0:128].astype(BF16)
            vhi_s[n_ctx:s_keys, 0:128] = vl_ref[:, 128:256].astype(BF16)
            if bounded:
                kmax = jnp.maximum(kmax, jnp.max(map_norms(kl), axis=0, keepdims=True))
        if bounded:
            kmax_s[...] = jnp.broadcast_to(kmax, kmax_s.shape)

    lp = lam_ref[...]
    lam = (jnp.exp(jnp.sum(lp[0:1, :] * lp[1:2, :], axis=-1, keepdims=True))
           - jnp.exp(jnp.sum(lp[2:3, :] * lp[3:4, :], axis=-1, keepdims=True))
           + lam_init)

    q = q_ref[...]
    if rope_q:
        q = _rope(q, cosq_ref[...], sinq_ref[...], 16)
    q = q * (DIFF_QK_DIM ** -0.5 * LOG2E)
    tq = q.shape[0]
    sub = min(DIFF_SUB, tq)
    lane256 = lax.broadcasted_iota(jnp.int32, (sub, 256), 1)
    lane128 = lax.broadcasted_iota(jnp.int32, (sub, 128), 1)
    seg128 = _seg_ones(128, 64)
    if bounded:
        bound_all = map_norms(q) * kmax_s[0:1, :] * SCORE_BOUND_MARGIN
    lhs_list, v_list, bounds = [], [], []
    for r0 in range(0, tq, sub):
        for mp in range(2 * DIFF_HEADS):
            in_map = (lane256 >= 32 * mp) & (lane256 < 32 * mp + 32)
            lhs_list.append(jnp.where(in_map, q[r0:r0 + sub, :], 0.0).astype(BF16))
            if bounded:
                bounds.append(bound_all[r0:r0 + sub, 32 * mp:32 * mp + 1])
        v_list += [vlo_s, vlo_s, vhi_s, vhi_s]

    if bounded:
        heads, l_min = _attend_diff_pairs_bounded(lhs_list, bounds, kt_s, v_list, lam)
        lmin_ref[...] = jnp.broadcast_to(l_min, lmin_ref.shape)
    else:
        heads = _attend_diff_pairs(lhs_list, kt_s, v_list, lam)
    for j, r0 in enumerate(range(0, tq, sub)):
        for half in range(2):
            o = jnp.where(lane128 < 64, heads[4 * j + 2 * half], heads[4 * j + 2 * half + 1])
            n = _seg_rms(o, 64, seg128) * ng_ref[:, 128 * half:128 * half + 128]
            n = n * (1.0 - lam_init)
            gate = _silu(dg_ref[r0:r0 + sub, 128 * half:128 * half + 128])
            y_ref[r0:r0 + sub, 128 * half:128 * half + 128] = (n * gate).astype(BF16)


def _diff_call(p_q, p_c, p_l, tabs, lam_params, ng, batch, n_ctx, n_lat, rope_q,
               lam_init, bounded):
    t_total = p_q.shape[0] // batch
    tq = min(ATTN_TQ, t_total)
    nq = t_total // tq
    in_specs = [
        pl.BlockSpec((tq, 256), lambda b, i: (b * nq + i, 9)),
        pl.BlockSpec((tq, 256), lambda b, i: (b * nq + i, 12)),
        pl.BlockSpec((n_ctx, 256), lambda b, i: (b, 10)),
        pl.BlockSpec((n_ctx, 256), lambda b, i: (b, 11)),
    ]
    args = [p_q, p_q, p_c, p_c]
    if n_lat:
        in_specs += [pl.BlockSpec((n_lat, 256), lambda b, i: (b, 10)),
                     pl.BlockSpec((n_lat, 256), lambda b, i: (b, 11))]
        args += [p_l, p_l]
    if rope_q:
        in_specs += [pl.BlockSpec((tq, 256), lambda b, i: (i, 0))] * 2
        args += [tabs["cos_d"], tabs["sin_d"]]
    if n_lat:
        in_specs += [pl.BlockSpec((n_lat, 256), lambda b, i: (0, 0))] * 2
        args += [tabs["cos_d"], tabs["sin_d"]]
    in_specs += [pl.BlockSpec((4, DIFF_QK_DIM), lambda b, i: (0, 0)),
                 pl.BlockSpec((1, 256), lambda b, i: (0, 0))]
    args += [lam_params, ng]
    s_keys = n_ctx + n_lat
    out_specs = [pl.BlockSpec((tq, 256), lambda b, i: (b * nq + i, 0))]
    out_shape = [jax.ShapeDtypeStruct((p_q.shape[0], 256), BF16)]
    scratch = [pltpu.VMEM((256, s_keys), BF16),
               pltpu.VMEM((s_keys, 128), BF16),
               pltpu.VMEM((s_keys, 128), BF16)]
    if bounded:
        out_specs.append(pl.BlockSpec((None, 8, 128), lambda b, i: (b * nq + i, 0, 0)))
        out_shape.append(jax.ShapeDtypeStruct((batch * nq, 8, 128), F32))
        scratch.append(pltpu.VMEM((8, 256), F32))
    outs = pl.pallas_call(
        functools.partial(_diff_kernel, n_ctx=n_ctx, n_lat=n_lat, rope_q=rope_q,
                          lam_init=lam_init, bounded=bounded),
        grid=(batch, nq),
        in_specs=in_specs,
        out_specs=out_specs,
        out_shape=out_shape,
        scratch_shapes=scratch,
        compiler_params=pltpu.CompilerParams(
            dimension_semantics=("arbitrary", "arbitrary"),
            vmem_limit_bytes=VMEM_LIMIT),
        name="diff_attn_bounded" if bounded else "diff_attn",
    )(*args)
    return (outs[0], outs[1]) if bounded else outs[0]


def _ssd_kernel(xc_ref, xl_ref, bc_ref, bl_ref, cc_ref, cl_ref, zc_ref, zl_ref,
                dtc_ref, dtl_ref, cwx_ref, cwb_ref, cwc_ref, cbx_ref, cbb_ref,
                cbc_ref, par_ref, dsk_ref, ng_ref, *rest, n_ctx, n_lat, ctx_out):
    if ctx_out:
        oc_ref, ol_ref = rest[0], rest[1]
        rest = rest[2:]
    else:
        oc_ref, ol_ref = None, rest[0]
        rest = rest[1:]
    xp_s, dtr_s, yacc_s, sloc_s, e_s, cbf_s, sin_s, dec_s, vun_s = rest

    g = pl.program_id(1)
    ncc = n_ctx // CHUNK
    ncl = n_lat // CHUNK
    nch = ncc + ncl
    GROUP = next(n for n in (6, 3, 2, 1) if nch % n == 0)
    t_all = n_ctx + n_lat
    lat0 = n_ctx + 2 * HALO

    zeros_h = jnp.zeros((HALO, 512), F32)
    xp_s[0:HALO, :] = zeros_h
    xp_s[HALO:HALO + n_ctx, 0:256] = xc_ref[...]
    xp_s[HALO:HALO + n_ctx, 256:384] = bc_ref[...]
    xp_s[HALO:HALO + n_ctx, 384:512] = cc_ref[...]
    xp_s[HALO + n_ctx:lat0, :] = zeros_h
    xp_s[lat0:lat0 + n_lat, 0:256] = xl_ref[...]
    xp_s[lat0:lat0 + n_lat, 256:384] = bl_ref[...]
    xp_s[lat0:lat0 + n_lat, 384:512] = cl_ref[...]
    xp_s[lat0 + n_lat:lat0 + n_lat + HALO, :] = zeros_h

    dtr_s[0:n_ctx, :] = dtc_ref[...]
    dtr_s[n_ctx:t_all, :] = dtl_ref[...]
    a_col = -jnp.exp(par_ref[0:16, :])
    bias_col = par_ref[16:32, :]

    def _softplus(v):
        return jnp.maximum(v, 0.0) + jnp.log1p(jnp.exp(-jnp.abs(v)))

    r128 = lax.broadcasted_iota(jnp.int32, (CHUNK, CHUNK), 0)
    c128 = lax.broadcasted_iota(jnp.int32, (CHUNK, CHUNK), 1)
    lower = c128 <= r128
    upper = c128 >= r128
    tril = jnp.where(lower, 1.0, 0.0).astype(BF16)
    triu = jnp.where(upper, 1.0, 0.0).astype(BF16)
    fwd_row = lax.broadcasted_iota(jnp.int32, (16, CHUNK), 0) < 8
    fwd_row1 = lax.broadcasted_iota(jnp.int32, (16, 1), 0) < 8
    er = lax.broadcasted_iota(jnp.int32, (CHUNK, 1024), 0)
    ec = lax.broadcasted_iota(jnp.int32, (CHUNK, 1024), 1)
    src_lane = 16 + 16 * (ec >> 9) + 8 * ((ec >> 8) & 1) + ((ec >> 6) & 3)
    expand = jnp.where(er == src_lane, 1.0, 0.0).astype(BF16)
    er0 = lax.broadcasted_iota(jnp.int32, (CHUNK, 512), 0)
    ec0 = lax.broadcasted_iota(jnp.int32, (CHUNK, 512), 1)
    expand_tot = jnp.where(er0 == 8 * (ec0 >> 8) + ((ec0 >> 6) & 3), 1.0, 0.0).astype(BF16)
    lane256 = lax.broadcasted_iota(jnp.int32, (CHUNK, 256), 1)
    lane128_1 = lax.broadcasted_iota(jnp.int32, (1, CHUNK), 1)

    cw = jnp.concatenate([cwx_ref[...], cwb_ref[...], cwc_ref[...]], axis=1)
    cb = jnp.concatenate([cbx_ref[...], cbb_ref[...], cbc_ref[...]], axis=1)
    dsk = dsk_ref[...]

    def phase_a(grp, carry):
        cs = [grp * GROUP + j for j in range(GROUP)]
        r0s = [pl.multiple_of(c * CHUNK, CHUNK) for c in cs]

        def conv(c):
            wstart = pl.multiple_of(c * CHUNK + jnp.where(c >= ncc, HALO, 0), 8)
            win = xp_s[pl.ds(wstart, CHUNK + 2 * HALO), :]
            acc = jnp.broadcast_to(cb, (CHUNK, 512))
            for k in range(CONV_K):
                d = k - CONV_K // 2
                if d == 0:
                    tap = win[HALO:HALO + CHUNK, :]
                else:
                    tap = pltpu.roll(win, (-d) % (CHUNK + 2 * HALO), 0)[HALO:HALO + CHUNK, :]
                acc = acc + cw[k:k + 1, :] * tap
            return _silu(acc)

        dtts = [_softplus(dtr_s[pl.ds(r0, CHUNK), :].T[0:16, :] + bias_col) for r0 in r0s]
        a_ts = [dtt * a_col for dtt in dtts]
        acol_ts = [jnp.where(fwd_row, _dot_exact_lhs(a_t, triu), _dot_exact_lhs(a_t, tril))
                   for a_t in a_ts]
        us = [conv(c) for c in cs]
        xus = [u[:, 0:256] for u in us]
        bts = [u[:, 256:384].T.astype(BF16) for u in us]
        cbfs = [u[:, 384:512].astype(BF16) for u in us]
        xbfs = [xu.astype(BF16) for xu in xus]
        gmats = [_dot(cbf, bt) for cbf, bt in zip(cbfs, bts)]
        tms = []
        for dtt, acol_t in zip(dtts, acol_ts):
            tot = jnp.where(fwd_row1, acol_t[:, CHUNK - 1:CHUNK], acol_t[:, 0:1])
            w_t = dtt * jnp.exp(tot - acol_t)
            e_t = jnp.exp(acol_t)
            stacked = jnp.concatenate(
                [acol_t, w_t, e_t, jnp.zeros((CHUNK - 48, CHUNK), F32)], axis=0)
            tms.append(stacked.T)
        wes = [_dot(tm.astype(BF16), expand) for tm in tms]
        decs = []
        for tm in tms:
            tot_row = jnp.where(lane128_1 < 8, tm[CHUNK - 1:CHUNK, :], tm[0:1, :])
            tot512 = _dot_exact_lhs(jnp.broadcast_to(tot_row, (8, CHUNK)), expand_tot)
            decs.append(jnp.exp(tot512))
        ydiags = [jnp.zeros((CHUNK, 256), F32) for _ in cs]
        for i in range(4):
            in_head = (lane256 >= 64 * i) & (lane256 < 64 * i + 64)
            for j in range(GROUP):
                tm, acol_t, dtt = tms[j], acol_ts[j], dtts[j]
                arg = jnp.where(lower, tm[:, i:i + 1] - acol_t[i:i + 1, :],
                                tm[:, 8 + i:9 + i] - acol_t[8 + i:9 + i, :])
                scale = (jnp.where(lower, dtt[i:i + 1, :], 0.0)
                         + jnp.where(upper, dtt[8 + i:9 + i, :], 0.0))
                wmat = (gmats[j] * jnp.exp(arg) * scale).astype(BF16)
                ydiags[j] = jnp.where(in_head, _dot(wmat, xbfs[j]), ydiags[j])
        for j, c in enumerate(cs):
            w512 = wes[j][:, 0:512]
            e512 = wes[j][:, 512:1024]
            xdw = (jnp.concatenate([xus[j], xus[j]], axis=1) * w512).astype(BF16)
            sloc_s[c] = _dot(bts[j], xdw)
            dec_s[c] = decs[j]
            e_s[c] = e512
            cbf_s[c] = cbfs[j]
            yacc_s[pl.ds(r0s[j], CHUNK), :] = ydiags[j] + xus[j] * dsk
        return carry

    lax.fori_loop(0, nch // GROUP, phase_a, 0)

    fwd_order = list(range(nch))
    bwd_order = list(range(ncc - 1, -1, -1)) + list(range(nch - 1, ncc - 1, -1))
    for order, lo in ((fwd_order, 0), (bwd_order, 256)):
        state = jnp.zeros((SSD_STATE, 256), F32)
        for c in order:
            sin_s[c, :, lo:lo + 256] = state.astype(BF16)
            state = (state * dec_s[c, 0:1, lo:lo + 256]
                     + sloc_s[c, :, lo:lo + 256])

    def phase_c(cs, z_ref, zrows):
        yos = [_dot(cbf_s[c], sin_s[c]) for c in cs]
        vs = []
        for c, zrow, yo in zip(cs, zrows, yos):
            r0 = c * CHUNK if isinstance(c, int) else pl.multiple_of(c * CHUNK, CHUNK)
            yo = yo * e_s[c]
            y = yacc_s[pl.ds(r0, CHUNK), :] + yo[:, 0:256] + yo[:, 256:512]
            vs.append((r0, y * _silu(z_ref[pl.ds(zrow, CHUNK), :])))

        @pl.when(g == 0)
        def _():
            for r0, v in vs:
                vun_s[pl.ds(r0, CHUNK), 0:256] = v

        @pl.when(g == 1)
        def _():
            for r0, v in vs:
                vun_s[pl.ds(r0, CHUNK), 256:512] = v

    if ctx_out:
        phase_c(list(range(ncc)), zc_ref, [c * CHUNK for c in range(ncc)])

    group_c = next(n for n in (4, 2, 1) if ncl % n == 0)

    def phase_c_lat(k, carry):
        ks = [k * group_c + j for j in range(group_c)]
        phase_c([kk + ncc for kk in ks], zl_ref,
                [pl.multiple_of(kk * CHUNK, CHUNK) for kk in ks])
        return carry

    lax.fori_loop(0, ncl // group_c, phase_c_lat, 0)

    @pl.when(g == SSD_GROUPS - 1)
    def _finalize():
        ng = ng_ref[...]

        def norm_rows(r0, nrows):
            v = vun_s[pl.ds(r0, nrows), :]
            ms = jnp.mean(v * v, axis=-1, keepdims=True)
            return (v * lax.rsqrt(ms + EPS) * ng).astype(BF16)

        if ctx_out:
            oc_ref[...] = norm_rows(0, n_ctx)

        def fin(k, carry):
            r0 = pl.multiple_of(k * 256, 256)
            ol_ref[pl.ds(r0, 256), :] = norm_rows(n_ctx + r0, 256)
            return carry

        lax.fori_loop(0, n_lat // 256, fin, 0)


def _ssd_call(p_c, p_l, conv_w8, conv_b, par, dsk, ng, batch, n_ctx, n_lat, ctx_out):
    nch = (n_ctx + n_lat) // CHUNK
    t_all = n_ctx + n_lat
    in_specs = [
        pl.BlockSpec((n_ctx, 256), lambda b, g: (b, g)),
        pl.BlockSpec((n_lat, 256), lambda b, g: (b, g)),
        pl.BlockSpec((n_ctx, 128), lambda b, g: (b, 4 + g)),
        pl.BlockSpec((n_lat, 128), lambda b, g: (b, 4 + g)),
        pl.BlockSpec((n_ctx, 128), lambda b, g: (b, 6 + g)),
        pl.BlockSpec((n_lat, 128), lambda b, g: (b, 6 + g)),
        pl.BlockSpec((n_ctx, 256), lambda b, g: (b, 4 + g)),
        pl.BlockSpec((n_lat, 256), lambda b, g: (b, 4 + g)),
        pl.BlockSpec((n_ctx, 128), lambda b, g: (b, 26 + g)),
        pl.BlockSpec((n_lat, 128), lambda b, g: (b, 26 + g)),
        pl.BlockSpec((8, 256), lambda b, g: (0, g)),
        pl.BlockSpec((8, 128), lambda b, g: (0, 4 + g)),
        pl.BlockSpec((8, 128), lambda b, g: (0, 6 + g)),
        pl.BlockSpec((1, 256), lambda b, g: (0, g)),
        pl.BlockSpec((1, 128), lambda b, g: (0, 4 + g)),
        pl.BlockSpec((1, 128), lambda b, g: (0, 6 + g)),
        pl.BlockSpec((None, 32, 128), lambda b, g: (g, 0, 0)),
        pl.BlockSpec((1, 256), lambda b, g: (0, g)),
        pl.BlockSpec((1, 512), lambda b, g: (0, 0)),
    ]
    args = [p_c, p_l, p_c, p_l, p_c, p_l, p_c, p_l, p_c, p_l,
            conv_w8, conv_w8, conv_w8, conv_b, conv_b, conv_b, par, dsk, ng]
    out_specs = [pl.BlockSpec((n_lat, 512), lambda b, g: (b, 0))]
    out_shape = [jax.ShapeDtypeStruct((batch * n_lat, 512), BF16)]
    if ctx_out:
        out_specs = [pl.BlockSpec((n_ctx, 512), lambda b, g: (b, 0))] + out_specs
        out_shape = [jax.ShapeDtypeStruct((batch * n_ctx, 512), BF16)] + out_shape
    scratch = [
        pltpu.VMEM((t_all + 3 * HALO, 512), F32),
        pltpu.VMEM((t_all, 128), F32),
        pltpu.VMEM((t_all, 256), F32),
        pltpu.VMEM((nch, SSD_STATE, 512), F32),
        pltpu.VMEM((nch, CHUNK, 512), F32),
        pltpu.VMEM((nch, CHUNK, 128), BF16),
        pltpu.VMEM((nch, SSD_STATE, 512), BF16),
        pltpu.VMEM((nch, 8, 512), F32),
        pltpu.VMEM((t_all, 512), F32),
    ]
    outs = pl.pallas_call(
        functools.partial(_ssd_kernel, n_ctx=n_ctx, n_lat=n_lat, ctx_out=ctx_out),
        grid=(batch, SSD_GROUPS),
        in_specs=in_specs,
        out_specs=out_specs,
        out_shape=out_shape,
        scratch_shapes=scratch,
        compiler_params=pltpu.CompilerParams(
            dimension_semantics=("arbitrary", "arbitrary"),
            vmem_limit_bytes=VMEM_LIMIT),
        name="ssd_scan",
    )(*args)
    if ctx_out:
        return outs[1], outs[0]
    return outs[0], None


def _rope_tables(n_lat):
    rows = n_lat // GRID_W
    row_idx = np.repeat(np.arange(rows), GRID_W).astype(np.float32)
    col_idx = (np.arange(rows * GRID_W) % GRID_W).astype(np.float32)

    def tables(dim, reps):
        quarter = dim // 4
        inv = (ROPE_BASE ** (-np.arange(quarter, dtype=np.float32) / quarter)).astype(np.float32)
        ang = np.concatenate([row_idx[:, None] * inv, col_idx[:, None] * inv], axis=-1)
        cos, sin = np.cos(ang.astype(np.float64)), np.sin(ang.astype(np.float64))
        cos2 = np.concatenate([cos, cos], axis=-1).astype(np.float32)
        sin2 = np.concatenate([-sin, sin], axis=-1).astype(np.float32)
        return jnp.asarray(np.tile(cos2, (1, reps))), jnp.asarray(np.tile(sin2, (1, reps)))

    cos_g, sin_g = tables(GQA_HEAD_DIM, 4)
    cos_d, sin_d = tables(DIFF_QK_DIM, 8)
    return {"cos_g": cos_g, "sin_g": sin_g, "cos_d": cos_d, "sin_d": sin_d}


def kernel(x, c, ctx, c_ctx, w_mod, b_mod, g_pre, g_post, w_in, conv_w, conv_b,
           a_log_fwd, a_log_bwd, dt_bias_fwd, dt_bias_bwd, d_skip, ssd_norm_g,
           q_norm_g, k_norm_g, diff_lambda, diff_norm_g, w_out):
    batch, n_lat, _ = x.shape
    n_ctx = ctx.shape[1]
    depth = w_mod.shape[0]
    assert n_lat % 512 == 0 and n_ctx % 256 == 0 and (batch * n_ctx) % 512 == 0
    assert batch + 1 <= 16

    in_perm, out_perm = _in_col_perm(), _out_row_perm()
    w_in_p = [_take_runs(w_in[l], in_perm, 1, IN_COLS).astype(BF16) for l in range(depth)]
    w_out_p = [_take_runs(w_out[l], out_perm, 0, None).astype(BF16) for l in range(depth)]
    conv_w8 = jnp.pad(conv_w, ((0, 0), (0, 8 - CONV_K), (0, 0)))
    conv_b1 = conv_b[:, None, :]

    def group16(fwd, bwd):
        out = jnp.zeros((depth, SSD_GROUPS, 16), F32)
        for g in range(SSD_GROUPS):
            out = out.at[:, g, 0:4].set(fwd[:, 4 * g:4 * g + 4])
            out = out.at[:, g, 8:12].set(bwd[:, 4 * g:4 * g + 4])
        return out

    ssd_par = jnp.broadcast_to(
        jnp.concatenate([group16(a_log_fwd, a_log_bwd),
                         group16(dt_bias_fwd, dt_bias_bwd)], axis=-1)[..., None],
        (depth, SSD_GROUPS, 32, 128))
    dsk = jnp.repeat(d_skip, SSD_HEAD_DIM, axis=1)[:, None, :]
    qg = jnp.tile(q_norm_g, (1, 4))[:, None, :]
    kg = jnp.tile(k_norm_g, (1, 2))[:, None, :]
    dng = jnp.tile(diff_norm_g, (1, 4))[:, None, :]
    tabs = _rope_tables(n_lat)

    cs = jnp.concatenate(
        [c, c_ctx[None, :], jnp.zeros((16 - batch - 1, D_MODEL), F32)], axis=0)
    mod_all = _mod_call(cs, w_mod, b_mod)

    h = x.reshape(batch * n_lat, D_MODEL)
    hc = ctx.reshape(batch * n_ctx, D_MODEL)
    for l in range(depth):
        ctx_out = l < depth - 1
        lam_init = 0.8 - 0.6 * float(np.exp(-0.3 * l))
        mod3 = mod_all[l][:, None, :]
        p_l = _inproj_call(h, mod3, g_pre[l][None, :], w_in_p[l], n_lat // 512, None)
        p_c = _inproj_call(hc, mod3, g_pre[l][None, :], w_in_p[l], None, batch)

        ys_l, ys_c = _ssd_call(p_c, p_l, conv_w8[l], conv_b1[l], ssd_par[l], dsk[l],
                               ssd_norm_g[l][None, :], batch, n_ctx, n_lat, ctx_out)
        yg_l = _gqa_call(p_l, p_c, p_l, tabs, qg[l], kg[l], batch, n_ctx, n_lat, True)
        diff_args = (p_l, p_c, p_l, tabs, diff_lambda[l], dng[l], batch, n_ctx, n_lat, True,
                     lam_init)
        yd_fast, row_sum_min = _diff_call(*diff_args, True)
        yd_l = lax.cond(jnp.min(row_sum_min) >= MIN_ROW_SUM,
                        lambda: yd_fast, lambda: _diff_call(*diff_args, False))
        h = _outproj_call(ys_l, yg_l, yd_l, h, mod3, g_post[l][None, :], w_out_p[l],
                          n_lat, None)
        if ctx_out:
            yg_c = _gqa_call(p_c, p_c, None, tabs, qg[l], kg[l], batch, n_ctx, 0, False)
            yd_c = _diff_call(p_c, p_c, None, tabs, diff_lambda[l], dng[l], batch,
                              n_ctx, 0, False, lam_init, False)
            hc = _outproj_call(ys_c, yg_c, yd_c, hc, mod3, g_post[l][None, :],
                               w_out_p[l], None, batch)
    return h.reshape(batch, n_lat, D_MODEL)
```

```python
import functools

import numpy as np
import jax
import jax.numpy as jnp
from jax import lax
from jax.experimental import pallas as pl
from jax.experimental.pallas import tpu as pltpu

F32 = jnp.float32
BF16 = jnp.bfloat16

D_MODEL = 1024
GRID_W = 64
ROPE_BASE = 10000.0
EPS = 1e-6
LOG2E = 1.4426950408889634

SSD_WIDTH = 512
SSD_HEADS = 8
SSD_HEAD_DIM = 64
SSD_GROUPS = 2
SSD_STATE = 128
CHUNK = 128
CONV_K = 5
HALO = 8
GQA_HEADS = 4
GQA_HEAD_DIM = 64
DIFF_HEADS = 4
DIFF_QK_DIM = 32
DIFF_V_DIM = 64

_IN_SPLITS = (("xbc", 1024), ("z", 512), ("dt", 16), ("gq", 256), ("gk", 128),
              ("gv", 128), ("gg", 256), ("dq", 256), ("dk", 256), ("dv", 256),
              ("dg", 256))
IN_COLS = sum(s for _, s in _IN_SPLITS)
NP = 28 * 128
GQ_HEAD_ORDER = (0, 2, 1, 3)

VMEM_LIMIT = 56 * 1024 * 1024
ATTN_TQ = 2048
ATTN_SUB = 512
DIFF_SUB = 512
ATTN_AHEAD = 2
DIFF_AHEAD = 1
SCORE_BOUND_MARGIN = 1.02
MIN_ROW_SUM = 2.0 ** -90
OUTPROJ_TM = 1024
OUTPROJ_SUB = 256


def _in_col_perm():
    off, o = {}, 0
    for name, size in _IN_SPLITS:
        off[name] = o
        o += size
    pad = IN_COLS
    cols = list(range(off["xbc"], off["xbc"] + 1024))
    cols += list(range(off["z"], off["z"] + 512))
    for h in GQ_HEAD_ORDER:
        cols += list(range(off["gq"] + 64 * h, off["gq"] + 64 * h + 64))
    cols += list(range(off["gk"], off["gk"] + 128))
    cols += list(range(off["gv"], off["gv"] + 128))
    for h in GQ_HEAD_ORDER:
        cols += list(range(off["gg"] + 64 * h, off["gg"] + 64 * h + 64))
    for name in ("dq", "dk", "dv", "dg"):
        cols += list(range(off[name], off[name] + 256))
    for g in range(SSD_GROUPS):
        blk = [pad] * 128
        for i in range(4):
            blk[i] = off["dt"] + 4 * g + i
            blk[8 + i] = off["dt"] + SSD_HEADS + 4 * g + i
        cols += blk
    assert len(cols) == NP
    return np.asarray(cols, np.int32)


def _out_row_perm():
    rows = list(range(SSD_WIDTH))
    for h in GQ_HEAD_ORDER:
        rows += list(range(SSD_WIDTH + 64 * h, SSD_WIDTH + 64 * h + 64))
    rows += list(range(SSD_WIDTH + 256, SSD_WIDTH + 512))
    return np.asarray(rows, np.int32)


def _take_runs(arr, idx, axis, pad_index):
    pieces, start = [], 0
    idx = [int(i) for i in idx]
    while start < len(idx):
        end = start + 1
        if idx[start] == pad_index:
            while end < len(idx) and idx[end] == pad_index:
                end += 1
            shape = list(arr.shape)
            shape[axis] = end - start
            pieces.append(jnp.zeros(shape, arr.dtype))
        else:
            while end < len(idx) and idx[end] == idx[end - 1] + 1 and idx[end] != pad_index:
                end += 1
            pieces.append(lax.slice_in_dim(arr, idx[start], idx[end - 1] + 1, axis=axis))
        start = end
    return jnp.concatenate(pieces, axis=axis)


def _dot(a, b):
    return jnp.dot(a, b, preferred_element_type=F32)


def _split3(x):
    hi = x.astype(BF16)
    r1 = x - hi.astype(F32)
    mid = r1.astype(BF16)
    lo = (r1 - mid.astype(F32)).astype(BF16)
    return hi, mid, lo


def _dot_exact_lhs(x, m_bf16):
    hi, mid, lo = _split3(x)
    return _dot(hi, m_bf16) + _dot(mid, m_bf16) + _dot(lo, m_bf16)


def _silu(x):
    return x * jax.nn.sigmoid(x)


def _seg_ones(width, seg):
    r = lax.broadcasted_iota(jnp.int32, (width, width), 0)
    c = lax.broadcasted_iota(jnp.int32, (width, width), 1)
    same = (r & ~(seg - 1)) == (c & ~(seg - 1))
    return jnp.where(same, 1.0, 0.0).astype(BF16)


def _seg_rms(x, seg, seg_mat):
    ss = _dot_exact_lhs(x * x, seg_mat)
    return x * lax.rsqrt(ss * (1.0 / seg) + EPS)


def _rope(x, cos, sin_signed, half):
    w = x.shape[-1]
    lane = lax.broadcasted_iota(jnp.int32, x.shape, 1)
    first = (lane & (2 * half - 1)) < half
    swapped = jnp.where(first, pltpu.roll(x, w - half, 1), pltpu.roll(x, half, 1))
    return x * cos + swapped * sin_signed


def _mod_kernel(cs_ref, w_ref, b_ref, o_ref):
    s = _silu(cs_ref[...]).astype(BF16)
    o_ref[...] = _dot(s, w_ref[...].astype(BF16)) + b_ref[...]


def _mod_call(cs, w_mod, b_mod):
    depth = w_mod.shape[0]
    nrow = cs.shape[0]
    tn = 1024
    return pl.pallas_call(
        _mod_kernel,
        grid=(depth, 3 * D_MODEL // tn),
        in_specs=[
            pl.BlockSpec((nrow, D_MODEL), lambda l, j: (0, 0)),
            pl.BlockSpec((None, D_MODEL, tn), lambda l, j: (l, 0, j)),
            pl.BlockSpec((None, 1, tn), lambda l, j: (l, 0, j)),
        ],
        out_specs=pl.BlockSpec((None, nrow, tn), lambda l, j: (l, 0, j)),
        out_shape=jax.ShapeDtypeStruct((depth, nrow, 3 * D_MODEL), F32),
        compiler_params=pltpu.CompilerParams(
            dimension_semantics=("arbitrary", "arbitrary")),
        name="mod_proj",
    )(cs, w_mod, b_mod.reshape(depth, 1, 3 * D_MODEL))


def _inproj_kernel(h_ref, mod_ref, g_ref, w_ref, o_ref):
    x = h_ref[...]
    ms = jnp.mean(x * x, axis=-1, keepdims=True)
    y = x * lax.rsqrt(ms + EPS) * g_ref[...]
    sh = mod_ref[:, 0:D_MODEL]
    sc = mod_ref[:, D_MODEL:2 * D_MODEL]
    u = (y * (1.0 + sc) + sh).astype(BF16)
    tn = 512
    for j in range(NP // tn):
        o_ref[:, j * tn:(j + 1) * tn] = _dot(u, w_ref[:, j * tn:(j + 1) * tn])


def _inproj_call(h, mod3, g_pre, w_bf16, tiles_per_row, fixed_row):
    n_tok = h.shape[0]
    tm = 512
    if fixed_row is None:
        mod_idx = lambda i: (i // tiles_per_row, 0, 0)
    else:
        mod_idx = lambda i: (fixed_row, 0, 0)
    return pl.pallas_call(
        _inproj_kernel,
        grid=(n_tok // tm,),
        in_specs=[
            pl.BlockSpec((tm, D_MODEL), lambda i: (i, 0)),
            pl.BlockSpec((None, 1, 3 * D_MODEL), mod_idx),
            pl.BlockSpec((1, D_MODEL), lambda i: (0, 0)),
            pl.BlockSpec((D_MODEL, NP), lambda i: (0, 0)),
        ],
        out_specs=pl.BlockSpec((tm, NP), lambda i: (i, 0)),
        out_shape=jax.ShapeDtypeStruct((n_tok, NP), F32),
        compiler_params=pltpu.CompilerParams(
            dimension_semantics=("arbitrary",), vmem_limit_bytes=VMEM_LIMIT),
        name="in_proj",
    )(h, mod3, g_pre, w_bf16)


def _outproj_kernel(ys_ref, yg_ref, yd_ref, h_ref, mod_ref, g_ref, w_ref, o_ref):
    tm = h_ref.shape[0]
    sub = min(OUTPROJ_SUB, tm)
    gt = mod_ref[:, 2 * D_MODEL:3 * D_MODEL]
    gain = g_ref[...]

    def project(r0):
        return (_dot(ys_ref[r0:r0 + sub, :], w_ref[0:512, :])
                + _dot(yg_ref[r0:r0 + sub, :], w_ref[512:768, :])
                + _dot(yd_ref[r0:r0 + sub, :], w_ref[768:1024, :]))

    o_next = project(0)
    for r0 in range(0, tm, sub):
        o = o_next
        if r0 + sub < tm:
            o_next = project(r0 + sub)
        ms = jnp.mean(o * o, axis=-1, keepdims=True)
        n = o * lax.rsqrt(ms + EPS) * gain
        o_ref[r0:r0 + sub, :] = h_ref[r0:r0 + sub, :] + gt * n


def _outproj_call(ys, yg, yd, h, mod3, g_post, w_bf16, rows_per_mod, fixed_row):
    n_tok = h.shape[0]
    tm = OUTPROJ_TM
    assert n_tok % tm == 0
    if fixed_row is None:
        assert rows_per_mod % tm == 0
        mod_idx = lambda i: (i // (rows_per_mod // tm), 0, 0)
    else:
        mod_idx = lambda i: (fixed_row, 0, 0)
    return pl.pallas_call(
        _outproj_kernel,
        grid=(n_tok // tm,),
        in_specs=[
            pl.BlockSpec((tm, 512), lambda i: (i, 0)),
            pl.BlockSpec((tm, 256), lambda i: (i, 0)),
            pl.BlockSpec((tm, 256), lambda i: (i, 0)),
            pl.BlockSpec((tm, D_MODEL), lambda i: (i, 0)),
            pl.BlockSpec((None, 1, 3 * D_MODEL), mod_idx),
            pl.BlockSpec((1, D_MODEL), lambda i: (0, 0)),
            pl.BlockSpec((D_MODEL, D_MODEL), lambda i: (0, 0)),
        ],
        out_specs=pl.BlockSpec((tm, D_MODEL), lambda i: (i, 0)),
        out_shape=jax.ShapeDtypeStruct((n_tok, D_MODEL), F32),
        compiler_params=pltpu.CompilerParams(
            dimension_semantics=("arbitrary",), vmem_limit_bytes=VMEM_LIMIT),
        name="out_proj",
    )(ys, yg, yd, h, mod3, g_post, w_bf16)


def _attend_many(lhs_list, kt_ref, vext_refs):
    def scores(i):
        s = _dot(lhs_list[i], kt_ref[...])
        return s, jnp.max(s, axis=-1, keepdims=True)

    outs = []
    n = len(lhs_list)
    ahead = [scores(i) for i in range(min(ATTN_AHEAD, n))]
    for i, vext_ref in enumerate(vext_refs):
        s, m = ahead.pop(0)
        if i + ATTN_AHEAD < n:
            ahead.append(scores(i + ATTN_AHEAD))
        p = jnp.exp2(s - m).astype(BF16)
        oe = _dot(p, vext_ref[...])
        outs.append(oe[:, 0:128] / oe[:, 128:256])
    return outs


def _attend_diff_pairs(lhs_list, kt_ref, v_refs, lam):
    n_heads = len(v_refs)

    def scores(h):
        s_a = _dot(lhs_list[2 * h], kt_ref[...])
        m_a = jnp.max(s_a, axis=-1, keepdims=True)
        s_b = _dot(lhs_list[2 * h + 1], kt_ref[...])
        m_b = jnp.max(s_b, axis=-1, keepdims=True)
        return s_a, m_a, s_b, m_b

    outs = []
    ahead = [scores(h) for h in range(min(DIFF_AHEAD, n_heads))]
    for h in range(n_heads):
        s_a, m_a, s_b, m_b = ahead.pop(0)
        if h + DIFF_AHEAD < n_heads:
            ahead.append(scores(h + DIFF_AHEAD))
        e_a = jnp.exp2(s_a - m_a)
        e_b = jnp.exp2(s_b - m_b)
        l_a = jnp.sum(e_a, axis=-1, keepdims=True)
        l_b = jnp.sum(e_b, axis=-1, keepdims=True)
        pc = (e_a - (lam * l_a / l_b) * e_b).astype(BF16)
        outs.append(_dot(pc, v_refs[h][...]) / l_a)
    return outs


def _attend_diff_pairs_bounded(lhs_list, bounds, kt_ref, v_refs, lam):
    n_heads = len(v_refs)

    def exps(h):
        e_a = jnp.exp2(_dot(lhs_list[2 * h], kt_ref[...]) - bounds[2 * h])
        e_b = jnp.exp2(_dot(lhs_list[2 * h + 1], kt_ref[...]) - bounds[2 * h + 1])
        return e_a, e_b

    outs, l_min = [], None
    ahead = [exps(h) for h in range(min(DIFF_AHEAD, n_heads))]
    for h in range(n_heads):
        e_a, e_b = ahead.pop(0)
        if h + DIFF_AHEAD < n_heads:
            ahead.append(exps(h + DIFF_AHEAD))
        l_a = jnp.sum(e_a, axis=-1, keepdims=True)
        l_b = jnp.sum(e_b, axis=-1, keepdims=True)
        pc = (e_a - (lam * l_a / l_b) * e_b).astype(BF16)
        outs.append(_dot(pc, v_refs[h][...]) / l_a)
        l_ab = jnp.minimum(l_a, l_b)
        l_min = l_ab if l_min is None else jnp.minimum(l_min, l_ab)
    return outs, jnp.min(l_min, axis=0, keepdims=True)


def _gqa_kernel(*refs, n_ctx, n_lat, rope_q):
    it = iter(refs)
    q_ref, gg_ref, kvc_ref = next(it), next(it), next(it)
    kvl_ref = next(it) if n_lat else None
    if rope_q:
        cosq_ref, sinq_ref = next(it), next(it)
    if n_lat:
        cosk_ref, sink_ref = next(it), next(it)
    qg_ref, kg_ref = next(it), next(it)
    y_ref = next(it)
    kt_s, vext_s = next(it), next(it)

    seg128 = _seg_ones(128, 64)

    @pl.when(pl.program_id(1) == 0)
    def _prep_kv():
        kc = _seg_rms(kvc_ref[:, 0:128], 64, seg128) * kg_ref[...]
        kt_s[:, 0:n_ctx] = kc.T.astype(BF16)
        vext_s[0:n_ctx, 0:128] = kvc_ref[:, 128:256].astype(BF16)
        if n_lat:
            kl = _seg_rms(kvl_ref[:, 0:128], 64, seg128) * kg_ref[...]
            kl = _rope(kl, cosk_ref[...], sink_ref[...], 32)
            kt_s[:, n_ctx:n_ctx + n_lat] = kl.T.astype(BF16)
            vext_s[n_ctx:n_ctx + n_lat, 0:128] = kvl_ref[:, 128:256].astype(BF16)
        vext_s[:, 128:256] = jnp.ones((n_ctx + n_lat, 128), BF16)

    seg256 = _seg_ones(256, 64)
    q = _seg_rms(q_ref[...], 64, seg256) * qg_ref[...]
    if rope_q:
        q = _rope(q, cosq_ref[...], sinq_ref[...], 32)
    q = q * (GQA_HEAD_DIM ** -0.5 * LOG2E)
    tq = q.shape[0]
    sub = min(ATTN_SUB, tq)
    lane = lax.broadcasted_iota(jnp.int32, (sub, 128), 1)
    lhs_list = []
    for r0 in range(0, tq, sub):
        for half in range(2):
            qh = q[r0:r0 + sub, 128 * half:128 * half + 128]
            for kv in range(2):
                in_kv = (lane >= 64 * kv) & (lane < 64 * kv + 64)
                lhs_list.append(jnp.where(in_kv, qh, 0.0).astype(BF16))
    outs = _attend_many(lhs_list, kt_s, [vext_s] * len(lhs_list))
    for j, r0 in enumerate(range(0, tq, sub)):
        for half in range(2):
            o = jnp.where(lane < 64, outs[4 * j + 2 * half], outs[4 * j + 2 * half + 1])
            gate = _silu(gg_ref[r0:r0 + sub, 128 * half:128 * half + 128])
            y_ref[r0:r0 + sub, 128 * half:128 * half + 128] = (o * gate).astype(BF16)


def _gqa_call(p_q, p_c, p_l, tabs, qg, kg, batch, n_ctx, n_lat, rope_q):
    t_total = p_q.shape[0] // batch
    tq = min(ATTN_TQ, t_total)
    nq = t_total // tq
    in_specs = [
        pl.BlockSpec((tq, 256), lambda b, i: (b * nq + i, 6)),
        pl.BlockSpec((tq, 256), lambda b, i: (b * nq + i, 8)),
        pl.BlockSpec((n_ctx, 256), lambda b, i: (b, 7)),
    ]
    args = [p_q, p_q, p_c]
    if n_lat:
        in_specs.append(pl.BlockSpec((n_lat, 256), lambda b, i: (b, 7)))
        args.append(p_l)
    if rope_q:
        in_specs += [pl.BlockSpec((tq, 256), lambda b, i: (i, 0))] * 2
        args += [tabs["cos_g"], tabs["sin_g"]]
    if n_lat:
        in_specs += [pl.BlockSpec((n_lat, 128), lambda b, i: (0, 0))] * 2
        args += [tabs["cos_g"], tabs["sin_g"]]
    in_specs += [pl.BlockSpec((1, 256), lambda b, i: (0, 0)),
                 pl.BlockSpec((1, 128), lambda b, i: (0, 0))]
    args += [qg, kg]
    s_keys = n_ctx + n_lat
    return pl.pallas_call(
        functools.partial(_gqa_kernel, n_ctx=n_ctx, n_lat=n_lat, rope_q=rope_q),
        grid=(batch, nq),
        in_specs=in_specs,
        out_specs=pl.BlockSpec((tq, 256), lambda b, i: (b * nq + i, 0)),
        out_shape=jax.ShapeDtypeStruct((p_q.shape[0], 256), BF16),
        scratch_shapes=[pltpu.VMEM((128, s_keys), BF16),
                        pltpu.VMEM((s_keys, 256), BF16)],
        compiler_params=pltpu.CompilerParams(
            dimension_semantics=("arbitrary", "arbitrary"),
            vmem_limit_bytes=VMEM_LIMIT),
        name="gqa_attn",
    )(*args)


def _diff_kernel(*refs, n_ctx, n_lat, rope_q, lam_init, bounded):
    it = iter(refs)
    q_ref, dg_ref, kc_ref, vc_ref = next(it), next(it), next(it), next(it)
    if n_lat:
        kl_ref, vl_ref = next(it), next(it)
    if rope_q:
        cosq_ref, sinq_ref = next(it), next(it)
    if n_lat:
        cosk_ref, sink_ref = next(it), next(it)
    lam_ref, ng_ref = next(it), next(it)
    y_ref = next(it)
    lmin_ref = next(it) if bounded else None
    kt_s, vlo_s, vhi_s = next(it), next(it), next(it)
    kmax_s = next(it) if bounded else None
    s_keys = n_ctx + n_lat
    seg32 = _seg_ones(256, DIFF_QK_DIM)

    def map_norms(x):
        return jnp.sqrt(_dot((x * x).astype(BF16), seg32))

    @pl.when(pl.program_id(1) == 0)
    def _prep_kv():
        kc = kc_ref[...]
        kt_s[:, 0:n_ctx] = kc.T.astype(BF16)
        vlo_s[0:n_ctx, 0:128] = vc_ref[:, 0:128].astype(BF16)
        vhi_s[0:n_ctx, 0:128] = vc_ref[:, 128:256].astype(BF16)
        if bounded:
            kmax = jnp.max(map_norms(kc), axis=0, keepdims=True)
        if n_lat:
            kl = _rope(kl_ref[...], cosk_ref[...], sink_ref[...], 16)
            kt_s[:, n_ctx:s_keys] = kl.T.astype(BF16)
            vlo_s[n_ctx:s_keys, 0:128] = vl_ref[:, 0:128].astype(BF16)
            vhi_s[n_ctx:s_keys, 0:128] = vl_ref[:, 128:256].astype(BF16)
            if bounded:
                kmax = jnp.maximum(kmax, jnp.max(map_norms(kl), axis=0, keepdims=True))
        if bounded:
            kmax_s[...] = jnp.broadcast_to(kmax, kmax_s.shape)

    lp = lam_ref[...]
    lam = (jnp.exp(jnp.sum(lp[0:1, :] * lp[1:2, :], axis=-1, keepdims=True))
           - jnp.exp(jnp.sum(lp[2:3, :] * lp[3:4, :], axis=-1, keepdims=True))
           + lam_init)

    q = q_ref[...]
    if rope_q:
        q = _rope(q, cosq_ref[...], sinq_ref[...], 16)
    q = q * (DIFF_QK_DIM ** -0.5 * LOG2E)
    tq = q.shape[0]
    sub = min(DIFF_SUB, tq)
    lane256 = lax.broadcasted_iota(jnp.int32, (sub, 256), 1)
    lane128 = lax.broadcasted_iota(jnp.int32, (sub, 128), 1)
    seg128 = _seg_ones(128, 64)
    if bounded:
        bound_all = map_norms(q) * kmax_s[0:1, :] * SCORE_BOUND_MARGIN
    lhs_list, v_list, bounds = [], [], []
    for r0 in range(0, tq, sub):
        for mp in range(2 * DIFF_HEADS):
            in_map = (lane256 >= 32 * mp) & (lane256 < 32 * mp + 32)
            lhs_list.append(jnp.where(in_map, q[r0:r0 + sub, :], 0.0).astype(BF16))
            if bounded:
                bounds.append(bound_all[r0:r0 + sub, 32 * mp:32 * mp + 1])
        v_list += [vlo_s, vlo_s, vhi_s, vhi_s]

    if bounded:
        heads, l_min = _attend_diff_pairs_bounded(lhs_list, bounds, kt_s, v_list, lam)
        lmin_ref[...] = jnp.broadcast_to(l_min, lmin_ref.shape)
    else:
        heads = _attend_diff_pairs(lhs_list, kt_s, v_list, lam)
    for j, r0 in enumerate(range(0, tq, sub)):
        for half in range(2):
            o = jnp.where(lane128 < 64, heads[4 * j + 2 * half], heads[4 * j + 2 * half + 1])
            n = _seg_rms(o, 64, seg128) * ng_ref[:, 128 * half:128 * half + 128]
            n = n * (1.0 - lam_init)
            gate = _silu(dg_ref[r0:r0 + sub, 128 * half:128 * half + 128])
            y_ref[r0:r0 + sub, 128 * half:128 * half + 128] = (n * gate).astype(BF16)


def _diff_call(p_q, p_c, p_l, tabs, lam_params, ng, batch, n_ctx, n_lat, rope_q,
               lam_init, bounded):
    t_total = p_q.shape[0] // batch
    tq = min(ATTN_TQ, t_total)
    nq = t_total // tq
    in_specs = [
        pl.BlockSpec((tq, 256), lambda b, i: (b * nq + i, 9)),
        pl.BlockSpec((tq, 256), lambda b, i: (b * nq + i, 12)),
        pl.BlockSpec((n_ctx, 256), lambda b, i: (b, 10)),
        pl.BlockSpec((n_ctx, 256), lambda b, i: (b, 11)),
    ]
    args = [p_q, p_q, p_c, p_c]
    if n_lat:
        in_specs += [pl.BlockSpec((n_lat, 256), lambda b, i: (b, 10)),
                     pl.BlockSpec((n_lat, 256), lambda b, i: (b, 11))]
        args += [p_l, p_l]
    if rope_q:
        in_specs += [pl.BlockSpec((tq, 256), lambda b, i: (i, 0))] * 2
        args += [tabs["cos_d"], tabs["sin_d"]]
    if n_lat:
        in_specs += [pl.BlockSpec((n_lat, 256), lambda b, i: (0, 0))] * 2
        args += [tabs["cos_d"], tabs["sin_d"]]
    in_specs += [pl.BlockSpec((4, DIFF_QK_DIM), lambda b, i: (0, 0)),
                 pl.BlockSpec((1, 256), lambda b, i: (0, 0))]
    args += [lam_params, ng]
    s_keys = n_ctx + n_lat
    out_specs = [pl.BlockSpec((tq, 256), lambda b, i: (b * nq + i, 0))]
    out_shape = [jax.ShapeDtypeStruct((p_q.shape[0], 256), BF16)]
    scratch = [pltpu.VMEM((256, s_keys), BF16),
               pltpu.VMEM((s_keys, 128), BF16),
               pltpu.VMEM((s_keys, 128), BF16)]
    if bounded:
        out_specs.append(pl.BlockSpec((None, 8, 128), lambda b, i: (b * nq + i, 0, 0)))
        out_shape.append(jax.ShapeDtypeStruct((batch * nq, 8, 128), F32))
        scratch.append(pltpu.VMEM((8, 256), F32))
    outs = pl.pallas_call(
        functools.partial(_diff_kernel, n_ctx=n_ctx, n_lat=n_lat, rope_q=rope_q,
                          lam_init=lam_init, bounded=bounded),
        grid=(batch, nq),
        in_specs=in_specs,
        out_specs=out_specs,
        out_shape=out_shape,
        scratch_shapes=scratch,
        compiler_params=pltpu.CompilerParams(
            dimension_semantics=("arbitrary", "arbitrary"),
            vmem_limit_bytes=VMEM_LIMIT),
        name="diff_attn_bounded" if bounded else "diff_attn",
    )(*args)
    return (outs[0], outs[1]) if bounded else outs[0]


def _ssd_kernel(xc_ref, xl_ref, bc_ref, bl_ref, cc_ref, cl_ref, zc_ref, zl_ref,
                dtc_ref, dtl_ref, cwx_ref, cwb_ref, cwc_ref, cbx_ref, cbb_ref,
                cbc_ref, par_ref, dsk_ref, ng_ref, *rest, n_ctx, n_lat, ctx_out):
    if ctx_out:
        oc_ref, ol_ref = rest[0], rest[1]
        rest = rest[2:]
    else:
        oc_ref, ol_ref = None, rest[0]
        rest = rest[1:]
    xp_s, dtr_s, yacc_s, sloc_s, e_s, cbf_s, sin_s, dec_s, vun_s = rest

    g = pl.program_id(1)
    ncc = n_ctx // CHUNK
    ncl = n_lat // CHUNK
    nch = ncc + ncl
    GROUP = next(n for n in (6, 3, 2, 1) if nch % n == 0)
    t_all = n_ctx + n_lat
    lat0 = n_ctx + 2 * HALO

    zeros_h = jnp.zeros((HALO, 512), F32)
    xp_s[0:HALO, :] = zeros_h
    xp_s[HALO:HALO + n_ctx, 0:256] = xc_ref[...]
    xp_s[HALO:HALO + n_ctx, 256:384] = bc_ref[...]
    xp_s[HALO:HALO + n_ctx, 384:512] = cc_ref[...]
    xp_s[HALO + n_ctx:lat0, :] = zeros_h
    xp_s[lat0:lat0 + n_lat, 0:256] = xl_ref[...]
    xp_s[lat0:lat0 + n_lat, 256:384] = bl_ref[...]
    xp_s[lat0:lat0 + n_lat, 384:512] = cl_ref[...]
    xp_s[lat0 + n_lat:lat0 + n_lat + HALO, :] = zeros_h

    dtr_s[0:n_ctx, :] = dtc_ref[...]
    dtr_s[n_ctx:t_all, :] = dtl_ref[...]
    a_col = -jnp.exp(par_ref[0:16, :])
    bias_col = par_ref[16:32, :]

    def _softplus(v):
        return jnp.maximum(v, 0.0) + jnp.log1p(jnp.exp(-jnp.abs(v)))

    r128 = lax.broadcasted_iota(jnp.int32, (CHUNK, CHUNK), 0)
    c128 = lax.broadcasted_iota(jnp.int32, (CHUNK, CHUNK), 1)
    lower = c128 <= r128
    upper = c128 >= r128
    tril = jnp.where(lower, 1.0, 0.0).astype(BF16)
    triu = jnp.where(upper, 1.0, 0.0).astype(BF16)
    fwd_row = lax.broadcasted_iota(jnp.int32, (16, CHUNK), 0) < 8
    fwd_row1 = lax.broadcasted_iota(jnp.int32, (16, 1), 0) < 8
    er = lax.broadcasted_iota(jnp.int32, (CHUNK, 1024), 0)
    ec = lax.broadcasted_iota(jnp.int32, (CHUNK, 1024), 1)
    src_lane = 16 + 16 * (ec >> 9) + 8 * ((ec >> 8) & 1) + ((ec >> 6) & 3)
    expand = jnp.where(er == src_lane, 1.0, 0.0).astype(BF16)
    er0 = lax.broadcasted_iota(jnp.int32, (CHUNK, 512), 0)
    ec0 = lax.broadcasted_iota(jnp.int32, (CHUNK, 512), 1)
    expand_tot = jnp.where(er0 == 8 * (ec0 >> 8) + ((ec0 >> 6) & 3), 1.0, 0.0).astype(BF16)
    lane256 = lax.broadcasted_iota(jnp.int32, (CHUNK, 256), 1)
    lane128_1 = lax.broadcasted_iota(jnp.int32, (1, CHUNK), 1)

    cw = jnp.concatenate([cwx_ref[...], cwb_ref[...], cwc_ref[...]], axis=1)
    cb = jnp.concatenate([cbx_ref[...], cbb_ref[...], cbc_ref[...]], axis=1)
    dsk = dsk_ref[...]

    def phase_a(grp, carry):
        cs = [grp * GROUP + j for j in range(GROUP)]
        r0s = [pl.multiple_of(c * CHUNK, CHUNK) for c in cs]

        def conv(c):
            wstart = pl.multiple_of(c * CHUNK + jnp.where(c >= ncc, HALO, 0), 8)
            win = xp_s[pl.ds(wstart, CHUNK + 2 * HALO), :]
            acc = jnp.broadcast_to(cb, (CHUNK, 512))
            for k in range(CONV_K):
                d = k - CONV_K // 2
                if d == 0:
                    tap = win[HALO:HALO + CHUNK, :]
                else:
                    tap = pltpu.roll(win, (-d) % (CHUNK + 2 * HALO), 0)[HALO:HALO + CHUNK, :]
                acc = acc + cw[k:k + 1, :] * tap
            return _silu(acc)

        dtts = [_softplus(dtr_s[pl.ds(r0, CHUNK), :].T[0:16, :] + bias_col) for r0 in r0s]
        a_ts = [dtt * a_col for dtt in dtts]
        acol_ts = [jnp.where(fwd_row, _dot_exact_lhs(a_t, triu), _dot_exact_lhs(a_t, tril))
                   for a_t in a_ts]
        us = [conv(c) for c in cs]
        xus = [u[:, 0:256] for u in us]
        bts = [u[:, 256:384].T.astype(BF16) for u in us]
        cbfs = [u[:, 384:512].astype(BF16) for u in us]
        xbfs = [xu.astype(BF16) for xu in xus]
        gmats = [_dot(cbf, bt) for cbf, bt in zip(cbfs, bts)]
        tms = []
        for dtt, acol_t in zip(dtts, acol_ts):
            tot = jnp.where(fwd_row1, acol_t[:, CHUNK - 1:CHUNK], acol_t[:, 0:1])
            w_t = dtt * jnp.exp(tot - acol_t)
            e_t = jnp.exp(acol_t)
            stacked = jnp.concatenate(
                [acol_t, w_t, e_t, jnp.zeros((CHUNK - 48, CHUNK), F32)], axis=0)
            tms.append(stacked.T)
        wes = [_dot(tm.astype(BF16), expand) for tm in tms]
        decs = []
        for tm in tms:
            tot_row = jnp.where(lane128_1 < 8, tm[CHUNK - 1:CHUNK, :], tm[0:1, :])
            tot512 = _dot_exact_lhs(jnp.broadcast_to(tot_row, (8, CHUNK)), expand_tot)
            decs.append(jnp.exp(tot512))
        ydiags = [jnp.zeros((CHUNK, 256), F32) for _ in cs]
        for i in range(4):
            in_head = (lane256 >= 64 * i) & (lane256 < 64 * i + 64)
            for j in range(GROUP):
                tm, acol_t, dtt = tms[j], acol_ts[j], dtts[j]
                arg = jnp.where(lower, tm[:, i:i + 1] - acol_t[i:i + 1, :],
                                tm[:, 8 + i:9 + i] - acol_t[8 + i:9 + i, :])
                scale = (jnp.where(lower, dtt[i:i + 1, :], 0.0)
                         + jnp.where(upper, dtt[8 + i:9 + i, :], 0.0))
                wmat = (gmats[j] * jnp.exp(arg) * scale).astype(BF16)
                ydiags[j] = jnp.where(in_head, _dot(wmat, xbfs[j]), ydiags[j])
        for j, c in enumerate(cs):
            w512 = wes[j][:, 0:512]
            e512 = wes[j][:, 512:1024]
            xdw = (jnp.concatenate([xus[j], xus[j]], axis=1) * w512).astype(BF16)
            sloc_s[c] = _dot(bts[j], xdw)
            dec_s[c] = decs[j]
            e_s[c] = e512
            cbf_s[c] = cbfs[j]
            yacc_s[pl.ds(r0s[j], CHUNK), :] = ydiags[j] + xus[j] * dsk
        return carry

    lax.fori_loop(0, nch // GROUP, phase_a, 0)

    fwd_order = list(range(nch))
    bwd_order = list(range(ncc - 1, -1, -1)) + list(range(nch - 1, ncc - 1, -1))
    for order, lo in ((fwd_order, 0), (bwd_order, 256)):
        state = jnp.zeros((SSD_STATE, 256), F32)
        for c in order:
            sin_s[c, :, lo:lo + 256] = state.astype(BF16)
            state = (state * dec_s[c, 0:1, lo:lo + 256]
                     + sloc_s[c, :, lo:lo + 256])

    def phase_c(cs, z_ref, zrows):
        yos = [_dot(cbf_s[c], sin_s[c]) for c in cs]
        vs = []
        for c, zrow, yo in zip(cs, zrows, yos):
            r0 = c * CHUNK if isinstance(c, int) else pl.multiple_of(c * CHUNK, CHUNK)
            yo = yo * e_s[c]
            y = yacc_s[pl.ds(r0, CHUNK), :] + yo[:, 0:256] + yo[:, 256:512]
            vs.append((r0, y * _silu(z_ref[pl.ds(zrow, CHUNK), :])))

        @pl.when(g == 0)
        def _():
            for r0, v in vs:
                vun_s[pl.ds(r0, CHUNK), 0:256] = v

        @pl.when(g == 1)
        def _():
            for r0, v in vs:
                vun_s[pl.ds(r0, CHUNK), 256:512] = v

    if ctx_out:
        phase_c(list(range(ncc)), zc_ref, [c * CHUNK for c in range(ncc)])

    group_c = next(n for n in (4, 2, 1) if ncl % n == 0)

    def phase_c_lat(k, carry):
        ks = [k * group_c + j for j in range(group_c)]
        phase_c([kk + ncc for kk in ks], zl_ref,
                [pl.multiple_of(kk * CHUNK, CHUNK) for kk in ks])
        return carry

    lax.fori_loop(0, ncl // group_c, phase_c_lat, 0)

    @pl.when(g == SSD_GROUPS - 1)
    def _finalize():
        ng = ng_ref[...]

        def norm_rows(r0, nrows):
            v = vun_s[pl.ds(r0, nrows), :]
            ms = jnp.mean(v * v, axis=-1, keepdims=True)
            return (v * lax.rsqrt(ms + EPS) * ng).astype(BF16)

        if ctx_out:
            oc_ref[...] = norm_rows(0, n_ctx)

        def fin(k, carry):
            r0 = pl.multiple_of(k * 256, 256)
            ol_ref[pl.ds(r0, 256), :] = norm_rows(n_ctx + r0, 256)
            return carry

        lax.fori_loop(0, n_lat // 256, fin, 0)


def _ssd_call(p_c, p_l, conv_w8, conv_b, par, dsk, ng, batch, n_ctx, n_lat, ctx_out):
    nch = (n_ctx + n_lat) // CHUNK
    t_all = n_ctx + n_lat
    in_specs = [
        pl.BlockSpec((n_ctx, 256), lambda b, g: (b, g)),
        pl.BlockSpec((n_lat, 256), lambda b, g: (b, g)),
        pl.BlockSpec((n_ctx, 128), lambda b, g: (b, 4 + g)),
        pl.BlockSpec((n_lat, 128), lambda b, g: (b, 4 + g)),
        pl.BlockSpec((n_ctx, 128), lambda b, g: (b, 6 + g)),
        pl.BlockSpec((n_lat, 128), lambda b, g: (b, 6 + g)),
        pl.BlockSpec((n_ctx, 256), lambda b, g: (b, 4 + g)),
        pl.BlockSpec((n_lat, 256), lambda b, g: (b, 4 + g)),
        pl.BlockSpec((n_ctx, 128), lambda b, g: (b, 26 + g)),
        pl.BlockSpec((n_lat, 128), lambda b, g: (b, 26 + g)),
        pl.BlockSpec((8, 256), lambda b, g: (0, g)),
        pl.BlockSpec((8, 128), lambda b, g: (0, 4 + g)),
        pl.BlockSpec((8, 128), lambda b, g: (0, 6 + g)),
        pl.BlockSpec((1, 256), lambda b, g: (0, g)),
        pl.BlockSpec((1, 128), lambda b, g: (0, 4 + g)),
        pl.BlockSpec((1, 128), lambda b, g: (0, 6 + g)),
        pl.BlockSpec((None, 32, 128), lambda b, g: (g, 0, 0)),
        pl.BlockSpec((1, 256), lambda b, g: (0, g)),
        pl.BlockSpec((1, 512), lambda b, g: (0, 0)),
    ]
    args = [p_c, p_l, p_c, p_l, p_c, p_l, p_c, p_l, p_c, p_l,
            conv_w8, conv_w8, conv_w8, conv_b, conv_b, conv_b, par, dsk, ng]
    out_specs = [pl.BlockSpec((n_lat, 512), lambda b, g: (b, 0))]
    out_shape = [jax.ShapeDtypeStruct((batch * n_lat, 512), BF16)]
    if ctx_out:
        out_specs = [pl.BlockSpec((n_ctx, 512), lambda b, g: (b, 0))] + out_specs
        out_shape = [jax.ShapeDtypeStruct((batch * n_ctx, 512), BF16)] + out_shape
    scratch = [
        pltpu.VMEM((t_all + 3 * HALO, 512), F32),
        pltpu.VMEM((t_all, 128), F32),
        pltpu.VMEM((t_all, 256), F32),
        pltpu.VMEM((nch, SSD_STATE, 512), F32),
        pltpu.VMEM((nch, CHUNK, 512), F32),
        pltpu.VMEM((nch, CHUNK, 128), BF16),
        pltpu.VMEM((nch, SSD_STATE, 512), BF16),
        pltpu.VMEM((nch, 8, 512), F32),
        pltpu.VMEM((t_all, 512), F32),
    ]
    outs = pl.pallas_call(
        functools.partial(_ssd_kernel, n_ctx=n_ctx, n_lat=n_lat, ctx_out=ctx_out),
        grid=(batch, SSD_GROUPS),
        in_specs=in_specs,
        out_specs=out_specs,
        out_shape=out_shape,
        scratch_shapes=scratch,
        compiler_params=pltpu.CompilerParams(
            dimension_semantics=("arbitrary", "arbitrary"),
            vmem_limit_bytes=VMEM_LIMIT),
        name="ssd_scan",
    )(*args)
    if ctx_out:
        return outs[1], outs[0]
    return outs[0], None


def _rope_tables(n_lat):
    rows = n_lat // GRID_W
    row_idx = np.repeat(np.arange(rows), GRID_W).astype(np.float32)
    col_idx = (np.arange(rows * GRID_W) % GRID_W).astype(np.float32)

    def tables(dim, reps):
        quarter = dim // 4
        inv = (ROPE_BASE ** (-np.arange(quarter, dtype=np.float32) / quarter)).astype(np.float32)
        ang = np.concatenate([row_idx[:, None] * inv, col_idx[:, None] * inv], axis=-1)
        cos, sin = np.cos(ang.astype(np.float64)), np.sin(ang.astype(np.float64))
        cos2 = np.concatenate([cos, cos], axis=-1).astype(np.float32)
        sin2 = np.concatenate([-sin, sin], axis=-1).astype(np.float32)
        return jnp.asarray(np.tile(cos2, (1, reps))), jnp.asarray(np.tile(sin2, (1, reps)))

    cos_g, sin_g = tables(GQA_HEAD_DIM, 4)
    cos_d, sin_d = tables(DIFF_QK_DIM, 8)
    return {"cos_g": cos_g, "sin_g": sin_g, "cos_d": cos_d, "sin_d": sin_d}


def kernel(x, c, ctx, c_ctx, w_mod, b_mod, g_pre, g_post, w_in, conv_w, conv_b,
           a_log_fwd, a_log_bwd, dt_bias_fwd, dt_bias_bwd, d_skip, ssd_norm_g,
           q_norm_g, k_norm_g, diff_lambda, diff_norm_g, w_out):
    batch, n_lat, _ = x.shape
    n_ctx = ctx.shape[1]
    depth = w_mod.shape[0]
    assert n_lat % 512 == 0 and n_ctx % 256 == 0 and (batch * n_ctx) % 512 == 0
    assert batch + 1 <= 16

    in_perm, out_perm = _in_col_perm(), _out_row_perm()
    w_in_p = [_take_runs(w_in[l], in_perm, 1, IN_COLS).astype(BF16) for l in range(depth)]
    w_out_p = [_take_runs(w_out[l], out_perm, 0, None).astype(BF16) for l in range(depth)]
    conv_w8 = jnp.pad(conv_w, ((0, 0), (0, 8 - CONV_K), (0, 0)))
    conv_b1 = conv_b[:, None, :]

    def group16(fwd, bwd):
        out = jnp.zeros((depth, SSD_GROUPS, 16), F32)
        for g in range(SSD_GROUPS):
            out = out.at[:, g, 0:4].set(fwd[:, 4 * g:4 * g + 4])
            out = out.at[:, g, 8:12].set(bwd[:, 4 * g:4 * g + 4])
        return out

    ssd_par = jnp.broadcast_to(
        jnp.concatenate([group16(a_log_fwd, a_log_bwd),
                         group16(dt_bias_fwd, dt_bias_bwd)], axis=-1)[..., None],
        (depth, SSD_GROUPS, 32, 128))
    dsk = jnp.repeat(d_skip, SSD_HEAD_DIM, axis=1)[:, None, :]
    qg = jnp.tile(q_norm_g, (1, 4))[:, None, :]
    kg = jnp.tile(k_norm_g, (1, 2))[:, None, :]
    dng = jnp.tile(diff_norm_g, (1, 4))[:, None, :]
    tabs = _rope_tables(n_lat)

    cs = jnp.concatenate(
        [c, c_ctx[None, :], jnp.zeros((16 - batch - 1, D_MODEL), F32)], axis=0)
    mod_all = _mod_call(cs, w_mod, b_mod)

    h = x.reshape(batch * n_lat, D_MODEL)
    hc = ctx.reshape(batch * n_ctx, D_MODEL)
    for l in range(depth):
        ctx_out = l < depth - 1
        lam_init = 0.8 - 0.6 * float(np.exp(-0.3 * l))
        mod3 = mod_all[l][:, None, :]
        p_l = _inproj_call(h, mod3, g_pre[l][None, :], w_in_p[l], n_lat // 512, None)
        p_c = _inproj_call(hc, mod3, g_pre[l][None, :], w_in_p[l], None, batch)

        ys_l, ys_c = _ssd_call(p_c, p_l, conv_w8[l], conv_b1[l], ssd_par[l], dsk[l],
                               ssd_norm_g[l][None, :], batch, n_ctx, n_lat, ctx_out)
        yg_l = _gqa_call(p_l, p_c, p_l, tabs, qg[l], kg[l], batch, n_ctx, n_lat, True)
        diff_args = (p_l, p_c, p_l, tabs, diff_lambda[l], dng[l], batch, n_ctx, n_lat, True,
                     lam_init)
        yd_fast, row_sum_min = _diff_call(*diff_args, True)
        yd_l = lax.cond(jnp.min(row_sum_min) >= MIN_ROW_SUM,
                        lambda: yd_fast, lambda: _diff_call(*diff_args, False))
        h = _outproj_call(ys_l, yg_l, yd_l, h, mod3, g_post[l][None, :], w_out_p[l],
                          n_lat, None)
        if ctx_out:
            yg_c = _gqa_call(p_c, p_c, None, tabs, qg[l], kg[l], batch, n_ctx, 0, False)
            yd_c = _diff_call(p_c, p_c, None, tabs, diff_lambda[l], dng[l], batch,
                              n_ctx, 0, False, lam_init, False)
            hc = _outproj_call(ys_c, yg_c, yd_c, hc, mod3, g_post[l][None, :],
                               w_out_p[l], None, batch)
    return h.reshape(batch, n_lat, D_MODEL)
```

```python
import functools

import numpy as np
import jax
import jax.numpy as jnp
from jax import lax
from jax.experimental import pallas as pl
from jax.experimental.pallas import tpu as pltpu

F32 = jnp.float32
BF16 = jnp.bfloat16

D_MODEL = 1024
GRID_W = 64
ROPE_BASE = 10000.0
EPS = 1e-6
LOG2E = 1.4426950408889634

SSD_WIDTH = 512
SSD_HEADS = 8
SSD_HEAD_DIM = 64
SSD_GROUPS = 2
SSD_STATE = 128
CHUNK = 128
CONV_K = 5
HALO = 8
GQA_HEADS = 4
GQA_HEAD_DIM = 64
DIFF_HEADS = 4
DIFF_QK_DIM = 32
DIFF_V_DIM = 64

_IN_SPLITS = (("xbc", 1024), ("z", 512), ("dt", 16), ("gq", 256), ("gk", 128),
              ("gv", 128), ("gg", 256), ("dq", 256), ("dk", 256), ("dv", 256),
              ("dg", 256))
IN_COLS = sum(s for _, s in _IN_SPLITS)
NP = 28 * 128
GQ_HEAD_ORDER = (0, 2, 1, 3)

VMEM_LIMIT = 56 * 1024 * 1024
ATTN_TQ = 2048
DIFF_TQ = 1024
ATTN_SUB = 512
DIFF_SUB = 512
ATTN_AHEAD = 2
DIFF_AHEAD = 1
SCORE_BOUND_MARGIN = 1.02
MIN_ROW_SUM = 2.0 ** -90
OUTPROJ_TM = 1024
OUTPROJ_SUB = 256


def _in_col_perm():
    off, o = {}, 0
    for name, size in _IN_SPLITS:
        off[name] = o
        o += size
    pad = IN_COLS
    cols = list(range(off["xbc"], off["xbc"] + 1024))
    cols += list(range(off["z"], off["z"] + 512))
    for h in GQ_HEAD_ORDER:
        cols += list(range(off["gq"] + 64 * h, off["gq"] + 64 * h + 64))
    cols += list(range(off["gk"], off["gk"] + 128))
    cols += list(range(off["gv"], off["gv"] + 128))
    for h in GQ_HEAD_ORDER:
        cols += list(range(off["gg"] + 64 * h, off["gg"] + 64 * h + 64))
    for name in ("dq", "dk", "dv", "dg"):
        cols += list(range(off[name], off[name] + 256))
    for g in range(SSD_GROUPS):
        blk = [pad] * 128
        for i in range(4):
            blk[i] = off["dt"] + 4 * g + i
            blk[8 + i] = off["dt"] + SSD_HEADS + 4 * g + i
        cols += blk
    assert len(cols) == NP
    return np.asarray(cols, np.int32)


def _out_row_perm():
    rows = list(range(SSD_WIDTH))
    for h in GQ_HEAD_ORDER:
        rows += list(range(SSD_WIDTH + 64 * h, SSD_WIDTH + 64 * h + 64))
    rows += list(range(SSD_WIDTH + 256, SSD_WIDTH + 512))
    return np.asarray(rows, np.int32)


def _take_runs(arr, idx, axis, pad_index):
    pieces, start = [], 0
    idx = [int(i) for i in idx]
    while start < len(idx):
        end = start + 1
        if idx[start] == pad_index:
            while end < len(idx) and idx[end] == pad_index:
                end += 1
            shape = list(arr.shape)
            shape[axis] = end - start
            pieces.append(jnp.zeros(shape, arr.dtype))
        else:
            while end < len(idx) and idx[end] == idx[end - 1] + 1 and idx[end] != pad_index:
                end += 1
            pieces.append(lax.slice_in_dim(arr, idx[start], idx[end - 1] + 1, axis=axis))
        start = end
    return jnp.concatenate(pieces, axis=axis)


def _dot(a, b):
    return jnp.dot(a, b, preferred_element_type=F32)


def _split3(x):
    hi = x.astype(BF16)
    r1 = x - hi.astype(F32)
    mid = r1.astype(BF16)
    lo = (r1 - mid.astype(F32)).astype(BF16)
    return hi, mid, lo


def _dot_exact_lhs(x, m_bf16):
    hi, mid, lo = _split3(x)
    return _dot(hi, m_bf16) + _dot(mid, m_bf16) + _dot(lo, m_bf16)


def _silu(x):
    return x * jax.nn.sigmoid(x)


def _seg_ones(width, seg):
    r = lax.broadcasted_iota(jnp.int32, (width, width), 0)
    c = lax.broadcasted_iota(jnp.int32, (width, width), 1)
    same = (r & ~(seg - 1)) == (c & ~(seg - 1))
    return jnp.where(same, 1.0, 0.0).astype(BF16)


def _seg_rms(x, seg, seg_mat):
    ss = _dot_exact_lhs(x * x, seg_mat)
    return x * lax.rsqrt(ss * (1.0 / seg) + EPS)


def _rope(x, cos, sin_signed, half):
    w = x.shape[-1]
    lane = lax.broadcasted_iota(jnp.int32, x.shape, 1)
    first = (lane & (2 * half - 1)) < half
    swapped = jnp.where(first, pltpu.roll(x, w - half, 1), pltpu.roll(x, half, 1))
    return x * cos + swapped * sin_signed


def _mod_kernel(cs_ref, w_ref, b_ref, o_ref):
    s = _silu(cs_ref[...]).astype(BF16)
    o_ref[...] = _dot(s, w_ref[...].astype(BF16)) + b_ref[...]


def _mod_call(cs, w_mod, b_mod):
    depth = w_mod.shape[0]
    nrow = cs.shape[0]
    tn = 1024
    return pl.pallas_call(
        _mod_kernel,
        grid=(depth, 3 * D_MODEL // tn),
        in_specs=[
            pl.BlockSpec((nrow, D_MODEL), lambda l, j: (0, 0)),
            pl.BlockSpec((None, D_MODEL, tn), lambda l, j: (l, 0, j)),
            pl.BlockSpec((None, 1, tn), lambda l, j: (l, 0, j)),
        ],
        out_specs=pl.BlockSpec((None, nrow, tn), lambda l, j: (l, 0, j)),
        out_shape=jax.ShapeDtypeStruct((depth, nrow, 3 * D_MODEL), F32),
        compiler_params=pltpu.CompilerParams(
            dimension_semantics=("arbitrary", "arbitrary")),
        name="mod_proj",
    )(cs, w_mod, b_mod.reshape(depth, 1, 3 * D_MODEL))


def _inproj_kernel(h_ref, mod_ref, g_ref, w_ref, o_ref):
    x = h_ref[...]
    ms = jnp.mean(x * x, axis=-1, keepdims=True)
    y = x * lax.rsqrt(ms + EPS) * g_ref[...]
    sh = mod_ref[:, 0:D_MODEL]
    sc = mod_ref[:, D_MODEL:2 * D_MODEL]
    u = (y * (1.0 + sc) + sh).astype(BF16)
    tn = 512
    for j in range(NP // tn):
        o_ref[:, j * tn:(j + 1) * tn] = _dot(u, w_ref[:, j * tn:(j + 1) * tn])


def _inproj_call(h, mod3, g_pre, w_bf16, tiles_per_row, fixed_row):
    n_tok = h.shape[0]
    tm = 512
    if fixed_row is None:
        mod_idx = lambda i: (i // tiles_per_row, 0, 0)
    else:
        mod_idx = lambda i: (fixed_row, 0, 0)
    return pl.pallas_call(
        _inproj_kernel,
        grid=(n_tok // tm,),
        in_specs=[
            pl.BlockSpec((tm, D_MODEL), lambda i: (i, 0)),
            pl.BlockSpec((None, 1, 3 * D_MODEL), mod_idx),
            pl.BlockSpec((1, D_MODEL), lambda i: (0, 0)),
            pl.BlockSpec((D_MODEL, NP), lambda i: (0, 0)),
        ],
        out_specs=pl.BlockSpec((tm, NP), lambda i: (i, 0)),
        out_shape=jax.ShapeDtypeStruct((n_tok, NP), F32),
        compiler_params=pltpu.CompilerParams(
            dimension_semantics=("arbitrary",), vmem_limit_bytes=VMEM_LIMIT),
        name="in_proj",
    )(h, mod3, g_pre, w_bf16)


def _outproj_kernel(ys_ref, yg_ref, yd_ref, h_ref, mod_ref, g_ref, w_ref, o_ref):
    tm = h_ref.shape[0]
    sub = min(OUTPROJ_SUB, tm)
    gt = mod_ref[:, 2 * D_MODEL:3 * D_MODEL]
    gain = g_ref[...]

    def project(r0):
        return (_dot(ys_ref[r0:r0 + sub, :], w_ref[0:512, :])
                + _dot(yg_ref[r0:r0 + sub, :], w_ref[512:768, :])
                + _dot(yd_ref[r0:r0 + sub, :], w_ref[768:1024, :]))

    o_next = project(0)
    for r0 in range(0, tm, sub):
        o = o_next
        if r0 + sub < tm:
            o_next = project(r0 + sub)
        ms = jnp.mean(o * o, axis=-1, keepdims=True)
        n = o * lax.rsqrt(ms + EPS) * gain
        o_ref[r0:r0 + sub, :] = h_ref[r0:r0 + sub, :] + gt * n


def _outproj_call(ys, yg, yd, h, mod3, g_post, w_bf16, rows_per_mod, fixed_row):
    n_tok = h.shape[0]
    tm = OUTPROJ_TM
    assert n_tok % tm == 0
    if fixed_row is None:
        assert rows_per_mod % tm == 0
        mod_idx = lambda i: (i // (rows_per_mod // tm), 0, 0)
    else:
        mod_idx = lambda i: (fixed_row, 0, 0)
    return pl.pallas_call(
        _outproj_kernel,
        grid=(n_tok // tm,),
        in_specs=[
            pl.BlockSpec((tm, 512), lambda i: (i, 0)),
            pl.BlockSpec((tm, 256), lambda i: (i, 0)),
            pl.BlockSpec((tm, 256), lambda i: (i, 0)),
            pl.BlockSpec((tm, D_MODEL), lambda i: (i, 0)),
            pl.BlockSpec((None, 1, 3 * D_MODEL), mod_idx),
            pl.BlockSpec((1, D_MODEL), lambda i: (0, 0)),
            pl.BlockSpec((D_MODEL, D_MODEL), lambda i: (0, 0)),
        ],
        out_specs=pl.BlockSpec((tm, D_MODEL), lambda i: (i, 0)),
        out_shape=jax.ShapeDtypeStruct((n_tok, D_MODEL), F32),
        compiler_params=pltpu.CompilerParams(
            dimension_semantics=("arbitrary",), vmem_limit_bytes=VMEM_LIMIT),
        name="out_proj",
    )(ys, yg, yd, h, mod3, g_post, w_bf16)


def _attend_many(lhs_list, kt_ref, vext_refs):
    def scores(i):
        s = _dot(lhs_list[i], kt_ref[...])
        return s, jnp.max(s, axis=-1, keepdims=True)

    outs = []
    n = len(lhs_list)
    ahead = [scores(i) for i in range(min(ATTN_AHEAD, n))]
    for i, vext_ref in enumerate(vext_refs):
        s, m = ahead.pop(0)
        if i + ATTN_AHEAD < n:
            ahead.append(scores(i + ATTN_AHEAD))
        p = jnp.exp2(s - m).astype(BF16)
        oe = _dot(p, vext_ref[...])
        outs.append(oe[:, 0:128] / oe[:, 128:256])
    return outs


def _attend_diff_pairs(lhs_list, kt_ref, v_refs, lam):
    n_heads = len(v_refs)

    def scores(h):
        s_a = _dot(lhs_list[2 * h], kt_ref[...])
        m_a = jnp.max(s_a, axis=-1, keepdims=True)
        s_b = _dot(lhs_list[2 * h + 1], kt_ref[...])
        m_b = jnp.max(s_b, axis=-1, keepdims=True)
        return s_a, m_a, s_b, m_b

    outs = []
    ahead = [scores(h) for h in range(min(DIFF_AHEAD, n_heads))]
    for h in range(n_heads):
        s_a, m_a, s_b, m_b = ahead.pop(0)
        if h + DIFF_AHEAD < n_heads:
            ahead.append(scores(h + DIFF_AHEAD))
        e_a = jnp.exp2(s_a - m_a)
        e_b = jnp.exp2(s_b - m_b)
        l_a = jnp.sum(e_a, axis=-1, keepdims=True)
        l_b = jnp.sum(e_b, axis=-1, keepdims=True)
        pc = (e_a - (lam * l_a / l_b) * e_b).astype(BF16)
        outs.append(_dot(pc, v_refs[h][...]) / l_a)
    return outs


def _attend_diff_pairs_bounded(lhs_list, bounds, kt_ref, v_refs, lam):
    n_heads = len(v_refs)

    def exps(h):
        e_a = jnp.exp2(_dot(lhs_list[2 * h], kt_ref[...]) - bounds[2 * h])
        e_b = jnp.exp2(_dot(lhs_list[2 * h + 1], kt_ref[...]) - bounds[2 * h + 1])
        return e_a, e_b

    outs, l_min = [], None
    ahead = [exps(h) for h in range(min(DIFF_AHEAD, n_heads))]
    for h in range(n_heads):
        e_a, e_b = ahead.pop(0)
        if h + DIFF_AHEAD < n_heads:
            ahead.append(exps(h + DIFF_AHEAD))
        l_a = jnp.sum(e_a, axis=-1, keepdims=True)
        l_b = jnp.sum(e_b, axis=-1, keepdims=True)
        pc = (e_a - (lam * l_a / l_b) * e_b).astype(BF16)
        outs.append(_dot(pc, v_refs[h][...]) / l_a)
        l_ab = jnp.minimum(l_a, l_b)
        l_min = l_ab if l_min is None else jnp.minimum(l_min, l_ab)
    return outs, jnp.min(l_min, axis=0, keepdims=True)


def _gqa_kernel(*refs, n_ctx, n_lat, rope_q):
    it = iter(refs)
    q_ref, gg_ref, kvc_ref = next(it), next(it), next(it)
    kvl_ref = next(it) if n_lat else None
    if rope_q:
        cosq_ref, sinq_ref = next(it), next(it)
    if n_lat:
        cosk_ref, sink_ref = next(it), next(it)
    qg_ref, kg_ref = next(it), next(it)
    y_ref = next(it)
    kt_s, vext_s = next(it), next(it)

    seg128 = _seg_ones(128, 64)

    @pl.when(pl.program_id(1) == 0)
    def _prep_kv():
        kc = _seg_rms(kvc_ref[:, 0:128], 64, seg128) * kg_ref[...]
        kt_s[:, 0:n_ctx] = kc.T.astype(BF16)
        vext_s[0:n_ctx, 0:128] = kvc_ref[:, 128:256].astype(BF16)
        if n_lat:
            kl = _seg_rms(kvl_ref[:, 0:128], 64, seg128) * kg_ref[...]
            kl = _rope(kl, cosk_ref[...], sink_ref[...], 32)
            kt_s[:, n_ctx:n_ctx + n_lat] = kl.T.astype(BF16)
            vext_s[n_ctx:n_ctx + n_lat, 0:128] = kvl_ref[:, 128:256].astype(BF16)
        vext_s[:, 128:256] = jnp.ones((n_ctx + n_lat, 128), BF16)

    seg256 = _seg_ones(256, 64)
    q = _seg_rms(q_ref[...], 64, seg256) * qg_ref[...]
    if rope_q:
        q = _rope(q, cosq_ref[...], sinq_ref[...], 32)
    q = q * (GQA_HEAD_DIM ** -0.5 * LOG2E)
    tq = q.shape[0]
    sub = min(ATTN_SUB, tq)
    lane = lax.broadcasted_iota(jnp.int32, (sub, 128), 1)
    lhs_list = []
    for r0 in range(0, tq, sub):
        for half in range(2):
            qh = q[r0:r0 + sub, 128 * half:128 * half + 128]
            for kv in range(2):
                in_kv = (lane >= 64 * kv) & (lane < 64 * kv + 64)
                lhs_list.append(jnp.where(in_kv, qh, 0.0).astype(BF16))
    outs = _attend_many(lhs_list, kt_s, [vext_s] * len(lhs_list))
    for j, r0 in enumerate(range(0, tq, sub)):
        for half in range(2):
            o = jnp.where(lane < 64, outs[4 * j + 2 * half], outs[4 * j + 2 * half + 1])
            gate = _silu(gg_ref[r0:r0 + sub, 128 * half:128 * half + 128])
            y_ref[r0:r0 + sub, 128 * half:128 * half + 128] = (o * gate).astype(BF16)


def _gqa_call(p_q, p_c, p_l, tabs, qg, kg, batch, n_ctx, n_lat, rope_q):
    t_total = p_q.shape[0] // batch
    tq = min(ATTN_TQ, t_total)
    nq = t_total // tq
    in_specs = [
        pl.BlockSpec((tq, 256), lambda b, i: (b * nq + i, 6)),
        pl.BlockSpec((tq, 256), lambda b, i: (b * nq + i, 8)),
        pl.BlockSpec((n_ctx, 256), lambda b, i: (b, 7)),
    ]
    args = [p_q, p_q, p_c]
    if n_lat:
        in_specs.append(pl.BlockSpec((n_lat, 256), lambda b, i: (b, 7)))
        args.append(p_l)
    if rope_q:
        in_specs += [pl.BlockSpec((tq, 256), lambda b, i: (i, 0))] * 2
        args += [tabs["cos_g"], tabs["sin_g"]]
    if n_lat:
        in_specs += [pl.BlockSpec((n_lat, 128), lambda b, i: (0, 0))] * 2
        args += [tabs["cos_g"], tabs["sin_g"]]
    in_specs += [pl.BlockSpec((1, 256), lambda b, i: (0, 0)),
                 pl.BlockSpec((1, 128), lambda b, i: (0, 0))]
    args += [qg, kg]
    s_keys = n_ctx + n_lat
    return pl.pallas_call(
        functools.partial(_gqa_kernel, n_ctx=n_ctx, n_lat=n_lat, rope_q=rope_q),
        grid=(batch, nq),
        in_specs=in_specs,
        out_specs=pl.BlockSpec((tq, 256), lambda b, i: (b * nq + i, 0)),
        out_shape=jax.ShapeDtypeStruct((p_q.shape[0], 256), BF16),
        scratch_shapes=[pltpu.VMEM((128, s_keys), BF16),
                        pltpu.VMEM((s_keys, 256), BF16)],
        compiler_params=pltpu.CompilerParams(
            dimension_semantics=("arbitrary", "arbitrary"),
            vmem_limit_bytes=VMEM_LIMIT),
        name="gqa_attn",
    )(*args)


def _diff_kernel(*refs, n_ctx, n_lat, rope_q, lam_init, bounded):
    it = iter(refs)
    q_ref, dg_ref, kc_ref, vc_ref = next(it), next(it), next(it), next(it)
    if n_lat:
        kl_ref, vl_ref = next(it), next(it)
    if rope_q:
        cosq_ref, sinq_ref = next(it), next(it)
    if n_lat:
        cosk_ref, sink_ref = next(it), next(it)
    lam_ref, ng_ref = next(it), next(it)
    y_ref = next(it)
    lmin_ref = next(it) if bounded else None
    kt_s, vlo_s, vhi_s = next(it), next(it), next(it)
    kmax_s = next(it) if bounded else None
    s_keys = n_ctx + n_lat
    seg32 = _seg_ones(256, DIFF_QK_DIM)

    def map_norms(x):
        return jnp.sqrt(_dot((x * x).astype(BF16), seg32))

    @pl.when(pl.program_id(1) == 0)
    def _prep_kv():
        kc = kc_ref[...]
        kt_s[:, 0:n_ctx] = kc.T.astype(BF16)
        vlo_s[0:n_ctx, 0:128] = vc_ref[:, 0:128].astype(BF16)
        vhi_s[0:n_ctx, 0:128] = vc_ref[:, 128:256].astype(BF16)
        if bounded:
            kmax = jnp.max(map_norms(kc), axis=0, keepdims=True)
        if n_lat:
            kl = _rope(kl_ref[...], cosk_ref[...], sink_ref[...], 16)
            kt_s[:, n_ctx:s_keys] = kl.T.astype(BF16)
            vlo_s[n_ctx:s_keys, 0:128] = vl_ref[:, 0:128].astype(BF16)
            vhi_s[n_ctx:s_keys, 0:128] = vl_ref[:, 128:256].astype(BF16)
            if bounded:
                kmax = jnp.maximum(kmax, jnp.max(map_norms(kl), axis=0, keepdims=True))
        if bounded:
            kmax_s[...] = jnp.broadcast_to(kmax, kmax_s.shape)

    lp = lam_ref[...]
    lam = (jnp.exp(jnp.sum(lp[0:1, :] * lp[1:2, :], axis=-1, keepdims=True))
           - jnp.exp(jnp.sum(lp[2:3, :] * lp[3:4, :], axis=-1, keepdims=True))
           + lam_init)

    q = q_ref[...]
    if rope_q:
        q = _rope(q, cosq_ref[...], sinq_ref[...], 16)
    q = q * (DIFF_QK_DIM ** -0.5 * LOG2E)
    tq = q.shape[0]
    sub = min(DIFF_SUB, tq)
    lane256 = lax.broadcasted_iota(jnp.int32, (sub, 256), 1)
    lane128 = lax.broadcasted_iota(jnp.int32, (sub, 128), 1)
    seg128 = _seg_ones(128, 64)
    if bounded:
        bound_all = map_norms(q) * kmax_s[0:1, :] * SCORE_BOUND_MARGIN
    lhs_list, v_list, bounds = [], [], []
    for r0 in range(0, tq, sub):
        for mp in range(2 * DIFF_HEADS):
            in_map = (lane256 >= 32 * mp) & (lane256 < 32 * mp + 32)
            lhs_list.append(jnp.where(in_map, q[r0:r0 + sub, :], 0.0).astype(BF16))
            if bounded:
                bounds.append(bound_all[r0:r0 + sub, 32 * mp:32 * mp + 1])
        v_list += [vlo_s, vlo_s, vhi_s, vhi_s]

    if bounded:
        heads, l_min = _attend_diff_pairs_bounded(lhs_list, bounds, kt_s, v_list, lam)
        lmin_ref[...] = jnp.broadcast_to(l_min, lmin_ref.shape)
    else:
        heads = _attend_diff_pairs(lhs_list, kt_s, v_list, lam)
    for j, r0 in enumerate(range(0, tq, sub)):
        for half in range(2):
            o = jnp.where(lane128 < 64, heads[4 * j + 2 * half], heads[4 * j + 2 * half + 1])
            n = _seg_rms(o, 64, seg128) * ng_ref[:, 128 * half:128 * half + 128]
            n = n * (1.0 - lam_init)
            gate = _silu(dg_ref[r0:r0 + sub, 128 * half:128 * half + 128])
            y_ref[r0:r0 + sub, 128 * half:128 * half + 128] = (n * gate).astype(BF16)


def _diff_call(p_q, p_c, p_l, tabs, lam_params, ng, batch, n_ctx, n_lat, rope_q,
               lam_init, bounded):
    t_total = p_q.shape[0] // batch
    tq = min(DIFF_TQ, t_total)
    nq = t_total // tq
    in_specs = [
        pl.BlockSpec((tq, 256), lambda b, i: (b * nq + i, 9)),
        pl.BlockSpec((tq, 256), lambda b, i: (b * nq + i, 12)),
        pl.BlockSpec((n_ctx, 256), lambda b, i: (b, 10)),
        pl.BlockSpec((n_ctx, 256), lambda b, i: (b, 11)),
    ]
    args = [p_q, p_q, p_c, p_c]
    if n_lat:
        in_specs += [pl.BlockSpec((n_lat, 256), lambda b, i: (b, 10)),
                     pl.BlockSpec((n_lat, 256), lambda b, i: (b, 11))]
        args += [p_l, p_l]
    if rope_q:
        in_specs += [pl.BlockSpec((tq, 256), lambda b, i: (i, 0))] * 2
        args += [tabs["cos_d"], tabs["sin_d"]]
    if n_lat:
        in_specs += [pl.BlockSpec((n_lat, 256), lambda b, i: (0, 0))] * 2
        args += [tabs["cos_d"], tabs["sin_d"]]
    in_specs += [pl.BlockSpec((4, DIFF_QK_DIM), lambda b, i: (0, 0)),
                 pl.BlockSpec((1, 256), lambda b, i: (0, 0))]
    args += [lam_params, ng]
    s_keys = n_ctx + n_lat
    out_specs = [pl.BlockSpec((tq, 256), lambda b, i: (b * nq + i, 0))]
    out_shape = [jax.ShapeDtypeStruct((p_q.shape[0], 256), BF16)]
    scratch = [pltpu.VMEM((256, s_keys), BF16),
               pltpu.VMEM((s_keys, 128), BF16),
               pltpu.VMEM((s_keys, 128), BF16)]
    if bounded:
        out_specs.append(pl.BlockSpec((None, 8, 128), lambda b, i: (b * nq + i, 0, 0)))
        out_shape.append(jax.ShapeDtypeStruct((batch * nq, 8, 128), F32))
        scratch.append(pltpu.VMEM((8, 256), F32))
    outs = pl.pallas_call(
        functools.partial(_diff_kernel, n_ctx=n_ctx, n_lat=n_lat, rope_q=rope_q,
                          lam_init=lam_init, bounded=bounded),
        grid=(batch, nq),
        in_specs=in_specs,
        out_specs=out_specs,
        out_shape=out_shape,
        scratch_shapes=scratch,
        compiler_params=pltpu.CompilerParams(
            dimension_semantics=("arbitrary", "arbitrary"),
            vmem_limit_bytes=VMEM_LIMIT),
        name="diff_attn_bounded" if bounded else "diff_attn",
    )(*args)
    return (outs[0], outs[1]) if bounded else outs[0]


def _ssd_kernel(xc_ref, xl_ref, bc_ref, bl_ref, cc_ref, cl_ref, zc_ref, zl_ref,
                dtc_ref, dtl_ref, cwx_ref, cwb_ref, cwc_ref, cbx_ref, cbb_ref,
                cbc_ref, par_ref, dsk_ref, ng_ref, *rest, n_ctx, n_lat, ctx_out):
    if ctx_out:
        oc_ref, ol_ref = rest[0], rest[1]
        rest = rest[2:]
    else:
        oc_ref, ol_ref = None, rest[0]
        rest = rest[1:]
    xp_s, dtr_s, yacc_s, sloc_s, e_s, cbf_s, sin_s, dec_s, vun_s = rest

    g = pl.program_id(1)
    ncc = n_ctx // CHUNK
    ncl = n_lat // CHUNK
    nch = ncc + ncl
    GROUP = next(n for n in (6, 3, 2, 1) if nch % n == 0)
    t_all = n_ctx + n_lat
    lat0 = n_ctx + 2 * HALO

    zeros_h = jnp.zeros((HALO, 512), F32)
    xp_s[0:HALO, :] = zeros_h
    xp_s[HALO:HALO + n_ctx, 0:256] = xc_ref[...]
    xp_s[HALO:HALO + n_ctx, 256:384] = bc_ref[...]
    xp_s[HALO:HALO + n_ctx, 384:512] = cc_ref[...]
    xp_s[HALO + n_ctx:lat0, :] = zeros_h
    xp_s[lat0:lat0 + n_lat, 0:256] = xl_ref[...]
    xp_s[lat0:lat0 + n_lat, 256:384] = bl_ref[...]
    xp_s[lat0:lat0 + n_lat, 384:512] = cl_ref[...]
    xp_s[lat0 + n_lat:lat0 + n_lat + HALO, :] = zeros_h

    dtr_s[0:n_ctx, :] = dtc_ref[...]
    dtr_s[n_ctx:t_all, :] = dtl_ref[...]
    a_col = -jnp.exp(par_ref[0:16, :])
    bias_col = par_ref[16:32, :]

    def _softplus(v):
        return jnp.maximum(v, 0.0) + jnp.log1p(jnp.exp(-jnp.abs(v)))

    r128 = lax.broadcasted_iota(jnp.int32, (CHUNK, CHUNK), 0)
    c128 = lax.broadcasted_iota(jnp.int32, (CHUNK, CHUNK), 1)
    lower = c128 <= r128
    upper = c128 >= r128
    tril = jnp.where(lower, 1.0, 0.0).astype(BF16)
    triu = jnp.where(upper, 1.0, 0.0).astype(BF16)
    fwd_row = lax.broadcasted_iota(jnp.int32, (16, CHUNK), 0) < 8
    fwd_row1 = lax.broadcasted_iota(jnp.int32, (16, 1), 0) < 8
    er = lax.broadcasted_iota(jnp.int32, (CHUNK, 1024), 0)
    ec = lax.broadcasted_iota(jnp.int32, (CHUNK, 1024), 1)
    src_lane = 16 + 16 * (ec >> 9) + 8 * ((ec >> 8) & 1) + ((ec >> 6) & 3)
    expand = jnp.where(er == src_lane, 1.0, 0.0).astype(BF16)
    er0 = lax.broadcasted_iota(jnp.int32, (CHUNK, 512), 0)
    ec0 = lax.broadcasted_iota(jnp.int32, (CHUNK, 512), 1)
    expand_tot = jnp.where(er0 == 8 * (ec0 >> 8) + ((ec0 >> 6) & 3), 1.0, 0.0).astype(BF16)
    lane256 = lax.broadcasted_iota(jnp.int32, (CHUNK, 256), 1)
    lane128_1 = lax.broadcasted_iota(jnp.int32, (1, CHUNK), 1)

    cw = jnp.concatenate([cwx_ref[...], cwb_ref[...], cwc_ref[...]], axis=1)
    cb = jnp.concatenate([cbx_ref[...], cbb_ref[...], cbc_ref[...]], axis=1)
    dsk = dsk_ref[...]

    def phase_a(grp, carry):
        cs = [grp * GROUP + j for j in range(GROUP)]
        r0s = [pl.multiple_of(c * CHUNK, CHUNK) for c in cs]

        def conv(c):
            wstart = pl.multiple_of(c * CHUNK + jnp.where(c >= ncc, HALO, 0), 8)
            win = xp_s[pl.ds(wstart, CHUNK + 2 * HALO), :]
            acc = jnp.broadcast_to(cb, (CHUNK, 512))
            for k in range(CONV_K):
                d = k - CONV_K // 2
                if d == 0:
                    tap = win[HALO:HALO + CHUNK, :]
                else:
                    tap = pltpu.roll(win, (-d) % (CHUNK + 2 * HALO), 0)[HALO:HALO + CHUNK, :]
                acc = acc + cw[k:k + 1, :] * tap
            return _silu(acc)

        dtts = [_softplus(dtr_s[pl.ds(r0, CHUNK), :].T[0:16, :] + bias_col) for r0 in r0s]
        a_ts = [dtt * a_col for dtt in dtts]
        acol_ts = [jnp.where(fwd_row, _dot_exact_lhs(a_t, triu), _dot_exact_lhs(a_t, tril))
                   for a_t in a_ts]
        us = [conv(c) for c in cs]
        xus = [u[:, 0:256] for u in us]
        bts = [u[:, 256:384].T.astype(BF16) for u in us]
        cbfs = [u[:, 384:512].astype(BF16) for u in us]
        xbfs = [xu.astype(BF16) for xu in xus]
        gmats = [_dot(cbf, bt) for cbf, bt in zip(cbfs, bts)]
        tms = []
        for dtt, acol_t in zip(dtts, acol_ts):
            tot = jnp.where(fwd_row1, acol_t[:, CHUNK - 1:CHUNK], acol_t[:, 0:1])
            w_t = dtt * jnp.exp(tot - acol_t)
            e_t = jnp.exp(acol_t)
            stacked = jnp.concatenate(
                [acol_t, w_t, e_t, jnp.zeros((CHUNK - 48, CHUNK), F32)], axis=0)
            tms.append(stacked.T)
        wes = [_dot(tm.astype(BF16), expand) for tm in tms]
        decs = []
        for tm in tms:
            tot_row = jnp.where(lane128_1 < 8, tm[CHUNK - 1:CHUNK, :], tm[0:1, :])
            tot512 = _dot_exact_lhs(jnp.broadcast_to(tot_row, (8, CHUNK)), expand_tot)
            decs.append(jnp.exp(tot512))
        ydiags = [jnp.zeros((CHUNK, 256), F32) for _ in cs]
        for i in range(4):
            in_head = (lane256 >= 64 * i) & (lane256 < 64 * i + 64)
            for j in range(GROUP):
                tm, acol_t, dtt = tms[j], acol_ts[j], dtts[j]
                arg = jnp.where(lower, tm[:, i:i + 1] - acol_t[i:i + 1, :],
                                tm[:, 8 + i:9 + i] - acol_t[8 + i:9 + i, :])
                scale = (jnp.where(lower, dtt[i:i + 1, :], 0.0)
                         + jnp.where(upper, dtt[8 + i:9 + i, :], 0.0))
                wmat = (gmats[j] * jnp.exp(arg) * scale).astype(BF16)
                ydiags[j] = jnp.where(in_head, _dot(wmat, xbfs[j]), ydiags[j])
        for j, c in enumerate(cs):
            w512 = wes[j][:, 0:512]
            e512 = wes[j][:, 512:1024]
            xdw = (jnp.concatenate([xus[j], xus[j]], axis=1) * w512).astype(BF16)
            sloc_s[c] = _dot(bts[j], xdw)
            dec_s[c] = decs[j]
            e_s[c] = e512
            cbf_s[c] = cbfs[j]
            yacc_s[pl.ds(r0s[j], CHUNK), :] = ydiags[j] + xus[j] * dsk
        return carry

    lax.fori_loop(0, nch // GROUP, phase_a, 0)

    fwd_order = list(range(nch))
    bwd_order = list(range(ncc - 1, -1, -1)) + list(range(nch - 1, ncc - 1, -1))
    for order, lo in ((fwd_order, 0), (bwd_order, 256)):
        state = jnp.zeros((SSD_STATE, 256), F32)
        for c in order:
            sin_s[c, :, lo:lo + 256] = state.astype(BF16)
            state = (state * dec_s[c, 0:1, lo:lo + 256]
                     + sloc_s[c, :, lo:lo + 256])

    def phase_c(cs, z_ref, zrows):
        yos = [_dot(cbf_s[c], sin_s[c]) for c in cs]
        vs = []
        for c, zrow, yo in zip(cs, zrows, yos):
            r0 = c * CHUNK if isinstance(c, int) else pl.multiple_of(c * CHUNK, CHUNK)
            yo = yo * e_s[c]
            y = yacc_s[pl.ds(r0, CHUNK), :] + yo[:, 0:256] + yo[:, 256:512]
            vs.append((r0, y * _silu(z_ref[pl.ds(zrow, CHUNK), :])))

        @pl.when(g == 0)
        def _():
            for r0, v in vs:
                vun_s[pl.ds(r0, CHUNK), 0:256] = v

        @pl.when(g == 1)
        def _():
            for r0, v in vs:
                vun_s[pl.ds(r0, CHUNK), 256:512] = v

    if ctx_out:
        phase_c(list(range(ncc)), zc_ref, [c * CHUNK for c in range(ncc)])

    group_c = next(n for n in (4, 2, 1) if ncl % n == 0)

    def phase_c_lat(k, carry):
        ks = [k * group_c + j for j in range(group_c)]
        phase_c([kk + ncc for kk in ks], zl_ref,
                [pl.multiple_of(kk * CHUNK, CHUNK) for kk in ks])
        return carry

    lax.fori_loop(0, ncl // group_c, phase_c_lat, 0)

    @pl.when(g == SSD_GROUPS - 1)
    def _finalize():
        ng = ng_ref[...]

        def norm_rows(r0, nrows):
            v = vun_s[pl.ds(r0, nrows), :]
            ms = jnp.mean(v * v, axis=-1, keepdims=True)
            return (v * lax.rsqrt(ms + EPS) * ng).astype(BF16)

        if ctx_out:
            oc_ref[...] = norm_rows(0, n_ctx)

        def fin(k, carry):
            r0 = pl.multiple_of(k * 256, 256)
            ol_ref[pl.ds(r0, 256), :] = norm_rows(n_ctx + r0, 256)
            return carry

        lax.fori_loop(0, n_lat // 256, fin, 0)


def _ssd_call(p_c, p_l, conv_w8, conv_b, par, dsk, ng, batch, n_ctx, n_lat, ctx_out):
    nch = (n_ctx + n_lat) // CHUNK
    t_all = n_ctx + n_lat
    in_specs = [
        pl.BlockSpec((n_ctx, 256), lambda b, g: (b, g)),
        pl.BlockSpec((n_lat, 256), lambda b, g: (b, g)),
        pl.BlockSpec((n_ctx, 128), lambda b, g: (b, 4 + g)),
        pl.BlockSpec((n_lat, 128), lambda b, g: (b, 4 + g)),
        pl.BlockSpec((n_ctx, 128), lambda b, g: (b, 6 + g)),
        pl.BlockSpec((n_lat, 128), lambda b, g: (b, 6 + g)),
        pl.BlockSpec((n_ctx, 256), lambda b, g: (b, 4 + g)),
        pl.BlockSpec((n_lat, 256), lambda b, g: (b, 4 + g)),
        pl.BlockSpec((n_ctx, 128), lambda b, g: (b, 26 + g)),
        pl.BlockSpec((n_lat, 128), lambda b, g: (b, 26 + g)),
        pl.BlockSpec((8, 256), lambda b, g: (0, g)),
        pl.BlockSpec((8, 128), lambda b, g: (0, 4 + g)),
        pl.BlockSpec((8, 128), lambda b, g: (0, 6 + g)),
        pl.BlockSpec((1, 256), lambda b, g: (0, g)),
        pl.BlockSpec((1, 128), lambda b, g: (0, 4 + g)),
        pl.BlockSpec((1, 128), lambda b, g: (0, 6 + g)),
        pl.BlockSpec((None, 32, 128), lambda b, g: (g, 0, 0)),
        pl.BlockSpec((1, 256), lambda b, g: (0, g)),
        pl.BlockSpec((1, 512), lambda b, g: (0, 0)),
    ]
    args = [p_c, p_l, p_c, p_l, p_c, p_l, p_c, p_l, p_c, p_l,
            conv_w8, conv_w8, conv_w8, conv_b, conv_b, conv_b, par, dsk, ng]
    out_specs = [pl.BlockSpec((n_lat, 512), lambda b, g: (b, 0))]
    out_shape = [jax.ShapeDtypeStruct((batch * n_lat, 512), BF16)]
    if ctx_out:
        out_specs = [pl.BlockSpec((n_ctx, 512), lambda b, g: (b, 0))] + out_specs
        out_shape = [jax.ShapeDtypeStruct((batch * n_ctx, 512), BF16)] + out_shape
    scratch = [
        pltpu.VMEM((t_all + 3 * HALO, 512), F32),
        pltpu.VMEM((t_all, 128), F32),
        pltpu.VMEM((t_all, 256), F32),
        pltpu.VMEM((nch, SSD_STATE, 512), F32),
        pltpu.VMEM((nch, CHUNK, 512), F32),
        pltpu.VMEM((nch, CHUNK, 128), BF16),
        pltpu.VMEM((nch, SSD_STATE, 512), BF16),
        pltpu.VMEM((nch, 8, 512), F32),
        pltpu.VMEM((t_all, 512), F32),
    ]
    outs = pl.pallas_call(
        functools.partial(_ssd_kernel, n_ctx=n_ctx, n_lat=n_lat, ctx_out=ctx_out),
        grid=(batch, SSD_GROUPS),
        in_specs=in_specs,
        out_specs=out_specs,
        out_shape=out_shape,
        scratch_shapes=scratch,
        compiler_params=pltpu.CompilerParams(
            dimension_semantics=("arbitrary", "arbitrary"),
            vmem_limit_bytes=VMEM_LIMIT),
        name="ssd_scan",
    )(*args)
    if ctx_out:
        return outs[1], outs[0]
    return outs[0], None


def _rope_tables(n_lat):
    rows = n_lat // GRID_W
    row_idx = np.repeat(np.arange(rows), GRID_W).astype(np.float32)
    col_idx = (np.arange(rows * GRID_W) % GRID_W).astype(np.float32)

    def tables(dim, reps):
        quarter = dim // 4
        inv = (ROPE_BASE ** (-np.arange(quarter, dtype=np.float32) / quarter)).astype(np.float32)
        ang = np.concatenate([row_idx[:, None] * inv, col_idx[:, None] * inv], axis=-1)
        cos, sin = np.cos(ang.astype(np.float64)), np.sin(ang.astype(np.float64))
        cos2 = np.concatenate([cos, cos], axis=-1).astype(np.float32)
        sin2 = np.concatenate([-sin, sin], axis=-1).astype(np.float32)
        return jnp.asarray(np.tile(cos2, (1, reps))), jnp.asarray(np.tile(sin2, (1, reps)))

    cos_g, sin_g = tables(GQA_HEAD_DIM, 4)
    cos_d, sin_d = tables(DIFF_QK_DIM, 8)
    return {"cos_g": cos_g, "sin_g": sin_g, "cos_d": cos_d, "sin_d": sin_d}


def kernel(x, c, ctx, c_ctx, w_mod, b_mod, g_pre, g_post, w_in, conv_w, conv_b,
           a_log_fwd, a_log_bwd, dt_bias_fwd, dt_bias_bwd, d_skip, ssd_norm_g,
           q_norm_g, k_norm_g, diff_lambda, diff_norm_g, w_out):
    batch, n_lat, _ = x.shape
    n_ctx = ctx.shape[1]
    depth = w_mod.shape[0]
    assert n_lat % 512 == 0 and n_ctx % 256 == 0 and (batch * n_ctx) % 512 == 0
    assert batch + 1 <= 16

    in_perm, out_perm = _in_col_perm(), _out_row_perm()
    w_in_p = [_take_runs(w_in[l], in_perm, 1, IN_COLS).astype(BF16) for l in range(depth)]
    w_out_p = [_take_runs(w_out[l], out_perm, 0, None).astype(BF16) for l in range(depth)]
    conv_w8 = jnp.pad(conv_w, ((0, 0), (0, 8 - CONV_K), (0, 0)))
    conv_b1 = conv_b[:, None, :]

    def group16(fwd, bwd):
        out = jnp.zeros((depth, SSD_GROUPS, 16), F32)
        for g in range(SSD_GROUPS):
            out = out.at[:, g, 0:4].set(fwd[:, 4 * g:4 * g + 4])
            out = out.at[:, g, 8:12].set(bwd[:, 4 * g:4 * g + 4])
        return out

    ssd_par = jnp.broadcast_to(
        jnp.concatenate([group16(a_log_fwd, a_log_bwd),
                         group16(dt_bias_fwd, dt_bias_bwd)], axis=-1)[..., None],
        (depth, SSD_GROUPS, 32, 128))
    dsk = jnp.repeat(d_skip, SSD_HEAD_DIM, axis=1)[:, None, :]
    qg = jnp.tile(q_norm_g, (1, 4))[:, None, :]
    kg = jnp.tile(k_norm_g, (1, 2))[:, None, :]
    dng = jnp.tile(diff_norm_g, (1, 4))[:, None, :]
    tabs = _rope_tables(n_lat)

    cs = jnp.concatenate(
        [c, c_ctx[None, :], jnp.zeros((16 - batch - 1, D_MODEL), F32)], axis=0)
    mod_all = _mod_call(cs, w_mod, b_mod)

    h = x.reshape(batch * n_lat, D_MODEL)
    hc = ctx.reshape(batch * n_ctx, D_MODEL)
    for l in range(depth):
        ctx_out = l < depth - 1
        lam_init = 0.8 - 0.6 * float(np.exp(-0.3 * l))
        mod3 = mod_all[l][:, None, :]
        p_l = _inproj_call(h, mod3, g_pre[l][None, :], w_in_p[l], n_lat // 512, None)
        p_c = _inproj_call(hc, mod3, g_pre[l][None, :], w_in_p[l], None, batch)

        ys_l, ys_c = _ssd_call(p_c, p_l, conv_w8[l], conv_b1[l], ssd_par[l], dsk[l],
                               ssd_norm_g[l][None, :], batch, n_ctx, n_lat, ctx_out)
        yg_l = _gqa_call(p_l, p_c, p_l, tabs, qg[l], kg[l], batch, n_ctx, n_lat, True)
        diff_args = (p_l, p_c, p_l, tabs, diff_lambda[l], dng[l], batch, n_ctx, n_lat, True,
                     lam_init)
        yd_fast, row_sum_min = _diff_call(*diff_args, True)
        yd_l = lax.cond(jnp.min(row_sum_min) >= MIN_ROW_SUM,
                        lambda: yd_fast, lambda: _diff_call(*diff_args, False))
        h = _outproj_call(ys_l, yg_l, yd_l, h, mod3, g_post[l][None, :], w_out_p[l],
                          n_lat, None)
        if ctx_out:
            yg_c = _gqa_call(p_c, p_c, None, tabs, qg[l], kg[l], batch, n_ctx, 0, False)
            yd_c = _diff_call(p_c, p_c, None, tabs, diff_lambda[l], dng[l], batch,
                              n_ctx, 0, False, lam_init, False)
            hc = _outproj_call(ys_c, yg_c, yd_c, hc, mod3, g_post[l][None, :],
                               w_out_p[l], None, batch)
    return h.reshape(batch, n_lat, D_MODEL)
```

```python
import functools

import numpy as np
import jax
import jax.numpy as jnp
from jax import lax
from jax.experimental import pallas as pl
from jax.experimental.pallas import tpu as pltpu

F32 = jnp.float32
BF16 = jnp.bfloat16

D_MODEL = 1024
GRID_W = 64
ROPE_BASE = 10000.0
EPS = 1e-6
LOG2E = 1.4426950408889634

SSD_WIDTH = 512
SSD_HEADS = 8
SSD_HEAD_DIM = 64
SSD_GROUPS = 2
SSD_STATE = 128
CHUNK = 128
CONV_K = 5
HALO = 8
GQA_HEADS = 4
GQA_HEAD_DIM = 64
DIFF_HEADS = 4
DIFF_QK_DIM = 32
DIFF_V_DIM = 64

_IN_SPLITS = (("xbc", 1024), ("z", 512), ("dt", 16), ("gq", 256), ("gk", 128),
              ("gv", 128), ("gg", 256), ("dq", 256), ("dk", 256), ("dv", 256),
              ("dg", 256))
IN_COLS = sum(s for _, s in _IN_SPLITS)
NP = 28 * 128
GQ_HEAD_ORDER = (0, 2, 1, 3)

VMEM_LIMIT = 56 * 1024 * 1024
ATTN_TQ = 1024
DIFF_TQ = 1024
INPROJ_TM = 1024
INPROJ_SUB = 512
ATTN_SUB = 512
DIFF_SUB = 512
ATTN_AHEAD = 2
DIFF_AHEAD = 1
SCORE_BOUND_MARGIN = 1.02
MIN_ROW_SUM = 2.0 ** -90
OUTPROJ_TM = 1024
OUTPROJ_SUB = 256


def _in_col_perm():
    off, o = {}, 0
    for name, size in _IN_SPLITS:
        off[name] = o
        o += size
    pad = IN_COLS
    cols = list(range(off["xbc"], off["xbc"] + 1024))
    cols += list(range(off["z"], off["z"] + 512))
    for h in GQ_HEAD_ORDER:
        cols += list(range(off["gq"] + 64 * h, off["gq"] + 64 * h + 64))
    cols += list(range(off["gk"], off["gk"] + 128))
    cols += list(range(off["gv"], off["gv"] + 128))
    for h in GQ_HEAD_ORDER:
        cols += list(range(off["gg"] + 64 * h, off["gg"] + 64 * h + 64))
    for name in ("dq", "dk", "dv", "dg"):
        cols += list(range(off[name], off[name] + 256))
    for g in range(SSD_GROUPS):
        blk = [pad] * 128
        for i in range(4):
            blk[i] = off["dt"] + 4 * g + i
            blk[8 + i] = off["dt"] + SSD_HEADS + 4 * g + i
        cols += blk
    assert len(cols) == NP
    return np.asarray(cols, np.int32)


def _out_row_perm():
    rows = list(range(SSD_WIDTH))
    for h in GQ_HEAD_ORDER:
        rows += list(range(SSD_WIDTH + 64 * h, SSD_WIDTH + 64 * h + 64))
    rows += list(range(SSD_WIDTH + 256, SSD_WIDTH + 512))
    return np.asarray(rows, np.int32)


def _take_runs(arr, idx, axis, pad_index):
    pieces, start = [], 0
    idx = [int(i) for i in idx]
    while start < len(idx):
        end = start + 1
        if idx[start] == pad_index:
            while end < len(idx) and idx[end] == pad_index:
                end += 1
            shape = list(arr.shape)
            shape[axis] = end - start
            pieces.append(jnp.zeros(shape, arr.dtype))
        else:
            while end < len(idx) and idx[end] == idx[end - 1] + 1 and idx[end] != pad_index:
                end += 1
            pieces.append(lax.slice_in_dim(arr, idx[start], idx[end - 1] + 1, axis=axis))
        start = end
    return jnp.concatenate(pieces, axis=axis)


def _dot(a, b):
    return jnp.dot(a, b, preferred_element_type=F32)


def _split3(x):
    hi = x.astype(BF16)
    r1 = x - hi.astype(F32)
    mid = r1.astype(BF16)
    lo = (r1 - mid.astype(F32)).astype(BF16)
    return hi, mid, lo


def _dot_exact_lhs(x, m_bf16):
    hi, mid, lo = _split3(x)
    return _dot(hi, m_bf16) + _dot(mid, m_bf16) + _dot(lo, m_bf16)


def _silu(x):
    return x * jax.nn.sigmoid(x)


def _seg_ones(width, seg):
    r = lax.broadcasted_iota(jnp.int32, (width, width), 0)
    c = lax.broadcasted_iota(jnp.int32, (width, width), 1)
    same = (r & ~(seg - 1)) == (c & ~(seg - 1))
    return jnp.where(same, 1.0, 0.0).astype(BF16)


def _seg_rms(x, seg, seg_mat):
    ss = _dot_exact_lhs(x * x, seg_mat)
    return x * lax.rsqrt(ss * (1.0 / seg) + EPS)


def _rope(x, cos, sin_signed, half):
    w = x.shape[-1]
    lane = lax.broadcasted_iota(jnp.int32, x.shape, 1)
    first = (lane & (2 * half - 1)) < half
    swapped = jnp.where(first, pltpu.roll(x, w - half, 1), pltpu.roll(x, half, 1))
    return x * cos + swapped * sin_signed


def _mod_kernel(cs_ref, w_ref, b_ref, o_ref):
    s = _silu(cs_ref[...]).astype(BF16)
    o_ref[...] = _dot(s, w_ref[...].astype(BF16)) + b_ref[...]


def _mod_call(cs, w_mod, b_mod):
    depth = w_mod.shape[0]
    nrow = cs.shape[0]
    tn = 1024
    return pl.pallas_call(
        _mod_kernel,
        grid=(depth, 3 * D_MODEL // tn),
        in_specs=[
            pl.BlockSpec((nrow, D_MODEL), lambda l, j: (0, 0)),
            pl.BlockSpec((None, D_MODEL, tn), lambda l, j: (l, 0, j)),
            pl.BlockSpec((None, 1, tn), lambda l, j: (l, 0, j)),
        ],
        out_specs=pl.BlockSpec((None, nrow, tn), lambda l, j: (l, 0, j)),
        out_shape=jax.ShapeDtypeStruct((depth, nrow, 3 * D_MODEL), F32),
        compiler_params=pltpu.CompilerParams(
            dimension_semantics=("arbitrary", "arbitrary")),
        name="mod_proj",
    )(cs, w_mod, b_mod.reshape(depth, 1, 3 * D_MODEL))


def _inproj_kernel(h_ref, mod_ref, g_ref, w_ref, o_ref):
    tm = h_ref.shape[0]
    sub = min(INPROJ_SUB, tm)
    sh = mod_ref[:, 0:D_MODEL]
    sc = mod_ref[:, D_MODEL:2 * D_MODEL]
    gain = g_ref[...]
    tn = 512
    for r0 in range(0, tm, sub):
        x = h_ref[r0:r0 + sub, :]
        ms = jnp.mean(x * x, axis=-1, keepdims=True)
        y = x * lax.rsqrt(ms + EPS) * gain
        u = (y * (1.0 + sc) + sh).astype(BF16)
        for j in range(NP // tn):
            o_ref[r0:r0 + sub, j * tn:(j + 1) * tn] = _dot(u, w_ref[:, j * tn:(j + 1) * tn])


def _inproj_call(h, mod3, g_pre, w_bf16, rows_per_mod, fixed_row):
    n_tok = h.shape[0]
    tm = INPROJ_TM
    assert n_tok % tm == 0
    if fixed_row is None:
        assert rows_per_mod % tm == 0
        mod_idx = lambda i: (i // (rows_per_mod // tm), 0, 0)
    else:
        mod_idx = lambda i: (fixed_row, 0, 0)
    return pl.pallas_call(
        _inproj_kernel,
        grid=(n_tok // tm,),
        in_specs=[
            pl.BlockSpec((tm, D_MODEL), lambda i: (i, 0)),
            pl.BlockSpec((None, 1, 3 * D_MODEL), mod_idx),
            pl.BlockSpec((1, D_MODEL), lambda i: (0, 0)),
            pl.BlockSpec((D_MODEL, NP), lambda i: (0, 0), pipeline_mode=pl.Buffered(1)),
        ],
        out_specs=pl.BlockSpec((tm, NP), lambda i: (i, 0)),
        out_shape=jax.ShapeDtypeStruct((n_tok, NP), F32),
        compiler_params=pltpu.CompilerParams(
            dimension_semantics=("arbitrary",), vmem_limit_bytes=VMEM_LIMIT),
        name="in_proj",
    )(h, mod3, g_pre, w_bf16)


def _outproj_kernel(ys_ref, yg_ref, yd_ref, h_ref, mod_ref, g_ref, w_ref, o_ref):
    tm = h_ref.shape[0]
    sub = min(OUTPROJ_SUB, tm)
    gt = mod_ref[:, 2 * D_MODEL:3 * D_MODEL]
    gain = g_ref[...]

    def project(r0):
        return (_dot(ys_ref[r0:r0 + sub, :], w_ref[0:512, :])
                + _dot(yg_ref[r0:r0 + sub, :], w_ref[512:768, :])
                + _dot(yd_ref[r0:r0 + sub, :], w_ref[768:1024, :]))

    o_next = project(0)
    for r0 in range(0, tm, sub):
        o = o_next
        if r0 + sub < tm:
            o_next = project(r0 + sub)
        ms = jnp.mean(o * o, axis=-1, keepdims=True)
        n = o * lax.rsqrt(ms + EPS) * gain
        o_ref[r0:r0 + sub, :] = h_ref[r0:r0 + sub, :] + gt * n


def _outproj_call(ys, yg, yd, h, mod3, g_post, w_bf16, rows_per_mod, fixed_row):
    n_tok = h.shape[0]
    tm = OUTPROJ_TM
    assert n_tok % tm == 0
    if fixed_row is None:
        assert rows_per_mod % tm == 0
        mod_idx = lambda i: (i // (rows_per_mod // tm), 0, 0)
    else:
        mod_idx = lambda i: (fixed_row, 0, 0)
    return pl.pallas_call(
        _outproj_kernel,
        grid=(n_tok // tm,),
        in_specs=[
            pl.BlockSpec((tm, 512), lambda i: (i, 0)),
            pl.BlockSpec((tm, 256), lambda i: (i, 0)),
            pl.BlockSpec((tm, 256), lambda i: (i, 0)),
            pl.BlockSpec((tm, D_MODEL), lambda i: (i, 0)),
            pl.BlockSpec((None, 1, 3 * D_MODEL), mod_idx),
            pl.BlockSpec((1, D_MODEL), lambda i: (0, 0)),
            pl.BlockSpec((D_MODEL, D_MODEL), lambda i: (0, 0)),
        ],
        out_specs=pl.BlockSpec((tm, D_MODEL), lambda i: (i, 0)),
        out_shape=jax.ShapeDtypeStruct((n_tok, D_MODEL), F32),
        compiler_params=pltpu.CompilerParams(
            dimension_semantics=("arbitrary",), vmem_limit_bytes=VMEM_LIMIT),
        name="out_proj",
    )(ys, yg, yd, h, mod3, g_post, w_bf16)


def _attend_many(lhs_list, kt_ref, vext_refs):
    def scores(i):
        s = _dot(lhs_list[i], kt_ref[...])
        return s, jnp.max(s, axis=-1, keepdims=True)

    outs = []
    n = len(lhs_list)
    ahead = [scores(i) for i in range(min(ATTN_AHEAD, n))]
    for i, vext_ref in enumerate(vext_refs):
        s, m = ahead.pop(0)
        if i + ATTN_AHEAD < n:
            ahead.append(scores(i + ATTN_AHEAD))
        p = jnp.exp2(s - m).astype(BF16)
        oe = _dot(p, vext_ref[...])
        outs.append(oe[:, 0:128] / oe[:, 128:256])
    return outs


def _attend_diff_pairs(lhs_list, kt_ref, v_refs, lam):
    n_heads = len(v_refs)

    def scores(h):
        s_a = _dot(lhs_list[2 * h], kt_ref[...])
        m_a = jnp.max(s_a, axis=-1, keepdims=True)
        s_b = _dot(lhs_list[2 * h + 1], kt_ref[...])
        m_b = jnp.max(s_b, axis=-1, keepdims=True)
        return s_a, m_a, s_b, m_b

    outs = []
    ahead = [scores(h) for h in range(min(DIFF_AHEAD, n_heads))]
    for h in range(n_heads):
        s_a, m_a, s_b, m_b = ahead.pop(0)
        if h + DIFF_AHEAD < n_heads:
            ahead.append(scores(h + DIFF_AHEAD))
        e_a = jnp.exp2(s_a - m_a)
        e_b = jnp.exp2(s_b - m_b)
        l_a = jnp.sum(e_a, axis=-1, keepdims=True)
        l_b = jnp.sum(e_b, axis=-1, keepdims=True)
        pc = (e_a - (lam * l_a / l_b) * e_b).astype(BF16)
        outs.append(_dot(pc, v_refs[h][...]) / l_a)
    return outs


def _attend_diff_pairs_bounded(lhs_list, bounds, kt_ref, v_refs, lam):
    n_heads = len(v_refs)

    def exps(h):
        e_a = jnp.exp2(_dot(lhs_list[2 * h], kt_ref[...]) - bounds[2 * h])
        e_b = jnp.exp2(_dot(lhs_list[2 * h + 1], kt_ref[...]) - bounds[2 * h + 1])
        return e_a, e_b

    outs, l_min = [], None
    ahead = [exps(h) for h in range(min(DIFF_AHEAD, n_heads))]
    for h in range(n_heads):
        e_a, e_b = ahead.pop(0)
        if h + DIFF_AHEAD < n_heads:
            ahead.append(exps(h + DIFF_AHEAD))
        l_a = jnp.sum(e_a, axis=-1, keepdims=True)
        l_b = jnp.sum(e_b, axis=-1, keepdims=True)
        pc = (e_a - (lam * l_a / l_b) * e_b).astype(BF16)
        outs.append(_dot(pc, v_refs[h][...]) / l_a)
        l_ab = jnp.minimum(l_a, l_b)
        l_min = l_ab if l_min is None else jnp.minimum(l_min, l_ab)
    return outs, jnp.min(l_min, axis=0, keepdims=True)


def _gqa_kernel(*refs, n_ctx, n_lat, rope_q):
    it = iter(refs)
    q_ref, gg_ref, kvc_ref = next(it), next(it), next(it)
    kvl_ref = next(it) if n_lat else None
    if rope_q:
        cosq_ref, sinq_ref = next(it), next(it)
    if n_lat:
        cosk_ref, sink_ref = next(it), next(it)
    qg_ref, kg_ref = next(it), next(it)
    y_ref = next(it)
    kt_s, vext_s = next(it), next(it)

    seg128 = _seg_ones(128, 64)

    @pl.when(pl.program_id(1) == 0)
    def _prep_kv():
        kc = _seg_rms(kvc_ref[:, 0:128], 64, seg128) * kg_ref[...]
        kt_s[:, 0:n_ctx] = kc.T.astype(BF16)
        vext_s[0:n_ctx, 0:128] = kvc_ref[:, 128:256].astype(BF16)
        if n_lat:
            kl = _seg_rms(kvl_ref[:, 0:128], 64, seg128) * kg_ref[...]
            kl = _rope(kl, cosk_ref[...], sink_ref[...], 32)
            kt_s[:, n_ctx:n_ctx + n_lat] = kl.T.astype(BF16)
            vext_s[n_ctx:n_ctx + n_lat, 0:128] = kvl_ref[:, 128:256].astype(BF16)
        vext_s[:, 128:256] = jnp.ones((n_ctx + n_lat, 128), BF16)

    seg256 = _seg_ones(256, 64)
    q = _seg_rms(q_ref[...], 64, seg256) * qg_ref[...]
    if rope_q:
        q = _rope(q, cosq_ref[...], sinq_ref[...], 32)
    q = q * (GQA_HEAD_DIM ** -0.5 * LOG2E)
    tq = q.shape[0]
    sub = min(ATTN_SUB, tq)
    lane = lax.broadcasted_iota(jnp.int32, (sub, 128), 1)
    lhs_list = []
    for r0 in range(0, tq, sub):
        for half in range(2):
            qh = q[r0:r0 + sub, 128 * half:128 * half + 128]
            for kv in range(2):
                in_kv = (lane >= 64 * kv) & (lane < 64 * kv + 64)
                lhs_list.append(jnp.where(in_kv, qh, 0.0).astype(BF16))
    outs = _attend_many(lhs_list, kt_s, [vext_s] * len(lhs_list))
    for j, r0 in enumerate(range(0, tq, sub)):
        for half in range(2):
            o = jnp.where(lane < 64, outs[4 * j + 2 * half], outs[4 * j + 2 * half + 1])
            gate = _silu(gg_ref[r0:r0 + sub, 128 * half:128 * half + 128])
            y_ref[r0:r0 + sub, 128 * half:128 * half + 128] = (o * gate).astype(BF16)


def _gqa_call(p_q, p_c, p_l, tabs, qg, kg, batch, n_ctx, n_lat, rope_q):
    t_total = p_q.shape[0] // batch
    tq = min(ATTN_TQ, t_total)
    nq = t_total // tq
    in_specs = [
        pl.BlockSpec((tq, 256), lambda b, i: (b * nq + i, 6)),
        pl.BlockSpec((tq, 256), lambda b, i: (b * nq + i, 8)),
        pl.BlockSpec((n_ctx, 256), lambda b, i: (b, 7)),
    ]
    args = [p_q, p_q, p_c]
    if n_lat:
        in_specs.append(pl.BlockSpec((n_lat, 256), lambda b, i: (b, 7)))
        args.append(p_l)
    if rope_q:
        in_specs += [pl.BlockSpec((tq, 256), lambda b, i: (i, 0))] * 2
        args += [tabs["cos_g"], tabs["sin_g"]]
    if n_lat:
        in_specs += [pl.BlockSpec((n_lat, 128), lambda b, i: (0, 0))] * 2
        args += [tabs["cos_g"], tabs["sin_g"]]
    in_specs += [pl.BlockSpec((1, 256), lambda b, i: (0, 0)),
                 pl.BlockSpec((1, 128), lambda b, i: (0, 0))]
    args += [qg, kg]
    s_keys = n_ctx + n_lat
    return pl.pallas_call(
        functools.partial(_gqa_kernel, n_ctx=n_ctx, n_lat=n_lat, rope_q=rope_q),
        grid=(batch, nq),
        in_specs=in_specs,
        out_specs=pl.BlockSpec((tq, 256), lambda b, i: (b * nq + i, 0)),
        out_shape=jax.ShapeDtypeStruct((p_q.shape[0], 256), BF16),
        scratch_shapes=[pltpu.VMEM((128, s_keys), BF16),
                        pltpu.VMEM((s_keys, 256), BF16)],
        compiler_params=pltpu.CompilerParams(
            dimension_semantics=("arbitrary", "arbitrary"),
            vmem_limit_bytes=VMEM_LIMIT),
        name="gqa_attn",
    )(*args)


def _diff_kernel(*refs, n_ctx, n_lat, rope_q, lam_init, bounded):
    it = iter(refs)
    q_ref, dg_ref, kc_ref, vc_ref = next(it), next(it), next(it), next(it)
    if n_lat:
        kl_ref, vl_ref = next(it), next(it)
    if rope_q:
        cosq_ref, sinq_ref = next(it), next(it)
    if n_lat:
        cosk_ref, sink_ref = next(it), next(it)
    lam_ref, ng_ref = next(it), next(it)
    y_ref = next(it)
    lmin_ref = next(it) if bounded else None
    kt_s, vlo_s, vhi_s = next(it), next(it), next(it)
    kmax_s = next(it) if bounded else None
    s_keys = n_ctx + n_lat
    seg32 = _seg_ones(256, DIFF_QK_DIM)

    def map_norms(x):
        return jnp.sqrt(_dot((x * x).astype(BF16), seg32))

    @pl.when(pl.program_id(1) == 0)
    def _prep_kv():
        kc = kc_ref[...]
        kt_s[:, 0:n_ctx] = kc.T.astype(BF16)
        vlo_s[0:n_ctx, 0:128] = vc_ref[:, 0:128].astype(BF16)
        vhi_s[0:n_ctx, 0:128] = vc_ref[:, 128:256].astype(BF16)
        if bounded:
            kmax = jnp.max(map_norms(kc), axis=0, keepdims=True)
        if n_lat:
            kl = _rope(kl_ref[...], cosk_ref[...], sink_ref[...], 16)
            kt_s[:, n_ctx:s_keys] = kl.T.astype(BF16)
            vlo_s[n_ctx:s_keys, 0:128] = vl_ref[:, 0:128].astype(BF16)
            vhi_s[n_ctx:s_keys, 0:128] = vl_ref[:, 128:256].astype(BF16)
            if bounded:
                kmax = jnp.maximum(kmax, jnp.max(map_norms(kl), axis=0, keepdims=True))
        if bounded:
            kmax_s[...] = jnp.broadcast_to(kmax, kmax_s.shape)

    lp = lam_ref[...]
    lam = (jnp.exp(jnp.sum(lp[0:1, :] * lp[1:2, :], axis=-1, keepdims=True))
           - jnp.exp(jnp.sum(lp[2:3, :] * lp[3:4, :], axis=-1, keepdims=True))
           + lam_init)

    q = q_ref[...]
    if rope_q:
        q = _rope(q, cosq_ref[...], sinq_ref[...], 16)
    q = q * (DIFF_QK_DIM ** -0.5 * LOG2E)
    tq = q.shape[0]
    sub = min(DIFF_SUB, tq)
    lane256 = lax.broadcasted_iota(jnp.int32, (sub, 256), 1)
    lane128 = lax.broadcasted_iota(jnp.int32, (sub, 128), 1)
    seg128 = _seg_ones(128, 64)
    if bounded:
        bound_all = map_norms(q) * kmax_s[0:1, :] * SCORE_BOUND_MARGIN
    lhs_list, v_list, bounds = [], [], []
    for r0 in range(0, tq, sub):
        for mp in range(2 * DIFF_HEADS):
            in_map = (lane256 >= 32 * mp) & (lane256 < 32 * mp + 32)
            lhs_list.append(jnp.where(in_map, q[r0:r0 + sub, :], 0.0).astype(BF16))
            if bounded:
                bounds.append(bound_all[r0:r0 + sub, 32 * mp:32 * mp + 1])
        v_list += [vlo_s, vlo_s, vhi_s, vhi_s]

    if bounded:
        heads, l_min = _attend_diff_pairs_bounded(lhs_list, bounds, kt_s, v_list, lam)
        lmin_ref[...] = jnp.broadcast_to(l_min, lmin_ref.shape)
    else:
        heads = _attend_diff_pairs(lhs_list, kt_s, v_list, lam)
    for j, r0 in enumerate(range(0, tq, sub)):
        for half in range(2):
            o = jnp.where(lane128 < 64, heads[4 * j + 2 * half], heads[4 * j + 2 * half + 1])
            n = _seg_rms(o, 64, seg128) * ng_ref[:, 128 * half:128 * half + 128]
            n = n * (1.0 - lam_init)
            gate = _silu(dg_ref[r0:r0 + sub, 128 * half:128 * half + 128])
            y_ref[r0:r0 + sub, 128 * half:128 * half + 128] = (n * gate).astype(BF16)


def _diff_call(p_q, p_c, p_l, tabs, lam_params, ng, batch, n_ctx, n_lat, rope_q,
               lam_init, bounded):
    t_total = p_q.shape[0] // batch
    tq = min(DIFF_TQ, t_total)
    nq = t_total // tq
    in_specs = [
        pl.BlockSpec((tq, 256), lambda b, i: (b * nq + i, 9)),
        pl.BlockSpec((tq, 256), lambda b, i: (b * nq + i, 12)),
        pl.BlockSpec((n_ctx, 256), lambda b, i: (b, 10)),
        pl.BlockSpec((n_ctx, 256), lambda b, i: (b, 11)),
    ]
    args = [p_q, p_q, p_c, p_c]
    if n_lat:
        in_specs += [pl.BlockSpec((n_lat, 256), lambda b, i: (b, 10)),
                     pl.BlockSpec((n_lat, 256), lambda b, i: (b, 11))]
        args += [p_l, p_l]
    if rope_q:
        in_specs += [pl.BlockSpec((tq, 256), lambda b, i: (i, 0))] * 2
        args += [tabs["cos_d"], tabs["sin_d"]]
    if n_lat:
        in_specs += [pl.BlockSpec((n_lat, 256), lambda b, i: (0, 0))] * 2
        args += [tabs["cos_d"], tabs["sin_d"]]
    in_specs += [pl.BlockSpec((4, DIFF_QK_DIM), lambda b, i: (0, 0)),
                 pl.BlockSpec((1, 256), lambda b, i: (0, 0))]
    args += [lam_params, ng]
    s_keys = n_ctx + n_lat
    out_specs = [pl.BlockSpec((tq, 256), lambda b, i: (b * nq + i, 0))]
    out_shape = [jax.ShapeDtypeStruct((p_q.shape[0], 256), BF16)]
    scratch = [pltpu.VMEM((256, s_keys), BF16),
               pltpu.VMEM((s_keys, 128), BF16),
               pltpu.VMEM((s_keys, 128), BF16)]
    if bounded:
        out_specs.append(pl.BlockSpec((None, 8, 128), lambda b, i: (b * nq + i, 0, 0)))
        out_shape.append(jax.ShapeDtypeStruct((batch * nq, 8, 128), F32))
        scratch.append(pltpu.VMEM((8, 256), F32))
    outs = pl.pallas_call(
        functools.partial(_diff_kernel, n_ctx=n_ctx, n_lat=n_lat, rope_q=rope_q,
                          lam_init=lam_init, bounded=bounded),
        grid=(batch, nq),
        in_specs=in_specs,
        out_specs=out_specs,
        out_shape=out_shape,
        scratch_shapes=scratch,
        compiler_params=pltpu.CompilerParams(
            dimension_semantics=("arbitrary", "arbitrary"),
            vmem_limit_bytes=VMEM_LIMIT),
        name="diff_attn_bounded" if bounded else "diff_attn",
    )(*args)
    return (outs[0], outs[1]) if bounded else outs[0]


def _ssd_kernel(xc_ref, xl_ref, bc_ref, bl_ref, cc_ref, cl_ref, zc_ref, zl_ref,
                dtc_ref, dtl_ref, cwx_ref, cwb_ref, cwc_ref, cbx_ref, cbb_ref,
                cbc_ref, par_ref, dsk_ref, ng_ref, *rest, n_ctx, n_lat, ctx_out):
    if ctx_out:
        oc_ref, ol_ref = rest[0], rest[1]
        rest = rest[2:]
    else:
        oc_ref, ol_ref = None, rest[0]
        rest = rest[1:]
    xp_s, dtr_s, yacc_s, sloc_s, e_s, cbf_s, sin_s, dec_s, vun_s = rest

    g = pl.program_id(1)
    ncc = n_ctx // CHUNK
    ncl = n_lat // CHUNK
    nch = ncc + ncl
    GROUP = next(n for n in (6, 3, 2, 1) if nch % n == 0)
    t_all = n_ctx + n_lat
    lat0 = n_ctx + 2 * HALO

    zeros_h = jnp.zeros((HALO, 512), F32)
    xp_s[0:HALO, :] = zeros_h
    xp_s[HALO:HALO + n_ctx, 0:256] = xc_ref[...]
    xp_s[HALO:HALO + n_ctx, 256:384] = bc_ref[...]
    xp_s[HALO:HALO + n_ctx, 384:512] = cc_ref[...]
    xp_s[HALO + n_ctx:lat0, :] = zeros_h
    xp_s[lat0:lat0 + n_lat, 0:256] = xl_ref[...]
    xp_s[lat0:lat0 + n_lat, 256:384] = bl_ref[...]
    xp_s[lat0:lat0 + n_lat, 384:512] = cl_ref[...]
    xp_s[lat0 + n_lat:lat0 + n_lat + HALO, :] = zeros_h

    dtr_s[0:n_ctx, :] = dtc_ref[...]
    dtr_s[n_ctx:t_all, :] = dtl_ref[...]
    a_col = -jnp.exp(par_ref[0:16, :])
    bias_col = par_ref[16:32, :]

    def _softplus(v):
        return jnp.maximum(v, 0.0) + jnp.log1p(jnp.exp(-jnp.abs(v)))

    r128 = lax.broadcasted_iota(jnp.int32, (CHUNK, CHUNK), 0)
    c128 = lax.broadcasted_iota(jnp.int32, (CHUNK, CHUNK), 1)
    lower = c128 <= r128
    upper = c128 >= r128
    tril = jnp.where(lower, 1.0, 0.0).astype(BF16)
    triu = jnp.where(upper, 1.0, 0.0).astype(BF16)
    fwd_row = lax.broadcasted_iota(jnp.int32, (16, CHUNK), 0) < 8
    fwd_row1 = lax.broadcasted_iota(jnp.int32, (16, 1), 0) < 8
    er = lax.broadcasted_iota(jnp.int32, (CHUNK, 1024), 0)
    ec = lax.broadcasted_iota(jnp.int32, (CHUNK, 1024), 1)
    src_lane = 16 + 16 * (ec >> 9) + 8 * ((ec >> 8) & 1) + ((ec >> 6) & 3)
    expand = jnp.where(er == src_lane, 1.0, 0.0).astype(BF16)
    er0 = lax.broadcasted_iota(jnp.int32, (CHUNK, 512), 0)
    ec0 = lax.broadcasted_iota(jnp.int32, (CHUNK, 512), 1)
    expand_tot = jnp.where(er0 == 8 * (ec0 >> 8) + ((ec0 >> 6) & 3), 1.0, 0.0).astype(BF16)
    lane256 = lax.broadcasted_iota(jnp.int32, (CHUNK, 256), 1)
    lane128_1 = lax.broadcasted_iota(jnp.int32, (1, CHUNK), 1)

    cw = jnp.concatenate([cwx_ref[...], cwb_ref[...], cwc_ref[...]], axis=1)
    cb = jnp.concatenate([cbx_ref[...], cbb_ref[...], cbc_ref[...]], axis=1)
    dsk = dsk_ref[...]

    def phase_a(grp, carry):
        cs = [grp * GROUP + j for j in range(GROUP)]
        r0s = [pl.multiple_of(c * CHUNK, CHUNK) for c in cs]

        def conv(c):
            wstart = pl.multiple_of(c * CHUNK + jnp.where(c >= ncc, HALO, 0), 8)
            win = xp_s[pl.ds(wstart, CHUNK + 2 * HALO), :]
            acc = jnp.broadcast_to(cb, (CHUNK, 512))
            for k in range(CONV_K):
                d = k - CONV_K // 2
                if d == 0:
                    tap = win[HALO:HALO + CHUNK, :]
                else:
                    tap = pltpu.roll(win, (-d) % (CHUNK + 2 * HALO), 0)[HALO:HALO + CHUNK, :]
                acc = acc + cw[k:k + 1, :] * tap
            return _silu(acc)

        dtts = [_softplus(dtr_s[pl.ds(r0, CHUNK), :].T[0:16, :] + bias_col) for r0 in r0s]
        a_ts = [dtt * a_col for dtt in dtts]
        acol_ts = [jnp.where(fwd_row, _dot_exact_lhs(a_t, triu), _dot_exact_lhs(a_t, tril))
                   for a_t in a_ts]
        us = [conv(c) for c in cs]
        xus = [u[:, 0:256] for u in us]
        bts = [u[:, 256:384].T.astype(BF16) for u in us]
        cbfs = [u[:, 384:512].astype(BF16) for u in us]
        xbfs = [xu.astype(BF16) for xu in xus]
        gmats = [_dot(cbf, bt) for cbf, bt in zip(cbfs, bts)]
        tms = []
        for dtt, acol_t in zip(dtts, acol_ts):
            tot = jnp.where(fwd_row1, acol_t[:, CHUNK - 1:CHUNK], acol_t[:, 0:1])
            w_t = dtt * jnp.exp(tot - acol_t)
            e_t = jnp.exp(acol_t)
            stacked = jnp.concatenate(
                [acol_t, w_t, e_t, jnp.zeros((CHUNK - 48, CHUNK), F32)], axis=0)
            tms.append(stacked.T)
        wes = [_dot(tm.astype(BF16), expand) for tm in tms]
        decs = []
        for tm in tms:
            tot_row = jnp.where(lane128_1 < 8, tm[CHUNK - 1:CHUNK, :], tm[0:1, :])
            tot512 = _dot_exact_lhs(jnp.broadcast_to(tot_row, (8, CHUNK)), expand_tot)
            decs.append(jnp.exp(tot512))
        ydiags = [jnp.zeros((CHUNK, 256), F32) for _ in cs]
        for i in range(4):
            in_head = (lane256 >= 64 * i) & (lane256 < 64 * i + 64)
            for j in range(GROUP):
                tm, acol_t, dtt = tms[j], acol_ts[j], dtts[j]
                arg = jnp.where(lower, tm[:, i:i + 1] - acol_t[i:i + 1, :],
                                tm[:, 8 + i:9 + i] - acol_t[8 + i:9 + i, :])
                scale = (jnp.where(lower, dtt[i:i + 1, :], 0.0)
                         + jnp.where(upper, dtt[8 + i:9 + i, :], 0.0))
                wmat = (gmats[j] * jnp.exp(arg) * scale).astype(BF16)
                ydiags[j] = jnp.where(in_head, _dot(wmat, xbfs[j]), ydiags[j])
        for j, c in enumerate(cs):
            w512 = wes[j][:, 0:512]
            e512 = wes[j][:, 512:1024]
            xdw = (jnp.concatenate([xus[j], xus[j]], axis=1) * w512).astype(BF16)
            sloc_s[c] = _dot(bts[j], xdw)
            dec_s[c] = decs[j]
            e_s[c] = e512
            cbf_s[c] = cbfs[j]
            yacc_s[pl.ds(r0s[j], CHUNK), :] = ydiags[j] + xus[j] * dsk
        return carry

    lax.fori_loop(0, nch // GROUP, phase_a, 0)

    fwd_order = list(range(nch))
    bwd_order = list(range(ncc - 1, -1, -1)) + list(range(nch - 1, ncc - 1, -1))
    for order, lo in ((fwd_order, 0), (bwd_order, 256)):
        state = jnp.zeros((SSD_STATE, 256), F32)
        for c in order:
            sin_s[c, :, lo:lo + 256] = state.astype(BF16)
            state = (state * dec_s[c, 0:1, lo:lo + 256]
                     + sloc_s[c, :, lo:lo + 256])

    def phase_c(cs, z_ref, zrows):
        yos = [_dot(cbf_s[c], sin_s[c]) for c in cs]
        vs = []
        for c, zrow, yo in zip(cs, zrows, yos):
            r0 = c * CHUNK if isinstance(c, int) else pl.multiple_of(c * CHUNK, CHUNK)
            yo = yo * e_s[c]
            y = yacc_s[pl.ds(r0, CHUNK), :] + yo[:, 0:256] + yo[:, 256:512]
            vs.append((r0, y * _silu(z_ref[pl.ds(zrow, CHUNK), :])))

        @pl.when(g == 0)
        def _():
            for r0, v in vs:
                vun_s[pl.ds(r0, CHUNK), 0:256] = v

        @pl.when(g == 1)
        def _():
            for r0, v in vs:
                vun_s[pl.ds(r0, CHUNK), 256:512] = v

    if ctx_out:
        phase_c(list(range(ncc)), zc_ref, [c * CHUNK for c in range(ncc)])

    group_c = next(n for n in (4, 2, 1) if ncl % n == 0)

    def phase_c_lat(k, carry):
        ks = [k * group_c + j for j in range(group_c)]
        phase_c([kk + ncc for kk in ks], zl_ref,
                [pl.multiple_of(kk * CHUNK, CHUNK) for kk in ks])
        return carry

    lax.fori_loop(0, ncl // group_c, phase_c_lat, 0)

    @pl.when(g == SSD_GROUPS - 1)
    def _finalize():
        ng = ng_ref[...]

        def norm_rows(r0, nrows):
            v = vun_s[pl.ds(r0, nrows), :]
            ms = jnp.mean(v * v, axis=-1, keepdims=True)
            return (v * lax.rsqrt(ms + EPS) * ng).astype(BF16)

        if ctx_out:
            oc_ref[...] = norm_rows(0, n_ctx)

        def fin(k, carry):
            r0 = pl.multiple_of(k * 256, 256)
            ol_ref[pl.ds(r0, 256), :] = norm_rows(n_ctx + r0, 256)
            return carry

        lax.fori_loop(0, n_lat // 256, fin, 0)


def _ssd_call(p_c, p_l, conv_w8, conv_b, par, dsk, ng, batch, n_ctx, n_lat, ctx_out):
    nch = (n_ctx + n_lat) // CHUNK
    t_all = n_ctx + n_lat
    in_specs = [
        pl.BlockSpec((n_ctx, 256), lambda b, g: (b, g)),
        pl.BlockSpec((n_lat, 256), lambda b, g: (b, g)),
        pl.BlockSpec((n_ctx, 128), lambda b, g: (b, 4 + g)),
        pl.BlockSpec((n_lat, 128), lambda b, g: (b, 4 + g)),
        pl.BlockSpec((n_ctx, 128), lambda b, g: (b, 6 + g)),
        pl.BlockSpec((n_lat, 128), lambda b, g: (b, 6 + g)),
        pl.BlockSpec((n_ctx, 256), lambda b, g: (b, 4 + g)),
        pl.BlockSpec((n_lat, 256), lambda b, g: (b, 4 + g)),
        pl.BlockSpec((n_ctx, 128), lambda b, g: (b, 26 + g)),
        pl.BlockSpec((n_lat, 128), lambda b, g: (b, 26 + g)),
        pl.BlockSpec((8, 256), lambda b, g: (0, g)),
        pl.BlockSpec((8, 128), lambda b, g: (0, 4 + g)),
        pl.BlockSpec((8, 128), lambda b, g: (0, 6 + g)),
        pl.BlockSpec((1, 256), lambda b, g: (0, g)),
        pl.BlockSpec((1, 128), lambda b, g: (0, 4 + g)),
        pl.BlockSpec((1, 128), lambda b, g: (0, 6 + g)),
        pl.BlockSpec((None, 32, 128), lambda b, g: (g, 0, 0)),
        pl.BlockSpec((1, 256), lambda b, g: (0, g)),
        pl.BlockSpec((1, 512), lambda b, g: (0, 0)),
    ]
    args = [p_c, p_l, p_c, p_l, p_c, p_l, p_c, p_l, p_c, p_l,
            conv_w8, conv_w8, conv_w8, conv_b, conv_b, conv_b, par, dsk, ng]
    out_specs = [pl.BlockSpec((n_lat, 512), lambda b, g: (b, 0))]
    out_shape = [jax.ShapeDtypeStruct((batch * n_lat, 512), BF16)]
    if ctx_out:
        out_specs = [pl.BlockSpec((n_ctx, 512), lambda b, g: (b, 0))] + out_specs
        out_shape = [jax.ShapeDtypeStruct((batch * n_ctx, 512), BF16)] + out_shape
    scratch = [
        pltpu.VMEM((t_all + 3 * HALO, 512), F32),
        pltpu.VMEM((t_all, 128), F32),
        pltpu.VMEM((t_all, 256), F32),
        pltpu.VMEM((nch, SSD_STATE, 512), F32),
        pltpu.VMEM((nch, CHUNK, 512), F32),
        pltpu.VMEM((nch, CHUNK, 128), BF16),
        pltpu.VMEM((nch, SSD_STATE, 512), BF16),
        pltpu.VMEM((nch, 8, 512), F32),
        pltpu.VMEM((t_all, 512), F32),
    ]
    outs = pl.pallas_call(
        functools.partial(_ssd_kernel, n_ctx=n_ctx, n_lat=n_lat, ctx_out=ctx_out),
        grid=(batch, SSD_GROUPS),
        in_specs=in_specs,
        out_specs=out_specs,
        out_shape=out_shape,
        scratch_shapes=scratch,
        compiler_params=pltpu.CompilerParams(
            dimension_semantics=("arbitrary", "arbitrary"),
            vmem_limit_bytes=VMEM_LIMIT),
        name="ssd_scan",
    )(*args)
    if ctx_out:
        return outs[1], outs[0]
    return outs[0], None


def _rope_tables(n_lat):
    rows = n_lat // GRID_W
    row_idx = np.repeat(np.arange(rows), GRID_W).astype(np.float32)
    col_idx = (np.arange(rows * GRID_W) % GRID_W).astype(np.float32)

    def tables(dim, reps):
        quarter = dim // 4
        inv = (ROPE_BASE ** (-np.arange(quarter, dtype=np.float32) / quarter)).astype(np.float32)
        ang = np.concatenate([row_idx[:, None] * inv, col_idx[:, None] * inv], axis=-1)
        cos, sin = np.cos(ang.astype(np.float64)), np.sin(ang.astype(np.float64))
        cos2 = np.concatenate([cos, cos], axis=-1).astype(np.float32)
        sin2 = np.concatenate([-sin, sin], axis=-1).astype(np.float32)
        return jnp.asarray(np.tile(cos2, (1, reps))), jnp.asarray(np.tile(sin2, (1, reps)))

    cos_g, sin_g = tables(GQA_HEAD_DIM, 4)
    cos_d, sin_d = tables(DIFF_QK_DIM, 8)
    return {"cos_g": cos_g, "sin_g": sin_g, "cos_d": cos_d, "sin_d": sin_d}


def kernel(x, c, ctx, c_ctx, w_mod, b_mod, g_pre, g_post, w_in, conv_w, conv_b,
           a_log_fwd, a_log_bwd, dt_bias_fwd, dt_bias_bwd, d_skip, ssd_norm_g,
           q_norm_g, k_norm_g, diff_lambda, diff_norm_g, w_out):
    batch, n_lat, _ = x.shape
    n_ctx = ctx.shape[1]
    depth = w_mod.shape[0]
    assert n_lat % 512 == 0 and n_ctx % 256 == 0 and (batch * n_ctx) % 512 == 0
    assert batch + 1 <= 16

    in_perm, out_perm = _in_col_perm(), _out_row_perm()
    w_in_p = [_take_runs(w_in[l], in_perm, 1, IN_COLS).astype(BF16) for l in range(depth)]
    w_out_p = [_take_runs(w_out[l], out_perm, 0, None).astype(BF16) for l in range(depth)]
    conv_w8 = jnp.pad(conv_w, ((0, 0), (0, 8 - CONV_K), (0, 0)))
    conv_b1 = conv_b[:, None, :]

    def group16(fwd, bwd):
        out = jnp.zeros((depth, SSD_GROUPS, 16), F32)
        for g in range(SSD_GROUPS):
            out = out.at[:, g, 0:4].set(fwd[:, 4 * g:4 * g + 4])
            out = out.at[:, g, 8:12].set(bwd[:, 4 * g:4 * g + 4])
        return out

    ssd_par = jnp.broadcast_to(
        jnp.concatenate([group16(a_log_fwd, a_log_bwd),
                         group16(dt_bias_fwd, dt_bias_bwd)], axis=-1)[..., None],
        (depth, SSD_GROUPS, 32, 128))
    dsk = jnp.repeat(d_skip, SSD_HEAD_DIM, axis=1)[:, None, :]
    qg = jnp.tile(q_norm_g, (1, 4))[:, None, :]
    kg = jnp.tile(k_norm_g, (1, 2))[:, None, :]
    dng = jnp.tile(diff_norm_g, (1, 4))[:, None, :]
    tabs = _rope_tables(n_lat)

    cs = jnp.concatenate(
        [c, c_ctx[None, :], jnp.zeros((16 - batch - 1, D_MODEL), F32)], axis=0)
    mod_all = _mod_call(cs, w_mod, b_mod)

    h = x.reshape(batch * n_lat, D_MODEL)
    hc = ctx.reshape(batch * n_ctx, D_MODEL)
    for l in range(depth):
        ctx_out = l < depth - 1
        lam_init = 0.8 - 0.6 * float(np.exp(-0.3 * l))
        mod3 = mod_all[l][:, None, :]
        p_l = _inproj_call(h, mod3, g_pre[l][None, :], w_in_p[l], n_lat, None)
        p_c = _inproj_call(hc, mod3, g_pre[l][None, :], w_in_p[l], None, batch)

        ys_l, ys_c = _ssd_call(p_c, p_l, conv_w8[l], conv_b1[l], ssd_par[l], dsk[l],
                               ssd_norm_g[l][None, :], batch, n_ctx, n_lat, ctx_out)
        yg_l = _gqa_call(p_l, p_c, p_l, tabs, qg[l], kg[l], batch, n_ctx, n_lat, True)
        diff_args = (p_l, p_c, p_l, tabs, diff_lambda[l], dng[l], batch, n_ctx, n_lat, True,
                     lam_init)
        yd_fast, row_sum_min = _diff_call(*diff_args, True)
        yd_l = lax.cond(jnp.min(row_sum_min) >= MIN_ROW_SUM,
                        lambda: yd_fast, lambda: _diff_call(*diff_args, False))
        h = _outproj_call(ys_l, yg_l, yd_l, h, mod3, g_post[l][None, :], w_out_p[l],
                          n_lat, None)
        if ctx_out:
            yg_c = _gqa_call(p_c, p_c, None, tabs, qg[l], kg[l], batch, n_ctx, 0, False)
            yd_c = _diff_call(p_c, p_c, None, tabs, diff_lambda[l], dng[l], batch,
                              n_ctx, 0, False, lam_init, False)
            hc = _outproj_call(ys_c, yg_c, yd_c, hc, mod3, g_post[l][None, :],
                               w_out_p[l], None, batch)
    return h.reshape(batch, n_lat, D_MODEL)
```

```python
import functools

import numpy as np
import jax
import jax.numpy as jnp
from jax import lax
from jax.experimental import pallas as pl
from jax.experimental.pallas import tpu as pltpu

F32 = jnp.float32
BF16 = jnp.bfloat16

D_MODEL = 1024
GRID_W = 64
ROPE_BASE = 10000.0
EPS = 1e-6
LOG2E = 1.4426950408889634

SSD_WIDTH = 512
SSD_HEADS = 8
SSD_HEAD_DIM = 64
SSD_GROUPS = 2
SSD_STATE = 128
CHUNK = 128
CONV_K = 5
HALO = 8
GQA_HEADS = 4
GQA_HEAD_DIM = 64
DIFF_HEADS = 4
DIFF_QK_DIM = 32
DIFF_V_DIM = 64

_IN_SPLITS = (("xbc", 1024), ("z", 512), ("dt", 16), ("gq", 256), ("gk", 128),
              ("gv", 128), ("gg", 256), ("dq", 256), ("dk", 256), ("dv", 256),
              ("dg", 256))
IN_COLS = sum(s for _, s in _IN_SPLITS)
NP = 28 * 128
GQ_HEAD_ORDER = (0, 2, 1, 3)

VMEM_LIMIT = 56 * 1024 * 1024
ATTN_TQ = 1024
DIFF_TQ = 1024
ATTN_SUB = 512
DIFF_SUB = 512
ATTN_AHEAD = 3
DIFF_AHEAD = 1
SCORE_BOUND_MARGIN = 1.02
MIN_ROW_SUM = 2.0 ** -90
OUTPROJ_TM = 1024
OUTPROJ_SUB = 256


def _in_col_perm():
    off, o = {}, 0
    for name, size in _IN_SPLITS:
        off[name] = o
        o += size
    pad = IN_COLS
    cols = list(range(off["xbc"], off["xbc"] + 1024))
    cols += list(range(off["z"], off["z"] + 512))
    for h in GQ_HEAD_ORDER:
        cols += list(range(off["gq"] + 64 * h, off["gq"] + 64 * h + 64))
    cols += list(range(off["gk"], off["gk"] + 128))
    cols += list(range(off["gv"], off["gv"] + 128))
    for h in GQ_HEAD_ORDER:
        cols += list(range(off["gg"] + 64 * h, off["gg"] + 64 * h + 64))
    for name in ("dq", "dk", "dv", "dg"):
        cols += list(range(off[name], off[name] + 256))
    for g in range(SSD_GROUPS):
        blk = [pad] * 128
        for i in range(4):
            blk[i] = off["dt"] + 4 * g + i
            blk[8 + i] = off["dt"] + SSD_HEADS + 4 * g + i
        cols += blk
    assert len(cols) == NP
    return np.asarray(cols, np.int32)


def _out_row_perm():
    rows = list(range(SSD_WIDTH))
    for h in GQ_HEAD_ORDER:
        rows += list(range(SSD_WIDTH + 64 * h, SSD_WIDTH + 64 * h + 64))
    rows += list(range(SSD_WIDTH + 256, SSD_WIDTH + 512))
    return np.asarray(rows, np.int32)


def _take_runs(arr, idx, axis, pad_index):
    pieces, start = [], 0
    idx = [int(i) for i in idx]
    while start < len(idx):
        end = start + 1
        if idx[start] == pad_index:
            while end < len(idx) and idx[end] == pad_index:
                end += 1
            shape = list(arr.shape)
            shape[axis] = end - start
            pieces.append(jnp.zeros(shape, arr.dtype))
        else:
            while end < len(idx) and idx[end] == idx[end - 1] + 1 and idx[end] != pad_index:
                end += 1
            pieces.append(lax.slice_in_dim(arr, idx[start], idx[end - 1] + 1, axis=axis))
        start = end
    return jnp.concatenate(pieces, axis=axis)


def _dot(a, b):
    return jnp.dot(a, b, preferred_element_type=F32)


def _split3(x):
    hi = x.astype(BF16)
    r1 = x - hi.astype(F32)
    mid = r1.astype(BF16)
    lo = (r1 - mid.astype(F32)).astype(BF16)
    return hi, mid, lo


def _dot_exact_lhs(x, m_bf16):
    hi, mid, lo = _split3(x)
    return _dot(hi, m_bf16) + _dot(mid, m_bf16) + _dot(lo, m_bf16)


def _silu(x):
    return x * jax.nn.sigmoid(x)


def _seg_ones(width, seg):
    r = lax.broadcasted_iota(jnp.int32, (width, width), 0)
    c = lax.broadcasted_iota(jnp.int32, (width, width), 1)
    same = (r & ~(seg - 1)) == (c & ~(seg - 1))
    return jnp.where(same, 1.0, 0.0).astype(BF16)


def _seg_rms(x, seg, seg_mat):
    ss = _dot_exact_lhs(x * x, seg_mat)
    return x * lax.rsqrt(ss * (1.0 / seg) + EPS)


def _rope(x, cos, sin_signed, half):
    w = x.shape[-1]
    lane = lax.broadcasted_iota(jnp.int32, x.shape, 1)
    first = (lane & (2 * half - 1)) < half
    swapped = jnp.where(first, pltpu.roll(x, w - half, 1), pltpu.roll(x, half, 1))
    return x * cos + swapped * sin_signed


def _mod_kernel(cs_ref, w_ref, b_ref, o_ref):
    s = _silu(cs_ref[...]).astype(BF16)
    o_ref[...] = _dot(s, w_ref[...].astype(BF16)) + b_ref[...]


def _mod_call(cs, w_mod, b_mod):
    depth = w_mod.shape[0]
    nrow = cs.shape[0]
    tn = 1024
    return pl.pallas_call(
        _mod_kernel,
        grid=(depth, 3 * D_MODEL // tn),
        in_specs=[
            pl.BlockSpec((nrow, D_MODEL), lambda l, j: (0, 0)),
            pl.BlockSpec((None, D_MODEL, tn), lambda l, j: (l, 0, j)),
            pl.BlockSpec((None, 1, tn), lambda l, j: (l, 0, j)),
        ],
        out_specs=pl.BlockSpec((None, nrow, tn), lambda l, j: (l, 0, j)),
        out_shape=jax.ShapeDtypeStruct((depth, nrow, 3 * D_MODEL), F32),
        compiler_params=pltpu.CompilerParams(
            dimension_semantics=("arbitrary", "arbitrary")),
        name="mod_proj",
    )(cs, w_mod, b_mod.reshape(depth, 1, 3 * D_MODEL))


def _inproj_kernel(h_ref, mod_ref, g_ref, w_ref, o_ref):
    x = h_ref[...]
    ms = jnp.mean(x * x, axis=-1, keepdims=True)
    y = x * lax.rsqrt(ms + EPS) * g_ref[...]
    sh = mod_ref[:, 0:D_MODEL]
    sc = mod_ref[:, D_MODEL:2 * D_MODEL]
    u = (y * (1.0 + sc) + sh).astype(BF16)
    tn = 512
    for j in range(NP // tn):
        o_ref[:, j * tn:(j + 1) * tn] = _dot(u, w_ref[:, j * tn:(j + 1) * tn])


def _inproj_call(h, mod3, g_pre, w_bf16, tiles_per_row, fixed_row):
    n_tok = h.shape[0]
    tm = 512
    if fixed_row is None:
        mod_idx = lambda i: (i // tiles_per_row, 0, 0)
    else:
        mod_idx = lambda i: (fixed_row, 0, 0)
    return pl.pallas_call(
        _inproj_kernel,
        grid=(n_tok // tm,),
        in_specs=[
            pl.BlockSpec((tm, D_MODEL), lambda i: (i, 0)),
            pl.BlockSpec((None, 1, 3 * D_MODEL), mod_idx),
            pl.BlockSpec((1, D_MODEL), lambda i: (0, 0)),
            pl.BlockSpec((D_MODEL, NP), lambda i: (0, 0)),
        ],
        out_specs=pl.BlockSpec((tm, NP), lambda i: (i, 0)),
        out_shape=jax.ShapeDtypeStruct((n_tok, NP), F32),
        compiler_params=pltpu.CompilerParams(
            dimension_semantics=("arbitrary",), vmem_limit_bytes=VMEM_LIMIT),
        name="in_proj",
    )(h, mod3, g_pre, w_bf16)


def _outproj_kernel(ys_ref, yg_ref, yd_ref, h_ref, mod_ref, g_ref, w_ref, o_ref):
    tm = h_ref.shape[0]
    sub = min(OUTPROJ_SUB, tm)
    gt = mod_ref[:, 2 * D_MODEL:3 * D_MODEL]
    gain = g_ref[...]

    def project(r0):
        return (_dot(ys_ref[r0:r0 + sub, :], w_ref[0:512, :])
                + _dot(yg_ref[r0:r0 + sub, :], w_ref[512:768, :])
                + _dot(yd_ref[r0:r0 + sub, :], w_ref[768:1024, :]))

    o_next = project(0)
    for r0 in range(0, tm, sub):
        o = o_next
        if r0 + sub < tm:
            o_next = project(r0 + sub)
        ms = jnp.mean(o * o, axis=-1, keepdims=True)
        n = o * lax.rsqrt(ms + EPS) * gain
        o_ref[r0:r0 + sub, :] = h_ref[r0:r0 + sub, :] + gt * n


def _outproj_call(ys, yg, yd, h, mod3, g_post, w_bf16, rows_per_mod, fixed_row):
    n_tok = h.shape[0]
    tm = OUTPROJ_TM
    assert n_tok % tm == 0
    if fixed_row is None:
        assert rows_per_mod % tm == 0
        mod_idx = lambda i: (i // (rows_per_mod // tm), 0, 0)
    else:
        mod_idx = lambda i: (fixed_row, 0, 0)
    return pl.pallas_call(
        _outproj_kernel,
        grid=(n_tok // tm,),
        in_specs=[
            pl.BlockSpec((tm, 512), lambda i: (i, 0)),
            pl.BlockSpec((tm, 256), lambda i: (i, 0)),
            pl.BlockSpec((tm, 256), lambda i: (i, 0)),
            pl.BlockSpec((tm, D_MODEL), lambda i: (i, 0)),
            pl.BlockSpec((None, 1, 3 * D_MODEL), mod_idx),
            pl.BlockSpec((1, D_MODEL), lambda i: (0, 0)),
            pl.BlockSpec((D_MODEL, D_MODEL), lambda i: (0, 0)),
        ],
        out_specs=pl.BlockSpec((tm, D_MODEL), lambda i: (i, 0)),
        out_shape=jax.ShapeDtypeStruct((n_tok, D_MODEL), F32),
        compiler_params=pltpu.CompilerParams(
            dimension_semantics=("arbitrary",), vmem_limit_bytes=VMEM_LIMIT),
        name="out_proj",
    )(ys, yg, yd, h, mod3, g_post, w_bf16)


def _attend_many(lhs_list, kt_ref, vext_refs):
    def scores(i):
        s = _dot(lhs_list[i], kt_ref[...])
        return s, jnp.max(s, axis=-1, keepdims=True)

    outs = []
    n = len(lhs_list)
    ahead = [scores(i) for i in range(min(ATTN_AHEAD, n))]
    for i, vext_ref in enumerate(vext_refs):
        s, m = ahead.pop(0)
        if i + ATTN_AHEAD < n:
            ahead.append(scores(i + ATTN_AHEAD))
        p = jnp.exp2(s - m).astype(BF16)
        oe = _dot(p, vext_ref[...])
        outs.append(oe[:, 0:128] / oe[:, 128:256])
    return outs


def _attend_diff_pairs(lhs_list, kt_ref, v_refs, lam):
    n_heads = len(v_refs)

    def scores(h):
        s_a = _dot(lhs_list[2 * h], kt_ref[...])
        m_a = jnp.max(s_a, axis=-1, keepdims=True)
        s_b = _dot(lhs_list[2 * h + 1], kt_ref[...])
        m_b = jnp.max(s_b, axis=-1, keepdims=True)
        return s_a, m_a, s_b, m_b

    outs = []
    ahead = [scores(h) for h in range(min(DIFF_AHEAD, n_heads))]
    for h in range(n_heads):
        s_a, m_a, s_b, m_b = ahead.pop(0)
        if h + DIFF_AHEAD < n_heads:
            ahead.append(scores(h + DIFF_AHEAD))
        e_a = jnp.exp2(s_a - m_a)
        e_b = jnp.exp2(s_b - m_b)
        l_a = jnp.sum(e_a, axis=-1, keepdims=True)
        l_b = jnp.sum(e_b, axis=-1, keepdims=True)
        pc = (e_a - (lam * l_a / l_b) * e_b).astype(BF16)
        outs.append(_dot(pc, v_refs[h][...]) / l_a)
    return outs


def _attend_diff_pairs_bounded(lhs_list, bounds, kt_ref, v_refs, lam):
    n_heads = len(v_refs)

    def exps(h):
        e_a = jnp.exp2(_dot(lhs_list[2 * h], kt_ref[...]) - bounds[2 * h])
        e_b = jnp.exp2(_dot(lhs_list[2 * h + 1], kt_ref[...]) - bounds[2 * h + 1])
        return e_a, e_b

    outs, l_min = [], None
    ahead = [exps(h) for h in range(min(DIFF_AHEAD, n_heads))]
    for h in range(n_heads):
        e_a, e_b = ahead.pop(0)
        if h + DIFF_AHEAD < n_heads:
            ahead.append(exps(h + DIFF_AHEAD))
        l_a = jnp.sum(e_a, axis=-1, keepdims=True)
        l_b = jnp.sum(e_b, axis=-1, keepdims=True)
        pc = (e_a - (lam * l_a / l_b) * e_b).astype(BF16)
        outs.append(_dot(pc, v_refs[h][...]) / l_a)
        l_ab = jnp.minimum(l_a, l_b)
        l_min = l_ab if l_min is None else jnp.minimum(l_min, l_ab)
    return outs, jnp.min(l_min, axis=0, keepdims=True)


def _gqa_kernel(*refs, n_ctx, n_lat, rope_q):
    it = iter(refs)
    q_ref, gg_ref, kvc_ref = next(it), next(it), next(it)
    kvl_ref = next(it) if n_lat else None
    if rope_q:
        cosq_ref, sinq_ref = next(it), next(it)
    if n_lat:
        cosk_ref, sink_ref = next(it), next(it)
    qg_ref, kg_ref = next(it), next(it)
    y_ref = next(it)
    kt_s, vext_s = next(it), next(it)

    seg128 = _seg_ones(128, 64)

    @pl.when(pl.program_id(1) == 0)
    def _prep_kv():
        kc = _seg_rms(kvc_ref[:, 0:128], 64, seg128) * kg_ref[...]
        kt_s[:, 0:n_ctx] = kc.T.astype(BF16)
        vext_s[0:n_ctx, 0:128] = kvc_ref[:, 128:256].astype(BF16)
        if n_lat:
            kl = _seg_rms(kvl_ref[:, 0:128], 64, seg128) * kg_ref[...]
            kl = _rope(kl, cosk_ref[...], sink_ref[...], 32)
            kt_s[:, n_ctx:n_ctx + n_lat] = kl.T.astype(BF16)
            vext_s[n_ctx:n_ctx + n_lat, 0:128] = kvl_ref[:, 128:256].astype(BF16)
        vext_s[:, 128:256] = jnp.ones((n_ctx + n_lat, 128), BF16)

    seg256 = _seg_ones(256, 64)
    q = _seg_rms(q_ref[...], 64, seg256) * qg_ref[...]
    if rope_q:
        q = _rope(q, cosq_ref[...], sinq_ref[...], 32)
    q = q * (GQA_HEAD_DIM ** -0.5 * LOG2E)
    tq = q.shape[0]
    sub = min(ATTN_SUB, tq)
    lane = lax.broadcasted_iota(jnp.int32, (sub, 128), 1)
    lhs_list = []
    for r0 in range(0, tq, sub):
        for half in range(2):
            qh = q[r0:r0 + sub, 128 * half:128 * half + 128]
            for kv in range(2):
                in_kv = (lane >= 64 * kv) & (lane < 64 * kv + 64)
                lhs_list.append(jnp.where(in_kv, qh, 0.0).astype(BF16))
    outs = _attend_many(lhs_list, kt_s, [vext_s] * len(lhs_list))
    for j, r0 in enumerate(range(0, tq, sub)):
        for half in range(2):
            o = jnp.where(lane < 64, outs[4 * j + 2 * half], outs[4 * j + 2 * half + 1])
            gate = _silu(gg_ref[r0:r0 + sub, 128 * half:128 * half + 128])
            y_ref[r0:r0 + sub, 128 * half:128 * half + 128] = (o * gate).astype(BF16)


def _gqa_call(p_q, p_c, p_l, tabs, qg, kg, batch, n_ctx, n_lat, rope_q):
    t_total = p_q.shape[0] // batch
    tq = min(ATTN_TQ, t_total)
    nq = t_total // tq
    in_specs = [
        pl.BlockSpec((tq, 256), lambda b, i: (b * nq + i, 6)),
        pl.BlockSpec((tq, 256), lambda b, i: (b * nq + i, 8)),
        pl.BlockSpec((n_ctx, 256), lambda b, i: (b, 7)),
    ]
    args = [p_q, p_q, p_c]
    if n_lat:
        in_specs.append(pl.BlockSpec((n_lat, 256), lambda b, i: (b, 7)))
        args.append(p_l)
    if rope_q:
        in_specs += [pl.BlockSpec((tq, 256), lambda b, i: (i, 0))] * 2
        args += [tabs["cos_g"], tabs["sin_g"]]
    if n_lat:
        in_specs += [pl.BlockSpec((n_lat, 128), lambda b, i: (0, 0))] * 2
        args += [tabs["cos_g"], tabs["sin_g"]]
    in_specs += [pl.BlockSpec((1, 256), lambda b, i: (0, 0)),
                 pl.BlockSpec((1, 128), lambda b, i: (0, 0))]
    args += [qg, kg]
    s_keys = n_ctx + n_lat
    return pl.pallas_call(
        functools.partial(_gqa_kernel, n_ctx=n_ctx, n_lat=n_lat, rope_q=rope_q),
        grid=(batch, nq),
        in_specs=in_specs,
        out_specs=pl.BlockSpec((tq, 256), lambda b, i: (b * nq + i, 0)),
        out_shape=jax.ShapeDtypeStruct((p_q.shape[0], 256), BF16),
        scratch_shapes=[pltpu.VMEM((128, s_keys), BF16),
                        pltpu.VMEM((s_keys, 256), BF16)],
        compiler_params=pltpu.CompilerParams(
            dimension_semantics=("arbitrary", "arbitrary"),
            vmem_limit_bytes=VMEM_LIMIT),
        name="gqa_attn",
    )(*args)


def _diff_kernel(*refs, n_ctx, n_lat, rope_q, lam_init, bounded):
    it = iter(refs)
    q_ref, dg_ref, kc_ref, vc_ref = next(it), next(it), next(it), next(it)
    if n_lat:
        kl_ref, vl_ref = next(it), next(it)
    if rope_q:
        cosq_ref, sinq_ref = next(it), next(it)
    if n_lat:
        cosk_ref, sink_ref = next(it), next(it)
    lam_ref, ng_ref = next(it), next(it)
    y_ref = next(it)
    lmin_ref = next(it) if bounded else None
    kt_s, vlo_s, vhi_s = next(it), next(it), next(it)
    kmax_s = next(it) if bounded else None
    s_keys = n_ctx + n_lat
    seg32 = _seg_ones(256, DIFF_QK_DIM)

    def map_norms(x):
        return jnp.sqrt(_dot((x * x).astype(BF16), seg32))

    @pl.when(pl.program_id(1) == 0)
    def _prep_kv():
        kc = kc_ref[...]
        kt_s[:, 0:n_ctx] = kc.T.astype(BF16)
        vlo_s[0:n_ctx, 0:128] = vc_ref[:, 0:128].astype(BF16)
        vhi_s[0:n_ctx, 0:128] = vc_ref[:, 128:256].astype(BF16)
        if bounded:
            kmax = jnp.max(map_norms(kc), axis=0, keepdims=True)
        if n_lat:
            kl = _rope(kl_ref[...], cosk_ref[...], sink_ref[...], 16)
            kt_s[:, n_ctx:s_keys] = kl.T.astype(BF16)
            vlo_s[n_ctx:s_keys, 0:128] = vl_ref[:, 0:128].astype(BF16)
            vhi_s[n_ctx:s_keys, 0:128] = vl_ref[:, 128:256].astype(BF16)
            if bounded:
                kmax = jnp.maximum(kmax, jnp.max(map_norms(kl), axis=0, keepdims=True))
        if bounded:
            kmax_s[...] = jnp.broadcast_to(kmax, kmax_s.shape)

    lp = lam_ref[...]
    lam = (jnp.exp(jnp.sum(lp[0:1, :] * lp[1:2, :], axis=-1, keepdims=True))
           - jnp.exp(jnp.sum(lp[2:3, :] * lp[3:4, :], axis=-1, keepdims=True))
           + lam_init)

    q = q_ref[...]
    if rope_q:
        q = _rope(q, cosq_ref[...], sinq_ref[...], 16)
    q = q * (DIFF_QK_DIM ** -0.5 * LOG2E)
    tq = q.shape[0]
    sub = min(DIFF_SUB, tq)
    lane256 = lax.broadcasted_iota(jnp.int32, (sub, 256), 1)
    lane128 = lax.broadcasted_iota(jnp.int32, (sub, 128), 1)
    seg128 = _seg_ones(128, 64)
    if bounded:
        bound_all = map_norms(q) * kmax_s[0:1, :] * SCORE_BOUND_MARGIN
    lhs_list, v_list, bounds = [], [], []
    for r0 in range(0, tq, sub):
        for mp in range(2 * DIFF_HEADS):
            in_map = (lane256 >= 32 * mp) & (lane256 < 32 * mp + 32)
            lhs_list.append(jnp.where(in_map, q[r0:r0 + sub, :], 0.0).astype(BF16))
            if bounded:
                bounds.append(bound_all[r0:r0 + sub, 32 * mp:32 * mp + 1])
        v_list += [vlo_s, vlo_s, vhi_s, vhi_s]

    if bounded:
        heads, l_min = _attend_diff_pairs_bounded(lhs_list, bounds, kt_s, v_list, lam)
        lmin_ref[...] = jnp.broadcast_to(l_min, lmin_ref.shape)
    else:
        heads = _attend_diff_pairs(lhs_list, kt_s, v_list, lam)
    for j, r0 in enumerate(range(0, tq, sub)):
        for half in range(2):
            o = jnp.where(lane128 < 64, heads[4 * j + 2 * half], heads[4 * j + 2 * half + 1])
            n = _seg_rms(o, 64, seg128) * ng_ref[:, 128 * half:128 * half + 128]
            n = n * (1.0 - lam_init)
            gate = _silu(dg_ref[r0:r0 + sub, 128 * half:128 * half + 128])
            y_ref[r0:r0 + sub, 128 * half:128 * half + 128] = (n * gate).astype(BF16)


def _diff_call(p_q, p_c, p_l, tabs, lam_params, ng, batch, n_ctx, n_lat, rope_q,
               lam_init, bounded):
    t_total = p_q.shape[0] // batch
    tq = min(DIFF_TQ, t_total)
    nq = t_total // tq
    in_specs = [
        pl.BlockSpec((tq, 256), lambda b, i: (b * nq + i, 9)),
        pl.BlockSpec((tq, 256), lambda b, i: (b * nq + i, 12)),
        pl.BlockSpec((n_ctx, 256), lambda b, i: (b, 10)),
        pl.BlockSpec((n_ctx, 256), lambda b, i: (b, 11)),
    ]
    args = [p_q, p_q, p_c, p_c]
    if n_lat:
        in_specs += [pl.BlockSpec((n_lat, 256), lambda b, i: (b, 10)),
                     pl.BlockSpec((n_lat, 256), lambda b, i: (b, 11))]
        args += [p_l, p_l]
    if rope_q:
        in_specs += [pl.BlockSpec((tq, 256), lambda b, i: (i, 0))] * 2
        args += [tabs["cos_d"], tabs["sin_d"]]
    if n_lat:
        in_specs += [pl.BlockSpec((n_lat, 256), lambda b, i: (0, 0))] * 2
        args += [tabs["cos_d"], tabs["sin_d"]]
    in_specs += [pl.BlockSpec((4, DIFF_QK_DIM), lambda b, i: (0, 0)),
                 pl.BlockSpec((1, 256), lambda b, i: (0, 0))]
    args += [lam_params, ng]
    s_keys = n_ctx + n_lat
    out_specs = [pl.BlockSpec((tq, 256), lambda b, i: (b * nq + i, 0))]
    out_shape = [jax.ShapeDtypeStruct((p_q.shape[0], 256), BF16)]
    scratch = [pltpu.VMEM((256, s_keys), BF16),
               pltpu.VMEM((s_keys, 128), BF16),
               pltpu.VMEM((s_keys, 128), BF16)]
    if bounded:
        out_specs.append(pl.BlockSpec((None, 8, 128), lambda b, i: (b * nq + i, 0, 0)))
        out_shape.append(jax.ShapeDtypeStruct((batch * nq, 8, 128), F32))
        scratch.append(pltpu.VMEM((8, 256), F32))
    outs = pl.pallas_call(
        functools.partial(_diff_kernel, n_ctx=n_ctx, n_lat=n_lat, rope_q=rope_q,
                          lam_init=lam_init, bounded=bounded),
        grid=(batch, nq),
        in_specs=in_specs,
        out_specs=out_specs,
        out_shape=out_shape,
        scratch_shapes=scratch,
        compiler_params=pltpu.CompilerParams(
            dimension_semantics=("arbitrary", "arbitrary"),
            vmem_limit_bytes=VMEM_LIMIT),
        name="diff_attn_bounded" if bounded else "diff_attn",
    )(*args)
    return (outs[0], outs[1]) if bounded else outs[0]


def _ssd_kernel(xc_ref, xl_ref, bc_ref, bl_ref, cc_ref, cl_ref, zc_ref, zl_ref,
                dtc_ref, dtl_ref, cwx_ref, cwb_ref, cwc_ref, cbx_ref, cbb_ref,
                cbc_ref, par_ref, dsk_ref, ng_ref, *rest, n_ctx, n_lat, ctx_out):
    if ctx_out:
        oc_ref, ol_ref = rest[0], rest[1]
        rest = rest[2:]
    else:
        oc_ref, ol_ref = None, rest[0]
        rest = rest[1:]
    xp_s, dtr_s, yacc_s, sloc_s, e_s, cbf_s, sin_s, dec_s, vun_s = rest

    g = pl.program_id(1)
    ncc = n_ctx // CHUNK
    ncl = n_lat // CHUNK
    nch = ncc + ncl
    GROUP = next(n for n in (9, 6, 3, 2, 1) if nch % n == 0)
    t_all = n_ctx + n_lat
    lat0 = n_ctx + 2 * HALO

    zeros_h = jnp.zeros((HALO, 512), F32)
    xp_s[0:HALO, :] = zeros_h
    xp_s[HALO:HALO + n_ctx, 0:256] = xc_ref[...]
    xp_s[HALO:HALO + n_ctx, 256:384] = bc_ref[...]
    xp_s[HALO:HALO + n_ctx, 384:512] = cc_ref[...]
    xp_s[HALO + n_ctx:lat0, :] = zeros_h
    xp_s[lat0:lat0 + n_lat, 0:256] = xl_ref[...]
    xp_s[lat0:lat0 + n_lat, 256:384] = bl_ref[...]
    xp_s[lat0:lat0 + n_lat, 384:512] = cl_ref[...]
    xp_s[lat0 + n_lat:lat0 + n_lat + HALO, :] = zeros_h

    dtr_s[0:n_ctx, :] = dtc_ref[...]
    dtr_s[n_ctx:t_all, :] = dtl_ref[...]
    a_col = -jnp.exp(par_ref[0:16, :])
    bias_col = par_ref[16:32, :]

    def _softplus(v):
        return jnp.maximum(v, 0.0) + jnp.log1p(jnp.exp(-jnp.abs(v)))

    r128 = lax.broadcasted_iota(jnp.int32, (CHUNK, CHUNK), 0)
    c128 = lax.broadcasted_iota(jnp.int32, (CHUNK, CHUNK), 1)
    lower = c128 <= r128
    upper = c128 >= r128
    tril = jnp.where(lower, 1.0, 0.0).astype(BF16)
    triu = jnp.where(upper, 1.0, 0.0).astype(BF16)
    fwd_row = lax.broadcasted_iota(jnp.int32, (16, CHUNK), 0) < 8
    fwd_row1 = lax.broadcasted_iota(jnp.int32, (16, 1), 0) < 8
    er = lax.broadcasted_iota(jnp.int32, (CHUNK, 1024), 0)
    ec = lax.broadcasted_iota(jnp.int32, (CHUNK, 1024), 1)
    src_lane = 16 + 16 * (ec >> 9) + 8 * ((ec >> 8) & 1) + ((ec >> 6) & 3)
    expand = jnp.where(er == src_lane, 1.0, 0.0).astype(BF16)
    er0 = lax.broadcasted_iota(jnp.int32, (CHUNK, 512), 0)
    ec0 = lax.broadcasted_iota(jnp.int32, (CHUNK, 512), 1)
    expand_tot = jnp.where(er0 == 8 * (ec0 >> 8) + ((ec0 >> 6) & 3), 1.0, 0.0).astype(BF16)
    lane256 = lax.broadcasted_iota(jnp.int32, (CHUNK, 256), 1)
    lane128_1 = lax.broadcasted_iota(jnp.int32, (1, CHUNK), 1)

    cw = jnp.concatenate([cwx_ref[...], cwb_ref[...], cwc_ref[...]], axis=1)
    cb = jnp.concatenate([cbx_ref[...], cbb_ref[...], cbc_ref[...]], axis=1)
    dsk = dsk_ref[...]

    def phase_a(grp, carry):
        cs = [grp * GROUP + j for j in range(GROUP)]
        r0s = [pl.multiple_of(c * CHUNK, CHUNK) for c in cs]

        def conv(c):
            wstart = pl.multiple_of(c * CHUNK + jnp.where(c >= ncc, HALO, 0), 8)
            win = xp_s[pl.ds(wstart, CHUNK + 2 * HALO), :]
            acc = jnp.broadcast_to(cb, (CHUNK, 512))
            for k in range(CONV_K):
                d = k - CONV_K // 2
                if d == 0:
                    tap = win[HALO:HALO + CHUNK, :]
                else:
                    tap = pltpu.roll(win, (-d) % (CHUNK + 2 * HALO), 0)[HALO:HALO + CHUNK, :]
                acc = acc + cw[k:k + 1, :] * tap
            return _silu(acc)

        dtts = [_softplus(dtr_s[pl.ds(r0, CHUNK), :].T[0:16, :] + bias_col) for r0 in r0s]
        a_ts = [dtt * a_col for dtt in dtts]
        acol_ts = [jnp.where(fwd_row, _dot_exact_lhs(a_t, triu), _dot_exact_lhs(a_t, tril))
                   for a_t in a_ts]
        us = [conv(c) for c in cs]
        xus = [u[:, 0:256] for u in us]
        bts = [u[:, 256:384].T.astype(BF16) for u in us]
        cbfs = [u[:, 384:512].astype(BF16) for u in us]
        xbfs = [xu.astype(BF16) for xu in xus]
        gmats = [_dot(cbf, bt) for cbf, bt in zip(cbfs, bts)]
        tms = []
        for dtt, acol_t in zip(dtts, acol_ts):
            tot = jnp.where(fwd_row1, acol_t[:, CHUNK - 1:CHUNK], acol_t[:, 0:1])
            w_t = dtt * jnp.exp(tot - acol_t)
            e_t = jnp.exp(acol_t)
            stacked = jnp.concatenate(
                [acol_t, w_t, e_t, jnp.zeros((CHUNK - 48, CHUNK), F32)], axis=0)
            tms.append(stacked.T)
        wes = [_dot(tm.astype(BF16), expand) for tm in tms]
        decs = []
        for tm in tms:
            tot_row = jnp.where(lane128_1 < 8, tm[CHUNK - 1:CHUNK, :], tm[0:1, :])
            tot512 = _dot_exact_lhs(jnp.broadcast_to(tot_row, (8, CHUNK)), expand_tot)
            decs.append(jnp.exp(tot512))
        ydiags = [jnp.zeros((CHUNK, 256), F32) for _ in cs]
        for i in range(4):
            in_head = (lane256 >= 64 * i) & (lane256 < 64 * i + 64)
            for j in range(GROUP):
                tm, acol_t, dtt = tms[j], acol_ts[j], dtts[j]
                arg = jnp.where(lower, tm[:, i:i + 1] - acol_t[i:i + 1, :],
                                tm[:, 8 + i:9 + i] - acol_t[8 + i:9 + i, :])
                scale = (jnp.where(lower, dtt[i:i + 1, :], 0.0)
                         + jnp.where(upper, dtt[8 + i:9 + i, :], 0.0))
                wmat = (gmats[j] * jnp.exp(arg) * scale).astype(BF16)
                ydiags[j] = jnp.where(in_head, _dot(wmat, xbfs[j]), ydiags[j])
        for j, c in enumerate(cs):
            w512 = wes[j][:, 0:512]
            e512 = wes[j][:, 512:1024]
            xdw = (jnp.concatenate([xus[j], xus[j]], axis=1) * w512).astype(BF16)
            sloc_s[c] = _dot(bts[j], xdw)
            dec_s[c] = decs[j]
            e_s[c] = e512
            cbf_s[c] = cbfs[j]
            yacc_s[pl.ds(r0s[j], CHUNK), :] = ydiags[j] + xus[j] * dsk
        return carry

    lax.fori_loop(0, nch // GROUP, phase_a, 0)

    fwd_order = list(range(nch))
    bwd_order = list(range(ncc - 1, -1, -1)) + list(range(nch - 1, ncc - 1, -1))
    for order, lo in ((fwd_order, 0), (bwd_order, 256)):
        state = jnp.zeros((SSD_STATE, 256), F32)
        for c in order:
            sin_s[c, :, lo:lo + 256] = state.astype(BF16)
            state = (state * dec_s[c, 0:1, lo:lo + 256]
                     + sloc_s[c, :, lo:lo + 256])

    def phase_c(cs, z_ref, zrows):
        yos = [_dot(cbf_s[c], sin_s[c]) for c in cs]
        vs = []
        for c, zrow, yo in zip(cs, zrows, yos):
            r0 = c * CHUNK if isinstance(c, int) else pl.multiple_of(c * CHUNK, CHUNK)
            yo = yo * e_s[c]
            y = yacc_s[pl.ds(r0, CHUNK), :] + yo[:, 0:256] + yo[:, 256:512]
            vs.append((r0, y * _silu(z_ref[pl.ds(zrow, CHUNK), :])))

        @pl.when(g == 0)
        def _():
            for r0, v in vs:
                vun_s[pl.ds(r0, CHUNK), 0:256] = v

        @pl.when(g == 1)
        def _():
            for r0, v in vs:
                vun_s[pl.ds(r0, CHUNK), 256:512] = v

    if ctx_out:
        phase_c(list(range(ncc)), zc_ref, [c * CHUNK for c in range(ncc)])

    group_c = next(n for n in (8, 4, 2, 1) if ncl % n == 0)

    def phase_c_lat(k, carry):
        ks = [k * group_c + j for j in range(group_c)]
        phase_c([kk + ncc for kk in ks], zl_ref,
                [pl.multiple_of(kk * CHUNK, CHUNK) for kk in ks])
        return carry

    lax.fori_loop(0, ncl // group_c, phase_c_lat, 0)

    @pl.when(g == SSD_GROUPS - 1)
    def _finalize():
        ng = ng_ref[...]

        def norm_rows(r0, nrows):
            v = vun_s[pl.ds(r0, nrows), :]
            ms = jnp.mean(v * v, axis=-1, keepdims=True)
            return (v * lax.rsqrt(ms + EPS) * ng).astype(BF16)

        if ctx_out:
            oc_ref[...] = norm_rows(0, n_ctx)

        def fin(k, carry):
            r0 = pl.multiple_of(k * 256, 256)
            ol_ref[pl.ds(r0, 256), :] = norm_rows(n_ctx + r0, 256)
            return carry

        lax.fori_loop(0, n_lat // 256, fin, 0)


def _ssd_call(p_c, p_l, conv_w8, conv_b, par, dsk, ng, batch, n_ctx, n_lat, ctx_out):
    nch = (n_ctx + n_lat) // CHUNK
    t_all = n_ctx + n_lat
    in_specs = [
        pl.BlockSpec((n_ctx, 256), lambda b, g: (b, g)),
        pl.BlockSpec((n_lat, 256), lambda b, g: (b, g)),
        pl.BlockSpec((n_ctx, 128), lambda b, g: (b, 4 + g)),
        pl.BlockSpec((n_lat, 128), lambda b, g: (b, 4 + g)),
        pl.BlockSpec((n_ctx, 128), lambda b, g: (b, 6 + g)),
        pl.BlockSpec((n_lat, 128), lambda b, g: (b, 6 + g)),
        pl.BlockSpec((n_ctx, 256), lambda b, g: (b, 4 + g)),
        pl.BlockSpec((n_lat, 256), lambda b, g: (b, 4 + g)),
        pl.BlockSpec((n_ctx, 128), lambda b, g: (b, 26 + g)),
        pl.BlockSpec((n_lat, 128), lambda b, g: (b, 26 + g)),
        pl.BlockSpec((8, 256), lambda b, g: (0, g)),
        pl.BlockSpec((8, 128), lambda b, g: (0, 4 + g)),
        pl.BlockSpec((8, 128), lambda b, g: (0, 6 + g)),
        pl.BlockSpec((1, 256), lambda b, g: (0, g)),
        pl.BlockSpec((1, 128), lambda b, g: (0, 4 + g)),
        pl.BlockSpec((1, 128), lambda b, g: (0, 6 + g)),
        pl.BlockSpec((None, 32, 128), lambda b, g: (g, 0, 0)),
        pl.BlockSpec((1, 256), lambda b, g: (0, g)),
        pl.BlockSpec((1, 512), lambda b, g: (0, 0)),
    ]
    args = [p_c, p_l, p_c, p_l, p_c, p_l, p_c, p_l, p_c, p_l,
            conv_w8, conv_w8, conv_w8, conv_b, conv_b, conv_b, par, dsk, ng]
    out_specs = [pl.BlockSpec((n_lat, 512), lambda b, g: (b, 0))]
    out_shape = [jax.ShapeDtypeStruct((batch * n_lat, 512), BF16)]
    if ctx_out:
        out_specs = [pl.BlockSpec((n_ctx, 512), lambda b, g: (b, 0))] + out_specs
        out_shape = [jax.ShapeDtypeStruct((batch * n_ctx, 512), BF16)] + out_shape
    scratch = [
        pltpu.VMEM((t_all + 3 * HALO, 512), F32),
        pltpu.VMEM((t_all, 128), F32),
        pltpu.VMEM((t_all, 256), F32),
        pltpu.VMEM((nch, SSD_STATE, 512), F32),
        pltpu.VMEM((nch, CHUNK, 512), F32),
        pltpu.VMEM((nch, CHUNK, 128), BF16),
        pltpu.VMEM((nch, SSD_STATE, 512), BF16),
        pltpu.VMEM((nch, 8, 512), F32),
        pltpu.VMEM((t_all, 512), F32),
    ]
    outs = pl.pallas_call(
        functools.partial(_ssd_kernel, n_ctx=n_ctx, n_lat=n_lat, ctx_out=ctx_out),
        grid=(batch, SSD_GROUPS),
        in_specs=in_specs,
        out_specs=out_specs,
        out_shape=out_shape,
        scratch_shapes=scratch,
        compiler_params=pltpu.CompilerParams(
            dimension_semantics=("arbitrary", "arbitrary"),
            vmem_limit_bytes=VMEM_LIMIT),
        name="ssd_scan",
    )(*args)
    if ctx_out:
        return outs[1], outs[0]
    return outs[0], None


def _rope_tables(n_lat):
    rows = n_lat // GRID_W
    row_idx = np.repeat(np.arange(rows), GRID_W).astype(np.float32)
    col_idx = (np.arange(rows * GRID_W) % GRID_W).astype(np.float32)

    def tables(dim, reps):
        quarter = dim // 4
        inv = (ROPE_BASE ** (-np.arange(quarter, dtype=np.float32) / quarter)).astype(np.float32)
        ang = np.concatenate([row_idx[:, None] * inv, col_idx[:, None] * inv], axis=-1)
        cos, sin = np.cos(ang.astype(np.float64)), np.sin(ang.astype(np.float64))
        cos2 = np.concatenate([cos, cos], axis=-1).astype(np.float32)
        sin2 = np.concatenate([-sin, sin], axis=-1).astype(np.float32)
        return jnp.asarray(np.tile(cos2, (1, reps))), jnp.asarray(np.tile(sin2, (1, reps)))

    cos_g, sin_g = tables(GQA_HEAD_DIM, 4)
    cos_d, sin_d = tables(DIFF_QK_DIM, 8)
    return {"cos_g": cos_g, "sin_g": sin_g, "cos_d": cos_d, "sin_d": sin_d}


def kernel(x, c, ctx, c_ctx, w_mod, b_mod, g_pre, g_post, w_in, conv_w, conv_b,
           a_log_fwd, a_log_bwd, dt_bias_fwd, dt_bias_bwd, d_skip, ssd_norm_g,
           q_norm_g, k_norm_g, diff_lambda, diff_norm_g, w_out):
    batch, n_lat, _ = x.shape
    n_ctx = ctx.shape[1]
    depth = w_mod.shape[0]
    assert n_lat % 512 == 0 and n_ctx % 256 == 0 and (batch * n_ctx) % 512 == 0
    assert batch + 1 <= 16

    in_perm, out_perm = _in_col_perm(), _out_row_perm()
    w_in_p = [_take_runs(w_in[l], in_perm, 1, IN_COLS).astype(BF16) for l in range(depth)]
    w_out_p = [_take_runs(w_out[l], out_perm, 0, None).astype(BF16) for l in range(depth)]
    conv_w8 = jnp.pad(conv_w, ((0, 0), (0, 8 - CONV_K), (0, 0)))
    conv_b1 = conv_b[:, None, :]

    def group16(fwd, bwd):
        out = jnp.zeros((depth, SSD_GROUPS, 16), F32)
        for g in range(SSD_GROUPS):
            out = out.at[:, g, 0:4].set(fwd[:, 4 * g:4 * g + 4])
            out = out.at[:, g, 8:12].set(bwd[:, 4 * g:4 * g + 4])
        return out

    ssd_par = jnp.broadcast_to(
        jnp.concatenate([group16(a_log_fwd, a_log_bwd),
                         group16(dt_bias_fwd, dt_bias_bwd)], axis=-1)[..., None],
        (depth, SSD_GROUPS, 32, 128))
    dsk = jnp.repeat(d_skip, SSD_HEAD_DIM, axis=1)[:, None, :]
    qg = jnp.tile(q_norm_g, (1, 4))[:, None, :]
    kg = jnp.tile(k_norm_g, (1, 2))[:, None, :]
    dng = jnp.tile(diff_norm_g, (1, 4))[:, None, :]
    tabs = _rope_tables(n_lat)

    cs = jnp.concatenate(
        [c, c_ctx[None, :], jnp.zeros((16 - batch - 1, D_MODEL), F32)], axis=0)
    mod_all = _mod_call(cs, w_mod, b_mod)

    h = x.reshape(batch * n_lat, D_MODEL)
    hc = ctx.reshape(batch * n_ctx, D_MODEL)
    for l in range(depth):
        ctx_out = l < depth - 1
        lam_init = 0.8 - 0.6 * float(np.exp(-0.3 * l))
        mod3 = mod_all[l][:, None, :]
        p_l = _inproj_call(h, mod3, g_pre[l][None, :], w_in_p[l], n_lat // 512, None)
        p_c = _inproj_call(hc, mod3, g_pre[l][None, :], w_in_p[l], None, batch)

        ys_l, ys_c = _ssd_call(p_c, p_l, conv_w8[l], conv_b1[l], ssd_par[l], dsk[l],
                               ssd_norm_g[l][None, :], batch, n_ctx, n_lat, ctx_out)
        yg_l = _gqa_call(p_l, p_c, p_l, tabs, qg[l], kg[l], batch, n_ctx, n_lat, True)
        diff_args = (p_l, p_c, p_l, tabs, diff_lambda[l], dng[l], batch, n_ctx, n_lat, True,
                     lam_init)
        yd_fast, row_sum_min = _diff_call(*diff_args, True)
        yd_l = lax.cond(jnp.min(row_sum_min) >= MIN_ROW_SUM,
                        lambda: yd_fast, lambda: _diff_call(*diff_args, False))
        h = _outproj_call(ys_l, yg_l, yd_l, h, mod3, g_post[l][None, :], w_out_p[l],
                          n_lat, None)
        if ctx_out:
            yg_c = _gqa_call(p_c, p_c, None, tabs, qg[l], kg[l], batch, n_ctx, 0, False)
            yd_c = _diff_call(p_c, p_c, None, tabs, diff_lambda[l], dng[l], batch,
                              n_ctx, 0, False, lam_init, False)
            hc = _outproj_call(ys_c, yg_c, yd_c, hc, mod3, g_post[l][None, :],
                               w_out_p[l], None, batch)
    return h.reshape(batch, n_lat, D_MODEL)
```

```python
import functools

import numpy as np
import jax
import jax.numpy as jnp
from jax import lax
from jax.experimental import pallas as pl
from jax.experimental.pallas import tpu as pltpu

F32 = jnp.float32
BF16 = jnp.bfloat16

D_MODEL = 1024
GRID_W = 64
ROPE_BASE = 10000.0
EPS = 1e-6
LOG2E = 1.4426950408889634

SSD_WIDTH = 512
SSD_HEADS = 8
SSD_HEAD_DIM = 64
SSD_GROUPS = 2
SSD_STATE = 128
CHUNK = 128
CONV_K = 5
HALO = 8
GQA_HEADS = 4
GQA_HEAD_DIM = 64
DIFF_HEADS = 4
DIFF_QK_DIM = 32
DIFF_V_DIM = 64

_IN_SPLITS = (("xbc", 1024), ("z", 512), ("dt", 16), ("gq", 256), ("gk", 128),
              ("gv", 128), ("gg", 256), ("dq", 256), ("dk", 256), ("dv", 256),
              ("dg", 256))
IN_COLS = sum(s for _, s in _IN_SPLITS)
NP = 28 * 128
GQ_HEAD_ORDER = (0, 2, 1, 3)

VMEM_LIMIT = 56 * 1024 * 1024
ATTN_TQ = 1024
DIFF_TQ = 1024
ATTN_SUB = 512
DIFF_SUB = 512
ATTN_AHEAD = 2
DIFF_AHEAD = 1
SCORE_BOUND_MARGIN = 1.02
MIN_ROW_SUM = 2.0 ** -90
OUTPROJ_TM = 1024
OUTPROJ_SUB = 256


def _in_col_perm():
    off, o = {}, 0
    for name, size in _IN_SPLITS:
        off[name] = o
        o += size
    pad = IN_COLS
    cols = list(range(off["xbc"], off["xbc"] + 1024))
    cols += list(range(off["z"], off["z"] + 512))
    for h in GQ_HEAD_ORDER:
        cols += list(range(off["gq"] + 64 * h, off["gq"] + 64 * h + 64))
    cols += list(range(off["gk"], off["gk"] + 128))
    cols += list(range(off["gv"], off["gv"] + 128))
    for h in GQ_HEAD_ORDER:
        cols += list(range(off["gg"] + 64 * h, off["gg"] + 64 * h + 64))
    for name in ("dq", "dk", "dv", "dg"):
        cols += list(range(off[name], off[name] + 256))
    for g in range(SSD_GROUPS):
        blk = [pad] * 128
        for i in range(4):
            blk[i] = off["dt"] + 4 * g + i
            blk[8 + i] = off["dt"] + SSD_HEADS + 4 * g + i
        cols += blk
    assert len(cols) == NP
    return np.asarray(cols, np.int32)


def _out_row_perm():
    rows = list(range(SSD_WIDTH))
    for h in GQ_HEAD_ORDER:
        rows += list(range(SSD_WIDTH + 64 * h, SSD_WIDTH + 64 * h + 64))
    rows += list(range(SSD_WIDTH + 256, SSD_WIDTH + 512))
    return np.asarray(rows, np.int32)


def _take_runs(arr, idx, axis, pad_index):
    pieces, start = [], 0
    idx = [int(i) for i in idx]
    while start < len(idx):
        end = start + 1
        if idx[start] == pad_index:
            while end < len(idx) and idx[end] == pad_index:
                end += 1
            shape = list(arr.shape)
            shape[axis] = end - start
            pieces.append(jnp.zeros(shape, arr.dtype))
        else:
            while end < len(idx) and idx[end] == idx[end - 1] + 1 and idx[end] != pad_index:
                end += 1
            pieces.append(lax.slice_in_dim(arr, idx[start], idx[end - 1] + 1, axis=axis))
        start = end
    return jnp.concatenate(pieces, axis=axis)


def _dot(a, b):
    return jnp.dot(a, b, preferred_element_type=F32)


def _split3(x):
    hi = x.astype(BF16)
    r1 = x - hi.astype(F32)
    mid = r1.astype(BF16)
    lo = (r1 - mid.astype(F32)).astype(BF16)
    return hi, mid, lo


def _dot_exact_lhs(x, m_bf16):
    hi, mid, lo = _split3(x)
    return _dot(hi, m_bf16) + _dot(mid, m_bf16) + _dot(lo, m_bf16)


def _silu(x):
    return x * jax.nn.sigmoid(x)


def _seg_ones(width, seg):
    r = lax.broadcasted_iota(jnp.int32, (width, width), 0)
    c = lax.broadcasted_iota(jnp.int32, (width, width), 1)
    same = (r & ~(seg - 1)) == (c & ~(seg - 1))
    return jnp.where(same, 1.0, 0.0).astype(BF16)


def _seg_rms(x, seg, seg_mat):
    ss = _dot_exact_lhs(x * x, seg_mat)
    return x * lax.rsqrt(ss * (1.0 / seg) + EPS)


def _rope(x, cos, sin_signed, half):
    w = x.shape[-1]
    lane = lax.broadcasted_iota(jnp.int32, x.shape, 1)
    first = (lane & (2 * half - 1)) < half
    swapped = jnp.where(first, pltpu.roll(x, w - half, 1), pltpu.roll(x, half, 1))
    return x * cos + swapped * sin_signed


def _mod_kernel(cs_ref, w_ref, b_ref, o_ref):
    s = _silu(cs_ref[...]).astype(BF16)
    o_ref[...] = _dot(s, w_ref[...].astype(BF16)) + b_ref[...]


def _mod_call(cs, w_mod, b_mod):
    depth = w_mod.shape[0]
    nrow = cs.shape[0]
    tn = 1024
    return pl.pallas_call(
        _mod_kernel,
        grid=(depth, 3 * D_MODEL // tn),
        in_specs=[
            pl.BlockSpec((nrow, D_MODEL), lambda l, j: (0, 0)),
            pl.BlockSpec((None, D_MODEL, tn), lambda l, j: (l, 0, j)),
            pl.BlockSpec((None, 1, tn), lambda l, j: (l, 0, j)),
        ],
        out_specs=pl.BlockSpec((None, nrow, tn), lambda l, j: (l, 0, j)),
        out_shape=jax.ShapeDtypeStruct((depth, nrow, 3 * D_MODEL), F32),
        compiler_params=pltpu.CompilerParams(
            dimension_semantics=("arbitrary", "arbitrary")),
        name="mod_proj",
    )(cs, w_mod, b_mod.reshape(depth, 1, 3 * D_MODEL))


def _inproj_kernel(h_ref, mod_ref, g_ref, w_ref, o_ref):
    x = h_ref[...]
    ms = jnp.mean(x * x, axis=-1, keepdims=True)
    y = x * lax.rsqrt(ms + EPS) * g_ref[...]
    sh = mod_ref[:, 0:D_MODEL]
    sc = mod_ref[:, D_MODEL:2 * D_MODEL]
    u = (y * (1.0 + sc) + sh).astype(BF16)
    tn = 512
    for j in range(NP // tn):
        o_ref[:, j * tn:(j + 1) * tn] = _dot(u, w_ref[:, j * tn:(j + 1) * tn])


def _inproj_call(h, mod3, g_pre, w_bf16, tiles_per_row, fixed_row):
    n_tok = h.shape[0]
    tm = 512
    if fixed_row is None:
        mod_idx = lambda i: (i // tiles_per_row, 0, 0)
    else:
        mod_idx = lambda i: (fixed_row, 0, 0)
    return pl.pallas_call(
        _inproj_kernel,
        grid=(n_tok // tm,),
        in_specs=[
            pl.BlockSpec((tm, D_MODEL), lambda i: (i, 0)),
            pl.BlockSpec((None, 1, 3 * D_MODEL), mod_idx),
            pl.BlockSpec((1, D_MODEL), lambda i: (0, 0)),
            pl.BlockSpec((D_MODEL, NP), lambda i: (0, 0)),
        ],
        out_specs=pl.BlockSpec((tm, NP), lambda i: (i, 0)),
        out_shape=jax.ShapeDtypeStruct((n_tok, NP), F32),
        compiler_params=pltpu.CompilerParams(
            dimension_semantics=("arbitrary",), vmem_limit_bytes=VMEM_LIMIT),
        name="in_proj",
    )(h, mod3, g_pre, w_bf16)


def _outproj_kernel(ys_ref, yg_ref, yd_ref, h_ref, mod_ref, g_ref, w_ref, o_ref):
    tm = h_ref.shape[0]
    sub = min(OUTPROJ_SUB, tm)
    gt = mod_ref[:, 2 * D_MODEL:3 * D_MODEL]
    gain = g_ref[...]

    def project(r0):
        return (_dot(ys_ref[r0:r0 + sub, :], w_ref[0:512, :])
                + _dot(yg_ref[r0:r0 + sub, :], w_ref[512:768, :])
                + _dot(yd_ref[r0:r0 + sub, :], w_ref[768:1024, :]))

    o_next = project(0)
    for r0 in range(0, tm, sub):
        o = o_next
        if r0 + sub < tm:
            o_next = project(r0 + sub)
        ms = jnp.mean(o * o, axis=-1, keepdims=True)
        n = o * lax.rsqrt(ms + EPS) * gain
        o_ref[r0:r0 + sub, :] = h_ref[r0:r0 + sub, :] + gt * n


def _outproj_call(ys, yg, yd, h, mod3, g_post, w_bf16, rows_per_mod, fixed_row):
    n_tok = h.shape[0]
    tm = OUTPROJ_TM
    assert n_tok % tm == 0
    if fixed_row is None:
        assert rows_per_mod % tm == 0
        mod_idx = lambda i: (i // (rows_per_mod // tm), 0, 0)
    else:
        mod_idx = lambda i: (fixed_row, 0, 0)
    return pl.pallas_call(
        _outproj_kernel,
        grid=(n_tok // tm,),
        in_specs=[
            pl.BlockSpec((tm, 512), lambda i: (i, 0)),
            pl.BlockSpec((tm, 256), lambda i: (i, 0)),
            pl.BlockSpec((tm, 256), lambda i: (i, 0)),
            pl.BlockSpec((tm, D_MODEL), lambda i: (i, 0)),
            pl.BlockSpec((None, 1, 3 * D_MODEL), mod_idx),
            pl.BlockSpec((1, D_MODEL), lambda i: (0, 0)),
            pl.BlockSpec((D_MODEL, D_MODEL), lambda i: (0, 0)),
        ],
        out_specs=pl.BlockSpec((tm, D_MODEL), lambda i: (i, 0)),
        out_shape=jax.ShapeDtypeStruct((n_tok, D_MODEL), F32),
        compiler_params=pltpu.CompilerParams(
            dimension_semantics=("arbitrary",), vmem_limit_bytes=VMEM_LIMIT),
        name="out_proj",
    )(ys, yg, yd, h, mod3, g_post, w_bf16)


def _attend_many(lhs_list, kt_ref, vext_refs):
    def scores(i):
        s = _dot(lhs_list[i], kt_ref[...])
        return s, jnp.max(s, axis=-1, keepdims=True)

    outs = []
    n = len(lhs_list)
    ahead = [scores(i) for i in range(min(ATTN_AHEAD, n))]
    for i, vext_ref in enumerate(vext_refs):
        s, m = ahead.pop(0)
        if i + ATTN_AHEAD < n:
            ahead.append(scores(i + ATTN_AHEAD))
        p = jnp.exp2(s - m).astype(BF16)
        oe = _dot(p, vext_ref[...])
        outs.append(oe[:, 0:128] / oe[:, 128:256])
    return outs


def _attend_diff_pairs(lhs_list, kt_ref, v_refs, lam):
    n_heads = len(v_refs)

    def scores(h):
        s_a = _dot(lhs_list[2 * h], kt_ref[...])
        m_a = jnp.max(s_a, axis=-1, keepdims=True)
        s_b = _dot(lhs_list[2 * h + 1], kt_ref[...])
        m_b = jnp.max(s_b, axis=-1, keepdims=True)
        return s_a, m_a, s_b, m_b

    outs = []
    ahead = [scores(h) for h in range(min(DIFF_AHEAD, n_heads))]
    for h in range(n_heads):
        s_a, m_a, s_b, m_b = ahead.pop(0)
        if h + DIFF_AHEAD < n_heads:
            ahead.append(scores(h + DIFF_AHEAD))
        e_a = jnp.exp2(s_a - m_a)
        e_b = jnp.exp2(s_b - m_b)
        l_a = jnp.sum(e_a, axis=-1, keepdims=True)
        l_b = jnp.sum(e_b, axis=-1, keepdims=True)
        pc = (e_a - (lam * l_a / l_b) * e_b).astype(BF16)
        outs.append(_dot(pc, v_refs[h][...]) / l_a)
    return outs


def _attend_diff_pairs_bounded(lhs_list, bounds, kt_ref, v_refs, lam):
    n_heads = len(v_refs)

    def exps(h):
        e_a = jnp.exp2(_dot(lhs_list[2 * h], kt_ref[...]) - bounds[2 * h])
        e_b = jnp.exp2(_dot(lhs_list[2 * h + 1], kt_ref[...]) - bounds[2 * h + 1])
        return e_a, e_b

    outs, l_min = [], None
    ahead = [exps(h) for h in range(min(DIFF_AHEAD, n_heads))]
    for h in range(n_heads):
        e_a, e_b = ahead.pop(0)
        if h + DIFF_AHEAD < n_heads:
            ahead.append(exps(h + DIFF_AHEAD))
        l_a = jnp.sum(e_a, axis=-1, keepdims=True)
        l_b = jnp.sum(e_b, axis=-1, keepdims=True)
        pc = (e_a - (lam * l_a / l_b) * e_b).astype(BF16)
        outs.append(_dot(pc, v_refs[h][...]) / l_a)
        l_ab = jnp.minimum(l_a, l_b)
        l_min = l_ab if l_min is None else jnp.minimum(l_min, l_ab)
    return outs, jnp.min(l_min, axis=0, keepdims=True)


def _gqa_kernel(*refs, n_ctx, n_lat, rope_q):
    it = iter(refs)
    q_ref, gg_ref, kvc_ref = next(it), next(it), next(it)
    kvl_ref = next(it) if n_lat else None
    if rope_q:
        cosq_ref, sinq_ref = next(it), next(it)
    if n_lat:
        cosk_ref, sink_ref = next(it), next(it)
    qg_ref, kg_ref = next(it), next(it)
    y_ref = next(it)
    kt_s, vext_s = next(it), next(it)

    seg128 = _seg_ones(128, 64)

    @pl.when(pl.program_id(1) == 0)
    def _prep_kv():
        kc = _seg_rms(kvc_ref[:, 0:128], 64, seg128) * kg_ref[...]
        kt_s[:, 0:n_ctx] = kc.T.astype(BF16)
        vext_s[0:n_ctx, 0:128] = kvc_ref[:, 128:256].astype(BF16)
        if n_lat:
            kl = _seg_rms(kvl_ref[:, 0:128], 64, seg128) * kg_ref[...]
            kl = _rope(kl, cosk_ref[...], sink_ref[...], 32)
            kt_s[:, n_ctx:n_ctx + n_lat] = kl.T.astype(BF16)
            vext_s[n_ctx:n_ctx + n_lat, 0:128] = kvl_ref[:, 128:256].astype(BF16)
        vext_s[:, 128:256] = jnp.ones((n_ctx + n_lat, 128), BF16)

    seg256 = _seg_ones(256, 64)
    q = _seg_rms(q_ref[...], 64, seg256) * qg_ref[...]
    if rope_q:
        q = _rope(q, cosq_ref[...], sinq_ref[...], 32)
    q = q * (GQA_HEAD_DIM ** -0.5 * LOG2E)
    tq = q.shape[0]
    sub = min(ATTN_SUB, tq)
    lane = lax.broadcasted_iota(jnp.int32, (sub, 128), 1)
    lhs_list = []
    for r0 in range(0, tq, sub):
        for half in range(2):
            qh = q[r0:r0 + sub, 128 * half:128 * half + 128]
            for kv in range(2):
                in_kv = (lane >= 64 * kv) & (lane < 64 * kv + 64)
                lhs_list.append(jnp.where(in_kv, qh, 0.0).astype(BF16))
    outs = _attend_many(lhs_list, kt_s, [vext_s] * len(lhs_list))
    for j, r0 in enumerate(range(0, tq, sub)):
        for half in range(2):
            o = jnp.where(lane < 64, outs[4 * j + 2 * half], outs[4 * j + 2 * half + 1])
            gate = _silu(gg_ref[r0:r0 + sub, 128 * half:128 * half + 128])
            y_ref[r0:r0 + sub, 128 * half:128 * half + 128] = (o * gate).astype(BF16)


def _gqa_call(p_q, p_c, p_l, tabs, qg, kg, batch, n_ctx, n_lat, rope_q):
    t_total = p_q.shape[0] // batch
    tq = min(ATTN_TQ, t_total)
    nq = t_total // tq
    in_specs = [
        pl.BlockSpec((tq, 256), lambda b, i: (b * nq + i, 6)),
        pl.BlockSpec((tq, 256), lambda b, i: (b * nq + i, 8)),
        pl.BlockSpec((n_ctx, 256), lambda b, i: (b, 7)),
    ]
    args = [p_q, p_q, p_c]
    if n_lat:
        in_specs.append(pl.BlockSpec((n_lat, 256), lambda b, i: (b, 7)))
        args.append(p_l)
    if rope_q:
        in_specs += [pl.BlockSpec((tq, 256), lambda b, i: (i, 0))] * 2
        args += [tabs["cos_g"], tabs["sin_g"]]
    if n_lat:
        in_specs += [pl.BlockSpec((n_lat, 128), lambda b, i: (0, 0))] * 2
        args += [tabs["cos_g"], tabs["sin_g"]]
    in_specs += [pl.BlockSpec((1, 256), lambda b, i: (0, 0)),
                 pl.BlockSpec((1, 128), lambda b, i: (0, 0))]
    args += [qg, kg]
    s_keys = n_ctx + n_lat
    return pl.pallas_call(
        functools.partial(_gqa_kernel, n_ctx=n_ctx, n_lat=n_lat, rope_q=rope_q),
        grid=(batch, nq),
        in_specs=in_specs,
        out_specs=pl.BlockSpec((tq, 256), lambda b, i: (b * nq + i, 0)),
        out_shape=jax.ShapeDtypeStruct((p_q.shape[0], 256), BF16),
        scratch_shapes=[pltpu.VMEM((128, s_keys), BF16),
                        pltpu.VMEM((s_keys, 256), BF16)],
        compiler_params=pltpu.CompilerParams(
            dimension_semantics=("arbitrary", "arbitrary"),
            vmem_limit_bytes=VMEM_LIMIT),
        name="gqa_attn",
    )(*args)


def _diff_kernel(*refs, n_ctx, n_lat, rope_q, lam_init, bounded):
    it = iter(refs)
    q_ref, dg_ref, kc_ref, vc_ref = next(it), next(it), next(it), next(it)
    if n_lat:
        kl_ref, vl_ref = next(it), next(it)
    if rope_q:
        cosq_ref, sinq_ref = next(it), next(it)
    if n_lat:
        cosk_ref, sink_ref = next(it), next(it)
    lam_ref, ng_ref = next(it), next(it)
    y_ref = next(it)
    lmin_ref = next(it) if bounded else None
    kt_s, vlo_s, vhi_s = next(it), next(it), next(it)
    kmax_s = next(it) if bounded else None
    s_keys = n_ctx + n_lat
    seg32 = _seg_ones(256, DIFF_QK_DIM)

    def map_norms(x):
        return jnp.sqrt(_dot((x * x).astype(BF16), seg32))

    @pl.when(pl.program_id(1) == 0)
    def _prep_kv():
        kc = kc_ref[...]
        kt_s[:, 0:n_ctx] = kc.T.astype(BF16)
        vlo_s[0:n_ctx, 0:128] = vc_ref[:, 0:128].astype(BF16)
        vhi_s[0:n_ctx, 0:128] = vc_ref[:, 128:256].astype(BF16)
        if bounded:
            kmax = jnp.max(map_norms(kc), axis=0, keepdims=True)
        if n_lat:
            kl = _rope(kl_ref[...], cosk_ref[...], sink_ref[...], 16)
            kt_s[:, n_ctx:s_keys] = kl.T.astype(BF16)
            vlo_s[n_ctx:s_keys, 0:128] = vl_ref[:, 0:128].astype(BF16)
            vhi_s[n_ctx:s_keys, 0:128] = vl_ref[:, 128:256].astype(BF16)
            if bounded:
                kmax = jnp.maximum(kmax, jnp.max(map_norms(kl), axis=0, keepdims=True))
        if bounded:
            kmax_s[...] = jnp.broadcast_to(kmax, kmax_s.shape)

    lp = lam_ref[...]
    lam = (jnp.exp(jnp.sum(lp[0:1, :] * lp[1:2, :], axis=-1, keepdims=True))
           - jnp.exp(jnp.sum(lp[2:3, :] * lp[3:4, :], axis=-1, keepdims=True))
           + lam_init)

    q = q_ref[...]
    if rope_q:
        q = _rope(q, cosq_ref[...], sinq_ref[...], 16)
    q = q * (DIFF_QK_DIM ** -0.5 * LOG2E)
    tq = q.shape[0]
    sub = min(DIFF_SUB, tq)
    lane256 = lax.broadcasted_iota(jnp.int32, (sub, 256), 1)
    lane128 = lax.broadcasted_iota(jnp.int32, (sub, 128), 1)
    seg128 = _seg_ones(128, 64)
    if bounded:
        bound_all = map_norms(q) * kmax_s[0:1, :] * SCORE_BOUND_MARGIN
    lhs_list, v_list, bounds = [], [], []
    for r0 in range(0, tq, sub):
        for mp in range(2 * DIFF_HEADS):
            in_map = (lane256 >= 32 * mp) & (lane256 < 32 * mp + 32)
            lhs_list.append(jnp.where(in_map, q[r0:r0 + sub, :], 0.0).astype(BF16))
            if bounded:
                bounds.append(bound_all[r0:r0 + sub, 32 * mp:32 * mp + 1])
        v_list += [vlo_s, vlo_s, vhi_s, vhi_s]

    if bounded:
        heads, l_min = _attend_diff_pairs_bounded(lhs_list, bounds, kt_s, v_list, lam)
        lmin_ref[...] = jnp.broadcast_to(l_min, lmin_ref.shape)
    else:
        heads = _attend_diff_pairs(lhs_list, kt_s, v_list, lam)
    for j, r0 in enumerate(range(0, tq, sub)):
        for half in range(2):
            o = jnp.where(lane128 < 64, heads[4 * j + 2 * half], heads[4 * j + 2 * half + 1])
            n = _seg_rms(o, 64, seg128) * ng_ref[:, 128 * half:128 * half + 128]
            n = n * (1.0 - lam_init)
            gate = _silu(dg_ref[r0:r0 + sub, 128 * half:128 * half + 128])
            y_ref[r0:r0 + sub, 128 * half:128 * half + 128] = (n * gate).astype(BF16)


def _diff_call(p_q, p_c, p_l, tabs, lam_params, ng, batch, n_ctx, n_lat, rope_q,
               lam_init, bounded):
    t_total = p_q.shape[0] // batch
    tq = min(DIFF_TQ, t_total)
    nq = t_total // tq
    in_specs = [
        pl.BlockSpec((tq, 256), lambda b, i: (b * nq + i, 9)),
        pl.BlockSpec((tq, 256), lambda b, i: (b * nq + i, 12)),
        pl.BlockSpec((n_ctx, 256), lambda b, i: (b, 10)),
        pl.BlockSpec((n_ctx, 256), lambda b, i: (b, 11)),
    ]
    args = [p_q, p_q, p_c, p_c]
    if n_lat:
        in_specs += [pl.BlockSpec((n_lat, 256), lambda b, i: (b, 10)),
                     pl.BlockSpec((n_lat, 256), lambda b, i: (b, 11))]
        args += [p_l, p_l]
    if rope_q:
        in_specs += [pl.BlockSpec((tq, 256), lambda b, i: (i, 0))] * 2
        args += [tabs["cos_d"], tabs["sin_d"]]
    if n_lat:
        in_specs += [pl.BlockSpec((n_lat, 256), lambda b, i: (0, 0))] * 2
        args += [tabs["cos_d"], tabs["sin_d"]]
    in_specs += [pl.BlockSpec((4, DIFF_QK_DIM), lambda b, i: (0, 0)),
                 pl.BlockSpec((1, 256), lambda b, i: (0, 0))]
    args += [lam_params, ng]
    s_keys = n_ctx + n_lat
    out_specs = [pl.BlockSpec((tq, 256), lambda b, i: (b * nq + i, 0))]
    out_shape = [jax.ShapeDtypeStruct((p_q.shape[0], 256), BF16)]
    scratch = [pltpu.VMEM((256, s_keys), BF16),
               pltpu.VMEM((s_keys, 128), BF16),
               pltpu.VMEM((s_keys, 128), BF16)]
    if bounded:
        out_specs.append(pl.BlockSpec((None, 8, 128), lambda b, i: (b * nq + i, 0, 0)))
        out_shape.append(jax.ShapeDtypeStruct((batch * nq, 8, 128), F32))
        scratch.append(pltpu.VMEM((8, 256), F32))
    outs = pl.pallas_call(
        functools.partial(_diff_kernel, n_ctx=n_ctx, n_lat=n_lat, rope_q=rope_q,
                          lam_init=lam_init, bounded=bounded),
        grid=(batch, nq),
        in_specs=in_specs,
        out_specs=out_specs,
        out_shape=out_shape,
        scratch_shapes=scratch,
        compiler_params=pltpu.CompilerParams(
            dimension_semantics=("arbitrary", "arbitrary"),
            vmem_limit_bytes=VMEM_LIMIT),
        name="diff_attn_bounded" if bounded else "diff_attn",
    )(*args)
    return (outs[0], outs[1]) if bounded else outs[0]


def _ssd_kernel(xc_ref, xl_ref, bc_ref, bl_ref, cc_ref, cl_ref, zc_ref, zl_ref,
                dtc_ref, dtl_ref, cwx_ref, cwb_ref, cwc_ref, cbx_ref, cbb_ref,
                cbc_ref, par_ref, dsk_ref, ng_ref, *rest, n_ctx, n_lat, ctx_out):
    if ctx_out:
        oc_ref, ol_ref = rest[0], rest[1]
        rest = rest[2:]
    else:
        oc_ref, ol_ref = None, rest[0]
        rest = rest[1:]
    xp_s, dtr_s, yacc_s, sloc_s, e_s, cbf_s, sin_s, dec_s, vun_s = rest

    g = pl.program_id(1)
    ncc = n_ctx // CHUNK
    ncl = n_lat // CHUNK
    nch = ncc + ncl
    GROUP = next(n for n in (18, 9, 6, 3, 2, 1) if nch % n == 0)
    t_all = n_ctx + n_lat
    lat0 = n_ctx + 2 * HALO

    zeros_h = jnp.zeros((HALO, 512), F32)
    xp_s[0:HALO, :] = zeros_h
    xp_s[HALO:HALO + n_ctx, 0:256] = xc_ref[...]
    xp_s[HALO:HALO + n_ctx, 256:384] = bc_ref[...]
    xp_s[HALO:HALO + n_ctx, 384:512] = cc_ref[...]
    xp_s[HALO + n_ctx:lat0, :] = zeros_h
    xp_s[lat0:lat0 + n_lat, 0:256] = xl_ref[...]
    xp_s[lat0:lat0 + n_lat, 256:384] = bl_ref[...]
    xp_s[lat0:lat0 + n_lat, 384:512] = cl_ref[...]
    xp_s[lat0 + n_lat:lat0 + n_lat + HALO, :] = zeros_h

    dtr_s[0:n_ctx, :] = dtc_ref[...]
    dtr_s[n_ctx:t_all, :] = dtl_ref[...]
    a_col = -jnp.exp(par_ref[0:16, :])
    bias_col = par_ref[16:32, :]

    def _softplus(v):
        return jnp.maximum(v, 0.0) + jnp.log1p(jnp.exp(-jnp.abs(v)))

    r128 = lax.broadcasted_iota(jnp.int32, (CHUNK, CHUNK), 0)
    c128 = lax.broadcasted_iota(jnp.int32, (CHUNK, CHUNK), 1)
    lower = c128 <= r128
    upper = c128 >= r128
    tril = jnp.where(lower, 1.0, 0.0).astype(BF16)
    triu = jnp.where(upper, 1.0, 0.0).astype(BF16)
    fwd_row = lax.broadcasted_iota(jnp.int32, (16, CHUNK), 0) < 8
    fwd_row1 = lax.broadcasted_iota(jnp.int32, (16, 1), 0) < 8
    er = lax.broadcasted_iota(jnp.int32, (CHUNK, 1024), 0)
    ec = lax.broadcasted_iota(jnp.int32, (CHUNK, 1024), 1)
    src_lane = 16 + 16 * (ec >> 9) + 8 * ((ec >> 8) & 1) + ((ec >> 6) & 3)
    expand = jnp.where(er == src_lane, 1.0, 0.0).astype(BF16)
    er0 = lax.broadcasted_iota(jnp.int32, (CHUNK, 512), 0)
    ec0 = lax.broadcasted_iota(jnp.int32, (CHUNK, 512), 1)
    expand_tot = jnp.where(er0 == 8 * (ec0 >> 8) + ((ec0 >> 6) & 3), 1.0, 0.0).astype(BF16)
    lane256 = lax.broadcasted_iota(jnp.int32, (CHUNK, 256), 1)
    lane128_1 = lax.broadcasted_iota(jnp.int32, (1, CHUNK), 1)

    cw = jnp.concatenate([cwx_ref[...], cwb_ref[...], cwc_ref[...]], axis=1)
    cb = jnp.concatenate([cbx_ref[...], cbb_ref[...], cbc_ref[...]], axis=1)
    dsk = dsk_ref[...]

    def phase_a(grp, carry):
        cs = [grp * GROUP + j for j in range(GROUP)]
        r0s = [pl.multiple_of(c * CHUNK, CHUNK) for c in cs]

        def conv(c):
            wstart = pl.multiple_of(c * CHUNK + jnp.where(c >= ncc, HALO, 0), 8)
            win = xp_s[pl.ds(wstart, CHUNK + 2 * HALO), :]
            acc = jnp.broadcast_to(cb, (CHUNK, 512))
            for k in range(CONV_K):
                d = k - CONV_K // 2
                if d == 0:
                    tap = win[HALO:HALO + CHUNK, :]
                else:
                    tap = pltpu.roll(win, (-d) % (CHUNK + 2 * HALO), 0)[HALO:HALO + CHUNK, :]
                acc = acc + cw[k:k + 1, :] * tap
            return _silu(acc)

        dtts = [_softplus(dtr_s[pl.ds(r0, CHUNK), :].T[0:16, :] + bias_col) for r0 in r0s]
        a_ts = [dtt * a_col for dtt in dtts]
        acol_ts = [jnp.where(fwd_row, _dot_exact_lhs(a_t, triu), _dot_exact_lhs(a_t, tril))
                   for a_t in a_ts]
        us = [conv(c) for c in cs]
        xus = [u[:, 0:256] for u in us]
        bts = [u[:, 256:384].T.astype(BF16) for u in us]
        cbfs = [u[:, 384:512].astype(BF16) for u in us]
        xbfs = [xu.astype(BF16) for xu in xus]
        gmats = [_dot(cbf, bt) for cbf, bt in zip(cbfs, bts)]
        tms = []
        for dtt, acol_t in zip(dtts, acol_ts):
            tot = jnp.where(fwd_row1, acol_t[:, CHUNK - 1:CHUNK], acol_t[:, 0:1])
            w_t = dtt * jnp.exp(tot - acol_t)
            e_t = jnp.exp(acol_t)
            stacked = jnp.concatenate(
                [acol_t, w_t, e_t, jnp.zeros((CHUNK - 48, CHUNK), F32)], axis=0)
            tms.append(stacked.T)
        wes = [_dot(tm.astype(BF16), expand) for tm in tms]
        decs = []
        for tm in tms:
            tot_row = jnp.where(lane128_1 < 8, tm[CHUNK - 1:CHUNK, :], tm[0:1, :])
            tot512 = _dot_exact_lhs(jnp.broadcast_to(tot_row, (8, CHUNK)), expand_tot)
            decs.append(jnp.exp(tot512))
        ydiags = [jnp.zeros((CHUNK, 256), F32) for _ in cs]
        for i in range(4):
            in_head = (lane256 >= 64 * i) & (lane256 < 64 * i + 64)
            for j in range(GROUP):
                tm, acol_t, dtt = tms[j], acol_ts[j], dtts[j]
                arg = jnp.where(lower, tm[:, i:i + 1] - acol_t[i:i + 1, :],
                                tm[:, 8 + i:9 + i] - acol_t[8 + i:9 + i, :])
                scale = (jnp.where(lower, dtt[i:i + 1, :], 0.0)
                         + jnp.where(upper, dtt[8 + i:9 + i, :], 0.0))
                wmat = (gmats[j] * jnp.exp(arg) * scale).astype(BF16)
                ydiags[j] = jnp.where(in_head, _dot(wmat, xbfs[j]), ydiags[j])
        for j, c in enumerate(cs):
            w512 = wes[j][:, 0:512]
            e512 = wes[j][:, 512:1024]
            xdw = (jnp.concatenate([xus[j], xus[j]], axis=1) * w512).astype(BF16)
            sloc_s[c] = _dot(bts[j], xdw)
            dec_s[c] = decs[j]
            e_s[c] = e512
            cbf_s[c] = cbfs[j]
            yacc_s[pl.ds(r0s[j], CHUNK), :] = ydiags[j] + xus[j] * dsk
        return carry

    lax.fori_loop(0, nch // GROUP, phase_a, 0)

    fwd_order = list(range(nch))
    bwd_order = list(range(ncc - 1, -1, -1)) + list(range(nch - 1, ncc - 1, -1))
    for order, lo in ((fwd_order, 0), (bwd_order, 256)):
        state = jnp.zeros((SSD_STATE, 256), F32)
        for c in order:
            sin_s[c, :, lo:lo + 256] = state.astype(BF16)
            state = (state * dec_s[c, 0:1, lo:lo + 256]
                     + sloc_s[c, :, lo:lo + 256])

    def phase_c(cs, z_ref, zrows):
        yos = [_dot(cbf_s[c], sin_s[c]) for c in cs]
        vs = []
        for c, zrow, yo in zip(cs, zrows, yos):
            r0 = c * CHUNK if isinstance(c, int) else pl.multiple_of(c * CHUNK, CHUNK)
            yo = yo * e_s[c]
            y = yacc_s[pl.ds(r0, CHUNK), :] + yo[:, 0:256] + yo[:, 256:512]
            vs.append((r0, y * _silu(z_ref[pl.ds(zrow, CHUNK), :])))

        @pl.when(g == 0)
        def _():
            for r0, v in vs:
                vun_s[pl.ds(r0, CHUNK), 0:256] = v

        @pl.when(g == 1)
        def _():
            for r0, v in vs:
                vun_s[pl.ds(r0, CHUNK), 256:512] = v

    if ctx_out:
        phase_c(list(range(ncc)), zc_ref, [c * CHUNK for c in range(ncc)])

    group_c = next(n for n in (16, 8, 4, 2, 1) if ncl % n == 0)

    def phase_c_lat(k, carry):
        ks = [k * group_c + j for j in range(group_c)]
        phase_c([kk + ncc for kk in ks], zl_ref,
                [pl.multiple_of(kk * CHUNK, CHUNK) for kk in ks])
        return carry

    lax.fori_loop(0, ncl // group_c, phase_c_lat, 0)

    @pl.when(g == SSD_GROUPS - 1)
    def _finalize():
        ng = ng_ref[...]

        def norm_rows(r0, nrows):
            v = vun_s[pl.ds(r0, nrows), :]
            ms = jnp.mean(v * v, axis=-1, keepdims=True)
            return (v * lax.rsqrt(ms + EPS) * ng).astype(BF16)

        if ctx_out:
            oc_ref[...] = norm_rows(0, n_ctx)

        def fin(k, carry):
            r0 = pl.multiple_of(k * 256, 256)
            ol_ref[pl.ds(r0, 256), :] = norm_rows(n_ctx + r0, 256)
            return carry

        lax.fori_loop(0, n_lat // 256, fin, 0)


def _ssd_call(p_c, p_l, conv_w8, conv_b, par, dsk, ng, batch, n_ctx, n_lat, ctx_out):
    nch = (n_ctx + n_lat) // CHUNK
    t_all = n_ctx + n_lat
    in_specs = [
        pl.BlockSpec((n_ctx, 256), lambda b, g: (b, g)),
        pl.BlockSpec((n_lat, 256), lambda b, g: (b, g)),
        pl.BlockSpec((n_ctx, 128), lambda b, g: (b, 4 + g)),
        pl.BlockSpec((n_lat, 128), lambda b, g: (b, 4 + g)),
        pl.BlockSpec((n_ctx, 128), lambda b, g: (b, 6 + g)),
        pl.BlockSpec((n_lat, 128), lambda b, g: (b, 6 + g)),
        pl.BlockSpec((n_ctx, 256), lambda b, g: (b, 4 + g)),
        pl.BlockSpec((n_lat, 256), lambda b, g: (b, 4 + g)),
        pl.BlockSpec((n_ctx, 128), lambda b, g: (b, 26 + g)),
        pl.BlockSpec((n_lat, 128), lambda b, g: (b, 26 + g)),
        pl.BlockSpec((8, 256), lambda b, g: (0, g)),
        pl.BlockSpec((8, 128), lambda b, g: (0, 4 + g)),
        pl.BlockSpec((8, 128), lambda b, g: (0, 6 + g)),
        pl.BlockSpec((1, 256), lambda b, g: (0, g)),
        pl.BlockSpec((1, 128), lambda b, g: (0, 4 + g)),
        pl.BlockSpec((1, 128), lambda b, g: (0, 6 + g)),
        pl.BlockSpec((None, 32, 128), lambda b, g: (g, 0, 0)),
        pl.BlockSpec((1, 256), lambda b, g: (0, g)),
        pl.BlockSpec((1, 512), lambda b, g: (0, 0)),
    ]
    args = [p_c, p_l, p_c, p_l, p_c, p_l, p_c, p_l, p_c, p_l,
            conv_w8, conv_w8, conv_w8, conv_b, conv_b, conv_b, par, dsk, ng]
    out_specs = [pl.BlockSpec((n_lat, 512), lambda b, g: (b, 0))]
    out_shape = [jax.ShapeDtypeStruct((batch * n_lat, 512), BF16)]
    if ctx_out:
        out_specs = [pl.BlockSpec((n_ctx, 512), lambda b, g: (b, 0))] + out_specs
        out_shape = [jax.ShapeDtypeStruct((batch * n_ctx, 512), BF16)] + out_shape
    scratch = [
        pltpu.VMEM((t_all + 3 * HALO, 512), F32),
        pltpu.VMEM((t_all, 128), F32),
        pltpu.VMEM((t_all, 256), F32),
        pltpu.VMEM((nch, SSD_STATE, 512), F32),
        pltpu.VMEM((nch, CHUNK, 512), F32),
        pltpu.VMEM((nch, CHUNK, 128), BF16),
        pltpu.VMEM((nch, SSD_STATE, 512), BF16),
        pltpu.VMEM((nch, 8, 512), F32),
        pltpu.VMEM((t_all, 512), F32),
    ]
    outs = pl.pallas_call(
        functools.partial(_ssd_kernel, n_ctx=n_ctx, n_lat=n_lat, ctx_out=ctx_out),
        grid=(batch, SSD_GROUPS),
        in_specs=in_specs,
        out_specs=out_specs,
        out_shape=out_shape,
        scratch_shapes=scratch,
        compiler_params=pltpu.CompilerParams(
            dimension_semantics=("arbitrary", "arbitrary"),
            vmem_limit_bytes=VMEM_LIMIT),
        name="ssd_scan",
    )(*args)
    if ctx_out:
        return outs[1], outs[0]
    return outs[0], None


def _rope_tables(n_lat):
    rows = n_lat // GRID_W
    row_idx = np.repeat(np.arange(rows), GRID_W).astype(np.float32)
    col_idx = (np.arange(rows * GRID_W) % GRID_W).astype(np.float32)

    def tables(dim, reps):
        quarter = dim // 4
        inv = (ROPE_BASE ** (-np.arange(quarter, dtype=np.float32) / quarter)).astype(np.float32)
        ang = np.concatenate([row_idx[:, None] * inv, col_idx[:, None] * inv], axis=-1)
        cos, sin = np.cos(ang.astype(np.float64)), np.sin(ang.astype(np.float64))
        cos2 = np.concatenate([cos, cos], axis=-1).astype(np.float32)
        sin2 = np.concatenate([-sin, sin], axis=-1).astype(np.float32)
        return jnp.asarray(np.tile(cos2, (1, reps))), jnp.asarray(np.tile(sin2, (1, reps)))

    cos_g, sin_g = tables(GQA_HEAD_DIM, 4)
    cos_d, sin_d = tables(DIFF_QK_DIM, 8)
    return {"cos_g": cos_g, "sin_g": sin_g, "cos_d": cos_d, "sin_d": sin_d}


def kernel(x, c, ctx, c_ctx, w_mod, b_mod, g_pre, g_post, w_in, conv_w, conv_b,
           a_log_fwd, a_log_bwd, dt_bias_fwd, dt_bias_bwd, d_skip, ssd_norm_g,
           q_norm_g, k_norm_g, diff_lambda, diff_norm_g, w_out):
    batch, n_lat, _ = x.shape
    n_ctx = ctx.shape[1]
    depth = w_mod.shape[0]
    assert n_lat % 512 == 0 and n_ctx % 256 == 0 and (batch * n_ctx) % 512 == 0
    assert batch + 1 <= 16

    in_perm, out_perm = _in_col_perm(), _out_row_perm()
    w_in_p = [_take_runs(w_in[l], in_perm, 1, IN_COLS).astype(BF16) for l in range(depth)]
    w_out_p = [_take_runs(w_out[l], out_perm, 0, None).astype(BF16) for l in range(depth)]
    conv_w8 = jnp.pad(conv_w, ((0, 0), (0, 8 - CONV_K), (0, 0)))
    conv_b1 = conv_b[:, None, :]

    def group16(fwd, bwd):
        out = jnp.zeros((depth, SSD_GROUPS, 16), F32)
        for g in range(SSD_GROUPS):
            out = out.at[:, g, 0:4].set(fwd[:, 4 * g:4 * g + 4])
            out = out.at[:, g, 8:12].set(bwd[:, 4 * g:4 * g + 4])
        return out

    ssd_par = jnp.broadcast_to(
        jnp.concatenate([group16(a_log_fwd, a_log_bwd),
                         group16(dt_bias_fwd, dt_bias_bwd)], axis=-1)[..., None],
        (depth, SSD_GROUPS, 32, 128))
    dsk = jnp.repeat(d_skip, SSD_HEAD_DIM, axis=1)[:, None, :]
    qg = jnp.tile(q_norm_g, (1, 4))[:, None, :]
    kg = jnp.tile(k_norm_g, (1, 2))[:, None, :]
    dng = jnp.tile(diff_norm_g, (1, 4))[:, None, :]
    tabs = _rope_tables(n_lat)

    cs = jnp.concatenate(
        [c, c_ctx[None, :], jnp.zeros((16 - batch - 1, D_MODEL), F32)], axis=0)
    mod_all = _mod_call(cs, w_mod, b_mod)

    h = x.reshape(batch * n_lat, D_MODEL)
    hc = ctx.reshape(batch * n_ctx, D_MODEL)
    for l in range(depth):
        ctx_out = l < depth - 1
        lam_init = 0.8 - 0.6 * float(np.exp(-0.3 * l))
        mod3 = mod_all[l][:, None, :]
        p_l = _inproj_call(h, mod3, g_pre[l][None, :], w_in_p[l], n_lat // 512, None)
        p_c = _inproj_call(hc, mod3, g_pre[l][None, :], w_in_p[l], None, batch)

        ys_l, ys_c = _ssd_call(p_c, p_l, conv_w8[l], conv_b1[l], ssd_par[l], dsk[l],
                               ssd_norm_g[l][None, :], batch, n_ctx, n_lat, ctx_out)
        yg_l = _gqa_call(p_l, p_c, p_l, tabs, qg[l], kg[l], batch, n_ctx, n_lat, True)
        diff_args = (p_l, p_c, p_l, tabs, diff_lambda[l], dng[l], batch, n_ctx, n_lat, True,
                     lam_init)
        yd_fast, row_sum_min = _diff_call(*diff_args, True)
        yd_l = lax.cond(jnp.min(row_sum_min) >= MIN_ROW_SUM,
                        lambda: yd_fast, lambda: _diff_call(*diff_args, False))
        h = _outproj_call(ys_l, yg_l, yd_l, h, mod3, g_post[l][None, :], w_out_p[l],
                          n_lat, None)
        if ctx_out:
            yg_c = _gqa_call(p_c, p_c, None, tabs, qg[l], kg[l], batch, n_ctx, 0, False)
            yd_c = _diff_call(p_c, p_c, None, tabs, diff_lambda[l], dng[l], batch,
                              n_ctx, 0, False, lam_init, False)
            hc = _outproj_call(ys_c, yg_c, yd_c, hc, mod3, g_post[l][None, :],
                               w_out_p[l], None, batch)
    return h.reshape(batch, n_lat, D_MODEL)
```

```python
import functools

import numpy as np
import jax
import jax.numpy as jnp
from jax import lax
from jax.experimental import pallas as pl
from jax.experimental.pallas import tpu as pltpu

F32 = jnp.float32
BF16 = jnp.bfloat16

D_MODEL = 1024
GRID_W = 64
ROPE_BASE = 10000.0
EPS = 1e-6
LOG2E = 1.4426950408889634

SSD_WIDTH = 512
SSD_HEADS = 8
SSD_HEAD_DIM = 64
SSD_GROUPS = 2
SSD_STATE = 128
CHUNK = 128
CONV_K = 5
HALO = 8
GQA_HEADS = 4
GQA_HEAD_DIM = 64
DIFF_HEADS = 4
DIFF_QK_DIM = 32
DIFF_V_DIM = 64

_IN_SPLITS = (("xbc", 1024), ("z", 512), ("dt", 16), ("gq", 256), ("gk", 128),
              ("gv", 128), ("gg", 256), ("dq", 256), ("dk", 256), ("dv", 256),
              ("dg", 256))
IN_COLS = sum(s for _, s in _IN_SPLITS)
NP = 28 * 128
GQ_HEAD_ORDER = (0, 2, 1, 3)

VMEM_LIMIT = 56 * 1024 * 1024
ATTN_TQ = 1024
DIFF_TQ = 1024
ATTN_SUB = 512
DIFF_SUB = 512
ATTN_AHEAD = 2
DIFF_AHEAD = 1
SCORE_BOUND_MARGIN = 1.02
MIN_ROW_SUM = 2.0 ** -90
OUTPROJ_TM = 1024
OUTPROJ_SUB = 256


def _in_col_perm():
    off, o = {}, 0
    for name, size in _IN_SPLITS:
        off[name] = o
        o += size
    pad = IN_COLS
    cols = list(range(off["xbc"], off["xbc"] + 1024))
    cols += list(range(off["z"], off["z"] + 512))
    for h in GQ_HEAD_ORDER:
        cols += list(range(off["gq"] + 64 * h, off["gq"] + 64 * h + 64))
    cols += list(range(off["gk"], off["gk"] + 128))
    cols += list(range(off["gv"], off["gv"] + 128))
    for h in GQ_HEAD_ORDER:
        cols += list(range(off["gg"] + 64 * h, off["gg"] + 64 * h + 64))
    for name in ("dq", "dk", "dv", "dg"):
        cols += list(range(off[name], off[name] + 256))
    for g in range(SSD_GROUPS):
        blk = [pad] * 128
        for i in range(4):
            blk[i] = off["dt"] + 4 * g + i
            blk[8 + i] = off["dt"] + SSD_HEADS + 4 * g + i
        cols += blk
    assert len(cols) == NP
    return np.asarray(cols, np.int32)


def _out_row_perm():
    rows = list(range(SSD_WIDTH))
    for h in GQ_HEAD_ORDER:
        rows += list(range(SSD_WIDTH + 64 * h, SSD_WIDTH + 64 * h + 64))
    rows += list(range(SSD_WIDTH + 256, SSD_WIDTH + 512))
    return np.asarray(rows, np.int32)


def _take_runs(arr, idx, axis, pad_index):
    pieces, start = [], 0
    idx = [int(i) for i in idx]
    while start < len(idx):
        end = start + 1
        if idx[start] == pad_index:
            while end < len(idx) and idx[end] == pad_index:
                end += 1
            shape = list(arr.shape)
            shape[axis] = end - start
            pieces.append(jnp.zeros(shape, arr.dtype))
        else:
            while end < len(idx) and idx[end] == idx[end - 1] + 1 and idx[end] != pad_index:
                end += 1
            pieces.append(lax.slice_in_dim(arr, idx[start], idx[end - 1] + 1, axis=axis))
        start = end
    return jnp.concatenate(pieces, axis=axis)


def _dot(a, b):
    return jnp.dot(a, b, preferred_element_type=F32)


def _split3(x):
    hi = x.astype(BF16)
    r1 = x - hi.astype(F32)
    mid = r1.astype(BF16)
    lo = (r1 - mid.astype(F32)).astype(BF16)
    return hi, mid, lo


def _dot_exact_lhs(x, m_bf16):
    hi, mid, lo = _split3(x)
    return _dot(hi, m_bf16) + _dot(mid, m_bf16) + _dot(lo, m_bf16)


def _silu(x):
    return x * jax.nn.sigmoid(x)


def _seg_ones(width, seg):
    r = lax.broadcasted_iota(jnp.int32, (width, width), 0)
    c = lax.broadcasted_iota(jnp.int32, (width, width), 1)
    same = (r & ~(seg - 1)) == (c & ~(seg - 1))
    return jnp.where(same, 1.0, 0.0).astype(BF16)


def _seg_rms(x, seg, seg_mat):
    ss = _dot_exact_lhs(x * x, seg_mat)
    return x * lax.rsqrt(ss * (1.0 / seg) + EPS)


def _rope(x, cos, sin_signed, half):
    w = x.shape[-1]
    lane = lax.broadcasted_iota(jnp.int32, x.shape, 1)
    first = (lane & (2 * half - 1)) < half
    swapped = jnp.where(first, pltpu.roll(x, w - half, 1), pltpu.roll(x, half, 1))
    return x * cos + swapped * sin_signed


def _mod_kernel(cs_ref, w_ref, b_ref, o_ref):
    s = _silu(cs_ref[...]).astype(BF16)
    o_ref[...] = _dot(s, w_ref[...].astype(BF16)) + b_ref[...]


def _mod_call(cs, w_mod, b_mod):
    depth = w_mod.shape[0]
    nrow = cs.shape[0]
    tn = 1024
    return pl.pallas_call(
        _mod_kernel,
        grid=(depth, 3 * D_MODEL // tn),
        in_specs=[
            pl.BlockSpec((nrow, D_MODEL), lambda l, j: (0, 0)),
            pl.BlockSpec((None, D_MODEL, tn), lambda l, j: (l, 0, j)),
            pl.BlockSpec((None, 1, tn), lambda l, j: (l, 0, j)),
        ],
        out_specs=pl.BlockSpec((None, nrow, tn), lambda l, j: (l, 0, j)),
        out_shape=jax.ShapeDtypeStruct((depth, nrow, 3 * D_MODEL), F32),
        compiler_params=pltpu.CompilerParams(
            dimension_semantics=("arbitrary", "arbitrary")),
        name="mod_proj",
    )(cs, w_mod, b_mod.reshape(depth, 1, 3 * D_MODEL))


def _inproj_kernel(h_ref, mod_ref, g_ref, w_ref, o_ref):
    x = h_ref[...]
    ms = jnp.mean(x * x, axis=-1, keepdims=True)
    y = x * lax.rsqrt(ms + EPS) * g_ref[...]
    sh = mod_ref[:, 0:D_MODEL]
    sc = mod_ref[:, D_MODEL:2 * D_MODEL]
    u = (y * (1.0 + sc) + sh).astype(BF16)
    tn = 512
    for j in range(NP // tn):
        o_ref[:, j * tn:(j + 1) * tn] = _dot(u, w_ref[:, j * tn:(j + 1) * tn])


def _inproj_call(h, mod3, g_pre, w_bf16, tiles_per_row, fixed_row):
    n_tok = h.shape[0]
    tm = 512
    if fixed_row is None:
        mod_idx = lambda i: (i // tiles_per_row, 0, 0)
    else:
        mod_idx = lambda i: (fixed_row, 0, 0)
    return pl.pallas_call(
        _inproj_kernel,
        grid=(n_tok // tm,),
        in_specs=[
            pl.BlockSpec((tm, D_MODEL), lambda i: (i, 0)),
            pl.BlockSpec((None, 1, 3 * D_MODEL), mod_idx),
            pl.BlockSpec((1, D_MODEL), lambda i: (0, 0)),
            pl.BlockSpec((D_MODEL, NP), lambda i: (0, 0)),
        ],
        out_specs=pl.BlockSpec((tm, NP), lambda i: (i, 0)),
        out_shape=jax.ShapeDtypeStruct((n_tok, NP), F32),
        compiler_params=pltpu.CompilerParams(
            dimension_semantics=("arbitrary",), vmem_limit_bytes=VMEM_LIMIT),
        name="in_proj",
    )(h, mod3, g_pre, w_bf16)


def _outproj_kernel(ys_ref, yg_ref, yd_ref, h_ref, mod_ref, g_ref, w_ref, o_ref):
    tm = h_ref.shape[0]
    sub = min(OUTPROJ_SUB, tm)
    gt = mod_ref[:, 2 * D_MODEL:3 * D_MODEL]
    gain = g_ref[...]

    def project(r0):
        return (_dot(ys_ref[r0:r0 + sub, :], w_ref[0:512, :])
                + _dot(yg_ref[r0:r0 + sub, :], w_ref[512:768, :])
                + _dot(yd_ref[r0:r0 + sub, :], w_ref[768:1024, :]))

    o_next = project(0)
    for r0 in range(0, tm, sub):
        o = o_next
        if r0 + sub < tm:
            o_next = project(r0 + sub)
        ms = jnp.mean(o * o, axis=-1, keepdims=True)
        n = o * lax.rsqrt(ms + EPS) * gain
        o_ref[r0:r0 + sub, :] = h_ref[r0:r0 + sub, :] + gt * n


def _outproj_call(ys, yg, yd, h, mod3, g_post, w_bf16, rows_per_mod, fixed_row):
    n_tok = h.shape[0]
    tm = OUTPROJ_TM
    assert n_tok % tm == 0
    if fixed_row is None:
        assert rows_per_mod % tm == 0
        mod_idx = lambda i: (i // (rows_per_mod // tm), 0, 0)
    else:
        mod_idx = lambda i: (fixed_row, 0, 0)
    return pl.pallas_call(
        _outproj_kernel,
        grid=(n_tok // tm,),
        in_specs=[
            pl.BlockSpec((tm, 512), lambda i: (i, 0)),
            pl.BlockSpec((tm, 256), lambda i: (i, 0)),
            pl.BlockSpec((tm, 256), lambda i: (i, 0)),
            pl.BlockSpec((tm, D_MODEL), lambda i: (i, 0)),
            pl.BlockSpec((None, 1, 3 * D_MODEL), mod_idx),
            pl.BlockSpec((1, D_MODEL), lambda i: (0, 0)),
            pl.BlockSpec((D_MODEL, D_MODEL), lambda i: (0, 0)),
        ],
        out_specs=pl.BlockSpec((tm, D_MODEL), lambda i: (i, 0)),
        out_shape=jax.ShapeDtypeStruct((n_tok, D_MODEL), F32),
        compiler_params=pltpu.CompilerParams(
            dimension_semantics=("arbitrary",), vmem_limit_bytes=VMEM_LIMIT),
        name="out_proj",
    )(ys, yg, yd, h, mod3, g_post, w_bf16)


def _attend_many(lhs_list, kt_ref, vext_refs):
    def scores(i):
        s = _dot(lhs_list[i], kt_ref[...])
        return s, jnp.max(s, axis=-1, keepdims=True)

    outs = []
    n = len(lhs_list)
    ahead = [scores(i) for i in range(min(ATTN_AHEAD, n))]
    for i, vext_ref in enumerate(vext_refs):
        s, m = ahead.pop(0)
        if i + ATTN_AHEAD < n:
            ahead.append(scores(i + ATTN_AHEAD))
        p = jnp.exp2(s - m).astype(BF16)
        oe = _dot(p, vext_ref[...])
        outs.append(oe[:, 0:128] / oe[:, 128:256])
    return outs


def _attend_diff_pairs(lhs_list, kt_ref, v_refs, lam):
    n_heads = len(v_refs)

    def scores(h):
        s_a = _dot(lhs_list[2 * h], kt_ref[...])
        m_a = jnp.max(s_a, axis=-1, keepdims=True)
        s_b = _dot(lhs_list[2 * h + 1], kt_ref[...])
        m_b = jnp.max(s_b, axis=-1, keepdims=True)
        return s_a, m_a, s_b, m_b

    outs = []
    ahead = [scores(h) for h in range(min(DIFF_AHEAD, n_heads))]
    for h in range(n_heads):
        s_a, m_a, s_b, m_b = ahead.pop(0)
        if h + DIFF_AHEAD < n_heads:
            ahead.append(scores(h + DIFF_AHEAD))
        e_a = jnp.exp2(s_a - m_a)
        e_b = jnp.exp2(s_b - m_b)
        l_a = jnp.sum(e_a, axis=-1, keepdims=True)
        l_b = jnp.sum(e_b, axis=-1, keepdims=True)
        pc = (e_a - (lam * l_a / l_b) * e_b).astype(BF16)
        outs.append(_dot(pc, v_refs[h][...]) / l_a)
    return outs


def _attend_diff_pairs_bounded(lhs_list, bounds, kt_ref, v_refs, lam):
    n_heads = len(v_refs)

    def exps(h):
        e_a = jnp.exp2(_dot(lhs_list[2 * h], kt_ref[...]) - bounds[2 * h])
        e_b = jnp.exp2(_dot(lhs_list[2 * h + 1], kt_ref[...]) - bounds[2 * h + 1])
        return e_a, e_b

    outs, l_min = [], None
    ahead = [exps(h) for h in range(min(DIFF_AHEAD, n_heads))]
    for h in range(n_heads):
        e_a, e_b = ahead.pop(0)
        if h + DIFF_AHEAD < n_heads:
            ahead.append(exps(h + DIFF_AHEAD))
        l_a = jnp.sum(e_a, axis=-1, keepdims=True)
        l_b = jnp.sum(e_b, axis=-1, keepdims=True)
        pc = (e_a - (lam * l_a / l_b) * e_b).astype(BF16)
        outs.append(_dot(pc, v_refs[h][...]) / l_a)
        l_ab = jnp.minimum(l_a, l_b)
        l_min = l_ab if l_min is None else jnp.minimum(l_min, l_ab)
    return outs, jnp.min(l_min, axis=0, keepdims=True)


def _gqa_kernel(*refs, n_ctx, n_lat, rope_q):
    it = iter(refs)
    q_ref, gg_ref, kvc_ref = next(it), next(it), next(it)
    kvl_ref = next(it) if n_lat else None
    if rope_q:
        cosq_ref, sinq_ref = next(it), next(it)
    if n_lat:
        cosk_ref, sink_ref = next(it), next(it)
    qg_ref, kg_ref = next(it), next(it)
    y_ref = next(it)
    kt_s, vext_s = next(it), next(it)

    seg128 = _seg_ones(128, 64)

    @pl.when(pl.program_id(1) == 0)
    def _prep_kv():
        kc = _seg_rms(kvc_ref[:, 0:128], 64, seg128) * kg_ref[...]
        kt_s[:, 0:n_ctx] = kc.T.astype(BF16)
        vext_s[0:n_ctx, 0:128] = kvc_ref[:, 128:256].astype(BF16)
        if n_lat:
            kl = _seg_rms(kvl_ref[:, 0:128], 64, seg128) * kg_ref[...]
            kl = _rope(kl, cosk_ref[...], sink_ref[...], 32)
            kt_s[:, n_ctx:n_ctx + n_lat] = kl.T.astype(BF16)
            vext_s[n_ctx:n_ctx + n_lat, 0:128] = kvl_ref[:, 128:256].astype(BF16)
        vext_s[:, 128:256] = jnp.ones((n_ctx + n_lat, 128), BF16)

    seg256 = _seg_ones(256, 64)
    q = _seg_rms(q_ref[...], 64, seg256) * qg_ref[...]
    if rope_q:
        q = _rope(q, cosq_ref[...], sinq_ref[...], 32)
    q = q * (GQA_HEAD_DIM ** -0.5 * LOG2E)
    tq = q.shape[0]
    sub = min(ATTN_SUB, tq)
    lane = lax.broadcasted_iota(jnp.int32, (sub, 128), 1)
    lhs_list = []
    for r0 in range(0, tq, sub):
        for half in range(2):
            qh = q[r0:r0 + sub, 128 * half:128 * half + 128]
            for kv in range(2):
                in_kv = (lane >= 64 * kv) & (lane < 64 * kv + 64)
                lhs_list.append(jnp.where(in_kv, qh, 0.0).astype(BF16))
    outs = _attend_many(lhs_list, kt_s, [vext_s] * len(lhs_list))
    for j, r0 in enumerate(range(0, tq, sub)):
        for half in range(2):
            o = jnp.where(lane < 64, outs[4 * j + 2 * half], outs[4 * j + 2 * half + 1])
            gate = _silu(gg_ref[r0:r0 + sub, 128 * half:128 * half + 128])
            y_ref[r0:r0 + sub, 128 * half:128 * half + 128] = (o * gate).astype(BF16)


def _gqa_call(p_q, p_c, p_l, tabs, qg, kg, batch, n_ctx, n_lat, rope_q):
    t_total = p_q.shape[0] // batch
    tq = min(ATTN_TQ, t_total)
    nq = t_total // tq
    in_specs = [
        pl.BlockSpec((tq, 256), lambda b, i: (b * nq + i, 6)),
        pl.BlockSpec((tq, 256), lambda b, i: (b * nq + i, 8)),
        pl.BlockSpec((n_ctx, 256), lambda b, i: (b, 7)),
    ]
    args = [p_q, p_q, p_c]
    if n_lat:
        in_specs.append(pl.BlockSpec((n_lat, 256), lambda b, i: (b, 7)))
        args.append(p_l)
    if rope_q:
        in_specs += [pl.BlockSpec((tq, 256), lambda b, i: (i, 0))] * 2
        args += [tabs["cos_g"], tabs["sin_g"]]
    if n_lat:
        in_specs += [pl.BlockSpec((n_lat, 128), lambda b, i: (0, 0))] * 2
        args += [tabs["cos_g"], tabs["sin_g"]]
    in_specs += [pl.BlockSpec((1, 256), lambda b, i: (0, 0)),
                 pl.BlockSpec((1, 128), lambda b, i: (0, 0))]
    args += [qg, kg]
    s_keys = n_ctx + n_lat
    return pl.pallas_call(
        functools.partial(_gqa_kernel, n_ctx=n_ctx, n_lat=n_lat, rope_q=rope_q),
        grid=(batch, nq),
        in_specs=in_specs,
        out_specs=pl.BlockSpec((tq, 256), lambda b, i: (b * nq + i, 0)),
        out_shape=jax.ShapeDtypeStruct((p_q.shape[0], 256), BF16),
        scratch_shapes=[pltpu.VMEM((128, s_keys), BF16),
                        pltpu.VMEM((s_keys, 256), BF16)],
        compiler_params=pltpu.CompilerParams(
            dimension_semantics=("arbitrary", "arbitrary"),
            vmem_limit_bytes=VMEM_LIMIT),
        name="gqa_attn",
    )(*args)


def _diff_kernel(*refs, n_ctx, n_lat, rope_q, lam_init, bounded):
    it = iter(refs)
    q_ref, dg_ref, kc_ref, vc_ref = next(it), next(it), next(it), next(it)
    if n_lat:
        kl_ref, vl_ref = next(it), next(it)
    if rope_q:
        cosq_ref, sinq_ref = next(it), next(it)
    if n_lat:
        cosk_ref, sink_ref = next(it), next(it)
    lam_ref, ng_ref = next(it), next(it)
    y_ref = next(it)
    lmin_ref = next(it) if bounded else None
    kt_s, vlo_s, vhi_s = next(it), next(it), next(it)
    kmax_s = next(it) if bounded else None
    s_keys = n_ctx + n_lat
    seg32 = _seg_ones(256, DIFF_QK_DIM)

    def map_norms(x):
        return jnp.sqrt(_dot((x * x).astype(BF16), seg32))

    @pl.when(pl.program_id(1) == 0)
    def _prep_kv():
        kc = kc_ref[...]
        kt_s[:, 0:n_ctx] = kc.T.astype(BF16)
        vlo_s[0:n_ctx, 0:128] = vc_ref[:, 0:128].astype(BF16)
        vhi_s[0:n_ctx, 0:128] = vc_ref[:, 128:256].astype(BF16)
        if bounded:
            kmax = jnp.max(map_norms(kc), axis=0, keepdims=True)
        if n_lat:
            kl = _rope(kl_ref[...], cosk_ref[...], sink_ref[...], 16)
            kt_s[:, n_ctx:s_keys] = kl.T.astype(BF16)
            vlo_s[n_ctx:s_keys, 0:128] = vl_ref[:, 0:128].astype(BF16)
            vhi_s[n_ctx:s_keys, 0:128] = vl_ref[:, 128:256].astype(BF16)
            if bounded:
                kmax = jnp.maximum(kmax, jnp.max(map_norms(kl), axis=0, keepdims=True))
        if bounded:
            kmax_s[...] = jnp.broadcast_to(kmax, kmax_s.shape)

    lp = lam_ref[...]
    lam = (jnp.exp(jnp.sum(lp[0:1, :] * lp[1:2, :], axis=-1, keepdims=True))
           - jnp.exp(jnp.sum(lp[2:3, :] * lp[3:4, :], axis=-1, keepdims=True))
           + lam_init)

    q = q_ref[...]
    if rope_q:
        q = _rope(q, cosq_ref[...], sinq_ref[...], 16)
    q = q * (DIFF_QK_DIM ** -0.5 * LOG2E)
    tq = q.shape[0]
    sub = min(DIFF_SUB, tq)
    lane256 = lax.broadcasted_iota(jnp.int32, (sub, 256), 1)
    lane128 = lax.broadcasted_iota(jnp.int32, (sub, 128), 1)
    seg128 = _seg_ones(128, 64)
    if bounded:
        bound_all = map_norms(q) * kmax_s[0:1, :] * SCORE_BOUND_MARGIN
    lhs_list, v_list, bounds = [], [], []
    for r0 in range(0, tq, sub):
        for mp in range(2 * DIFF_HEADS):
            in_map = (lane256 >= 32 * mp) & (lane256 < 32 * mp + 32)
            lhs_list.append(jnp.where(in_map, q[r0:r0 + sub, :], 0.0).astype(BF16))
            if bounded:
                bounds.append(bound_all[r0:r0 + sub, 32 * mp:32 * mp + 1])
        v_list += [vlo_s, vlo_s, vhi_s, vhi_s]

    if bounded:
        heads, l_min = _attend_diff_pairs_bounded(lhs_list, bounds, kt_s, v_list, lam)
        lmin_ref[...] = jnp.broadcast_to(l_min, lmin_ref.shape)
    else:
        heads = _attend_diff_pairs(lhs_list, kt_s, v_list, lam)
    for j, r0 in enumerate(range(0, tq, sub)):
        for half in range(2):
            o = jnp.where(lane128 < 64, heads[4 * j + 2 * half], heads[4 * j + 2 * half + 1])
            n = _seg_rms(o, 64, seg128) * ng_ref[:, 128 * half:128 * half + 128]
            n = n * (1.0 - lam_init)
            gate = _silu(dg_ref[r0:r0 + sub, 128 * half:128 * half + 128])
            y_ref[r0:r0 + sub, 128 * half:128 * half + 128] = (n * gate).astype(BF16)


def _diff_call(p_q, p_c, p_l, tabs, lam_params, ng, batch, n_ctx, n_lat, rope_q,
               lam_init, bounded):
    t_total = p_q.shape[0] // batch
    tq = min(DIFF_TQ, t_total)
    nq = t_total // tq
    in_specs = [
        pl.BlockSpec((tq, 256), lambda b, i: (b * nq + i, 9)),
        pl.BlockSpec((tq, 256), lambda b, i: (b * nq + i, 12)),
        pl.BlockSpec((n_ctx, 256), lambda b, i: (b, 10)),
        pl.BlockSpec((n_ctx, 256), lambda b, i: (b, 11)),
    ]
    args = [p_q, p_q, p_c, p_c]
    if n_lat:
        in_specs += [pl.BlockSpec((n_lat, 256), lambda b, i: (b, 10)),
                     pl.BlockSpec((n_lat, 256), lambda b, i: (b, 11))]
        args += [p_l, p_l]
    if rope_q:
        in_specs += [pl.BlockSpec((tq, 256), lambda b, i: (i, 0))] * 2
        args += [tabs["cos_d"], tabs["sin_d"]]
    if n_lat:
        in_specs += [pl.BlockSpec((n_lat, 256), lambda b, i: (0, 0))] * 2
        args += [tabs["cos_d"], tabs["sin_d"]]
    in_specs += [pl.BlockSpec((4, DIFF_QK_DIM), lambda b, i: (0, 0)),
                 pl.BlockSpec((1, 256), lambda b, i: (0, 0))]
    args += [lam_params, ng]
    s_keys = n_ctx + n_lat
    out_specs = [pl.BlockSpec((tq, 256), lambda b, i: (b * nq + i, 0))]
    out_shape = [jax.ShapeDtypeStruct((p_q.shape[0], 256), BF16)]
    scratch = [pltpu.VMEM((256, s_keys), BF16),
               pltpu.VMEM((s_keys, 128), BF16),
               pltpu.VMEM((s_keys, 128), BF16)]
    if bounded:
        out_specs.append(pl.BlockSpec((None, 8, 128), lambda b, i: (b * nq + i, 0, 0)))
        out_shape.append(jax.ShapeDtypeStruct((batch * nq, 8, 128), F32))
        scratch.append(pltpu.VMEM((8, 256), F32))
    outs = pl.pallas_call(
        functools.partial(_diff_kernel, n_ctx=n_ctx, n_lat=n_lat, rope_q=rope_q,
                          lam_init=lam_init, bounded=bounded),
        grid=(batch, nq),
        in_specs=in_specs,
        out_specs=out_specs,
        out_shape=out_shape,
        scratch_shapes=scratch,
        compiler_params=pltpu.CompilerParams(
            dimension_semantics=("arbitrary", "arbitrary"),
            vmem_limit_bytes=VMEM_LIMIT),
        name="diff_attn_bounded" if bounded else "diff_attn",
    )(*args)
    return (outs[0], outs[1]) if bounded else outs[0]


def _ssd_kernel(xc_ref, xl_ref, bc_ref, bl_ref, cc_ref, cl_ref, zc_ref, zl_ref,
                dtc_ref, dtl_ref, cwx_ref, cwb_ref, cwc_ref, cbx_ref, cbb_ref,
                cbc_ref, par_ref, dsk_ref, ng_ref, *rest, n_ctx, n_lat, ctx_out):
    if ctx_out:
        oc_ref, ol_ref = rest[0], rest[1]
        rest = rest[2:]
    else:
        oc_ref, ol_ref = None, rest[0]
        rest = rest[1:]
    xp_s, dtr_s, yacc_s, sloc_s, e_s, cbf_s, sin_s, dec_s, vun_s = rest

    g = pl.program_id(1)
    ncc = n_ctx // CHUNK
    ncl = n_lat // CHUNK
    nch = ncc + ncl
    GROUP = next(n for n in (18, 9, 6, 3, 2, 1) if nch % n == 0)
    t_all = n_ctx + n_lat
    lat0 = n_ctx + 2 * HALO

    zeros_h = jnp.zeros((HALO, 512), F32)
    xp_s[0:HALO, :] = zeros_h
    xp_s[HALO:HALO + n_ctx, 0:256] = xc_ref[...]
    xp_s[HALO:HALO + n_ctx, 256:384] = bc_ref[...]
    xp_s[HALO:HALO + n_ctx, 384:512] = cc_ref[...]
    xp_s[HALO + n_ctx:lat0, :] = zeros_h
    xp_s[lat0:lat0 + n_lat, 0:256] = xl_ref[...]
    xp_s[lat0:lat0 + n_lat, 256:384] = bl_ref[...]
    xp_s[lat0:lat0 + n_lat, 384:512] = cl_ref[...]
    xp_s[lat0 + n_lat:lat0 + n_lat + HALO, :] = zeros_h

    dtr_s[0:n_ctx, :] = dtc_ref[...]
    dtr_s[n_ctx:t_all, :] = dtl_ref[...]
    a_col = -jnp.exp(par_ref[0:16, :])
    bias_col = par_ref[16:32, :]

    def _softplus(v):
        return jnp.maximum(v, 0.0) + jnp.log1p(jnp.exp(-jnp.abs(v)))

    r128 = lax.broadcasted_iota(jnp.int32, (CHUNK, CHUNK), 0)
    c128 = lax.broadcasted_iota(jnp.int32, (CHUNK, CHUNK), 1)
    lower = c128 <= r128
    upper = c128 >= r128
    tril = jnp.where(lower, 1.0, 0.0).astype(BF16)
    triu = jnp.where(upper, 1.0, 0.0).astype(BF16)
    fwd_row = lax.broadcasted_iota(jnp.int32, (16, CHUNK), 0) < 8
    fwd_row1 = lax.broadcasted_iota(jnp.int32, (16, 1), 0) < 8
    er = lax.broadcasted_iota(jnp.int32, (CHUNK, 1024), 0)
    ec = lax.broadcasted_iota(jnp.int32, (CHUNK, 1024), 1)
    src_lane = 16 + 16 * (ec >> 9) + 8 * ((ec >> 8) & 1) + ((ec >> 6) & 3)
    expand = jnp.where(er == src_lane, 1.0, 0.0).astype(BF16)
    er0 = lax.broadcasted_iota(jnp.int32, (CHUNK, 512), 0)
    ec0 = lax.broadcasted_iota(jnp.int32, (CHUNK, 512), 1)
    expand_tot = jnp.where(er0 == 8 * (ec0 >> 8) + ((ec0 >> 6) & 3), 1.0, 0.0).astype(BF16)
    lane256 = lax.broadcasted_iota(jnp.int32, (CHUNK, 256), 1)
    lane128_1 = lax.broadcasted_iota(jnp.int32, (1, CHUNK), 1)

    cw = jnp.concatenate([cwx_ref[...], cwb_ref[...], cwc_ref[...]], axis=1)
    cb = jnp.concatenate([cbx_ref[...], cbb_ref[...], cbc_ref[...]], axis=1)
    dsk = dsk_ref[...]

    def phase_a(grp, carry):
        cs = [grp * GROUP + j for j in range(GROUP)]
        r0s = [pl.multiple_of(c * CHUNK, CHUNK) for c in cs]

        def conv(c):
            wstart = pl.multiple_of(c * CHUNK + jnp.where(c >= ncc, HALO, 0), 8)
            win = xp_s[pl.ds(wstart, CHUNK + 2 * HALO), :]
            acc = jnp.broadcast_to(cb, (CHUNK, 512))
            for k in range(CONV_K):
                d = k - CONV_K // 2
                if d == 0:
                    tap = win[HALO:HALO + CHUNK, :]
                else:
                    tap = pltpu.roll(win, (-d) % (CHUNK + 2 * HALO), 0)[HALO:HALO + CHUNK, :]
                acc = acc + cw[k:k + 1, :] * tap
            return _silu(acc)

        dtts = [_softplus(dtr_s[pl.ds(r0, CHUNK), :].T[0:16, :] + bias_col) for r0 in r0s]
        a_ts = [dtt * a_col for dtt in dtts]
        acol_ts = [jnp.where(fwd_row, _dot_exact_lhs(a_t, triu), _dot_exact_lhs(a_t, tril))
                   for a_t in a_ts]
        us = [conv(c) for c in cs]
        xus = [u[:, 0:256] for u in us]
        bts = [u[:, 256:384].T.astype(BF16) for u in us]
        cbfs = [u[:, 384:512].astype(BF16) for u in us]
        xbfs = [xu.astype(BF16) for xu in xus]
        gmats = [_dot(cbf, bt) for cbf, bt in zip(cbfs, bts)]
        tms = []
        for dtt, acol_t in zip(dtts, acol_ts):
            tot = jnp.where(fwd_row1, acol_t[:, CHUNK - 1:CHUNK], acol_t[:, 0:1])
            w_t = dtt * jnp.exp(tot - acol_t)
            e_t = jnp.exp(acol_t)
            stacked = jnp.concatenate(
                [acol_t, w_t, e_t, jnp.zeros((CHUNK - 48, CHUNK), F32)], axis=0)
            tms.append(stacked.T)
        wes = [_dot(tm.astype(BF16), expand) for tm in tms]
        decs = []
        for tm in tms:
            tot_row = jnp.where(lane128_1 < 8, tm[CHUNK - 1:CHUNK, :], tm[0:1, :])
            tot512 = _dot_exact_lhs(jnp.broadcast_to(tot_row, (8, CHUNK)), expand_tot)
            decs.append(jnp.exp(tot512))
        ydiags = [jnp.zeros((CHUNK, 256), F32) for _ in cs]
        for i in range(4):
            in_head = (lane256 >= 64 * i) & (lane256 < 64 * i + 64)
            for j in range(GROUP):
                tm, acol_t, dtt = tms[j], acol_ts[j], dtts[j]
                arg = jnp.where(lower, tm[:, i:i + 1] - acol_t[i:i + 1, :],
                                tm[:, 8 + i:9 + i] - acol_t[8 + i:9 + i, :])
                scale = (jnp.where(lower, dtt[i:i + 1, :], 0.0)
                         + jnp.where(upper, dtt[8 + i:9 + i, :], 0.0))
                wmat = (gmats[j] * jnp.exp(arg) * scale).astype(BF16)
                ydiags[j] = jnp.where(in_head, _dot(wmat, xbfs[j]), ydiags[j])
        for j, c in enumerate(cs):
            w512 = wes[j][:, 0:512]
            e512 = wes[j][:, 512:1024]
            xdw = (jnp.concatenate([xus[j], xus[j]], axis=1) * w512).astype(BF16)
            sloc_s[c] = _dot(bts[j], xdw)
            dec_s[c] = decs[j]
            e_s[c] = e512
            cbf_s[c] = cbfs[j]
            yacc_s[pl.ds(r0s[j], CHUNK), :] = ydiags[j] + xus[j] * dsk
        return carry

    lax.fori_loop(0, nch // GROUP, phase_a, 0)

    fwd_order = list(range(nch))
    bwd_order = list(range(ncc - 1, -1, -1)) + list(range(nch - 1, ncc - 1, -1))
    states = [jnp.zeros((SSD_STATE, 256), F32), jnp.zeros((SSD_STATE, 256), F32)]
    for cf, cb in zip(fwd_order, bwd_order):
        for d, (c, lo) in enumerate(((cf, 0), (cb, 256))):
            sin_s[c, :, lo:lo + 256] = states[d].astype(BF16)
            states[d] = (states[d] * dec_s[c, 0:1, lo:lo + 256]
                         + sloc_s[c, :, lo:lo + 256])

    def phase_c(cs, z_ref, zrows):
        yos = [_dot(cbf_s[c], sin_s[c]) for c in cs]
        vs = []
        for c, zrow, yo in zip(cs, zrows, yos):
            r0 = c * CHUNK if isinstance(c, int) else pl.multiple_of(c * CHUNK, CHUNK)
            yo = yo * e_s[c]
            y = yacc_s[pl.ds(r0, CHUNK), :] + yo[:, 0:256] + yo[:, 256:512]
            vs.append((r0, y * _silu(z_ref[pl.ds(zrow, CHUNK), :])))

        @pl.when(g == 0)
        def _():
            for r0, v in vs:
                vun_s[pl.ds(r0, CHUNK), 0:256] = v

        @pl.when(g == 1)
        def _():
            for r0, v in vs:
                vun_s[pl.ds(r0, CHUNK), 256:512] = v

    if ctx_out:
        phase_c(list(range(ncc)), zc_ref, [c * CHUNK for c in range(ncc)])

    group_c = next(n for n in (16, 8, 4, 2, 1) if ncl % n == 0)

    def phase_c_lat(k, carry):
        ks = [k * group_c + j for j in range(group_c)]
        phase_c([kk + ncc for kk in ks], zl_ref,
                [pl.multiple_of(kk * CHUNK, CHUNK) for kk in ks])
        return carry

    lax.fori_loop(0, ncl // group_c, phase_c_lat, 0)

    @pl.when(g == SSD_GROUPS - 1)
    def _finalize():
        ng = ng_ref[...]

        def norm_rows(r0, nrows):
            v = vun_s[pl.ds(r0, nrows), :]
            ms = jnp.mean(v * v, axis=-1, keepdims=True)
            return (v * lax.rsqrt(ms + EPS) * ng).astype(BF16)

        if ctx_out:
            oc_ref[...] = norm_rows(0, n_ctx)

        fin_blocks = next(n for n in (4, 2, 1) if (n_lat // 256) % n == 0)

        def fin(k, carry):
            r0s = [pl.multiple_of((k * fin_blocks + j) * 256, 256) for j in range(fin_blocks)]
            normed = [norm_rows(n_ctx + r0, 256) for r0 in r0s]
            for r0, v in zip(r0s, normed):
                ol_ref[pl.ds(r0, 256), :] = v
            return carry

        lax.fori_loop(0, n_lat // 256 // fin_blocks, fin, 0)


def _ssd_call(p_c, p_l, conv_w8, conv_b, par, dsk, ng, batch, n_ctx, n_lat, ctx_out):
    nch = (n_ctx + n_lat) // CHUNK
    t_all = n_ctx + n_lat
    in_specs = [
        pl.BlockSpec((n_ctx, 256), lambda b, g: (b, g)),
        pl.BlockSpec((n_lat, 256), lambda b, g: (b, g)),
        pl.BlockSpec((n_ctx, 128), lambda b, g: (b, 4 + g)),
        pl.BlockSpec((n_lat, 128), lambda b, g: (b, 4 + g)),
        pl.BlockSpec((n_ctx, 128), lambda b, g: (b, 6 + g)),
        pl.BlockSpec((n_lat, 128), lambda b, g: (b, 6 + g)),
        pl.BlockSpec((n_ctx, 256), lambda b, g: (b, 4 + g)),
        pl.BlockSpec((n_lat, 256), lambda b, g: (b, 4 + g)),
        pl.BlockSpec((n_ctx, 128), lambda b, g: (b, 26 + g)),
        pl.BlockSpec((n_lat, 128), lambda b, g: (b, 26 + g)),
        pl.BlockSpec((8, 256), lambda b, g: (0, g)),
        pl.BlockSpec((8, 128), lambda b, g: (0, 4 + g)),
        pl.BlockSpec((8, 128), lambda b, g: (0, 6 + g)),
        pl.BlockSpec((1, 256), lambda b, g: (0, g)),
        pl.BlockSpec((1, 128), lambda b, g: (0, 4 + g)),
        pl.BlockSpec((1, 128), lambda b, g: (0, 6 + g)),
        pl.BlockSpec((None, 32, 128), lambda b, g: (g, 0, 0)),
        pl.BlockSpec((1, 256), lambda b, g: (0, g)),
        pl.BlockSpec((1, 512), lambda b, g: (0, 0)),
    ]
    args = [p_c, p_l, p_c, p_l, p_c, p_l, p_c, p_l, p_c, p_l,
            conv_w8, conv_w8, conv_w8, conv_b, conv_b, conv_b, par, dsk, ng]
    out_specs = [pl.BlockSpec((n_lat, 512), lambda b, g: (b, 0))]
    out_shape = [jax.ShapeDtypeStruct((batch * n_lat, 512), BF16)]
    if ctx_out:
        out_specs = [pl.BlockSpec((n_ctx, 512), lambda b, g: (b, 0))] + out_specs
        out_shape = [jax.ShapeDtypeStruct((batch * n_ctx, 512), BF16)] + out_shape
    scratch = [
        pltpu.VMEM((t_all + 3 * HALO, 512), F32),
        pltpu.VMEM((t_all, 128), F32),
        pltpu.VMEM((t_all, 256), F32),
        pltpu.VMEM((nch, SSD_STATE, 512), F32),
        pltpu.VMEM((nch, CHUNK, 512), F32),
        pltpu.VMEM((nch, CHUNK, 128), BF16),
        pltpu.VMEM((nch, SSD_STATE, 512), BF16),
        pltpu.VMEM((nch, 8, 512), F32),
        pltpu.VMEM((t_all, 512), F32),
    ]
    outs = pl.pallas_call(
        functools.partial(_ssd_kernel, n_ctx=n_ctx, n_lat=n_lat, ctx_out=ctx_out),
        grid=(batch, SSD_GROUPS),
        in_specs=in_specs,
        out_specs=out_specs,
        out_shape=out_shape,
        scratch_shapes=scratch,
        compiler_params=pltpu.CompilerParams(
            dimension_semantics=("arbitrary", "arbitrary"),
            vmem_limit_bytes=VMEM_LIMIT),
        name="ssd_scan",
    )(*args)
    if ctx_out:
        return outs[1], outs[0]
    return outs[0], None


def _rope_tables(n_lat):
    rows = n_lat // GRID_W
    row_idx = np.repeat(np.arange(rows), GRID_W).astype(np.float32)
    col_idx = (np.arange(rows * GRID_W) % GRID_W).astype(np.float32)

    def tables(dim, reps):
        quarter = dim // 4
        inv = (ROPE_BASE ** (-np.arange(quarter, dtype=np.float32) / quarter)).astype(np.float32)
        ang = np.concatenate([row_idx[:, None] * inv, col_idx[:, None] * inv], axis=-1)
        cos, sin = np.cos(ang.astype(np.float64)), np.sin(ang.astype(np.float64))
        cos2 = np.concatenate([cos, cos], axis=-1).astype(np.float32)
        sin2 = np.concatenate([-sin, sin], axis=-1).astype(np.float32)
        return jnp.asarray(np.tile(cos2, (1, reps))), jnp.asarray(np.tile(sin2, (1, reps)))

    cos_g, sin_g = tables(GQA_HEAD_DIM, 4)
    cos_d, sin_d = tables(DIFF_QK_DIM, 8)
    return {"cos_g": cos_g, "sin_g": sin_g, "cos_d": cos_d, "sin_d": sin_d}


def kernel(x, c, ctx, c_ctx, w_mod, b_mod, g_pre, g_post, w_in, conv_w, conv_b,
           a_log_fwd, a_log_bwd, dt_bias_fwd, dt_bias_bwd, d_skip, ssd_norm_g,
           q_norm_g, k_norm_g, diff_lambda, diff_norm_g, w_out):
    batch, n_lat, _ = x.shape
    n_ctx = ctx.shape[1]
    depth = w_mod.shape[0]
    assert n_lat % 512 == 0 and n_ctx % 256 == 0 and (batch * n_ctx) % 512 == 0
    assert batch + 1 <= 16

    in_perm, out_perm = _in_col_perm(), _out_row_perm()
    w_in_p = [_take_runs(w_in[l], in_perm, 1, IN_COLS).astype(BF16) for l in range(depth)]
    w_out_p = [_take_runs(w_out[l], out_perm, 0, None).astype(BF16) for l in range(depth)]
    conv_w8 = jnp.pad(conv_w, ((0, 0), (0, 8 - CONV_K), (0, 0)))
    conv_b1 = conv_b[:, None, :]

    def group16(fwd, bwd):
        out = jnp.zeros((depth, SSD_GROUPS, 16), F32)
        for g in range(SSD_GROUPS):
            out = out.at[:, g, 0:4].set(fwd[:, 4 * g:4 * g + 4])
            out = out.at[:, g, 8:12].set(bwd[:, 4 * g:4 * g + 4])
        return out

    ssd_par = jnp.broadcast_to(
        jnp.concatenate([group16(a_log_fwd, a_log_bwd),
                         group16(dt_bias_fwd, dt_bias_bwd)], axis=-1)[..., None],
        (depth, SSD_GROUPS, 32, 128))
    dsk = jnp.repeat(d_skip, SSD_HEAD_DIM, axis=1)[:, None, :]
    qg = jnp.tile(q_norm_g, (1, 4))[:, None, :]
    kg = jnp.tile(k_norm_g, (1, 2))[:, None, :]
    dng = jnp.tile(diff_norm_g, (1, 4))[:, None, :]
    tabs = _rope_tables(n_lat)

    cs = jnp.concatenate(
        [c, c_ctx[None, :], jnp.zeros((16 - batch - 1, D_MODEL), F32)], axis=0)
    mod_all = _mod_call(cs, w_mod, b_mod)

    h = x.reshape(batch * n_lat, D_MODEL)
    hc = ctx.reshape(batch * n_ctx, D_MODEL)
    for l in range(depth):
        ctx_out = l < depth - 1
        lam_init = 0.8 - 0.6 * float(np.exp(-0.3 * l))
        mod3 = mod_all[l][:, None, :]
        p_l = _inproj_call(h, mod3, g_pre[l][None, :], w_in_p[l], n_lat // 512, None)
        p_c = _inproj_call(hc, mod3, g_pre[l][None, :], w_in_p[l], None, batch)

        ys_l, ys_c = _ssd_call(p_c, p_l, conv_w8[l], conv_b1[l], ssd_par[l], dsk[l],
                               ssd_norm_g[l][None, :], batch, n_ctx, n_lat, ctx_out)
        yg_l = _gqa_call(p_l, p_c, p_l, tabs, qg[l], kg[l], batch, n_ctx, n_lat, True)
        diff_args = (p_l, p_c, p_l, tabs, diff_lambda[l], dng[l], batch, n_ctx, n_lat, True,
                     lam_init)
        yd_fast, row_sum_min = _diff_call(*diff_args, True)
        yd_l = lax.cond(jnp.min(row_sum_min) >= MIN_ROW_SUM,
                        lambda: yd_fast, lambda: _diff_call(*diff_args, False))
        h = _outproj_call(ys_l, yg_l, yd_l, h, mod3, g_post[l][None, :], w_out_p[l],
                          n_lat, None)
        if ctx_out:
            yg_c = _gqa_call(p_c, p_c, None, tabs, qg[l], kg[l], batch, n_ctx, 0, False)
            yd_c = _diff_call(p_c, p_c, None, tabs, diff_lambda[l], dng[l], batch,
                              n_ctx, 0, False, lam_init, False)
            hc = _outproj_call(ys_c, yg_c, yd_c, hc, mod3, g_post[l][None, :],
                               w_out_p[l], None, batch)
    return h.reshape(batch, n_lat, D_MODEL)
```

```python
import functools

import numpy as np
import jax
import jax.numpy as jnp
from jax import lax
from jax.experimental import pallas as pl
from jax.experimental.pallas import tpu as pltpu

F32 = jnp.float32
BF16 = jnp.bfloat16

D_MODEL = 1024
GRID_W = 64
ROPE_BASE = 10000.0
EPS = 1e-6
LOG2E = 1.4426950408889634

SSD_WIDTH = 512
SSD_HEADS = 8
SSD_HEAD_DIM = 64
SSD_GROUPS = 2
SSD_STATE = 128
CHUNK = 128
CONV_K = 5
HALO = 8
GQA_HEADS = 4
GQA_HEAD_DIM = 64
DIFF_HEADS = 4
DIFF_QK_DIM = 32
DIFF_V_DIM = 64

_IN_SPLITS = (("xbc", 1024), ("z", 512), ("dt", 16), ("gq", 256), ("gk", 128),
              ("gv", 128), ("gg", 256), ("dq", 256), ("dk", 256), ("dv", 256),
              ("dg", 256))
IN_COLS = sum(s for _, s in _IN_SPLITS)
NP = 28 * 128
GQ_HEAD_ORDER = (0, 2, 1, 3)

VMEM_LIMIT = 56 * 1024 * 1024
ATTN_TQ = 1024
DIFF_TQ = 1024
ATTN_SUB = 512
DIFF_SUB = 512
ATTN_AHEAD = 2
DIFF_AHEAD = 1
SCORE_BOUND_MARGIN = 1.02
MIN_ROW_SUM = 2.0 ** -90
OUTPROJ_TM = 1024
OUTPROJ_SUB = 256


def _in_col_perm():
    off, o = {}, 0
    for name, size in _IN_SPLITS:
        off[name] = o
        o += size
    pad = IN_COLS
    cols = list(range(off["xbc"], off["xbc"] + 1024))
    cols += list(range(off["z"], off["z"] + 512))
    for h in GQ_HEAD_ORDER:
        cols += list(range(off["gq"] + 64 * h, off["gq"] + 64 * h + 64))
    cols += list(range(off["gk"], off["gk"] + 128))
    cols += list(range(off["gv"], off["gv"] + 128))
    for h in GQ_HEAD_ORDER:
        cols += list(range(off["gg"] + 64 * h, off["gg"] + 64 * h + 64))
    for name in ("dq", "dk", "dv", "dg"):
        cols += list(range(off[name], off[name] + 256))
    for g in range(SSD_GROUPS):
        blk = [pad] * 128
        for i in range(4):
            blk[i] = off["dt"] + 4 * g + i
            blk[8 + i] = off["dt"] + SSD_HEADS + 4 * g + i
        cols += blk
    assert len(cols) == NP
    return np.asarray(cols, np.int32)


def _out_row_perm():
    rows = list(range(SSD_WIDTH))
    for h in GQ_HEAD_ORDER:
        rows += list(range(SSD_WIDTH + 64 * h, SSD_WIDTH + 64 * h + 64))
    rows += list(range(SSD_WIDTH + 256, SSD_WIDTH + 512))
    return np.asarray(rows, np.int32)


def _take_runs(arr, idx, axis, pad_index):
    pieces, start = [], 0
    idx = [int(i) for i in idx]
    while start < len(idx):
        end = start + 1
        if idx[start] == pad_index:
            while end < len(idx) and idx[end] == pad_index:
                end += 1
            shape = list(arr.shape)
            shape[axis] = end - start
            pieces.append(jnp.zeros(shape, arr.dtype))
        else:
            while end < len(idx) and idx[end] == idx[end - 1] + 1 and idx[end] != pad_index:
                end += 1
            pieces.append(lax.slice_in_dim(arr, idx[start], idx[end - 1] + 1, axis=axis))
        start = end
    return jnp.concatenate(pieces, axis=axis)


def _dot(a, b):
    return jnp.dot(a, b, preferred_element_type=F32)


def _split3(x):
    hi = x.astype(BF16)
    r1 = x - hi.astype(F32)
    mid = r1.astype(BF16)
    lo = (r1 - mid.astype(F32)).astype(BF16)
    return hi, mid, lo


def _dot_exact_lhs(x, m_bf16):
    hi, mid, lo = _split3(x)
    return _dot(hi, m_bf16) + _dot(mid, m_bf16) + _dot(lo, m_bf16)


def _silu(x):
    return x * jax.nn.sigmoid(x)


def _seg_ones(width, seg):
    r = lax.broadcasted_iota(jnp.int32, (width, width), 0)
    c = lax.broadcasted_iota(jnp.int32, (width, width), 1)
    same = (r & ~(seg - 1)) == (c & ~(seg - 1))
    return jnp.where(same, 1.0, 0.0).astype(BF16)


def _seg_rms(x, seg, seg_mat):
    ss = _dot_exact_lhs(x * x, seg_mat)
    return x * lax.rsqrt(ss * (1.0 / seg) + EPS)


def _rope(x, cos, sin_signed, half):
    w = x.shape[-1]
    lane = lax.broadcasted_iota(jnp.int32, x.shape, 1)
    first = (lane & (2 * half - 1)) < half
    swapped = jnp.where(first, pltpu.roll(x, w - half, 1), pltpu.roll(x, half, 1))
    return x * cos + swapped * sin_signed


def _mod_kernel(cs_ref, w_ref, b_ref, o_ref):
    s = _silu(cs_ref[...]).astype(BF16)
    o_ref[...] = _dot(s, w_ref[...].astype(BF16)) + b_ref[...]


def _mod_call(cs, w_mod, b_mod):
    depth = w_mod.shape[0]
    nrow = cs.shape[0]
    tn = 1024
    return pl.pallas_call(
        _mod_kernel,
        grid=(depth, 3 * D_MODEL // tn),
        in_specs=[
            pl.BlockSpec((nrow, D_MODEL), lambda l, j: (0, 0)),
            pl.BlockSpec((None, D_MODEL, tn), lambda l, j: (l, 0, j)),
            pl.BlockSpec((None, 1, tn), lambda l, j: (l, 0, j)),
        ],
        out_specs=pl.BlockSpec((None, nrow, tn), lambda l, j: (l, 0, j)),
        out_shape=jax.ShapeDtypeStruct((depth, nrow, 3 * D_MODEL), F32),
        compiler_params=pltpu.CompilerParams(
            dimension_semantics=("arbitrary", "arbitrary")),
        name="mod_proj",
    )(cs, w_mod, b_mod.reshape(depth, 1, 3 * D_MODEL))


def _inproj_kernel(h0_ref, mod0_ref, hn_ref, modn_ref, g_ref, w_ref, o_ref, u_s):
    i = pl.program_id(0)
    slot = lax.rem(i, 2)

    def modulate(h_ref, mod_ref, dst_slot):
        x = h_ref[...]
        ms = jnp.mean(x * x, axis=-1, keepdims=True)
        y = x * lax.rsqrt(ms + EPS) * g_ref[...]
        sh = mod_ref[:, 0:D_MODEL]
        sc = mod_ref[:, D_MODEL:2 * D_MODEL]
        u_s[dst_slot] = (y * (1.0 + sc) + sh).astype(BF16)

    @pl.when(i == 0)
    def _first_tile():
        modulate(h0_ref, mod0_ref, 0)

    tn = 512
    for j in range(NP // tn):
        o_ref[:, j * tn:(j + 1) * tn] = _dot(u_s[slot], w_ref[:, j * tn:(j + 1) * tn])
    modulate(hn_ref, modn_ref, 1 - slot)


def _inproj_call(h, mod3, g_pre, w_bf16, tiles_per_row, fixed_row):
    n_tok = h.shape[0]
    tm = 512
    n_tiles = n_tok // tm
    nxt = lambda i: jnp.minimum(i + 1, n_tiles - 1)
    if fixed_row is None:
        mod_row = lambda t: t // tiles_per_row
    else:
        mod_row = lambda t: fixed_row
    return pl.pallas_call(
        _inproj_kernel,
        grid=(n_tiles,),
        in_specs=[
            pl.BlockSpec((tm, D_MODEL), lambda i: (0, 0)),
            pl.BlockSpec((None, 1, 3 * D_MODEL), lambda i: (mod_row(0), 0, 0)),
            pl.BlockSpec((tm, D_MODEL), lambda i: (nxt(i), 0)),
            pl.BlockSpec((None, 1, 3 * D_MODEL), lambda i: (mod_row(nxt(i)), 0, 0)),
            pl.BlockSpec((1, D_MODEL), lambda i: (0, 0)),
            pl.BlockSpec((D_MODEL, NP), lambda i: (0, 0)),
        ],
        out_specs=pl.BlockSpec((tm, NP), lambda i: (i, 0)),
        out_shape=jax.ShapeDtypeStruct((n_tok, NP), F32),
        scratch_shapes=[pltpu.VMEM((2, tm, D_MODEL), BF16)],
        compiler_params=pltpu.CompilerParams(
            dimension_semantics=("arbitrary",), vmem_limit_bytes=VMEM_LIMIT),
        name="in_proj",
    )(h, mod3, h, mod3, g_pre, w_bf16)


def _outproj_kernel(ys_ref, yg_ref, yd_ref, h_ref, mod_ref, g_ref, w_ref, o_ref):
    tm = h_ref.shape[0]
    sub = min(OUTPROJ_SUB, tm)
    gt = mod_ref[:, 2 * D_MODEL:3 * D_MODEL]
    gain = g_ref[...]

    def project(r0):
        return (_dot(ys_ref[r0:r0 + sub, :], w_ref[0:512, :])
                + _dot(yg_ref[r0:r0 + sub, :], w_ref[512:768, :])
                + _dot(yd_ref[r0:r0 + sub, :], w_ref[768:1024, :]))

    o_next = project(0)
    for r0 in range(0, tm, sub):
        o = o_next
        if r0 + sub < tm:
            o_next = project(r0 + sub)
        ms = jnp.mean(o * o, axis=-1, keepdims=True)
        n = o * lax.rsqrt(ms + EPS) * gain
        o_ref[r0:r0 + sub, :] = h_ref[r0:r0 + sub, :] + gt * n


def _outproj_call(ys, yg, yd, h, mod3, g_post, w_bf16, rows_per_mod, fixed_row):
    n_tok = h.shape[0]
    tm = OUTPROJ_TM
    assert n_tok % tm == 0
    if fixed_row is None:
        assert rows_per_mod % tm == 0
        mod_idx = lambda i: (i // (rows_per_mod // tm), 0, 0)
    else:
        mod_idx = lambda i: (fixed_row, 0, 0)
    return pl.pallas_call(
        _outproj_kernel,
        grid=(n_tok // tm,),
        in_specs=[
            pl.BlockSpec((tm, 512), lambda i: (i, 0)),
            pl.BlockSpec((tm, 256), lambda i: (i, 0)),
            pl.BlockSpec((tm, 256), lambda i: (i, 0)),
            pl.BlockSpec((tm, D_MODEL), lambda i: (i, 0)),
            pl.BlockSpec((None, 1, 3 * D_MODEL), mod_idx),
            pl.BlockSpec((1, D_MODEL), lambda i: (0, 0)),
            pl.BlockSpec((D_MODEL, D_MODEL), lambda i: (0, 0)),
        ],
        out_specs=pl.BlockSpec((tm, D_MODEL), lambda i: (i, 0)),
        out_shape=jax.ShapeDtypeStruct((n_tok, D_MODEL), F32),
        compiler_params=pltpu.CompilerParams(
            dimension_semantics=("arbitrary",), vmem_limit_bytes=VMEM_LIMIT),
        name="out_proj",
    )(ys, yg, yd, h, mod3, g_post, w_bf16)


def _attend_many(lhs_list, kt_ref, vext_refs):
    def scores(i):
        s = _dot(lhs_list[i], kt_ref[...])
        return s, jnp.max(s, axis=-1, keepdims=True)

    outs = []
    n = len(lhs_list)
    ahead = [scores(i) for i in range(min(ATTN_AHEAD, n))]
    for i, vext_ref in enumerate(vext_refs):
        s, m = ahead.pop(0)
        if i + ATTN_AHEAD < n:
            ahead.append(scores(i + ATTN_AHEAD))
        p = jnp.exp2(s - m).astype(BF16)
        oe = _dot(p, vext_ref[...])
        outs.append(oe[:, 0:128] / oe[:, 128:256])
    return outs


def _attend_diff_pairs(lhs_list, kt_ref, v_refs, lam):
    n_heads = len(v_refs)

    def scores(h):
        s_a = _dot(lhs_list[2 * h], kt_ref[...])
        m_a = jnp.max(s_a, axis=-1, keepdims=True)
        s_b = _dot(lhs_list[2 * h + 1], kt_ref[...])
        m_b = jnp.max(s_b, axis=-1, keepdims=True)
        return s_a, m_a, s_b, m_b

    outs = []
    ahead = [scores(h) for h in range(min(DIFF_AHEAD, n_heads))]
    for h in range(n_heads):
        s_a, m_a, s_b, m_b = ahead.pop(0)
        if h + DIFF_AHEAD < n_heads:
            ahead.append(scores(h + DIFF_AHEAD))
        e_a = jnp.exp2(s_a - m_a)
        e_b = jnp.exp2(s_b - m_b)
        l_a = jnp.sum(e_a, axis=-1, keepdims=True)
        l_b = jnp.sum(e_b, axis=-1, keepdims=True)
        pc = (e_a - (lam * l_a / l_b) * e_b).astype(BF16)
        outs.append(_dot(pc, v_refs[h][...]) / l_a)
    return outs


def _attend_diff_pairs_bounded(lhs_list, bounds, kt_ref, v_refs, lam):
    n_heads = len(v_refs)

    def exps(h):
        e_a = jnp.exp2(_dot(lhs_list[2 * h], kt_ref[...]) - bounds[2 * h])
        e_b = jnp.exp2(_dot(lhs_list[2 * h + 1], kt_ref[...]) - bounds[2 * h + 1])
        return e_a, e_b

    outs, l_min = [], None
    ahead = [exps(h) for h in range(min(DIFF_AHEAD, n_heads))]
    for h in range(n_heads):
        e_a, e_b = ahead.pop(0)
        if h + DIFF_AHEAD < n_heads:
            ahead.append(exps(h + DIFF_AHEAD))
        l_a = jnp.sum(e_a, axis=-1, keepdims=True)
        l_b = jnp.sum(e_b, axis=-1, keepdims=True)
        pc = (e_a - (lam * l_a / l_b) * e_b).astype(BF16)
        outs.append(_dot(pc, v_refs[h][...]) / l_a)
        l_ab = jnp.minimum(l_a, l_b)
        l_min = l_ab if l_min is None else jnp.minimum(l_min, l_ab)
    return outs, jnp.min(l_min, axis=0, keepdims=True)


def _gqa_kernel(*refs, n_ctx, n_lat, rope_q):
    it = iter(refs)
    q_ref, gg_ref, kvc_ref = next(it), next(it), next(it)
    kvl_ref = next(it) if n_lat else None
    if rope_q:
        cosq_ref, sinq_ref = next(it), next(it)
    if n_lat:
        cosk_ref, sink_ref = next(it), next(it)
    qg_ref, kg_ref = next(it), next(it)
    y_ref = next(it)
    kt_s, vext_s = next(it), next(it)

    seg128 = _seg_ones(128, 64)

    @pl.when(pl.program_id(1) == 0)
    def _prep_kv():
        kc = _seg_rms(kvc_ref[:, 0:128], 64, seg128) * kg_ref[...]
        kt_s[:, 0:n_ctx] = kc.T.astype(BF16)
        vext_s[0:n_ctx, 0:128] = kvc_ref[:, 128:256].astype(BF16)
        if n_lat:
            kl = _seg_rms(kvl_ref[:, 0:128], 64, seg128) * kg_ref[...]
            kl = _rope(kl, cosk_ref[...], sink_ref[...], 32)
            kt_s[:, n_ctx:n_ctx + n_lat] = kl.T.astype(BF16)
            vext_s[n_ctx:n_ctx + n_lat, 0:128] = kvl_ref[:, 128:256].astype(BF16)
        vext_s[:, 128:256] = jnp.ones((n_ctx + n_lat, 128), BF16)

    seg256 = _seg_ones(256, 64)
    q = _seg_rms(q_ref[...], 64, seg256) * qg_ref[...]
    if rope_q:
        q = _rope(q, cosq_ref[...], sinq_ref[...], 32)
    q = q * (GQA_HEAD_DIM ** -0.5 * LOG2E)
    tq = q.shape[0]
    sub = min(ATTN_SUB, tq)
    lane = lax.broadcasted_iota(jnp.int32, (sub, 128), 1)
    lhs_list = []
    for r0 in range(0, tq, sub):
        for half in range(2):
            qh = q[r0:r0 + sub, 128 * half:128 * half + 128]
            for kv in range(2):
                in_kv = (lane >= 64 * kv) & (lane < 64 * kv + 64)
                lhs_list.append(jnp.where(in_kv, qh, 0.0).astype(BF16))
    outs = _attend_many(lhs_list, kt_s, [vext_s] * len(lhs_list))
    for j, r0 in enumerate(range(0, tq, sub)):
        for half in range(2):
            o = jnp.where(lane < 64, outs[4 * j + 2 * half], outs[4 * j + 2 * half + 1])
            gate = _silu(gg_ref[r0:r0 + sub, 128 * half:128 * half + 128])
            y_ref[r0:r0 + sub, 128 * half:128 * half + 128] = (o * gate).astype(BF16)


def _gqa_call(p_q, p_c, p_l, tabs, qg, kg, batch, n_ctx, n_lat, rope_q):
    t_total = p_q.shape[0] // batch
    tq = min(ATTN_TQ, t_total)
    nq = t_total // tq
    in_specs = [
        pl.BlockSpec((tq, 256), lambda b, i: (b * nq + i, 6)),
        pl.BlockSpec((tq, 256), lambda b, i: (b * nq + i, 8)),
        pl.BlockSpec((n_ctx, 256), lambda b, i: (b, 7)),
    ]
    args = [p_q, p_q, p_c]
    if n_lat:
        in_specs.append(pl.BlockSpec((n_lat, 256), lambda b, i: (b, 7)))
        args.append(p_l)
    if rope_q:
        in_specs += [pl.BlockSpec((tq, 256), lambda b, i: (i, 0))] * 2
        args += [tabs["cos_g"], tabs["sin_g"]]
    if n_lat:
        in_specs += [pl.BlockSpec((n_lat, 128), lambda b, i: (0, 0))] * 2
        args += [tabs["cos_g"], tabs["sin_g"]]
    in_specs += [pl.BlockSpec((1, 256), lambda b, i: (0, 0)),
                 pl.BlockSpec((1, 128), lambda b, i: (0, 0))]
    args += [qg, kg]
    s_keys = n_ctx + n_lat
    return pl.pallas_call(
        functools.partial(_gqa_kernel, n_ctx=n_ctx, n_lat=n_lat, rope_q=rope_q),
        grid=(batch, nq),
        in_specs=in_specs,
        out_specs=pl.BlockSpec((tq, 256), lambda b, i: (b * nq + i, 0)),
        out_shape=jax.ShapeDtypeStruct((p_q.shape[0], 256), BF16),
        scratch_shapes=[pltpu.VMEM((128, s_keys), BF16),
                        pltpu.VMEM((s_keys, 256), BF16)],
        compiler_params=pltpu.CompilerParams(
            dimension_semantics=("arbitrary", "arbitrary"),
            vmem_limit_bytes=VMEM_LIMIT),
        name="gqa_attn",
    )(*args)


def _diff_kernel(*refs, n_ctx, n_lat, rope_q, lam_init, bounded):
    it = iter(refs)
    q_ref, dg_ref, kc_ref, vc_ref = next(it), next(it), next(it), next(it)
    if n_lat:
        kl_ref, vl_ref = next(it), next(it)
    if rope_q:
        cosq_ref, sinq_ref = next(it), next(it)
    if n_lat:
        cosk_ref, sink_ref = next(it), next(it)
    lam_ref, ng_ref = next(it), next(it)
    y_ref = next(it)
    lmin_ref = next(it) if bounded else None
    kt_s, vlo_s, vhi_s = next(it), next(it), next(it)
    kmax_s = next(it) if bounded else None
    s_keys = n_ctx + n_lat
    seg32 = _seg_ones(256, DIFF_QK_DIM)

    def map_norms(x):
        return jnp.sqrt(_dot((x * x).astype(BF16), seg32))

    @pl.when(pl.program_id(1) == 0)
    def _prep_kv():
        kc = kc_ref[...]
        kt_s[:, 0:n_ctx] = kc.T.astype(BF16)
        vlo_s[0:n_ctx, 0:128] = vc_ref[:, 0:128].astype(BF16)
        vhi_s[0:n_ctx, 0:128] = vc_ref[:, 128:256].astype(BF16)
        if bounded:
            kmax = jnp.max(map_norms(kc), axis=0, keepdims=True)
        if n_lat:
            kl = _rope(kl_ref[...], cosk_ref[...], sink_ref[...], 16)
            kt_s[:, n_ctx:s_keys] = kl.T.astype(BF16)
            vlo_s[n_ctx:s_keys, 0:128] = vl_ref[:, 0:128].astype(BF16)
            vhi_s[n_ctx:s_keys, 0:128] = vl_ref[:, 128:256].astype(BF16)
            if bounded:
                kmax = jnp.maximum(kmax, jnp.max(map_norms(kl), axis=0, keepdims=True))
        if bounded:
            kmax_s[...] = jnp.broadcast_to(kmax, kmax_s.shape)

    lp = lam_ref[...]
    lam = (jnp.exp(jnp.sum(lp[0:1, :] * lp[1:2, :], axis=-1, keepdims=True))
           - jnp.exp(jnp.sum(lp[2:3, :] * lp[3:4, :], axis=-1, keepdims=True))
           + lam_init)

    q = q_ref[...]
    if rope_q:
        q = _rope(q, cosq_ref[...], sinq_ref[...], 16)
    q = q * (DIFF_QK_DIM ** -0.5 * LOG2E)
    tq = q.shape[0]
    sub = min(DIFF_SUB, tq)
    lane256 = lax.broadcasted_iota(jnp.int32, (sub, 256), 1)
    lane128 = lax.broadcasted_iota(jnp.int32, (sub, 128), 1)
    seg128 = _seg_ones(128, 64)
    if bounded:
        bound_all = map_norms(q) * kmax_s[0:1, :] * SCORE_BOUND_MARGIN
    lhs_list, v_list, bounds = [], [], []
    for r0 in range(0, tq, sub):
        for mp in range(2 * DIFF_HEADS):
            in_map = (lane256 >= 32 * mp) & (lane256 < 32 * mp + 32)
            lhs_list.append(jnp.where(in_map, q[r0:r0 + sub, :], 0.0).astype(BF16))
            if bounded:
                bounds.append(bound_all[r0:r0 + sub, 32 * mp:32 * mp + 1])
        v_list += [vlo_s, vlo_s, vhi_s, vhi_s]

    if bounded:
        heads, l_min = _attend_diff_pairs_bounded(lhs_list, bounds, kt_s, v_list, lam)
        lmin_ref[...] = jnp.broadcast_to(l_min, lmin_ref.shape)
    else:
        heads = _attend_diff_pairs(lhs_list, kt_s, v_list, lam)
    for j, r0 in enumerate(range(0, tq, sub)):
        for half in range(2):
            o = jnp.where(lane128 < 64, heads[4 * j + 2 * half], heads[4 * j + 2 * half + 1])
            n = _seg_rms(o, 64, seg128) * ng_ref[:, 128 * half:128 * half + 128]
            n = n * (1.0 - lam_init)
            gate = _silu(dg_ref[r0:r0 + sub, 128 * half:128 * half + 128])
            y_ref[r0:r0 + sub, 128 * half:128 * half + 128] = (n * gate).astype(BF16)


def _diff_call(p_q, p_c, p_l, tabs, lam_params, ng, batch, n_ctx, n_lat, rope_q,
               lam_init, bounded):
    t_total = p_q.shape[0] // batch
    tq = min(DIFF_TQ, t_total)
    nq = t_total // tq
    in_specs = [
        pl.BlockSpec((tq, 256), lambda b, i: (b * nq + i, 9)),
        pl.BlockSpec((tq, 256), lambda b, i: (b * nq + i, 12)),
        pl.BlockSpec((n_ctx, 256), lambda b, i: (b, 10)),
        pl.BlockSpec((n_ctx, 256), lambda b, i: (b, 11)),
    ]
    args = [p_q, p_q, p_c, p_c]
    if n_lat:
        in_specs += [pl.BlockSpec((n_lat, 256), lambda b, i: (b, 10)),
                     pl.BlockSpec((n_lat, 256), lambda b, i: (b, 11))]
        args += [p_l, p_l]
    if rope_q:
        in_specs += [pl.BlockSpec((tq, 256), lambda b, i: (i, 0))] * 2
        args += [tabs["cos_d"], tabs["sin_d"]]
    if n_lat:
        in_specs += [pl.BlockSpec((n_lat, 256), lambda b, i: (0, 0))] * 2
        args += [tabs["cos_d"], tabs["sin_d"]]
    in_specs += [pl.BlockSpec((4, DIFF_QK_DIM), lambda b, i: (0, 0)),
                 pl.BlockSpec((1, 256), lambda b, i: (0, 0))]
    args += [lam_params, ng]
    s_keys = n_ctx + n_lat
    out_specs = [pl.BlockSpec((tq, 256), lambda b, i: (b * nq + i, 0))]
    out_shape = [jax.ShapeDtypeStruct((p_q.shape[0], 256), BF16)]
    scratch = [pltpu.VMEM((256, s_keys), BF16),
               pltpu.VMEM((s_keys, 128), BF16),
               pltpu.VMEM((s_keys, 128), BF16)]
    if bounded:
        out_specs.append(pl.BlockSpec((None, 8, 128), lambda b, i: (b * nq + i, 0, 0)))
        out_shape.append(jax.ShapeDtypeStruct((batch * nq, 8, 128), F32))
        scratch.append(pltpu.VMEM((8, 256), F32))
    outs = pl.pallas_call(
        functools.partial(_diff_kernel, n_ctx=n_ctx, n_lat=n_lat, rope_q=rope_q,
                          lam_init=lam_init, bounded=bounded),
        grid=(batch, nq),
        in_specs=in_specs,
        out_specs=out_specs,
        out_shape=out_shape,
        scratch_shapes=scratch,
        compiler_params=pltpu.CompilerParams(
            dimension_semantics=("arbitrary", "arbitrary"),
            vmem_limit_bytes=VMEM_LIMIT),
        name="diff_attn_bounded" if bounded else "diff_attn",
    )(*args)
    return (outs[0], outs[1]) if bounded else outs[0]


def _ssd_kernel(xc_ref, xl_ref, bc_ref, bl_ref, cc_ref, cl_ref, zc_ref, zl_ref,
                dtc_ref, dtl_ref, cwx_ref, cwb_ref, cwc_ref, cbx_ref, cbb_ref,
                cbc_ref, par_ref, dsk_ref, ng_ref, *rest, n_ctx, n_lat, ctx_out):
    if ctx_out:
        oc_ref, ol_ref = rest[0], rest[1]
        rest = rest[2:]
    else:
        oc_ref, ol_ref = None, rest[0]
        rest = rest[1:]
    xp_s, dtr_s, yacc_s, sloc_s, e_s, cbf_s, sin_s, dec_s, vun_s = rest

    g = pl.program_id(1)
    ncc = n_ctx // CHUNK
    ncl = n_lat // CHUNK
    nch = ncc + ncl
    GROUP = next(n for n in (18, 9, 6, 3, 2, 1) if nch % n == 0)
    t_all = n_ctx + n_lat
    lat0 = n_ctx + 2 * HALO

    zeros_h = jnp.zeros((HALO, 512), F32)
    xp_s[0:HALO, :] = zeros_h
    xp_s[HALO:HALO + n_ctx, 0:256] = xc_ref[...]
    xp_s[HALO:HALO + n_ctx, 256:384] = bc_ref[...]
    xp_s[HALO:HALO + n_ctx, 384:512] = cc_ref[...]
    xp_s[HALO + n_ctx:lat0, :] = zeros_h
    xp_s[lat0:lat0 + n_lat, 0:256] = xl_ref[...]
    xp_s[lat0:lat0 + n_lat, 256:384] = bl_ref[...]
    xp_s[lat0:lat0 + n_lat, 384:512] = cl_ref[...]
    xp_s[lat0 + n_lat:lat0 + n_lat + HALO, :] = zeros_h

    dtr_s[0:n_ctx, :] = dtc_ref[...]
    dtr_s[n_ctx:t_all, :] = dtl_ref[...]
    a_col = -jnp.exp(par_ref[0:16, :])
    bias_col = par_ref[16:32, :]

    def _softplus(v):
        return jnp.maximum(v, 0.0) + jnp.log1p(jnp.exp(-jnp.abs(v)))

    r128 = lax.broadcasted_iota(jnp.int32, (CHUNK, CHUNK), 0)
    c128 = lax.broadcasted_iota(jnp.int32, (CHUNK, CHUNK), 1)
    lower = c128 <= r128
    upper = c128 >= r128
    tril = jnp.where(lower, 1.0, 0.0).astype(BF16)
    triu = jnp.where(upper, 1.0, 0.0).astype(BF16)
    fwd_row = lax.broadcasted_iota(jnp.int32, (16, CHUNK), 0) < 8
    fwd_row1 = lax.broadcasted_iota(jnp.int32, (16, 1), 0) < 8
    er = lax.broadcasted_iota(jnp.int32, (CHUNK, 1024), 0)
    ec = lax.broadcasted_iota(jnp.int32, (CHUNK, 1024), 1)
    src_lane = 16 + 16 * (ec >> 9) + 8 * ((ec >> 8) & 1) + ((ec >> 6) & 3)
    expand = jnp.where(er == src_lane, 1.0, 0.0).astype(BF16)
    er0 = lax.broadcasted_iota(jnp.int32, (CHUNK, 512), 0)
    ec0 = lax.broadcasted_iota(jnp.int32, (CHUNK, 512), 1)
    expand_tot = jnp.where(er0 == 8 * (ec0 >> 8) + ((ec0 >> 6) & 3), 1.0, 0.0).astype(BF16)
    lane256 = lax.broadcasted_iota(jnp.int32, (CHUNK, 256), 1)
    lane128_1 = lax.broadcasted_iota(jnp.int32, (1, CHUNK), 1)

    cw = jnp.concatenate([cwx_ref[...], cwb_ref[...], cwc_ref[...]], axis=1)
    cb = jnp.concatenate([cbx_ref[...], cbb_ref[...], cbc_ref[...]], axis=1)
    dsk = dsk_ref[...]

    def phase_a(grp, carry):
        cs = [grp * GROUP + j for j in range(GROUP)]
        r0s = [pl.multiple_of(c * CHUNK, CHUNK) for c in cs]

        def conv(c):
            wstart = pl.multiple_of(c * CHUNK + jnp.where(c >= ncc, HALO, 0), 8)
            win = xp_s[pl.ds(wstart, CHUNK + 2 * HALO), :]
            acc = jnp.broadcast_to(cb, (CHUNK, 512))
            for k in range(CONV_K):
                d = k - CONV_K // 2
                if d == 0:
                    tap = win[HALO:HALO + CHUNK, :]
                else:
                    tap = pltpu.roll(win, (-d) % (CHUNK + 2 * HALO), 0)[HALO:HALO + CHUNK, :]
                acc = acc + cw[k:k + 1, :] * tap
            return _silu(acc)

        dtts = [_softplus(dtr_s[pl.ds(r0, CHUNK), :].T[0:16, :] + bias_col) for r0 in r0s]
        a_ts = [dtt * a_col for dtt in dtts]
        acol_ts = [jnp.where(fwd_row, _dot_exact_lhs(a_t, triu), _dot_exact_lhs(a_t, tril))
                   for a_t in a_ts]
        us = [conv(c) for c in cs]
        xus = [u[:, 0:256] for u in us]
        bts = [u[:, 256:384].T.astype(BF16) for u in us]
        cbfs = [u[:, 384:512].astype(BF16) for u in us]
        xbfs = [xu.astype(BF16) for xu in xus]
        gmats = [_dot(cbf, bt) for cbf, bt in zip(cbfs, bts)]
        tms = []
        for dtt, acol_t in zip(dtts, acol_ts):
            tot = jnp.where(fwd_row1, acol_t[:, CHUNK - 1:CHUNK], acol_t[:, 0:1])
            w_t = dtt * jnp.exp(tot - acol_t)
            e_t = jnp.exp(acol_t)
            stacked = jnp.concatenate(
                [acol_t, w_t, e_t, jnp.zeros((CHUNK - 48, CHUNK), F32)], axis=0)
            tms.append(stacked.T)
        wes = [_dot(tm.astype(BF16), expand) for tm in tms]
        decs = []
        for tm in tms:
            tot_row = jnp.where(lane128_1 < 8, tm[CHUNK - 1:CHUNK, :], tm[0:1, :])
            tot512 = _dot_exact_lhs(jnp.broadcast_to(tot_row, (8, CHUNK)), expand_tot)
            decs.append(jnp.exp(tot512))
        ydiags = [jnp.zeros((CHUNK, 256), F32) for _ in cs]
        for i in range(4):
            in_head = (lane256 >= 64 * i) & (lane256 < 64 * i + 64)
            for j in range(GROUP):
                tm, acol_t, dtt = tms[j], acol_ts[j], dtts[j]
                arg = jnp.where(lower, tm[:, i:i + 1] - acol_t[i:i + 1, :],
                                tm[:, 8 + i:9 + i] - acol_t[8 + i:9 + i, :])
                scale = (jnp.where(lower, dtt[i:i + 1, :], 0.0)
                         + jnp.where(upper, dtt[8 + i:9 + i, :], 0.0))
                wmat = (gmats[j] * jnp.exp(arg) * scale).astype(BF16)
                ydiags[j] = jnp.where(in_head, _dot(wmat, xbfs[j]), ydiags[j])
        for j, c in enumerate(cs):
            w512 = wes[j][:, 0:512]
            e512 = wes[j][:, 512:1024]
            xdw = (jnp.concatenate([xus[j], xus[j]], axis=1) * w512).astype(BF16)
            sloc_s[c] = _dot(bts[j], xdw)
            dec_s[c] = decs[j]
            e_s[c] = e512
            cbf_s[c] = cbfs[j]
            yacc_s[pl.ds(r0s[j], CHUNK), :] = ydiags[j] + xus[j] * dsk
        return carry

    lax.fori_loop(0, nch // GROUP, phase_a, 0)

    fwd_order = list(range(nch))
    bwd_order = list(range(ncc - 1, -1, -1)) + list(range(nch - 1, ncc - 1, -1))
    states = [jnp.zeros((SSD_STATE, 256), F32), jnp.zeros((SSD_STATE, 256), F32)]
    for cf, cb in zip(fwd_order, bwd_order):
        for d, (c, lo) in enumerate(((cf, 0), (cb, 256))):
            sin_s[c, :, lo:lo + 256] = states[d].astype(BF16)
            states[d] = (states[d] * dec_s[c, 0:1, lo:lo + 256]
                         + sloc_s[c, :, lo:lo + 256])

    def phase_c(cs, z_ref, zrows):
        yos = [_dot(cbf_s[c], sin_s[c]) for c in cs]
        vs = []
        for c, zrow, yo in zip(cs, zrows, yos):
            r0 = c * CHUNK if isinstance(c, int) else pl.multiple_of(c * CHUNK, CHUNK)
            yo = yo * e_s[c]
            y = yacc_s[pl.ds(r0, CHUNK), :] + yo[:, 0:256] + yo[:, 256:512]
            vs.append((r0, y * _silu(z_ref[pl.ds(zrow, CHUNK), :])))

        @pl.when(g == 0)
        def _():
            for r0, v in vs:
                vun_s[pl.ds(r0, CHUNK), 0:256] = v

        @pl.when(g == 1)
        def _():
            for r0, v in vs:
                vun_s[pl.ds(r0, CHUNK), 256:512] = v

    if ctx_out:
        phase_c(list(range(ncc)), zc_ref, [c * CHUNK for c in range(ncc)])

    group_c = next(n for n in (16, 8, 4, 2, 1) if ncl % n == 0)

    def phase_c_lat(k, carry):
        ks = [k * group_c + j for j in range(group_c)]
        phase_c([kk + ncc for kk in ks], zl_ref,
                [pl.multiple_of(kk * CHUNK, CHUNK) for kk in ks])
        return carry

    lax.fori_loop(0, ncl // group_c, phase_c_lat, 0)

    @pl.when(g == SSD_GROUPS - 1)
    def _finalize():
        ng = ng_ref[...]

        def norm_rows(r0, nrows):
            v = vun_s[pl.ds(r0, nrows), :]
            ms = jnp.mean(v * v, axis=-1, keepdims=True)
            return (v * lax.rsqrt(ms + EPS) * ng).astype(BF16)

        if ctx_out:
            oc_ref[...] = norm_rows(0, n_ctx)

        fin_blocks = next(n for n in (4, 2, 1) if (n_lat // 256) % n == 0)

        def fin(k, carry):
            r0s = [pl.multiple_of((k * fin_blocks + j) * 256, 256) for j in range(fin_blocks)]
            normed = [norm_rows(n_ctx + r0, 256) for r0 in r0s]
            for r0, v in zip(r0s, normed):
                ol_ref[pl.ds(r0, 256), :] = v
            return carry

        lax.fori_loop(0, n_lat // 256 // fin_blocks, fin, 0)


def _ssd_call(p_c, p_l, conv_w8, conv_b, par, dsk, ng, batch, n_ctx, n_lat, ctx_out):
    nch = (n_ctx + n_lat) // CHUNK
    t_all = n_ctx + n_lat
    in_specs = [
        pl.BlockSpec((n_ctx, 256), lambda b, g: (b, g)),
        pl.BlockSpec((n_lat, 256), lambda b, g: (b, g)),
        pl.BlockSpec((n_ctx, 128), lambda b, g: (b, 4 + g)),
        pl.BlockSpec((n_lat, 128), lambda b, g: (b, 4 + g)),
        pl.BlockSpec((n_ctx, 128), lambda b, g: (b, 6 + g)),
        pl.BlockSpec((n_lat, 128), lambda b, g: (b, 6 + g)),
        pl.BlockSpec((n_ctx, 256), lambda b, g: (b, 4 + g)),
        pl.BlockSpec((n_lat, 256), lambda b, g: (b, 4 + g)),
        pl.BlockSpec((n_ctx, 128), lambda b, g: (b, 26 + g)),
        pl.BlockSpec((n_lat, 128), lambda b, g: (b, 26 + g)),
        pl.BlockSpec((8, 256), lambda b, g: (0, g)),
        pl.BlockSpec((8, 128), lambda b, g: (0, 4 + g)),
        pl.BlockSpec((8, 128), lambda b, g: (0, 6 + g)),
        pl.BlockSpec((1, 256), lambda b, g: (0, g)),
        pl.BlockSpec((1, 128), lambda b, g: (0, 4 + g)),
        pl.BlockSpec((1, 128), lambda b, g: (0, 6 + g)),
        pl.BlockSpec((None, 32, 128), lambda b, g: (g, 0, 0)),
        pl.BlockSpec((1, 256), lambda b, g: (0, g)),
        pl.BlockSpec((1, 512), lambda b, g: (0, 0)),
    ]
    args = [p_c, p_l, p_c, p_l, p_c, p_l, p_c, p_l, p_c, p_l,
            conv_w8, conv_w8, conv_w8, conv_b, conv_b, conv_b, par, dsk, ng]
    out_specs = [pl.BlockSpec((n_lat, 512), lambda b, g: (b, 0))]
    out_shape = [jax.ShapeDtypeStruct((batch * n_lat, 512), BF16)]
    if ctx_out:
        out_specs = [pl.BlockSpec((n_ctx, 512), lambda b, g: (b, 0))] + out_specs
        out_shape = [jax.ShapeDtypeStruct((batch * n_ctx, 512), BF16)] + out_shape
    scratch = [
        pltpu.VMEM((t_all + 3 * HALO, 512), F32),
        pltpu.VMEM((t_all, 128), F32),
        pltpu.VMEM((t_all, 256), F32),
        pltpu.VMEM((nch, SSD_STATE, 512), F32),
        pltpu.VMEM((nch, CHUNK, 512), F32),
        pltpu.VMEM((nch, CHUNK, 128), BF16),
        pltpu.VMEM((nch, SSD_STATE, 512), BF16),
        pltpu.VMEM((nch, 8, 512), F32),
        pltpu.VMEM((t_all, 512), F32),
    ]
    outs = pl.pallas_call(
        functools.partial(_ssd_kernel, n_ctx=n_ctx, n_lat=n_lat, ctx_out=ctx_out),
        grid=(batch, SSD_GROUPS),
        in_specs=in_specs,
        out_specs=out_specs,
        out_shape=out_shape,
        scratch_shapes=scratch,
        compiler_params=pltpu.CompilerParams(
            dimension_semantics=("arbitrary", "arbitrary"),
            vmem_limit_bytes=VMEM_LIMIT),
        name="ssd_scan",
    )(*args)
    if ctx_out:
        return outs[1], outs[0]
    return outs[0], None


def _rope_tables(n_lat):
    rows = n_lat // GRID_W
    row_idx = np.repeat(np.arange(rows), GRID_W).astype(np.float32)
    col_idx = (np.arange(rows * GRID_W) % GRID_W).astype(np.float32)

    def tables(dim, reps):
        quarter = dim // 4
        inv = (ROPE_BASE ** (-np.arange(quarter, dtype=np.float32) / quarter)).astype(np.float32)
        ang = np.concatenate([row_idx[:, None] * inv, col_idx[:, None] * inv], axis=-1)
        cos, sin = np.cos(ang.astype(np.float64)), np.sin(ang.astype(np.float64))
        cos2 = np.concatenate([cos, cos], axis=-1).astype(np.float32)
        sin2 = np.concatenate([-sin, sin], axis=-1).astype(np.float32)
        return jnp.asarray(np.tile(cos2, (1, reps))), jnp.asarray(np.tile(sin2, (1, reps)))

    cos_g, sin_g = tables(GQA_HEAD_DIM, 4)
    cos_d, sin_d = tables(DIFF_QK_DIM, 8)
    return {"cos_g": cos_g, "sin_g": sin_g, "cos_d": cos_d, "sin_d": sin_d}


def kernel(x, c, ctx, c_ctx, w_mod, b_mod, g_pre, g_post, w_in, conv_w, conv_b,
           a_log_fwd, a_log_bwd, dt_bias_fwd, dt_bias_bwd, d_skip, ssd_norm_g,
           q_norm_g, k_norm_g, diff_lambda, diff_norm_g, w_out):
    batch, n_lat, _ = x.shape
    n_ctx = ctx.shape[1]
    depth = w_mod.shape[0]
    assert n_lat % 512 == 0 and n_ctx % 256 == 0 and (batch * n_ctx) % 512 == 0
    assert batch + 1 <= 16

    in_perm, out_perm = _in_col_perm(), _out_row_perm()
    w_in_p = [_take_runs(w_in[l], in_perm, 1, IN_COLS).astype(BF16) for l in range(depth)]
    w_out_p = [_take_runs(w_out[l], out_perm, 0, None).astype(BF16) for l in range(depth)]
    conv_w8 = jnp.pad(conv_w, ((0, 0), (0, 8 - CONV_K), (0, 0)))
    conv_b1 = conv_b[:, None, :]

    def group16(fwd, bwd):
        out = jnp.zeros((depth, SSD_GROUPS, 16), F32)
        for g in range(SSD_GROUPS):
            out = out.at[:, g, 0:4].set(fwd[:, 4 * g:4 * g + 4])
            out = out.at[:, g, 8:12].set(bwd[:, 4 * g:4 * g + 4])
        return out

    ssd_par = jnp.broadcast_to(
        jnp.concatenate([group16(a_log_fwd, a_log_bwd),
                         group16(dt_bias_fwd, dt_bias_bwd)], axis=-1)[..., None],
        (depth, SSD_GROUPS, 32, 128))
    dsk = jnp.repeat(d_skip, SSD_HEAD_DIM, axis=1)[:, None, :]
    qg = jnp.tile(q_norm_g, (1, 4))[:, None, :]
    kg = jnp.tile(k_norm_g, (1, 2))[:, None, :]
    dng = jnp.tile(diff_norm_g, (1, 4))[:, None, :]
    tabs = _rope_tables(n_lat)

    cs = jnp.concatenate(
        [c, c_ctx[None, :], jnp.zeros((16 - batch - 1, D_MODEL), F32)], axis=0)
    mod_all = _mod_call(cs, w_mod, b_mod)

    h = x.reshape(batch * n_lat, D_MODEL)
    hc = ctx.reshape(batch * n_ctx, D_MODEL)
    for l in range(depth):
        ctx_out = l < depth - 1
        lam_init = 0.8 - 0.6 * float(np.exp(-0.3 * l))
        mod3 = mod_all[l][:, None, :]
        p_l = _inproj_call(h, mod3, g_pre[l][None, :], w_in_p[l], n_lat // 512, None)
        p_c = _inproj_call(hc, mod3, g_pre[l][None, :], w_in_p[l], None, batch)

        ys_l, ys_c = _ssd_call(p_c, p_l, conv_w8[l], conv_b1[l], ssd_par[l], dsk[l],
                               ssd_norm_g[l][None, :], batch, n_ctx, n_lat, ctx_out)
        yg_l = _gqa_call(p_l, p_c, p_l, tabs, qg[l], kg[l], batch, n_ctx, n_lat, True)
        diff_args = (p_l, p_c, p_l, tabs, diff_lambda[l], dng[l], batch, n_ctx, n_lat, True,
                     lam_init)
        yd_fast, row_sum_min = _diff_call(*diff_args, True)
        yd_l = lax.cond(jnp.min(row_sum_min) >= MIN_ROW_SUM,
                        lambda: yd_fast, lambda: _diff_call(*diff_args, False))
        h = _outproj_call(ys_l, yg_l, yd_l, h, mod3, g_post[l][None, :], w_out_p[l],
                          n_lat, None)
        if ctx_out:
            yg_c = _gqa_call(p_c, p_c, None, tabs, qg[l], kg[l], batch, n_ctx, 0, False)
            yd_c = _diff_call(p_c, p_c, None, tabs, diff_lambda[l], dng[l], batch,
                              n_ctx, 0, False, lam_init, False)
            hc = _outproj_call(ys_c, yg_c, yd_c, hc, mod3, g_post[l][None, :],
                               w_out_p[l], None, batch)
    return h.reshape(batch, n_lat, D_MODEL)
```

```python
import functools

import numpy as np
import jax
import jax.numpy as jnp
from jax import lax
from jax.experimental import pallas as pl
from jax.experimental.pallas import tpu as pltpu

F32 = jnp.float32
BF16 = jnp.bfloat16

D_MODEL = 1024
GRID_W = 64
ROPE_BASE = 10000.0
EPS = 1e-6
LOG2E = 1.4426950408889634

SSD_WIDTH = 512
SSD_HEADS = 8
SSD_HEAD_DIM = 64
SSD_GROUPS = 2
SSD_STATE = 128
CHUNK = 128
CONV_K = 5
HALO = 8
GQA_HEADS = 4
GQA_HEAD_DIM = 64
DIFF_HEADS = 4
DIFF_QK_DIM = 32
DIFF_V_DIM = 64

_IN_SPLITS = (("xbc", 1024), ("z", 512), ("dt", 16), ("gq", 256), ("gk", 128),
              ("gv", 128), ("gg", 256), ("dq", 256), ("dk", 256), ("dv", 256),
              ("dg", 256))
IN_COLS = sum(s for _, s in _IN_SPLITS)
NP = 28 * 128
GQ_HEAD_ORDER = (0, 2, 1, 3)

VMEM_LIMIT = 56 * 1024 * 1024
ATTN_TQ = 1024
DIFF_TQ = 1024
ATTN_SUB = 512
DIFF_SUB = 512
ATTN_AHEAD = 2
DIFF_AHEAD = 1
SCORE_BOUND_MARGIN = 1.02
MIN_ROW_SUM = 2.0 ** -90
OUTPROJ_TM = 1024
OUTPROJ_SUB = 256


def _in_col_perm():
    off, o = {}, 0
    for name, size in _IN_SPLITS:
        off[name] = o
        o += size
    pad = IN_COLS
    cols = list(range(off["xbc"], off["xbc"] + 1024))
    cols += list(range(off["z"], off["z"] + 512))
    for h in GQ_HEAD_ORDER:
        cols += list(range(off["gq"] + 64 * h, off["gq"] + 64 * h + 64))
    cols += list(range(off["gk"], off["gk"] + 128))
    cols += list(range(off["gv"], off["gv"] + 128))
    for h in GQ_HEAD_ORDER:
        cols += list(range(off["gg"] + 64 * h, off["gg"] + 64 * h + 64))
    for name in ("dq", "dk", "dv", "dg"):
        cols += list(range(off[name], off[name] + 256))
    for g in range(SSD_GROUPS):
        blk = [pad] * 128
        for i in range(4):
            blk[i] = off["dt"] + 4 * g + i
            blk[8 + i] = off["dt"] + SSD_HEADS + 4 * g + i
        cols += blk
    assert len(cols) == NP
    return np.asarray(cols, np.int32)


def _out_row_perm():
    rows = list(range(SSD_WIDTH))
    for h in GQ_HEAD_ORDER:
        rows += list(range(SSD_WIDTH + 64 * h, SSD_WIDTH + 64 * h + 64))
    rows += list(range(SSD_WIDTH + 256, SSD_WIDTH + 512))
    return np.asarray(rows, np.int32)


def _take_runs(arr, idx, axis, pad_index):
    pieces, start = [], 0
    idx = [int(i) for i in idx]
    while start < len(idx):
        end = start + 1
        if idx[start] == pad_index:
            while end < len(idx) and idx[end] == pad_index:
                end += 1
            shape = list(arr.shape)
            shape[axis] = end - start
            pieces.append(jnp.zeros(shape, arr.dtype))
        else:
            while end < len(idx) and idx[end] == idx[end - 1] + 1 and idx[end] != pad_index:
                end += 1
            pieces.append(lax.slice_in_dim(arr, idx[start], idx[end - 1] + 1, axis=axis))
        start = end
    return jnp.concatenate(pieces, axis=axis)


def _dot(a, b):
    return jnp.dot(a, b, preferred_element_type=F32)


def _split3(x):
    hi = x.astype(BF16)
    r1 = x - hi.astype(F32)
    mid = r1.astype(BF16)
    lo = (r1 - mid.astype(F32)).astype(BF16)
    return hi, mid, lo


def _dot_exact_lhs(x, m_bf16):
    hi, mid, lo = _split3(x)
    return _dot(hi, m_bf16) + _dot(mid, m_bf16) + _dot(lo, m_bf16)


def _silu(x):
    return x * jax.nn.sigmoid(x)


def _seg_ones(width, seg):
    r = lax.broadcasted_iota(jnp.int32, (width, width), 0)
    c = lax.broadcasted_iota(jnp.int32, (width, width), 1)
    same = (r & ~(seg - 1)) == (c & ~(seg - 1))
    return jnp.where(same, 1.0, 0.0).astype(BF16)


def _seg_rms(x, seg, seg_mat):
    ss = _dot_exact_lhs(x * x, seg_mat)
    return x * lax.rsqrt(ss * (1.0 / seg) + EPS)


def _rope(x, cos, sin_signed, half):
    w = x.shape[-1]
    lane = lax.broadcasted_iota(jnp.int32, x.shape, 1)
    first = (lane & (2 * half - 1)) < half
    swapped = jnp.where(first, pltpu.roll(x, w - half, 1), pltpu.roll(x, half, 1))
    return x * cos + swapped * sin_signed


def _mod_kernel(cs_ref, w_ref, b_ref, o_ref):
    s = _silu(cs_ref[...]).astype(BF16)
    o_ref[...] = _dot(s, w_ref[...].astype(BF16)) + b_ref[...]


def _mod_call(cs, w_mod, b_mod):
    depth = w_mod.shape[0]
    nrow = cs.shape[0]
    tn = 1024
    return pl.pallas_call(
        _mod_kernel,
        grid=(depth, 3 * D_MODEL // tn),
        in_specs=[
            pl.BlockSpec((nrow, D_MODEL), lambda l, j: (0, 0)),
            pl.BlockSpec((None, D_MODEL, tn), lambda l, j: (l, 0, j)),
            pl.BlockSpec((None, 1, tn), lambda l, j: (l, 0, j)),
        ],
        out_specs=pl.BlockSpec((None, nrow, tn), lambda l, j: (l, 0, j)),
        out_shape=jax.ShapeDtypeStruct((depth, nrow, 3 * D_MODEL), F32),
        compiler_params=pltpu.CompilerParams(
            dimension_semantics=("arbitrary", "arbitrary")),
        name="mod_proj",
    )(cs, w_mod, b_mod.reshape(depth, 1, 3 * D_MODEL))


def _inproj_kernel(h_ref, mod_ref, g_ref, w_ref, o_ref):
    x = h_ref[...]
    ms = jnp.mean(x * x, axis=-1, keepdims=True)
    y = x * lax.rsqrt(ms + EPS) * g_ref[...]
    sh = mod_ref[:, 0:D_MODEL]
    sc = mod_ref[:, D_MODEL:2 * D_MODEL]
    u = (y * (1.0 + sc) + sh).astype(BF16)
    tn = 512
    for j in range(NP // tn):
        o_ref[:, j * tn:(j + 1) * tn] = _dot(u, w_ref[:, j * tn:(j + 1) * tn])


def _inproj_call(h, mod3, g_pre, w_bf16, tiles_per_row, fixed_row):
    n_tok = h.shape[0]
    tm = 512
    if fixed_row is None:
        mod_idx = lambda i: (i // tiles_per_row, 0, 0)
    else:
        mod_idx = lambda i: (fixed_row, 0, 0)
    return pl.pallas_call(
        _inproj_kernel,
        grid=(n_tok // tm,),
        in_specs=[
            pl.BlockSpec((tm, D_MODEL), lambda i: (i, 0)),
            pl.BlockSpec((None, 1, 3 * D_MODEL), mod_idx),
            pl.BlockSpec((1, D_MODEL), lambda i: (0, 0)),
            pl.BlockSpec((D_MODEL, NP), lambda i: (0, 0)),
        ],
        out_specs=pl.BlockSpec((tm, NP), lambda i: (i, 0)),
        out_shape=jax.ShapeDtypeStruct((n_tok, NP), F32),
        compiler_params=pltpu.CompilerParams(
            dimension_semantics=("arbitrary",), vmem_limit_bytes=VMEM_LIMIT),
        name="in_proj",
    )(h, mod3, g_pre, w_bf16)


def _outproj_kernel(ys_ref, yg_ref, yd_ref, h_ref, mod_ref, g_ref, w_ref, o_ref):
    tm = h_ref.shape[0]
    sub = min(OUTPROJ_SUB, tm)
    gt = mod_ref[:, 2 * D_MODEL:3 * D_MODEL]
    gain = g_ref[...]

    def project(r0):
        return (_dot(ys_ref[r0:r0 + sub, :], w_ref[0:512, :])
                + _dot(yg_ref[r0:r0 + sub, :], w_ref[512:768, :])
                + _dot(yd_ref[r0:r0 + sub, :], w_ref[768:1024, :]))

    o_next = project(0)
    for r0 in range(0, tm, sub):
        o = o_next
        if r0 + sub < tm:
            o_next = project(r0 + sub)
        ms = jnp.mean(o * o, axis=-1, keepdims=True)
        n = o * lax.rsqrt(ms + EPS) * gain
        o_ref[r0:r0 + sub, :] = h_ref[r0:r0 + sub, :] + gt * n


def _outproj_call(ys, yg, yd, h, mod3, g_post, w_bf16, rows_per_mod, fixed_row):
    n_tok = h.shape[0]
    tm = OUTPROJ_TM
    assert n_tok % tm == 0
    if fixed_row is None:
        assert rows_per_mod % tm == 0
        mod_idx = lambda i: (i // (rows_per_mod // tm), 0, 0)
    else:
        mod_idx = lambda i: (fixed_row, 0, 0)
    return pl.pallas_call(
        _outproj_kernel,
        grid=(n_tok // tm,),
        in_specs=[
            pl.BlockSpec((tm, 512), lambda i: (i, 0)),
            pl.BlockSpec((tm, 256), lambda i: (i, 0)),
            pl.BlockSpec((tm, 256), lambda i: (i, 0)),
            pl.BlockSpec((tm, D_MODEL), lambda i: (i, 0)),
            pl.BlockSpec((None, 1, 3 * D_MODEL), mod_idx),
            pl.BlockSpec((1, D_MODEL), lambda i: (0, 0)),
            pl.BlockSpec((D_MODEL, D_MODEL), lambda i: (0, 0)),
        ],
        out_specs=pl.BlockSpec((tm, D_MODEL), lambda i: (i, 0)),
        out_shape=jax.ShapeDtypeStruct((n_tok, D_MODEL), F32),
        compiler_params=pltpu.CompilerParams(
            dimension_semantics=("arbitrary",), vmem_limit_bytes=VMEM_LIMIT),
        name="out_proj",
    )(ys, yg, yd, h, mod3, g_post, w_bf16)


def _attend_many(lhs_list, kt_ref, vext_refs):
    def scores(i):
        s = _dot(lhs_list[i], kt_ref[...])
        return s, jnp.max(s, axis=-1, keepdims=True)

    outs = []
    n = len(lhs_list)
    ahead = [scores(i) for i in range(min(ATTN_AHEAD, n))]
    for i, vext_ref in enumerate(vext_refs):
        s, m = ahead.pop(0)
        if i + ATTN_AHEAD < n:
            ahead.append(scores(i + ATTN_AHEAD))
        p = jnp.exp2(s - m).astype(BF16)
        oe = _dot(p, vext_ref[...])
        outs.append(oe[:, 0:128] / oe[:, 128:256])
    return outs


def _attend_diff_pairs(lhs_list, kt_ref, v_refs, lam):
    n_heads = len(v_refs)

    def scores(h):
        s_a = _dot(lhs_list[2 * h], kt_ref[...])
        m_a = jnp.max(s_a, axis=-1, keepdims=True)
        s_b = _dot(lhs_list[2 * h + 1], kt_ref[...])
        m_b = jnp.max(s_b, axis=-1, keepdims=True)
        return s_a, m_a, s_b, m_b

    outs = []
    ahead = [scores(h) for h in range(min(DIFF_AHEAD, n_heads))]
    for h in range(n_heads):
        s_a, m_a, s_b, m_b = ahead.pop(0)
        if h + DIFF_AHEAD < n_heads:
            ahead.append(scores(h + DIFF_AHEAD))
        e_a = jnp.exp2(s_a - m_a)
        e_b = jnp.exp2(s_b - m_b)
        l_a = jnp.sum(e_a, axis=-1, keepdims=True)
        l_b = jnp.sum(e_b, axis=-1, keepdims=True)
        pc = (e_a - (lam * l_a / l_b) * e_b).astype(BF16)
        outs.append(_dot(pc, v_refs[h][...]) / l_a)
    return outs


def _attend_diff_pairs_bounded(lhs_list, bounds, kt_ref, v_refs, lam):
    n_heads = len(v_refs)

    def exps(h):
        e_a = jnp.exp2(_dot(lhs_list[2 * h], kt_ref[...]) - bounds[2 * h])
        e_b = jnp.exp2(_dot(lhs_list[2 * h + 1], kt_ref[...]) - bounds[2 * h + 1])
        return e_a, e_b

    outs, l_min = [], None
    ahead = [exps(h) for h in range(min(DIFF_AHEAD, n_heads))]
    for h in range(n_heads):
        e_a, e_b = ahead.pop(0)
        if h + DIFF_AHEAD < n_heads:
            ahead.append(exps(h + DIFF_AHEAD))
        l_a = jnp.sum(e_a, axis=-1, keepdims=True)
        l_b = jnp.sum(e_b, axis=-1, keepdims=True)
        pc = (e_a - (lam * l_a / l_b) * e_b).astype(BF16)
        outs.append(_dot(pc, v_refs[h][...]) / l_a)
        l_ab = jnp.minimum(l_a, l_b)
        l_min = l_ab if l_min is None else jnp.minimum(l_min, l_ab)
    return outs, jnp.min(l_min, axis=0, keepdims=True)


def _gqa_kernel(*refs, n_ctx, n_lat, rope_q):
    it = iter(refs)
    q_ref, gg_ref, kvc_ref = next(it), next(it), next(it)
    kvl_ref = next(it) if n_lat else None
    if rope_q:
        cosq_ref, sinq_ref = next(it), next(it)
    if n_lat:
        cosk_ref, sink_ref = next(it), next(it)
    qg_ref, kg_ref = next(it), next(it)
    y_ref = next(it)
    kt_s, vext_s = next(it), next(it)

    seg128 = _seg_ones(128, 64)

    @pl.when(pl.program_id(1) == 0)
    def _prep_kv():
        kc = _seg_rms(kvc_ref[:, 0:128], 64, seg128) * kg_ref[...]
        kt_s[:, 0:n_ctx] = kc.T.astype(BF16)
        vext_s[0:n_ctx, 0:128] = kvc_ref[:, 128:256].astype(BF16)
        if n_lat:
            kl = _seg_rms(kvl_ref[:, 0:128], 64, seg128) * kg_ref[...]
            kl = _rope(kl, cosk_ref[...], sink_ref[...], 32)
            kt_s[:, n_ctx:n_ctx + n_lat] = kl.T.astype(BF16)
            vext_s[n_ctx:n_ctx + n_lat, 0:128] = kvl_ref[:, 128:256].astype(BF16)
        vext_s[:, 128:256] = jnp.ones((n_ctx + n_lat, 128), BF16)

    seg256 = _seg_ones(256, 64)
    q = _seg_rms(q_ref[...], 64, seg256) * qg_ref[...]
    if rope_q:
        q = _rope(q, cosq_ref[...], sinq_ref[...], 32)
    q = q * (GQA_HEAD_DIM ** -0.5 * LOG2E)
    tq = q.shape[0]
    sub = min(ATTN_SUB, tq)
    lane = lax.broadcasted_iota(jnp.int32, (sub, 128), 1)
    lhs_list = []
    for r0 in range(0, tq, sub):
        for half in range(2):
            qh = q[r0:r0 + sub, 128 * half:128 * half + 128]
            for kv in range(2):
                in_kv = (lane >= 64 * kv) & (lane < 64 * kv + 64)
                lhs_list.append(jnp.where(in_kv, qh, 0.0).astype(BF16))
    outs = _attend_many(lhs_list, kt_s, [vext_s] * len(lhs_list))
    for j, r0 in enumerate(range(0, tq, sub)):
        for half in range(2):
            o = jnp.where(lane < 64, outs[4 * j + 2 * half], outs[4 * j + 2 * half + 1])
            gate = _silu(gg_ref[r0:r0 + sub, 128 * half:128 * half + 128])
            y_ref[r0:r0 + sub, 128 * half:128 * half + 128] = (o * gate).astype(BF16)


def _gqa_call(p_q, p_c, p_l, tabs, qg, kg, batch, n_ctx, n_lat, rope_q):
    t_total = p_q.shape[0] // batch
    tq = min(ATTN_TQ, t_total)
    nq = t_total // tq
    in_specs = [
        pl.BlockSpec((tq, 256), lambda b, i: (b * nq + i, 6)),
        pl.BlockSpec((tq, 256), lambda b, i: (b * nq + i, 8)),
        pl.BlockSpec((n_ctx, 256), lambda b, i: (b, 7)),
    ]
    args = [p_q, p_q, p_c]
    if n_lat:
        in_specs.append(pl.BlockSpec((n_lat, 256), lambda b, i: (b, 7)))
        args.append(p_l)
    if rope_q:
        in_specs += [pl.BlockSpec((tq, 256), lambda b, i: (i, 0))] * 2
        args += [tabs["cos_g"], tabs["sin_g"]]
    if n_lat:
        in_specs += [pl.BlockSpec((n_lat, 128), lambda b, i: (0, 0))] * 2
        args += [tabs["cos_g"], tabs["sin_g"]]
    in_specs += [pl.BlockSpec((1, 256), lambda b, i: (0, 0)),
                 pl.BlockSpec((1, 128), lambda b, i: (0, 0))]
    args += [qg, kg]
    s_keys = n_ctx + n_lat
    return pl.pallas_call(
        functools.partial(_gqa_kernel, n_ctx=n_ctx, n_lat=n_lat, rope_q=rope_q),
        grid=(batch, nq),
        in_specs=in_specs,
        out_specs=pl.BlockSpec((tq, 256), lambda b, i: (b * nq + i, 0)),
        out_shape=jax.ShapeDtypeStruct((p_q.shape[0], 256), BF16),
        scratch_shapes=[pltpu.VMEM((128, s_keys), BF16),
                        pltpu.VMEM((s_keys, 256), BF16)],
        compiler_params=pltpu.CompilerParams(
            dimension_semantics=("arbitrary", "arbitrary"),
            vmem_limit_bytes=VMEM_LIMIT),
        name="gqa_attn",
    )(*args)


def _diff_kernel(*refs, n_ctx, n_lat, rope_q, lam_init, bounded, in_place):
    it = iter(refs)
    q_ref, dg_ref, kc_ref, vc_ref = next(it), next(it), next(it), next(it)
    if n_lat:
        kl_ref, vl_ref = next(it), next(it)
    if rope_q:
        cosq_ref, sinq_ref = next(it), next(it)
    if n_lat:
        cosk_ref, sink_ref = next(it), next(it)
    lam_ref, ng_ref = next(it), next(it)
    if in_place:
        next(it)
    y_ref = next(it)
    lmin_ref = next(it) if bounded else None
    kt_s, vlo_s, vhi_s = next(it), next(it), next(it)
    kmax_s = next(it) if bounded else None
    s_keys = n_ctx + n_lat
    seg32 = _seg_ones(256, DIFF_QK_DIM)

    def map_norms(x):
        return jnp.sqrt(_dot((x * x).astype(BF16), seg32))

    @pl.when(pl.program_id(1) == 0)
    def _prep_kv():
        kc = kc_ref[...]
        kt_s[:, 0:n_ctx] = kc.T.astype(BF16)
        vlo_s[0:n_ctx, 0:128] = vc_ref[:, 0:128].astype(BF16)
        vhi_s[0:n_ctx, 0:128] = vc_ref[:, 128:256].astype(BF16)
        if bounded:
            kmax = jnp.max(map_norms(kc), axis=0, keepdims=True)
        if n_lat:
            kl = _rope(kl_ref[...], cosk_ref[...], sink_ref[...], 16)
            kt_s[:, n_ctx:s_keys] = kl.T.astype(BF16)
            vlo_s[n_ctx:s_keys, 0:128] = vl_ref[:, 0:128].astype(BF16)
            vhi_s[n_ctx:s_keys, 0:128] = vl_ref[:, 128:256].astype(BF16)
            if bounded:
                kmax = jnp.maximum(kmax, jnp.max(map_norms(kl), axis=0, keepdims=True))
        if bounded:
            kmax_s[...] = jnp.broadcast_to(kmax, kmax_s.shape)

    lp = lam_ref[...]
    lam = (jnp.exp(jnp.sum(lp[0:1, :] * lp[1:2, :], axis=-1, keepdims=True))
           - jnp.exp(jnp.sum(lp[2:3, :] * lp[3:4, :], axis=-1, keepdims=True))
           + lam_init)

    q = q_ref[...]
    if rope_q:
        q = _rope(q, cosq_ref[...], sinq_ref[...], 16)
    q = q * (DIFF_QK_DIM ** -0.5 * LOG2E)
    tq = q.shape[0]
    sub = min(DIFF_SUB, tq)
    lane256 = lax.broadcasted_iota(jnp.int32, (sub, 256), 1)
    lane128 = lax.broadcasted_iota(jnp.int32, (sub, 128), 1)
    seg128 = _seg_ones(128, 64)
    if bounded:
        bound_all = map_norms(q) * kmax_s[0:1, :] * SCORE_BOUND_MARGIN
    lhs_list, v_list, bounds = [], [], []
    for r0 in range(0, tq, sub):
        for mp in range(2 * DIFF_HEADS):
            in_map = (lane256 >= 32 * mp) & (lane256 < 32 * mp + 32)
            lhs_list.append(jnp.where(in_map, q[r0:r0 + sub, :], 0.0).astype(BF16))
            if bounded:
                bounds.append(bound_all[r0:r0 + sub, 32 * mp:32 * mp + 1])
        v_list += [vlo_s, vlo_s, vhi_s, vhi_s]

    if bounded:
        heads, l_min = _attend_diff_pairs_bounded(lhs_list, bounds, kt_s, v_list, lam)
        lmin_ref[...] = jnp.broadcast_to(l_min, lmin_ref.shape)
    else:
        heads = _attend_diff_pairs(lhs_list, kt_s, v_list, lam)
    for j, r0 in enumerate(range(0, tq, sub)):
        for half in range(2):
            o = jnp.where(lane128 < 64, heads[4 * j + 2 * half], heads[4 * j + 2 * half + 1])
            n = _seg_rms(o, 64, seg128) * ng_ref[:, 128 * half:128 * half + 128]
            n = n * (1.0 - lam_init)
            gate = _silu(dg_ref[r0:r0 + sub, 128 * half:128 * half + 128])
            y_ref[r0:r0 + sub, 128 * half:128 * half + 128] = (n * gate).astype(BF16)


def _diff_call(p_q, p_c, p_l, tabs, lam_params, ng, batch, n_ctx, n_lat, rope_q,
               lam_init, bounded, overwrite=None):
    t_total = p_q.shape[0] // batch
    tq = min(DIFF_TQ, t_total)
    nq = t_total // tq
    in_specs = [
        pl.BlockSpec((tq, 256), lambda b, i: (b * nq + i, 9)),
        pl.BlockSpec((tq, 256), lambda b, i: (b * nq + i, 12)),
        pl.BlockSpec((n_ctx, 256), lambda b, i: (b, 10)),
        pl.BlockSpec((n_ctx, 256), lambda b, i: (b, 11)),
    ]
    args = [p_q, p_q, p_c, p_c]
    if n_lat:
        in_specs += [pl.BlockSpec((n_lat, 256), lambda b, i: (b, 10)),
                     pl.BlockSpec((n_lat, 256), lambda b, i: (b, 11))]
        args += [p_l, p_l]
    if rope_q:
        in_specs += [pl.BlockSpec((tq, 256), lambda b, i: (i, 0))] * 2
        args += [tabs["cos_d"], tabs["sin_d"]]
    if n_lat:
        in_specs += [pl.BlockSpec((n_lat, 256), lambda b, i: (0, 0))] * 2
        args += [tabs["cos_d"], tabs["sin_d"]]
    in_specs += [pl.BlockSpec((4, DIFF_QK_DIM), lambda b, i: (0, 0)),
                 pl.BlockSpec((1, 256), lambda b, i: (0, 0))]
    args += [lam_params, ng]
    aliases = {}
    if overwrite is not None:
        in_specs.append(pl.BlockSpec(memory_space=pl.ANY))
        args.append(overwrite)
        aliases = {len(args) - 1: 0}
    s_keys = n_ctx + n_lat
    out_specs = [pl.BlockSpec((tq, 256), lambda b, i: (b * nq + i, 0))]
    out_shape = [jax.ShapeDtypeStruct((p_q.shape[0], 256), BF16)]
    scratch = [pltpu.VMEM((256, s_keys), BF16),
               pltpu.VMEM((s_keys, 128), BF16),
               pltpu.VMEM((s_keys, 128), BF16)]
    if bounded:
        out_specs.append(pl.BlockSpec((None, 8, 128), lambda b, i: (b * nq + i, 0, 0)))
        out_shape.append(jax.ShapeDtypeStruct((batch * nq, 8, 128), F32))
        scratch.append(pltpu.VMEM((8, 256), F32))
    outs = pl.pallas_call(
        functools.partial(_diff_kernel, n_ctx=n_ctx, n_lat=n_lat, rope_q=rope_q,
                          lam_init=lam_init, bounded=bounded,
                          in_place=overwrite is not None),
        grid=(batch, nq),
        in_specs=in_specs,
        out_specs=out_specs,
        out_shape=out_shape,
        scratch_shapes=scratch,
        input_output_aliases=aliases,
        compiler_params=pltpu.CompilerParams(
            dimension_semantics=("arbitrary", "arbitrary"),
            vmem_limit_bytes=VMEM_LIMIT),
        name="diff_attn_bounded" if bounded else "diff_attn",
    )(*args)
    return (outs[0], outs[1]) if bounded else outs[0]


def _ssd_kernel(xc_ref, xl_ref, bc_ref, bl_ref, cc_ref, cl_ref, zc_ref, zl_ref,
                dtc_ref, dtl_ref, cwx_ref, cwb_ref, cwc_ref, cbx_ref, cbb_ref,
                cbc_ref, par_ref, dsk_ref, ng_ref, *rest, n_ctx, n_lat, ctx_out):
    if ctx_out:
        oc_ref, ol_ref = rest[0], rest[1]
        rest = rest[2:]
    else:
        oc_ref, ol_ref = None, rest[0]
        rest = rest[1:]
    xp_s, dtr_s, yacc_s, sloc_s, e_s, cbf_s, sin_s, dec_s, vun_s = rest

    g = pl.program_id(1)
    ncc = n_ctx // CHUNK
    ncl = n_lat // CHUNK
    nch = ncc + ncl
    GROUP = next(n for n in (18, 9, 6, 3, 2, 1) if nch % n == 0)
    t_all = n_ctx + n_lat
    lat0 = n_ctx + 2 * HALO

    zeros_h = jnp.zeros((HALO, 512), F32)
    xp_s[0:HALO, :] = zeros_h
    xp_s[HALO:HALO + n_ctx, 0:256] = xc_ref[...]
    xp_s[HALO:HALO + n_ctx, 256:384] = bc_ref[...]
    xp_s[HALO:HALO + n_ctx, 384:512] = cc_ref[...]
    xp_s[HALO + n_ctx:lat0, :] = zeros_h
    xp_s[lat0:lat0 + n_lat, 0:256] = xl_ref[...]
    xp_s[lat0:lat0 + n_lat, 256:384] = bl_ref[...]
    xp_s[lat0:lat0 + n_lat, 384:512] = cl_ref[...]
    xp_s[lat0 + n_lat:lat0 + n_lat + HALO, :] = zeros_h

    dtr_s[0:n_ctx, :] = dtc_ref[...]
    dtr_s[n_ctx:t_all, :] = dtl_ref[...]
    a_col = -jnp.exp(par_ref[0:16, :])
    bias_col = par_ref[16:32, :]

    def _softplus(v):
        return jnp.maximum(v, 0.0) + jnp.log1p(jnp.exp(-jnp.abs(v)))

    r128 = lax.broadcasted_iota(jnp.int32, (CHUNK, CHUNK), 0)
    c128 = lax.broadcasted_iota(jnp.int32, (CHUNK, CHUNK), 1)
    lower = c128 <= r128
    upper = c128 >= r128
    tril = jnp.where(lower, 1.0, 0.0).astype(BF16)
    triu = jnp.where(upper, 1.0, 0.0).astype(BF16)
    fwd_row = lax.broadcasted_iota(jnp.int32, (16, CHUNK), 0) < 8
    fwd_row1 = lax.broadcasted_iota(jnp.int32, (16, 1), 0) < 8
    er = lax.broadcasted_iota(jnp.int32, (CHUNK, 1024), 0)
    ec = lax.broadcasted_iota(jnp.int32, (CHUNK, 1024), 1)
    src_lane = 16 + 16 * (ec >> 9) + 8 * ((ec >> 8) & 1) + ((ec >> 6) & 3)
    expand = jnp.where(er == src_lane, 1.0, 0.0).astype(BF16)
    er0 = lax.broadcasted_iota(jnp.int32, (CHUNK, 512), 0)
    ec0 = lax.broadcasted_iota(jnp.int32, (CHUNK, 512), 1)
    expand_tot = jnp.where(er0 == 8 * (ec0 >> 8) + ((ec0 >> 6) & 3), 1.0, 0.0).astype(BF16)
    lane256 = lax.broadcasted_iota(jnp.int32, (CHUNK, 256), 1)
    lane128_1 = lax.broadcasted_iota(jnp.int32, (1, CHUNK), 1)

    cw = jnp.concatenate([cwx_ref[...], cwb_ref[...], cwc_ref[...]], axis=1)
    cb = jnp.concatenate([cbx_ref[...], cbb_ref[...], cbc_ref[...]], axis=1)
    dsk = dsk_ref[...]

    def phase_a(grp, carry):
        cs = [grp * GROUP + j for j in range(GROUP)]
        r0s = [pl.multiple_of(c * CHUNK, CHUNK) for c in cs]

        def conv(c):
            wstart = pl.multiple_of(c * CHUNK + jnp.where(c >= ncc, HALO, 0), 8)
            win = xp_s[pl.ds(wstart, CHUNK + 2 * HALO), :]
            acc = jnp.broadcast_to(cb, (CHUNK, 512))
            for k in range(CONV_K):
                d = k - CONV_K // 2
                if d == 0:
                    tap = win[HALO:HALO + CHUNK, :]
                else:
                    tap = pltpu.roll(win, (-d) % (CHUNK + 2 * HALO), 0)[HALO:HALO + CHUNK, :]
                acc = acc + cw[k:k + 1, :] * tap
            return _silu(acc)

        dtts = [_softplus(dtr_s[pl.ds(r0, CHUNK), :].T[0:16, :] + bias_col) for r0 in r0s]
        a_ts = [dtt * a_col for dtt in dtts]
        acol_ts = [jnp.where(fwd_row, _dot_exact_lhs(a_t, triu), _dot_exact_lhs(a_t, tril))
                   for a_t in a_ts]
        us = [conv(c) for c in cs]
        xus = [u[:, 0:256] for u in us]
        bts = [u[:, 256:384].T.astype(BF16) for u in us]
        cbfs = [u[:, 384:512].astype(BF16) for u in us]
        xbfs = [xu.astype(BF16) for xu in xus]
        gmats = [_dot(cbf, bt) for cbf, bt in zip(cbfs, bts)]
        tms = []
        for dtt, acol_t in zip(dtts, acol_ts):
            tot = jnp.where(fwd_row1, acol_t[:, CHUNK - 1:CHUNK], acol_t[:, 0:1])
            w_t = dtt * jnp.exp(tot - acol_t)
            e_t = jnp.exp(acol_t)
            stacked = jnp.concatenate(
                [acol_t, w_t, e_t, jnp.zeros((CHUNK - 48, CHUNK), F32)], axis=0)
            tms.append(stacked.T)
        wes = [_dot(tm.astype(BF16), expand) for tm in tms]
        decs = []
        for tm in tms:
            tot_row = jnp.where(lane128_1 < 8, tm[CHUNK - 1:CHUNK, :], tm[0:1, :])
            tot512 = _dot_exact_lhs(jnp.broadcast_to(tot_row, (8, CHUNK)), expand_tot)
            decs.append(jnp.exp(tot512))
        ydiags = [jnp.zeros((CHUNK, 256), F32) for _ in cs]
        for i in range(4):
            in_head = (lane256 >= 64 * i) & (lane256 < 64 * i + 64)
            for j in range(GROUP):
                tm, acol_t, dtt = tms[j], acol_ts[j], dtts[j]
                arg = jnp.where(lower, tm[:, i:i + 1] - acol_t[i:i + 1, :],
                                tm[:, 8 + i:9 + i] - acol_t[8 + i:9 + i, :])
                scale = (jnp.where(lower, dtt[i:i + 1, :], 0.0)
                         + jnp.where(upper, dtt[8 + i:9 + i, :], 0.0))
                wmat = (gmats[j] * jnp.exp(arg) * scale).astype(BF16)
                ydiags[j] = jnp.where(in_head, _dot(wmat, xbfs[j]), ydiags[j])
        for j, c in enumerate(cs):
            w512 = wes[j][:, 0:512]
            e512 = wes[j][:, 512:1024]
            xdw = (jnp.concatenate([xus[j], xus[j]], axis=1) * w512).astype(BF16)
            sloc_s[c] = _dot(bts[j], xdw)
            dec_s[c] = decs[j]
            e_s[c] = e512
            cbf_s[c] = cbfs[j]
            yacc_s[pl.ds(r0s[j], CHUNK), :] = ydiags[j] + xus[j] * dsk
        return carry

    lax.fori_loop(0, nch // GROUP, phase_a, 0)

    fwd_order = list(range(nch))
    bwd_order = list(range(ncc - 1, -1, -1)) + list(range(nch - 1, ncc - 1, -1))
    states = [jnp.zeros((SSD_STATE, 256), F32), jnp.zeros((SSD_STATE, 256), F32)]
    for cf, cb in zip(fwd_order, bwd_order):
        for d, (c, lo) in enumerate(((cf, 0), (cb, 256))):
            sin_s[c, :, lo:lo + 256] = states[d].astype(BF16)
            states[d] = (states[d] * dec_s[c, 0:1, lo:lo + 256]
                         + sloc_s[c, :, lo:lo + 256])

    def phase_c(cs, z_ref, zrows):
        yos = [_dot(cbf_s[c], sin_s[c]) for c in cs]
        vs = []
        for c, zrow, yo in zip(cs, zrows, yos):
            r0 = c * CHUNK if isinstance(c, int) else pl.multiple_of(c * CHUNK, CHUNK)
            yo = yo * e_s[c]
            y = yacc_s[pl.ds(r0, CHUNK), :] + yo[:, 0:256] + yo[:, 256:512]
            vs.append((r0, y * _silu(z_ref[pl.ds(zrow, CHUNK), :])))

        @pl.when(g == 0)
        def _():
            for r0, v in vs:
                vun_s[pl.ds(r0, CHUNK), 0:256] = v

        @pl.when(g == 1)
        def _():
            for r0, v in vs:
                vun_s[pl.ds(r0, CHUNK), 256:512] = v

    if ctx_out:
        phase_c(list(range(ncc)), zc_ref, [c * CHUNK for c in range(ncc)])

    group_c = next(n for n in (16, 8, 4, 2, 1) if ncl % n == 0)

    def phase_c_lat(k, carry):
        ks = [k * group_c + j for j in range(group_c)]
        phase_c([kk + ncc for kk in ks], zl_ref,
                [pl.multiple_of(kk * CHUNK, CHUNK) for kk in ks])
        return carry

    lax.fori_loop(0, ncl // group_c, phase_c_lat, 0)

    @pl.when(g == SSD_GROUPS - 1)
    def _finalize():
        ng = ng_ref[...]

        def norm_rows(r0, nrows):
            v = vun_s[pl.ds(r0, nrows), :]
            ms = jnp.mean(v * v, axis=-1, keepdims=True)
            return (v * lax.rsqrt(ms + EPS) * ng).astype(BF16)

        if ctx_out:
            oc_ref[...] = norm_rows(0, n_ctx)

        fin_blocks = next(n for n in (4, 2, 1) if (n_lat // 256) % n == 0)

        def fin(k, carry):
            r0s = [pl.multiple_of((k * fin_blocks + j) * 256, 256) for j in range(fin_blocks)]
            normed = [norm_rows(n_ctx + r0, 256) for r0 in r0s]
            for r0, v in zip(r0s, normed):
                ol_ref[pl.ds(r0, 256), :] = v
            return carry

        lax.fori_loop(0, n_lat // 256 // fin_blocks, fin, 0)


def _ssd_call(p_c, p_l, conv_w8, conv_b, par, dsk, ng, batch, n_ctx, n_lat, ctx_out):
    nch = (n_ctx + n_lat) // CHUNK
    t_all = n_ctx + n_lat
    in_specs = [
        pl.BlockSpec((n_ctx, 256), lambda b, g: (b, g)),
        pl.BlockSpec((n_lat, 256), lambda b, g: (b, g)),
        pl.BlockSpec((n_ctx, 128), lambda b, g: (b, 4 + g)),
        pl.BlockSpec((n_lat, 128), lambda b, g: (b, 4 + g)),
        pl.BlockSpec((n_ctx, 128), lambda b, g: (b, 6 + g)),
        pl.BlockSpec((n_lat, 128), lambda b, g: (b, 6 + g)),
        pl.BlockSpec((n_ctx, 256), lambda b, g: (b, 4 + g)),
        pl.BlockSpec((n_lat, 256), lambda b, g: (b, 4 + g)),
        pl.BlockSpec((n_ctx, 128), lambda b, g: (b, 26 + g)),
        pl.BlockSpec((n_lat, 128), lambda b, g: (b, 26 + g)),
        pl.BlockSpec((8, 256), lambda b, g: (0, g)),
        pl.BlockSpec((8, 128), lambda b, g: (0, 4 + g)),
        pl.BlockSpec((8, 128), lambda b, g: (0, 6 + g)),
        pl.BlockSpec((1, 256), lambda b, g: (0, g)),
        pl.BlockSpec((1, 128), lambda b, g: (0, 4 + g)),
        pl.BlockSpec((1, 128), lambda b, g: (0, 6 + g)),
        pl.BlockSpec((None, 32, 128), lambda b, g: (g, 0, 0)),
        pl.BlockSpec((1, 256), lambda b, g: (0, g)),
        pl.BlockSpec((1, 512), lambda b, g: (0, 0)),
    ]
    args = [p_c, p_l, p_c, p_l, p_c, p_l, p_c, p_l, p_c, p_l,
            conv_w8, conv_w8, conv_w8, conv_b, conv_b, conv_b, par, dsk, ng]
    out_specs = [pl.BlockSpec((n_lat, 512), lambda b, g: (b, 0))]
    out_shape = [jax.ShapeDtypeStruct((batch * n_lat, 512), BF16)]
    if ctx_out:
        out_specs = [pl.BlockSpec((n_ctx, 512), lambda b, g: (b, 0))] + out_specs
        out_shape = [jax.ShapeDtypeStruct((batch * n_ctx, 512), BF16)] + out_shape
    scratch = [
        pltpu.VMEM((t_all + 3 * HALO, 512), F32),
        pltpu.VMEM((t_all, 128), F32),
        pltpu.VMEM((t_all, 256), F32),
        pltpu.VMEM((nch, SSD_STATE, 512), F32),
        pltpu.VMEM((nch, CHUNK, 512), F32),
        pltpu.VMEM((nch, CHUNK, 128), BF16),
        pltpu.VMEM((nch, SSD_STATE, 512), BF16),
        pltpu.VMEM((nch, 8, 512), F32),
        pltpu.VMEM((t_all, 512), F32),
    ]
    outs = pl.pallas_call(
        functools.partial(_ssd_kernel, n_ctx=n_ctx, n_lat=n_lat, ctx_out=ctx_out),
        grid=(batch, SSD_GROUPS),
        in_specs=in_specs,
        out_specs=out_specs,
        out_shape=out_shape,
        scratch_shapes=scratch,
        compiler_params=pltpu.CompilerParams(
            dimension_semantics=("arbitrary", "arbitrary"),
            vmem_limit_bytes=VMEM_LIMIT),
        name="ssd_scan",
    )(*args)
    if ctx_out:
        return outs[1], outs[0]
    return outs[0], None


def _rope_tables(n_lat):
    rows = n_lat // GRID_W
    row_idx = np.repeat(np.arange(rows), GRID_W).astype(np.float32)
    col_idx = (np.arange(rows * GRID_W) % GRID_W).astype(np.float32)

    def tables(dim, reps):
        quarter = dim // 4
        inv = (ROPE_BASE ** (-np.arange(quarter, dtype=np.float32) / quarter)).astype(np.float32)
        ang = np.concatenate([row_idx[:, None] * inv, col_idx[:, None] * inv], axis=-1)
        cos, sin = np.cos(ang.astype(np.float64)), np.sin(ang.astype(np.float64))
        cos2 = np.concatenate([cos, cos], axis=-1).astype(np.float32)
        sin2 = np.concatenate([-sin, sin], axis=-1).astype(np.float32)
        return jnp.asarray(np.tile(cos2, (1, reps))), jnp.asarray(np.tile(sin2, (1, reps)))

    cos_g, sin_g = tables(GQA_HEAD_DIM, 4)
    cos_d, sin_d = tables(DIFF_QK_DIM, 8)
    return {"cos_g": cos_g, "sin_g": sin_g, "cos_d": cos_d, "sin_d": sin_d}


def kernel(x, c, ctx, c_ctx, w_mod, b_mod, g_pre, g_post, w_in, conv_w, conv_b,
           a_log_fwd, a_log_bwd, dt_bias_fwd, dt_bias_bwd, d_skip, ssd_norm_g,
           q_norm_g, k_norm_g, diff_lambda, diff_norm_g, w_out):
    batch, n_lat, _ = x.shape
    n_ctx = ctx.shape[1]
    depth = w_mod.shape[0]
    assert n_lat % 512 == 0 and n_ctx % 256 == 0 and (batch * n_ctx) % 512 == 0
    assert batch + 1 <= 16

    in_perm, out_perm = _in_col_perm(), _out_row_perm()
    w_in_p = [_take_runs(w_in[l], in_perm, 1, IN_COLS).astype(BF16) for l in range(depth)]
    w_out_p = [_take_runs(w_out[l], out_perm, 0, None).astype(BF16) for l in range(depth)]
    conv_w8 = jnp.pad(conv_w, ((0, 0), (0, 8 - CONV_K), (0, 0)))
    conv_b1 = conv_b[:, None, :]

    def group16(fwd, bwd):
        out = jnp.zeros((depth, SSD_GROUPS, 16), F32)
        for g in range(SSD_GROUPS):
            out = out.at[:, g, 0:4].set(fwd[:, 4 * g:4 * g + 4])
            out = out.at[:, g, 8:12].set(bwd[:, 4 * g:4 * g + 4])
        return out

    ssd_par = jnp.broadcast_to(
        jnp.concatenate([group16(a_log_fwd, a_log_bwd),
                         group16(dt_bias_fwd, dt_bias_bwd)], axis=-1)[..., None],
        (depth, SSD_GROUPS, 32, 128))
    dsk = jnp.repeat(d_skip, SSD_HEAD_DIM, axis=1)[:, None, :]
    qg = jnp.tile(q_norm_g, (1, 4))[:, None, :]
    kg = jnp.tile(k_norm_g, (1, 2))[:, None, :]
    dng = jnp.tile(diff_norm_g, (1, 4))[:, None, :]
    tabs = _rope_tables(n_lat)

    cs = jnp.concatenate(
        [c, c_ctx[None, :], jnp.zeros((16 - batch - 1, D_MODEL), F32)], axis=0)
    mod_all = _mod_call(cs, w_mod, b_mod)

    h = x.reshape(batch * n_lat, D_MODEL)
    hc = ctx.reshape(batch * n_ctx, D_MODEL)
    for l in range(depth):
        ctx_out = l < depth - 1
        lam_init = 0.8 - 0.6 * float(np.exp(-0.3 * l))
        mod3 = mod_all[l][:, None, :]
        p_l = _inproj_call(h, mod3, g_pre[l][None, :], w_in_p[l], n_lat // 512, None)
        p_c = _inproj_call(hc, mod3, g_pre[l][None, :], w_in_p[l], None, batch)

        ys_l, ys_c = _ssd_call(p_c, p_l, conv_w8[l], conv_b1[l], ssd_par[l], dsk[l],
                               ssd_norm_g[l][None, :], batch, n_ctx, n_lat, ctx_out)
        yg_l = _gqa_call(p_l, p_c, p_l, tabs, qg[l], kg[l], batch, n_ctx, n_lat, True)
        diff_args = (p_l, p_c, p_l, tabs, diff_lambda[l], dng[l], batch, n_ctx, n_lat, True,
                     lam_init)
        yd_fast, row_sum_min = _diff_call(*diff_args, True)
        yd_l = lax.cond(jnp.min(row_sum_min) >= MIN_ROW_SUM,
                        lambda y: y, lambda y: _diff_call(*diff_args, False, overwrite=y),
                        yd_fast)
        h = _outproj_call(ys_l, yg_l, yd_l, h, mod3, g_post[l][None, :], w_out_p[l],
                          n_lat, None)
        if ctx_out:
            yg_c = _gqa_call(p_c, p_c, None, tabs, qg[l], kg[l], batch, n_ctx, 0, False)
            yd_c = _diff_call(p_c, p_c, None, tabs, diff_lambda[l], dng[l], batch,
                              n_ctx, 0, False, lam_init, False)
            hc = _outproj_call(ys_c, yg_c, yd_c, hc, mod3, g_post[l][None, :],
                               w_out_p[l], None, batch)
    return h.reshape(batch, n_lat, D_MODEL)
```

```python
import functools

import numpy as np
import jax
import jax.numpy as jnp
from jax import lax
from jax.experimental import pallas as pl
from jax.experimental.pallas import tpu as pltpu

F32 = jnp.float32
BF16 = jnp.bfloat16

D_MODEL = 1024
GRID_W = 64
ROPE_BASE = 10000.0
EPS = 1e-6
LOG2E = 1.4426950408889634

SSD_WIDTH = 512
SSD_HEADS = 8
SSD_HEAD_DIM = 64
SSD_GROUPS = 2
SSD_STATE = 128
CHUNK = 128
CONV_K = 5
HALO = 8
GQA_HEAD_DIM = 64
DIFF_HEADS = 4
DIFF_QK_DIM = 32

_IN_SPLITS = (("xbc", 1024), ("z", 512), ("dt", 16), ("gq", 256), ("gk", 128),
              ("gv", 128), ("gg", 256), ("dq", 256), ("dk", 256), ("dv", 256),
              ("dg", 256))
IN_COLS = sum(s for _, s in _IN_SPLITS)
NP = 28 * 128
COL_X, COL_B, COL_C, COL_Z, COL_GQ, COL_GK, COL_GG = 0, 4, 6, 8, 12, 14, 16
COL_DQ, COL_DK, COL_DV, COL_DG, COL_DT = 18, 20, 22, 24, 26
GQ_HEAD_ORDER = (0, 2, 1, 3)

VMEM_LIMIT = 56 * 1024 * 1024
INPROJ_TM = 512
ATTN_TQ = 1024
DIFF_TQ = 1024
ATTN_SUB = 512
DIFF_SUB = 512
ATTN_AHEAD = 2
DIFF_AHEAD = 1
SCORE_BOUND_MARGIN = 1.02
MIN_ROW_SUM = 2.0 ** -90
OUTPROJ_TM = 1024
OUTPROJ_SUB = 512


def _in_col_perm():
    off, o = {}, 0
    for name, size in _IN_SPLITS:
        off[name] = o
        o += size
    pad = IN_COLS
    cols = list(range(off["xbc"], off["xbc"] + 1024))
    cols += list(range(off["z"], off["z"] + 512))
    for h in GQ_HEAD_ORDER:
        cols += list(range(off["gq"] + 64 * h, off["gq"] + 64 * h + 64))
    cols += list(range(off["gk"], off["gk"] + 128))
    cols += list(range(off["gv"], off["gv"] + 128))
    for h in GQ_HEAD_ORDER:
        cols += list(range(off["gg"] + 64 * h, off["gg"] + 64 * h + 64))
    for name in ("dq", "dk", "dv", "dg"):
        cols += list(range(off[name], off[name] + 256))
    for g in range(SSD_GROUPS):
        blk = [pad] * 128
        for i in range(4):
            blk[i] = off["dt"] + 4 * g + i
            blk[8 + i] = off["dt"] + SSD_HEADS + 4 * g + i
        cols += blk
    assert len(cols) == NP
    return np.asarray(cols, np.int32)


def _out_row_perm():
    rows = list(range(SSD_WIDTH))
    for h in GQ_HEAD_ORDER:
        rows += list(range(SSD_WIDTH + 64 * h, SSD_WIDTH + 64 * h + 64))
    rows += list(range(SSD_WIDTH + 256, SSD_WIDTH + 512))
    return np.asarray(rows, np.int32)


def _take_runs(arr, idx, axis, pad_index):
    pieces, start = [], 0
    idx = [int(i) for i in idx]
    while start < len(idx):
        end = start + 1
        if idx[start] == pad_index:
            while end < len(idx) and idx[end] == pad_index:
                end += 1
            shape = list(arr.shape)
            shape[axis] = end - start
            pieces.append(jnp.zeros(shape, arr.dtype))
        else:
            while end < len(idx) and idx[end] == idx[end - 1] + 1 and idx[end] != pad_index:
                end += 1
            pieces.append(lax.slice_in_dim(arr, idx[start], idx[end - 1] + 1, axis=axis))
        start = end
    return jnp.concatenate(pieces, axis=axis)


def _dot(a, b):
    return jnp.dot(a, b, preferred_element_type=F32)


def _split3(x):
    hi = x.astype(BF16)
    r1 = x - hi.astype(F32)
    mid = r1.astype(BF16)
    lo = (r1 - mid.astype(F32)).astype(BF16)
    return hi, mid, lo


def _dot_exact_lhs(x, m_bf16):
    hi, mid, lo = _split3(x)
    return _dot(hi, m_bf16) + _dot(mid, m_bf16) + _dot(lo, m_bf16)


def _silu(x):
    return x * jax.nn.sigmoid(x)


def _seg_ones(width, seg):
    r = lax.broadcasted_iota(jnp.int32, (width, width), 0)
    c = lax.broadcasted_iota(jnp.int32, (width, width), 1)
    same = (r & ~(seg - 1)) == (c & ~(seg - 1))
    return jnp.where(same, 1.0, 0.0).astype(BF16)


def _seg_rms(x, seg, seg_mat):
    ss = _dot_exact_lhs(x * x, seg_mat)
    return x * lax.rsqrt(ss * (1.0 / seg) + EPS)


def _rope(x, cos, sin_signed, half):
    w = x.shape[-1]
    lane = lax.broadcasted_iota(jnp.int32, x.shape, 1)
    first = (lane & (2 * half - 1)) < half
    swapped = jnp.where(first, pltpu.roll(x, w - half, 1), pltpu.roll(x, half, 1))
    return x * cos + swapped * sin_signed


def _mod_kernel(cs_ref, w_ref, b_ref, o_ref):
    s = _silu(cs_ref[...]).astype(BF16)
    o_ref[...] = _dot(s, w_ref[...].astype(BF16)) + b_ref[...]


def _mod_call(cs, w_mod, b_mod):
    depth = w_mod.shape[0]
    nrow = cs.shape[0]
    tn = 1024
    return pl.pallas_call(
        _mod_kernel,
        grid=(depth, 3 * D_MODEL // tn),
        in_specs=[
            pl.BlockSpec((nrow, D_MODEL), lambda l, j: (0, 0)),
            pl.BlockSpec((None, D_MODEL, tn), lambda l, j: (l, 0, j)),
            pl.BlockSpec((None, 1, tn), lambda l, j: (l, 0, j)),
        ],
        out_specs=pl.BlockSpec((None, nrow, tn), lambda l, j: (l, 0, j)),
        out_shape=jax.ShapeDtypeStruct((depth, nrow, 3 * D_MODEL), F32),
        compiler_params=pltpu.CompilerParams(
            dimension_semantics=("arbitrary", "arbitrary")),
        name="mod_proj",
    )(cs, w_mod, b_mod.reshape(depth, 1, 3 * D_MODEL))


def _inproj_kernel(h_ref, mod_ref, g_ref, w_ref, o_ref):
    x = h_ref[...]
    ms = jnp.mean(x * x, axis=-1, keepdims=True)
    y = x * lax.rsqrt(ms + EPS) * g_ref[...]
    sh = mod_ref[:, 0:D_MODEL]
    sc = mod_ref[:, D_MODEL:2 * D_MODEL]
    u = (y * (1.0 + sc) + sh).astype(BF16)
    tn = 512
    for j in range(NP // tn):
        o_ref[:, j * tn:(j + 1) * tn] = _dot(u, w_ref[:, j * tn:(j + 1) * tn])


def _inproj_call(h, mod3, g_pre, w_bf16, tiles_per_row, fixed_row):
    n_tok = h.shape[0]
    tm = INPROJ_TM
    if fixed_row is None:
        mod_idx = lambda i: (i // tiles_per_row, 0, 0)
    else:
        mod_idx = lambda i: (fixed_row, 0, 0)
    return pl.pallas_call(
        _inproj_kernel,
        grid=(n_tok // tm,),
        in_specs=[
            pl.BlockSpec((tm, D_MODEL), lambda i: (i, 0)),
            pl.BlockSpec((None, 1, 3 * D_MODEL), mod_idx),
            pl.BlockSpec((1, D_MODEL), lambda i: (0, 0)),
            pl.BlockSpec((D_MODEL, NP), lambda i: (0, 0)),
        ],
        out_specs=pl.BlockSpec((tm, NP), lambda i: (i, 0)),
        out_shape=jax.ShapeDtypeStruct((n_tok, NP), F32),
        compiler_params=pltpu.CompilerParams(
            dimension_semantics=("arbitrary",), vmem_limit_bytes=VMEM_LIMIT),
        name="in_proj",
    )(h, mod3, g_pre, w_bf16)


def _outproj_kernel(ys_ref, yg_ref, yd_ref, h_ref, mod_ref, g_ref, w_ref, o_ref):
    tm = h_ref.shape[0]
    sub = min(OUTPROJ_SUB, tm)
    gt = mod_ref[:, 2 * D_MODEL:3 * D_MODEL]
    gain = g_ref[...]

    def project(r0):
        return (_dot(ys_ref[r0:r0 + sub, :], w_ref[0:512, :])
                + _dot(yg_ref[r0:r0 + sub, :], w_ref[512:768, :])
                + _dot(yd_ref[r0:r0 + sub, :], w_ref[768:1024, :]))

    o_next = project(0)
    for r0 in range(0, tm, sub):
        o = o_next
        if r0 + sub < tm:
            o_next = project(r0 + sub)
        ms = jnp.mean(o * o, axis=-1, keepdims=True)
        n = o * lax.rsqrt(ms + EPS) * gain
        o_ref[r0:r0 + sub, :] = h_ref[r0:r0 + sub, :] + gt * n


def _outproj_call(ys, yg, yd, h, mod3, g_post, w_bf16, rows_per_mod, fixed_row):
    n_tok = h.shape[0]
    tm = OUTPROJ_TM
    assert n_tok % tm == 0
    if fixed_row is None:
        assert rows_per_mod % tm == 0
        mod_idx = lambda i: (i // (rows_per_mod // tm), 0, 0)
    else:
        mod_idx = lambda i: (fixed_row, 0, 0)
    return pl.pallas_call(
        _outproj_kernel,
        grid=(n_tok // tm,),
        in_specs=[
            pl.BlockSpec((tm, 512), lambda i: (i, 0)),
            pl.BlockSpec((tm, 256), lambda i: (i, 0)),
            pl.BlockSpec((tm, 256), lambda i: (i, 0)),
            pl.BlockSpec((tm, D_MODEL), lambda i: (i, 0)),
            pl.BlockSpec((None, 1, 3 * D_MODEL), mod_idx),
            pl.BlockSpec((1, D_MODEL), lambda i: (0, 0)),
            pl.BlockSpec((D_MODEL, D_MODEL), lambda i: (0, 0)),
        ],
        out_specs=pl.BlockSpec((tm, D_MODEL), lambda i: (i, 0)),
        out_shape=jax.ShapeDtypeStruct((n_tok, D_MODEL), F32),
        compiler_params=pltpu.CompilerParams(
            dimension_semantics=("arbitrary",), vmem_limit_bytes=VMEM_LIMIT),
        name="out_proj",
    )(ys, yg, yd, h, mod3, g_post, w_bf16)


def _attend_many(lhs_list, kt_ref, vext_refs):
    def scores(i):
        s = _dot(lhs_list[i], kt_ref[...])
        return s, jnp.max(s, axis=-1, keepdims=True)

    outs = []
    n = len(lhs_list)
    ahead = [scores(i) for i in range(min(ATTN_AHEAD, n))]
    for i, vext_ref in enumerate(vext_refs):
        s, m = ahead.pop(0)
        if i + ATTN_AHEAD < n:
            ahead.append(scores(i + ATTN_AHEAD))
        p = jnp.exp2(s - m).astype(BF16)
        oe = _dot(p, vext_ref[...])
        outs.append(oe[:, 0:128] / oe[:, 128:256])
    return outs


def _attend_diff_pairs(lhs_list, kt_ref, v_refs, lam):
    n_heads = len(v_refs)

    def scores(h):
        s_a = _dot(lhs_list[2 * h], kt_ref[...])
        m_a = jnp.max(s_a, axis=-1, keepdims=True)
        s_b = _dot(lhs_list[2 * h + 1], kt_ref[...])
        m_b = jnp.max(s_b, axis=-1, keepdims=True)
        return s_a, m_a, s_b, m_b

    outs = []
    ahead = [scores(h) for h in range(min(DIFF_AHEAD, n_heads))]
    for h in range(n_heads):
        s_a, m_a, s_b, m_b = ahead.pop(0)
        if h + DIFF_AHEAD < n_heads:
            ahead.append(scores(h + DIFF_AHEAD))
        e_a = jnp.exp2(s_a - m_a)
        e_b = jnp.exp2(s_b - m_b)
        l_a = jnp.sum(e_a, axis=-1, keepdims=True)
        l_b = jnp.sum(e_b, axis=-1, keepdims=True)
        pc = (e_a - (lam * l_a / l_b) * e_b).astype(BF16)
        outs.append(_dot(pc, v_refs[h][...]) / l_a)
    return outs


def _attend_diff_pairs_bounded(lhs_list, bounds, kt_ref, v_refs, lam):
    n_heads = len(v_refs)

    def exps(h):
        e_a = jnp.exp2(_dot(lhs_list[2 * h], kt_ref[...]) - bounds[2 * h])
        e_b = jnp.exp2(_dot(lhs_list[2 * h + 1], kt_ref[...]) - bounds[2 * h + 1])
        return e_a, e_b

    outs, l_min = [], None
    ahead = [exps(h) for h in range(min(DIFF_AHEAD, n_heads))]
    for h in range(n_heads):
        e_a, e_b = ahead.pop(0)
        if h + DIFF_AHEAD < n_heads:
            ahead.append(exps(h + DIFF_AHEAD))
        l_a = jnp.sum(e_a, axis=-1, keepdims=True)
        l_b = jnp.sum(e_b, axis=-1, keepdims=True)
        pc = (e_a - (lam * l_a / l_b) * e_b).astype(BF16)
        outs.append(_dot(pc, v_refs[h][...]) / l_a)
        l_ab = jnp.minimum(l_a, l_b)
        l_min = l_ab if l_min is None else jnp.minimum(l_min, l_ab)
    return outs, jnp.min(l_min, axis=0, keepdims=True)


def _gqa_kernel(*refs, n_ctx, n_lat, rope_q):
    it = iter(refs)
    q_ref, gg_ref, kvc_ref = next(it), next(it), next(it)
    kvl_ref = next(it) if n_lat else None
    if rope_q:
        cosq_ref, sinq_ref = next(it), next(it)
    if n_lat:
        cosk_ref, sink_ref = next(it), next(it)
    qg_ref, kg_ref = next(it), next(it)
    y_ref = next(it)
    kt_s, vext_s = next(it), next(it)

    seg128 = _seg_ones(128, 64)

    @pl.when(pl.program_id(1) == 0)
    def _prep_kv():
        kc = _seg_rms(kvc_ref[:, 0:128], 64, seg128) * kg_ref[...]
        kt_s[:, 0:n_ctx] = kc.T.astype(BF16)
        vext_s[0:n_ctx, 0:128] = kvc_ref[:, 128:256].astype(BF16)
        if n_lat:
            kl = _seg_rms(kvl_ref[:, 0:128], 64, seg128) * kg_ref[...]
            kl = _rope(kl, cosk_ref[...], sink_ref[...], 32)
            kt_s[:, n_ctx:n_ctx + n_lat] = kl.T.astype(BF16)
            vext_s[n_ctx:n_ctx + n_lat, 0:128] = kvl_ref[:, 128:256].astype(BF16)
        vext_s[:, 128:256] = jnp.ones((n_ctx + n_lat, 128), BF16)

    seg256 = _seg_ones(256, 64)
    q = _seg_rms(q_ref[...], 64, seg256) * qg_ref[...]
    if rope_q:
        q = _rope(q, cosq_ref[...], sinq_ref[...], 32)
    q = q * (GQA_HEAD_DIM ** -0.5 * LOG2E)
    tq = q.shape[0]
    sub = min(ATTN_SUB, tq)
    lane = lax.broadcasted_iota(jnp.int32, (sub, 128), 1)
    lhs_list = []
    for r0 in range(0, tq, sub):
        for half in range(2):
            qh = q[r0:r0 + sub, 128 * half:128 * half + 128]
            for kv in range(2):
                in_kv = (lane >= 64 * kv) & (lane < 64 * kv + 64)
                lhs_list.append(jnp.where(in_kv, qh, 0.0).astype(BF16))
    outs = _attend_many(lhs_list, kt_s, [vext_s] * len(lhs_list))
    for j, r0 in enumerate(range(0, tq, sub)):
        for half in range(2):
            o = jnp.where(lane < 64, outs[4 * j + 2 * half], outs[4 * j + 2 * half + 1])
            gate = _silu(gg_ref[r0:r0 + sub, 128 * half:128 * half + 128])
            y_ref[r0:r0 + sub, 128 * half:128 * half + 128] = (o * gate).astype(BF16)


def _gqa_call(p_q, p_c, p_l, tabs, qg, kg, batch, n_ctx, n_lat, rope_q):
    t_total = p_q.shape[0] // batch
    tq = min(ATTN_TQ, t_total)
    nq = t_total // tq
    in_specs = [
        pl.BlockSpec((tq, 256), lambda b, i: (b * nq + i, COL_GQ // 2)),
        pl.BlockSpec((tq, 256), lambda b, i: (b * nq + i, COL_GG // 2)),
        pl.BlockSpec((n_ctx, 256), lambda b, i: (b, COL_GK // 2)),
    ]
    args = [p_q, p_q, p_c]
    if n_lat:
        in_specs.append(pl.BlockSpec((n_lat, 256), lambda b, i: (b, COL_GK // 2)))
        args.append(p_l)
    if rope_q:
        in_specs += [pl.BlockSpec((tq, 256), lambda b, i: (i, 0))] * 2
        args += [tabs["cos_g"], tabs["sin_g"]]
    if n_lat:
        in_specs += [pl.BlockSpec((n_lat, 128), lambda b, i: (0, 0))] * 2
        args += [tabs["cos_g"], tabs["sin_g"]]
    in_specs += [pl.BlockSpec((1, 256), lambda b, i: (0, 0)),
                 pl.BlockSpec((1, 128), lambda b, i: (0, 0))]
    args += [qg, kg]
    s_keys = n_ctx + n_lat
    return pl.pallas_call(
        functools.partial(_gqa_kernel, n_ctx=n_ctx, n_lat=n_lat, rope_q=rope_q),
        grid=(batch, nq),
        in_specs=in_specs,
        out_specs=pl.BlockSpec((tq, 256), lambda b, i: (b * nq + i, 0)),
        out_shape=jax.ShapeDtypeStruct((p_q.shape[0], 256), BF16),
        scratch_shapes=[pltpu.VMEM((128, s_keys), BF16),
                        pltpu.VMEM((s_keys, 256), BF16)],
        compiler_params=pltpu.CompilerParams(
            dimension_semantics=("arbitrary", "arbitrary"),
            vmem_limit_bytes=VMEM_LIMIT),
        name="gqa_attn",
    )(*args)


def _diff_kernel(*refs, n_ctx, n_lat, rope_q, lam_init, bounded):
    it = iter(refs)
    q_ref, dg_ref, kc_ref, vc_ref = next(it), next(it), next(it), next(it)
    if n_lat:
        kl_ref, vl_ref = next(it), next(it)
    if rope_q:
        cosq_ref, sinq_ref = next(it), next(it)
    if n_lat:
        cosk_ref, sink_ref = next(it), next(it)
    lam_ref, ng_ref = next(it), next(it)
    y_ref = next(it)
    lmin_ref = next(it) if bounded else None
    kt_s, vlo_s, vhi_s = next(it), next(it), next(it)
    kmax_s = next(it) if bounded else None
    s_keys = n_ctx + n_lat
    seg32 = _seg_ones(256, DIFF_QK_DIM)

    def map_norms(x):
        return jnp.sqrt(_dot((x * x).astype(BF16), seg32))

    @pl.when(pl.program_id(1) == 0)
    def _prep_kv():
        kc = kc_ref[...]
        kt_s[:, 0:n_ctx] = kc.T.astype(BF16)
        vlo_s[0:n_ctx, 0:128] = vc_ref[:, 0:128].astype(BF16)
        vhi_s[0:n_ctx, 0:128] = vc_ref[:, 128:256].astype(BF16)
        if bounded:
            kmax = jnp.max(map_norms(kc), axis=0, keepdims=True)
        if n_lat:
            kl = _rope(kl_ref[...], cosk_ref[...], sink_ref[...], 16)
            kt_s[:, n_ctx:s_keys] = kl.T.astype(BF16)
            vlo_s[n_ctx:s_keys, 0:128] = vl_ref[:, 0:128].astype(BF16)
            vhi_s[n_ctx:s_keys, 0:128] = vl_ref[:, 128:256].astype(BF16)
            if bounded:
                kmax = jnp.maximum(kmax, jnp.max(map_norms(kl), axis=0, keepdims=True))
        if bounded:
            kmax_s[...] = jnp.broadcast_to(kmax, kmax_s.shape)

    lp = lam_ref[...]
    lam = (jnp.exp(jnp.sum(lp[0:1, :] * lp[1:2, :], axis=-1, keepdims=True))
           - jnp.exp(jnp.sum(lp[2:3, :] * lp[3:4, :], axis=-1, keepdims=True))
           + lam_init)

    q = q_ref[...]
    if rope_q:
        q = _rope(q, cosq_ref[...], sinq_ref[...], 16)
    q = q * (DIFF_QK_DIM ** -0.5 * LOG2E)
    tq = q.shape[0]
    sub = min(DIFF_SUB, tq)
    lane256 = lax.broadcasted_iota(jnp.int32, (sub, 256), 1)
    lane128 = lax.broadcasted_iota(jnp.int32, (sub, 128), 1)
    seg128 = _seg_ones(128, 64)
    if bounded:
        bound_all = map_norms(q) * kmax_s[0:1, :] * SCORE_BOUND_MARGIN
    lhs_list, v_list, bounds = [], [], []
    for r0 in range(0, tq, sub):
        for mp in range(2 * DIFF_HEADS):
            in_map = (lane256 >= 32 * mp) & (lane256 < 32 * mp + 32)
            lhs_list.append(jnp.where(in_map, q[r0:r0 + sub, :], 0.0).astype(BF16))
            if bounded:
                bounds.append(bound_all[r0:r0 + sub, 32 * mp:32 * mp + 1])
        v_list += [vlo_s, vlo_s, vhi_s, vhi_s]

    if bounded:
        heads, l_min = _attend_diff_pairs_bounded(lhs_list, bounds, kt_s, v_list, lam)
        lmin_ref[...] = jnp.broadcast_to(l_min, lmin_ref.shape)
    else:
        heads = _attend_diff_pairs(lhs_list, kt_s, v_list, lam)
    for j, r0 in enumerate(range(0, tq, sub)):
        for half in range(2):
            o = jnp.where(lane128 < 64, heads[4 * j + 2 * half], heads[4 * j + 2 * half + 1])
            n = _seg_rms(o, 64, seg128) * ng_ref[:, 128 * half:128 * half + 128]
            n = n * (1.0 - lam_init)
            gate = _silu(dg_ref[r0:r0 + sub, 128 * half:128 * half + 128])
            y_ref[r0:r0 + sub, 128 * half:128 * half + 128] = (n * gate).astype(BF16)


def _diff_call(p_q, p_c, p_l, tabs, lam_params, ng, batch, n_ctx, n_lat, rope_q,
               lam_init, bounded):
    t_total = p_q.shape[0] // batch
    tq = min(DIFF_TQ, t_total)
    nq = t_total // tq
    in_specs = [
        pl.BlockSpec((tq, 256), lambda b, i: (b * nq + i, COL_DQ // 2)),
        pl.BlockSpec((tq, 256), lambda b, i: (b * nq + i, COL_DG // 2)),
        pl.BlockSpec((n_ctx, 256), lambda b, i: (b, COL_DK // 2)),
        pl.BlockSpec((n_ctx, 256), lambda b, i: (b, COL_DV // 2)),
    ]
    args = [p_q, p_q, p_c, p_c]
    if n_lat:
        in_specs += [pl.BlockSpec((n_lat, 256), lambda b, i: (b, COL_DK // 2)),
                     pl.BlockSpec((n_lat, 256), lambda b, i: (b, COL_DV // 2))]
        args += [p_l, p_l]
    if rope_q:
        in_specs += [pl.BlockSpec((tq, 256), lambda b, i: (i, 0))] * 2
        args += [tabs["cos_d"], tabs["sin_d"]]
    if n_lat:
        in_specs += [pl.BlockSpec((n_lat, 256), lambda b, i: (0, 0))] * 2
        args += [tabs["cos_d"], tabs["sin_d"]]
    in_specs += [pl.BlockSpec((4, DIFF_QK_DIM), lambda b, i: (0, 0)),
                 pl.BlockSpec((1, 256), lambda b, i: (0, 0))]
    args += [lam_params, ng]
    s_keys = n_ctx + n_lat
    out_specs = [pl.BlockSpec((tq, 256), lambda b, i: (b * nq + i, 0))]
    out_shape = [jax.ShapeDtypeStruct((p_q.shape[0], 256), BF16)]
    scratch = [pltpu.VMEM((256, s_keys), BF16),
               pltpu.VMEM((s_keys, 128), BF16),
               pltpu.VMEM((s_keys, 128), BF16)]
    if bounded:
        out_specs.append(pl.BlockSpec((None, 8, 128), lambda b, i: (b * nq + i, 0, 0)))
        out_shape.append(jax.ShapeDtypeStruct((batch * nq, 8, 128), F32))
        scratch.append(pltpu.VMEM((8, 256), F32))
    outs = pl.pallas_call(
        functools.partial(_diff_kernel, n_ctx=n_ctx, n_lat=n_lat, rope_q=rope_q,
                          lam_init=lam_init, bounded=bounded),
        grid=(batch, nq),
        in_specs=in_specs,
        out_specs=out_specs,
        out_shape=out_shape,
        scratch_shapes=scratch,
        compiler_params=pltpu.CompilerParams(
            dimension_semantics=("arbitrary", "arbitrary"),
            vmem_limit_bytes=VMEM_LIMIT),
        name="diff_attn_bounded" if bounded else "diff_attn",
    )(*args)
    return (outs[0], outs[1]) if bounded else outs[0]


def _ssd_kernel(xc_ref, xl_ref, bc_ref, bl_ref, cc_ref, cl_ref, zc_ref, zl_ref,
                dtc_ref, dtl_ref, cwx_ref, cwb_ref, cwc_ref, cbx_ref, cbb_ref,
                cbc_ref, par_ref, dsk_ref, ng_ref, *rest, n_ctx, n_lat, ctx_out):
    if ctx_out:
        oc_ref, ol_ref = rest[0], rest[1]
        rest = rest[2:]
    else:
        oc_ref, ol_ref = None, rest[0]
        rest = rest[1:]
    xp_s, dtr_s, yacc_s, sloc_s, e_s, cbf_s, sin_s, dec_s, vun_s = rest

    g = pl.program_id(1)
    ncc = n_ctx // CHUNK
    ncl = n_lat // CHUNK
    nch = ncc + ncl
    GROUP = next(n for n in (18, 9, 6, 3, 2, 1) if nch % n == 0)
    t_all = n_ctx + n_lat
    lat0 = n_ctx + 2 * HALO

    zeros_h = jnp.zeros((HALO, 512), F32)
    xp_s[0:HALO, :] = zeros_h
    xp_s[HALO:HALO + n_ctx, 0:256] = xc_ref[...]
    xp_s[HALO:HALO + n_ctx, 256:384] = bc_ref[...]
    xp_s[HALO:HALO + n_ctx, 384:512] = cc_ref[...]
    xp_s[HALO + n_ctx:lat0, :] = zeros_h
    xp_s[lat0:lat0 + n_lat, 0:256] = xl_ref[...]
    xp_s[lat0:lat0 + n_lat, 256:384] = bl_ref[...]
    xp_s[lat0:lat0 + n_lat, 384:512] = cl_ref[...]
    xp_s[lat0 + n_lat:lat0 + n_lat + HALO, :] = zeros_h

    dtr_s[0:n_ctx, :] = dtc_ref[...]
    dtr_s[n_ctx:t_all, :] = dtl_ref[...]
    a_col = -jnp.exp(par_ref[0:16, :])
    bias_col = par_ref[16:32, :]

    def _softplus(v):
        return jnp.maximum(v, 0.0) + jnp.log1p(jnp.exp(-jnp.abs(v)))

    r128 = lax.broadcasted_iota(jnp.int32, (CHUNK, CHUNK), 0)
    c128 = lax.broadcasted_iota(jnp.int32, (CHUNK, CHUNK), 1)
    lower = c128 <= r128
    upper = c128 >= r128
    tril = jnp.where(lower, 1.0, 0.0).astype(BF16)
    triu = jnp.where(upper, 1.0, 0.0).astype(BF16)
    fwd_row = lax.broadcasted_iota(jnp.int32, (16, CHUNK), 0) < 8
    fwd_row1 = lax.broadcasted_iota(jnp.int32, (16, 1), 0) < 8
    er = lax.broadcasted_iota(jnp.int32, (CHUNK, 1024), 0)
    ec = lax.broadcasted_iota(jnp.int32, (CHUNK, 1024), 1)
    src_lane = 16 + 16 * (ec >> 9) + 8 * ((ec >> 8) & 1) + ((ec >> 6) & 3)
    expand = jnp.where(er == src_lane, 1.0, 0.0).astype(BF16)
    er0 = lax.broadcasted_iota(jnp.int32, (CHUNK, 512), 0)
    ec0 = lax.broadcasted_iota(jnp.int32, (CHUNK, 512), 1)
    expand_tot = jnp.where(er0 == 8 * (ec0 >> 8) + ((ec0 >> 6) & 3), 1.0, 0.0).astype(BF16)
    lane256 = lax.broadcasted_iota(jnp.int32, (CHUNK, 256), 1)
    lane128_1 = lax.broadcasted_iota(jnp.int32, (1, CHUNK), 1)

    cw = jnp.concatenate([cwx_ref[...], cwb_ref[...], cwc_ref[...]], axis=1)
    cb = jnp.concatenate([cbx_ref[...], cbb_ref[...], cbc_ref[...]], axis=1)
    dsk = dsk_ref[...]

    def phase_a(grp, carry):
        cs = [grp * GROUP + j for j in range(GROUP)]
        r0s = [pl.multiple_of(c * CHUNK, CHUNK) for c in cs]

        def conv(c):
            wstart = pl.multiple_of(c * CHUNK + jnp.where(c >= ncc, HALO, 0), 8)
            win = xp_s[pl.ds(wstart, CHUNK + 2 * HALO), :]
            acc = jnp.broadcast_to(cb, (CHUNK, 512))
            for k in range(CONV_K):
                d = k - CONV_K // 2
                if d == 0:
                    tap = win[HALO:HALO + CHUNK, :]
                else:
                    tap = pltpu.roll(win, (-d) % (CHUNK + 2 * HALO), 0)[HALO:HALO + CHUNK, :]
                acc = acc + cw[k:k + 1, :] * tap
            return _silu(acc)

        dtts = [_softplus(dtr_s[pl.ds(r0, CHUNK), :].T[0:16, :] + bias_col) for r0 in r0s]
        a_ts = [dtt * a_col for dtt in dtts]
        acol_ts = [jnp.where(fwd_row, _dot_exact_lhs(a_t, triu), _dot_exact_lhs(a_t, tril))
                   for a_t in a_ts]
        us = [conv(c) for c in cs]
        xus = [u[:, 0:256] for u in us]
        bts = [u[:, 256:384].T.astype(BF16) for u in us]
        cbfs = [u[:, 384:512].astype(BF16) for u in us]
        xbfs = [xu.astype(BF16) for xu in xus]
        gmats = [_dot(cbf, bt) for cbf, bt in zip(cbfs, bts)]
        tms = []
        for dtt, acol_t in zip(dtts, acol_ts):
            tot = jnp.where(fwd_row1, acol_t[:, CHUNK - 1:CHUNK], acol_t[:, 0:1])
            w_t = dtt * jnp.exp(tot - acol_t)
            e_t = jnp.exp(acol_t)
            stacked = jnp.concatenate(
                [acol_t, w_t, e_t, jnp.zeros((CHUNK - 48, CHUNK), F32)], axis=0)
            tms.append(stacked.T)
        wes = [_dot(tm.astype(BF16), expand) for tm in tms]
        decs = []
        for tm in tms:
            tot_row = jnp.where(lane128_1 < 8, tm[CHUNK - 1:CHUNK, :], tm[0:1, :])
            tot512 = _dot_exact_lhs(jnp.broadcast_to(tot_row, (8, CHUNK)), expand_tot)
            decs.append(jnp.exp(tot512))
        ydiags = [jnp.zeros((CHUNK, 256), F32) for _ in cs]
        for i in range(4):
            in_head = (lane256 >= 64 * i) & (lane256 < 64 * i + 64)
            for j in range(GROUP):
                tm, acol_t, dtt = tms[j], acol_ts[j], dtts[j]
                arg = jnp.where(lower, tm[:, i:i + 1] - acol_t[i:i + 1, :],
                                tm[:, 8 + i:9 + i] - acol_t[8 + i:9 + i, :])
                scale = (jnp.where(lower, dtt[i:i + 1, :], 0.0)
                         + jnp.where(upper, dtt[8 + i:9 + i, :], 0.0))
                wmat = (gmats[j] * jnp.exp(arg) * scale).astype(BF16)
                ydiags[j] = jnp.where(in_head, _dot(wmat, xbfs[j]), ydiags[j])
        for j, c in enumerate(cs):
            w512 = wes[j][:, 0:512]
            e512 = wes[j][:, 512:1024]
            xdw = (jnp.concatenate([xus[j], xus[j]], axis=1) * w512).astype(BF16)
            sloc_s[c] = _dot(bts[j], xdw)
            dec_s[c] = decs[j]
            e_s[c] = e512
            cbf_s[c] = cbfs[j]
            yacc_s[pl.ds(r0s[j], CHUNK), :] = ydiags[j] + xus[j] * dsk
        return carry

    lax.fori_loop(0, nch // GROUP, phase_a, 0)

    fwd_order = list(range(nch))
    bwd_order = list(range(ncc - 1, -1, -1)) + list(range(nch - 1, ncc - 1, -1))
    states = [jnp.zeros((SSD_STATE, 256), F32), jnp.zeros((SSD_STATE, 256), F32)]
    for cf, cb in zip(fwd_order, bwd_order):
        for d, (c, lo) in enumerate(((cf, 0), (cb, 256))):
            sin_s[c, :, lo:lo + 256] = states[d].astype(BF16)
            states[d] = (states[d] * dec_s[c, 0:1, lo:lo + 256]
                         + sloc_s[c, :, lo:lo + 256])

    def phase_c(cs, z_ref, zrows):
        yos = [_dot(cbf_s[c], sin_s[c]) for c in cs]
        vs = []
        for c, zrow, yo in zip(cs, zrows, yos):
            r0 = c * CHUNK if isinstance(c, int) else pl.multiple_of(c * CHUNK, CHUNK)
            yo = yo * e_s[c]
            y = yacc_s[pl.ds(r0, CHUNK), :] + yo[:, 0:256] + yo[:, 256:512]
            vs.append((r0, y * _silu(z_ref[pl.ds(zrow, CHUNK), :])))

        @pl.when(g == 0)
        def _():
            for r0, v in vs:
                vun_s[pl.ds(r0, CHUNK), 0:256] = v

        @pl.when(g == 1)
        def _():
            for r0, v in vs:
                vun_s[pl.ds(r0, CHUNK), 256:512] = v

    if ctx_out:
        phase_c(list(range(ncc)), zc_ref, [c * CHUNK for c in range(ncc)])

    group_c = next(n for n in (16, 8, 4, 2, 1) if ncl % n == 0)

    def phase_c_lat(k, carry):
        ks = [k * group_c + j for j in range(group_c)]
        phase_c([kk + ncc for kk in ks], zl_ref,
                [pl.multiple_of(kk * CHUNK, CHUNK) for kk in ks])
        return carry

    lax.fori_loop(0, ncl // group_c, phase_c_lat, 0)

    @pl.when(g == SSD_GROUPS - 1)
    def _finalize():
        ng = ng_ref[...]

        def norm_rows(r0, nrows):
            v = vun_s[pl.ds(r0, nrows), :]
            ms = jnp.mean(v * v, axis=-1, keepdims=True)
            return (v * lax.rsqrt(ms + EPS) * ng).astype(BF16)

        if ctx_out:
            oc_ref[...] = norm_rows(0, n_ctx)

        fin_blocks = next(n for n in (4, 2, 1) if (n_lat // 256) % n == 0)

        def fin(k, carry):
            r0s = [pl.multiple_of((k * fin_blocks + j) * 256, 256) for j in range(fin_blocks)]
            normed = [norm_rows(n_ctx + r0, 256) for r0 in r0s]
            for r0, v in zip(r0s, normed):
                ol_ref[pl.ds(r0, 256), :] = v
            return carry

        lax.fori_loop(0, n_lat // 256 // fin_blocks, fin, 0)


def _ssd_call(p_c, p_l, conv_w8, conv_b, par, dsk, ng, batch, n_ctx, n_lat, ctx_out):
    nch = (n_ctx + n_lat) // CHUNK
    t_all = n_ctx + n_lat
    in_specs = [
        pl.BlockSpec((n_ctx, 256), lambda b, g: (b, COL_X // 2 + g)),
        pl.BlockSpec((n_lat, 256), lambda b, g: (b, COL_X // 2 + g)),
        pl.BlockSpec((n_ctx, 128), lambda b, g: (b, COL_B + g)),
        pl.BlockSpec((n_lat, 128), lambda b, g: (b, COL_B + g)),
        pl.BlockSpec((n_ctx, 128), lambda b, g: (b, COL_C + g)),
        pl.BlockSpec((n_lat, 128), lambda b, g: (b, COL_C + g)),
        pl.BlockSpec((n_ctx, 256), lambda b, g: (b, COL_Z // 2 + g)),
        pl.BlockSpec((n_lat, 256), lambda b, g: (b, COL_Z // 2 + g)),
        pl.BlockSpec((n_ctx, 128), lambda b, g: (b, COL_DT + g)),
        pl.BlockSpec((n_lat, 128), lambda b, g: (b, COL_DT + g)),
        pl.BlockSpec((8, 256), lambda b, g: (0, COL_X // 2 + g)),
        pl.BlockSpec((8, 128), lambda b, g: (0, COL_B + g)),
        pl.BlockSpec((8, 128), lambda b, g: (0, COL_C + g)),
        pl.BlockSpec((1, 256), lambda b, g: (0, COL_X // 2 + g)),
        pl.BlockSpec((1, 128), lambda b, g: (0, COL_B + g)),
        pl.BlockSpec((1, 128), lambda b, g: (0, COL_C + g)),
        pl.BlockSpec((None, 32, 128), lambda b, g: (g, 0, 0)),
        pl.BlockSpec((1, 256), lambda b, g: (0, g)),
        pl.BlockSpec((1, 512), lambda b, g: (0, 0)),
    ]
    args = [p_c, p_l, p_c, p_l, p_c, p_l, p_c, p_l, p_c, p_l,
            conv_w8, conv_w8, conv_w8, conv_b, conv_b, conv_b, par, dsk, ng]
    out_specs = [pl.BlockSpec((n_lat, 512), lambda b, g: (b, 0))]
    out_shape = [jax.ShapeDtypeStruct((batch * n_lat, 512), BF16)]
    if ctx_out:
        out_specs = [pl.BlockSpec((n_ctx, 512), lambda b, g: (b, 0))] + out_specs
        out_shape = [jax.ShapeDtypeStruct((batch * n_ctx, 512), BF16)] + out_shape
    scratch = [
        pltpu.VMEM((t_all + 3 * HALO, 512), F32),
        pltpu.VMEM((t_all, 128), F32),
        pltpu.VMEM((t_all, 256), F32),
        pltpu.VMEM((nch, SSD_STATE, 512), F32),
        pltpu.VMEM((nch, CHUNK, 512), F32),
        pltpu.VMEM((nch, CHUNK, 128), BF16),
        pltpu.VMEM((nch, SSD_STATE, 512), BF16),
        pltpu.VMEM((nch, 8, 512), F32),
        pltpu.VMEM((t_all, 512), F32),
    ]
    outs = pl.pallas_call(
        functools.partial(_ssd_kernel, n_ctx=n_ctx, n_lat=n_lat, ctx_out=ctx_out),
        grid=(batch, SSD_GROUPS),
        in_specs=in_specs,
        out_specs=out_specs,
        out_shape=out_shape,
        scratch_shapes=scratch,
        compiler_params=pltpu.CompilerParams(
            dimension_semantics=("arbitrary", "arbitrary"),
            vmem_limit_bytes=VMEM_LIMIT),
        name="ssd_scan",
    )(*args)
    if ctx_out:
        return outs[1], outs[0]
    return outs[0], None


def _rope_tables(n_lat):
    rows = n_lat // GRID_W
    row_idx = np.repeat(np.arange(rows), GRID_W).astype(np.float32)
    col_idx = (np.arange(rows * GRID_W) % GRID_W).astype(np.float32)

    def tables(dim, reps):
        quarter = dim // 4
        inv = (ROPE_BASE ** (-np.arange(quarter, dtype=np.float32) / quarter)).astype(np.float32)
        ang = np.concatenate([row_idx[:, None] * inv, col_idx[:, None] * inv], axis=-1)
        cos, sin = np.cos(ang.astype(np.float64)), np.sin(ang.astype(np.float64))
        cos2 = np.concatenate([cos, cos], axis=-1).astype(np.float32)
        sin2 = np.concatenate([-sin, sin], axis=-1).astype(np.float32)
        return jnp.asarray(np.tile(cos2, (1, reps))), jnp.asarray(np.tile(sin2, (1, reps)))

    cos_g, sin_g = tables(GQA_HEAD_DIM, 4)
    cos_d, sin_d = tables(DIFF_QK_DIM, 8)
    return {"cos_g": cos_g, "sin_g": sin_g, "cos_d": cos_d, "sin_d": sin_d}


def kernel(x, c, ctx, c_ctx, w_mod, b_mod, g_pre, g_post, w_in, conv_w, conv_b,
           a_log_fwd, a_log_bwd, dt_bias_fwd, dt_bias_bwd, d_skip, ssd_norm_g,
           q_norm_g, k_norm_g, diff_lambda, diff_norm_g, w_out):
    batch, n_lat, _ = x.shape
    n_ctx = ctx.shape[1]
    depth = w_mod.shape[0]
    assert n_lat % 512 == 0 and n_ctx % 256 == 0 and (batch * n_ctx) % 512 == 0
    assert batch + 1 <= 16

    in_perm, out_perm = _in_col_perm(), _out_row_perm()
    w_in_p = [_take_runs(w_in[l], in_perm, 1, IN_COLS).astype(BF16) for l in range(depth)]
    w_out_p = [_take_runs(w_out[l], out_perm, 0, None).astype(BF16) for l in range(depth)]
    conv_w8 = jnp.pad(conv_w, ((0, 0), (0, 8 - CONV_K), (0, 0)))
    conv_b1 = conv_b[:, None, :]

    def group16(fwd, bwd):
        out = jnp.zeros((depth, SSD_GROUPS, 16), F32)
        for g in range(SSD_GROUPS):
            out = out.at[:, g, 0:4].set(fwd[:, 4 * g:4 * g + 4])
            out = out.at[:, g, 8:12].set(bwd[:, 4 * g:4 * g + 4])
        return out

    ssd_par = jnp.broadcast_to(
        jnp.concatenate([group16(a_log_fwd, a_log_bwd),
                         group16(dt_bias_fwd, dt_bias_bwd)], axis=-1)[..., None],
        (depth, SSD_GROUPS, 32, 128))
    dsk = jnp.repeat(d_skip, SSD_HEAD_DIM, axis=1)[:, None, :]
    qg = jnp.tile(q_norm_g, (1, 4))[:, None, :]
    kg = jnp.tile(k_norm_g, (1, 2))[:, None, :]
    dng = jnp.tile(diff_norm_g, (1, 4))[:, None, :]
    tabs = _rope_tables(n_lat)

    cs = jnp.concatenate(
        [c, c_ctx[None, :], jnp.zeros((16 - batch - 1, D_MODEL), F32)], axis=0)
    mod_all = _mod_call(cs, w_mod, b_mod)

    h = x.reshape(batch * n_lat, D_MODEL)
    hc = ctx.reshape(batch * n_ctx, D_MODEL)
    for l in range(depth):
        ctx_out = l < depth - 1
        lam_init = 0.8 - 0.6 * float(np.exp(-0.3 * l))
        mod3 = mod_all[l][:, None, :]
        p_l = _inproj_call(h, mod3, g_pre[l][None, :], w_in_p[l], n_lat // INPROJ_TM, None)
        p_c = _inproj_call(hc, mod3, g_pre[l][None, :], w_in_p[l], None, batch)

        ys_l, ys_c = _ssd_call(p_c, p_l, conv_w8[l], conv_b1[l], ssd_par[l], dsk[l],
                               ssd_norm_g[l][None, :], batch, n_ctx, n_lat, ctx_out)
        yg_l = _gqa_call(p_l, p_c, p_l, tabs, qg[l], kg[l], batch, n_ctx, n_lat, True)
        diff_args = (p_l, p_c, p_l, tabs, diff_lambda[l], dng[l], batch, n_ctx, n_lat, True,
                     lam_init)
        yd_fast, row_sum_min = _diff_call(*diff_args, True)
        yd_l = lax.cond(jnp.min(row_sum_min) >= MIN_ROW_SUM,
                        lambda: yd_fast, lambda: _diff_call(*diff_args, False))
        h = _outproj_call(ys_l, yg_l, yd_l, h, mod3, g_post[l][None, :], w_out_p[l],
                          n_lat, None)
        if ctx_out:
            yg_c = _gqa_call(p_c, p_c, None, tabs, qg[l], kg[l], batch, n_ctx, 0, False)
            yd_c = _diff_call(p_c, p_c, None, tabs, diff_lambda[l], dng[l], batch,
                              n_ctx, 0, False, lam_init, False)
            hc = _outproj_call(ys_c, yg_c, yd_c, hc, mod3, g_post[l][None, :],
                               w_out_p[l], None, batch)
    return h.reshape(batch, n_lat, D_MODEL)
```

```python
import functools

import numpy as np
import jax
import jax.numpy as jnp
from jax import lax
from jax.experimental import pallas as pl
from jax.experimental.pallas import tpu as pltpu

F32 = jnp.float32
BF16 = jnp.bfloat16

D_MODEL = 1024
GRID_W = 64
ROPE_BASE = 10000.0
EPS = 1e-6
LOG2E = 1.4426950408889634

SSD_WIDTH = 512
SSD_HEADS = 8
SSD_HEAD_DIM = 64
SSD_GROUPS = 2
SSD_STATE = 128
CHUNK = 128
CONV_K = 5
HALO = 8
GQA_HEAD_DIM = 64
DIFF_HEADS = 4
DIFF_QK_DIM = 32

_IN_SPLITS = (("xbc", 1024), ("z", 512), ("dt", 16), ("gq", 256), ("gk", 128),
              ("gv", 128), ("gg", 256), ("dq", 256), ("dk", 256), ("dv", 256),
              ("dg", 256))
IN_COLS = sum(s for _, s in _IN_SPLITS)
NP = 28 * 128
COL_X, COL_B, COL_C, COL_Z, COL_GQ, COL_GK, COL_GG = 0, 4, 6, 8, 12, 14, 16
COL_DQ, COL_DK, COL_DV, COL_DG, COL_DT = 18, 20, 22, 24, 26
GQ_HEAD_ORDER = (0, 2, 1, 3)

VMEM_LIMIT = 56 * 1024 * 1024
INPROJ_TM = 512
ATTN_TQ = 1024
DIFF_TQ = 1024
ATTN_SUB = 512
DIFF_SUB = 512
ATTN_AHEAD = 2
DIFF_AHEAD = 1
SCORE_BOUND_MARGIN = 1.02
MIN_ROW_SUM = 2.0 ** -90
OUTPROJ_TM = 1024
OUTPROJ_SUB = 512


def _in_col_perm():
    off, o = {}, 0
    for name, size in _IN_SPLITS:
        off[name] = o
        o += size
    pad = IN_COLS
    cols = list(range(off["xbc"], off["xbc"] + 1024))
    cols += list(range(off["z"], off["z"] + 512))
    for h in GQ_HEAD_ORDER:
        cols += list(range(off["gq"] + 64 * h, off["gq"] + 64 * h + 64))
    cols += list(range(off["gk"], off["gk"] + 128))
    cols += list(range(off["gv"], off["gv"] + 128))
    for h in GQ_HEAD_ORDER:
        cols += list(range(off["gg"] + 64 * h, off["gg"] + 64 * h + 64))
    for name in ("dq", "dk", "dv", "dg"):
        cols += list(range(off[name], off[name] + 256))
    for g in range(SSD_GROUPS):
        blk = [pad] * 128
        for i in range(4):
            blk[i] = off["dt"] + 4 * g + i
            blk[8 + i] = off["dt"] + SSD_HEADS + 4 * g + i
        cols += blk
    assert len(cols) == NP
    return np.asarray(cols, np.int32)


def _out_row_perm():
    rows = list(range(SSD_WIDTH))
    for h in GQ_HEAD_ORDER:
        rows += list(range(SSD_WIDTH + 64 * h, SSD_WIDTH + 64 * h + 64))
    rows += list(range(SSD_WIDTH + 256, SSD_WIDTH + 512))
    return np.asarray(rows, np.int32)


def _take_runs(arr, idx, axis, pad_index):
    pieces, start = [], 0
    idx = [int(i) for i in idx]
    while start < len(idx):
        end = start + 1
        if idx[start] == pad_index:
            while end < len(idx) and idx[end] == pad_index:
                end += 1
            shape = list(arr.shape)
            shape[axis] = end - start
            pieces.append(jnp.zeros(shape, arr.dtype))
        else:
            while end < len(idx) and idx[end] == idx[end - 1] + 1 and idx[end] != pad_index:
                end += 1
            pieces.append(lax.slice_in_dim(arr, idx[start], idx[end - 1] + 1, axis=axis))
        start = end
    return jnp.concatenate(pieces, axis=axis)


def _dot(a, b):
    return jnp.dot(a, b, preferred_element_type=F32)


def _split3(x):
    hi = x.astype(BF16)
    r1 = x - hi.astype(F32)
    mid = r1.astype(BF16)
    lo = (r1 - mid.astype(F32)).astype(BF16)
    return hi, mid, lo


def _dot_exact_lhs(x, m_bf16):
    hi, mid, lo = _split3(x)
    return _dot(hi, m_bf16) + _dot(mid, m_bf16) + _dot(lo, m_bf16)


def _silu(x):
    return x * jax.nn.sigmoid(x)


def _seg_ones(width, seg):
    r = lax.broadcasted_iota(jnp.int32, (width, width), 0)
    c = lax.broadcasted_iota(jnp.int32, (width, width), 1)
    same = (r & ~(seg - 1)) == (c & ~(seg - 1))
    return jnp.where(same, 1.0, 0.0).astype(BF16)


def _seg_rms(x, seg, seg_mat):
    ss = _dot_exact_lhs(x * x, seg_mat)
    return x * lax.rsqrt(ss * (1.0 / seg) + EPS)


def _rope(x, cos, sin_signed, half):
    w = x.shape[-1]
    lane = lax.broadcasted_iota(jnp.int32, x.shape, 1)
    first = (lane & (2 * half - 1)) < half
    swapped = jnp.where(first, pltpu.roll(x, w - half, 1), pltpu.roll(x, half, 1))
    return x * cos + swapped * sin_signed


def _mod_kernel(cs_ref, w_ref, b_ref, o_ref):
    s = _silu(cs_ref[...]).astype(BF16)
    o_ref[...] = _dot(s, w_ref[...].astype(BF16)) + b_ref[...]


def _mod_call(cs, w_mod, b_mod):
    depth = w_mod.shape[0]
    nrow = cs.shape[0]
    tn = 1024
    return pl.pallas_call(
        _mod_kernel,
        grid=(depth, 3 * D_MODEL // tn),
        in_specs=[
            pl.BlockSpec((nrow, D_MODEL), lambda l, j: (0, 0)),
            pl.BlockSpec((None, D_MODEL, tn), lambda l, j: (l, 0, j)),
            pl.BlockSpec((None, 1, tn), lambda l, j: (l, 0, j)),
        ],
        out_specs=pl.BlockSpec((None, nrow, tn), lambda l, j: (l, 0, j)),
        out_shape=jax.ShapeDtypeStruct((depth, nrow, 3 * D_MODEL), F32),
        compiler_params=pltpu.CompilerParams(
            dimension_semantics=("arbitrary", "arbitrary")),
        name="mod_proj",
    )(cs, w_mod, b_mod.reshape(depth, 1, 3 * D_MODEL))


def _inproj_kernel(h_ref, mod_ref, g_ref, w_ref, o_ref):
    x = h_ref[...]
    ms = jnp.mean(x * x, axis=-1, keepdims=True)
    y = x * lax.rsqrt(ms + EPS) * g_ref[...]
    sh = mod_ref[:, 0:D_MODEL]
    sc = mod_ref[:, D_MODEL:2 * D_MODEL]
    u = (y * (1.0 + sc) + sh).astype(BF16)
    tn = 512
    for j in range(NP // tn):
        o_ref[:, j * tn:(j + 1) * tn] = _dot(u, w_ref[:, j * tn:(j + 1) * tn])


def _inproj_call(h, mod3, g_pre, w_bf16, tiles_per_row, fixed_row):
    n_tok = h.shape[0]
    tm = INPROJ_TM
    if fixed_row is None:
        mod_idx = lambda i: (i // tiles_per_row, 0, 0)
    else:
        mod_idx = lambda i: (fixed_row, 0, 0)
    return pl.pallas_call(
        _inproj_kernel,
        grid=(n_tok // tm,),
        in_specs=[
            pl.BlockSpec((tm, D_MODEL), lambda i: (i, 0)),
            pl.BlockSpec((None, 1, 3 * D_MODEL), mod_idx),
            pl.BlockSpec((1, D_MODEL), lambda i: (0, 0)),
            pl.BlockSpec((D_MODEL, NP), lambda i: (0, 0)),
        ],
        out_specs=pl.BlockSpec((tm, NP), lambda i: (i, 0)),
        out_shape=jax.ShapeDtypeStruct((n_tok, NP), F32),
        compiler_params=pltpu.CompilerParams(
            dimension_semantics=("arbitrary",), vmem_limit_bytes=VMEM_LIMIT),
        name="in_proj",
    )(h, mod3, g_pre, w_bf16)


def _outproj_kernel(ys_ref, yg_ref, yd_ref, h_ref, mod_ref, g_ref, w_ref, o_ref):
    tm = h_ref.shape[0]
    sub = min(OUTPROJ_SUB, tm)
    gt = mod_ref[:, 2 * D_MODEL:3 * D_MODEL]
    gain = g_ref[...]

    def project(r0):
        return (_dot(ys_ref[r0:r0 + sub, :], w_ref[0:512, :])
                + _dot(yg_ref[r0:r0 + sub, :], w_ref[512:768, :])
                + _dot(yd_ref[r0:r0 + sub, :], w_ref[768:1024, :]))

    o_next = project(0)
    for r0 in range(0, tm, sub):
        o = o_next
        if r0 + sub < tm:
            o_next = project(r0 + sub)
        ms = jnp.mean(o * o, axis=-1, keepdims=True)
        n = o * lax.rsqrt(ms + EPS) * gain
        o_ref[r0:r0 + sub, :] = h_ref[r0:r0 + sub, :] + gt * n


def _outproj_call(ys, yg, yd, h, mod3, g_post, w_bf16, rows_per_mod, fixed_row):
    n_tok = h.shape[0]
    tm = OUTPROJ_TM
    assert n_tok % tm == 0
    if fixed_row is None:
        assert rows_per_mod % tm == 0
        mod_idx = lambda i: (i // (rows_per_mod // tm), 0, 0)
    else:
        mod_idx = lambda i: (fixed_row, 0, 0)
    return pl.pallas_call(
        _outproj_kernel,
        grid=(n_tok // tm,),
        in_specs=[
            pl.BlockSpec((tm, 512), lambda i: (i, 0)),
            pl.BlockSpec((tm, 256), lambda i: (i, 0)),
            pl.BlockSpec((tm, 256), lambda i: (i, 0)),
            pl.BlockSpec((tm, D_MODEL), lambda i: (i, 0)),
            pl.BlockSpec((None, 1, 3 * D_MODEL), mod_idx),
            pl.BlockSpec((1, D_MODEL), lambda i: (0, 0)),
            pl.BlockSpec((D_MODEL, D_MODEL), lambda i: (0, 0)),
        ],
        out_specs=pl.BlockSpec((tm, D_MODEL), lambda i: (i, 0)),
        out_shape=jax.ShapeDtypeStruct((n_tok, D_MODEL), F32),
        compiler_params=pltpu.CompilerParams(
            dimension_semantics=("arbitrary",), vmem_limit_bytes=VMEM_LIMIT),
        name="out_proj",
    )(ys, yg, yd, h, mod3, g_post, w_bf16)


def _attend_many(lhs_list, kt_ref, vext_refs):
    def scores(i):
        s = _dot(lhs_list[i], kt_ref[...])
        return s, jnp.max(s, axis=-1, keepdims=True)

    outs = []
    n = len(lhs_list)
    ahead = [scores(i) for i in range(min(ATTN_AHEAD, n))]
    for i, vext_ref in enumerate(vext_refs):
        s, m = ahead.pop(0)
        if i + ATTN_AHEAD < n:
            ahead.append(scores(i + ATTN_AHEAD))
        p = jnp.exp2(s - m).astype(BF16)
        oe = _dot(p, vext_ref[...])
        outs.append(oe[:, 0:128] / oe[:, 128:256])
    return outs


def _attend_diff_pairs(lhs_list, kt_ref, v_refs, lam):
    n_heads = len(v_refs)

    def scores(h):
        s_a = _dot(lhs_list[2 * h], kt_ref[...])
        m_a = jnp.max(s_a, axis=-1, keepdims=True)
        s_b = _dot(lhs_list[2 * h + 1], kt_ref[...])
        m_b = jnp.max(s_b, axis=-1, keepdims=True)
        return s_a, m_a, s_b, m_b

    outs = []
    ahead = [scores(h) for h in range(min(DIFF_AHEAD, n_heads))]
    for h in range(n_heads):
        s_a, m_a, s_b, m_b = ahead.pop(0)
        if h + DIFF_AHEAD < n_heads:
            ahead.append(scores(h + DIFF_AHEAD))
        e_a = jnp.exp2(s_a - m_a)
        e_b = jnp.exp2(s_b - m_b)
        l_a = jnp.sum(e_a, axis=-1, keepdims=True)
        l_b = jnp.sum(e_b, axis=-1, keepdims=True)
        pc = (e_a - (lam * l_a / l_b) * e_b).astype(BF16)
        outs.append(_dot(pc, v_refs[h][...]) / l_a)
    return outs


def _attend_diff_pairs_bounded(lhs_list, bounds, kt_ref, v_refs, lam):
    n_heads = len(v_refs)

    def exps(h):
        e_a = jnp.exp2(_dot(lhs_list[2 * h], kt_ref[...]) - bounds[2 * h])
        e_b = jnp.exp2(_dot(lhs_list[2 * h + 1], kt_ref[...]) - bounds[2 * h + 1])
        return e_a, e_b

    outs, l_min = [], None
    ahead = [exps(h) for h in range(min(DIFF_AHEAD, n_heads))]
    for h in range(n_heads):
        e_a, e_b = ahead.pop(0)
        if h + DIFF_AHEAD < n_heads:
            ahead.append(exps(h + DIFF_AHEAD))
        l_a = jnp.sum(e_a, axis=-1, keepdims=True)
        l_b = jnp.sum(e_b, axis=-1, keepdims=True)
        pc = (e_a - (lam * l_a / l_b) * e_b).astype(BF16)
        outs.append(_dot(pc, v_refs[h][...]) / l_a)
        l_ab = jnp.minimum(l_a, l_b)
        l_min = l_ab if l_min is None else jnp.minimum(l_min, l_ab)
    return outs, jnp.min(l_min, axis=0, keepdims=True)


def _gqa_kernel(*refs, n_ctx, n_lat, rope_q):
    it = iter(refs)
    q_ref, gg_ref, kvc_ref = next(it), next(it), next(it)
    kvl_ref = next(it) if n_lat else None
    if rope_q:
        cosq_ref, sinq_ref = next(it), next(it)
    if n_lat:
        cosk_ref, sink_ref = next(it), next(it)
    qg_ref, kg_ref = next(it), next(it)
    y_ref = next(it)
    kt_s, vext_s = next(it), next(it)

    seg128 = _seg_ones(128, 64)

    @pl.when(pl.program_id(1) == 0)
    def _prep_kv():
        kc = _seg_rms(kvc_ref[:, 0:128], 64, seg128) * kg_ref[...]
        kt_s[:, 0:n_ctx] = kc.T.astype(BF16)
        vext_s[0:n_ctx, 0:128] = kvc_ref[:, 128:256].astype(BF16)
        if n_lat:
            kl = _seg_rms(kvl_ref[:, 0:128], 64, seg128) * kg_ref[...]
            kl = _rope(kl, cosk_ref[...], sink_ref[...], 32)
            kt_s[:, n_ctx:n_ctx + n_lat] = kl.T.astype(BF16)
            vext_s[n_ctx:n_ctx + n_lat, 0:128] = kvl_ref[:, 128:256].astype(BF16)
        vext_s[:, 128:256] = jnp.ones((n_ctx + n_lat, 128), BF16)

    seg256 = _seg_ones(256, 64)
    q = _seg_rms(q_ref[...], 64, seg256) * qg_ref[...]
    if rope_q:
        q = _rope(q, cosq_ref[...], sinq_ref[...], 32)
    q = q * (GQA_HEAD_DIM ** -0.5 * LOG2E)
    tq = q.shape[0]
    sub = min(ATTN_SUB, tq)
    lane = lax.broadcasted_iota(jnp.int32, (sub, 128), 1)
    lhs_list = []
    for r0 in range(0, tq, sub):
        for half in range(2):
            qh = q[r0:r0 + sub, 128 * half:128 * half + 128]
            for kv in range(2):
                in_kv = (lane >= 64 * kv) & (lane < 64 * kv + 64)
                lhs_list.append(jnp.where(in_kv, qh, 0.0).astype(BF16))
    outs = _attend_many(lhs_list, kt_s, [vext_s] * len(lhs_list))
    for j, r0 in enumerate(range(0, tq, sub)):
        for half in range(2):
            o = jnp.where(lane < 64, outs[4 * j + 2 * half], outs[4 * j + 2 * half + 1])
            gate = _silu(gg_ref[r0:r0 + sub, 128 * half:128 * half + 128])
            y_ref[r0:r0 + sub, 128 * half:128 * half + 128] = (o * gate).astype(BF16)


def _gqa_call(p_q, p_c, p_l, tabs, qg, kg, batch, n_ctx, n_lat, rope_q):
    t_total = p_q.shape[0] // batch
    tq = min(ATTN_TQ, t_total)
    nq = t_total // tq
    in_specs = [
        pl.BlockSpec((tq, 256), lambda b, i: (b * nq + i, COL_GQ // 2)),
        pl.BlockSpec((tq, 256), lambda b, i: (b * nq + i, COL_GG // 2)),
        pl.BlockSpec((n_ctx, 256), lambda b, i: (b, COL_GK // 2)),
    ]
    args = [p_q, p_q, p_c]
    if n_lat:
        in_specs.append(pl.BlockSpec((n_lat, 256), lambda b, i: (b, COL_GK // 2)))
        args.append(p_l)
    if rope_q:
        in_specs += [pl.BlockSpec((tq, 256), lambda b, i: (i, 0))] * 2
        args += [tabs["cos_g"], tabs["sin_g"]]
    if n_lat:
        in_specs += [pl.BlockSpec((n_lat, 128), lambda b, i: (0, 0))] * 2
        args += [tabs["cos_g"], tabs["sin_g"]]
    in_specs += [pl.BlockSpec((1, 256), lambda b, i: (0, 0)),
                 pl.BlockSpec((1, 128), lambda b, i: (0, 0))]
    args += [qg, kg]
    s_keys = n_ctx + n_lat
    return pl.pallas_call(
        functools.partial(_gqa_kernel, n_ctx=n_ctx, n_lat=n_lat, rope_q=rope_q),
        grid=(batch, nq),
        in_specs=in_specs,
        out_specs=pl.BlockSpec((tq, 256), lambda b, i: (b * nq + i, 0)),
        out_shape=jax.ShapeDtypeStruct((p_q.shape[0], 256), BF16),
        scratch_shapes=[pltpu.VMEM((128, s_keys), BF16),
                        pltpu.VMEM((s_keys, 256), BF16)],
        compiler_params=pltpu.CompilerParams(
            dimension_semantics=("arbitrary", "arbitrary"),
            vmem_limit_bytes=VMEM_LIMIT),
        name="gqa_attn",
    )(*args)


def _diff_kernel(*refs, n_ctx, n_lat, rope_q, lam_init, bounded):
    it = iter(refs)
    q_ref, dg_ref, kc_ref, vc_ref = next(it), next(it), next(it), next(it)
    if n_lat:
        kl_ref, vl_ref = next(it), next(it)
    if rope_q:
        cosq_ref, sinq_ref = next(it), next(it)
    if n_lat:
        cosk_ref, sink_ref = next(it), next(it)
    lam_ref, ng_ref = next(it), next(it)
    y_ref = next(it)
    lmin_ref = next(it) if bounded else None
    kt_s, vlo_s, vhi_s = next(it), next(it), next(it)
    kmax_s = next(it) if bounded else None
    s_keys = n_ctx + n_lat
    seg32 = _seg_ones(256, DIFF_QK_DIM)

    def map_norms(x):
        return jnp.sqrt(_dot((x * x).astype(BF16), seg32))

    @pl.when(pl.program_id(1) == 0)
    def _prep_kv():
        kc = kc_ref[...]
        kt_s[:, 0:n_ctx] = kc.T.astype(BF16)
        vlo_s[0:n_ctx, 0:128] = vc_ref[:, 0:128].astype(BF16)
        vhi_s[0:n_ctx, 0:128] = vc_ref[:, 128:256].astype(BF16)
        if bounded:
            kmax = jnp.max(map_norms(kc), axis=0, keepdims=True)
        if n_lat:
            kl = _rope(kl_ref[...], cosk_ref[...], sink_ref[...], 16)
            kt_s[:, n_ctx:s_keys] = kl.T.astype(BF16)
            vlo_s[n_ctx:s_keys, 0:128] = vl_ref[:, 0:128].astype(BF16)
            vhi_s[n_ctx:s_keys, 0:128] = vl_ref[:, 128:256].astype(BF16)
            if bounded:
                kmax = jnp.maximum(kmax, jnp.max(map_norms(kl), axis=0, keepdims=True))
        if bounded:
            kmax_s[...] = jnp.broadcast_to(kmax, kmax_s.shape)

    lp = lam_ref[...]
    lam = (jnp.exp(jnp.sum(lp[0:1, :] * lp[1:2, :], axis=-1, keepdims=True))
           - jnp.exp(jnp.sum(lp[2:3, :] * lp[3:4, :], axis=-1, keepdims=True))
           + lam_init)

    q = q_ref[...]
    if rope_q:
        q = _rope(q, cosq_ref[...], sinq_ref[...], 16)
    q = q * (DIFF_QK_DIM ** -0.5 * LOG2E)
    tq = q.shape[0]
    sub = min(DIFF_SUB, tq)
    lane256 = lax.broadcasted_iota(jnp.int32, (sub, 256), 1)
    lane128 = lax.broadcasted_iota(jnp.int32, (sub, 128), 1)
    seg128 = _seg_ones(128, 64)
    if bounded:
        bound_all = map_norms(q) * kmax_s[0:1, :] * SCORE_BOUND_MARGIN
    lhs_list, v_list, bounds = [], [], []
    for r0 in range(0, tq, sub):
        for mp in range(2 * DIFF_HEADS):
            in_map = (lane256 >= 32 * mp) & (lane256 < 32 * mp + 32)
            lhs_list.append(jnp.where(in_map, q[r0:r0 + sub, :], 0.0).astype(BF16))
            if bounded:
                bounds.append(bound_all[r0:r0 + sub, 32 * mp:32 * mp + 1])
        v_list += [vlo_s, vlo_s, vhi_s, vhi_s]

    if bounded:
        heads, l_min = _attend_diff_pairs_bounded(lhs_list, bounds, kt_s, v_list, lam)
        lmin_ref[...] = jnp.broadcast_to(l_min, lmin_ref.shape)
    else:
        heads = _attend_diff_pairs(lhs_list, kt_s, v_list, lam)
    for j, r0 in enumerate(range(0, tq, sub)):
        for half in range(2):
            o = jnp.where(lane128 < 64, heads[4 * j + 2 * half], heads[4 * j + 2 * half + 1])
            n = _seg_rms(o, 64, seg128) * ng_ref[:, 128 * half:128 * half + 128]
            n = n * (1.0 - lam_init)
            gate = _silu(dg_ref[r0:r0 + sub, 128 * half:128 * half + 128])
            y_ref[r0:r0 + sub, 128 * half:128 * half + 128] = (n * gate).astype(BF16)


def _diff_call(p_q, p_c, p_l, tabs, lam_params, ng, batch, n_ctx, n_lat, rope_q,
               lam_init, bounded):
    t_total = p_q.shape[0] // batch
    tq = min(DIFF_TQ, t_total)
    nq = t_total // tq
    in_specs = [
        pl.BlockSpec((tq, 256), lambda b, i: (b * nq + i, COL_DQ // 2)),
        pl.BlockSpec((tq, 256), lambda b, i: (b * nq + i, COL_DG // 2)),
        pl.BlockSpec((n_ctx, 256), lambda b, i: (b, COL_DK // 2)),
        pl.BlockSpec((n_ctx, 256), lambda b, i: (b, COL_DV // 2)),
    ]
    args = [p_q, p_q, p_c, p_c]
    if n_lat:
        in_specs += [pl.BlockSpec((n_lat, 256), lambda b, i: (b, COL_DK // 2)),
                     pl.BlockSpec((n_lat, 256), lambda b, i: (b, COL_DV // 2))]
        args += [p_l, p_l]
    if rope_q:
        in_specs += [pl.BlockSpec((tq, 256), lambda b, i: (i, 0))] * 2
        args += [tabs["cos_d"], tabs["sin_d"]]
    if n_lat:
        in_specs += [pl.BlockSpec((n_lat, 256), lambda b, i: (0, 0))] * 2
        args += [tabs["cos_d"], tabs["sin_d"]]
    in_specs += [pl.BlockSpec((4, DIFF_QK_DIM), lambda b, i: (0, 0)),
                 pl.BlockSpec((1, 256), lambda b, i: (0, 0))]
    args += [lam_params, ng]
    s_keys = n_ctx + n_lat
    out_specs = [pl.BlockSpec((tq, 256), lambda b, i: (b * nq + i, 0))]
    out_shape = [jax.ShapeDtypeStruct((p_q.shape[0], 256), BF16)]
    scratch = [pltpu.VMEM((256, s_keys), BF16),
               pltpu.VMEM((s_keys, 128), BF16),
               pltpu.VMEM((s_keys, 128), BF16)]
    if bounded:
        out_specs.append(pl.BlockSpec((None, 8, 128), lambda b, i: (b * nq + i, 0, 0)))
        out_shape.append(jax.ShapeDtypeStruct((batch * nq, 8, 128), F32))
        scratch.append(pltpu.VMEM((8, 256), F32))
    outs = pl.pallas_call(
        functools.partial(_diff_kernel, n_ctx=n_ctx, n_lat=n_lat, rope_q=rope_q,
                          lam_init=lam_init, bounded=bounded),
        grid=(batch, nq),
        in_specs=in_specs,
        out_specs=out_specs,
        out_shape=out_shape,
        scratch_shapes=scratch,
        compiler_params=pltpu.CompilerParams(
            dimension_semantics=("arbitrary", "arbitrary"),
            vmem_limit_bytes=VMEM_LIMIT),
        name="diff_attn_bounded" if bounded else "diff_attn",
    )(*args)
    return (outs[0], outs[1]) if bounded else outs[0]


def _ssd_kernel(xc_ref, xl_ref, bc_ref, bl_ref, cc_ref, cl_ref, zc_ref, zl_ref,
                dtc_ref, dtl_ref, cwx_ref, cwb_ref, cwc_ref, cbx_ref, cbb_ref,
                cbc_ref, par_ref, dsk_ref, ng_ref, *rest, n_ctx, n_lat, ctx_out):
    if ctx_out:
        oc_ref, ol_ref = rest[0], rest[1]
        rest = rest[2:]
    else:
        oc_ref, ol_ref = None, rest[0]
        rest = rest[1:]
    xp_s, dtr_s, yacc_s, sloc_s, e_s, cbf_s, sin_s, dec_s, vun_s = rest

    g = pl.program_id(1)
    ncc = n_ctx // CHUNK
    ncl = n_lat // CHUNK
    nch = ncc + ncl
    GROUP = next(n for n in (18, 9, 6, 3, 2, 1) if nch % n == 0)
    t_all = n_ctx + n_lat
    lat0 = n_ctx + 2 * HALO

    zeros_h = jnp.zeros((HALO, 512), F32)
    xp_s[0:HALO, :] = zeros_h
    xp_s[HALO:HALO + n_ctx, 0:256] = xc_ref[...]
    xp_s[HALO:HALO + n_ctx, 256:384] = bc_ref[...]
    xp_s[HALO:HALO + n_ctx, 384:512] = cc_ref[...]
    xp_s[HALO + n_ctx:lat0, :] = zeros_h
    xp_s[lat0:lat0 + n_lat, 0:256] = xl_ref[...]
    xp_s[lat0:lat0 + n_lat, 256:384] = bl_ref[...]
    xp_s[lat0:lat0 + n_lat, 384:512] = cl_ref[...]
    xp_s[lat0 + n_lat:lat0 + n_lat + HALO, :] = zeros_h

    dtr_s[0:n_ctx, :] = dtc_ref[...]
    dtr_s[n_ctx:t_all, :] = dtl_ref[...]
    a_col = -jnp.exp(par_ref[0:16, :])
    bias_col = par_ref[16:32, :]

    def _softplus(v):
        return jnp.maximum(v, 0.0) + jnp.log1p(jnp.exp(-jnp.abs(v)))

    r128 = lax.broadcasted_iota(jnp.int32, (CHUNK, CHUNK), 0)
    c128 = lax.broadcasted_iota(jnp.int32, (CHUNK, CHUNK), 1)
    lower = c128 <= r128
    upper = c128 >= r128
    tril = jnp.where(lower, 1.0, 0.0).astype(BF16)
    triu = jnp.where(upper, 1.0, 0.0).astype(BF16)
    fwd_row = lax.broadcasted_iota(jnp.int32, (16, CHUNK), 0) < 8
    fwd_row1 = lax.broadcasted_iota(jnp.int32, (16, 1), 0) < 8
    er = lax.broadcasted_iota(jnp.int32, (CHUNK, 1024), 0)
    ec = lax.broadcasted_iota(jnp.int32, (CHUNK, 1024), 1)
    src_lane = 16 + 16 * (ec >> 9) + 8 * ((ec >> 8) & 1) + ((ec >> 6) & 3)
    expand = jnp.where(er == src_lane, 1.0, 0.0).astype(BF16)
    er0 = lax.broadcasted_iota(jnp.int32, (CHUNK, 512), 0)
    ec0 = lax.broadcasted_iota(jnp.int32, (CHUNK, 512), 1)
    expand_tot = jnp.where(er0 == 8 * (ec0 >> 8) + ((ec0 >> 6) & 3), 1.0, 0.0).astype(BF16)
    lane256 = lax.broadcasted_iota(jnp.int32, (CHUNK, 256), 1)
    lane128_1 = lax.broadcasted_iota(jnp.int32, (1, CHUNK), 1)

    cw = jnp.concatenate([cwx_ref[...], cwb_ref[...], cwc_ref[...]], axis=1)
    cb = jnp.concatenate([cbx_ref[...], cbb_ref[...], cbc_ref[...]], axis=1)
    dsk = dsk_ref[...]

    def phase_a(grp, carry):
        cs = [grp * GROUP + j for j in range(GROUP)]
        r0s = [pl.multiple_of(c * CHUNK, CHUNK) for c in cs]

        def conv(c):
            wstart = pl.multiple_of(c * CHUNK + jnp.where(c >= ncc, HALO, 0), 8)
            win = xp_s[pl.ds(wstart, CHUNK + 2 * HALO), :]
            acc = jnp.broadcast_to(cb, (CHUNK, 512))
            for k in range(CONV_K):
                d = k - CONV_K // 2
                if d == 0:
                    tap = win[HALO:HALO + CHUNK, :]
                else:
                    tap = pltpu.roll(win, (-d) % (CHUNK + 2 * HALO), 0)[HALO:HALO + CHUNK, :]
                acc = acc + cw[k:k + 1, :] * tap
            return _silu(acc)

        dtts = [_softplus(dtr_s[pl.ds(r0, CHUNK), :].T[0:16, :] + bias_col) for r0 in r0s]
        a_ts = [dtt * a_col for dtt in dtts]
        acol_ts = [jnp.where(fwd_row, _dot_exact_lhs(a_t, triu), _dot_exact_lhs(a_t, tril))
                   for a_t in a_ts]
        us = [conv(c) for c in cs]
        xus = [u[:, 0:256] for u in us]
        bts = [u[:, 256:384].T.astype(BF16) for u in us]
        cbfs = [u[:, 384:512].astype(BF16) for u in us]
        xbfs = [xu.astype(BF16) for xu in xus]
        gmats = [_dot(cbf, bt) for cbf, bt in zip(cbfs, bts)]
        tms = []
        for dtt, acol_t in zip(dtts, acol_ts):
            tot = jnp.where(fwd_row1, acol_t[:, CHUNK - 1:CHUNK], acol_t[:, 0:1])
            w_t = dtt * jnp.exp(tot - acol_t)
            e_t = jnp.exp(acol_t)
            stacked = jnp.concatenate(
                [acol_t, w_t, e_t, jnp.zeros((CHUNK - 48, CHUNK), F32)], axis=0)
            tms.append(stacked.T)
        wes = [_dot(tm.astype(BF16), expand) for tm in tms]
        decs = []
        for tm in tms:
            tot_row = jnp.where(lane128_1 < 8, tm[CHUNK - 1:CHUNK, :], tm[0:1, :])
            tot512 = _dot_exact_lhs(jnp.broadcast_to(tot_row, (8, CHUNK)), expand_tot)
            decs.append(jnp.exp(tot512))
        ydiags = [jnp.zeros((CHUNK, 256), F32) for _ in cs]
        for i in range(4):
            in_head = (lane256 >= 64 * i) & (lane256 < 64 * i + 64)
            for j in range(GROUP):
                tm, acol_t, dtt = tms[j], acol_ts[j], dtts[j]
                arg = jnp.where(lower, tm[:, i:i + 1] - acol_t[i:i + 1, :],
                                tm[:, 8 + i:9 + i] - acol_t[8 + i:9 + i, :])
                scale = (jnp.where(lower, dtt[i:i + 1, :], 0.0)
                         + jnp.where(upper, dtt[8 + i:9 + i, :], 0.0))
                wmat = (gmats[j] * jnp.exp(arg) * scale).astype(BF16)
                ydiags[j] = jnp.where(in_head, _dot(wmat, xbfs[j]), ydiags[j])
        for j, c in enumerate(cs):
            w512 = wes[j][:, 0:512]
            e512 = wes[j][:, 512:1024]
            xdw = (jnp.concatenate([xus[j], xus[j]], axis=1) * w512).astype(BF16)
            sloc_s[c] = _dot(bts[j], xdw)
            dec_s[c] = decs[j]
            e_s[c] = e512
            cbf_s[c] = cbfs[j]
            yacc_s[pl.ds(r0s[j], CHUNK), :] = ydiags[j] + xus[j] * dsk
        return carry

    lax.fori_loop(0, nch // GROUP, phase_a, 0)

    fwd_order = list(range(nch))
    bwd_order = list(range(ncc - 1, -1, -1)) + list(range(nch - 1, ncc - 1, -1))
    states = [jnp.zeros((SSD_STATE, 256), F32), jnp.zeros((SSD_STATE, 256), F32)]
    for cf, cb in zip(fwd_order, bwd_order):
        for d, (c, lo) in enumerate(((cf, 0), (cb, 256))):
            sin_s[c, :, lo:lo + 256] = states[d].astype(BF16)
            states[d] = (states[d] * dec_s[c, 0:1, lo:lo + 256]
                         + sloc_s[c, :, lo:lo + 256])

    def phase_c(cs, z_ref, zrows):
        yos = [_dot(cbf_s[c], sin_s[c]) for c in cs]
        vs = []
        for c, zrow, yo in zip(cs, zrows, yos):
            r0 = c * CHUNK if isinstance(c, int) else pl.multiple_of(c * CHUNK, CHUNK)
            yo = yo * e_s[c]
            y = yacc_s[pl.ds(r0, CHUNK), :] + yo[:, 0:256] + yo[:, 256:512]
            vs.append((r0, y * _silu(z_ref[pl.ds(zrow, CHUNK), :])))

        @pl.when(g == 0)
        def _():
            for r0, v in vs:
                vun_s[pl.ds(r0, CHUNK), 0:256] = v

        @pl.when(g == 1)
        def _():
            for r0, v in vs:
                vun_s[pl.ds(r0, CHUNK), 256:512] = v

    if ctx_out:
        phase_c(list(range(ncc)), zc_ref, [c * CHUNK for c in range(ncc)])

    group_c = next(n for n in (16, 8, 4, 2, 1) if ncl % n == 0)

    def phase_c_lat(k, carry):
        ks = [k * group_c + j for j in range(group_c)]
        phase_c([kk + ncc for kk in ks], zl_ref,
                [pl.multiple_of(kk * CHUNK, CHUNK) for kk in ks])
        return carry

    lax.fori_loop(0, ncl // group_c, phase_c_lat, 0)

    @pl.when(g == SSD_GROUPS - 1)
    def _finalize():
        ng = ng_ref[...]

        def norm_rows(r0, nrows):
            v = vun_s[pl.ds(r0, nrows), :]
            ms = jnp.mean(v * v, axis=-1, keepdims=True)
            return (v * lax.rsqrt(ms + EPS) * ng).astype(BF16)

        if ctx_out:
            oc_ref[...] = norm_rows(0, n_ctx)

        fin_blocks = next(n for n in (4, 2, 1) if (n_lat // 256) % n == 0)

        def fin(k, carry):
            r0s = [pl.multiple_of((k * fin_blocks + j) * 256, 256) for j in range(fin_blocks)]
            normed = [norm_rows(n_ctx + r0, 256) for r0 in r0s]
            for r0, v in zip(r0s, normed):
                ol_ref[pl.ds(r0, 256), :] = v
            return carry

        lax.fori_loop(0, n_lat // 256 // fin_blocks, fin, 0)


def _ssd_call(p_c, p_l, conv_w8, conv_b, par, dsk, ng, batch, n_ctx, n_lat, ctx_out):
    nch = (n_ctx + n_lat) // CHUNK
    t_all = n_ctx + n_lat
    in_specs = [
        pl.BlockSpec((n_ctx, 256), lambda b, g: (b, COL_X // 2 + g)),
        pl.BlockSpec((n_lat, 256), lambda b, g: (b, COL_X // 2 + g)),
        pl.BlockSpec((n_ctx, 128), lambda b, g: (b, COL_B + g)),
        pl.BlockSpec((n_lat, 128), lambda b, g: (b, COL_B + g)),
        pl.BlockSpec((n_ctx, 128), lambda b, g: (b, COL_C + g)),
        pl.BlockSpec((n_lat, 128), lambda b, g: (b, COL_C + g)),
        pl.BlockSpec((n_ctx, 256), lambda b, g: (b, COL_Z // 2 + g)),
        pl.BlockSpec((n_lat, 256), lambda b, g: (b, COL_Z // 2 + g)),
        pl.BlockSpec((n_ctx, 128), lambda b, g: (b, COL_DT + g)),
        pl.BlockSpec((n_lat, 128), lambda b, g: (b, COL_DT + g)),
        pl.BlockSpec((8, 256), lambda b, g: (0, COL_X // 2 + g)),
        pl.BlockSpec((8, 128), lambda b, g: (0, COL_B + g)),
        pl.BlockSpec((8, 128), lambda b, g: (0, COL_C + g)),
        pl.BlockSpec((1, 256), lambda b, g: (0, COL_X // 2 + g)),
        pl.BlockSpec((1, 128), lambda b, g: (0, COL_B + g)),
        pl.BlockSpec((1, 128), lambda b, g: (0, COL_C + g)),
        pl.BlockSpec((None, 32, 128), lambda b, g: (g, 0, 0)),
        pl.BlockSpec((1, 256), lambda b, g: (0, g)),
        pl.BlockSpec((1, 512), lambda b, g: (0, 0)),
    ]
    args = [p_c, p_l, p_c, p_l, p_c, p_l, p_c, p_l, p_c, p_l,
            conv_w8, conv_w8, conv_w8, conv_b, conv_b, conv_b, par, dsk, ng]
    out_specs = [pl.BlockSpec((n_lat, 512), lambda b, g: (b, 0))]
    out_shape = [jax.ShapeDtypeStruct((batch * n_lat, 512), BF16)]
    if ctx_out:
        out_specs = [pl.BlockSpec((n_ctx, 512), lambda b, g: (b, 0))] + out_specs
        out_shape = [jax.ShapeDtypeStruct((batch * n_ctx, 512), BF16)] + out_shape
    scratch = [
        pltpu.VMEM((t_all + 3 * HALO, 512), F32),
        pltpu.VMEM((t_all, 128), F32),
        pltpu.VMEM((t_all, 256), F32),
        pltpu.VMEM((nch, SSD_STATE, 512), F32),
        pltpu.VMEM((nch, CHUNK, 512), F32),
        pltpu.VMEM((nch, CHUNK, 128), BF16),
        pltpu.VMEM((nch, SSD_STATE, 512), BF16),
        pltpu.VMEM((nch, 8, 512), F32),
        pltpu.VMEM((t_all, 512), F32),
    ]
    outs = pl.pallas_call(
        functools.partial(_ssd_kernel, n_ctx=n_ctx, n_lat=n_lat, ctx_out=ctx_out),
        grid=(batch, SSD_GROUPS),
        in_specs=in_specs,
        out_specs=out_specs,
        out_shape=out_shape,
        scratch_shapes=scratch,
        compiler_params=pltpu.CompilerParams(
            dimension_semantics=("arbitrary", "arbitrary"),
            vmem_limit_bytes=VMEM_LIMIT),
        name="ssd_scan",
    )(*args)
    if ctx_out:
        return outs[1], outs[0]
    return outs[0], None


def _rope_tables(n_lat):
    rows = n_lat // GRID_W
    row_idx = np.repeat(np.arange(rows), GRID_W).astype(np.float32)
    col_idx = (np.arange(rows * GRID_W) % GRID_W).astype(np.float32)

    def tables(dim, reps):
        quarter = dim // 4
        inv = (ROPE_BASE ** (-np.arange(quarter, dtype=np.float32) / quarter)).astype(np.float32)
        ang = np.concatenate([row_idx[:, None] * inv, col_idx[:, None] * inv], axis=-1)
        cos, sin = np.cos(ang.astype(np.float64)), np.sin(ang.astype(np.float64))
        cos2 = np.concatenate([cos, cos], axis=-1).astype(np.float32)
        sin2 = np.concatenate([-sin, sin], axis=-1).astype(np.float32)
        return jnp.asarray(np.tile(cos2, (1, reps))), jnp.asarray(np.tile(sin2, (1, reps)))

    cos_g, sin_g = tables(GQA_HEAD_DIM, 4)
    cos_d, sin_d = tables(DIFF_QK_DIM, 8)
    return {"cos_g": cos_g, "sin_g": sin_g, "cos_d": cos_d, "sin_d": sin_d}


def kernel(x, c, ctx, c_ctx, w_mod, b_mod, g_pre, g_post, w_in, conv_w, conv_b,
           a_log_fwd, a_log_bwd, dt_bias_fwd, dt_bias_bwd, d_skip, ssd_norm_g,
           q_norm_g, k_norm_g, diff_lambda, diff_norm_g, w_out):
    batch, n_lat, _ = x.shape
    n_ctx = ctx.shape[1]
    depth = w_mod.shape[0]
    assert n_lat % 512 == 0 and n_ctx % 256 == 0 and (batch * n_ctx) % 512 == 0
    assert batch + 1 <= 16

    in_perm, out_perm = _in_col_perm(), _out_row_perm()
    w_in_p = [_take_runs(w_in[l], in_perm, 1, IN_COLS).astype(BF16) for l in range(depth)]
    w_out_p = [_take_runs(w_out[l], out_perm, 0, None).astype(BF16) for l in range(depth)]
    conv_w8 = jnp.pad(conv_w, ((0, 0), (0, 8 - CONV_K), (0, 0)))
    conv_b1 = conv_b[:, None, :]

    def group16(fwd, bwd):
        out = jnp.zeros((depth, SSD_GROUPS, 16), F32)
        for g in range(SSD_GROUPS):
            out = out.at[:, g, 0:4].set(fwd[:, 4 * g:4 * g + 4])
            out = out.at[:, g, 8:12].set(bwd[:, 4 * g:4 * g + 4])
        return out

    ssd_par = jnp.broadcast_to(
        jnp.concatenate([group16(a_log_fwd, a_log_bwd),
                         group16(dt_bias_fwd, dt_bias_bwd)], axis=-1)[..., None],
        (depth, SSD_GROUPS, 32, 128))
    dsk = jnp.repeat(d_skip, SSD_HEAD_DIM, axis=1)[:, None, :]
    qg = jnp.tile(q_norm_g, (1, 4))[:, None, :]
    kg = jnp.tile(k_norm_g, (1, 2))[:, None, :]
    dng = jnp.tile(diff_norm_g, (1, 4))[:, None, :]
    tabs = _rope_tables(n_lat)

    cs = jnp.concatenate(
        [c, c_ctx[None, :], jnp.zeros((16 - batch - 1, D_MODEL), F32)], axis=0)
    mod_all = _mod_call(cs, w_mod, b_mod)

    h = x.reshape(batch * n_lat, D_MODEL)
    hc = ctx.reshape(batch * n_ctx, D_MODEL)
    for l in range(depth):
        ctx_out = l < depth - 1
        lam_init = 0.8 - 0.6 * float(np.exp(-0.3 * l))
        mod3 = mod_all[l][:, None, :]
        p_l = _inproj_call(h, mod3, g_pre[l][None, :], w_in_p[l], n_lat // INPROJ_TM, None)
        p_c = _inproj_call(hc, mod3, g_pre[l][None, :], w_in_p[l], None, batch)

        ys_l, ys_c = _ssd_call(p_c, p_l, conv_w8[l], conv_b1[l], ssd_par[l], dsk[l],
                               ssd_norm_g[l][None, :], batch, n_ctx, n_lat, ctx_out)
        yg_l = _gqa_call(p_l, p_c, p_l, tabs, qg[l], kg[l], batch, n_ctx, n_lat, True)
        diff_args = (p_l, p_c, p_l, tabs, diff_lambda[l], dng[l], batch, n_ctx, n_lat, True,
                     lam_init)
        yd_fast, row_sum_min = _diff_call(*diff_args, True)

        def project_out(yd, h_in=h, ys=ys_l, yg=yg_l, l=l, mod3=mod3):
            return _outproj_call(ys, yg, yd, h_in, mod3, g_post[l][None, :], w_out_p[l],
                                 n_lat, None)

        h = lax.cond(jnp.min(row_sum_min) >= MIN_ROW_SUM,
                     lambda: project_out(yd_fast),
                     lambda: project_out(_diff_call(*diff_args, False)))
        if ctx_out:
            yg_c = _gqa_call(p_c, p_c, None, tabs, qg[l], kg[l], batch, n_ctx, 0, False)
            yd_c = _diff_call(p_c, p_c, None, tabs, diff_lambda[l], dng[l], batch,
                              n_ctx, 0, False, lam_init, False)
            hc = _outproj_call(ys_c, yg_c, yd_c, hc, mod3, g_post[l][None, :],
                               w_out_p[l], None, batch)
    return h.reshape(batch, n_lat, D_MODEL)
```

```python
import functools

import numpy as np
import jax
import jax.numpy as jnp
from jax import lax
from jax.experimental import pallas as pl
from jax.experimental.pallas import tpu as pltpu

F32 = jnp.float32
BF16 = jnp.bfloat16

D_MODEL = 1024
GRID_W = 64
ROPE_BASE = 10000.0
EPS = 1e-6
LOG2E = 1.4426950408889634

SSD_WIDTH = 512
SSD_HEADS = 8
SSD_HEAD_DIM = 64
SSD_GROUPS = 2
SSD_STATE = 128
CHUNK = 128
CONV_K = 5
HALO = 8
GQA_HEAD_DIM = 64
DIFF_HEADS = 4
DIFF_QK_DIM = 32

_IN_SPLITS = (("xbc", 1024), ("z", 512), ("dt", 16), ("gq", 256), ("gk", 128),
              ("gv", 128), ("gg", 256), ("dq", 256), ("dk", 256), ("dv", 256),
              ("dg", 256))
IN_COLS = sum(s for _, s in _IN_SPLITS)
NP = 28 * 128
COL_X, COL_B, COL_C, COL_Z, COL_GQ, COL_GK, COL_GG = 0, 4, 6, 8, 12, 14, 16
COL_DQ, COL_DK, COL_DV, COL_DG, COL_DT = 18, 20, 22, 24, 26
GQ_HEAD_ORDER = (0, 2, 1, 3)

VMEM_LIMIT = 56 * 1024 * 1024
INPROJ_TM = 512
ATTN_TQ = 1024
DIFF_TQ = 1024
ATTN_SUB = 512
DIFF_SUB = 512
ATTN_AHEAD = 2
DIFF_AHEAD = 1
SCORE_BOUND_MARGIN = 1.02
MIN_ROW_SUM = 2.0 ** -90
OUTPROJ_TM = 1024
OUTPROJ_SUB = 512


def _in_col_perm():
    off, o = {}, 0
    for name, size in _IN_SPLITS:
        off[name] = o
        o += size
    pad = IN_COLS
    cols = list(range(off["xbc"], off["xbc"] + 1024))
    cols += list(range(off["z"], off["z"] + 512))
    for h in GQ_HEAD_ORDER:
        cols += list(range(off["gq"] + 64 * h, off["gq"] + 64 * h + 64))
    cols += list(range(off["gk"], off["gk"] + 128))
    cols += list(range(off["gv"], off["gv"] + 128))
    for h in GQ_HEAD_ORDER:
        cols += list(range(off["gg"] + 64 * h, off["gg"] + 64 * h + 64))
    for name in ("dq", "dk", "dv", "dg"):
        cols += list(range(off[name], off[name] + 256))
    for g in range(SSD_GROUPS):
        blk = [pad] * 128
        for i in range(4):
            blk[i] = off["dt"] + 4 * g + i
            blk[8 + i] = off["dt"] + SSD_HEADS + 4 * g + i
        cols += blk
    assert len(cols) == NP
    return np.asarray(cols, np.int32)


def _out_row_perm():
    rows = list(range(SSD_WIDTH))
    for h in GQ_HEAD_ORDER:
        rows += list(range(SSD_WIDTH + 64 * h, SSD_WIDTH + 64 * h + 64))
    rows += list(range(SSD_WIDTH + 256, SSD_WIDTH + 512))
    return np.asarray(rows, np.int32)


def _take_runs(arr, idx, axis, pad_index):
    pieces, start = [], 0
    idx = [int(i) for i in idx]
    while start < len(idx):
        end = start + 1
        if idx[start] == pad_index:
            while end < len(idx) and idx[end] == pad_index:
                end += 1
            shape = list(arr.shape)
            shape[axis] = end - start
            pieces.append(jnp.zeros(shape, arr.dtype))
        else:
            while end < len(idx) and idx[end] == idx[end - 1] + 1 and idx[end] != pad_index:
                end += 1
            pieces.append(lax.slice_in_dim(arr, idx[start], idx[end - 1] + 1, axis=axis))
        start = end
    return jnp.concatenate(pieces, axis=axis)


def _dot(a, b):
    return jnp.dot(a, b, preferred_element_type=F32)


def _split3(x):
    hi = x.astype(BF16)
    r1 = x - hi.astype(F32)
    mid = r1.astype(BF16)
    lo = (r1 - mid.astype(F32)).astype(BF16)
    return hi, mid, lo


def _dot_exact_lhs(x, m_bf16):
    hi, mid, lo = _split3(x)
    return _dot(hi, m_bf16) + _dot(mid, m_bf16) + _dot(lo, m_bf16)


def _silu(x):
    return x * jax.nn.sigmoid(x)


def _seg_ones(width, seg):
    r = lax.broadcasted_iota(jnp.int32, (width, width), 0)
    c = lax.broadcasted_iota(jnp.int32, (width, width), 1)
    same = (r & ~(seg - 1)) == (c & ~(seg - 1))
    return jnp.where(same, 1.0, 0.0).astype(BF16)


def _seg_rms(x, seg, seg_mat):
    ss = _dot_exact_lhs(x * x, seg_mat)
    return x * lax.rsqrt(ss * (1.0 / seg) + EPS)


def _rope(x, cos, sin_signed, half):
    w = x.shape[-1]
    lane = lax.broadcasted_iota(jnp.int32, x.shape, 1)
    first = (lane & (2 * half - 1)) < half
    swapped = jnp.where(first, pltpu.roll(x, w - half, 1), pltpu.roll(x, half, 1))
    return x * cos + swapped * sin_signed


def _mod_kernel(cs_ref, w_ref, b_ref, o_ref):
    s = _silu(cs_ref[...]).astype(BF16)
    o_ref[...] = _dot(s, w_ref[...].astype(BF16)) + b_ref[...]


def _mod_call(cs, w_mod, b_mod):
    depth = w_mod.shape[0]
    nrow = cs.shape[0]
    tn = 1024
    return pl.pallas_call(
        _mod_kernel,
        grid=(depth, 3 * D_MODEL // tn),
        in_specs=[
            pl.BlockSpec((nrow, D_MODEL), lambda l, j: (0, 0)),
            pl.BlockSpec((None, D_MODEL, tn), lambda l, j: (l, 0, j)),
            pl.BlockSpec((None, 1, tn), lambda l, j: (l, 0, j)),
        ],
        out_specs=pl.BlockSpec((None, nrow, tn), lambda l, j: (l, 0, j)),
        out_shape=jax.ShapeDtypeStruct((depth, nrow, 3 * D_MODEL), F32),
        compiler_params=pltpu.CompilerParams(
            dimension_semantics=("arbitrary", "arbitrary")),
        name="mod_proj",
    )(cs, w_mod, b_mod.reshape(depth, 1, 3 * D_MODEL))


def _inproj_kernel(h_ref, mod_ref, g_ref, w_ref, o_ref):
    x = h_ref[...]
    ms = jnp.mean(x * x, axis=-1, keepdims=True)
    y = x * lax.rsqrt(ms + EPS) * g_ref[...]
    sh = mod_ref[:, 0:D_MODEL]
    sc = mod_ref[:, D_MODEL:2 * D_MODEL]
    u = (y * (1.0 + sc) + sh).astype(BF16)
    tn = 512
    for j in range(NP // tn):
        o_ref[:, j * tn:(j + 1) * tn] = _dot(u, w_ref[:, j * tn:(j + 1) * tn])


def _inproj_call(h, mod3, g_pre, w_bf16, tiles_per_row, fixed_row):
    n_tok = h.shape[0]
    tm = INPROJ_TM
    if fixed_row is None:
        mod_idx = lambda i: (i // tiles_per_row, 0, 0)
    else:
        mod_idx = lambda i: (fixed_row, 0, 0)
    return pl.pallas_call(
        _inproj_kernel,
        grid=(n_tok // tm,),
        in_specs=[
            pl.BlockSpec((tm, D_MODEL), lambda i: (i, 0)),
            pl.BlockSpec((None, 1, 3 * D_MODEL), mod_idx),
            pl.BlockSpec((1, D_MODEL), lambda i: (0, 0)),
            pl.BlockSpec((D_MODEL, NP), lambda i: (0, 0)),
        ],
        out_specs=pl.BlockSpec((tm, NP), lambda i: (i, 0)),
        out_shape=jax.ShapeDtypeStruct((n_tok, NP), F32),
        compiler_params=pltpu.CompilerParams(
            dimension_semantics=("arbitrary",), vmem_limit_bytes=VMEM_LIMIT),
        name="in_proj",
    )(h, mod3, g_pre, w_bf16)


def _outproj_kernel(ys_ref, yg_ref, yd_ref, h_ref, mod_ref, g_ref, w_ref, o_ref):
    tm = h_ref.shape[0]
    sub = min(OUTPROJ_SUB, tm)
    gt = mod_ref[:, 2 * D_MODEL:3 * D_MODEL]
    gain = g_ref[...]

    def project(r0):
        return (_dot(ys_ref[r0:r0 + sub, :], w_ref[0:512, :])
                + _dot(yg_ref[r0:r0 + sub, :], w_ref[512:768, :])
                + _dot(yd_ref[r0:r0 + sub, :], w_ref[768:1024, :]))

    o_next = project(0)
    for r0 in range(0, tm, sub):
        o = o_next
        if r0 + sub < tm:
            o_next = project(r0 + sub)
        ms = jnp.mean(o * o, axis=-1, keepdims=True)
        n = o * lax.rsqrt(ms + EPS) * gain
        o_ref[r0:r0 + sub, :] = h_ref[r0:r0 + sub, :] + gt * n


def _outproj_call(ys, yg, yd, h, mod3, g_post, w_bf16, rows_per_mod, fixed_row):
    n_tok = h.shape[0]
    tm = OUTPROJ_TM
    assert n_tok % tm == 0
    if fixed_row is None:
        assert rows_per_mod % tm == 0
        mod_idx = lambda i: (i // (rows_per_mod // tm), 0, 0)
    else:
        mod_idx = lambda i: (fixed_row, 0, 0)
    return pl.pallas_call(
        _outproj_kernel,
        grid=(n_tok // tm,),
        in_specs=[
            pl.BlockSpec((tm, 512), lambda i: (i, 0)),
            pl.BlockSpec((tm, 256), lambda i: (i, 0)),
            pl.BlockSpec((tm, 256), lambda i: (i, 0)),
            pl.BlockSpec((tm, D_MODEL), lambda i: (i, 0)),
            pl.BlockSpec((None, 1, 3 * D_MODEL), mod_idx),
            pl.BlockSpec((1, D_MODEL), lambda i: (0, 0)),
            pl.BlockSpec((D_MODEL, D_MODEL), lambda i: (0, 0)),
        ],
        out_specs=pl.BlockSpec((tm, D_MODEL), lambda i: (i, 0)),
        out_shape=jax.ShapeDtypeStruct((n_tok, D_MODEL), F32),
        compiler_params=pltpu.CompilerParams(
            dimension_semantics=("arbitrary",), vmem_limit_bytes=VMEM_LIMIT),
        name="out_proj",
    )(ys, yg, yd, h, mod3, g_post, w_bf16)


def _attend_many(lhs_list, kt_ref, vext_refs):
    def scores(i):
        s = _dot(lhs_list[i], kt_ref[...])
        return s, jnp.max(s, axis=-1, keepdims=True)

    outs = []
    n = len(lhs_list)
    ahead = [scores(i) for i in range(min(ATTN_AHEAD, n))]
    for i, vext_ref in enumerate(vext_refs):
        s, m = ahead.pop(0)
        if i + ATTN_AHEAD < n:
            ahead.append(scores(i + ATTN_AHEAD))
        p = jnp.exp2(s - m).astype(BF16)
        oe = _dot(p, vext_ref[...])
        outs.append(oe[:, 0:128] / oe[:, 128:256])
    return outs


def _attend_diff_pairs(lhs_list, kt_ref, v_refs, lam):
    n_heads = len(v_refs)

    def scores(h):
        s_a = _dot(lhs_list[2 * h], kt_ref[...])
        m_a = jnp.max(s_a, axis=-1, keepdims=True)
        s_b = _dot(lhs_list[2 * h + 1], kt_ref[...])
        m_b = jnp.max(s_b, axis=-1, keepdims=True)
        return s_a, m_a, s_b, m_b

    outs = []
    ahead = [scores(h) for h in range(min(DIFF_AHEAD, n_heads))]
    for h in range(n_heads):
        s_a, m_a, s_b, m_b = ahead.pop(0)
        if h + DIFF_AHEAD < n_heads:
            ahead.append(scores(h + DIFF_AHEAD))
        e_a = jnp.exp2(s_a - m_a)
        e_b = jnp.exp2(s_b - m_b)
        l_a = jnp.sum(e_a, axis=-1, keepdims=True)
        l_b = jnp.sum(e_b, axis=-1, keepdims=True)
        pc = (e_a - (lam * l_a / l_b) * e_b).astype(BF16)
        outs.append(_dot(pc, v_refs[h][...]) / l_a)
    return outs


def _attend_diff_pairs_bounded(lhs_list, bounds, kt_ref, v_refs, lam):
    n_heads = len(v_refs)

    def exps(h):
        e_a = jnp.exp2(_dot(lhs_list[2 * h], kt_ref[...]) - bounds[2 * h])
        e_b = jnp.exp2(_dot(lhs_list[2 * h + 1], kt_ref[...]) - bounds[2 * h + 1])
        return e_a, e_b

    outs, l_min = [], None
    ahead = [exps(h) for h in range(min(DIFF_AHEAD, n_heads))]
    for h in range(n_heads):
        e_a, e_b = ahead.pop(0)
        if h + DIFF_AHEAD < n_heads:
            ahead.append(exps(h + DIFF_AHEAD))
        l_a = jnp.sum(e_a, axis=-1, keepdims=True)
        l_b = jnp.sum(e_b, axis=-1, keepdims=True)
        pc = (e_a - (lam * l_a / l_b) * e_b).astype(BF16)
        outs.append(_dot(pc, v_refs[h][...]) / l_a)
        l_ab = jnp.minimum(l_a, l_b)
        l_min = l_ab if l_min is None else jnp.minimum(l_min, l_ab)
    return outs, jnp.min(l_min, axis=0, keepdims=True)


def _gqa_kernel(*refs, n_ctx, n_lat, rope_q):
    it = iter(refs)
    q_ref, gg_ref, kvc_ref = next(it), next(it), next(it)
    kvl_ref = next(it) if n_lat else None
    if rope_q:
        cosq_ref, sinq_ref = next(it), next(it)
    if n_lat:
        cosk_ref, sink_ref = next(it), next(it)
    qg_ref, kg_ref = next(it), next(it)
    y_ref = next(it)
    kt_s, vext_s = next(it), next(it)

    seg128 = _seg_ones(128, 64)

    @pl.when(pl.program_id(1) == 0)
    def _prep_kv():
        kc = _seg_rms(kvc_ref[:, 0:128], 64, seg128) * kg_ref[...]
        kt_s[:, 0:n_ctx] = kc.T.astype(BF16)
        vext_s[0:n_ctx, 0:128] = kvc_ref[:, 128:256].astype(BF16)
        if n_lat:
            kl = _seg_rms(kvl_ref[:, 0:128], 64, seg128) * kg_ref[...]
            kl = _rope(kl, cosk_ref[...], sink_ref[...], 32)
            kt_s[:, n_ctx:n_ctx + n_lat] = kl.T.astype(BF16)
            vext_s[n_ctx:n_ctx + n_lat, 0:128] = kvl_ref[:, 128:256].astype(BF16)
        vext_s[:, 128:256] = jnp.ones((n_ctx + n_lat, 128), BF16)

    seg256 = _seg_ones(256, 64)
    q = _seg_rms(q_ref[...], 64, seg256) * qg_ref[...]
    if rope_q:
        q = _rope(q, cosq_ref[...], sinq_ref[...], 32)
    q = q * (GQA_HEAD_DIM ** -0.5 * LOG2E)
    tq = q.shape[0]
    sub = min(ATTN_SUB, tq)
    lane = lax.broadcasted_iota(jnp.int32, (sub, 128), 1)
    lhs_list = []
    for r0 in range(0, tq, sub):
        for half in range(2):
            qh = q[r0:r0 + sub, 128 * half:128 * half + 128]
            for kv in range(2):
                in_kv = (lane >= 64 * kv) & (lane < 64 * kv + 64)
                lhs_list.append(jnp.where(in_kv, qh, 0.0).astype(BF16))
    outs = _attend_many(lhs_list, kt_s, [vext_s] * len(lhs_list))
    for j, r0 in enumerate(range(0, tq, sub)):
        for half in range(2):
            o = jnp.where(lane < 64, outs[4 * j + 2 * half], outs[4 * j + 2 * half + 1])
            gate = _silu(gg_ref[r0:r0 + sub, 128 * half:128 * half + 128])
            y_ref[r0:r0 + sub, 128 * half:128 * half + 128] = (o * gate).astype(BF16)


def _gqa_call(p_q, p_c, p_l, tabs, qg, kg, batch, n_ctx, n_lat, rope_q):
    t_total = p_q.shape[0] // batch
    tq = min(ATTN_TQ, t_total)
    nq = t_total // tq
    in_specs = [
        pl.BlockSpec((tq, 256), lambda b, i: (b * nq + i, COL_GQ // 2)),
        pl.BlockSpec((tq, 256), lambda b, i: (b * nq + i, COL_GG // 2)),
        pl.BlockSpec((n_ctx, 256), lambda b, i: (b, COL_GK // 2)),
    ]
    args = [p_q, p_q, p_c]
    if n_lat:
        in_specs.append(pl.BlockSpec((n_lat, 256), lambda b, i: (b, COL_GK // 2)))
        args.append(p_l)
    if rope_q:
        in_specs += [pl.BlockSpec((tq, 256), lambda b, i: (i, 0))] * 2
        args += [tabs["cos_g"], tabs["sin_g"]]
    if n_lat:
        in_specs += [pl.BlockSpec((n_lat, 128), lambda b, i: (0, 0))] * 2
        args += [tabs["cos_g"], tabs["sin_g"]]
    in_specs += [pl.BlockSpec((1, 256), lambda b, i: (0, 0)),
                 pl.BlockSpec((1, 128), lambda b, i: (0, 0))]
    args += [qg, kg]
    s_keys = n_ctx + n_lat
    return pl.pallas_call(
        functools.partial(_gqa_kernel, n_ctx=n_ctx, n_lat=n_lat, rope_q=rope_q),
        grid=(batch, nq),
        in_specs=in_specs,
        out_specs=pl.BlockSpec((tq, 256), lambda b, i: (b * nq + i, 0)),
        out_shape=jax.ShapeDtypeStruct((p_q.shape[0], 256), BF16),
        scratch_shapes=[pltpu.VMEM((128, s_keys), BF16),
                        pltpu.VMEM((s_keys, 256), BF16)],
        compiler_params=pltpu.CompilerParams(
            dimension_semantics=("arbitrary", "arbitrary"),
            vmem_limit_bytes=VMEM_LIMIT),
        name="gqa_attn",
    )(*args)


def _diff_kernel(*refs, n_ctx, n_lat, rope_q, lam_init, bounded):
    it = iter(refs)
    q_ref, dg_ref, kc_ref, vc_ref = next(it), next(it), next(it), next(it)
    if n_lat:
        kl_ref, vl_ref = next(it), next(it)
    if rope_q:
        cosq_ref, sinq_ref = next(it), next(it)
    if n_lat:
        cosk_ref, sink_ref = next(it), next(it)
    lam_ref, ng_ref = next(it), next(it)
    y_ref = next(it)
    lmin_ref = next(it) if bounded else None
    kt_s, vlo_s, vhi_s = next(it), next(it), next(it)
    kmax_s = next(it) if bounded else None
    s_keys = n_ctx + n_lat
    seg32 = _seg_ones(256, DIFF_QK_DIM)

    def map_norms(x):
        return jnp.sqrt(_dot((x * x).astype(BF16), seg32))

    @pl.when(pl.program_id(1) == 0)
    def _prep_kv():
        kc = kc_ref[...]
        kt_s[:, 0:n_ctx] = kc.T.astype(BF16)
        vlo_s[0:n_ctx, 0:128] = vc_ref[:, 0:128].astype(BF16)
        vhi_s[0:n_ctx, 0:128] = vc_ref[:, 128:256].astype(BF16)
        if bounded:
            kmax = jnp.max(map_norms(kc), axis=0, keepdims=True)
        if n_lat:
            kl = _rope(kl_ref[...], cosk_ref[...], sink_ref[...], 16)
            kt_s[:, n_ctx:s_keys] = kl.T.astype(BF16)
            vlo_s[n_ctx:s_keys, 0:128] = vl_ref[:, 0:128].astype(BF16)
            vhi_s[n_ctx:s_keys, 0:128] = vl_ref[:, 128:256].astype(BF16)
            if bounded:
                kmax = jnp.maximum(kmax, jnp.max(map_norms(kl), axis=0, keepdims=True))
        if bounded:
            kmax_s[...] = jnp.broadcast_to(kmax, kmax_s.shape)

    lp = lam_ref[...]
    lam = (jnp.exp(jnp.sum(lp[0:1, :] * lp[1:2, :], axis=-1, keepdims=True))
           - jnp.exp(jnp.sum(lp[2:3, :] * lp[3:4, :], axis=-1, keepdims=True))
           + lam_init)

    q = q_ref[...]
    if rope_q:
        q = _rope(q, cosq_ref[...], sinq_ref[...], 16)
    q = q * (DIFF_QK_DIM ** -0.5 * LOG2E)
    tq = q.shape[0]
    sub = min(DIFF_SUB, tq)
    lane256 = lax.broadcasted_iota(jnp.int32, (sub, 256), 1)
    lane128 = lax.broadcasted_iota(jnp.int32, (sub, 128), 1)
    seg128 = _seg_ones(128, 64)
    if bounded:
        bound_all = map_norms(q) * kmax_s[0:1, :] * SCORE_BOUND_MARGIN
    lhs_list, v_list, bounds = [], [], []
    for r0 in range(0, tq, sub):
        for mp in range(2 * DIFF_HEADS):
            in_map = (lane256 >= 32 * mp) & (lane256 < 32 * mp + 32)
            lhs_list.append(jnp.where(in_map, q[r0:r0 + sub, :], 0.0).astype(BF16))
            if bounded:
                bounds.append(bound_all[r0:r0 + sub, 32 * mp:32 * mp + 1])
        v_list += [vlo_s, vlo_s, vhi_s, vhi_s]

    if bounded:
        heads, l_min = _attend_diff_pairs_bounded(lhs_list, bounds, kt_s, v_list, lam)
        lmin_ref[...] = jnp.broadcast_to(l_min, lmin_ref.shape)
    else:
        heads = _attend_diff_pairs(lhs_list, kt_s, v_list, lam)
    for j, r0 in enumerate(range(0, tq, sub)):
        for half in range(2):
            o = jnp.where(lane128 < 64, heads[4 * j + 2 * half], heads[4 * j + 2 * half + 1])
            n = _seg_rms(o, 64, seg128) * ng_ref[:, 128 * half:128 * half + 128]
            n = n * (1.0 - lam_init)
            gate = _silu(dg_ref[r0:r0 + sub, 128 * half:128 * half + 128])
            y_ref[r0:r0 + sub, 128 * half:128 * half + 128] = (n * gate).astype(BF16)


def _diff_call(p_q, p_c, p_l, tabs, lam_params, ng, batch, n_ctx, n_lat, rope_q,
               lam_init, bounded):
    t_total = p_q.shape[0] // batch
    tq = min(DIFF_TQ, t_total)
    nq = t_total // tq
    in_specs = [
        pl.BlockSpec((tq, 256), lambda b, i: (b * nq + i, COL_DQ // 2)),
        pl.BlockSpec((tq, 256), lambda b, i: (b * nq + i, COL_DG // 2)),
        pl.BlockSpec((n_ctx, 256), lambda b, i: (b, COL_DK // 2)),
        pl.BlockSpec((n_ctx, 256), lambda b, i: (b, COL_DV // 2)),
    ]
    args = [p_q, p_q, p_c, p_c]
    if n_lat:
        in_specs += [pl.BlockSpec((n_lat, 256), lambda b, i: (b, COL_DK // 2)),
                     pl.BlockSpec((n_lat, 256), lambda b, i: (b, COL_DV // 2))]
        args += [p_l, p_l]
    if rope_q:
        in_specs += [pl.BlockSpec((tq, 256), lambda b, i: (i, 0))] * 2
        args += [tabs["cos_d"], tabs["sin_d"]]
    if n_lat:
        in_specs += [pl.BlockSpec((n_lat, 256), lambda b, i: (0, 0))] * 2
        args += [tabs["cos_d"], tabs["sin_d"]]
    in_specs += [pl.BlockSpec((4, DIFF_QK_DIM), lambda b, i: (0, 0)),
                 pl.BlockSpec((1, 256), lambda b, i: (0, 0))]
    args += [lam_params, ng]
    s_keys = n_ctx + n_lat
    out_specs = [pl.BlockSpec((tq, 256), lambda b, i: (b * nq + i, 0))]
    out_shape = [jax.ShapeDtypeStruct((p_q.shape[0], 256), BF16)]
    scratch = [pltpu.VMEM((256, s_keys), BF16),
               pltpu.VMEM((s_keys, 128), BF16),
               pltpu.VMEM((s_keys, 128), BF16)]
    if bounded:
        out_specs.append(pl.BlockSpec((None, 8, 128), lambda b, i: (b * nq + i, 0, 0)))
        out_shape.append(jax.ShapeDtypeStruct((batch * nq, 8, 128), F32))
        scratch.append(pltpu.VMEM((8, 256), F32))
    outs = pl.pallas_call(
        functools.partial(_diff_kernel, n_ctx=n_ctx, n_lat=n_lat, rope_q=rope_q,
                          lam_init=lam_init, bounded=bounded),
        grid=(batch, nq),
        in_specs=in_specs,
        out_specs=out_specs,
        out_shape=out_shape,
        scratch_shapes=scratch,
        compiler_params=pltpu.CompilerParams(
            dimension_semantics=("arbitrary", "arbitrary"),
            vmem_limit_bytes=VMEM_LIMIT),
        name="diff_attn_bounded" if bounded else "diff_attn",
    )(*args)
    return (outs[0], outs[1]) if bounded else outs[0]


def _ssd_kernel(xc_ref, xl_ref, bc_ref, bl_ref, cc_ref, cl_ref, zc_ref, zl_ref,
                dtc_ref, dtl_ref, cwx_ref, cwb_ref, cwc_ref, cbx_ref, cbb_ref,
                cbc_ref, par_ref, dsk_ref, ng_ref, *rest, n_ctx, n_lat, ctx_out):
    if ctx_out:
        oc_ref, ol_ref = rest[0], rest[1]
        rest = rest[2:]
    else:
        oc_ref, ol_ref = None, rest[0]
        rest = rest[1:]
    xp_s, dtr_s, yacc_s, sloc_s, e_s, cbf_s, sin_s, dec_s, vun_s = rest

    g = pl.program_id(1)
    ncc = n_ctx // CHUNK
    ncl = n_lat // CHUNK
    nch = ncc + ncl
    GROUP = next(n for n in (18, 9, 6, 3, 2, 1) if nch % n == 0)
    t_all = n_ctx + n_lat
    lat0 = n_ctx + 2 * HALO

    zeros_h = jnp.zeros((HALO, 512), F32)
    xp_s[0:HALO, :] = zeros_h
    xp_s[HALO:HALO + n_ctx, 0:256] = xc_ref[...]
    xp_s[HALO:HALO + n_ctx, 256:384] = bc_ref[...]
    xp_s[HALO:HALO + n_ctx, 384:512] = cc_ref[...]
    xp_s[HALO + n_ctx:lat0, :] = zeros_h
    xp_s[lat0:lat0 + n_lat, 0:256] = xl_ref[...]
    xp_s[lat0:lat0 + n_lat, 256:384] = bl_ref[...]
    xp_s[lat0:lat0 + n_lat, 384:512] = cl_ref[...]
    xp_s[lat0 + n_lat:lat0 + n_lat + HALO, :] = zeros_h

    dtr_s[0:n_ctx, :] = dtc_ref[...]
    dtr_s[n_ctx:t_all, :] = dtl_ref[...]
    a_col = -jnp.exp(par_ref[0:16, :])
    bias_col = par_ref[16:32, :]

    def _softplus(v):
        return jnp.maximum(v, 0.0) + jnp.log1p(jnp.exp(-jnp.abs(v)))

    r128 = lax.broadcasted_iota(jnp.int32, (CHUNK, CHUNK), 0)
    c128 = lax.broadcasted_iota(jnp.int32, (CHUNK, CHUNK), 1)
    lower = c128 <= r128
    upper = c128 >= r128
    tril = jnp.where(lower, 1.0, 0.0).astype(BF16)
    triu = jnp.where(upper, 1.0, 0.0).astype(BF16)
    fwd_row = lax.broadcasted_iota(jnp.int32, (16, CHUNK), 0) < 8
    fwd_row1 = lax.broadcasted_iota(jnp.int32, (16, 1), 0) < 8
    er = lax.broadcasted_iota(jnp.int32, (CHUNK, 1024), 0)
    ec = lax.broadcasted_iota(jnp.int32, (CHUNK, 1024), 1)
    src_lane = 16 + 16 * (ec >> 9) + 8 * ((ec >> 8) & 1) + ((ec >> 6) & 3)
    expand = jnp.where(er == src_lane, 1.0, 0.0).astype(BF16)
    er0 = lax.broadcasted_iota(jnp.int32, (CHUNK, 512), 0)
    ec0 = lax.broadcasted_iota(jnp.int32, (CHUNK, 512), 1)
    expand_tot = jnp.where(er0 == 8 * (ec0 >> 8) + ((ec0 >> 6) & 3), 1.0, 0.0).astype(BF16)
    lane256 = lax.broadcasted_iota(jnp.int32, (CHUNK, 256), 1)
    lane128_1 = lax.broadcasted_iota(jnp.int32, (1, CHUNK), 1)

    cw = jnp.concatenate([cwx_ref[...], cwb_ref[...], cwc_ref[...]], axis=1)
    cb = jnp.concatenate([cbx_ref[...], cbb_ref[...], cbc_ref[...]], axis=1)
    dsk = dsk_ref[...]

    def phase_a(grp, carry):
        cs = [grp * GROUP + j for j in range(GROUP)]
        r0s = [pl.multiple_of(c * CHUNK, CHUNK) for c in cs]

        def conv(c):
            wstart = pl.multiple_of(c * CHUNK + jnp.where(c >= ncc, HALO, 0), 8)
            win = xp_s[pl.ds(wstart, CHUNK + 2 * HALO), :]
            acc = jnp.broadcast_to(cb, (CHUNK, 512))
            for k in range(CONV_K):
                d = k - CONV_K // 2
                if d == 0:
                    tap = win[HALO:HALO + CHUNK, :]
                else:
                    tap = pltpu.roll(win, (-d) % (CHUNK + 2 * HALO), 0)[HALO:HALO + CHUNK, :]
                acc = acc + cw[k:k + 1, :] * tap
            return _silu(acc)

        dtts = [_softplus(dtr_s[pl.ds(r0, CHUNK), :].T[0:16, :] + bias_col) for r0 in r0s]
        a_ts = [dtt * a_col for dtt in dtts]
        acol_ts = [jnp.where(fwd_row, _dot_exact_lhs(a_t, triu), _dot_exact_lhs(a_t, tril))
                   for a_t in a_ts]
        us = [conv(c) for c in cs]
        xus = [u[:, 0:256] for u in us]
        bts = [u[:, 256:384].T.astype(BF16) for u in us]
        cbfs = [u[:, 384:512].astype(BF16) for u in us]
        xbfs = [xu.astype(BF16) for xu in xus]
        gmats = [_dot(cbf, bt) for cbf, bt in zip(cbfs, bts)]
        tms = []
        for dtt, acol_t in zip(dtts, acol_ts):
            tot = jnp.where(fwd_row1, acol_t[:, CHUNK - 1:CHUNK], acol_t[:, 0:1])
            w_t = dtt * jnp.exp(tot - acol_t)
            e_t = jnp.exp(acol_t)
            stacked = jnp.concatenate(
                [acol_t, w_t, e_t, jnp.zeros((CHUNK - 48, CHUNK), F32)], axis=0)
            tms.append(stacked.T)
        wes = [_dot(tm.astype(BF16), expand) for tm in tms]
        decs = []
        for tm in tms:
            tot_row = jnp.where(lane128_1 < 8, tm[CHUNK - 1:CHUNK, :], tm[0:1, :])
            tot512 = _dot_exact_lhs(jnp.broadcast_to(tot_row, (8, CHUNK)), expand_tot)
            decs.append(jnp.exp(tot512))
        ydiags = [jnp.zeros((CHUNK, 256), F32) for _ in cs]
        for i in range(4):
            in_head = (lane256 >= 64 * i) & (lane256 < 64 * i + 64)
            for j in range(GROUP):
                tm, acol_t, dtt = tms[j], acol_ts[j], dtts[j]
                arg = jnp.where(lower, tm[:, i:i + 1] - acol_t[i:i + 1, :],
                                tm[:, 8 + i:9 + i] - acol_t[8 + i:9 + i, :])
                dt_f, dt_b = dtt[i:i + 1, :], dtt[8 + i:9 + i, :]
                scale = jnp.where(c128 < r128, dt_f, jnp.where(c128 > r128, dt_b, dt_f + dt_b))
                wmat = (gmats[j] * jnp.exp(arg) * scale).astype(BF16)
                ydiags[j] = jnp.where(in_head, _dot(wmat, xbfs[j]), ydiags[j])
        for j, c in enumerate(cs):
            w512 = wes[j][:, 0:512]
            e512 = wes[j][:, 512:1024]
            xdw = (jnp.concatenate([xus[j], xus[j]], axis=1) * w512).astype(BF16)
            sloc_s[c] = _dot(bts[j], xdw)
            dec_s[c] = decs[j]
            e_s[c] = e512
            cbf_s[c] = cbfs[j]
            yacc_s[pl.ds(r0s[j], CHUNK), :] = ydiags[j] + xus[j] * dsk
        return carry

    lax.fori_loop(0, nch // GROUP, phase_a, 0)

    fwd_order = list(range(nch))
    bwd_order = list(range(ncc - 1, -1, -1)) + list(range(nch - 1, ncc - 1, -1))
    states = [jnp.zeros((SSD_STATE, 256), F32), jnp.zeros((SSD_STATE, 256), F32)]
    for cf, cb in zip(fwd_order, bwd_order):
        for d, (c, lo) in enumerate(((cf, 0), (cb, 256))):
            sin_s[c, :, lo:lo + 256] = states[d].astype(BF16)
            states[d] = (states[d] * dec_s[c, 0:1, lo:lo + 256]
                         + sloc_s[c, :, lo:lo + 256])

    def phase_c(cs, z_ref, zrows):
        yos = [_dot(cbf_s[c], sin_s[c]) for c in cs]
        vs = []
        for c, zrow, yo in zip(cs, zrows, yos):
            r0 = c * CHUNK if isinstance(c, int) else pl.multiple_of(c * CHUNK, CHUNK)
            yo = yo * e_s[c]
            y = yacc_s[pl.ds(r0, CHUNK), :] + yo[:, 0:256] + yo[:, 256:512]
            vs.append((r0, y * _silu(z_ref[pl.ds(zrow, CHUNK), :])))

        @pl.when(g == 0)
        def _():
            for r0, v in vs:
                vun_s[pl.ds(r0, CHUNK), 0:256] = v

        @pl.when(g == 1)
        def _():
            for r0, v in vs:
                vun_s[pl.ds(r0, CHUNK), 256:512] = v

    if ctx_out:
        phase_c(list(range(ncc)), zc_ref, [c * CHUNK for c in range(ncc)])

    group_c = next(n for n in (16, 8, 4, 2, 1) if ncl % n == 0)

    def phase_c_lat(k, carry):
        ks = [k * group_c + j for j in range(group_c)]
        phase_c([kk + ncc for kk in ks], zl_ref,
                [pl.multiple_of(kk * CHUNK, CHUNK) for kk in ks])
        return carry

    lax.fori_loop(0, ncl // group_c, phase_c_lat, 0)

    @pl.when(g == SSD_GROUPS - 1)
    def _finalize():
        ng = ng_ref[...]

        def norm_rows(r0, nrows):
            v = vun_s[pl.ds(r0, nrows), :]
            ms = jnp.mean(v * v, axis=-1, keepdims=True)
            return (v * lax.rsqrt(ms + EPS) * ng).astype(BF16)

        if ctx_out:
            oc_ref[...] = norm_rows(0, n_ctx)

        fin_blocks = next(n for n in (4, 2, 1) if (n_lat // 256) % n == 0)

        def fin(k, carry):
            r0s = [pl.multiple_of((k * fin_blocks + j) * 256, 256) for j in range(fin_blocks)]
            normed = [norm_rows(n_ctx + r0, 256) for r0 in r0s]
            for r0, v in zip(r0s, normed):
                ol_ref[pl.ds(r0, 256), :] = v
            return carry

        lax.fori_loop(0, n_lat // 256 // fin_blocks, fin, 0)


def _ssd_call(p_c, p_l, conv_w8, conv_b, par, dsk, ng, batch, n_ctx, n_lat, ctx_out):
    nch = (n_ctx + n_lat) // CHUNK
    t_all = n_ctx + n_lat
    in_specs = [
        pl.BlockSpec((n_ctx, 256), lambda b, g: (b, COL_X // 2 + g)),
        pl.BlockSpec((n_lat, 256), lambda b, g: (b, COL_X // 2 + g)),
        pl.BlockSpec((n_ctx, 128), lambda b, g: (b, COL_B + g)),
        pl.BlockSpec((n_lat, 128), lambda b, g: (b, COL_B + g)),
        pl.BlockSpec((n_ctx, 128), lambda b, g: (b, COL_C + g)),
        pl.BlockSpec((n_lat, 128), lambda b, g: (b, COL_C + g)),
        pl.BlockSpec((n_ctx, 256), lambda b, g: (b, COL_Z // 2 + g)),
        pl.BlockSpec((n_lat, 256), lambda b, g: (b, COL_Z // 2 + g)),
        pl.BlockSpec((n_ctx, 128), lambda b, g: (b, COL_DT + g)),
        pl.BlockSpec((n_lat, 128), lambda b, g: (b, COL_DT + g)),
        pl.BlockSpec((8, 256), lambda b, g: (0, COL_X // 2 + g)),
        pl.BlockSpec((8, 128), lambda b, g: (0, COL_B + g)),
        pl.BlockSpec((8, 128), lambda b, g: (0, COL_C + g)),
        pl.BlockSpec((1, 256), lambda b, g: (0, COL_X // 2 + g)),
        pl.BlockSpec((1, 128), lambda b, g: (0, COL_B + g)),
        pl.BlockSpec((1, 128), lambda b, g: (0, COL_C + g)),
        pl.BlockSpec((None, 32, 128), lambda b, g: (g, 0, 0)),
        pl.BlockSpec((1, 256), lambda b, g: (0, g)),
        pl.BlockSpec((1, 512), lambda b, g: (0, 0)),
    ]
    args = [p_c, p_l, p_c, p_l, p_c, p_l, p_c, p_l, p_c, p_l,
            conv_w8, conv_w8, conv_w8, conv_b, conv_b, conv_b, par, dsk, ng]
    out_specs = [pl.BlockSpec((n_lat, 512), lambda b, g: (b, 0))]
    out_shape = [jax.ShapeDtypeStruct((batch * n_lat, 512), BF16)]
    if ctx_out:
        out_specs = [pl.BlockSpec((n_ctx, 512), lambda b, g: (b, 0))] + out_specs
        out_shape = [jax.ShapeDtypeStruct((batch * n_ctx, 512), BF16)] + out_shape
    scratch = [
        pltpu.VMEM((t_all + 3 * HALO, 512), F32),
        pltpu.VMEM((t_all, 128), F32),
        pltpu.VMEM((t_all, 256), F32),
        pltpu.VMEM((nch, SSD_STATE, 512), F32),
        pltpu.VMEM((nch, CHUNK, 512), F32),
        pltpu.VMEM((nch, CHUNK, 128), BF16),
        pltpu.VMEM((nch, SSD_STATE, 512), BF16),
        pltpu.VMEM((nch, 8, 512), F32),
        pltpu.VMEM((t_all, 512), F32),
    ]
    outs = pl.pallas_call(
        functools.partial(_ssd_kernel, n_ctx=n_ctx, n_lat=n_lat, ctx_out=ctx_out),
        grid=(batch, SSD_GROUPS),
        in_specs=in_specs,
        out_specs=out_specs,
        out_shape=out_shape,
        scratch_shapes=scratch,
        compiler_params=pltpu.CompilerParams(
            dimension_semantics=("arbitrary", "arbitrary"),
            vmem_limit_bytes=VMEM_LIMIT),
        name="ssd_scan",
    )(*args)
    if ctx_out:
        return outs[1], outs[0]
    return outs[0], None


def _rope_tables(n_lat):
    rows = n_lat // GRID_W
    row_idx = np.repeat(np.arange(rows), GRID_W).astype(np.float32)
    col_idx = (np.arange(rows * GRID_W) % GRID_W).astype(np.float32)

    def tables(dim, reps):
        quarter = dim // 4
        inv = (ROPE_BASE ** (-np.arange(quarter, dtype=np.float32) / quarter)).astype(np.float32)
        ang = np.concatenate([row_idx[:, None] * inv, col_idx[:, None] * inv], axis=-1)
        cos, sin = np.cos(ang.astype(np.float64)), np.sin(ang.astype(np.float64))
        cos2 = np.concatenate([cos, cos], axis=-1).astype(np.float32)
        sin2 = np.concatenate([-sin, sin], axis=-1).astype(np.float32)
        return jnp.asarray(np.tile(cos2, (1, reps))), jnp.asarray(np.tile(sin2, (1, reps)))

    cos_g, sin_g = tables(GQA_HEAD_DIM, 4)
    cos_d, sin_d = tables(DIFF_QK_DIM, 8)
    return {"cos_g": cos_g, "sin_g": sin_g, "cos_d": cos_d, "sin_d": sin_d}


def kernel(x, c, ctx, c_ctx, w_mod, b_mod, g_pre, g_post, w_in, conv_w, conv_b,
           a_log_fwd, a_log_bwd, dt_bias_fwd, dt_bias_bwd, d_skip, ssd_norm_g,
           q_norm_g, k_norm_g, diff_lambda, diff_norm_g, w_out):
    batch, n_lat, _ = x.shape
    n_ctx = ctx.shape[1]
    depth = w_mod.shape[0]
    assert n_lat % 512 == 0 and n_ctx % 256 == 0 and (batch * n_ctx) % 512 == 0
    assert batch + 1 <= 16

    in_perm, out_perm = _in_col_perm(), _out_row_perm()
    w_in_p = [_take_runs(w_in[l], in_perm, 1, IN_COLS).astype(BF16) for l in range(depth)]
    w_out_p = [_take_runs(w_out[l], out_perm, 0, None).astype(BF16) for l in range(depth)]
    conv_w8 = jnp.pad(conv_w, ((0, 0), (0, 8 - CONV_K), (0, 0)))
    conv_b1 = conv_b[:, None, :]

    def group16(fwd, bwd):
        out = jnp.zeros((depth, SSD_GROUPS, 16), F32)
        for g in range(SSD_GROUPS):
            out = out.at[:, g, 0:4].set(fwd[:, 4 * g:4 * g + 4])
            out = out.at[:, g, 8:12].set(bwd[:, 4 * g:4 * g + 4])
        return out

    ssd_par = jnp.broadcast_to(
        jnp.concatenate([group16(a_log_fwd, a_log_bwd),
                         group16(dt_bias_fwd, dt_bias_bwd)], axis=-1)[..., None],
        (depth, SSD_GROUPS, 32, 128))
    dsk = jnp.repeat(d_skip, SSD_HEAD_DIM, axis=1)[:, None, :]
    qg = jnp.tile(q_norm_g, (1, 4))[:, None, :]
    kg = jnp.tile(k_norm_g, (1, 2))[:, None, :]
    dng = jnp.tile(diff_norm_g, (1, 4))[:, None, :]
    tabs = _rope_tables(n_lat)

    cs = jnp.concatenate(
        [c, c_ctx[None, :], jnp.zeros((16 - batch - 1, D_MODEL), F32)], axis=0)
    mod_all = _mod_call(cs, w_mod, b_mod)

    h = x.reshape(batch * n_lat, D_MODEL)
    hc = ctx.reshape(batch * n_ctx, D_MODEL)
    for l in range(depth):
        ctx_out = l < depth - 1
        lam_init = 0.8 - 0.6 * float(np.exp(-0.3 * l))
        mod3 = mod_all[l][:, None, :]
        p_l = _inproj_call(h, mod3, g_pre[l][None, :], w_in_p[l], n_lat // INPROJ_TM, None)
        p_c = _inproj_call(hc, mod3, g_pre[l][None, :], w_in_p[l], None, batch)

        ys_l, ys_c = _ssd_call(p_c, p_l, conv_w8[l], conv_b1[l], ssd_par[l], dsk[l],
                               ssd_norm_g[l][None, :], batch, n_ctx, n_lat, ctx_out)
        yg_l = _gqa_call(p_l, p_c, p_l, tabs, qg[l], kg[l], batch, n_ctx, n_lat, True)
        diff_args = (p_l, p_c, p_l, tabs, diff_lambda[l], dng[l], batch, n_ctx, n_lat, True,
                     lam_init)
        yd_fast, row_sum_min = _diff_call(*diff_args, True)
        yd_l = lax.cond(jnp.min(row_sum_min) >= MIN_ROW_SUM,
                        lambda: yd_fast, lambda: _diff_call(*diff_args, False))
        h = _outproj_call(ys_l, yg_l, yd_l, h, mod3, g_post[l][None, :], w_out_p[l],
                          n_lat, None)
        if ctx_out:
            yg_c = _gqa_call(p_c, p_c, None, tabs, qg[l], kg[l], batch, n_ctx, 0, False)
            yd_c = _diff_call(p_c, p_c, None, tabs, diff_lambda[l], dng[l], batch,
                              n_ctx, 0, False, lam_init, False)
            hc = _outproj_call(ys_c, yg_c, yd_c, hc, mod3, g_post[l][None, :],
                               w_out_p[l], None, batch)
    return h.reshape(batch, n_lat, D_MODEL)
```

```python
import functools

import numpy as np
import jax
import jax.numpy as jnp
from jax import lax
from jax.experimental import pallas as pl
from jax.experimental.pallas import tpu as pltpu

F32 = jnp.float32
BF16 = jnp.bfloat16

D_MODEL = 1024
GRID_W = 64
ROPE_BASE = 10000.0
EPS = 1e-6
LOG2E = 1.4426950408889634

SSD_WIDTH = 512
SSD_HEADS = 8
SSD_HEAD_DIM = 64
SSD_GROUPS = 2
SSD_STATE = 128
CHUNK = 128
CONV_K = 5
HALO = 8
GQA_HEAD_DIM = 64
DIFF_HEADS = 4
DIFF_QK_DIM = 32

_IN_SPLITS = (("xbc", 1024), ("z", 512), ("dt", 16), ("gq", 256), ("gk", 128),
              ("gv", 128), ("gg", 256), ("dq", 256), ("dk", 256), ("dv", 256),
              ("dg", 256))
IN_COLS = sum(s for _, s in _IN_SPLITS)
NP = 28 * 128
COL_X, COL_B, COL_C, COL_Z, COL_GQ, COL_GK, COL_GG = 0, 4, 6, 8, 12, 14, 16
COL_DQ, COL_DK, COL_DV, COL_DG, COL_DT = 18, 20, 22, 24, 26
GQ_HEAD_ORDER = (0, 2, 1, 3)

VMEM_LIMIT = 56 * 1024 * 1024
INPROJ_TM = 512
ATTN_TQ = 1024
DIFF_TQ = 1024
ATTN_SUB = 1024
DIFF_SUB = 512
ATTN_AHEAD = 1
DIFF_AHEAD = 1
SCORE_BOUND_MARGIN = 1.02
MIN_ROW_SUM = 2.0 ** -90
OUTPROJ_TM = 1024
OUTPROJ_SUB = 512


def _in_col_perm():
    off, o = {}, 0
    for name, size in _IN_SPLITS:
        off[name] = o
        o += size
    pad = IN_COLS
    cols = list(range(off["xbc"], off["xbc"] + 1024))
    cols += list(range(off["z"], off["z"] + 512))
    for h in GQ_HEAD_ORDER:
        cols += list(range(off["gq"] + 64 * h, off["gq"] + 64 * h + 64))
    cols += list(range(off["gk"], off["gk"] + 128))
    cols += list(range(off["gv"], off["gv"] + 128))
    for h in GQ_HEAD_ORDER:
        cols += list(range(off["gg"] + 64 * h, off["gg"] + 64 * h + 64))
    for name in ("dq", "dk", "dv", "dg"):
        cols += list(range(off[name], off[name] + 256))
    for g in range(SSD_GROUPS):
        blk = [pad] * 128
        for i in range(4):
            blk[i] = off["dt"] + 4 * g + i
            blk[8 + i] = off["dt"] + SSD_HEADS + 4 * g + i
        cols += blk
    assert len(cols) == NP
    return np.asarray(cols, np.int32)


def _out_row_perm():
    rows = list(range(SSD_WIDTH))
    for h in GQ_HEAD_ORDER:
        rows += list(range(SSD_WIDTH + 64 * h, SSD_WIDTH + 64 * h + 64))
    rows += list(range(SSD_WIDTH + 256, SSD_WIDTH + 512))
    return np.asarray(rows, np.int32)


def _take_runs(arr, idx, axis, pad_index):
    pieces, start = [], 0
    idx = [int(i) for i in idx]
    while start < len(idx):
        end = start + 1
        if idx[start] == pad_index:
            while end < len(idx) and idx[end] == pad_index:
                end += 1
            shape = list(arr.shape)
            shape[axis] = end - start
            pieces.append(jnp.zeros(shape, arr.dtype))
        else:
            while end < len(idx) and idx[end] == idx[end - 1] + 1 and idx[end] != pad_index:
                end += 1
            pieces.append(lax.slice_in_dim(arr, idx[start], idx[end - 1] + 1, axis=axis))
        start = end
    return jnp.concatenate(pieces, axis=axis)


def _dot(a, b):
    return jnp.dot(a, b, preferred_element_type=F32)


def _split3(x):
    hi = x.astype(BF16)
    r1 = x - hi.astype(F32)
    mid = r1.astype(BF16)
    lo = (r1 - mid.astype(F32)).astype(BF16)
    return hi, mid, lo


def _dot_exact_lhs(x, m_bf16):
    hi, mid, lo = _split3(x)
    return _dot(hi, m_bf16) + _dot(mid, m_bf16) + _dot(lo, m_bf16)


def _silu(x):
    return x * jax.nn.sigmoid(x)


def _seg_ones(width, seg):
    r = lax.broadcasted_iota(jnp.int32, (width, width), 0)
    c = lax.broadcasted_iota(jnp.int32, (width, width), 1)
    same = (r & ~(seg - 1)) == (c & ~(seg - 1))
    return jnp.where(same, 1.0, 0.0).astype(BF16)


def _seg_rms(x, seg, seg_mat):
    ss = _dot_exact_lhs(x * x, seg_mat)
    return x * lax.rsqrt(ss * (1.0 / seg) + EPS)


def _rope(x, cos, sin_signed, half):
    w = x.shape[-1]
    lane = lax.broadcasted_iota(jnp.int32, x.shape, 1)
    first = (lane & (2 * half - 1)) < half
    swapped = jnp.where(first, pltpu.roll(x, w - half, 1), pltpu.roll(x, half, 1))
    return x * cos + swapped * sin_signed


def _mod_kernel(cs_ref, w_ref, b_ref, o_ref):
    s = _silu(cs_ref[...]).astype(BF16)
    o_ref[...] = _dot(s, w_ref[...].astype(BF16)) + b_ref[...]


def _mod_call(cs, w_mod, b_mod):
    depth = w_mod.shape[0]
    nrow = cs.shape[0]
    tn = 1024
    return pl.pallas_call(
        _mod_kernel,
        grid=(depth, 3 * D_MODEL // tn),
        in_specs=[
            pl.BlockSpec((nrow, D_MODEL), lambda l, j: (0, 0)),
            pl.BlockSpec((None, D_MODEL, tn), lambda l, j: (l, 0, j)),
            pl.BlockSpec((None, 1, tn), lambda l, j: (l, 0, j)),
        ],
        out_specs=pl.BlockSpec((None, nrow, tn), lambda l, j: (l, 0, j)),
        out_shape=jax.ShapeDtypeStruct((depth, nrow, 3 * D_MODEL), F32),
        compiler_params=pltpu.CompilerParams(
            dimension_semantics=("arbitrary", "arbitrary")),
        name="mod_proj",
    )(cs, w_mod, b_mod.reshape(depth, 1, 3 * D_MODEL))


def _inproj_kernel(h_ref, mod_ref, g_ref, w_ref, o_ref):
    x = h_ref[...]
    ms = jnp.mean(x * x, axis=-1, keepdims=True)
    y = x * lax.rsqrt(ms + EPS) * g_ref[...]
    sh = mod_ref[:, 0:D_MODEL]
    sc = mod_ref[:, D_MODEL:2 * D_MODEL]
    u = (y * (1.0 + sc) + sh).astype(BF16)
    tn = 512
    for j in range(NP // tn):
        o_ref[:, j * tn:(j + 1) * tn] = _dot(u, w_ref[:, j * tn:(j + 1) * tn])


def _inproj_call(h, mod3, g_pre, w_bf16, tiles_per_row, fixed_row):
    n_tok = h.shape[0]
    tm = INPROJ_TM
    if fixed_row is None:
        mod_idx = lambda i: (i // tiles_per_row, 0, 0)
    else:
        mod_idx = lambda i: (fixed_row, 0, 0)
    return pl.pallas_call(
        _inproj_kernel,
        grid=(n_tok // tm,),
        in_specs=[
            pl.BlockSpec((tm, D_MODEL), lambda i: (i, 0)),
            pl.BlockSpec((None, 1, 3 * D_MODEL), mod_idx),
            pl.BlockSpec((1, D_MODEL), lambda i: (0, 0)),
            pl.BlockSpec((D_MODEL, NP), lambda i: (0, 0)),
        ],
        out_specs=pl.BlockSpec((tm, NP), lambda i: (i, 0)),
        out_shape=jax.ShapeDtypeStruct((n_tok, NP), F32),
        compiler_params=pltpu.CompilerParams(
            dimension_semantics=("arbitrary",), vmem_limit_bytes=VMEM_LIMIT),
        name="in_proj",
    )(h, mod3, g_pre, w_bf16)


def _outproj_kernel(ys_ref, yg_ref, yd_ref, h_ref, mod_ref, g_ref, w_ref, o_ref):
    tm = h_ref.shape[0]
    sub = min(OUTPROJ_SUB, tm)
    gt = mod_ref[:, 2 * D_MODEL:3 * D_MODEL]
    gain = g_ref[...]

    def project(r0):
        return (_dot(ys_ref[r0:r0 + sub, :], w_ref[0:512, :])
                + _dot(yg_ref[r0:r0 + sub, :], w_ref[512:768, :])
                + _dot(yd_ref[r0:r0 + sub, :], w_ref[768:1024, :]))

    o_next = project(0)
    for r0 in range(0, tm, sub):
        o = o_next
        if r0 + sub < tm:
            o_next = project(r0 + sub)
        ms = jnp.mean(o * o, axis=-1, keepdims=True)
        n = o * lax.rsqrt(ms + EPS) * gain
        o_ref[r0:r0 + sub, :] = h_ref[r0:r0 + sub, :] + gt * n


def _outproj_call(ys, yg, yd, h, mod3, g_post, w_bf16, rows_per_mod, fixed_row):
    n_tok = h.shape[0]
    tm = OUTPROJ_TM
    assert n_tok % tm == 0
    if fixed_row is None:
        assert rows_per_mod % tm == 0
        mod_idx = lambda i: (i // (rows_per_mod // tm), 0, 0)
    else:
        mod_idx = lambda i: (fixed_row, 0, 0)
    return pl.pallas_call(
        _outproj_kernel,
        grid=(n_tok // tm,),
        in_specs=[
            pl.BlockSpec((tm, 512), lambda i: (i, 0)),
            pl.BlockSpec((tm, 256), lambda i: (i, 0)),
            pl.BlockSpec((tm, 256), lambda i: (i, 0)),
            pl.BlockSpec((tm, D_MODEL), lambda i: (i, 0)),
            pl.BlockSpec((None, 1, 3 * D_MODEL), mod_idx),
            pl.BlockSpec((1, D_MODEL), lambda i: (0, 0)),
            pl.BlockSpec((D_MODEL, D_MODEL), lambda i: (0, 0)),
        ],
        out_specs=pl.BlockSpec((tm, D_MODEL), lambda i: (i, 0)),
        out_shape=jax.ShapeDtypeStruct((n_tok, D_MODEL), F32),
        compiler_params=pltpu.CompilerParams(
            dimension_semantics=("arbitrary",), vmem_limit_bytes=VMEM_LIMIT),
        name="out_proj",
    )(ys, yg, yd, h, mod3, g_post, w_bf16)


def _attend_many(lhs_list, kt_ref, vext_refs):
    def scores(i):
        s = _dot(lhs_list[i], kt_ref[...])
        return s, jnp.max(s, axis=-1, keepdims=True)

    outs = []
    n = len(lhs_list)
    ahead = [scores(i) for i in range(min(ATTN_AHEAD, n))]
    for i, vext_ref in enumerate(vext_refs):
        s, m = ahead.pop(0)
        if i + ATTN_AHEAD < n:
            ahead.append(scores(i + ATTN_AHEAD))
        p = jnp.exp2(s - m).astype(BF16)
        oe = _dot(p, vext_ref[...])
        outs.append(oe[:, 0:128] / oe[:, 128:256])
    return outs


def _attend_diff_pairs(lhs_list, kt_ref, v_refs, lam):
    n_heads = len(v_refs)

    def scores(h):
        s_a = _dot(lhs_list[2 * h], kt_ref[...])
        m_a = jnp.max(s_a, axis=-1, keepdims=True)
        s_b = _dot(lhs_list[2 * h + 1], kt_ref[...])
        m_b = jnp.max(s_b, axis=-1, keepdims=True)
        return s_a, m_a, s_b, m_b

    outs = []
    ahead = [scores(h) for h in range(min(DIFF_AHEAD, n_heads))]
    for h in range(n_heads):
        s_a, m_a, s_b, m_b = ahead.pop(0)
        if h + DIFF_AHEAD < n_heads:
            ahead.append(scores(h + DIFF_AHEAD))
        e_a = jnp.exp2(s_a - m_a)
        e_b = jnp.exp2(s_b - m_b)
        l_a = jnp.sum(e_a, axis=-1, keepdims=True)
        l_b = jnp.sum(e_b, axis=-1, keepdims=True)
        pc = (e_a - (lam * l_a / l_b) * e_b).astype(BF16)
        outs.append(_dot(pc, v_refs[h][...]) / l_a)
    return outs


def _attend_diff_pairs_bounded(lhs_list, bounds, kt_ref, v_refs, lam):
    n_heads = len(v_refs)

    def exps(h):
        e_a = jnp.exp2(_dot(lhs_list[2 * h], kt_ref[...]) - bounds[2 * h])
        e_b = jnp.exp2(_dot(lhs_list[2 * h + 1], kt_ref[...]) - bounds[2 * h + 1])
        return e_a, e_b

    outs, l_min = [], None
    ahead = [exps(h) for h in range(min(DIFF_AHEAD, n_heads))]
    for h in range(n_heads):
        e_a, e_b = ahead.pop(0)
        if h + DIFF_AHEAD < n_heads:
            ahead.append(exps(h + DIFF_AHEAD))
        l_a = jnp.sum(e_a, axis=-1, keepdims=True)
        l_b = jnp.sum(e_b, axis=-1, keepdims=True)
        pc = (e_a - (lam * l_a / l_b) * e_b).astype(BF16)
        outs.append(_dot(pc, v_refs[h][...]) / l_a)
        l_ab = jnp.minimum(l_a, l_b)
        l_min = l_ab if l_min is None else jnp.minimum(l_min, l_ab)
    return outs, jnp.min(l_min, axis=0, keepdims=True)


def _gqa_kernel(*refs, n_ctx, n_lat, rope_q):
    it = iter(refs)
    q_ref, gg_ref, kvc_ref = next(it), next(it), next(it)
    kvl_ref = next(it) if n_lat else None
    if rope_q:
        cosq_ref, sinq_ref = next(it), next(it)
    if n_lat:
        cosk_ref, sink_ref = next(it), next(it)
    qg_ref, kg_ref = next(it), next(it)
    y_ref = next(it)
    kt_s, vext_s = next(it), next(it)

    seg128 = _seg_ones(128, 64)

    @pl.when(pl.program_id(1) == 0)
    def _prep_kv():
        kc = _seg_rms(kvc_ref[:, 0:128], 64, seg128) * kg_ref[...]
        kt_s[:, 0:n_ctx] = kc.T.astype(BF16)
        vext_s[0:n_ctx, 0:128] = kvc_ref[:, 128:256].astype(BF16)
        if n_lat:
            kl = _seg_rms(kvl_ref[:, 0:128], 64, seg128) * kg_ref[...]
            kl = _rope(kl, cosk_ref[...], sink_ref[...], 32)
            kt_s[:, n_ctx:n_ctx + n_lat] = kl.T.astype(BF16)
            vext_s[n_ctx:n_ctx + n_lat, 0:128] = kvl_ref[:, 128:256].astype(BF16)
        vext_s[:, 128:256] = jnp.ones((n_ctx + n_lat, 128), BF16)

    seg256 = _seg_ones(256, 64)
    q = _seg_rms(q_ref[...], 64, seg256) * qg_ref[...]
    if rope_q:
        q = _rope(q, cosq_ref[...], sinq_ref[...], 32)
    q = q * (GQA_HEAD_DIM ** -0.5 * LOG2E)
    tq = q.shape[0]
    sub = min(ATTN_SUB, tq)
    lane = lax.broadcasted_iota(jnp.int32, (sub, 128), 1)
    lhs_list = []
    for r0 in range(0, tq, sub):
        for half in range(2):
            qh = q[r0:r0 + sub, 128 * half:128 * half + 128]
            for kv in range(2):
                in_kv = (lane >= 64 * kv) & (lane < 64 * kv + 64)
                lhs_list.append(jnp.where(in_kv, qh, 0.0).astype(BF16))
    outs = _attend_many(lhs_list, kt_s, [vext_s] * len(lhs_list))
    for j, r0 in enumerate(range(0, tq, sub)):
        for half in range(2):
            o = jnp.where(lane < 64, outs[4 * j + 2 * half], outs[4 * j + 2 * half + 1])
            gate = _silu(gg_ref[r0:r0 + sub, 128 * half:128 * half + 128])
            y_ref[r0:r0 + sub, 128 * half:128 * half + 128] = (o * gate).astype(BF16)


def _gqa_call(p_q, p_c, p_l, tabs, qg, kg, batch, n_ctx, n_lat, rope_q):
    t_total = p_q.shape[0] // batch
    tq = min(ATTN_TQ, t_total)
    nq = t_total // tq
    in_specs = [
        pl.BlockSpec((tq, 256), lambda b, i: (b * nq + i, COL_GQ // 2)),
        pl.BlockSpec((tq, 256), lambda b, i: (b * nq + i, COL_GG // 2)),
        pl.BlockSpec((n_ctx, 256), lambda b, i: (b, COL_GK // 2)),
    ]
    args = [p_q, p_q, p_c]
    if n_lat:
        in_specs.append(pl.BlockSpec((n_lat, 256), lambda b, i: (b, COL_GK // 2)))
        args.append(p_l)
    if rope_q:
        in_specs += [pl.BlockSpec((tq, 256), lambda b, i: (i, 0))] * 2
        args += [tabs["cos_g"], tabs["sin_g"]]
    if n_lat:
        in_specs += [pl.BlockSpec((n_lat, 128), lambda b, i: (0, 0))] * 2
        args += [tabs["cos_g"], tabs["sin_g"]]
    in_specs += [pl.BlockSpec((1, 256), lambda b, i: (0, 0)),
                 pl.BlockSpec((1, 128), lambda b, i: (0, 0))]
    args += [qg, kg]
    s_keys = n_ctx + n_lat
    return pl.pallas_call(
        functools.partial(_gqa_kernel, n_ctx=n_ctx, n_lat=n_lat, rope_q=rope_q),
        grid=(batch, nq),
        in_specs=in_specs,
        out_specs=pl.BlockSpec((tq, 256), lambda b, i: (b * nq + i, 0)),
        out_shape=jax.ShapeDtypeStruct((p_q.shape[0], 256), BF16),
        scratch_shapes=[pltpu.VMEM((128, s_keys), BF16),
                        pltpu.VMEM((s_keys, 256), BF16)],
        compiler_params=pltpu.CompilerParams(
            dimension_semantics=("arbitrary", "arbitrary"),
            vmem_limit_bytes=VMEM_LIMIT),
        name="gqa_attn",
    )(*args)


def _diff_kernel(*refs, n_ctx, n_lat, rope_q, lam_init, bounded):
    it = iter(refs)
    q_ref, dg_ref, kc_ref, vc_ref = next(it), next(it), next(it), next(it)
    if n_lat:
        kl_ref, vl_ref = next(it), next(it)
    if rope_q:
        cosq_ref, sinq_ref = next(it), next(it)
    if n_lat:
        cosk_ref, sink_ref = next(it), next(it)
    lam_ref, ng_ref = next(it), next(it)
    y_ref = next(it)
    lmin_ref = next(it) if bounded else None
    kt_s, vlo_s, vhi_s = next(it), next(it), next(it)
    kmax_s = next(it) if bounded else None
    s_keys = n_ctx + n_lat
    seg32 = _seg_ones(256, DIFF_QK_DIM)

    def map_norms(x):
        return jnp.sqrt(_dot((x * x).astype(BF16), seg32))

    @pl.when(pl.program_id(1) == 0)
    def _prep_kv():
        kc = kc_ref[...]
        kt_s[:, 0:n_ctx] = kc.T.astype(BF16)
        vlo_s[0:n_ctx, 0:128] = vc_ref[:, 0:128].astype(BF16)
        vhi_s[0:n_ctx, 0:128] = vc_ref[:, 128:256].astype(BF16)
        if bounded:
            kmax = jnp.max(map_norms(kc), axis=0, keepdims=True)
        if n_lat:
            kl = _rope(kl_ref[...], cosk_ref[...], sink_ref[...], 16)
            kt_s[:, n_ctx:s_keys] = kl.T.astype(BF16)
            vlo_s[n_ctx:s_keys, 0:128] = vl_ref[:, 0:128].astype(BF16)
            vhi_s[n_ctx:s_keys, 0:128] = vl_ref[:, 128:256].astype(BF16)
            if bounded:
                kmax = jnp.maximum(kmax, jnp.max(map_norms(kl), axis=0, keepdims=True))
        if bounded:
            kmax_s[...] = jnp.broadcast_to(kmax, kmax_s.shape)

    lp = lam_ref[...]
    lam = (jnp.exp(jnp.sum(lp[0:1, :] * lp[1:2, :], axis=-1, keepdims=True))
           - jnp.exp(jnp.sum(lp[2:3, :] * lp[3:4, :], axis=-1, keepdims=True))
           + lam_init)

    q = q_ref[...]
    if rope_q:
        q = _rope(q, cosq_ref[...], sinq_ref[...], 16)
    q = q * (DIFF_QK_DIM ** -0.5 * LOG2E)
    tq = q.shape[0]
    sub = min(DIFF_SUB, tq)
    lane256 = lax.broadcasted_iota(jnp.int32, (sub, 256), 1)
    lane128 = lax.broadcasted_iota(jnp.int32, (sub, 128), 1)
    seg128 = _seg_ones(128, 64)
    if bounded:
        bound_all = map_norms(q) * kmax_s[0:1, :] * SCORE_BOUND_MARGIN
    lhs_list, v_list, bounds = [], [], []
    for r0 in range(0, tq, sub):
        for mp in range(2 * DIFF_HEADS):
            in_map = (lane256 >= 32 * mp) & (lane256 < 32 * mp + 32)
            lhs_list.append(jnp.where(in_map, q[r0:r0 + sub, :], 0.0).astype(BF16))
            if bounded:
                bounds.append(bound_all[r0:r0 + sub, 32 * mp:32 * mp + 1])
        v_list += [vlo_s, vlo_s, vhi_s, vhi_s]

    if bounded:
        heads, l_min = _attend_diff_pairs_bounded(lhs_list, bounds, kt_s, v_list, lam)
        lmin_ref[...] = jnp.broadcast_to(l_min, lmin_ref.shape)
    else:
        heads = _attend_diff_pairs(lhs_list, kt_s, v_list, lam)
    for j, r0 in enumerate(range(0, tq, sub)):
        for half in range(2):
            o = jnp.where(lane128 < 64, heads[4 * j + 2 * half], heads[4 * j + 2 * half + 1])
            n = _seg_rms(o, 64, seg128) * ng_ref[:, 128 * half:128 * half + 128]
            n = n * (1.0 - lam_init)
            gate = _silu(dg_ref[r0:r0 + sub, 128 * half:128 * half + 128])
            y_ref[r0:r0 + sub, 128 * half:128 * half + 128] = (n * gate).astype(BF16)


def _diff_call(p_q, p_c, p_l, tabs, lam_params, ng, batch, n_ctx, n_lat, rope_q,
               lam_init, bounded):
    t_total = p_q.shape[0] // batch
    tq = min(DIFF_TQ, t_total)
    nq = t_total // tq
    in_specs = [
        pl.BlockSpec((tq, 256), lambda b, i: (b * nq + i, COL_DQ // 2)),
        pl.BlockSpec((tq, 256), lambda b, i: (b * nq + i, COL_DG // 2)),
        pl.BlockSpec((n_ctx, 256), lambda b, i: (b, COL_DK // 2)),
        pl.BlockSpec((n_ctx, 256), lambda b, i: (b, COL_DV // 2)),
    ]
    args = [p_q, p_q, p_c, p_c]
    if n_lat:
        in_specs += [pl.BlockSpec((n_lat, 256), lambda b, i: (b, COL_DK // 2)),
                     pl.BlockSpec((n_lat, 256), lambda b, i: (b, COL_DV // 2))]
        args += [p_l, p_l]
    if rope_q:
        in_specs += [pl.BlockSpec((tq, 256), lambda b, i: (i, 0))] * 2
        args += [tabs["cos_d"], tabs["sin_d"]]
    if n_lat:
        in_specs += [pl.BlockSpec((n_lat, 256), lambda b, i: (0, 0))] * 2
        args += [tabs["cos_d"], tabs["sin_d"]]
    in_specs += [pl.BlockSpec((4, DIFF_QK_DIM), lambda b, i: (0, 0)),
                 pl.BlockSpec((1, 256), lambda b, i: (0, 0))]
    args += [lam_params, ng]
    s_keys = n_ctx + n_lat
    out_specs = [pl.BlockSpec((tq, 256), lambda b, i: (b * nq + i, 0))]
    out_shape = [jax.ShapeDtypeStruct((p_q.shape[0], 256), BF16)]
    scratch = [pltpu.VMEM((256, s_keys), BF16),
               pltpu.VMEM((s_keys, 128), BF16),
               pltpu.VMEM((s_keys, 128), BF16)]
    if bounded:
        out_specs.append(pl.BlockSpec((None, 8, 128), lambda b, i: (b * nq + i, 0, 0)))
        out_shape.append(jax.ShapeDtypeStruct((batch * nq, 8, 128), F32))
        scratch.append(pltpu.VMEM((8, 256), F32))
    outs = pl.pallas_call(
        functools.partial(_diff_kernel, n_ctx=n_ctx, n_lat=n_lat, rope_q=rope_q,
                          lam_init=lam_init, bounded=bounded),
        grid=(batch, nq),
        in_specs=in_specs,
        out_specs=out_specs,
        out_shape=out_shape,
        scratch_shapes=scratch,
        compiler_params=pltpu.CompilerParams(
            dimension_semantics=("arbitrary", "arbitrary"),
            vmem_limit_bytes=VMEM_LIMIT),
        name="diff_attn_bounded" if bounded else "diff_attn",
    )(*args)
    return (outs[0], outs[1]) if bounded else outs[0]


def _ssd_kernel(xc_ref, xl_ref, bc_ref, bl_ref, cc_ref, cl_ref, zc_ref, zl_ref,
                dtc_ref, dtl_ref, cwx_ref, cwb_ref, cwc_ref, cbx_ref, cbb_ref,
                cbc_ref, par_ref, dsk_ref, ng_ref, *rest, n_ctx, n_lat, ctx_out):
    if ctx_out:
        oc_ref, ol_ref = rest[0], rest[1]
        rest = rest[2:]
    else:
        oc_ref, ol_ref = None, rest[0]
        rest = rest[1:]
    xp_s, dtr_s, yacc_s, sloc_s, e_s, cbf_s, sin_s, dec_s, vun_s = rest

    g = pl.program_id(1)
    ncc = n_ctx // CHUNK
    ncl = n_lat // CHUNK
    nch = ncc + ncl
    GROUP = next(n for n in (18, 9, 6, 3, 2, 1) if nch % n == 0)
    t_all = n_ctx + n_lat
    lat0 = n_ctx + 2 * HALO

    zeros_h = jnp.zeros((HALO, 512), F32)
    xp_s[0:HALO, :] = zeros_h
    xp_s[HALO:HALO + n_ctx, 0:256] = xc_ref[...]
    xp_s[HALO:HALO + n_ctx, 256:384] = bc_ref[...]
    xp_s[HALO:HALO + n_ctx, 384:512] = cc_ref[...]
    xp_s[HALO + n_ctx:lat0, :] = zeros_h
    xp_s[lat0:lat0 + n_lat, 0:256] = xl_ref[...]
    xp_s[lat0:lat0 + n_lat, 256:384] = bl_ref[...]
    xp_s[lat0:lat0 + n_lat, 384:512] = cl_ref[...]
    xp_s[lat0 + n_lat:lat0 + n_lat + HALO, :] = zeros_h

    dtr_s[0:n_ctx, :] = dtc_ref[...]
    dtr_s[n_ctx:t_all, :] = dtl_ref[...]
    a_col = -jnp.exp(par_ref[0:16, :])
    bias_col = par_ref[16:32, :]

    def _softplus(v):
        return jnp.maximum(v, 0.0) + jnp.log1p(jnp.exp(-jnp.abs(v)))

    r128 = lax.broadcasted_iota(jnp.int32, (CHUNK, CHUNK), 0)
    c128 = lax.broadcasted_iota(jnp.int32, (CHUNK, CHUNK), 1)
    lower = c128 <= r128
    upper = c128 >= r128
    tril = jnp.where(lower, 1.0, 0.0).astype(BF16)
    triu = jnp.where(upper, 1.0, 0.0).astype(BF16)
    fwd_row = lax.broadcasted_iota(jnp.int32, (16, CHUNK), 0) < 8
    fwd_row1 = lax.broadcasted_iota(jnp.int32, (16, 1), 0) < 8
    er = lax.broadcasted_iota(jnp.int32, (CHUNK, 1024), 0)
    ec = lax.broadcasted_iota(jnp.int32, (CHUNK, 1024), 1)
    src_lane = 16 + 16 * (ec >> 9) + 8 * ((ec >> 8) & 1) + ((ec >> 6) & 3)
    expand = jnp.where(er == src_lane, 1.0, 0.0).astype(BF16)
    er0 = lax.broadcasted_iota(jnp.int32, (CHUNK, 512), 0)
    ec0 = lax.broadcasted_iota(jnp.int32, (CHUNK, 512), 1)
    expand_tot = jnp.where(er0 == 8 * (ec0 >> 8) + ((ec0 >> 6) & 3), 1.0, 0.0).astype(BF16)
    lane256 = lax.broadcasted_iota(jnp.int32, (CHUNK, 256), 1)
    lane128_1 = lax.broadcasted_iota(jnp.int32, (1, CHUNK), 1)

    cw = jnp.concatenate([cwx_ref[...], cwb_ref[...], cwc_ref[...]], axis=1)
    cb = jnp.concatenate([cbx_ref[...], cbb_ref[...], cbc_ref[...]], axis=1)
    dsk = dsk_ref[...]

    def phase_a(grp, carry):
        cs = [grp * GROUP + j for j in range(GROUP)]
        r0s = [pl.multiple_of(c * CHUNK, CHUNK) for c in cs]

        def conv(c):
            wstart = pl.multiple_of(c * CHUNK + jnp.where(c >= ncc, HALO, 0), 8)
            win = xp_s[pl.ds(wstart, CHUNK + 2 * HALO), :]
            acc = jnp.broadcast_to(cb, (CHUNK, 512))
            for k in range(CONV_K):
                d = k - CONV_K // 2
                if d == 0:
                    tap = win[HALO:HALO + CHUNK, :]
                else:
                    tap = pltpu.roll(win, (-d) % (CHUNK + 2 * HALO), 0)[HALO:HALO + CHUNK, :]
                acc = acc + cw[k:k + 1, :] * tap
            return _silu(acc)

        dtts = [_softplus(dtr_s[pl.ds(r0, CHUNK), :].T[0:16, :] + bias_col) for r0 in r0s]
        a_ts = [dtt * a_col for dtt in dtts]
        acol_ts = [jnp.where(fwd_row, _dot_exact_lhs(a_t, triu), _dot_exact_lhs(a_t, tril))
                   for a_t in a_ts]
        us = [conv(c) for c in cs]
        xus = [u[:, 0:256] for u in us]
        bts = [u[:, 256:384].T.astype(BF16) for u in us]
        cbfs = [u[:, 384:512].astype(BF16) for u in us]
        xbfs = [xu.astype(BF16) for xu in xus]
        gmats = [_dot(cbf, bt) for cbf, bt in zip(cbfs, bts)]
        tms = []
        for dtt, acol_t in zip(dtts, acol_ts):
            tot = jnp.where(fwd_row1, acol_t[:, CHUNK - 1:CHUNK], acol_t[:, 0:1])
            w_t = dtt * jnp.exp(tot - acol_t)
            e_t = jnp.exp(acol_t)
            stacked = jnp.concatenate(
                [acol_t, w_t, e_t, jnp.zeros((CHUNK - 48, CHUNK), F32)], axis=0)
            tms.append(stacked.T)
        wes = [_dot(tm.astype(BF16), expand) for tm in tms]
        decs = []
        for tm in tms:
            tot_row = jnp.where(lane128_1 < 8, tm[CHUNK - 1:CHUNK, :], tm[0:1, :])
            tot512 = _dot_exact_lhs(jnp.broadcast_to(tot_row, (8, CHUNK)), expand_tot)
            decs.append(jnp.exp(tot512))
        ydiags = [jnp.zeros((CHUNK, 256), F32) for _ in cs]
        for i in range(4):
            in_head = (lane256 >= 64 * i) & (lane256 < 64 * i + 64)
            for j in range(GROUP):
                tm, acol_t, dtt = tms[j], acol_ts[j], dtts[j]
                arg = jnp.where(lower, tm[:, i:i + 1] - acol_t[i:i + 1, :],
                                tm[:, 8 + i:9 + i] - acol_t[8 + i:9 + i, :])
                scale = (jnp.where(lower, dtt[i:i + 1, :], 0.0)
                         + jnp.where(upper, dtt[8 + i:9 + i, :], 0.0))
                wmat = (gmats[j] * jnp.exp(arg) * scale).astype(BF16)
                ydiags[j] = jnp.where(in_head, _dot(wmat, xbfs[j]), ydiags[j])
        for j, c in enumerate(cs):
            w512 = wes[j][:, 0:512]
            e512 = wes[j][:, 512:1024]
            xdw = (jnp.concatenate([xus[j], xus[j]], axis=1) * w512).astype(BF16)
            sloc_s[c] = _dot(bts[j], xdw)
            dec_s[c] = decs[j]
            e_s[c] = e512
            cbf_s[c] = cbfs[j]
            yacc_s[pl.ds(r0s[j], CHUNK), :] = ydiags[j] + xus[j] * dsk
        return carry

    lax.fori_loop(0, nch // GROUP, phase_a, 0)

    fwd_order = list(range(nch))
    bwd_order = list(range(ncc - 1, -1, -1)) + list(range(nch - 1, ncc - 1, -1))
    states = [jnp.zeros((SSD_STATE, 256), F32), jnp.zeros((SSD_STATE, 256), F32)]
    for cf, cb in zip(fwd_order, bwd_order):
        for d, (c, lo) in enumerate(((cf, 0), (cb, 256))):
            sin_s[c, :, lo:lo + 256] = states[d].astype(BF16)
            states[d] = (states[d] * dec_s[c, 0:1, lo:lo + 256]
                         + sloc_s[c, :, lo:lo + 256])

    def phase_c(cs, z_ref, zrows):
        yos = [_dot(cbf_s[c], sin_s[c]) for c in cs]
        vs = []
        for c, zrow, yo in zip(cs, zrows, yos):
            r0 = c * CHUNK if isinstance(c, int) else pl.multiple_of(c * CHUNK, CHUNK)
            yo = yo * e_s[c]
            y = yacc_s[pl.ds(r0, CHUNK), :] + yo[:, 0:256] + yo[:, 256:512]
            vs.append((r0, y * _silu(z_ref[pl.ds(zrow, CHUNK), :])))

        @pl.when(g == 0)
        def _():
            for r0, v in vs:
                vun_s[pl.ds(r0, CHUNK), 0:256] = v

        @pl.when(g == 1)
        def _():
            for r0, v in vs:
                vun_s[pl.ds(r0, CHUNK), 256:512] = v

    if ctx_out:
        phase_c(list(range(ncc)), zc_ref, [c * CHUNK for c in range(ncc)])

    group_c = next(n for n in (16, 8, 4, 2, 1) if ncl % n == 0)

    def phase_c_lat(k, carry):
        ks = [k * group_c + j for j in range(group_c)]
        phase_c([kk + ncc for kk in ks], zl_ref,
                [pl.multiple_of(kk * CHUNK, CHUNK) for kk in ks])
        return carry

    lax.fori_loop(0, ncl // group_c, phase_c_lat, 0)

    @pl.when(g == SSD_GROUPS - 1)
    def _finalize():
        ng = ng_ref[...]

        def norm_rows(r0, nrows):
            v = vun_s[pl.ds(r0, nrows), :]
            ms = jnp.mean(v * v, axis=-1, keepdims=True)
            return (v * lax.rsqrt(ms + EPS) * ng).astype(BF16)

        if ctx_out:
            oc_ref[...] = norm_rows(0, n_ctx)

        fin_blocks = next(n for n in (4, 2, 1) if (n_lat // 256) % n == 0)

        def fin(k, carry):
            r0s = [pl.multiple_of((k * fin_blocks + j) * 256, 256) for j in range(fin_blocks)]
            normed = [norm_rows(n_ctx + r0, 256) for r0 in r0s]
            for r0, v in zip(r0s, normed):
                ol_ref[pl.ds(r0, 256), :] = v
            return carry

        lax.fori_loop(0, n_lat // 256 // fin_blocks, fin, 0)


def _ssd_call(p_c, p_l, conv_w8, conv_b, par, dsk, ng, batch, n_ctx, n_lat, ctx_out):
    nch = (n_ctx + n_lat) // CHUNK
    t_all = n_ctx + n_lat
    in_specs = [
        pl.BlockSpec((n_ctx, 256), lambda b, g: (b, COL_X // 2 + g)),
        pl.BlockSpec((n_lat, 256), lambda b, g: (b, COL_X // 2 + g)),
        pl.BlockSpec((n_ctx, 128), lambda b, g: (b, COL_B + g)),
        pl.BlockSpec((n_lat, 128), lambda b, g: (b, COL_B + g)),
        pl.BlockSpec((n_ctx, 128), lambda b, g: (b, COL_C + g)),
        pl.BlockSpec((n_lat, 128), lambda b, g: (b, COL_C + g)),
        pl.BlockSpec((n_ctx, 256), lambda b, g: (b, COL_Z // 2 + g)),
        pl.BlockSpec((n_lat, 256), lambda b, g: (b, COL_Z // 2 + g)),
        pl.BlockSpec((n_ctx, 128), lambda b, g: (b, COL_DT + g)),
        pl.BlockSpec((n_lat, 128), lambda b, g: (b, COL_DT + g)),
        pl.BlockSpec((8, 256), lambda b, g: (0, COL_X // 2 + g)),
        pl.BlockSpec((8, 128), lambda b, g: (0, COL_B + g)),
        pl.BlockSpec((8, 128), lambda b, g: (0, COL_C + g)),
        pl.BlockSpec((1, 256), lambda b, g: (0, COL_X // 2 + g)),
        pl.BlockSpec((1, 128), lambda b, g: (0, COL_B + g)),
        pl.BlockSpec((1, 128), lambda b, g: (0, COL_C + g)),
        pl.BlockSpec((None, 32, 128), lambda b, g: (g, 0, 0)),
        pl.BlockSpec((1, 256), lambda b, g: (0, g)),
        pl.BlockSpec((1, 512), lambda b, g: (0, 0)),
    ]
    args = [p_c, p_l, p_c, p_l, p_c, p_l, p_c, p_l, p_c, p_l,
            conv_w8, conv_w8, conv_w8, conv_b, conv_b, conv_b, par, dsk, ng]
    out_specs = [pl.BlockSpec((n_lat, 512), lambda b, g: (b, 0))]
    out_shape = [jax.ShapeDtypeStruct((batch * n_lat, 512), BF16)]
    if ctx_out:
        out_specs = [pl.BlockSpec((n_ctx, 512), lambda b, g: (b, 0))] + out_specs
        out_shape = [jax.ShapeDtypeStruct((batch * n_ctx, 512), BF16)] + out_shape
    scratch = [
        pltpu.VMEM((t_all + 3 * HALO, 512), F32),
        pltpu.VMEM((t_all, 128), F32),
        pltpu.VMEM((t_all, 256), F32),
        pltpu.VMEM((nch, SSD_STATE, 512), F32),
        pltpu.VMEM((nch, CHUNK, 512), F32),
        pltpu.VMEM((nch, CHUNK, 128), BF16),
        pltpu.VMEM((nch, SSD_STATE, 512), BF16),
        pltpu.VMEM((nch, 8, 512), F32),
        pltpu.VMEM((t_all, 512), F32),
    ]
    outs = pl.pallas_call(
        functools.partial(_ssd_kernel, n_ctx=n_ctx, n_lat=n_lat, ctx_out=ctx_out),
        grid=(batch, SSD_GROUPS),
        in_specs=in_specs,
        out_specs=out_specs,
        out_shape=out_shape,
        scratch_shapes=scratch,
        compiler_params=pltpu.CompilerParams(
            dimension_semantics=("arbitrary", "arbitrary"),
            vmem_limit_bytes=VMEM_LIMIT),
        name="ssd_scan",
    )(*args)
    if ctx_out:
        return outs[1], outs[0]
    return outs[0], None


def _rope_tables(n_lat):
    rows = n_lat // GRID_W
    row_idx = np.repeat(np.arange(rows), GRID_W).astype(np.float32)
    col_idx = (np.arange(rows * GRID_W) % GRID_W).astype(np.float32)

    def tables(dim, reps):
        quarter = dim // 4
        inv = (ROPE_BASE ** (-np.arange(quarter, dtype=np.float32) / quarter)).astype(np.float32)
        ang = np.concatenate([row_idx[:, None] * inv, col_idx[:, None] * inv], axis=-1)
        cos, sin = np.cos(ang.astype(np.float64)), np.sin(ang.astype(np.float64))
        cos2 = np.concatenate([cos, cos], axis=-1).astype(np.float32)
        sin2 = np.concatenate([-sin, sin], axis=-1).astype(np.float32)
        return jnp.asarray(np.tile(cos2, (1, reps))), jnp.asarray(np.tile(sin2, (1, reps)))

    cos_g, sin_g = tables(GQA_HEAD_DIM, 4)
    cos_d, sin_d = tables(DIFF_QK_DIM, 8)
    return {"cos_g": cos_g, "sin_g": sin_g, "cos_d": cos_d, "sin_d": sin_d}


def kernel(x, c, ctx, c_ctx, w_mod, b_mod, g_pre, g_post, w_in, conv_w, conv_b,
           a_log_fwd, a_log_bwd, dt_bias_fwd, dt_bias_bwd, d_skip, ssd_norm_g,
           q_norm_g, k_norm_g, diff_lambda, diff_norm_g, w_out):
    batch, n_lat, _ = x.shape
    n_ctx = ctx.shape[1]
    depth = w_mod.shape[0]
    assert n_lat % 512 == 0 and n_ctx % 256 == 0 and (batch * n_ctx) % 512 == 0
    assert batch + 1 <= 16

    in_perm, out_perm = _in_col_perm(), _out_row_perm()
    w_in_p = [_take_runs(w_in[l], in_perm, 1, IN_COLS).astype(BF16) for l in range(depth)]
    w_out_p = [_take_runs(w_out[l], out_perm, 0, None).astype(BF16) for l in range(depth)]
    conv_w8 = jnp.pad(conv_w, ((0, 0), (0, 8 - CONV_K), (0, 0)))
    conv_b1 = conv_b[:, None, :]

    def group16(fwd, bwd):
        out = jnp.zeros((depth, SSD_GROUPS, 16), F32)
        for g in range(SSD_GROUPS):
            out = out.at[:, g, 0:4].set(fwd[:, 4 * g:4 * g + 4])
            out = out.at[:, g, 8:12].set(bwd[:, 4 * g:4 * g + 4])
        return out

    ssd_par = jnp.broadcast_to(
        jnp.concatenate([group16(a_log_fwd, a_log_bwd),
                         group16(dt_bias_fwd, dt_bias_bwd)], axis=-1)[..., None],
        (depth, SSD_GROUPS, 32, 128))
    dsk = jnp.repeat(d_skip, SSD_HEAD_DIM, axis=1)[:, None, :]
    qg = jnp.tile(q_norm_g, (1, 4))[:, None, :]
    kg = jnp.tile(k_norm_g, (1, 2))[:, None, :]
    dng = jnp.tile(diff_norm_g, (1, 4))[:, None, :]
    tabs = _rope_tables(n_lat)

    cs = jnp.concatenate(
        [c, c_ctx[None, :], jnp.zeros((16 - batch - 1, D_MODEL), F32)], axis=0)
    mod_all = _mod_call(cs, w_mod, b_mod)

    h = x.reshape(batch * n_lat, D_MODEL)
    hc = ctx.reshape(batch * n_ctx, D_MODEL)
    for l in range(depth):
        ctx_out = l < depth - 1
        lam_init = 0.8 - 0.6 * float(np.exp(-0.3 * l))
        mod3 = mod_all[l][:, None, :]
        p_l = _inproj_call(h, mod3, g_pre[l][None, :], w_in_p[l], n_lat // INPROJ_TM, None)
        p_c = _inproj_call(hc, mod3, g_pre[l][None, :], w_in_p[l], None, batch)

        ys_l, ys_c = _ssd_call(p_c, p_l, conv_w8[l], conv_b1[l], ssd_par[l], dsk[l],
                               ssd_norm_g[l][None, :], batch, n_ctx, n_lat, ctx_out)
        yg_l = _gqa_call(p_l, p_c, p_l, tabs, qg[l], kg[l], batch, n_ctx, n_lat, True)
        diff_args = (p_l, p_c, p_l, tabs, diff_lambda[l], dng[l], batch, n_ctx, n_lat, True,
                     lam_init)
        yd_fast, row_sum_min = _diff_call(*diff_args, True)
        yd_l = lax.cond(jnp.min(row_sum_min) >= MIN_ROW_SUM,
                        lambda: yd_fast, lambda: _diff_call(*diff_args, False))
        h = _outproj_call(ys_l, yg_l, yd_l, h, mod3, g_post[l][None, :], w_out_p[l],
                          n_lat, None)
        if ctx_out:
            yg_c = _gqa_call(p_c, p_c, None, tabs, qg[l], kg[l], batch, n_ctx, 0, False)
            yd_c = _diff_call(p_c, p_c, None, tabs, diff_lambda[l], dng[l], batch,
                              n_ctx, 0, False, lam_init, False)
            hc = _outproj_call(ys_c, yg_c, yd_c, hc, mod3, g_post[l][None, :],
                               w_out_p[l], None, batch)
    return h.reshape(batch, n_lat, D_MODEL)
```
